```python
import jax, jax.numpy as jnp
from jax import lax
import numpy as np

D_MODEL = 1024
BATCH = 16
SEQ = 2048
DEPTH = 1

D_MIX = D_MODEL
W_POOL = D_MIX // 2
W_CONV = D_MIX - W_POOL
POOL_WINDOWS = (2, 4, 8, 16)
N_POOL_GROUPS = len(POOL_WINDOWS)
POOL_GC = W_POOL // N_POOL_GROUPS
CONV_WIDTH = 31
N_CONV_HEADS = 8
D_IN = W_POOL * 2 + W_CONV * 3
RMS_EPS = 1e-6
LN_EPS = 1e-5

kernel_name = "hybrid_pool_conformer_gated_layer"


def _rmsnorm(x, g):
    xf = x.astype(jnp.float32)
    y = xf * lax.rsqrt(jnp.mean(xf * xf, axis=-1, keepdims=True) + RMS_EPS)
    return (y * g.astype(jnp.float32)).astype(x.dtype)


def _layernorm(x, g, b):
    xf = x.astype(jnp.float32)
    mu = jnp.mean(xf, axis=-1, keepdims=True)
    var = jnp.mean(jnp.square(xf - mu), axis=-1, keepdims=True)
    y = (xf - mu) * lax.rsqrt(var + LN_EPS)
    return (y * g.astype(jnp.float32) + b.astype(jnp.float32)).astype(x.dtype)


def _causal_window_mean(u, window):
    S = u.shape[1]
    cs = jnp.cumsum(u.astype(jnp.float32), axis=1)
    cs = jnp.concatenate([jnp.zeros_like(cs[:, :1]), cs], axis=1)
    t = jnp.arange(S)
    hi = t + 1
    lo = jnp.maximum(hi - window, 0)
    sums = cs[:, hi] - cs[:, lo]
    cnt = (hi - lo).astype(jnp.float32)[None, :, None]
    return (sums / cnt).astype(u.dtype)


def _pool_mixer(u, pool_w, pool_b, pool_scale):
    B, S, _ = u.shape
    ug = u.reshape(B, S, N_POOL_GROUPS, POOL_GC)
    pooled = jnp.stack(
        [_causal_window_mean(ug[:, :, gi], w) for gi, w in enumerate(POOL_WINDOWS)],
        axis=2)
    d = pooled - ug
    z = jnp.einsum('bsgc,gcd->bsgd', d, pool_w) + pool_b
    return z.reshape(B, S, W_POOL) * pool_scale


def _conformer_conv(v, g, conv_dw, conv_b, ln_g, ln_b, pw_w, pw_b):
    h = v * jax.nn.sigmoid(g)
    C = h.shape[-1]
    h = lax.conv_general_dilated(
        h, conv_dw[:, None, :].astype(h.dtype),
        window_strides=(1,),
        padding=[(CONV_WIDTH - 1, 0)],
        dimension_numbers=('NWC', 'WIO', 'NWC'),
        feature_group_count=C) + conv_b
    h = _layernorm(h, ln_g, ln_b)
    h = jax.nn.silu(h)
    return jnp.einsum('bsc,cd->bsd', h, pw_w) + pw_b


def _fwd_setup_inputs(seed: int = 0) -> dict:
    key = jax.random.key(seed)
    ks = jax.random.split(key, 16)
    f32 = jnp.float32
    x = jax.random.normal(ks[0], (BATCH, SEQ, D_MODEL), f32)
    norm_g = 1.0 + 0.02 * jax.random.normal(ks[1], (D_MODEL,), f32)
    w_in = jax.random.normal(ks[2], (D_MODEL, D_IN), f32) * D_MODEL ** -0.5
    pool_w = jax.random.normal(ks[3], (N_POOL_GROUPS, POOL_GC, POOL_GC), f32) * POOL_GC ** -0.5
    pool_b = 0.02 * jax.random.normal(ks[4], (N_POOL_GROUPS, POOL_GC), f32)
    pool_scale = 0.5 + 0.05 * jax.random.normal(ks[5], (W_POOL,), f32)
    conv_dw = jax.random.normal(ks[6], (CONV_WIDTH, W_CONV), f32) * CONV_WIDTH ** -0.5
    conv_b = 0.02 * jax.random.normal(ks[7], (W_CONV,), f32)
    ln_g = 1.0 + 0.02 * jax.random.normal(ks[8], (W_CONV,), f32)
    ln_b = 0.02 * jax.random.normal(ks[9], (W_CONV,), f32)
    pw_w = jax.random.normal(ks[10], (W_CONV, W_CONV), f32) * W_CONV ** -0.5
    pw_b = 0.02 * jax.random.normal(ks[11], (W_CONV,), f32)
    w_out = jax.random.normal(ks[12], (D_MIX, D_MODEL), f32) * D_MIX ** -0.5
    final_g = 1.0 + 0.02 * jax.random.normal(ks[13], (D_MODEL,), f32)
    return {"x": x, "norm_g": norm_g, "w_in": w_in,
            "pool_w": pool_w, "pool_b": pool_b, "pool_scale": pool_scale,
            "conv_dw": conv_dw, "conv_b": conv_b, "ln_g": ln_g, "ln_b": ln_b,
            "pw_w": pw_w, "pw_b": pw_b, "w_out": w_out, "final_g": final_g}


def _fwd_reference(x, norm_g, w_in, pool_w, pool_b, pool_scale, conv_dw, conv_b,
              ln_g, ln_b, pw_w, pw_b, w_out, final_g):
    h = x
    for _ in range(DEPTH):
        hn = _rmsnorm(h, norm_g)
        proj = jnp.einsum('bsd,de->bse', hn, w_in)
        o = 0
        a_val = proj[..., o:o + W_POOL]; o += W_POOL
        a_gate = proj[..., o:o + W_POOL]; o += W_POOL
        b_val = proj[..., o:o + W_CONV]; o += W_CONV
        b_glu = proj[..., o:o + W_CONV]; o += W_CONV
        b_gate = proj[..., o:o + W_CONV]
        y_a = _pool_mixer(a_val, pool_w, pool_b, pool_scale) * jax.nn.silu(a_gate)
        y_b = _conformer_conv(b_val, b_glu, conv_dw, conv_b, ln_g, ln_b, pw_w, pw_b) * jax.nn.silu(b_gate)
        y = jnp.concatenate([y_a, y_b], axis=-1)
        h = h + jnp.einsum('bse,ed->bsd', y, w_out)
    return _rmsnorm(h, final_g)


import jax as _jax
import jax.numpy as _jnp

TWIN_FORMAT = 'train_step'
FWD_PARAMS = ['x', 'norm_g', 'w_in', 'pool_w', 'pool_b', 'pool_scale', 'conv_dw', 'conv_b', 'ln_g', 'ln_b', 'pw_w', 'pw_b', 'w_out', 'final_g']
TWIN_WEIGHTS = ['norm_g', 'w_in', 'pool_w', 'pool_b', 'pool_scale', 'conv_dw', 'conv_b', 'ln_g', 'ln_b', 'pw_w', 'pw_b', 'w_out', 'final_g']
TWIN_DIFF_INPUT = 'x'
TWIN_INPUTS = ['x', 'norm_g', 'w_in', 'pool_w', 'pool_b', 'pool_scale', 'conv_dw', 'conv_b', 'ln_g', 'ln_b', 'pw_w', 'pw_b', 'w_out', 'final_g', 'loss_target', 'm_norm_g', 'm_w_in', 'm_pool_w', 'm_pool_b', 'm_pool_scale', 'm_conv_dw', 'm_conv_b', 'm_ln_g', 'm_ln_b', 'm_pw_w', 'm_pw_b', 'm_w_out', 'm_final_g', 'v_norm_g', 'v_w_in', 'v_pool_w', 'v_pool_b', 'v_pool_scale', 'v_conv_dw', 'v_conv_b', 'v_ln_g', 'v_ln_b', 'v_pw_w', 'v_pw_b', 'v_w_out', 'v_final_g']
TWIN_OUTPUTS = ['loss', 'grad_x', 'grad_norm_g', 'grad_w_in', 'grad_pool_w', 'grad_pool_b', 'grad_pool_scale', 'grad_conv_dw', 'grad_conv_b', 'grad_ln_g', 'grad_ln_b', 'grad_pw_w', 'grad_pw_b', 'grad_w_out', 'grad_final_g', 'delta_norm_g', 'delta_w_in', 'delta_pool_w', 'delta_pool_b', 'delta_pool_scale', 'delta_conv_dw', 'delta_conv_b', 'delta_ln_g', 'delta_ln_b', 'delta_pw_w', 'delta_pw_b', 'delta_w_out', 'delta_final_g', 'new_m_norm_g', 'new_m_w_in', 'new_m_pool_w', 'new_m_pool_b', 'new_m_pool_scale', 'new_m_conv_dw', 'new_m_conv_b', 'new_m_ln_g', 'new_m_ln_b', 'new_m_pw_w', 'new_m_pw_b', 'new_m_w_out', 'new_m_final_g', 'new_v_norm_g', 'new_v_w_in', 'new_v_pool_w', 'new_v_pool_b', 'new_v_pool_scale', 'new_v_conv_dw', 'new_v_conv_b', 'new_v_ln_g', 'new_v_ln_b', 'new_v_pw_w', 'new_v_pw_b', 'new_v_w_out', 'new_v_final_g']
TWIN_LEAF_KINDS = {'loss': 'loss', 'grad_x': 'grad_x', 'grad_norm_g': 'grad_w', 'grad_w_in': 'grad_w', 'grad_pool_w': 'grad_w', 'grad_pool_b': 'grad_w', 'grad_pool_scale': 'grad_w', 'grad_conv_dw': 'grad_w', 'grad_conv_b': 'grad_w', 'grad_ln_g': 'grad_w', 'grad_ln_b': 'grad_w', 'grad_pw_w': 'grad_w', 'grad_pw_b': 'grad_w', 'grad_w_out': 'grad_w', 'grad_final_g': 'grad_w', 'delta_norm_g': 'delta_w', 'delta_w_in': 'delta_w', 'delta_pool_w': 'delta_w', 'delta_pool_b': 'delta_w', 'delta_pool_scale': 'delta_w', 'delta_conv_dw': 'delta_w', 'delta_conv_b': 'delta_w', 'delta_ln_g': 'delta_w', 'delta_ln_b': 'delta_w', 'delta_pw_w': 'delta_w', 'delta_pw_b': 'delta_w', 'delta_w_out': 'delta_w', 'delta_final_g': 'delta_w', 'new_m_norm_g': 'new_m', 'new_m_w_in': 'new_m', 'new_m_pool_w': 'new_m', 'new_m_pool_b': 'new_m', 'new_m_pool_scale': 'new_m', 'new_m_conv_dw': 'new_m', 'new_m_conv_b': 'new_m', 'new_m_ln_g': 'new_m', 'new_m_ln_b': 'new_m', 'new_m_pw_w': 'new_m', 'new_m_pw_b': 'new_m', 'new_m_w_out': 'new_m', 'new_m_final_g': 'new_m', 'new_v_norm_g': 'new_v', 'new_v_w_in': 'new_v', 'new_v_pool_w': 'new_v', 'new_v_pool_b': 'new_v', 'new_v_pool_scale': 'new_v', 'new_v_conv_dw': 'new_v', 'new_v_conv_b': 'new_v', 'new_v_ln_g': 'new_v', 'new_v_ln_b': 'new_v', 'new_v_pw_w': 'new_v', 'new_v_pw_b': 'new_v', 'new_v_w_out': 'new_v', 'new_v_final_g': 'new_v'}


def _forward(args):
    return _fwd_reference(*[args[k] for k in FWD_PARAMS])


def _output_shape():
    out = _jax.eval_shape(lambda: _forward(_fwd_setup_inputs(0)))
    return out.shape, out.dtype

N_MICROBATCH = 1
ADAM_LR = 0.001
ADAM_B1 = 0.9
ADAM_B2 = 0.999
ADAM_EPS = 1e-08
ADAM_WD = 0.01
ADAM_STEP = 10
PER_EXAMPLE_BATCH_AXIS = {'x': 0, 'loss_target': 0}
SHARED_INPUTS = []
_WEIGHT_DTYPES = {'norm_g': _jnp.float32, 'w_in': _jnp.float32, 'pool_w': _jnp.float32, 'pool_b': _jnp.float32, 'pool_scale': _jnp.float32, 'conv_dw': _jnp.float32, 'conv_b': _jnp.float32, 'ln_g': _jnp.float32, 'ln_b': _jnp.float32, 'pw_w': _jnp.float32, 'pw_b': _jnp.float32, 'w_out': _jnp.float32, 'final_g': _jnp.float32}
MOMENT_SCALE = {'norm_g': 7.789461e-02, 'w_in': 4.987438e-02, 'pool_w': 4.418059e-02, 'pool_b': 5.405225e-02, 'pool_scale': 8.565236e-02, 'conv_dw': 6.300667e-02, 'conv_b': 1.385696e-01, 'ln_g': 7.549251e-02, 'ln_b': 6.409290e-02, 'pw_w': 6.068191e-02, 'pw_b': 1.059864e-01, 'w_out': 5.314529e-02, 'final_g': 3.198642e+01}


def _to_microbatches(a, axis):
    t = _jnp.moveaxis(a, axis, 0)
    t = t.reshape((N_MICROBATCH, t.shape[0] // N_MICROBATCH) + t.shape[1:])
    return _jnp.moveaxis(t, 1, axis + 1)


def setup_inputs(seed: int = 0) -> dict:
    inp = _fwd_setup_inputs(seed)
    key = _jax.random.fold_in(_jax.random.key(seed), 7919)
    shape, _ = _output_shape()
    out = dict(inp)
    out["loss_target"] = _jax.random.normal(_jax.random.fold_in(key, 0), shape, _jnp.float32)
    for i, name in enumerate(TWIN_WEIGHTS):
        w = inp[name].astype(_jnp.float32)
        if MOMENT_SCALE is None:
            s = _jnp.sqrt(_jnp.mean(_jnp.square(w)) + 1e-30)
        else:
            s = MOMENT_SCALE[name]
        km, kv = _jax.random.split(_jax.random.fold_in(key, i + 1))
        out[name] = w
        out["m_" + name] = s * _jax.random.normal(km, w.shape, _jnp.float32)
        out["v_" + name] = (s * s) * _jax.random.uniform(kv, w.shape, _jnp.float32, 0.5, 1.5)
    if N_MICROBATCH > 1:
        for name, axis in PER_EXAMPLE_BATCH_AXIS.items():
            out[name] = _to_microbatches(out[name], axis)
    return {'x': out['x'], 'norm_g': out['norm_g'], 'w_in': out['w_in'], 'pool_w': out['pool_w'], 'pool_b': out['pool_b'], 'pool_scale': out['pool_scale'], 'conv_dw': out['conv_dw'], 'conv_b': out['conv_b'], 'ln_g': out['ln_g'], 'ln_b': out['ln_b'], 'pw_w': out['pw_w'], 'pw_b': out['pw_b'], 'w_out': out['w_out'], 'final_g': out['final_g'], 'loss_target': out['loss_target'], 'm_norm_g': out['m_norm_g'], 'm_w_in': out['m_w_in'], 'm_pool_w': out['m_pool_w'], 'm_pool_b': out['m_pool_b'], 'm_pool_scale': out['m_pool_scale'], 'm_conv_dw': out['m_conv_dw'], 'm_conv_b': out['m_conv_b'], 'm_ln_g': out['m_ln_g'], 'm_ln_b': out['m_ln_b'], 'm_pw_w': out['m_pw_w'], 'm_pw_b': out['m_pw_b'], 'm_w_out': out['m_w_out'], 'm_final_g': out['m_final_g'], 'v_norm_g': out['v_norm_g'], 'v_w_in': out['v_w_in'], 'v_pool_w': out['v_pool_w'], 'v_pool_b': out['v_pool_b'], 'v_pool_scale': out['v_pool_scale'], 'v_conv_dw': out['v_conv_dw'], 'v_conv_b': out['v_conv_b'], 'v_ln_g': out['v_ln_g'], 'v_ln_b': out['v_ln_b'], 'v_pw_w': out['v_pw_w'], 'v_pw_b': out['v_pw_b'], 'v_w_out': out['v_w_out'], 'v_final_g': out['v_final_g']}


def _loss(weights, diff, rest, loss_target):
    with _jax.named_scope("forward"):
        args = {**rest, TWIN_DIFF_INPUT: diff, **{k: w.astype(_WEIGHT_DTYPES[k]) for k, w in weights.items()}}
        y = _forward(args)
    with _jax.named_scope("loss_head"):
        err = _jnp.square(y.astype(_jnp.float32) - loss_target)
        return 0.5 * _jnp.sum(_jnp.mean(err, axis=-1)) if err.ndim else 0.5 * err


def _adamw(w, g, m, v):
    m = ADAM_B1 * m + (1.0 - ADAM_B1) * g
    v = ADAM_B2 * v + (1.0 - ADAM_B2) * _jnp.square(g)
    m_hat = m / (1.0 - ADAM_B1 ** ADAM_STEP)
    v_hat = v / (1.0 - ADAM_B2 ** ADAM_STEP)
    delta = -ADAM_LR * (m_hat / (_jnp.sqrt(v_hat) + ADAM_EPS) + ADAM_WD * w)
    return delta, m, v


def reference(x, norm_g, w_in, pool_w, pool_b, pool_scale, conv_dw, conv_b, ln_g, ln_b, pw_w, pw_b, w_out, final_g, loss_target, m_norm_g, m_w_in, m_pool_w, m_pool_b, m_pool_scale, m_conv_dw, m_conv_b, m_ln_g, m_ln_b, m_pw_w, m_pw_b, m_w_out, m_final_g, v_norm_g, v_w_in, v_pool_w, v_pool_b, v_pool_scale, v_conv_dw, v_conv_b, v_ln_g, v_ln_b, v_pw_w, v_pw_b, v_w_out, v_final_g):
    given = dict(x=x, norm_g=norm_g, w_in=w_in, pool_w=pool_w, pool_b=pool_b, pool_scale=pool_scale, conv_dw=conv_dw, conv_b=conv_b, ln_g=ln_g, ln_b=ln_b, pw_w=pw_w, pw_b=pw_b, w_out=w_out, final_g=final_g, loss_target=loss_target, m_norm_g=m_norm_g, m_w_in=m_w_in, m_pool_w=m_pool_w, m_pool_b=m_pool_b, m_pool_scale=m_pool_scale, m_conv_dw=m_conv_dw, m_conv_b=m_conv_b, m_ln_g=m_ln_g, m_ln_b=m_ln_b, m_pw_w=m_pw_w, m_pw_b=m_pw_b, m_w_out=m_w_out, m_final_g=m_final_g, v_norm_g=v_norm_g, v_w_in=v_w_in, v_pool_w=v_pool_w, v_pool_b=v_pool_b, v_pool_scale=v_pool_scale, v_conv_dw=v_conv_dw, v_conv_b=v_conv_b, v_ln_g=v_ln_g, v_ln_b=v_ln_b, v_pw_w=v_pw_w, v_pw_b=v_pw_b, v_w_out=v_w_out, v_final_g=v_final_g)
    weights = {n: given[n] for n in TWIN_WEIGHTS}
    shared = {n: given[n] for n in SHARED_INPUTS}
    per_example = {n: given[n] for n in ['x']}
    grad_fn = _jax.value_and_grad(_loss, argnums=(0, 1))

    def one_microbatch(ex, loss_target):
        ex = dict(ex)
        diff = ex.pop(TWIN_DIFF_INPUT)
        return grad_fn(weights, diff, {**shared, **ex}, loss_target)

    if N_MICROBATCH == 1:
        loss, (grad_w, grad_x) = one_microbatch(per_example, given["loss_target"])
    else:
        def body(carry, xs):
            loss_sum, grad_sum = carry
            l_k, (gw_k, gx_k) = one_microbatch(xs[0], xs[1])
            with _jax.named_scope("update"):
                return (loss_sum + l_k, _jax.tree.map(_jnp.add, grad_sum, gw_k)), gx_k

        init = (_jnp.zeros((), _jnp.float32), _jax.tree.map(_jnp.zeros_like, weights))
        (loss, grad_w), grad_x = _jax.lax.scan(body, init, (per_example, given["loss_target"]))
    with _jax.named_scope("update"):
        delta_w, new_m, new_v = {}, {}, {}
        for n in TWIN_WEIGHTS:
            delta_w[n], new_m[n], new_v[n] = _adamw(weights[n], grad_w[n], given["m_" + n], given["v_" + n])
    return (loss, grad_x, *[grad_w[n] for n in TWIN_WEIGHTS], *[delta_w[n] for n in TWIN_WEIGHTS],
            *[new_m[n] for n in TWIN_WEIGHTS], *[new_v[n] for n in TWIN_WEIGHTS])
```

```python
import functools

import jax
import jax.numpy as jnp
from jax import lax
from jax.experimental import pallas as pl
from jax.experimental.pallas import tpu as pltpu

F32 = jnp.float32
BF16 = jnp.bfloat16

D_MODEL = 1024
W_MIX = 512
D_IN = 5 * W_MIX
POOL_WINDOWS = (2, 4, 8, 16)
POOL_GC = 128
CONV_WIDTH = 31
RMS_EPS = 1e-6
LN_EPS = 1e-5
N_DEV = 8

LANES = 128
SUBLANES = 8
TILE_M = 512
POOL_HALO = 16
CONV_HALO = 32
CONV_ROWS = 128
VMEM_LIMIT = 56 * 1024 * 1024

VMEM = pl.BlockSpec(memory_space=pltpu.VMEM)


def _silu(v):
    return v * jax.nn.sigmoid(v)


def _dsilu(v):
    s = jax.nn.sigmoid(v)
    return s * (1.0 + v * (1.0 - s))


def _colsum8(v):
    m, n = v.shape
    return jnp.sum(v.reshape(m // SUBLANES, SUBLANES, n), axis=0)


def _row_index(shape, t0):
    return lax.broadcasted_iota(jnp.int32, shape, 0) + t0


def _conv_taps():
    taps = {s: [] for s in range(SUBLANES)}
    for k in range(CONV_WIDTH):
        e = k + CONV_HALO - (CONV_WIDTH - 1)
        taps[e % SUBLANES].append((e // SUBLANES, k))
    return taps


def _fwd_body(x_ref, tgt_ref, ng_ref, win_ref, poolw_ref, poolb_ref, pools_ref, dw_ref, cb_ref, lg_ref, lb_ref,
              pww_ref, pwb_ref, wout_ref, fg_ref,
              proj_ref, h1_ref, z_ref, ob_ref, dpool_ref, dh_ref, loss_ref, gfg_ref,
              uext, hext, y_s):
    b = pl.program_id(0)
    i = pl.program_id(1)
    tm = TILE_M

    @pl.when(i == 0)
    def _():
        uext[0:POOL_HALO, :] = jnp.zeros((POOL_HALO, W_MIX), F32)
        hext[0:CONV_HALO, :] = jnp.zeros((CONV_HALO, W_MIX), F32)
        hext[CONV_HALO + tm:, :] = jnp.zeros((SUBLANES, W_MIX), F32)

    @pl.when((b == 0) & (i == 0))
    def _():
        loss_ref[...] = jnp.zeros_like(loss_ref)
        gfg_ref[...] = jnp.zeros_like(gfg_ref)

    x = x_ref[0]
    r1 = lax.rsqrt(jnp.mean(x * x, axis=-1, keepdims=True) + RMS_EPS)
    hn = (x * r1 * ng_ref[...]).astype(BF16)
    proj_ref[0] = jnp.dot(hn, win_ref[...], preferred_element_type=F32)

    uext[POOL_HALO:, :] = proj_ref[0, :, 0:W_MIX]
    t = _row_index((tm, POOL_GC), i * tm)
    for g, w in enumerate(POOL_WINDOWS):
        lanes = slice(g * POOL_GC, (g + 1) * POOL_GC)
        e = uext[:, lanes]
        s = e
        sh = 1
        while sh < w:
            s = s + pltpu.roll(s, sh, 0)
            sh *= 2
        inv_cnt = 1.0 / jnp.minimum(t + 1, w).astype(F32)
        d = (s[POOL_HALO:] * inv_cnt - e[POOL_HALO:]).astype(BF16)
        dpool_ref[0, :, lanes] = d
        z = jnp.dot(d, poolw_ref[g].astype(BF16), preferred_element_type=F32) + poolb_ref[:, lanes]
        z_ref[0, :, lanes] = z
        a_gate = proj_ref[0, :, W_MIX + g * POOL_GC:W_MIX + (g + 1) * POOL_GC]
        y_s[:, lanes] = (z * pools_ref[:, lanes] * _silu(a_gate)).astype(BF16)

    hext[CONV_HALO:CONV_HALO + tm, :] = proj_ref[0, :, 2 * W_MIX:3 * W_MIX] * jax.nn.sigmoid(proj_ref[0, :, 3 * W_MIX:4 * W_MIX])
    taps = _conv_taps()
    for c0 in range(0, tm, CONV_ROWS):
        for l0 in range(0, W_MIX, LANES):
            lanes = slice(l0, l0 + LANES)
            acc = jnp.zeros((CONV_ROWS, LANES), F32) + cb_ref[:, lanes]
            for s in range(SUBLANES):
                part = jnp.zeros((CONV_ROWS + SUBLANES, LANES), F32)
                for q, k in taps[s]:
                    r0 = c0 + SUBLANES * q
                    part = part + hext[r0:r0 + CONV_ROWS + SUBLANES, lanes] * dw_ref[k:k + 1, lanes]
                acc = acc + part[s:s + CONV_ROWS]
            h1_ref[0, c0:c0 + CONV_ROWS, lanes] = acc
    h1 = h1_ref[0]
    mu = jnp.mean(h1, axis=-1, keepdims=True)
    xc = h1 - mu
    rstd = lax.rsqrt(jnp.mean(xc * xc, axis=-1, keepdims=True) + LN_EPS)
    h3 = _silu(xc * rstd * lg_ref[...] + lb_ref[...])
    ob = jnp.dot(h3.astype(BF16), pww_ref[...], preferred_element_type=F32) + pwb_ref[...]
    ob_ref[0] = ob
    y_s[:, W_MIX:] = (ob * _silu(proj_ref[0, :, 4 * W_MIX:5 * W_MIX])).astype(BF16)

    h = x + jnp.dot(y_s[...], wout_ref[...], preferred_element_type=F32)
    r2 = lax.rsqrt(jnp.mean(h * h, axis=-1, keepdims=True) + RMS_EPS)
    hr = h * r2
    err = hr * fg_ref[...] - tgt_ref[0]
    loss_ref[...] += jnp.sum(err * err) * (0.5 / D_MODEL)
    dout = err * (1.0 / D_MODEL)
    gfg_ref[...] += _colsum8(dout * hr)
    gy = dout * fg_ref[...]
    dh_ref[0] = r2 * (gy - hr * jnp.mean(gy * hr, axis=-1, keepdims=True))

    uext[0:POOL_HALO, :] = uext[tm:tm + POOL_HALO, :]
    hext[0:CONV_HALO, :] = hext[tm:tm + CONV_HALO, :]


def _row(v):
    return v.reshape(1, -1)


def _fwd(x, tgt, norm_g, w_in_b, pool_w, pool_b, pool_scale, conv_dw_f, conv_b, ln_g, ln_b, pw_w_b, pw_b, w_out_b, final_g):
    bl, s, _ = x.shape
    tm = TILE_M
    tok = lambda n: pl.BlockSpec((1, tm, n), lambda b, i: (b, i, 0))
    acc = lambda n: pl.BlockSpec((SUBLANES, n), lambda b, i: (0, 0))
    act = lambda n: jax.ShapeDtypeStruct((bl, s, n), F32)
    return pl.pallas_call(
        functools.partial(_fwd_body),
        name="fwd",
        grid=(bl, s // tm),
        in_specs=[tok(D_MODEL), tok(D_MODEL)] + [VMEM] * 13,
        out_specs=[tok(D_IN), tok(W_MIX), tok(W_MIX), tok(W_MIX), tok(W_MIX), tok(D_MODEL), acc(LANES), acc(D_MODEL)],
        out_shape=[act(D_IN), act(W_MIX), act(W_MIX), act(W_MIX), jax.ShapeDtypeStruct((bl, s, W_MIX), BF16), act(D_MODEL),
                   jax.ShapeDtypeStruct((SUBLANES, LANES), F32), jax.ShapeDtypeStruct((SUBLANES, D_MODEL), F32)],
        scratch_shapes=[pltpu.VMEM((POOL_HALO + tm, W_MIX), F32), pltpu.VMEM((CONV_HALO + tm + SUBLANES, W_MIX), F32),
                        pltpu.VMEM((tm, D_MODEL), BF16)],
        compiler_params=pltpu.CompilerParams(dimension_semantics=("arbitrary", "arbitrary"), vmem_limit_bytes=VMEM_LIMIT),
    )(x, tgt, _row(norm_g), w_in_b, pool_w, pool_b.reshape(1, W_MIX), _row(pool_scale), conv_dw_f, _row(conv_b),
      _row(ln_g), _row(ln_b), pw_w_b, _row(pw_b), w_out_b, _row(final_g))


NT_DIMS = (((1,), (1,)), ((), ()))
TN_DIMS = (((0,), (0,)), ((), ()))
V_POOL_B, V_POOL_SCALE, V_CONV_B, V_LN_G, V_LN_B, V_PW_B = range(6)


def _bwd_taps():
    taps = {s: [] for s in range(SUBLANES)}
    for k in range(CONV_WIDTH):
        e = CONV_WIDTH - 1 - k
        taps[e % SUBLANES].append((e // SUBLANES, k))
    return taps


def _bwd1_body(n_tiles, dh_ref, proj_ref, h1_ref, z_ref, ob_ref, dpool_ref, wout_ref, pww_ref, poolw_ref, pools_ref, dw_ref,
               lg_ref, lb_ref,
               dproj_ref, gwout_ref, gpww_ref, gpoolw_ref, gvec_ref, gdw_ref,
               vext, gext, gsh, y_s):
    b = pl.program_id(0)
    i = pl.program_id(1)
    tm = TILE_M
    t0 = (n_tiles - 1 - i) * tm

    @pl.when(i == 0)
    def _():
        vext[tm:, :] = jnp.zeros((POOL_HALO, W_MIX), F32)
        gext[tm:, :] = jnp.zeros((CONV_HALO, W_MIX), F32)

    @pl.when((b == 0) & (i == 0))
    def _():
        gwout_ref[...] = jnp.zeros_like(gwout_ref)
        gpww_ref[...] = jnp.zeros_like(gpww_ref)
        gpoolw_ref[...] = jnp.zeros_like(gpoolw_ref)
        gvec_ref[...] = jnp.zeros_like(gvec_ref)
        gdw_ref[...] = jnp.zeros_like(gdw_ref)

    dhb = dh_ref[0].astype(BF16)
    dy_a = lax.dot_general(dhb, wout_ref[0:W_MIX, :], NT_DIMS, preferred_element_type=F32)
    dy_b = lax.dot_general(dhb, wout_ref[W_MIX:, :], NT_DIMS, preferred_element_type=F32)

    a_gate = proj_ref[0, :, W_MIX:2 * W_MIX]
    sig_a = jax.nn.sigmoid(a_gate)
    silu_a = a_gate * sig_a
    z = z_ref[0]
    out_a = z * pools_ref[...]
    y_s[:, 0:W_MIX] = (out_a * silu_a).astype(BF16)
    d_out_a = dy_a * silu_a
    d_ag = dy_a * out_a * (sig_a * (1.0 + a_gate * (1.0 - sig_a)))
    dproj_ref[0, :, W_MIX:2 * W_MIX] = d_ag.astype(BF16)
    gvec_ref[V_POOL_SCALE] += _colsum8(d_out_a * z)
    dz = d_out_a * pools_ref[...]
    gvec_ref[V_POOL_B] += _colsum8(dz)
    t = _row_index((tm, POOL_GC), t0)
    n_ext = tm + POOL_HALO
    for g, w in enumerate(POOL_WINDOWS):
        lanes = slice(g * POOL_GC, (g + 1) * POOL_GC)
        dzg = dz[:, lanes].astype(BF16)
        gpoolw_ref[g] += lax.dot_general(dpool_ref[0, :, lanes], dzg, TN_DIMS, preferred_element_type=F32)
        dd = lax.dot_general(dzg, poolw_ref[g].astype(BF16), NT_DIMS, preferred_element_type=F32)
        vext[0:tm, lanes] = dd * (1.0 / jnp.minimum(t + 1, w).astype(F32))
        s = vext[:, lanes]
        sh = 1
        while sh < w:
            s = s + pltpu.roll(s, n_ext - sh, 0)
            sh *= 2
        dproj_ref[0, :, lanes] = (s[0:tm] - dd).astype(BF16)
    vext[tm:, :] = vext[0:POOL_HALO, :]

    b_gate = proj_ref[0, :, 4 * W_MIX:5 * W_MIX]
    sig_b = jax.nn.sigmoid(b_gate)
    silu_b = b_gate * sig_b
    ob = ob_ref[0]
    y_s[:, W_MIX:] = (ob * silu_b).astype(BF16)
    d_ob = dy_b * silu_b
    dproj_ref[0, :, 4 * W_MIX:5 * W_MIX] = (dy_b * ob * (sig_b * (1.0 + b_gate * (1.0 - sig_b)))).astype(BF16)
    gvec_ref[V_PW_B] += _colsum8(d_ob)
    gwout_ref[...] += lax.dot_general(y_s[...], dhb, TN_DIMS, preferred_element_type=F32)

    h1 = h1_ref[0]
    mu = jnp.mean(h1, axis=-1, keepdims=True)
    xc = h1 - mu
    rstd = lax.rsqrt(jnp.mean(xc * xc, axis=-1, keepdims=True) + LN_EPS)
    xhat = xc * rstd
    h2 = xhat * lg_ref[...] + lb_ref[...]
    sig2 = jax.nn.sigmoid(h2)
    d_obb = d_ob.astype(BF16)
    gpww_ref[...] += lax.dot_general((h2 * sig2).astype(BF16), d_obb, TN_DIMS, preferred_element_type=F32)
    dh2 = lax.dot_general(d_obb, pww_ref[...], NT_DIMS, preferred_element_type=F32) * (sig2 * (1.0 + h2 * (1.0 - sig2)))
    gvec_ref[V_LN_G] += _colsum8(dh2 * xhat)
    gvec_ref[V_LN_B] += _colsum8(dh2)
    dxh = dh2 * lg_ref[...]
    dh1 = rstd * (dxh - jnp.mean(dxh, axis=-1, keepdims=True) - xhat * jnp.mean(dxh * xhat, axis=-1, keepdims=True))
    gvec_ref[V_CONV_B] += _colsum8(dh1)
    gext[0:tm, :] = dh1

    taps = _bwd_taps()
    win = CONV_ROWS + CONV_HALO - SUBLANES
    for c0 in range(0, tm, CONV_ROWS):
        for l0 in range(0, W_MIX, LANES):
            lanes = slice(l0, l0 + LANES)
            b_val = proj_ref[0, c0:c0 + CONV_ROWS, 2 * W_MIX + l0:2 * W_MIX + l0 + LANES]
            sg = jax.nn.sigmoid(proj_ref[0, c0:c0 + CONV_ROWS, 3 * W_MIX + l0:3 * W_MIX + l0 + LANES])
            h0c = b_val * sg
            acc = jnp.zeros((CONV_ROWS, LANES), F32)
            for s in range(SUBLANES):
                gsh[...] = gext[c0 + s:c0 + s + win, lanes]
                for q, k in taps[s]:
                    gq = gsh[SUBLANES * q:SUBLANES * q + CONV_ROWS, :]
                    acc = acc + gq * dw_ref[k:k + 1, lanes]
                    gdw_ref[k, :, lanes] += _colsum8(h0c * gq)
            d_bval = acc * sg
            dproj_ref[0, c0:c0 + CONV_ROWS, 2 * W_MIX + l0:2 * W_MIX + l0 + LANES] = d_bval.astype(BF16)
            dproj_ref[0, c0:c0 + CONV_ROWS, 3 * W_MIX + l0:3 * W_MIX + l0 + LANES] = (
                d_bval * b_val * (1.0 - sg)).astype(BF16)
    gext[tm:, :] = gext[0:CONV_HALO, :]


def _bwd1(dh, proj, h1, z, ob, dpool, w_out_b, pw_w_b, pool_w, pool_scale, conv_dw_f, ln_g, ln_b):
    bl, s, _ = dh.shape
    tm = TILE_M
    nt = s // tm
    tok = lambda n: pl.BlockSpec((1, tm, n), lambda b, i: (b, nt - 1 - i, 0))
    res = lambda shape: pl.BlockSpec(shape, lambda b, i: (0,) * len(shape))
    out_shapes = [(D_MODEL, D_MODEL), (W_MIX, W_MIX), (len(POOL_WINDOWS), POOL_GC, POOL_GC), (6, SUBLANES, W_MIX),
                  (CONV_WIDTH, SUBLANES, W_MIX)]
    return pl.pallas_call(
        functools.partial(_bwd1_body, nt),
        name="bwd1",
        grid=(bl, nt),
        in_specs=[tok(D_MODEL), tok(D_IN), tok(W_MIX), tok(W_MIX), tok(W_MIX), tok(W_MIX)] + [VMEM] * 7,
        out_specs=[tok(D_IN)] + [res(sh) for sh in out_shapes],
        out_shape=[jax.ShapeDtypeStruct((bl, s, D_IN), BF16)] + [jax.ShapeDtypeStruct(sh, F32) for sh in out_shapes],
        scratch_shapes=[pltpu.VMEM((tm + POOL_HALO, W_MIX), F32), pltpu.VMEM((tm + CONV_HALO, W_MIX), F32),
                        pltpu.VMEM((CONV_ROWS + CONV_HALO - SUBLANES, LANES), F32), pltpu.VMEM((tm, D_MODEL), BF16)],
        compiler_params=pltpu.CompilerParams(dimension_semantics=("arbitrary", "arbitrary"), vmem_limit_bytes=VMEM_LIMIT),
    )(dh, proj, h1, z, ob, dpool, w_out_b, pw_w_b, pool_w, _row(pool_scale), conv_dw_f, _row(ln_g), _row(ln_b))


def _bwd2_body(n_steps, x_ref, dh_ref, dproj_ref, ng_ref, win_ref, gx_ref, gwin_ref, gng_ref, acc):
    step = pl.program_id(0) * pl.num_programs(1) + pl.program_id(1)

    @pl.when(step == 0)
    def _():
        acc[...] = jnp.zeros_like(acc)
        gng_ref[...] = jnp.zeros_like(gng_ref)

    x = x_ref[0]
    r1 = lax.rsqrt(jnp.mean(x * x, axis=-1, keepdims=True) + RMS_EPS)
    xr = x * r1
    hn = (xr * ng_ref[...]).astype(BF16)
    dproj = dproj_ref[0]
    acc[...] += lax.dot_general(hn, dproj, TN_DIMS, preferred_element_type=F32)
    dhn = lax.dot_general(dproj, win_ref[...], NT_DIMS, preferred_element_type=F32)
    gng_ref[...] += _colsum8(dhn * xr)
    gy = dhn * ng_ref[...]
    gx_ref[0] = dh_ref[0] + r1 * (gy - xr * jnp.mean(gy * xr, axis=-1, keepdims=True))

    @pl.when(step == n_steps - 1)
    def _():
        nb = D_IN // N_DEV
        for k in range(N_DEV):
            gwin_ref[k] = acc[:, k * nb:(k + 1) * nb]


def _bwd2(x, dh, dproj, norm_g, w_in_b):
    bl, s, _ = x.shape
    tm = TILE_M
    nt = s // tm
    tok = lambda n: pl.BlockSpec((1, tm, n), lambda b, i: (b, i, 0))
    return pl.pallas_call(
        functools.partial(_bwd2_body, bl * nt),
        name="bwd2",
        grid=(bl, nt),
        in_specs=[tok(D_MODEL), tok(D_MODEL), tok(D_IN), VMEM, VMEM],
        out_specs=[tok(D_MODEL), VMEM, pl.BlockSpec((SUBLANES, D_MODEL), lambda b, i: (0, 0))],
        out_shape=[jax.ShapeDtypeStruct((bl, s, D_MODEL), F32), jax.ShapeDtypeStruct((N_DEV, D_MODEL, D_IN // N_DEV), F32),
                   jax.ShapeDtypeStruct((SUBLANES, D_MODEL), F32)],
        scratch_shapes=[pltpu.VMEM((D_MODEL, D_IN), F32)],
        compiler_params=pltpu.CompilerParams(dimension_semantics=("arbitrary", "arbitrary"), vmem_limit_bytes=VMEM_LIMIT),
    )(x, dh, dproj, _row(norm_g), w_in_b)


MESH = pl.DeviceIdType.MESH
W_IN_BLK = D_IN // N_DEV
W_OUT_BLK = D_MODEL // N_DEV
PW_BLK = W_MIX // N_DEV
DW_BLK = W_MIX // N_DEV
N_PEER = N_DEV - 1


def _place():
    x, y, c = lax.axis_index("x"), lax.axis_index("y"), lax.axis_index("c")
    other_chips = [(1 - x, y), (x, 1 - y), (1 - x, 1 - y)]
    return x, y, c, other_chips


def _remote(srcs, dsts, send_sems, recv_sems, k, to):
    return [pltpu.make_async_remote_copy(src_ref=s, dst_ref=d, send_sem=send_sems.at[a, k], recv_sem=recv_sems.at[a, k],
                                         device_id=to, device_id_type=MESH)
            for a, (s, d) in enumerate(zip(srcs, dsts))]


def _all_gather(slots, send_sems, recv_sems):
    x, y, c, other_chips = _place()
    me, sibling = (x, y, c), (x, y, 1 - c)

    def copy(k, block, to):
        refs = slots(*block)
        return _remote(refs, refs, send_sems, recv_sems, k, to)

    first = copy(0, me, sibling)
    for j, chip in enumerate(other_chips):
        first += copy(1 + j, me, (*chip, c))
    for cp in first:
        cp.start()
    passed = []
    for j, chip in enumerate(other_chips):
        for cp in copy(1 + j, (*chip, c), me):
            cp.wait_recv()
        fwd = copy(4 + j, (*chip, c), sibling)
        for cp in fwd:
            cp.start()
        passed += fwd
    for cp in copy(0, sibling, me):
        cp.wait_recv()
    for j, chip in enumerate(other_chips):
        for cp in copy(4 + j, (*chip, 1 - c), me):
            cp.wait_recv()
    for cp in first + passed:
        cp.wait_send()


def _gather_body(win_ref, wout_ref, pww_ref, dw_ref, winf_ref, woutf_ref, pwwf_ref, dwf_ref,
                 win_blk, dw_blk, send_sems, recv_sems):
    def slots(px, py, pc):
        k = 4 * px + 2 * py + pc
        return (win_blk.at[k], woutf_ref.at[pl.ds(pl.multiple_of(k * W_OUT_BLK, W_OUT_BLK), W_OUT_BLK), :],
                pwwf_ref.at[pl.ds(pl.multiple_of(k * PW_BLK, PW_BLK), PW_BLK), :], dw_blk.at[k])

    x, y, c, _ = _place()
    mine = slots(x, y, c)
    mine[0][...] = win_ref[...].astype(BF16)
    mine[1][...] = wout_ref[...].astype(BF16)
    mine[2][...] = pww_ref[...].astype(BF16)
    mine[3][...] = dw_ref[...]
    _all_gather(slots, send_sems, recv_sems)
    for k in range(N_DEV):
        winf_ref[:, k * W_IN_BLK:(k + 1) * W_IN_BLK] = win_blk[k]
        dwf_ref[:, k * DW_BLK:(k + 1) * DW_BLK] = dw_blk[k]


def _gather_weights(w_in, w_out, pw_w, conv_dw):
    return pl.pallas_call(
        functools.partial(_gather_body),
        name="gather",
        in_specs=[VMEM] * 4,
        out_specs=[VMEM] * 4,
        out_shape=[jax.ShapeDtypeStruct((D_MODEL, D_IN), BF16), jax.ShapeDtypeStruct((D_MODEL, D_MODEL), BF16),
                   jax.ShapeDtypeStruct((W_MIX, W_MIX), BF16), jax.ShapeDtypeStruct((CONV_WIDTH, W_MIX), F32)],
        scratch_shapes=[pltpu.VMEM((N_DEV, D_MODEL, W_IN_BLK), BF16), pltpu.VMEM((N_DEV, CONV_WIDTH, DW_BLK), F32),
                        pltpu.SemaphoreType.DMA((4, N_PEER)), pltpu.SemaphoreType.DMA((4, N_PEER))],
        compiler_params=pltpu.CompilerParams(vmem_limit_bytes=VMEM_LIMIT),
    )(w_in, w_out, pw_w, conv_dw)


R_LOSS, R_FINAL_G, R_NORM_G, R_POOL, R_CONV, R_LN = range(6)
VEC_ROWS = 16
DW_ROWS = 32


def _reduce_body(gwin_ref, gwout_ref, gpww_ref, loss_ref, gfg_ref, gng_ref, gvec_ref, gdw_ref, gpoolw_ref,
                 owin_ref, owout_ref, opww_ref, ovec_ref, odw_ref, opoolw_ref,
                 a_win, a_wout, a_pww, sb_win, sb_wout, sb_pww, rb_win, rb_wout, rb_pww, all_vec, all_dw, all_poolw,
                 send_a, recv_a, send_b, recv_b, send_g, recv_g):
    x, y, c, other_chips = _place()
    sibling = (x, y, 1 - c)
    k_me = 4 * x + 2 * y + c

    def sum8(v):
        return jnp.sum(v, axis=0, keepdims=True)

    vec = all_vec.at[k_me]
    vec[...] = jnp.zeros((VEC_ROWS, D_MODEL), F32)
    vec[R_LOSS:R_LOSS + 1, :] = jnp.broadcast_to(loss_ref[0:1, 0:1], (1, D_MODEL))
    vec[R_FINAL_G:R_FINAL_G + 1, :] = sum8(gfg_ref[...])
    vec[R_NORM_G:R_NORM_G + 1, :] = sum8(gng_ref[...])
    for row, (lo, hi) in ((R_POOL, (V_POOL_B, V_POOL_SCALE)), (R_CONV, (V_CONV_B, V_LN_G)), (R_LN, (V_LN_B, V_PW_B))):
        vec[row:row + 1, 0:W_MIX] = sum8(gvec_ref[lo])
        vec[row:row + 1, W_MIX:] = sum8(gvec_ref[hi])
    dwp = all_dw.at[k_me]
    dwp[0:CONV_WIDTH, :] = jnp.sum(gdw_ref[...], axis=1)
    dwp[CONV_WIDTH:, :] = jnp.zeros((DW_ROWS - CONV_WIDTH, W_MIX), F32)
    all_poolw[k_me] = gpoolw_ref[...].reshape(len(POOL_WINDOWS) * POOL_GC, POOL_GC)

    def block(k):
        return (gwin_ref.at[k], gwout_ref.at[pl.ds(pl.multiple_of(k * W_OUT_BLK, W_OUT_BLK), W_OUT_BLK), :],
                gpww_ref.at[pl.ds(pl.multiple_of(k * PW_BLK, PW_BLK), PW_BLK), :])

    def landed(chip):
        return (a_win.at[chip], a_wout.at[chip], a_pww.at[chip])

    step_a = []
    for px in range(2):
        for py in range(2):
            chip = 2 * px + py
            step_a += _remote(block(4 * px + 2 * py + (1 - c)), landed(chip), send_a, recv_a, chip, sibling)
    for cp in step_a:
        cp.start()

    def chip_sum(px, py):
        chip = 2 * px + py
        for cp in _remote(landed(chip), landed(chip), send_a, recv_a, chip, sibling):
            cp.wait_recv()
        return [m[...] + r[...] for m, r in zip(block(4 * px + 2 * py + c), landed(chip))]

    step_b = []
    for j, (px, py) in enumerate(other_chips):
        out = (sb_win.at[j], sb_wout.at[j], sb_pww.at[j])
        for o, v in zip(out, chip_sum(px, py)):
            o[...] = v.astype(BF16)
        cps = _remote(out, (rb_win.at[j], rb_wout.at[j], rb_pww.at[j]), send_b, recv_b, j, (px, py, c))
        for cp in cps:
            cp.start()
        step_b += cps
    total = chip_sum(x, y)

    def slots(px, py, pc):
        k = 4 * px + 2 * py + pc
        return (all_vec.at[k], all_dw.at[k], all_poolw.at[k])

    _all_gather(slots, send_g, recv_g)

    for j in range(len(other_chips)):
        got = (rb_win.at[j], rb_wout.at[j], rb_pww.at[j])
        for cp in _remote(got, got, send_b, recv_b, j, sibling):
            cp.wait_recv()
        total = [t + g[...].astype(F32) for t, g in zip(total, got)]
    owin_ref[...], owout_ref[...], opww_ref[...] = total
    for cp in step_a + step_b:
        cp.wait_send()

    for o, a in ((ovec_ref, all_vec), (odw_ref, all_dw), (opoolw_ref, all_poolw)):
        s = a[0]
        for k in range(1, N_DEV):
            s = s + a[k]
        o[...] = s


def _reduce_grads(gwin_blk, gwout, gpww, loss8, gfg8, gng8, gvec, gdw8, gpoolw):
    n_pw = len(POOL_WINDOWS) * POOL_GC
    blk = ((D_MODEL, W_IN_BLK), (W_OUT_BLK, D_MODEL), (PW_BLK, W_MIX))
    rep = ((VEC_ROWS, D_MODEL), (DW_ROWS, W_MIX), (n_pw, POOL_GC))
    return pl.pallas_call(
        functools.partial(_reduce_body),
        name="reduce",
        in_specs=[VMEM] * 9,
        out_specs=[VMEM] * 6,
        out_shape=[jax.ShapeDtypeStruct(s, F32) for s in blk + rep],
        scratch_shapes=[pltpu.VMEM((4,) + s, F32) for s in blk]
        + [pltpu.VMEM((3,) + s, BF16) for s in blk] * 2
        + [pltpu.VMEM((N_DEV,) + s, F32) for s in rep]
        + [pltpu.SemaphoreType.DMA((3, 4)), pltpu.SemaphoreType.DMA((3, 4)),
           pltpu.SemaphoreType.DMA((3, 3)), pltpu.SemaphoreType.DMA((3, 3)),
           pltpu.SemaphoreType.DMA((3, N_PEER)), pltpu.SemaphoreType.DMA((3, N_PEER))],
        compiler_params=pltpu.CompilerParams(vmem_limit_bytes=VMEM_LIMIT),
    )(gwin_blk, gwout, gpww, loss8, gfg8, gng8, gvec, gdw8, gpoolw)


ADAM_LR = 0.001
ADAM_B1 = 0.9
ADAM_B2 = 0.999
ADAM_EPS = 1e-08
ADAM_WD = 0.01
ADAM_STEP = 10


def _adamw_body(n, *refs):
    ws, gs, ms, vs = (refs[k * n:(k + 1) * n] for k in range(4))
    ds, m2s, v2s = (refs[(4 + k) * n:(5 + k) * n] for k in range(3))
    for w_ref, g_ref, m_ref, v_ref, d_ref, m2_ref, v2_ref in zip(ws, gs, ms, vs, ds, m2s, v2s):
        g = g_ref[...]
        m = ADAM_B1 * m_ref[...] + (1.0 - ADAM_B1) * g
        v = ADAM_B2 * v_ref[...] + (1.0 - ADAM_B2) * (g * g)
        m_hat = m / (1.0 - ADAM_B1 ** ADAM_STEP)
        v_hat = v / (1.0 - ADAM_B2 ** ADAM_STEP)
        d_ref[...] = -ADAM_LR * (m_hat / (jnp.sqrt(v_hat) + ADAM_EPS) + ADAM_WD * w_ref[...])
        m2_ref[...] = m
        v2_ref[...] = v


def _adamw(ws, gs, ms, vs):
    n = len(ws)
    shapes = [w.shape for w in ws]
    flat = lambda a: a.reshape(1, -1) if a.ndim == 1 else a
    args = [flat(a) for a in (*ws, *gs, *ms, *vs)]
    outs = pl.pallas_call(
        functools.partial(_adamw_body, n),
        name="adamw",
        in_specs=[VMEM] * (4 * n),
        out_specs=[VMEM] * (3 * n),
        out_shape=[jax.ShapeDtypeStruct(a.shape, F32) for a in args[:n]] * 3,
        compiler_params=pltpu.CompilerParams(vmem_limit_bytes=VMEM_LIMIT),
    )(*args)
    outs = [o.reshape(shapes[k % n]) for k, o in enumerate(outs)]
    return outs[:n], outs[n:2 * n], outs[2 * n:]


def kernel(x, norm_g, w_in, pool_w, pool_b, pool_scale, conv_dw, conv_b, ln_g, ln_b, pw_w, pw_b, w_out, final_g, loss_target, m_norm_g, m_w_in, m_pool_w, m_pool_b, m_pool_scale, m_conv_dw, m_conv_b, m_ln_g, m_ln_b, m_pw_w, m_pw_b, m_w_out, m_final_g, v_norm_g, v_w_in, v_pool_w, v_pool_b, v_pool_scale, v_conv_dw, v_conv_b, v_ln_g, v_ln_b, v_pw_w, v_pw_b, v_w_out, v_final_g):
    w_in_b, w_out_b, pw_w_b, conv_dw_f = _gather_weights(w_in, w_out, pw_w, conv_dw)
    proj, h1, z, ob, dpool, dh, loss8, gfg8 = _fwd(x, loss_target, norm_g, w_in_b, pool_w, pool_b, pool_scale, conv_dw_f,
                                                   conv_b, ln_g, ln_b, pw_w_b, pw_b, w_out_b, final_g)
    dproj, gwout, gpww, gpoolw, gvec, gdw8 = _bwd1(dh, proj, h1, z, ob, dpool, w_out_b, pw_w_b, pool_w, pool_scale,
                                                   conv_dw_f, ln_g, ln_b)
    grad_x, gwin_blk, gng8 = _bwd2(x, dh, dproj, norm_g, w_in_b)
    g_w_in, g_w_out, g_pw_w, vec, dw_sum, poolw_sum = _reduce_grads(gwin_blk, gwout, gpww, loss8, gfg8, gng8, gvec, gdw8, gpoolw)

    me = 4 * lax.axis_index("x") + 2 * lax.axis_index("y") + lax.axis_index("c")
    n_g = len(POOL_WINDOWS)
    grads = dict(
        norm_g=vec[R_NORM_G], w_in=g_w_in, pool_w=poolw_sum.reshape(n_g, POOL_GC, POOL_GC),
        pool_b=vec[R_POOL, :W_MIX].reshape(n_g, POOL_GC), pool_scale=vec[R_POOL, W_MIX:],
        conv_dw=lax.dynamic_slice(dw_sum, (0, me * DW_BLK), (CONV_WIDTH, DW_BLK)), conv_b=vec[R_CONV, :W_MIX],
        ln_g=vec[R_CONV, W_MIX:], ln_b=vec[R_LN, :W_MIX], pw_w=g_pw_w, pw_b=vec[R_LN, W_MIX:], w_out=g_w_out,
        final_g=vec[R_FINAL_G])
    weights = dict(norm_g=norm_g, w_in=w_in, pool_w=pool_w, pool_b=pool_b, pool_scale=pool_scale, conv_dw=conv_dw, conv_b=conv_b,
                   ln_g=ln_g, ln_b=ln_b, pw_w=pw_w, pw_b=pw_b, w_out=w_out, final_g=final_g)
    m_in = dict(norm_g=m_norm_g, w_in=m_w_in, pool_w=m_pool_w, pool_b=m_pool_b, pool_scale=m_pool_scale, conv_dw=m_conv_dw,
                conv_b=m_conv_b, ln_g=m_ln_g, ln_b=m_ln_b, pw_w=m_pw_w, pw_b=m_pw_b, w_out=m_w_out, final_g=m_final_g)
    v_in = dict(norm_g=v_norm_g, w_in=v_w_in, pool_w=v_pool_w, pool_b=v_pool_b, pool_scale=v_pool_scale, conv_dw=v_conv_dw,
                conv_b=v_conv_b, ln_g=v_ln_g, ln_b=v_ln_b, pw_w=v_pw_w, pw_b=v_pw_b, w_out=v_w_out, final_g=v_final_g)
    names = list(weights)
    delta, new_m, new_v = _adamw([weights[n] for n in names], [grads[n] for n in names], [m_in[n] for n in names],
                                 [v_in[n] for n in names])
    return (vec[R_LOSS, 0], grad_x, *[grads[n] for n in names], *delta, *new_m, *new_v)
```

```python
import functools

import jax
import jax.numpy as jnp
from jax import lax
from jax.experimental import pallas as pl
from jax.experimental.pallas import tpu as pltpu

F32 = jnp.float32
BF16 = jnp.bfloat16

D_MODEL = 1024
W_MIX = 512
D_IN = 5 * W_MIX
POOL_WINDOWS = (2, 4, 8, 16)
POOL_GC = 128
CONV_WIDTH = 31
RMS_EPS = 1e-6
LN_EPS = 1e-5
N_DEV = 8

LANES = 128
SUBLANES = 8
TILE_M = 512
POOL_HALO = 16
CONV_HALO = 32
CONV_ROWS = 128
VMEM_LIMIT = 56 * 1024 * 1024

VMEM = pl.BlockSpec(memory_space=pltpu.VMEM)


def _silu(v):
    return v * jax.nn.sigmoid(v)


def _dsilu(v):
    s = jax.nn.sigmoid(v)
    return s * (1.0 + v * (1.0 - s))


def _colsum8(v):
    m, n = v.shape
    return jnp.sum(v.reshape(m // SUBLANES, SUBLANES, n), axis=0)


def _row_index(shape, t0):
    return lax.broadcasted_iota(jnp.int32, shape, 0) + t0


def _conv_taps():
    taps = {s: [] for s in range(SUBLANES)}
    for k in range(CONV_WIDTH):
        e = k + CONV_HALO - (CONV_WIDTH - 1)
        taps[e % SUBLANES].append((e // SUBLANES, k))
    return taps


def _fwd_body(x_ref, tgt_ref, ng_ref, win_ref, poolw_ref, poolb_ref, pools_ref, dw_ref, cb_ref, lg_ref, lb_ref,
              pww_ref, pwb_ref, wout_ref, fg_ref,
              proj_ref, h1_ref, z_ref, ob_ref, dpool_ref, dh_ref, loss_ref, gfg_ref,
              uext, hext, y_s):
    b = pl.program_id(0)
    i = pl.program_id(1)
    tm = TILE_M

    @pl.when(i == 0)
    def _():
        uext[0:POOL_HALO, :] = jnp.zeros((POOL_HALO, W_MIX), F32)
        hext[0:CONV_HALO, :] = jnp.zeros((CONV_HALO, W_MIX), F32)
        hext[CONV_HALO + tm:, :] = jnp.zeros((SUBLANES, W_MIX), F32)

    @pl.when((b == 0) & (i == 0))
    def _():
        loss_ref[...] = jnp.zeros_like(loss_ref)
        gfg_ref[...] = jnp.zeros_like(gfg_ref)

    x = x_ref[0]
    r1 = lax.rsqrt(jnp.mean(x * x, axis=-1, keepdims=True) + RMS_EPS)
    hn = (x * r1 * ng_ref[...]).astype(BF16)
    proj_ref[0] = jnp.dot(hn, win_ref[...], preferred_element_type=F32)

    uext[POOL_HALO:, :] = proj_ref[0, :, 0:W_MIX]
    t = _row_index((tm, POOL_GC), i * tm)
    for g, w in enumerate(POOL_WINDOWS):
        lanes = slice(g * POOL_GC, (g + 1) * POOL_GC)
        e = uext[:, lanes]
        s = e
        sh = 1
        while sh < w:
            s = s + pltpu.roll(s, sh, 0)
            sh *= 2
        inv_cnt = 1.0 / jnp.minimum(t + 1, w).astype(F32)
        d = (s[POOL_HALO:] * inv_cnt - e[POOL_HALO:]).astype(BF16)
        dpool_ref[0, :, lanes] = d
        z = jnp.dot(d, poolw_ref[g].astype(BF16), preferred_element_type=F32) + poolb_ref[:, lanes]
        z_ref[0, :, lanes] = z
        a_gate = proj_ref[0, :, W_MIX + g * POOL_GC:W_MIX + (g + 1) * POOL_GC]
        y_s[:, lanes] = (z * pools_ref[:, lanes] * _silu(a_gate)).astype(BF16)

    hext[CONV_HALO:CONV_HALO + tm, :] = proj_ref[0, :, 2 * W_MIX:3 * W_MIX] * jax.nn.sigmoid(proj_ref[0, :, 3 * W_MIX:4 * W_MIX])
    taps = _conv_taps()
    for c0 in range(0, tm, CONV_ROWS):
        for l0 in range(0, W_MIX, LANES):
            lanes = slice(l0, l0 + LANES)
            acc = jnp.zeros((CONV_ROWS, LANES), F32) + cb_ref[:, lanes]
            for s in range(SUBLANES):
                part = jnp.zeros((CONV_ROWS + SUBLANES, LANES), F32)
                for q, k in taps[s]:
                    r0 = c0 + SUBLANES * q
                    part = part + hext[r0:r0 + CONV_ROWS + SUBLANES, lanes] * dw_ref[k:k + 1, lanes]
                acc = acc + part[s:s + CONV_ROWS]
            h1_ref[0, c0:c0 + CONV_ROWS, lanes] = acc
    h1 = h1_ref[0]
    mu = jnp.mean(h1, axis=-1, keepdims=True)
    xc = h1 - mu
    rstd = lax.rsqrt(jnp.mean(xc * xc, axis=-1, keepdims=True) + LN_EPS)
    h3 = _silu(xc * rstd * lg_ref[...] + lb_ref[...])
    ob = jnp.dot(h3.astype(BF16), pww_ref[...], preferred_element_type=F32) + pwb_ref[...]
    ob_ref[0] = ob
    y_s[:, W_MIX:] = (ob * _silu(proj_ref[0, :, 4 * W_MIX:5 * W_MIX])).astype(BF16)

    h = x + jnp.dot(y_s[...], wout_ref[...], preferred_element_type=F32)
    r2 = lax.rsqrt(jnp.mean(h * h, axis=-1, keepdims=True) + RMS_EPS)
    hr = h * r2
    err = hr * fg_ref[...] - tgt_ref[0]
    loss_ref[...] += jnp.sum(err * err) * (0.5 / D_MODEL)
    dout = err * (1.0 / D_MODEL)
    gfg_ref[...] += _colsum8(dout * hr)
    gy = dout * fg_ref[...]
    dh_ref[0] = r2 * (gy - hr * jnp.mean(gy * hr, axis=-1, keepdims=True))

    uext[0:POOL_HALO, :] = uext[tm:tm + POOL_HALO, :]
    hext[0:CONV_HALO, :] = hext[tm:tm + CONV_HALO, :]


def _row(v):
    return v.reshape(1, -1)


def _fwd(x, tgt, norm_g, w_in_b, pool_w, pool_b, pool_scale, conv_dw_f, conv_b, ln_g, ln_b, pw_w_b, pw_b, w_out_b, final_g):
    bl, s, _ = x.shape
    tm = TILE_M
    tok = lambda n: pl.BlockSpec((1, tm, n), lambda b, i: (b, i, 0))
    acc = lambda n: pl.BlockSpec((SUBLANES, n), lambda b, i: (0, 0))
    act = lambda n: jax.ShapeDtypeStruct((bl, s, n), F32)
    return pl.pallas_call(
        functools.partial(_fwd_body),
        name="fwd",
        grid=(bl, s // tm),
        in_specs=[tok(D_MODEL), tok(D_MODEL)] + [VMEM] * 13,
        out_specs=[tok(D_IN), tok(W_MIX), tok(W_MIX), tok(W_MIX), tok(W_MIX), tok(D_MODEL), acc(LANES), acc(D_MODEL)],
        out_shape=[act(D_IN), act(W_MIX), act(W_MIX), act(W_MIX), jax.ShapeDtypeStruct((bl, s, W_MIX), BF16), act(D_MODEL),
                   jax.ShapeDtypeStruct((SUBLANES, LANES), F32), jax.ShapeDtypeStruct((SUBLANES, D_MODEL), F32)],
        scratch_shapes=[pltpu.VMEM((POOL_HALO + tm, W_MIX), F32), pltpu.VMEM((CONV_HALO + tm + SUBLANES, W_MIX), F32),
                        pltpu.VMEM((tm, D_MODEL), BF16)],
        compiler_params=pltpu.CompilerParams(dimension_semantics=("arbitrary", "arbitrary"), vmem_limit_bytes=VMEM_LIMIT),
    )(x, tgt, _row(norm_g), w_in_b, pool_w, pool_b.reshape(1, W_MIX), _row(pool_scale), conv_dw_f, _row(conv_b),
      _row(ln_g), _row(ln_b), pw_w_b, _row(pw_b), w_out_b, _row(final_g))


NT_DIMS = (((1,), (1,)), ((), ()))
TN_DIMS = (((0,), (0,)), ((), ()))
V_POOL_B, V_POOL_SCALE, V_CONV_B, V_LN_G, V_LN_B, V_PW_B = range(6)


def _bwd_taps():
    taps = {s: [] for s in range(SUBLANES)}
    for k in range(CONV_WIDTH):
        e = CONV_WIDTH - 1 - k
        taps[e % SUBLANES].append((e // SUBLANES, k))
    return taps


def _bwd1_body(n_tiles, dh_ref, proj_ref, h1_ref, z_ref, ob_ref, dpool_ref, wout_ref, pww_ref, poolw_ref, pools_ref, dw_ref,
               lg_ref, lb_ref,
               dproj_ref, gwout_ref, gpww_ref, gpoolw_ref, gvec_ref, gdw_ref,
               vext, gext, gsh, y_s):
    b = pl.program_id(0)
    i = pl.program_id(1)
    tm = TILE_M
    t0 = (n_tiles - 1 - i) * tm

    @pl.when(i == 0)
    def _():
        vext[tm:, :] = jnp.zeros((POOL_HALO, W_MIX), F32)
        gext[tm:, :] = jnp.zeros((CONV_HALO, W_MIX), F32)

    @pl.when((b == 0) & (i == 0))
    def _():
        gwout_ref[...] = jnp.zeros_like(gwout_ref)
        gpww_ref[...] = jnp.zeros_like(gpww_ref)
        gpoolw_ref[...] = jnp.zeros_like(gpoolw_ref)
        gvec_ref[...] = jnp.zeros_like(gvec_ref)
        gdw_ref[...] = jnp.zeros_like(gdw_ref)

    dhb = dh_ref[0].astype(BF16)
    dy_a = lax.dot_general(dhb, wout_ref[0:W_MIX, :], NT_DIMS, preferred_element_type=F32)
    dy_b = lax.dot_general(dhb, wout_ref[W_MIX:, :], NT_DIMS, preferred_element_type=F32)

    a_gate = proj_ref[0, :, W_MIX:2 * W_MIX]
    sig_a = jax.nn.sigmoid(a_gate)
    silu_a = a_gate * sig_a
    z = z_ref[0]
    out_a = z * pools_ref[...]
    y_s[:, 0:W_MIX] = (out_a * silu_a).astype(BF16)
    d_out_a = dy_a * silu_a
    d_ag = dy_a * out_a * (sig_a * (1.0 + a_gate * (1.0 - sig_a)))
    dproj_ref[0, :, W_MIX:2 * W_MIX] = d_ag.astype(BF16)
    gvec_ref[V_POOL_SCALE] += _colsum8(d_out_a * z)
    dz = d_out_a * pools_ref[...]
    gvec_ref[V_POOL_B] += _colsum8(dz)
    t = _row_index((tm, POOL_GC), t0)
    n_ext = tm + POOL_HALO
    for g, w in enumerate(POOL_WINDOWS):
        lanes = slice(g * POOL_GC, (g + 1) * POOL_GC)
        dzg = dz[:, lanes].astype(BF16)
        gpoolw_ref[g] += lax.dot_general(dpool_ref[0, :, lanes], dzg, TN_DIMS, preferred_element_type=F32)
        dd = lax.dot_general(dzg, poolw_ref[g].astype(BF16), NT_DIMS, preferred_element_type=F32)
        vext[0:tm, lanes] = dd * (1.0 / jnp.minimum(t + 1, w).astype(F32))
        s = vext[:, lanes]
        sh = 1
        while sh < w:
            s = s + pltpu.roll(s, n_ext - sh, 0)
            sh *= 2
        dproj_ref[0, :, lanes] = (s[0:tm] - dd).astype(BF16)
    vext[tm:, :] = vext[0:POOL_HALO, :]

    b_gate = proj_ref[0, :, 4 * W_MIX:5 * W_MIX]
    sig_b = jax.nn.sigmoid(b_gate)
    silu_b = b_gate * sig_b
    ob = ob_ref[0]
    y_s[:, W_MIX:] = (ob * silu_b).astype(BF16)
    d_ob = dy_b * silu_b
    dproj_ref[0, :, 4 * W_MIX:5 * W_MIX] = (dy_b * ob * (sig_b * (1.0 + b_gate * (1.0 - sig_b)))).astype(BF16)
    gvec_ref[V_PW_B] += _colsum8(d_ob)
    gwout_ref[...] += lax.dot_general(y_s[...], dhb, TN_DIMS, preferred_element_type=F32)

    h1 = h1_ref[0]
    mu = jnp.mean(h1, axis=-1, keepdims=True)
    xc = h1 - mu
    rstd = lax.rsqrt(jnp.mean(xc * xc, axis=-1, keepdims=True) + LN_EPS)
    xhat = xc * rstd
    h2 = xhat * lg_ref[...] + lb_ref[...]
    sig2 = jax.nn.sigmoid(h2)
    d_obb = d_ob.astype(BF16)
    gpww_ref[...] += lax.dot_general((h2 * sig2).astype(BF16), d_obb, TN_DIMS, preferred_element_type=F32)
    dh2 = lax.dot_general(d_obb, pww_ref[...], NT_DIMS, preferred_element_type=F32) * (sig2 * (1.0 + h2 * (1.0 - sig2)))
    gvec_ref[V_LN_G] += _colsum8(dh2 * xhat)
    gvec_ref[V_LN_B] += _colsum8(dh2)
    dxh = dh2 * lg_ref[...]
    dh1 = rstd * (dxh - jnp.mean(dxh, axis=-1, keepdims=True) - xhat * jnp.mean(dxh * xhat, axis=-1, keepdims=True))
    gvec_ref[V_CONV_B] += _colsum8(dh1)
    gext[0:tm, :] = dh1

    taps = _bwd_taps()
    win = CONV_ROWS + CONV_HALO - SUBLANES
    for c0 in range(0, tm, CONV_ROWS):
        for l0 in range(0, W_MIX, LANES):
            lanes = slice(l0, l0 + LANES)
            b_val = proj_ref[0, c0:c0 + CONV_ROWS, 2 * W_MIX + l0:2 * W_MIX + l0 + LANES]
            sg = jax.nn.sigmoid(proj_ref[0, c0:c0 + CONV_ROWS, 3 * W_MIX + l0:3 * W_MIX + l0 + LANES])
            h0c = b_val * sg
            acc = jnp.zeros((CONV_ROWS, LANES), F32)
            for s in range(SUBLANES):
                gsh[...] = gext[c0 + s:c0 + s + win, lanes]
                for q, k in taps[s]:
                    gq = gsh[SUBLANES * q:SUBLANES * q + CONV_ROWS, :]
                    acc = acc + gq * dw_ref[k:k + 1, lanes]
                    gdw_ref[k, :, lanes] += _colsum8(h0c * gq)
            d_bval = acc * sg
            dproj_ref[0, c0:c0 + CONV_ROWS, 2 * W_MIX + l0:2 * W_MIX + l0 + LANES] = d_bval.astype(BF16)
            dproj_ref[0, c0:c0 + CONV_ROWS, 3 * W_MIX + l0:3 * W_MIX + l0 + LANES] = (
                d_bval * b_val * (1.0 - sg)).astype(BF16)
    gext[tm:, :] = gext[0:CONV_HALO, :]


def _bwd1(dh, proj, h1, z, ob, dpool, w_out_b, pw_w_b, pool_w, pool_scale, conv_dw_f, ln_g, ln_b):
    bl, s, _ = dh.shape
    tm = TILE_M
    nt = s // tm
    tok = lambda n: pl.BlockSpec((1, tm, n), lambda b, i: (b, nt - 1 - i, 0))
    res = lambda shape: pl.BlockSpec(shape, lambda b, i: (0,) * len(shape))
    out_shapes = [(D_MODEL, D_MODEL), (W_MIX, W_MIX), (len(POOL_WINDOWS), POOL_GC, POOL_GC), (6, SUBLANES, W_MIX),
                  (CONV_WIDTH, SUBLANES, W_MIX)]
    return pl.pallas_call(
        functools.partial(_bwd1_body, nt),
        name="bwd1",
        grid=(bl, nt),
        in_specs=[tok(D_MODEL), tok(D_IN), tok(W_MIX), tok(W_MIX), tok(W_MIX), tok(W_MIX)] + [VMEM] * 7,
        out_specs=[tok(D_IN)] + [res(sh) for sh in out_shapes],
        out_shape=[jax.ShapeDtypeStruct((bl, s, D_IN), BF16)] + [jax.ShapeDtypeStruct(sh, F32) for sh in out_shapes],
        scratch_shapes=[pltpu.VMEM((tm + POOL_HALO, W_MIX), F32), pltpu.VMEM((tm + CONV_HALO, W_MIX), F32),
                        pltpu.VMEM((CONV_ROWS + CONV_HALO - SUBLANES, LANES), F32), pltpu.VMEM((tm, D_MODEL), BF16)],
        compiler_params=pltpu.CompilerParams(dimension_semantics=("arbitrary", "arbitrary"), vmem_limit_bytes=VMEM_LIMIT),
    )(dh, proj, h1, z, ob, dpool, w_out_b, pw_w_b, pool_w, _row(pool_scale), conv_dw_f, _row(ln_g), _row(ln_b))


MESH = pl.DeviceIdType.MESH
W_IN_BLK = D_IN // N_DEV
W_OUT_BLK = D_MODEL // N_DEV
PW_BLK = W_MIX // N_DEV
DW_BLK = W_MIX // N_DEV
N_PEER = N_DEV - 1


def _place():
    x, y, c = lax.axis_index("x"), lax.axis_index("y"), lax.axis_index("c")
    other_chips = [(1 - x, y), (x, 1 - y), (1 - x, 1 - y)]
    return x, y, c, other_chips


def _remote(srcs, dsts, send_sems, recv_sems, k, to):
    return [pltpu.make_async_remote_copy(src_ref=s, dst_ref=d, send_sem=send_sems.at[a, k], recv_sem=recv_sems.at[a, k],
                                         device_id=to, device_id_type=MESH)
            for a, (s, d) in enumerate(zip(srcs, dsts))]


def _all_gather(slots, send_sems, recv_sems):
    x, y, c, other_chips = _place()
    me, sibling = (x, y, c), (x, y, 1 - c)

    def copy(k, block, to):
        refs = slots(*block)
        return _remote(refs, refs, send_sems, recv_sems, k, to)

    first = copy(0, me, sibling)
    for j, chip in enumerate(other_chips):
        first += copy(1 + j, me, (*chip, c))
    for cp in first:
        cp.start()
    passed = []
    for j, chip in enumerate(other_chips):
        for cp in copy(1 + j, (*chip, c), me):
            cp.wait_recv()
        fwd = copy(4 + j, (*chip, c), sibling)
        for cp in fwd:
            cp.start()
        passed += fwd
    for cp in copy(0, sibling, me):
        cp.wait_recv()
    for j, chip in enumerate(other_chips):
        for cp in copy(4 + j, (*chip, 1 - c), me):
            cp.wait_recv()
    for cp in first + passed:
        cp.wait_send()


def _gather_body(win_ref, wout_ref, pww_ref, dw_ref, winf_ref, woutf_ref, pwwf_ref, dwf_ref,
                 win_blk, dw_blk, send_sems, recv_sems):
    def slots(px, py, pc):
        k = 4 * px + 2 * py + pc
        return (win_blk.at[k], woutf_ref.at[pl.ds(pl.multiple_of(k * W_OUT_BLK, W_OUT_BLK), W_OUT_BLK), :],
                pwwf_ref.at[pl.ds(pl.multiple_of(k * PW_BLK, PW_BLK), PW_BLK), :], dw_blk.at[k])

    x, y, c, _ = _place()
    mine = slots(x, y, c)
    mine[0][...] = win_ref[...].astype(BF16)
    mine[1][...] = wout_ref[...].astype(BF16)
    mine[2][...] = pww_ref[...].astype(BF16)
    mine[3][...] = dw_ref[...]
    _all_gather(slots, send_sems, recv_sems)
    for k in range(N_DEV):
        winf_ref[:, k * W_IN_BLK:(k + 1) * W_IN_BLK] = win_blk[k]
        dwf_ref[:, k * DW_BLK:(k + 1) * DW_BLK] = dw_blk[k]


def _gather_weights(w_in, w_out, pw_w, conv_dw):
    return pl.pallas_call(
        functools.partial(_gather_body),
        name="gather",
        in_specs=[VMEM] * 4,
        out_specs=[VMEM] * 4,
        out_shape=[jax.ShapeDtypeStruct((D_MODEL, D_IN), BF16), jax.ShapeDtypeStruct((D_MODEL, D_MODEL), BF16),
                   jax.ShapeDtypeStruct((W_MIX, W_MIX), BF16), jax.ShapeDtypeStruct((CONV_WIDTH, W_MIX), F32)],
        scratch_shapes=[pltpu.VMEM((N_DEV, D_MODEL, W_IN_BLK), BF16), pltpu.VMEM((N_DEV, CONV_WIDTH, DW_BLK), F32),
                        pltpu.SemaphoreType.DMA((4, N_PEER)), pltpu.SemaphoreType.DMA((4, N_PEER))],
        compiler_params=pltpu.CompilerParams(vmem_limit_bytes=VMEM_LIMIT),
    )(w_in, w_out, pw_w, conv_dw)


R_LOSS, R_FINAL_G, R_POOL, R_CONV, R_LN = range(5)
VEC_ROWS = 16
DW_ROWS = 32


def _ag_start(slots, send_sems, recv_sems):
    x, y, c, other_chips = _place()
    mine = slots(x, y, c)
    for k, to in enumerate([(x, y, 1 - c)] + [(*chip, c) for chip in other_chips]):
        for cp in _remote(mine, mine, send_sems, recv_sems, k, to):
            cp.start()


def _ag_forward(slots, send_sems, recv_sems):
    x, y, c, other_chips = _place()
    for j, chip in enumerate(other_chips):
        refs = slots(*chip, c)
        for cp in _remote(refs, refs, send_sems, recv_sems, 1 + j, (x, y, 1 - c)):
            cp.wait_recv()
        for cp in _remote(refs, refs, send_sems, recv_sems, 4 + j, (x, y, 1 - c)):
            cp.start()


def _ag_finish(slots, send_sems, recv_sems):
    x, y, c, other_chips = _place()
    sibling = (x, y, 1 - c)
    for k, block in enumerate([(x, y, 1 - c)] + [None] * 3 + [(*chip, 1 - c) for chip in other_chips]):
        if block is not None:
            refs = slots(*block)
            for cp in _remote(refs, refs, send_sems, recv_sems, k, sibling):
                cp.wait_recv()
    mine = slots(x, y, c)
    for k, block in enumerate([mine] * 4 + [slots(*chip, c) for chip in other_chips]):
        for cp in _remote(block, block, send_sems, recv_sems, k, sibling):
            cp.wait_send()


def _rs_send_a(to_sibling, landed, send_a, recv_a):
    x, y, c, _ = _place()
    for chip in range(4):
        for cp in _remote(to_sibling(2 * chip + (1 - c)), landed(chip), send_a, recv_a, chip, (x, y, 1 - c)):
            cp.start()


def _rs_chip_sum(mine, landed, send_a, recv_a, chip):
    x, y, c, _ = _place()
    for cp in _remote(landed(chip), landed(chip), send_a, recv_a, chip, (x, y, 1 - c)):
        cp.wait_recv()
    return [m[...] + r[...] for m, r in zip(mine(chip), landed(chip))]


def _rs_send_b(mine, landed, staged, arrived, send_a, recv_a, send_b, recv_b):
    x, y, c, other_chips = _place()
    for j, (px, py) in enumerate(other_chips):
        for o, v in zip(staged(j), _rs_chip_sum(mine, landed, send_a, recv_a, 2 * px + py)):
            o[...] = v.astype(BF16)
        for cp in _remote(staged(j), arrived(j), send_b, recv_b, j, (px, py, c)):
            cp.start()


def _rs_finish(to_sibling, mine, landed, staged, arrived, send_a, recv_a, send_b, recv_b):
    x, y, c, _ = _place()
    sibling = (x, y, 1 - c)
    total = _rs_chip_sum(mine, landed, send_a, recv_a, 2 * x + y)
    for j in range(3):
        for cp in _remote(arrived(j), arrived(j), send_b, recv_b, j, sibling):
            cp.wait_recv()
        total = [t + g[...].astype(F32) for t, g in zip(total, arrived(j))]
    for chip in range(4):
        for cp in _remote(to_sibling(2 * chip + (1 - c)), landed(chip), send_a, recv_a, chip, sibling):
            cp.wait_send()
    for j in range(3):
        for cp in _remote(staged(j), arrived(j), send_b, recv_b, j, sibling):
            cp.wait_send()
    return total


EXCHANGE_STEP = 2


def _bwd2a_body(n_steps, x_ref, dproj_ref, ng_ref, gwout_ref, gpww_ref, loss_ref, gfg_ref, gvec_ref, gdw_ref, gpoolw_ref,
                gwin_ref, owout_ref, opww_ref, ovec_ref, odw_ref, opoolw_ref,
                acc, a_wout, a_pww, sb_wout, sb_pww, rb_wout, rb_pww, all_vec, all_dw, all_poolw,
                send_a, recv_a, send_b, recv_b, send_g, recv_g):
    step = pl.program_id(0) * pl.num_programs(1) + pl.program_id(1)
    x, y, c, _ = _place()

    def rows(ref, k, n):
        return ref.at[pl.ds(pl.multiple_of(k * n, n), n), :]

    to_sibling = lambda k: (rows(gwout_ref, k, W_OUT_BLK), rows(gpww_ref, k, PW_BLK))
    mine = lambda chip: to_sibling(2 * chip + c)
    landed = lambda chip: (a_wout.at[chip], a_pww.at[chip])
    staged = lambda j: (sb_wout.at[j], sb_pww.at[j])
    arrived = lambda j: (rb_wout.at[j], rb_pww.at[j])

    def slots(px, py, pc):
        k = 4 * px + 2 * py + pc
        return (all_vec.at[k], all_dw.at[k], all_poolw.at[k])

    @pl.when(step == 0)
    def _():
        acc[...] = jnp.zeros_like(acc)
        sum8 = lambda v: jnp.sum(v, axis=0, keepdims=True)
        vec, dwp, pwp = slots(x, y, c)
        vec[...] = jnp.zeros((VEC_ROWS, D_MODEL), F32)
        vec[R_LOSS:R_LOSS + 1, :] = jnp.broadcast_to(loss_ref[0:1, 0:1], (1, D_MODEL))
        vec[R_FINAL_G:R_FINAL_G + 1, :] = sum8(gfg_ref[...])
        for row, (lo, hi) in ((R_POOL, (V_POOL_B, V_POOL_SCALE)), (R_CONV, (V_CONV_B, V_LN_G)), (R_LN, (V_LN_B, V_PW_B))):
            vec[row:row + 1, 0:W_MIX] = sum8(gvec_ref[lo])
            vec[row:row + 1, W_MIX:] = sum8(gvec_ref[hi])
        dwp[0:CONV_WIDTH, :] = jnp.sum(gdw_ref[...], axis=1)
        dwp[CONV_WIDTH:, :] = jnp.zeros((DW_ROWS - CONV_WIDTH, W_MIX), F32)
        pwp[...] = gpoolw_ref[...].reshape(len(POOL_WINDOWS) * POOL_GC, POOL_GC)
        _rs_send_a(to_sibling, landed, send_a, recv_a)
        _ag_start(slots, send_g, recv_g)

    x_t = x_ref[0]
    r1 = lax.rsqrt(jnp.mean(x_t * x_t, axis=-1, keepdims=True) + RMS_EPS)
    hn = (x_t * r1 * ng_ref[...]).astype(BF16)
    acc[...] += lax.dot_general(hn, dproj_ref[0], TN_DIMS, preferred_element_type=F32)

    @pl.when(step == EXCHANGE_STEP)
    def _():
        _rs_send_b(mine, landed, staged, arrived, send_a, recv_a, send_b, recv_b)
        _ag_forward(slots, send_g, recv_g)

    @pl.when(step == n_steps - 1)
    def _():
        for k in range(N_DEV):
            gwin_ref[k] = acc[:, k * W_IN_BLK:(k + 1) * W_IN_BLK]
        owout_ref[...], opww_ref[...] = _rs_finish(to_sibling, mine, landed, staged, arrived, send_a, recv_a, send_b, recv_b)
        _ag_finish(slots, send_g, recv_g)
        for o, a in ((ovec_ref, all_vec), (odw_ref, all_dw), (opoolw_ref, all_poolw)):
            s = a[0]
            for k in range(1, N_DEV):
                s = s + a[k]
            o[...] = s


def _bwd2a(x, dproj, norm_g, gwout, gpww, loss8, gfg8, gvec, gdw8, gpoolw):
    bl, s, _ = x.shape
    tm = TILE_M
    nt = s // tm
    tok = lambda n: pl.BlockSpec((1, tm, n), lambda b, i: (b, i, 0))
    n_pw = len(POOL_WINDOWS) * POOL_GC
    blk = ((W_OUT_BLK, D_MODEL), (PW_BLK, W_MIX))
    rep = ((VEC_ROWS, D_MODEL), (DW_ROWS, W_MIX), (n_pw, POOL_GC))
    dma = pltpu.SemaphoreType.DMA
    return pl.pallas_call(
        functools.partial(_bwd2a_body, bl * nt),
        name="bwd2a",
        grid=(bl, nt),
        in_specs=[tok(D_MODEL), tok(D_IN)] + [VMEM] * 8,
        out_specs=[VMEM] * 6,
        out_shape=[jax.ShapeDtypeStruct((N_DEV, D_MODEL, W_IN_BLK), F32)] + [jax.ShapeDtypeStruct(sh, F32) for sh in blk + rep],
        scratch_shapes=[pltpu.VMEM((D_MODEL, D_IN), F32)]
        + [pltpu.VMEM((4,) + sh, F32) for sh in blk]
        + [pltpu.VMEM((3,) + sh, BF16) for sh in blk] * 2
        + [pltpu.VMEM((N_DEV,) + sh, F32) for sh in rep]
        + [dma((2, 4)), dma((2, 4)), dma((2, 3)), dma((2, 3)), dma((3, N_PEER)), dma((3, N_PEER))],
        compiler_params=pltpu.CompilerParams(dimension_semantics=("arbitrary", "arbitrary"), vmem_limit_bytes=VMEM_LIMIT),
    )(x, dproj, _row(norm_g), gwout, gpww, loss8, gfg8, gvec, gdw8, gpoolw)


def _bwd2b_body(n_steps, x_ref, dh_ref, dproj_ref, ng_ref, win_ref, gwin_hbm,
                gx_ref, owin_ref, ong_ref,
                gng, mine_win, a_win, sb_win, rb_win, all_ng, local_sems, send_a, recv_a, send_b, recv_b, send_n, recv_n):
    step = pl.program_id(0) * pl.num_programs(1) + pl.program_id(1)
    x, y, c, other_chips = _place()
    sibling = (x, y, 1 - c)

    to_sibling = lambda k: (gwin_hbm.at[k],)
    mine = lambda chip: (mine_win.at[chip],)
    landed = lambda chip: (a_win.at[chip],)
    staged = lambda j: (sb_win.at[j],)
    arrived = lambda j: (rb_win.at[j],)
    fetch = lambda chip: pltpu.make_async_copy(gwin_hbm.at[2 * chip + c], mine_win.at[chip], local_sems.at[chip])

    @pl.when(step == 0)
    def _():
        gng[...] = jnp.zeros_like(gng)
        for chip in range(4):
            fetch(chip).start()
        _rs_send_a(to_sibling, landed, send_a, recv_a)

    x_t = x_ref[0]
    r1 = lax.rsqrt(jnp.mean(x_t * x_t, axis=-1, keepdims=True) + RMS_EPS)
    xr = x_t * r1
    dhn = lax.dot_general(dproj_ref[0], win_ref[...], NT_DIMS, preferred_element_type=F32)
    gng[...] += _colsum8(dhn * xr)
    gy = dhn * ng_ref[...]
    gx_ref[0] = dh_ref[0] + r1 * (gy - xr * jnp.mean(gy * xr, axis=-1, keepdims=True))

    @pl.when(step == EXCHANGE_STEP)
    def _():
        for chip in range(4):
            fetch(chip).wait()
        _rs_send_b(mine, landed, staged, arrived, send_a, recv_a, send_b, recv_b)

    k_me = 4 * x + 2 * y + c
    peers = [sibling] + [(*chip, c) for chip in other_chips] + [(*chip, 1 - c) for chip in other_chips]
    ng_copy = lambda r: pltpu.make_async_remote_copy(src_ref=all_ng.at[k_me], dst_ref=all_ng.at[k_me], send_sem=send_n.at[r],
                                                     recv_sem=recv_n.at[r], device_id=peers[r], device_id_type=MESH)

    @pl.when(step == n_steps - 1)
    def _():
        all_ng[k_me] = gng[...]
        for r in range(N_PEER):
            ng_copy(r).start()
        (owin_ref[...],) = _rs_finish(to_sibling, mine, landed, staged, arrived, send_a, recv_a, send_b, recv_b)
        for r in range(N_PEER):
            ng_copy(r).wait()
        s = all_ng[0]
        for k in range(1, N_DEV):
            s = s + all_ng[k]
        ong_ref[...] = jnp.broadcast_to(jnp.sum(s, axis=0, keepdims=True), ong_ref.shape)


def _bwd2b(x, dh, dproj, norm_g, w_in_b, gwin_blk):
    bl, s, _ = x.shape
    tm = TILE_M
    nt = s // tm
    tok = lambda n: pl.BlockSpec((1, tm, n), lambda b, i: (b, i, 0))
    blk = (D_MODEL, W_IN_BLK)
    dma = pltpu.SemaphoreType.DMA
    return pl.pallas_call(
        functools.partial(_bwd2b_body, bl * nt),
        name="bwd2b",
        grid=(bl, nt),
        in_specs=[tok(D_MODEL), tok(D_MODEL), tok(D_IN), VMEM, VMEM, pl.BlockSpec(memory_space=pl.ANY)],
        out_specs=[tok(D_MODEL), VMEM, VMEM],
        out_shape=[jax.ShapeDtypeStruct((bl, s, D_MODEL), F32), jax.ShapeDtypeStruct(blk, F32),
                   jax.ShapeDtypeStruct((SUBLANES, D_MODEL), F32)],
        scratch_shapes=[pltpu.VMEM((SUBLANES, D_MODEL), F32), pltpu.VMEM((4,) + blk, F32), pltpu.VMEM((4,) + blk, F32),
                        pltpu.VMEM((3,) + blk, BF16), pltpu.VMEM((3,) + blk, BF16), pltpu.VMEM((N_DEV, SUBLANES, D_MODEL), F32),
                        dma((4,)), dma((1, 4)), dma((1, 4)), dma((1, 3)), dma((1, 3)), dma((N_PEER,)), dma((N_PEER,))],
        compiler_params=pltpu.CompilerParams(dimension_semantics=("arbitrary", "arbitrary"), vmem_limit_bytes=VMEM_LIMIT),
    )(x, dh, dproj, _row(norm_g), w_in_b, gwin_blk)


ADAM_LR = 0.001
ADAM_B1 = 0.9
ADAM_B2 = 0.999
ADAM_EPS = 1e-08
ADAM_WD = 0.01
ADAM_STEP = 10


def _adamw_body(n, *refs):
    ws, gs, ms, vs = (refs[k * n:(k + 1) * n] for k in range(4))
    outs = refs[4 * n + 1:]
    ds, m2s, v2s = (outs[k * n:(k + 1) * n] for k in range(3))
    for w_ref, g_ref, m_ref, v_ref, d_ref, m2_ref, v2_ref in zip(ws, gs, ms, vs, ds, m2s, v2s):
        g = g_ref[...]
        m = ADAM_B1 * m_ref[...] + (1.0 - ADAM_B1) * g
        v = ADAM_B2 * v_ref[...] + (1.0 - ADAM_B2) * (g * g)
        m_hat = m / (1.0 - ADAM_B1 ** ADAM_STEP)
        v_hat = v / (1.0 - ADAM_B2 ** ADAM_STEP)
        d_ref[...] = -ADAM_LR * (m_hat / (jnp.sqrt(v_hat) + ADAM_EPS) + ADAM_WD * w_ref[...])
        m2_ref[...] = m
        v2_ref[...] = v


def _adamw(ws, gs, ms, vs, carried):
    n = len(ws)
    shapes = [w.shape for w in ws]
    flat = lambda a: a.reshape(1, -1) if a.ndim == 1 else a
    args = [flat(a) for a in (*ws, *gs, *ms, *vs)]
    hbm = pl.BlockSpec(memory_space=pl.ANY)
    outs = pl.pallas_call(
        functools.partial(_adamw_body, n),
        name="adamw",
        in_specs=[VMEM] * (4 * n) + [hbm],
        out_specs=[VMEM] * (3 * n) + [hbm],
        out_shape=[jax.ShapeDtypeStruct(a.shape, F32) for a in args[:n]] * 3 + [jax.ShapeDtypeStruct(carried.shape, carried.dtype)],
        input_output_aliases={4 * n: 3 * n},
        compiler_params=pltpu.CompilerParams(vmem_limit_bytes=VMEM_LIMIT),
    )(*args, carried)
    carried = outs[3 * n]
    outs = [o.reshape(shapes[k % n]) for k, o in enumerate(outs[:3 * n])]
    return outs[:n], outs[n:2 * n], outs[2 * n:], carried


def kernel(x, norm_g, w_in, pool_w, pool_b, pool_scale, conv_dw, conv_b, ln_g, ln_b, pw_w, pw_b, w_out, final_g, loss_target, m_norm_g, m_w_in, m_pool_w, m_pool_b, m_pool_scale, m_conv_dw, m_conv_b, m_ln_g, m_ln_b, m_pw_w, m_pw_b, m_w_out, m_final_g, v_norm_g, v_w_in, v_pool_w, v_pool_b, v_pool_scale, v_conv_dw, v_conv_b, v_ln_g, v_ln_b, v_pw_w, v_pw_b, v_w_out, v_final_g):
    w_in_b, w_out_b, pw_w_b, conv_dw_f = _gather_weights(w_in, w_out, pw_w, conv_dw)
    proj, h1, z, ob, dpool, dh, loss8, gfg8 = _fwd(x, loss_target, norm_g, w_in_b, pool_w, pool_b, pool_scale, conv_dw_f,
                                                   conv_b, ln_g, ln_b, pw_w_b, pw_b, w_out_b, final_g)
    dproj, gwout, gpww, gpoolw, gvec, gdw8 = _bwd1(dh, proj, h1, z, ob, dpool, w_out_b, pw_w_b, pool_w, pool_scale,
                                                   conv_dw_f, ln_g, ln_b)
    gwin_blk, g_w_out, g_pw_w, vec, dw_sum, poolw_sum = _bwd2a(x, dproj, norm_g, gwout, gpww, loss8, gfg8, gvec, gdw8, gpoolw)
    grad_x, g_w_in, g_norm_g = _bwd2b(x, dh, dproj, norm_g, w_in_b, gwin_blk)

    me = 4 * lax.axis_index("x") + 2 * lax.axis_index("y") + lax.axis_index("c")
    n_g = len(POOL_WINDOWS)
    grads = dict(
        norm_g=g_norm_g[0], w_in=g_w_in, pool_w=poolw_sum.reshape(n_g, POOL_GC, POOL_GC),
        pool_b=vec[R_POOL, :W_MIX].reshape(n_g, POOL_GC), pool_scale=vec[R_POOL, W_MIX:],
        conv_dw=lax.dynamic_slice(dw_sum, (0, me * DW_BLK), (CONV_WIDTH, DW_BLK)), conv_b=vec[R_CONV, :W_MIX],
        ln_g=vec[R_CONV, W_MIX:], ln_b=vec[R_LN, :W_MIX], pw_w=g_pw_w, pw_b=vec[R_LN, W_MIX:], w_out=g_w_out,
        final_g=vec[R_FINAL_G])
    weights = dict(norm_g=norm_g, w_in=w_in, pool_w=pool_w, pool_b=pool_b, pool_scale=pool_scale, conv_dw=conv_dw, conv_b=conv_b,
                   ln_g=ln_g, ln_b=ln_b, pw_w=pw_w, pw_b=pw_b, w_out=w_out, final_g=final_g)
    m_in = dict(norm_g=m_norm_g, w_in=m_w_in, pool_w=m_pool_w, pool_b=m_pool_b, pool_scale=m_pool_scale, conv_dw=m_conv_dw,
                conv_b=m_conv_b, ln_g=m_ln_g, ln_b=m_ln_b, pw_w=m_pw_w, pw_b=m_pw_b, w_out=m_w_out, final_g=m_final_g)
    v_in = dict(norm_g=v_norm_g, w_in=v_w_in, pool_w=v_pool_w, pool_b=v_pool_b, pool_scale=v_pool_scale, conv_dw=v_conv_dw,
                conv_b=v_conv_b, ln_g=v_ln_g, ln_b=v_ln_b, pw_w=v_pw_w, pw_b=v_pw_b, w_out=v_w_out, final_g=v_final_g)
    names = list(weights)
    delta, new_m, new_v, grad_x = _adamw([weights[n] for n in names], [grads[n] for n in names], [m_in[n] for n in names],
                                         [v_in[n] for n in names], grad_x)
    return (vec[R_LOSS, 0], grad_x, *[grads[n] for n in names], *delta, *new_m, *new_v)
```

```python
import functools

import jax
import jax.numpy as jnp
from jax import lax
from jax.experimental import pallas as pl
from jax.experimental.pallas import tpu as pltpu

F32 = jnp.float32
BF16 = jnp.bfloat16

D_MODEL = 1024
W_MIX = 512
D_IN = 5 * W_MIX
POOL_WINDOWS = (2, 4, 8, 16)
POOL_GC = 128
CONV_WIDTH = 31
RMS_EPS = 1e-6
LN_EPS = 1e-5
N_DEV = 8

LANES = 128
SUBLANES = 8
TILE_M = 512
POOL_HALO = 16
CONV_HALO = 32
CONV_ROWS = 128
VMEM_LIMIT = 56 * 1024 * 1024

VMEM = pl.BlockSpec(memory_space=pltpu.VMEM)


def _silu(v):
    return v * jax.nn.sigmoid(v)


def _dsilu(v):
    s = jax.nn.sigmoid(v)
    return s * (1.0 + v * (1.0 - s))


def _colsum8(v):
    m, n = v.shape
    return jnp.sum(v.reshape(m // SUBLANES, SUBLANES, n), axis=0)


def _row_index(shape, t0):
    return lax.broadcasted_iota(jnp.int32, shape, 0) + t0


def _conv_taps():
    taps = {s: [] for s in range(SUBLANES)}
    for k in range(CONV_WIDTH):
        e = k + CONV_HALO - (CONV_WIDTH - 1)
        taps[e % SUBLANES].append((e // SUBLANES, k))
    return taps


def _fwd_body(x_ref, tgt_ref, ng_ref, win_ref, poolw_ref, poolb_ref, pools_ref, dw_ref, cb_ref, lg_ref, lb_ref,
              pww_ref, pwb_ref, wout_ref, fg_ref,
              proj_ref, h1_ref, z_ref, ob_ref, dpool_ref, dh_ref, loss_ref, gfg_ref,
              uext, hext, y_s):
    b = pl.program_id(0)
    i = pl.program_id(1)
    tm = TILE_M

    @pl.when(i == 0)
    def _():
        uext[0:POOL_HALO, :] = jnp.zeros((POOL_HALO, W_MIX), F32)
        hext[0:CONV_HALO, :] = jnp.zeros((CONV_HALO, W_MIX), F32)
        hext[CONV_HALO + tm:, :] = jnp.zeros((SUBLANES, W_MIX), F32)

    @pl.when((b == 0) & (i == 0))
    def _():
        loss_ref[...] = jnp.zeros_like(loss_ref)
        gfg_ref[...] = jnp.zeros_like(gfg_ref)

    x = x_ref[0]
    r1 = lax.rsqrt(jnp.mean(x * x, axis=-1, keepdims=True) + RMS_EPS)
    hn = (x * r1 * ng_ref[...]).astype(BF16)
    proj_ref[0] = jnp.dot(hn, win_ref[...], preferred_element_type=F32)

    uext[POOL_HALO:, :] = proj_ref[0, :, 0:W_MIX]
    t = _row_index((tm, POOL_GC), i * tm)
    for g, w in enumerate(POOL_WINDOWS):
        lanes = slice(g * POOL_GC, (g + 1) * POOL_GC)
        e = uext[:, lanes]
        s = e
        sh = 1
        while sh < w:
            s = s + pltpu.roll(s, sh, 0)
            sh *= 2
        inv_cnt = 1.0 / jnp.minimum(t + 1, w).astype(F32)
        d = (s[POOL_HALO:] * inv_cnt - e[POOL_HALO:]).astype(BF16)
        dpool_ref[0, :, lanes] = d
        z = jnp.dot(d, poolw_ref[g].astype(BF16), preferred_element_type=F32) + poolb_ref[:, lanes]
        z_ref[0, :, lanes] = z
        a_gate = proj_ref[0, :, W_MIX + g * POOL_GC:W_MIX + (g + 1) * POOL_GC]
        y_s[:, lanes] = (z * pools_ref[:, lanes] * _silu(a_gate)).astype(BF16)

    hext[CONV_HALO:CONV_HALO + tm, :] = proj_ref[0, :, 2 * W_MIX:3 * W_MIX] * jax.nn.sigmoid(proj_ref[0, :, 3 * W_MIX:4 * W_MIX])
    taps = _conv_taps()
    for c0 in range(0, tm, CONV_ROWS):
        for l0 in range(0, W_MIX, LANES):
            lanes = slice(l0, l0 + LANES)
            acc = jnp.zeros((CONV_ROWS, LANES), F32) + cb_ref[:, lanes]
            for s in range(SUBLANES):
                part = jnp.zeros((CONV_ROWS + SUBLANES, LANES), F32)
                for q, k in taps[s]:
                    r0 = c0 + SUBLANES * q
                    part = part + hext[r0:r0 + CONV_ROWS + SUBLANES, lanes] * dw_ref[k:k + 1, lanes]
                acc = acc + part[s:s + CONV_ROWS]
            h1_ref[0, c0:c0 + CONV_ROWS, lanes] = acc
    h1 = h1_ref[0]
    mu = jnp.mean(h1, axis=-1, keepdims=True)
    xc = h1 - mu
    rstd = lax.rsqrt(jnp.mean(xc * xc, axis=-1, keepdims=True) + LN_EPS)
    h3 = _silu(xc * rstd * lg_ref[...] + lb_ref[...])
    ob = jnp.dot(h3.astype(BF16), pww_ref[...], preferred_element_type=F32) + pwb_ref[...]
    ob_ref[0] = ob
    y_s[:, W_MIX:] = (ob * _silu(proj_ref[0, :, 4 * W_MIX:5 * W_MIX])).astype(BF16)

    h = x + jnp.dot(y_s[...], wout_ref[...], preferred_element_type=F32)
    r2 = lax.rsqrt(jnp.mean(h * h, axis=-1, keepdims=True) + RMS_EPS)
    hr = h * r2
    err = hr * fg_ref[...] - tgt_ref[0]
    loss_ref[...] += jnp.sum(err * err) * (0.5 / D_MODEL)
    dout = err * (1.0 / D_MODEL)
    gfg_ref[...] += _colsum8(dout * hr)
    gy = dout * fg_ref[...]
    dh_ref[0] = r2 * (gy - hr * jnp.mean(gy * hr, axis=-1, keepdims=True))

    uext[0:POOL_HALO, :] = uext[tm:tm + POOL_HALO, :]
    hext[0:CONV_HALO, :] = hext[tm:tm + CONV_HALO, :]


def _row(v):
    return v.reshape(1, -1)


def _fwd(x, tgt, norm_g, w_in_b, pool_w, pool_b, pool_scale, conv_dw_f, conv_b, ln_g, ln_b, pw_w_b, pw_b, w_out_b, final_g):
    bl, s, _ = x.shape
    tm = TILE_M
    tok = lambda n: pl.BlockSpec((1, tm, n), lambda b, i: (b, i, 0))
    acc = lambda n: pl.BlockSpec((SUBLANES, n), lambda b, i: (0, 0))
    act = lambda n: jax.ShapeDtypeStruct((bl, s, n), F32)
    return pl.pallas_call(
        functools.partial(_fwd_body),
        name="fwd",
        grid=(bl, s // tm),
        in_specs=[tok(D_MODEL), tok(D_MODEL)] + [VMEM] * 13,
        out_specs=[tok(D_IN), tok(W_MIX), tok(W_MIX), tok(W_MIX), tok(W_MIX), tok(D_MODEL), acc(LANES), acc(D_MODEL)],
        out_shape=[act(D_IN), act(W_MIX), act(W_MIX), act(W_MIX), jax.ShapeDtypeStruct((bl, s, W_MIX), BF16), act(D_MODEL),
                   jax.ShapeDtypeStruct((SUBLANES, LANES), F32), jax.ShapeDtypeStruct((SUBLANES, D_MODEL), F32)],
        scratch_shapes=[pltpu.VMEM((POOL_HALO + tm, W_MIX), F32), pltpu.VMEM((CONV_HALO + tm + SUBLANES, W_MIX), F32),
                        pltpu.VMEM((tm, D_MODEL), BF16)],
        compiler_params=pltpu.CompilerParams(dimension_semantics=("arbitrary", "arbitrary"), vmem_limit_bytes=VMEM_LIMIT),
    )(x, tgt, _row(norm_g), w_in_b, pool_w, pool_b.reshape(1, W_MIX), _row(pool_scale), conv_dw_f, _row(conv_b),
      _row(ln_g), _row(ln_b), pw_w_b, _row(pw_b), w_out_b, _row(final_g))


NT_DIMS = (((1,), (1,)), ((), ()))
TN_DIMS = (((0,), (0,)), ((), ()))
V_POOL_B, V_POOL_SCALE, V_CONV_B, V_LN_G, V_LN_B, V_PW_B = range(6)


def _bwd_taps():
    taps = {s: [] for s in range(SUBLANES)}
    for k in range(CONV_WIDTH):
        e = CONV_WIDTH - 1 - k
        taps[e % SUBLANES].append((e // SUBLANES, k))
    return taps


def _bwd1_body(n_tiles, dh_ref, proj_ref, h1_ref, z_ref, ob_ref, dpool_ref, wout_ref, pww_ref, poolw_ref, pools_ref, dw_ref,
               lg_ref, lb_ref,
               dproj_ref, gwout_ref, gpww_ref, gpoolw_ref, gvec_ref, gdw_ref,
               vext, gext, gsh, y_s):
    b = pl.program_id(0)
    i = pl.program_id(1)
    tm = TILE_M
    t0 = (n_tiles - 1 - i) * tm

    @pl.when(i == 0)
    def _():
        vext[tm:, :] = jnp.zeros((POOL_HALO, W_MIX), F32)
        gext[tm:, :] = jnp.zeros((CONV_HALO, W_MIX), F32)

    @pl.when((b == 0) & (i == 0))
    def _():
        gwout_ref[...] = jnp.zeros_like(gwout_ref)
        gpww_ref[...] = jnp.zeros_like(gpww_ref)
        gpoolw_ref[...] = jnp.zeros_like(gpoolw_ref)
        gvec_ref[...] = jnp.zeros_like(gvec_ref)
        gdw_ref[...] = jnp.zeros_like(gdw_ref)

    dhb = dh_ref[0].astype(BF16)
    dy_a = lax.dot_general(dhb, wout_ref[0:W_MIX, :], NT_DIMS, preferred_element_type=F32)
    dy_b = lax.dot_general(dhb, wout_ref[W_MIX:, :], NT_DIMS, preferred_element_type=F32)

    a_gate = proj_ref[0, :, W_MIX:2 * W_MIX]
    sig_a = jax.nn.sigmoid(a_gate)
    silu_a = a_gate * sig_a
    z = z_ref[0]
    out_a = z * pools_ref[...]
    y_s[:, 0:W_MIX] = (out_a * silu_a).astype(BF16)
    d_out_a = dy_a * silu_a
    d_ag = dy_a * out_a * (sig_a * (1.0 + a_gate * (1.0 - sig_a)))
    dproj_ref[0, :, W_MIX:2 * W_MIX] = d_ag.astype(BF16)
    gvec_ref[V_POOL_SCALE] += _colsum8(d_out_a * z)
    dz = d_out_a * pools_ref[...]
    gvec_ref[V_POOL_B] += _colsum8(dz)
    t = _row_index((tm, POOL_GC), t0)
    n_ext = tm + POOL_HALO
    for g, w in enumerate(POOL_WINDOWS):
        lanes = slice(g * POOL_GC, (g + 1) * POOL_GC)
        dzg = dz[:, lanes].astype(BF16)
        gpoolw_ref[g] += lax.dot_general(dpool_ref[0, :, lanes], dzg, TN_DIMS, preferred_element_type=F32)
        dd = lax.dot_general(dzg, poolw_ref[g].astype(BF16), NT_DIMS, preferred_element_type=F32)
        vext[0:tm, lanes] = dd * (1.0 / jnp.minimum(t + 1, w).astype(F32))
        s = vext[:, lanes]
        sh = 1
        while sh < w:
            s = s + pltpu.roll(s, n_ext - sh, 0)
            sh *= 2
        dproj_ref[0, :, lanes] = (s[0:tm] - dd).astype(BF16)
    vext[tm:, :] = vext[0:POOL_HALO, :]

    b_gate = proj_ref[0, :, 4 * W_MIX:5 * W_MIX]
    sig_b = jax.nn.sigmoid(b_gate)
    silu_b = b_gate * sig_b
    ob = ob_ref[0]
    y_s[:, W_MIX:] = (ob * silu_b).astype(BF16)
    d_ob = dy_b * silu_b
    dproj_ref[0, :, 4 * W_MIX:5 * W_MIX] = (dy_b * ob * (sig_b * (1.0 + b_gate * (1.0 - sig_b)))).astype(BF16)
    gvec_ref[V_PW_B] += _colsum8(d_ob)
    gwout_ref[...] += lax.dot_general(y_s[...], dhb, TN_DIMS, preferred_element_type=F32)

    h1 = h1_ref[0]
    mu = jnp.mean(h1, axis=-1, keepdims=True)
    xc = h1 - mu
    rstd = lax.rsqrt(jnp.mean(xc * xc, axis=-1, keepdims=True) + LN_EPS)
    xhat = xc * rstd
    h2 = xhat * lg_ref[...] + lb_ref[...]
    sig2 = jax.nn.sigmoid(h2)
    d_obb = d_ob.astype(BF16)
    gpww_ref[...] += lax.dot_general((h2 * sig2).astype(BF16), d_obb, TN_DIMS, preferred_element_type=F32)
    dh2 = lax.dot_general(d_obb, pww_ref[...], NT_DIMS, preferred_element_type=F32) * (sig2 * (1.0 + h2 * (1.0 - sig2)))
    gvec_ref[V_LN_G] += _colsum8(dh2 * xhat)
    gvec_ref[V_LN_B] += _colsum8(dh2)
    dxh = dh2 * lg_ref[...]
    dh1 = rstd * (dxh - jnp.mean(dxh, axis=-1, keepdims=True) - xhat * jnp.mean(dxh * xhat, axis=-1, keepdims=True))
    gvec_ref[V_CONV_B] += _colsum8(dh1)
    gext[0:tm, :] = dh1

    taps = _bwd_taps()
    win = CONV_ROWS + CONV_HALO - SUBLANES
    for c0 in range(0, tm, CONV_ROWS):
        for l0 in range(0, W_MIX, LANES):
            lanes = slice(l0, l0 + LANES)
            b_val = proj_ref[0, c0:c0 + CONV_ROWS, 2 * W_MIX + l0:2 * W_MIX + l0 + LANES]
            sg = jax.nn.sigmoid(proj_ref[0, c0:c0 + CONV_ROWS, 3 * W_MIX + l0:3 * W_MIX + l0 + LANES])
            h0c = b_val * sg
            acc = jnp.zeros((CONV_ROWS, LANES), F32)
            for s in range(SUBLANES):
                gsh[...] = gext[c0 + s:c0 + s + win, lanes]
                for q, k in taps[s]:
                    gq = gsh[SUBLANES * q:SUBLANES * q + CONV_ROWS, :]
                    acc = acc + gq * dw_ref[k:k + 1, lanes]
                    gdw_ref[k, :, lanes] += _colsum8(h0c * gq)
            d_bval = acc * sg
            dproj_ref[0, c0:c0 + CONV_ROWS, 2 * W_MIX + l0:2 * W_MIX + l0 + LANES] = d_bval.astype(BF16)
            dproj_ref[0, c0:c0 + CONV_ROWS, 3 * W_MIX + l0:3 * W_MIX + l0 + LANES] = (
                d_bval * b_val * (1.0 - sg)).astype(BF16)
    gext[tm:, :] = gext[0:CONV_HALO, :]


def _bwd1(dh, proj, h1, z, ob, dpool, w_out_b, pw_w_b, pool_w, pool_scale, conv_dw_f, ln_g, ln_b):
    bl, s, _ = dh.shape
    tm = TILE_M
    nt = s // tm
    tok = lambda n: pl.BlockSpec((1, tm, n), lambda b, i: (b, nt - 1 - i, 0))
    res = lambda shape: pl.BlockSpec(shape, lambda b, i: (0,) * len(shape))
    out_shapes = [(D_MODEL, D_MODEL), (W_MIX, W_MIX), (len(POOL_WINDOWS), POOL_GC, POOL_GC), (6, SUBLANES, W_MIX),
                  (CONV_WIDTH, SUBLANES, W_MIX)]
    return pl.pallas_call(
        functools.partial(_bwd1_body, nt),
        name="bwd1",
        grid=(bl, nt),
        in_specs=[tok(D_MODEL), tok(D_IN), tok(W_MIX), tok(W_MIX), tok(W_MIX), tok(W_MIX)] + [VMEM] * 7,
        out_specs=[tok(D_IN)] + [res(sh) for sh in out_shapes],
        out_shape=[jax.ShapeDtypeStruct((bl, s, D_IN), BF16)] + [jax.ShapeDtypeStruct(sh, F32) for sh in out_shapes],
        scratch_shapes=[pltpu.VMEM((tm + POOL_HALO, W_MIX), F32), pltpu.VMEM((tm + CONV_HALO, W_MIX), F32),
                        pltpu.VMEM((CONV_ROWS + CONV_HALO - SUBLANES, LANES), F32), pltpu.VMEM((tm, D_MODEL), BF16)],
        compiler_params=pltpu.CompilerParams(dimension_semantics=("arbitrary", "arbitrary"), vmem_limit_bytes=VMEM_LIMIT),
    )(dh, proj, h1, z, ob, dpool, w_out_b, pw_w_b, pool_w, _row(pool_scale), conv_dw_f, _row(ln_g), _row(ln_b))


MESH = pl.DeviceIdType.MESH
W_IN_BLK = D_IN // N_DEV
W_OUT_BLK = D_MODEL // N_DEV
PW_BLK = W_MIX // N_DEV
DW_BLK = W_MIX // N_DEV
N_PEER = N_DEV - 1


def _place():
    x, y, c = lax.axis_index("x"), lax.axis_index("y"), lax.axis_index("c")
    other_chips = [(1 - x, y), (x, 1 - y), (1 - x, 1 - y)]
    return x, y, c, other_chips


def _remote(srcs, dsts, send_sems, recv_sems, k, to):
    return [pltpu.make_async_remote_copy(src_ref=s, dst_ref=d, send_sem=send_sems.at[a, k], recv_sem=recv_sems.at[a, k],
                                         device_id=to, device_id_type=MESH)
            for a, (s, d) in enumerate(zip(srcs, dsts))]


def _all_gather(slots, send_sems, recv_sems):
    x, y, c, other_chips = _place()
    me, sibling = (x, y, c), (x, y, 1 - c)

    def copy(k, block, to):
        refs = slots(*block)
        return _remote(refs, refs, send_sems, recv_sems, k, to)

    first = copy(0, me, sibling)
    for j, chip in enumerate(other_chips):
        first += copy(1 + j, me, (*chip, c))
    for cp in first:
        cp.start()
    passed = []
    for j, chip in enumerate(other_chips):
        for cp in copy(1 + j, (*chip, c), me):
            cp.wait_recv()
        fwd = copy(4 + j, (*chip, c), sibling)
        for cp in fwd:
            cp.start()
        passed += fwd
    for cp in copy(0, sibling, me):
        cp.wait_recv()
    for j, chip in enumerate(other_chips):
        for cp in copy(4 + j, (*chip, 1 - c), me):
            cp.wait_recv()
    for cp in first + passed:
        cp.wait_send()


def _gather_body(win_ref, wout_ref, pww_ref, dw_ref, winf_ref, woutf_ref, pwwf_ref, dwf_ref,
                 win_blk, dw_blk, send_sems, recv_sems):
    def slots(px, py, pc):
        k = 4 * px + 2 * py + pc
        return (win_blk.at[k], woutf_ref.at[pl.ds(pl.multiple_of(k * W_OUT_BLK, W_OUT_BLK), W_OUT_BLK), :],
                pwwf_ref.at[pl.ds(pl.multiple_of(k * PW_BLK, PW_BLK), PW_BLK), :], dw_blk.at[k])

    x, y, c, _ = _place()
    mine = slots(x, y, c)
    mine[0][...] = win_ref[...].astype(BF16)
    mine[1][...] = wout_ref[...].astype(BF16)
    mine[2][...] = pww_ref[...].astype(BF16)
    mine[3][...] = dw_ref[...]
    _all_gather(slots, send_sems, recv_sems)
    for k in range(N_DEV):
        winf_ref[:, k * W_IN_BLK:(k + 1) * W_IN_BLK] = win_blk[k]
        dwf_ref[:, k * DW_BLK:(k + 1) * DW_BLK] = dw_blk[k]


def _gather_weights(w_in, w_out, pw_w, conv_dw):
    return pl.pallas_call(
        functools.partial(_gather_body),
        name="gather",
        in_specs=[VMEM] * 4,
        out_specs=[VMEM] * 4,
        out_shape=[jax.ShapeDtypeStruct((D_MODEL, D_IN), BF16), jax.ShapeDtypeStruct((D_MODEL, D_MODEL), BF16),
                   jax.ShapeDtypeStruct((W_MIX, W_MIX), BF16), jax.ShapeDtypeStruct((CONV_WIDTH, W_MIX), F32)],
        scratch_shapes=[pltpu.VMEM((N_DEV, D_MODEL, W_IN_BLK), BF16), pltpu.VMEM((N_DEV, CONV_WIDTH, DW_BLK), F32),
                        pltpu.SemaphoreType.DMA((4, N_PEER)), pltpu.SemaphoreType.DMA((4, N_PEER))],
        compiler_params=pltpu.CompilerParams(vmem_limit_bytes=VMEM_LIMIT),
    )(w_in, w_out, pw_w, conv_dw)


R_LOSS, R_FINAL_G, R_POOL, R_CONV, R_LN = range(5)
VEC_ROWS = 16
DW_ROWS = 32


def _ag_start(slots, send_sems, recv_sems):
    x, y, c, other_chips = _place()
    mine = slots(x, y, c)
    for k, to in enumerate([(x, y, 1 - c)] + [(*chip, c) for chip in other_chips]):
        for cp in _remote(mine, mine, send_sems, recv_sems, k, to):
            cp.start()


def _ag_forward(slots, send_sems, recv_sems):
    x, y, c, other_chips = _place()
    for j, chip in enumerate(other_chips):
        refs = slots(*chip, c)
        for cp in _remote(refs, refs, send_sems, recv_sems, 1 + j, (x, y, 1 - c)):
            cp.wait_recv()
        for cp in _remote(refs, refs, send_sems, recv_sems, 4 + j, (x, y, 1 - c)):
            cp.start()


def _ag_finish(slots, send_sems, recv_sems):
    x, y, c, other_chips = _place()
    sibling = (x, y, 1 - c)
    for k, block in enumerate([(x, y, 1 - c)] + [None] * 3 + [(*chip, 1 - c) for chip in other_chips]):
        if block is not None:
            refs = slots(*block)
            for cp in _remote(refs, refs, send_sems, recv_sems, k, sibling):
                cp.wait_recv()
    mine = slots(x, y, c)
    for k, block in enumerate([mine] * 4 + [slots(*chip, c) for chip in other_chips]):
        for cp in _remote(block, block, send_sems, recv_sems, k, sibling):
            cp.wait_send()


def _rs_send_a(to_sibling, landed, send_a, recv_a):
    x, y, c, _ = _place()
    for chip in range(4):
        for cp in _remote(to_sibling(2 * chip + (1 - c)), landed(chip), send_a, recv_a, chip, (x, y, 1 - c)):
            cp.start()


def _rs_chip_sum(mine, landed, send_a, recv_a, chip):
    x, y, c, _ = _place()
    for cp in _remote(landed(chip), landed(chip), send_a, recv_a, chip, (x, y, 1 - c)):
        cp.wait_recv()
    return [m[...] + r[...] for m, r in zip(mine(chip), landed(chip))]


def _rs_send_b(mine, landed, staged, arrived, send_a, recv_a, send_b, recv_b):
    x, y, c, other_chips = _place()
    for j, (px, py) in enumerate(other_chips):
        for o, v in zip(staged(j), _rs_chip_sum(mine, landed, send_a, recv_a, 2 * px + py)):
            o[...] = v.astype(BF16)
        for cp in _remote(staged(j), arrived(j), send_b, recv_b, j, (px, py, c)):
            cp.start()


def _rs_finish(to_sibling, mine, landed, staged, arrived, send_a, recv_a, send_b, recv_b):
    x, y, c, _ = _place()
    sibling = (x, y, 1 - c)
    total = _rs_chip_sum(mine, landed, send_a, recv_a, 2 * x + y)
    for j in range(3):
        for cp in _remote(arrived(j), arrived(j), send_b, recv_b, j, sibling):
            cp.wait_recv()
        total = [t + g[...].astype(F32) for t, g in zip(total, arrived(j))]
    for chip in range(4):
        for cp in _remote(to_sibling(2 * chip + (1 - c)), landed(chip), send_a, recv_a, chip, sibling):
            cp.wait_send()
    for j in range(3):
        for cp in _remote(staged(j), arrived(j), send_b, recv_b, j, sibling):
            cp.wait_send()
    return total


EXCHANGE_STEP = 2


def _bwd2a_body(n_steps, x_ref, dproj_ref, ng_ref, gwout_ref, gpww_ref, loss_ref, gfg_ref, gvec_ref, gdw_ref, gpoolw_ref,
                stwin_ref, ownwin_ref, owout_ref, opww_ref, ovec_ref, odw_ref, opoolw_ref,
                acc, slab4, a_win, a_wout, a_pww, sb_wout, sb_pww, rb_wout, rb_pww, all_vec, all_dw, all_poolw,
                send_a, recv_a, send_b, recv_b, send_g, recv_g, send_w, recv_w):
    step = pl.program_id(0) * pl.num_programs(1) + pl.program_id(1)
    x, y, c, other_chips = _place()

    def rows(ref, k, n):
        return ref.at[pl.ds(pl.multiple_of(k * n, n), n), :]

    to_sibling = lambda k: (rows(gwout_ref, k, W_OUT_BLK), rows(gpww_ref, k, PW_BLK))
    mine = lambda chip: to_sibling(2 * chip + c)
    landed = lambda chip: (a_wout.at[chip], a_pww.at[chip])
    staged = lambda j: (sb_wout.at[j], sb_pww.at[j])
    arrived = lambda j: (rb_wout.at[j], rb_pww.at[j])

    def slots(px, py, pc):
        k = 4 * px + 2 * py + pc
        return (all_vec.at[k], all_dw.at[k], all_poolw.at[k])

    @pl.when(step == 0)
    def _():
        acc[...] = jnp.zeros_like(acc)
        sum8 = lambda v: jnp.sum(v, axis=0, keepdims=True)
        vec, dwp, pwp = slots(x, y, c)
        vec[...] = jnp.zeros((VEC_ROWS, D_MODEL), F32)
        vec[R_LOSS:R_LOSS + 1, :] = jnp.broadcast_to(loss_ref[0:1, 0:1], (1, D_MODEL))
        vec[R_FINAL_G:R_FINAL_G + 1, :] = sum8(gfg_ref[...])
        for row, (lo, hi) in ((R_POOL, (V_POOL_B, V_POOL_SCALE)), (R_CONV, (V_CONV_B, V_LN_G)), (R_LN, (V_LN_B, V_PW_B))):
            vec[row:row + 1, 0:W_MIX] = sum8(gvec_ref[lo])
            vec[row:row + 1, W_MIX:] = sum8(gvec_ref[hi])
        dwp[0:CONV_WIDTH, :] = jnp.sum(gdw_ref[...], axis=1)
        dwp[CONV_WIDTH:, :] = jnp.zeros((DW_ROWS - CONV_WIDTH, W_MIX), F32)
        pwp[...] = gpoolw_ref[...].reshape(len(POOL_WINDOWS) * POOL_GC, POOL_GC)
        _rs_send_a(to_sibling, landed, send_a, recv_a)
        _ag_start(slots, send_g, recv_g)

    x_t = x_ref[0]
    r1 = lax.rsqrt(jnp.mean(x_t * x_t, axis=-1, keepdims=True) + RMS_EPS)
    hn = (x_t * r1 * ng_ref[...]).astype(BF16)
    acc[...] += lax.dot_general(hn, dproj_ref[0], TN_DIMS, preferred_element_type=F32)

    @pl.when(step == EXCHANGE_STEP)
    def _():
        _rs_send_b(mine, landed, staged, arrived, send_a, recv_a, send_b, recv_b)
        _ag_forward(slots, send_g, recv_g)

    @pl.when(step == n_steps - 1)
    def _():
        slab = lambda k: acc[:, k * W_IN_BLK:(k + 1) * W_IN_BLK]
        w_copy = lambda chip: _remote((slab4.at[chip],), (a_win.at[chip],), send_w, recv_w, chip, (x, y, 1 - c))[0]
        for cc in range(2):
            @pl.when(c == cc)
            def _():
                for chip in range(4):
                    slab4[chip] = slab(2 * chip + (1 - cc))
        for chip in range(4):
            w_copy(chip).start()

        owout_ref[...], opww_ref[...] = _rs_finish(to_sibling, mine, landed, staged, arrived, send_a, recv_a, send_b, recv_b)
        _ag_finish(slots, send_g, recv_g)
        for o, a in ((ovec_ref, all_vec), (odw_ref, all_dw), (opoolw_ref, all_poolw)):
            s = a[0]
            for k in range(1, N_DEV):
                s = s + a[k]
            o[...] = s

        for chip in range(4):
            w_copy(chip).wait()
        for cc in range(2):
            @pl.when(c == cc)
            def _():
                for chip in range(4):
                    slab4[chip] = slab(2 * chip + cc) + a_win[chip]
        for j, (px, py) in enumerate(other_chips):
            stwin_ref[j] = slab4[2 * px + py].astype(BF16)
        ownwin_ref[...] = slab4[2 * x + y]


def _bwd2a(x, dproj, norm_g, gwout, gpww, loss8, gfg8, gvec, gdw8, gpoolw):
    bl, s, _ = x.shape
    tm = TILE_M
    nt = s // tm
    tok = lambda n: pl.BlockSpec((1, tm, n), lambda b, i: (b, i, 0))
    n_pw = len(POOL_WINDOWS) * POOL_GC
    win = (D_MODEL, W_IN_BLK)
    blk = ((W_OUT_BLK, D_MODEL), (PW_BLK, W_MIX))
    rep = ((VEC_ROWS, D_MODEL), (DW_ROWS, W_MIX), (n_pw, POOL_GC))
    dma = pltpu.SemaphoreType.DMA
    return pl.pallas_call(
        functools.partial(_bwd2a_body, bl * nt),
        name="bwd2a",
        grid=(bl, nt),
        in_specs=[tok(D_MODEL), tok(D_IN)] + [VMEM] * 8,
        out_specs=[VMEM] * 7,
        out_shape=[jax.ShapeDtypeStruct((3,) + win, BF16), jax.ShapeDtypeStruct(win, F32)]
        + [jax.ShapeDtypeStruct(sh, F32) for sh in blk + rep],
        scratch_shapes=[pltpu.VMEM((D_MODEL, D_IN), F32), pltpu.VMEM((4,) + win, F32), pltpu.VMEM((4,) + win, F32)]
        + [pltpu.VMEM((4,) + sh, F32) for sh in blk]
        + [pltpu.VMEM((3,) + sh, BF16) for sh in blk] * 2
        + [pltpu.VMEM((N_DEV,) + sh, F32) for sh in rep]
        + [dma((2, 4)), dma((2, 4)), dma((2, 3)), dma((2, 3)), dma((3, N_PEER)), dma((3, N_PEER)), dma((1, 4)), dma((1, 4))],
        compiler_params=pltpu.CompilerParams(dimension_semantics=("arbitrary", "arbitrary"), vmem_limit_bytes=VMEM_LIMIT),
    )(x, dproj, _row(norm_g), gwout, gpww, loss8, gfg8, gvec, gdw8, gpoolw)


def _bwd2b_body(n_steps, x_ref, dh_ref, dproj_ref, ng_ref, win_ref, stwin_hbm, ownwin_ref,
                gx_ref, owin_ref, ong_ref,
                gng, rb_win, all_ng, send_b, recv_b, send_n, recv_n):
    step = pl.program_id(0) * pl.num_programs(1) + pl.program_id(1)
    x, y, c, other_chips = _place()
    sibling = (x, y, 1 - c)
    b_copy = lambda j: pltpu.make_async_remote_copy(src_ref=stwin_hbm.at[j], dst_ref=rb_win.at[j], send_sem=send_b.at[j],
                                                    recv_sem=recv_b.at[j], device_id=(*other_chips[j], c), device_id_type=MESH)

    @pl.when(step == 0)
    def _():
        gng[...] = jnp.zeros_like(gng)
        for j in range(3):
            b_copy(j).start()

    x_t = x_ref[0]
    r1 = lax.rsqrt(jnp.mean(x_t * x_t, axis=-1, keepdims=True) + RMS_EPS)
    xr = x_t * r1
    dhn = lax.dot_general(dproj_ref[0], win_ref[...], NT_DIMS, preferred_element_type=F32)
    gng[...] += _colsum8(dhn * xr)
    gy = dhn * ng_ref[...]
    gx_ref[0] = dh_ref[0] + r1 * (gy - xr * jnp.mean(gy * xr, axis=-1, keepdims=True))

    k_me = 4 * x + 2 * y + c
    peers = [sibling] + [(*chip, c) for chip in other_chips] + [(*chip, 1 - c) for chip in other_chips]
    ng_copy = lambda r: pltpu.make_async_remote_copy(src_ref=all_ng.at[k_me], dst_ref=all_ng.at[k_me], send_sem=send_n.at[r],
                                                     recv_sem=recv_n.at[r], device_id=peers[r], device_id_type=MESH)

    @pl.when(step == n_steps - 1)
    def _():
        all_ng[k_me] = gng[...]
        for r in range(N_PEER):
            ng_copy(r).start()
        total = ownwin_ref[...]
        for j in range(3):
            b_copy(j).wait()
            total = total + rb_win[j].astype(F32)
        owin_ref[...] = total
        for r in range(N_PEER):
            ng_copy(r).wait()
        s = all_ng[0]
        for k in range(1, N_DEV):
            s = s + all_ng[k]
        ong_ref[...] = jnp.broadcast_to(jnp.sum(s, axis=0, keepdims=True), ong_ref.shape)


def _bwd2b(x, dh, dproj, norm_g, w_in_b, st_win, own_win):
    bl, s, _ = x.shape
    tm = TILE_M
    nt = s // tm
    tok = lambda n: pl.BlockSpec((1, tm, n), lambda b, i: (b, i, 0))
    blk = (D_MODEL, W_IN_BLK)
    dma = pltpu.SemaphoreType.DMA
    return pl.pallas_call(
        functools.partial(_bwd2b_body, bl * nt),
        name="bwd2b",
        grid=(bl, nt),
        in_specs=[tok(D_MODEL), tok(D_MODEL), tok(D_IN), VMEM, VMEM, pl.BlockSpec(memory_space=pl.ANY), VMEM],
        out_specs=[tok(D_MODEL), VMEM, VMEM],
        out_shape=[jax.ShapeDtypeStruct((bl, s, D_MODEL), F32), jax.ShapeDtypeStruct(blk, F32),
                   jax.ShapeDtypeStruct((SUBLANES, D_MODEL), F32)],
        scratch_shapes=[pltpu.VMEM((SUBLANES, D_MODEL), F32), pltpu.VMEM((3,) + blk, BF16),
                        pltpu.VMEM((N_DEV, SUBLANES, D_MODEL), F32), dma((3,)), dma((3,)), dma((N_PEER,)), dma((N_PEER,))],
        compiler_params=pltpu.CompilerParams(dimension_semantics=("arbitrary", "arbitrary"), vmem_limit_bytes=VMEM_LIMIT),
    )(x, dh, dproj, _row(norm_g), w_in_b, st_win, own_win)


ADAM_LR = 0.001
ADAM_B1 = 0.9
ADAM_B2 = 0.999
ADAM_EPS = 1e-08
ADAM_WD = 0.01
ADAM_STEP = 10


def _adamw_body(n, *refs):
    ws, gs, ms, vs = (refs[k * n:(k + 1) * n] for k in range(4))
    ds, m2s, v2s = (refs[(4 + k) * n:(5 + k) * n] for k in range(3))
    for w_ref, g_ref, m_ref, v_ref, d_ref, m2_ref, v2_ref in zip(ws, gs, ms, vs, ds, m2s, v2s):
        g = g_ref[...]
        m = ADAM_B1 * m_ref[...] + (1.0 - ADAM_B1) * g
        v = ADAM_B2 * v_ref[...] + (1.0 - ADAM_B2) * (g * g)
        m_hat = m / (1.0 - ADAM_B1 ** ADAM_STEP)
        v_hat = v / (1.0 - ADAM_B2 ** ADAM_STEP)
        d_ref[...] = -ADAM_LR * (m_hat / (jnp.sqrt(v_hat) + ADAM_EPS) + ADAM_WD * w_ref[...])
        m2_ref[...] = m
        v2_ref[...] = v


def _adamw(ws, gs, ms, vs):
    n = len(ws)
    shapes = [w.shape for w in ws]
    flat = lambda a: a.reshape(1, -1) if a.ndim == 1 else a
    args = [flat(a) for a in (*ws, *gs, *ms, *vs)]
    outs = pl.pallas_call(
        functools.partial(_adamw_body, n),
        name="adamw",
        in_specs=[VMEM] * (4 * n),
        out_specs=[VMEM] * (3 * n),
        out_shape=[jax.ShapeDtypeStruct(a.shape, F32) for a in args[:n]] * 3,
        compiler_params=pltpu.CompilerParams(vmem_limit_bytes=VMEM_LIMIT),
    )(*args)
    outs = [o.reshape(shapes[k % n]) for k, o in enumerate(outs)]
    return outs[:n], outs[n:2 * n], outs[2 * n:]


def kernel(x, norm_g, w_in, pool_w, pool_b, pool_scale, conv_dw, conv_b, ln_g, ln_b, pw_w, pw_b, w_out, final_g, loss_target, m_norm_g, m_w_in, m_pool_w, m_pool_b, m_pool_scale, m_conv_dw, m_conv_b, m_ln_g, m_ln_b, m_pw_w, m_pw_b, m_w_out, m_final_g, v_norm_g, v_w_in, v_pool_w, v_pool_b, v_pool_scale, v_conv_dw, v_conv_b, v_ln_g, v_ln_b, v_pw_w, v_pw_b, v_w_out, v_final_g):
    w_in_b, w_out_b, pw_w_b, conv_dw_f = _gather_weights(w_in, w_out, pw_w, conv_dw)
    proj, h1, z, ob, dpool, dh, loss8, gfg8 = _fwd(x, loss_target, norm_g, w_in_b, pool_w, pool_b, pool_scale, conv_dw_f,
                                                   conv_b, ln_g, ln_b, pw_w_b, pw_b, w_out_b, final_g)
    dproj, gwout, gpww, gpoolw, gvec, gdw8 = _bwd1(dh, proj, h1, z, ob, dpool, w_out_b, pw_w_b, pool_w, pool_scale,
                                                   conv_dw_f, ln_g, ln_b)
    st_win, own_win, g_w_out, g_pw_w, vec, dw_sum, poolw_sum = _bwd2a(x, dproj, norm_g, gwout, gpww, loss8, gfg8, gvec, gdw8,
                                                                      gpoolw)
    grad_x, g_w_in, g_norm_g = _bwd2b(x, dh, dproj, norm_g, w_in_b, st_win, own_win)

    me = 4 * lax.axis_index("x") + 2 * lax.axis_index("y") + lax.axis_index("c")
    n_g = len(POOL_WINDOWS)
    grads = dict(
        norm_g=g_norm_g[0], w_in=g_w_in, pool_w=poolw_sum.reshape(n_g, POOL_GC, POOL_GC),
        pool_b=vec[R_POOL, :W_MIX].reshape(n_g, POOL_GC), pool_scale=vec[R_POOL, W_MIX:],
        conv_dw=lax.dynamic_slice(dw_sum, (0, me * DW_BLK), (CONV_WIDTH, DW_BLK)), conv_b=vec[R_CONV, :W_MIX],
        ln_g=vec[R_CONV, W_MIX:], ln_b=vec[R_LN, :W_MIX], pw_w=g_pw_w, pw_b=vec[R_LN, W_MIX:], w_out=g_w_out,
        final_g=vec[R_FINAL_G])
    weights = dict(norm_g=norm_g, w_in=w_in, pool_w=pool_w, pool_b=pool_b, pool_scale=pool_scale, conv_dw=conv_dw, conv_b=conv_b,
                   ln_g=ln_g, ln_b=ln_b, pw_w=pw_w, pw_b=pw_b, w_out=w_out, final_g=final_g)
    m_in = dict(norm_g=m_norm_g, w_in=m_w_in, pool_w=m_pool_w, pool_b=m_pool_b, pool_scale=m_pool_scale, conv_dw=m_conv_dw,
                conv_b=m_conv_b, ln_g=m_ln_g, ln_b=m_ln_b, pw_w=m_pw_w, pw_b=m_pw_b, w_out=m_w_out, final_g=m_final_g)
    v_in = dict(norm_g=v_norm_g, w_in=v_w_in, pool_w=v_pool_w, pool_b=v_pool_b, pool_scale=v_pool_scale, conv_dw=v_conv_dw,
                conv_b=v_conv_b, ln_g=v_ln_g, ln_b=v_ln_b, pw_w=v_pw_w, pw_b=v_pw_b, w_out=v_w_out, final_g=v_final_g)
    names = list(weights)
    delta, new_m, new_v = _adamw([weights[n] for n in names], [grads[n] for n in names], [m_in[n] for n in names],
                                 [v_in[n] for n in names])
    return (vec[R_LOSS, 0], grad_x, *[grads[n] for n in names], *delta, *new_m, *new_v)
```

```python
import functools

import jax
import jax.numpy as jnp
from jax import lax
from jax.experimental import pallas as pl
from jax.experimental.pallas import tpu as pltpu

F32 = jnp.float32
BF16 = jnp.bfloat16

D_MODEL = 1024
W_MIX = 512
D_IN = 5 * W_MIX
POOL_WINDOWS = (2, 4, 8, 16)
POOL_GC = 128
CONV_WIDTH = 31
RMS_EPS = 1e-6
LN_EPS = 1e-5
N_DEV = 8

LANES = 128
SUBLANES = 8
TILE_M = 512
POOL_HALO = 16
CONV_HALO = 32
CONV_ROWS = 128
VMEM_LIMIT = 56 * 1024 * 1024

VMEM = pl.BlockSpec(memory_space=pltpu.VMEM)


def _silu(v):
    return v * jax.nn.sigmoid(v)


def _dsilu(v):
    s = jax.nn.sigmoid(v)
    return s * (1.0 + v * (1.0 - s))


def _colsum8(v):
    m, n = v.shape
    return jnp.sum(v.reshape(m // SUBLANES, SUBLANES, n), axis=0)


def _row_index(shape, t0):
    return lax.broadcasted_iota(jnp.int32, shape, 0) + t0


def _conv_taps():
    taps = {s: [] for s in range(SUBLANES)}
    for k in range(CONV_WIDTH):
        e = k + CONV_HALO - (CONV_WIDTH - 1)
        taps[e % SUBLANES].append((e // SUBLANES, k))
    return taps


def _fwd_body(x_ref, tgt_ref, proj_ref, poolw_ref, poolb_ref, pools_ref, dw_ref, cb_ref, lg_ref, lb_ref,
              pww_ref, pwb_ref, wout_ref, fg_ref,
              h1_ref, z_ref, ob_ref, dpool_ref, dh_ref, loss_ref, gfg_ref,
              uext, hext, y_s):
    b = pl.program_id(0)
    i = pl.program_id(1)
    tm = TILE_M

    @pl.when(i == 0)
    def _():
        uext[0:POOL_HALO, :] = jnp.zeros((POOL_HALO, W_MIX), F32)
        hext[0:CONV_HALO, :] = jnp.zeros((CONV_HALO, W_MIX), F32)
        hext[CONV_HALO + tm:, :] = jnp.zeros((SUBLANES, W_MIX), F32)

    @pl.when((b == 0) & (i == 0))
    def _():
        loss_ref[...] = jnp.zeros_like(loss_ref)
        gfg_ref[...] = jnp.zeros_like(gfg_ref)

    uext[POOL_HALO:, :] = proj_ref[0, :, 0:W_MIX]
    t = _row_index((tm, POOL_GC), i * tm)
    for g, w in enumerate(POOL_WINDOWS):
        lanes = slice(g * POOL_GC, (g + 1) * POOL_GC)
        e = uext[:, lanes]
        s = e
        sh = 1
        while sh < w:
            s = s + pltpu.roll(s, sh, 0)
            sh *= 2
        inv_cnt = 1.0 / jnp.minimum(t + 1, w).astype(F32)
        d = (s[POOL_HALO:] * inv_cnt - e[POOL_HALO:]).astype(BF16)
        dpool_ref[0, :, lanes] = d
        z = jnp.dot(d, poolw_ref[g].astype(BF16), preferred_element_type=F32) + poolb_ref[:, lanes]
        z_ref[0, :, lanes] = z
        a_gate = proj_ref[0, :, W_MIX + g * POOL_GC:W_MIX + (g + 1) * POOL_GC]
        y_s[:, lanes] = (z * pools_ref[:, lanes] * _silu(a_gate)).astype(BF16)

    hext[CONV_HALO:CONV_HALO + tm, :] = proj_ref[0, :, 2 * W_MIX:3 * W_MIX] * jax.nn.sigmoid(proj_ref[0, :, 3 * W_MIX:4 * W_MIX])
    taps = _conv_taps()
    for c0 in range(0, tm, CONV_ROWS):
        for l0 in range(0, W_MIX, LANES):
            lanes = slice(l0, l0 + LANES)
            acc = jnp.zeros((CONV_ROWS, LANES), F32) + cb_ref[:, lanes]
            for s in range(SUBLANES):
                part = jnp.zeros((CONV_ROWS + SUBLANES, LANES), F32)
                for q, k in taps[s]:
                    r0 = c0 + SUBLANES * q
                    part = part + hext[r0:r0 + CONV_ROWS + SUBLANES, lanes] * dw_ref[k:k + 1, lanes]
                acc = acc + part[s:s + CONV_ROWS]
            h1_ref[0, c0:c0 + CONV_ROWS, lanes] = acc
    h1 = h1_ref[0]
    mu = jnp.mean(h1, axis=-1, keepdims=True)
    xc = h1 - mu
    rstd = lax.rsqrt(jnp.mean(xc * xc, axis=-1, keepdims=True) + LN_EPS)
    h3 = _silu(xc * rstd * lg_ref[...] + lb_ref[...])
    ob = jnp.dot(h3.astype(BF16), pww_ref[...], preferred_element_type=F32) + pwb_ref[...]
    ob_ref[0] = ob
    y_s[:, W_MIX:] = (ob * _silu(proj_ref[0, :, 4 * W_MIX:5 * W_MIX])).astype(BF16)

    h = x_ref[0] + jnp.dot(y_s[...], wout_ref[...], preferred_element_type=F32)
    r2 = lax.rsqrt(jnp.mean(h * h, axis=-1, keepdims=True) + RMS_EPS)
    hr = h * r2
    err = hr * fg_ref[...] - tgt_ref[0]
    loss_ref[...] += jnp.sum(err * err) * (0.5 / D_MODEL)
    dout = err * (1.0 / D_MODEL)
    gfg_ref[...] += _colsum8(dout * hr)
    gy = dout * fg_ref[...]
    dh_ref[0] = r2 * (gy - hr * jnp.mean(gy * hr, axis=-1, keepdims=True))

    uext[0:POOL_HALO, :] = uext[tm:tm + POOL_HALO, :]
    hext[0:CONV_HALO, :] = hext[tm:tm + CONV_HALO, :]


def _row(v):
    return v.reshape(1, -1)


def _fwd(x, tgt, proj, pool_w, pool_b, pool_scale, conv_dw_f, conv_b, ln_g, ln_b, pw_w_b, pw_b, w_out_b, final_g):
    bl, s, _ = x.shape
    tm = TILE_M
    tok = lambda n: pl.BlockSpec((1, tm, n), lambda b, i: (b, i, 0))
    acc = lambda n: pl.BlockSpec((SUBLANES, n), lambda b, i: (0, 0))
    act = lambda n: jax.ShapeDtypeStruct((bl, s, n), F32)
    return pl.pallas_call(
        functools.partial(_fwd_body),
        name="fwd",
        grid=(bl, s // tm),
        in_specs=[tok(D_MODEL), tok(D_MODEL), tok(D_IN)] + [VMEM] * 11,
        out_specs=[tok(W_MIX), tok(W_MIX), tok(W_MIX), tok(W_MIX), tok(D_MODEL), acc(LANES), acc(D_MODEL)],
        out_shape=[act(W_MIX), act(W_MIX), act(W_MIX), jax.ShapeDtypeStruct((bl, s, W_MIX), BF16), act(D_MODEL),
                   jax.ShapeDtypeStruct((SUBLANES, LANES), F32), jax.ShapeDtypeStruct((SUBLANES, D_MODEL), F32)],
        scratch_shapes=[pltpu.VMEM((POOL_HALO + tm, W_MIX), F32), pltpu.VMEM((CONV_HALO + tm + SUBLANES, W_MIX), F32),
                        pltpu.VMEM((tm, D_MODEL), BF16)],
        compiler_params=pltpu.CompilerParams(dimension_semantics=("arbitrary", "arbitrary"), vmem_limit_bytes=VMEM_LIMIT),
    )(x, tgt, proj, pool_w, pool_b.reshape(1, W_MIX), _row(pool_scale), conv_dw_f, _row(conv_b),
      _row(ln_g), _row(ln_b), pw_w_b, _row(pw_b), w_out_b, _row(final_g))


NT_DIMS = (((1,), (1,)), ((), ()))
TN_DIMS = (((0,), (0,)), ((), ()))
V_POOL_B, V_POOL_SCALE, V_CONV_B, V_LN_G, V_LN_B, V_PW_B = range(6)


def _bwd_taps():
    taps = {s: [] for s in range(SUBLANES)}
    for k in range(CONV_WIDTH):
        e = CONV_WIDTH - 1 - k
        taps[e % SUBLANES].append((e // SUBLANES, k))
    return taps


def _bwd1_body(n_tiles, dh_ref, proj_ref, h1_ref, z_ref, ob_ref, dpool_ref, wout_ref, pww_ref, poolw_ref, pools_ref, dw_ref,
               lg_ref, lb_ref,
               dproj_ref, gwout_ref, gpww_ref, gpoolw_ref, gvec_ref, gdw_ref,
               vext, gext, gsh, y_s):
    b = pl.program_id(0)
    i = pl.program_id(1)
    tm = TILE_M
    t0 = (n_tiles - 1 - i) * tm

    @pl.when(i == 0)
    def _():
        vext[tm:, :] = jnp.zeros((POOL_HALO, W_MIX), F32)
        gext[tm:, :] = jnp.zeros((CONV_HALO, W_MIX), F32)

    @pl.when((b == 0) & (i == 0))
    def _():
        gwout_ref[...] = jnp.zeros_like(gwout_ref)
        gpww_ref[...] = jnp.zeros_like(gpww_ref)
        gpoolw_ref[...] = jnp.zeros_like(gpoolw_ref)
        gvec_ref[...] = jnp.zeros_like(gvec_ref)
        gdw_ref[...] = jnp.zeros_like(gdw_ref)

    dhb = dh_ref[0].astype(BF16)
    dy_a = lax.dot_general(dhb, wout_ref[0:W_MIX, :], NT_DIMS, preferred_element_type=F32)
    dy_b = lax.dot_general(dhb, wout_ref[W_MIX:, :], NT_DIMS, preferred_element_type=F32)

    a_gate = proj_ref[0, :, W_MIX:2 * W_MIX]
    sig_a = jax.nn.sigmoid(a_gate)
    silu_a = a_gate * sig_a
    z = z_ref[0]
    out_a = z * pools_ref[...]
    y_s[:, 0:W_MIX] = (out_a * silu_a).astype(BF16)
    d_out_a = dy_a * silu_a
    d_ag = dy_a * out_a * (sig_a * (1.0 + a_gate * (1.0 - sig_a)))
    dproj_ref[0, :, W_MIX:2 * W_MIX] = d_ag.astype(BF16)
    gvec_ref[V_POOL_SCALE] += _colsum8(d_out_a * z)
    dz = d_out_a * pools_ref[...]
    gvec_ref[V_POOL_B] += _colsum8(dz)
    t = _row_index((tm, POOL_GC), t0)
    n_ext = tm + POOL_HALO
    for g, w in enumerate(POOL_WINDOWS):
        lanes = slice(g * POOL_GC, (g + 1) * POOL_GC)
        dzg = dz[:, lanes].astype(BF16)
        gpoolw_ref[g] += lax.dot_general(dpool_ref[0, :, lanes], dzg, TN_DIMS, preferred_element_type=F32)
        dd = lax.dot_general(dzg, poolw_ref[g].astype(BF16), NT_DIMS, preferred_element_type=F32)
        vext[0:tm, lanes] = dd * (1.0 / jnp.minimum(t + 1, w).astype(F32))
        s = vext[:, lanes]
        sh = 1
        while sh < w:
            s = s + pltpu.roll(s, n_ext - sh, 0)
            sh *= 2
        dproj_ref[0, :, lanes] = (s[0:tm] - dd).astype(BF16)
    vext[tm:, :] = vext[0:POOL_HALO, :]

    b_gate = proj_ref[0, :, 4 * W_MIX:5 * W_MIX]
    sig_b = jax.nn.sigmoid(b_gate)
    silu_b = b_gate * sig_b
    ob = ob_ref[0]
    y_s[:, W_MIX:] = (ob * silu_b).astype(BF16)
    d_ob = dy_b * silu_b
    dproj_ref[0, :, 4 * W_MIX:5 * W_MIX] = (dy_b * ob * (sig_b * (1.0 + b_gate * (1.0 - sig_b)))).astype(BF16)
    gvec_ref[V_PW_B] += _colsum8(d_ob)
    gwout_ref[...] += lax.dot_general(y_s[...], dhb, TN_DIMS, preferred_element_type=F32)

    h1 = h1_ref[0]
    mu = jnp.mean(h1, axis=-1, keepdims=True)
    xc = h1 - mu
    rstd = lax.rsqrt(jnp.mean(xc * xc, axis=-1, keepdims=True) + LN_EPS)
    xhat = xc * rstd
    h2 = xhat * lg_ref[...] + lb_ref[...]
    sig2 = jax.nn.sigmoid(h2)
    d_obb = d_ob.astype(BF16)
    gpww_ref[...] += lax.dot_general((h2 * sig2).astype(BF16), d_obb, TN_DIMS, preferred_element_type=F32)
    dh2 = lax.dot_general(d_obb, pww_ref[...], NT_DIMS, preferred_element_type=F32) * (sig2 * (1.0 + h2 * (1.0 - sig2)))
    gvec_ref[V_LN_G] += _colsum8(dh2 * xhat)
    gvec_ref[V_LN_B] += _colsum8(dh2)
    dxh = dh2 * lg_ref[...]
    dh1 = rstd * (dxh - jnp.mean(dxh, axis=-1, keepdims=True) - xhat * jnp.mean(dxh * xhat, axis=-1, keepdims=True))
    gvec_ref[V_CONV_B] += _colsum8(dh1)
    gext[0:tm, :] = dh1

    taps = _bwd_taps()
    win = CONV_ROWS + CONV_HALO - SUBLANES
    for c0 in range(0, tm, CONV_ROWS):
        for l0 in range(0, W_MIX, LANES):
            lanes = slice(l0, l0 + LANES)
            b_val = proj_ref[0, c0:c0 + CONV_ROWS, 2 * W_MIX + l0:2 * W_MIX + l0 + LANES]
            sg = jax.nn.sigmoid(proj_ref[0, c0:c0 + CONV_ROWS, 3 * W_MIX + l0:3 * W_MIX + l0 + LANES])
            h0c = b_val * sg
            acc = jnp.zeros((CONV_ROWS, LANES), F32)
            for s in range(SUBLANES):
                gsh[...] = gext[c0 + s:c0 + s + win, lanes]
                for q, k in taps[s]:
                    gq = gsh[SUBLANES * q:SUBLANES * q + CONV_ROWS, :]
                    acc = acc + gq * dw_ref[k:k + 1, lanes]
                    gdw_ref[k, :, lanes] += _colsum8(h0c * gq)
            d_bval = acc * sg
            dproj_ref[0, c0:c0 + CONV_ROWS, 2 * W_MIX + l0:2 * W_MIX + l0 + LANES] = d_bval.astype(BF16)
            dproj_ref[0, c0:c0 + CONV_ROWS, 3 * W_MIX + l0:3 * W_MIX + l0 + LANES] = (
                d_bval * b_val * (1.0 - sg)).astype(BF16)
    gext[tm:, :] = gext[0:CONV_HALO, :]


def _bwd1(dh, proj, h1, z, ob, dpool, w_out_b, pw_w_b, pool_w, pool_scale, conv_dw_f, ln_g, ln_b):
    bl, s, _ = dh.shape
    tm = TILE_M
    nt = s // tm
    tok = lambda n: pl.BlockSpec((1, tm, n), lambda b, i: (b, nt - 1 - i, 0))
    res = lambda shape: pl.BlockSpec(shape, lambda b, i: (0,) * len(shape))
    out_shapes = [(D_MODEL, D_MODEL), (W_MIX, W_MIX), (len(POOL_WINDOWS), POOL_GC, POOL_GC), (6, SUBLANES, W_MIX),
                  (CONV_WIDTH, SUBLANES, W_MIX)]
    return pl.pallas_call(
        functools.partial(_bwd1_body, nt),
        name="bwd1",
        grid=(bl, nt),
        in_specs=[tok(D_MODEL), tok(D_IN), tok(W_MIX), tok(W_MIX), tok(W_MIX), tok(W_MIX)] + [VMEM] * 7,
        out_specs=[tok(D_IN)] + [res(sh) for sh in out_shapes],
        out_shape=[jax.ShapeDtypeStruct((bl, s, D_IN), BF16)] + [jax.ShapeDtypeStruct(sh, F32) for sh in out_shapes],
        scratch_shapes=[pltpu.VMEM((tm + POOL_HALO, W_MIX), F32), pltpu.VMEM((tm + CONV_HALO, W_MIX), F32),
                        pltpu.VMEM((CONV_ROWS + CONV_HALO - SUBLANES, LANES), F32), pltpu.VMEM((tm, D_MODEL), BF16)],
        compiler_params=pltpu.CompilerParams(dimension_semantics=("arbitrary", "arbitrary"), vmem_limit_bytes=VMEM_LIMIT),
    )(dh, proj, h1, z, ob, dpool, w_out_b, pw_w_b, pool_w, _row(pool_scale), conv_dw_f, _row(ln_g), _row(ln_b))


MESH = pl.DeviceIdType.MESH
W_IN_BLK = D_IN // N_DEV
W_OUT_BLK = D_MODEL // N_DEV
PW_BLK = W_MIX // N_DEV
DW_BLK = W_MIX // N_DEV
N_PEER = N_DEV - 1


def _place():
    x, y, c = lax.axis_index("x"), lax.axis_index("y"), lax.axis_index("c")
    other_chips = [(1 - x, y), (x, 1 - y), (1 - x, 1 - y)]
    return x, y, c, other_chips


def _remote(srcs, dsts, send_sems, recv_sems, k, to):
    return [pltpu.make_async_remote_copy(src_ref=s, dst_ref=d, send_sem=send_sems.at[a, k], recv_sem=recv_sems.at[a, k],
                                         device_id=to, device_id_type=MESH)
            for a, (s, d) in enumerate(zip(srcs, dsts))]


SLAB = 2 * W_IN_BLK
F_SIB, F_X, F_Y, F_ON, F_PASS_X, F_PASS_Y, F_PASS_D = range(7)
W_IN_ONLY, OTHERS = (0,), (1, 2, 3)


def _inproj_body(n_tiles, order_ref, x_ref, ng_ref, wint_ref, wout_ref, pww_ref, dw_ref,
                 proj_ref, wt_ref, woutf_ref, pwwf_ref, dwf_ref,
                 hn_s, wt_s, wout_s, pww_s, dw_blk, send_sems, recv_sems):
    s = pl.program_id(0)
    i = pl.program_id(1)
    tm = TILE_M
    x, y, c, _ = _place()
    sibling = (x, y, 1 - c)
    xn, yn, dg = (1 - x, y), (x, 1 - y), (1 - x, 1 - y)
    pick = lambda a, b: tuple(jnp.where(c == 1, p, q) for p, q in zip(a, b))
    on_from, on_to = pick(xn, yn), pick(yn, xn)

    def blocks(px, py, pc):
        k = 4 * px + 2 * py + pc
        rows = lambda ref, n: ref.at[pl.ds(pl.multiple_of(k * n, n), n), :]
        return (rows(wt_s, W_IN_BLK), rows(wout_s, W_OUT_BLK), rows(pww_s, PW_BLK), dw_blk.at[k])

    def copies(arrays, flow, block, to):
        refs = blocks(*block)
        return [pltpu.make_async_remote_copy(src_ref=refs[a], dst_ref=refs[a], send_sem=send_sems.at[a, flow],
                                             recv_sem=recv_sems.at[a, flow], device_id=to, device_id_type=MESH)
                for a in arrays]

    def start(arrays):
        for flow, to in ((F_SIB, sibling), (F_X, (*xn, c)), (F_Y, (*yn, c))):
            for cp in copies(arrays, flow, (x, y, c), to):
                cp.start()

    def direct(arrays):
        for flow, chip in ((F_X, xn), (F_Y, yn)):
            for cp in copies(arrays, flow, (*chip, c), sibling):
                cp.wait_recv()
        for cp in copies(arrays, F_ON, (*on_from, c), (*on_to, c)):
            cp.start()
        for flow, chip in ((F_PASS_X, xn), (F_PASS_Y, yn)):
            for cp in copies(arrays, flow, (*chip, c), sibling):
                cp.start()

    def diagonal(arrays):
        for cp in copies(arrays, F_ON, (*dg, c), sibling):
            cp.wait_recv()
        for cp in copies(arrays, F_PASS_D, (*dg, c), sibling):
            cp.start()

    def from_sibling(arrays, flow, chip):
        for cp in copies(arrays, flow, (*chip, 1 - c), sibling):
            cp.wait_recv()

    def finish(arrays):
        mine = (x, y, c)
        for flow, block in ((F_SIB, mine), (F_X, mine), (F_Y, mine), (F_ON, (*on_from, c)), (F_PASS_X, (*xn, c)),
                            (F_PASS_Y, (*yn, c)), (F_PASS_D, (*dg, c))):
            for cp in copies(arrays, flow, block, sibling):
                cp.wait_send()

    first = (s == 0) & (i == 0)

    @pl.when(first)
    def _():
        mine = blocks(x, y, c)
        mine[0][...] = wint_ref[...].astype(BF16)
        mine[1][...] = wout_ref[...].astype(BF16)
        mine[2][...] = pww_ref[...].astype(BF16)
        mine[3][...] = dw_ref[...]
        start(W_IN_ONLY)

    rows_i = pl.ds(pl.multiple_of(i * tm, tm), tm)

    @pl.when(s == 0)
    def _():
        xt = x_ref[0]
        r1 = lax.rsqrt(jnp.mean(xt * xt, axis=-1, keepdims=True) + RMS_EPS)
        hn_s[rows_i, :] = (xt * r1 * ng_ref[...]).astype(BF16)

    @pl.when(first)
    def _():
        from_sibling(W_IN_ONLY, F_SIB, (x, y))

    @pl.when((s == 1) & (i == 0))
    def _():
        direct(W_IN_ONLY)
        start(OTHERS)
        from_sibling(W_IN_ONLY, F_PASS_X, xn)

    @pl.when((s == 2) & (i == 0))
    def _():
        from_sibling(W_IN_ONLY, F_PASS_Y, yn)

    @pl.when((s == 3) & (i == 0))
    def _():
        diagonal(W_IN_ONLY)
        direct(OTHERS)
        from_sibling(W_IN_ONLY, F_PASS_D, dg)

    slab = wt_s[pl.ds(pl.multiple_of(order_ref[s] * SLAB, SLAB), SLAB), :]
    proj_ref[0] = lax.dot_general(hn_s[rows_i, :], slab, NT_DIMS, preferred_element_type=F32)

    @pl.when((s == 3) & (i == n_tiles - 1))
    def _():
        from_sibling(OTHERS, F_SIB, (x, y))
        from_sibling(OTHERS, F_PASS_X, xn)
        from_sibling(OTHERS, F_PASS_Y, yn)
        diagonal(OTHERS)
        from_sibling(OTHERS, F_PASS_D, dg)
        finish(W_IN_ONLY)
        finish(OTHERS)
        wt_ref[...] = wt_s[...]
        woutf_ref[...] = wout_s[...]
        pwwf_ref[...] = pww_s[...]
        for k in range(N_DEV):
            dwf_ref[:, k * DW_BLK:(k + 1) * DW_BLK] = dw_blk[k]


def _inproj(x, norm_g, w_in_t, w_out, pw_w, conv_dw):
    bl, s, _ = x.shape
    tm = TILE_M
    nt = s // tm
    n_tiles = bl * nt
    xi, yi = lax.axis_index("x"), lax.axis_index("y")
    order = jnp.stack([2 * xi + yi, 2 * (1 - xi) + yi, 2 * xi + (1 - yi), 2 * (1 - xi) + (1 - yi)]).astype(jnp.int32)

    def x_map(sl, i, order_ref):
        i = jnp.where(sl == 0, i, n_tiles - 1)
        return (i // nt, i % nt, 0)

    whole = lambda: pl.BlockSpec(memory_space=pltpu.VMEM)
    return pl.pallas_call(
        functools.partial(_inproj_body, n_tiles),
        name="inproj",
        grid_spec=pltpu.PrefetchScalarGridSpec(
            num_scalar_prefetch=1,
            grid=(4, n_tiles),
            in_specs=[pl.BlockSpec((1, tm, D_MODEL), x_map)] + [whole() for _ in range(5)],
            out_specs=[pl.BlockSpec((1, tm, SLAB), lambda sl, i, order_ref: (i // nt, i % nt, order_ref[sl]))]
            + [whole() for _ in range(4)],
            scratch_shapes=[pltpu.VMEM((n_tiles * tm, D_MODEL), BF16), pltpu.VMEM((D_IN, D_MODEL), BF16),
                            pltpu.VMEM((D_MODEL, D_MODEL), BF16), pltpu.VMEM((W_MIX, W_MIX), BF16),
                            pltpu.VMEM((N_DEV, CONV_WIDTH, DW_BLK), F32),
                            pltpu.SemaphoreType.DMA((4, 7)), pltpu.SemaphoreType.DMA((4, 7))],
        ),
        out_shape=[jax.ShapeDtypeStruct((bl, s, D_IN), F32), jax.ShapeDtypeStruct((D_IN, D_MODEL), BF16),
                   jax.ShapeDtypeStruct((D_MODEL, D_MODEL), BF16), jax.ShapeDtypeStruct((W_MIX, W_MIX), BF16),
                   jax.ShapeDtypeStruct((CONV_WIDTH, W_MIX), F32)],
        compiler_params=pltpu.CompilerParams(dimension_semantics=("arbitrary", "arbitrary"), vmem_limit_bytes=VMEM_LIMIT),
    )(order, x, _row(norm_g), w_in_t, w_out, pw_w, conv_dw)


R_LOSS, R_FINAL_G, R_POOL, R_CONV, R_LN = range(5)
VEC_ROWS = 16
DW_ROWS = 32


def _ag_start(slots, send_sems, recv_sems):
    x, y, c, other_chips = _place()
    mine = slots(x, y, c)
    for k, to in enumerate([(x, y, 1 - c)] + [(*chip, c) for chip in other_chips]):
        for cp in _remote(mine, mine, send_sems, recv_sems, k, to):
            cp.start()


def _ag_forward(slots, send_sems, recv_sems):
    x, y, c, other_chips = _place()
    for j, chip in enumerate(other_chips):
        refs = slots(*chip, c)
        for cp in _remote(refs, refs, send_sems, recv_sems, 1 + j, (x, y, 1 - c)):
            cp.wait_recv()
        for cp in _remote(refs, refs, send_sems, recv_sems, 4 + j, (x, y, 1 - c)):
            cp.start()


def _ag_finish(slots, send_sems, recv_sems):
    x, y, c, other_chips = _place()
    sibling = (x, y, 1 - c)
    for k, block in enumerate([(x, y, 1 - c)] + [None] * 3 + [(*chip, 1 - c) for chip in other_chips]):
        if block is not None:
            refs = slots(*block)
            for cp in _remote(refs, refs, send_sems, recv_sems, k, sibling):
                cp.wait_recv()
    mine = slots(x, y, c)
    for k, block in enumerate([mine] * 4 + [slots(*chip, c) for chip in other_chips]):
        for cp in _remote(block, block, send_sems, recv_sems, k, sibling):
            cp.wait_send()


def _rs_send_a(to_sibling, landed, send_a, recv_a):
    x, y, c, _ = _place()
    for chip in range(4):
        for cp in _remote(to_sibling(2 * chip + (1 - c)), landed(chip), send_a, recv_a, chip, (x, y, 1 - c)):
            cp.start()


def _rs_chip_sum(mine, landed, send_a, recv_a, chip):
    x, y, c, _ = _place()
    for cp in _remote(landed(chip), landed(chip), send_a, recv_a, chip, (x, y, 1 - c)):
        cp.wait_recv()
    return [m[...] + r[...] for m, r in zip(mine(chip), landed(chip))]


def _rs_send_b(mine, landed, staged, arrived, send_a, recv_a, send_b, recv_b):
    x, y, c, other_chips = _place()
    for j, (px, py) in enumerate(other_chips):
        for o, v in zip(staged(j), _rs_chip_sum(mine, landed, send_a, recv_a, 2 * px + py)):
            o[...] = v.astype(BF16)
        for cp in _remote(staged(j), arrived(j), send_b, recv_b, j, (px, py, c)):
            cp.start()


def _rs_finish(to_sibling, mine, landed, staged, arrived, send_a, recv_a, send_b, recv_b):
    x, y, c, _ = _place()
    sibling = (x, y, 1 - c)
    total = _rs_chip_sum(mine, landed, send_a, recv_a, 2 * x + y)
    for j in range(3):
        for cp in _remote(arrived(j), arrived(j), send_b, recv_b, j, sibling):
            cp.wait_recv()
        total = [t + g[...].astype(F32) for t, g in zip(total, arrived(j))]
    for chip in range(4):
        for cp in _remote(to_sibling(2 * chip + (1 - c)), landed(chip), send_a, recv_a, chip, sibling):
            cp.wait_send()
    for j in range(3):
        for cp in _remote(staged(j), arrived(j), send_b, recv_b, j, sibling):
            cp.wait_send()
    return total


EXCHANGE_STEP = 2


def _bwd2a_body(n_steps, x_ref, dproj_ref, ng_ref, gwout_ref, gpww_ref, loss_ref, gfg_ref, gvec_ref, gdw_ref, gpoolw_ref,
                stwin_ref, ownwin_ref, owout_ref, opww_ref, ovec_ref, odw_ref, opoolw_ref,
                acc, slab4, a_win, a_wout, a_pww, sb_wout, sb_pww, rb_wout, rb_pww, all_vec, all_dw, all_poolw,
                send_a, recv_a, send_b, recv_b, send_g, recv_g, send_w, recv_w):
    step = pl.program_id(0) * pl.num_programs(1) + pl.program_id(1)
    x, y, c, other_chips = _place()

    def rows(ref, k, n):
        return ref.at[pl.ds(pl.multiple_of(k * n, n), n), :]

    to_sibling = lambda k: (rows(gwout_ref, k, W_OUT_BLK), rows(gpww_ref, k, PW_BLK))
    mine = lambda chip: to_sibling(2 * chip + c)
    landed = lambda chip: (a_wout.at[chip], a_pww.at[chip])
    staged = lambda j: (sb_wout.at[j], sb_pww.at[j])
    arrived = lambda j: (rb_wout.at[j], rb_pww.at[j])

    def slots(px, py, pc):
        k = 4 * px + 2 * py + pc
        return (all_vec.at[k], all_dw.at[k], all_poolw.at[k])

    @pl.when(step == 0)
    def _():
        acc[...] = jnp.zeros_like(acc)
        sum8 = lambda v: jnp.sum(v, axis=0, keepdims=True)
        vec, dwp, pwp = slots(x, y, c)
        vec[...] = jnp.zeros((VEC_ROWS, D_MODEL), F32)
        vec[R_LOSS:R_LOSS + 1, :] = jnp.broadcast_to(loss_ref[0:1, 0:1], (1, D_MODEL))
        vec[R_FINAL_G:R_FINAL_G + 1, :] = sum8(gfg_ref[...])
        for row, (lo, hi) in ((R_POOL, (V_POOL_B, V_POOL_SCALE)), (R_CONV, (V_CONV_B, V_LN_G)), (R_LN, (V_LN_B, V_PW_B))):
            vec[row:row + 1, 0:W_MIX] = sum8(gvec_ref[lo])
            vec[row:row + 1, W_MIX:] = sum8(gvec_ref[hi])
        dwp[0:CONV_WIDTH, :] = jnp.sum(gdw_ref[...], axis=1)
        dwp[CONV_WIDTH:, :] = jnp.zeros((DW_ROWS - CONV_WIDTH, W_MIX), F32)
        pwp[...] = gpoolw_ref[...].reshape(len(POOL_WINDOWS) * POOL_GC, POOL_GC)
        _rs_send_a(to_sibling, landed, send_a, recv_a)
        _ag_start(slots, send_g, recv_g)

    x_t = x_ref[0]
    r1 = lax.rsqrt(jnp.mean(x_t * x_t, axis=-1, keepdims=True) + RMS_EPS)
    hn = (x_t * r1 * ng_ref[...]).astype(BF16)
    acc[...] += lax.dot_general(hn, dproj_ref[0], TN_DIMS, preferred_element_type=F32)

    @pl.when(step == EXCHANGE_STEP)
    def _():
        _rs_send_b(mine, landed, staged, arrived, send_a, recv_a, send_b, recv_b)
        _ag_forward(slots, send_g, recv_g)

    @pl.when(step == n_steps - 1)
    def _():
        slab = lambda k: acc[:, k * W_IN_BLK:(k + 1) * W_IN_BLK]
        w_copy = lambda chip: _remote((slab4.at[chip],), (a_win.at[chip],), send_w, recv_w, chip, (x, y, 1 - c))[0]
        for cc in range(2):
            @pl.when(c == cc)
            def _():
                for chip in range(4):
                    slab4[chip] = slab(2 * chip + (1 - cc))
        for chip in range(4):
            w_copy(chip).start()

        owout_ref[...], opww_ref[...] = _rs_finish(to_sibling, mine, landed, staged, arrived, send_a, recv_a, send_b, recv_b)
        _ag_finish(slots, send_g, recv_g)
        for o, a in ((ovec_ref, all_vec), (odw_ref, all_dw), (opoolw_ref, all_poolw)):
            s = a[0]
            for k in range(1, N_DEV):
                s = s + a[k]
            o[...] = s

        for chip in range(4):
            w_copy(chip).wait()
        for cc in range(2):
            @pl.when(c == cc)
            def _():
                for chip in range(4):
                    slab4[chip] = slab(2 * chip + cc) + a_win[chip]
        for j, (px, py) in enumerate(other_chips):
            stwin_ref[j] = slab4[2 * px + py].astype(BF16)
        ownwin_ref[...] = slab4[2 * x + y]


def _bwd2a(x, dproj, norm_g, gwout, gpww, loss8, gfg8, gvec, gdw8, gpoolw):
    bl, s, _ = x.shape
    tm = TILE_M
    nt = s // tm
    tok = lambda n: pl.BlockSpec((1, tm, n), lambda b, i: (b, i, 0))
    n_pw = len(POOL_WINDOWS) * POOL_GC
    win = (D_MODEL, W_IN_BLK)
    blk = ((W_OUT_BLK, D_MODEL), (PW_BLK, W_MIX))
    rep = ((VEC_ROWS, D_MODEL), (DW_ROWS, W_MIX), (n_pw, POOL_GC))
    dma = pltpu.SemaphoreType.DMA
    return pl.pallas_call(
        functools.partial(_bwd2a_body, bl * nt),
        name="bwd2a",
        grid=(bl, nt),
        in_specs=[tok(D_MODEL), tok(D_IN)] + [VMEM] * 8,
        out_specs=[VMEM] * 7,
        out_shape=[jax.ShapeDtypeStruct((3,) + win, BF16), jax.ShapeDtypeStruct(win, F32)]
        + [jax.ShapeDtypeStruct(sh, F32) for sh in blk + rep],
        scratch_shapes=[pltpu.VMEM((D_MODEL, D_IN), F32), pltpu.VMEM((4,) + win, F32), pltpu.VMEM((4,) + win, F32)]
        + [pltpu.VMEM((4,) + sh, F32) for sh in blk]
        + [pltpu.VMEM((3,) + sh, BF16) for sh in blk] * 2
        + [pltpu.VMEM((N_DEV,) + sh, F32) for sh in rep]
        + [dma((2, 4)), dma((2, 4)), dma((2, 3)), dma((2, 3)), dma((3, N_PEER)), dma((3, N_PEER)), dma((1, 4)), dma((1, 4))],
        compiler_params=pltpu.CompilerParams(dimension_semantics=("arbitrary", "arbitrary"), vmem_limit_bytes=VMEM_LIMIT),
    )(x, dproj, _row(norm_g), gwout, gpww, loss8, gfg8, gvec, gdw8, gpoolw)


def _bwd2b_body(n_steps, x_ref, dh_ref, dproj_ref, ng_ref, wint_ref, stwin_hbm, ownwin_ref,
                gx_ref, owin_ref, ong_ref,
                gng, rb_win, all_ng, send_b, recv_b, send_n, recv_n):
    step = pl.program_id(0) * pl.num_programs(1) + pl.program_id(1)
    x, y, c, other_chips = _place()
    sibling = (x, y, 1 - c)
    b_copy = lambda j: pltpu.make_async_remote_copy(src_ref=stwin_hbm.at[j], dst_ref=rb_win.at[j], send_sem=send_b.at[j],
                                                    recv_sem=recv_b.at[j], device_id=(*other_chips[j], c), device_id_type=MESH)

    @pl.when(step == 0)
    def _():
        gng[...] = jnp.zeros_like(gng)
        for j in range(3):
            b_copy(j).start()

    x_t = x_ref[0]
    r1 = lax.rsqrt(jnp.mean(x_t * x_t, axis=-1, keepdims=True) + RMS_EPS)
    xr = x_t * r1
    dhn = jnp.dot(dproj_ref[0], wint_ref[...], preferred_element_type=F32)
    gng[...] += _colsum8(dhn * xr)
    gy = dhn * ng_ref[...]
    gx_ref[0] = dh_ref[0] + r1 * (gy - xr * jnp.mean(gy * xr, axis=-1, keepdims=True))

    k_me = 4 * x + 2 * y + c
    peers = [sibling] + [(*chip, c) for chip in other_chips] + [(*chip, 1 - c) for chip in other_chips]
    ng_copy = lambda r: pltpu.make_async_remote_copy(src_ref=all_ng.at[k_me], dst_ref=all_ng.at[k_me], send_sem=send_n.at[r],
                                                     recv_sem=recv_n.at[r], device_id=peers[r], device_id_type=MESH)

    @pl.when(step == n_steps - 1)
    def _():
        all_ng[k_me] = gng[...]
        for r in range(N_PEER):
            ng_copy(r).start()
        total = ownwin_ref[...]
        for j in range(3):
            b_copy(j).wait()
            total = total + rb_win[j].astype(F32)
        owin_ref[...] = total
        for r in range(N_PEER):
            ng_copy(r).wait()
        s = all_ng[0]
        for k in range(1, N_DEV):
            s = s + all_ng[k]
        ong_ref[...] = jnp.broadcast_to(jnp.sum(s, axis=0, keepdims=True), ong_ref.shape)


def _bwd2b(x, dh, dproj, norm_g, w_in_t, st_win, own_win):
    bl, s, _ = x.shape
    tm = TILE_M
    nt = s // tm
    tok = lambda n: pl.BlockSpec((1, tm, n), lambda b, i: (b, i, 0))
    blk = (D_MODEL, W_IN_BLK)
    dma = pltpu.SemaphoreType.DMA
    return pl.pallas_call(
        functools.partial(_bwd2b_body, bl * nt),
        name="bwd2b",
        grid=(bl, nt),
        in_specs=[tok(D_MODEL), tok(D_MODEL), tok(D_IN), VMEM, VMEM, pl.BlockSpec(memory_space=pl.ANY), VMEM],
        out_specs=[tok(D_MODEL), VMEM, VMEM],
        out_shape=[jax.ShapeDtypeStruct((bl, s, D_MODEL), F32), jax.ShapeDtypeStruct(blk, F32),
                   jax.ShapeDtypeStruct((SUBLANES, D_MODEL), F32)],
        scratch_shapes=[pltpu.VMEM((SUBLANES, D_MODEL), F32), pltpu.VMEM((3,) + blk, BF16),
                        pltpu.VMEM((N_DEV, SUBLANES, D_MODEL), F32), dma((3,)), dma((3,)), dma((N_PEER,)), dma((N_PEER,))],
        compiler_params=pltpu.CompilerParams(dimension_semantics=("arbitrary", "arbitrary"), vmem_limit_bytes=VMEM_LIMIT),
    )(x, dh, dproj, _row(norm_g), w_in_t, st_win, own_win)


ADAM_LR = 0.001
ADAM_B1 = 0.9
ADAM_B2 = 0.999
ADAM_EPS = 1e-08
ADAM_WD = 0.01
ADAM_STEP = 10


def _adamw_body(n, *refs):
    ws, gs, ms, vs = (refs[k * n:(k + 1) * n] for k in range(4))
    ds, m2s, v2s = (refs[(4 + k) * n:(5 + k) * n] for k in range(3))
    for w_ref, g_ref, m_ref, v_ref, d_ref, m2_ref, v2_ref in zip(ws, gs, ms, vs, ds, m2s, v2s):
        g = g_ref[...]
        m = ADAM_B1 * m_ref[...] + (1.0 - ADAM_B1) * g
        v = ADAM_B2 * v_ref[...] + (1.0 - ADAM_B2) * (g * g)
        m_hat = m / (1.0 - ADAM_B1 ** ADAM_STEP)
        v_hat = v / (1.0 - ADAM_B2 ** ADAM_STEP)
        d_ref[...] = -ADAM_LR * (m_hat / (jnp.sqrt(v_hat) + ADAM_EPS) + ADAM_WD * w_ref[...])
        m2_ref[...] = m
        v2_ref[...] = v


def _adamw(ws, gs, ms, vs):
    n = len(ws)
    shapes = [w.shape for w in ws]
    flat = lambda a: a.reshape(1, -1) if a.ndim == 1 else a
    args = [flat(a) for a in (*ws, *gs, *ms, *vs)]
    outs = pl.pallas_call(
        functools.partial(_adamw_body, n),
        name="adamw",
        in_specs=[VMEM] * (4 * n),
        out_specs=[VMEM] * (3 * n),
        out_shape=[jax.ShapeDtypeStruct(a.shape, F32) for a in args[:n]] * 3,
        compiler_params=pltpu.CompilerParams(vmem_limit_bytes=VMEM_LIMIT),
    )(*args)
    outs = [o.reshape(shapes[k % n]) for k, o in enumerate(outs)]
    return outs[:n], outs[n:2 * n], outs[2 * n:]


def kernel(x, norm_g, w_in, pool_w, pool_b, pool_scale, conv_dw, conv_b, ln_g, ln_b, pw_w, pw_b, w_out, final_g, loss_target, m_norm_g, m_w_in, m_pool_w, m_pool_b, m_pool_scale, m_conv_dw, m_conv_b, m_ln_g, m_ln_b, m_pw_w, m_pw_b, m_w_out, m_final_g, v_norm_g, v_w_in, v_pool_w, v_pool_b, v_pool_scale, v_conv_dw, v_conv_b, v_ln_g, v_ln_b, v_pw_w, v_pw_b, v_w_out, v_final_g):
    proj, w_in_t, w_out_b, pw_w_b, conv_dw_f = _inproj(x, norm_g, w_in.T, w_out, pw_w, conv_dw)
    h1, z, ob, dpool, dh, loss8, gfg8 = _fwd(x, loss_target, proj, pool_w, pool_b, pool_scale, conv_dw_f,
                                             conv_b, ln_g, ln_b, pw_w_b, pw_b, w_out_b, final_g)
    dproj, gwout, gpww, gpoolw, gvec, gdw8 = _bwd1(dh, proj, h1, z, ob, dpool, w_out_b, pw_w_b, pool_w, pool_scale,
                                                   conv_dw_f, ln_g, ln_b)
    st_win, own_win, g_w_out, g_pw_w, vec, dw_sum, poolw_sum = _bwd2a(x, dproj, norm_g, gwout, gpww, loss8, gfg8, gvec, gdw8,
                                                                      gpoolw)
    grad_x, g_w_in, g_norm_g = _bwd2b(x, dh, dproj, norm_g, w_in_t, st_win, own_win)

    me = 4 * lax.axis_index("x") + 2 * lax.axis_index("y") + lax.axis_index("c")
    n_g = len(POOL_WINDOWS)
    grads = dict(
        norm_g=g_norm_g[0], w_in=g_w_in, pool_w=poolw_sum.reshape(n_g, POOL_GC, POOL_GC),
        pool_b=vec[R_POOL, :W_MIX].reshape(n_g, POOL_GC), pool_scale=vec[R_POOL, W_MIX:],
        conv_dw=lax.dynamic_slice(dw_sum, (0, me * DW_BLK), (CONV_WIDTH, DW_BLK)), conv_b=vec[R_CONV, :W_MIX],
        ln_g=vec[R_CONV, W_MIX:], ln_b=vec[R_LN, :W_MIX], pw_w=g_pw_w, pw_b=vec[R_LN, W_MIX:], w_out=g_w_out,
        final_g=vec[R_FINAL_G])
    weights = dict(norm_g=norm_g, w_in=w_in, pool_w=pool_w, pool_b=pool_b, pool_scale=pool_scale, conv_dw=conv_dw, conv_b=conv_b,
                   ln_g=ln_g, ln_b=ln_b, pw_w=pw_w, pw_b=pw_b, w_out=w_out, final_g=final_g)
    m_in = dict(norm_g=m_norm_g, w_in=m_w_in, pool_w=m_pool_w, pool_b=m_pool_b, pool_scale=m_pool_scale, conv_dw=m_conv_dw,
                conv_b=m_conv_b, ln_g=m_ln_g, ln_b=m_ln_b, pw_w=m_pw_w, pw_b=m_pw_b, w_out=m_w_out, final_g=m_final_g)
    v_in = dict(norm_g=v_norm_g, w_in=v_w_in, pool_w=v_pool_w, pool_b=v_pool_b, pool_scale=v_pool_scale, conv_dw=v_conv_dw,
                conv_b=v_conv_b, ln_g=v_ln_g, ln_b=v_ln_b, pw_w=v_pw_w, pw_b=v_pw_b, w_out=v_w_out, final_g=v_final_g)
    names = list(weights)
    delta, new_m, new_v = _adamw([weights[n] for n in names], [grads[n] for n in names], [m_in[n] for n in names],
                                 [v_in[n] for n in names])
    return (vec[R_LOSS, 0], grad_x, *[grads[n] for n in names], *delta, *new_m, *new_v)
```

```python
import functools

import jax
import jax.numpy as jnp
from jax import lax
from jax.experimental import pallas as pl
from jax.experimental.pallas import tpu as pltpu

F32 = jnp.float32
BF16 = jnp.bfloat16

D_MODEL = 1024
W_MIX = 512
D_IN = 5 * W_MIX
POOL_WINDOWS = (2, 4, 8, 16)
POOL_GC = 128
CONV_WIDTH = 31
RMS_EPS = 1e-6
LN_EPS = 1e-5
N_DEV = 8

LANES = 128
SUBLANES = 8
TILE_M = 512
TILE_IN = 1024
POOL_HALO = 16
CONV_HALO = 32
CONV_ROWS = 128
VMEM_LIMIT = 56 * 1024 * 1024

VMEM = pl.BlockSpec(memory_space=pltpu.VMEM)


def _silu(v):
    return v * jax.nn.sigmoid(v)


def _dsilu(v):
    s = jax.nn.sigmoid(v)
    return s * (1.0 + v * (1.0 - s))


def _colsum8(v):
    m, n = v.shape
    return jnp.sum(v.reshape(m // SUBLANES, SUBLANES, n), axis=0)


def _row_index(shape, t0):
    return lax.broadcasted_iota(jnp.int32, shape, 0) + t0


def _conv_taps():
    taps = {s: [] for s in range(SUBLANES)}
    for k in range(CONV_WIDTH):
        e = k + CONV_HALO - (CONV_WIDTH - 1)
        taps[e % SUBLANES].append((e // SUBLANES, k))
    return taps


def _fwd_body(x_ref, tgt_ref, proj_ref, poolw_ref, poolb_ref, pools_ref, dw_ref, cb_ref, lg_ref, lb_ref,
              pww_ref, pwb_ref, wout_ref, fg_ref,
              h1_ref, z_ref, ob_ref, dpool_ref, dh_ref, loss_ref, gfg_ref,
              uext, hext, y_s):
    b = pl.program_id(0)
    i = pl.program_id(1)
    tm = TILE_M

    @pl.when(i == 0)
    def _():
        uext[0:POOL_HALO, :] = jnp.zeros((POOL_HALO, W_MIX), F32)
        hext[0:CONV_HALO, :] = jnp.zeros((CONV_HALO, W_MIX), F32)
        hext[CONV_HALO + tm:, :] = jnp.zeros((SUBLANES, W_MIX), F32)

    @pl.when((b == 0) & (i == 0))
    def _():
        loss_ref[...] = jnp.zeros_like(loss_ref)
        gfg_ref[...] = jnp.zeros_like(gfg_ref)

    uext[POOL_HALO:, :] = proj_ref[0, :, 0:W_MIX]
    t = _row_index((tm, POOL_GC), i * tm)
    for g, w in enumerate(POOL_WINDOWS):
        lanes = slice(g * POOL_GC, (g + 1) * POOL_GC)
        e = uext[:, lanes]
        s = e
        sh = 1
        while sh < w:
            s = s + pltpu.roll(s, sh, 0)
            sh *= 2
        inv_cnt = 1.0 / jnp.minimum(t + 1, w).astype(F32)
        d = (s[POOL_HALO:] * inv_cnt - e[POOL_HALO:]).astype(BF16)
        dpool_ref[0, :, lanes] = d
        z = jnp.dot(d, poolw_ref[g].astype(BF16), preferred_element_type=F32) + poolb_ref[:, lanes]
        z_ref[0, :, lanes] = z
        a_gate = proj_ref[0, :, W_MIX + g * POOL_GC:W_MIX + (g + 1) * POOL_GC]
        y_s[:, lanes] = (z * pools_ref[:, lanes] * _silu(a_gate)).astype(BF16)

    hext[CONV_HALO:CONV_HALO + tm, :] = proj_ref[0, :, 2 * W_MIX:3 * W_MIX] * jax.nn.sigmoid(proj_ref[0, :, 3 * W_MIX:4 * W_MIX])
    taps = _conv_taps()
    for c0 in range(0, tm, CONV_ROWS):
        for l0 in range(0, W_MIX, LANES):
            lanes = slice(l0, l0 + LANES)
            acc = jnp.zeros((CONV_ROWS, LANES), F32) + cb_ref[:, lanes]
            for s in range(SUBLANES):
                part = jnp.zeros((CONV_ROWS + SUBLANES, LANES), F32)
                for q, k in taps[s]:
                    r0 = c0 + SUBLANES * q
                    part = part + hext[r0:r0 + CONV_ROWS + SUBLANES, lanes] * dw_ref[k:k + 1, lanes]
                acc = acc + part[s:s + CONV_ROWS]
            h1_ref[0, c0:c0 + CONV_ROWS, lanes] = acc
    h1 = h1_ref[0]
    mu = jnp.mean(h1, axis=-1, keepdims=True)
    xc = h1 - mu
    rstd = lax.rsqrt(jnp.mean(xc * xc, axis=-1, keepdims=True) + LN_EPS)
    h3 = _silu(xc * rstd * lg_ref[...] + lb_ref[...])
    ob = jnp.dot(h3.astype(BF16), pww_ref[...], preferred_element_type=F32) + pwb_ref[...]
    ob_ref[0] = ob
    y_s[:, W_MIX:] = (ob * _silu(proj_ref[0, :, 4 * W_MIX:5 * W_MIX])).astype(BF16)

    h = x_ref[0] + jnp.dot(y_s[...], wout_ref[...], preferred_element_type=F32)
    r2 = lax.rsqrt(jnp.mean(h * h, axis=-1, keepdims=True) + RMS_EPS)
    hr = h * r2
    err = hr * fg_ref[...] - tgt_ref[0]
    loss_ref[...] += jnp.sum(err * err) * (0.5 / D_MODEL)
    dout = err * (1.0 / D_MODEL)
    gfg_ref[...] += _colsum8(dout * hr)
    gy = dout * fg_ref[...]
    dh_ref[0] = r2 * (gy - hr * jnp.mean(gy * hr, axis=-1, keepdims=True))

    uext[0:POOL_HALO, :] = uext[tm:tm + POOL_HALO, :]
    hext[0:CONV_HALO, :] = hext[tm:tm + CONV_HALO, :]


def _row(v):
    return v.reshape(1, -1)


def _fwd(x, tgt, proj, pool_w, pool_b, pool_scale, conv_dw_f, conv_b, ln_g, ln_b, pw_w_b, pw_b, w_out_b, final_g):
    bl, s, _ = x.shape
    tm = TILE_M
    tok = lambda n: pl.BlockSpec((1, tm, n), lambda b, i: (b, i, 0))
    acc = lambda n: pl.BlockSpec((SUBLANES, n), lambda b, i: (0, 0))
    act = lambda n: jax.ShapeDtypeStruct((bl, s, n), F32)
    return pl.pallas_call(
        functools.partial(_fwd_body),
        name="fwd",
        grid=(bl, s // tm),
        in_specs=[tok(D_MODEL), tok(D_MODEL), tok(D_IN)] + [VMEM] * 11,
        out_specs=[tok(W_MIX), tok(W_MIX), tok(W_MIX), tok(W_MIX), tok(D_MODEL), acc(LANES), acc(D_MODEL)],
        out_shape=[act(W_MIX), act(W_MIX), act(W_MIX), jax.ShapeDtypeStruct((bl, s, W_MIX), BF16), act(D_MODEL),
                   jax.ShapeDtypeStruct((SUBLANES, LANES), F32), jax.ShapeDtypeStruct((SUBLANES, D_MODEL), F32)],
        scratch_shapes=[pltpu.VMEM((POOL_HALO + tm, W_MIX), F32), pltpu.VMEM((CONV_HALO + tm + SUBLANES, W_MIX), F32),
                        pltpu.VMEM((tm, D_MODEL), BF16)],
        compiler_params=pltpu.CompilerParams(dimension_semantics=("arbitrary", "arbitrary"), vmem_limit_bytes=VMEM_LIMIT),
    )(x, tgt, proj, pool_w, pool_b.reshape(1, W_MIX), _row(pool_scale), conv_dw_f, _row(conv_b),
      _row(ln_g), _row(ln_b), pw_w_b, _row(pw_b), w_out_b, _row(final_g))


NT_DIMS = (((1,), (1,)), ((), ()))
TN_DIMS = (((0,), (0,)), ((), ()))
V_POOL_B, V_POOL_SCALE, V_CONV_B, V_LN_G, V_LN_B, V_PW_B = range(6)


def _bwd_taps():
    taps = {s: [] for s in range(SUBLANES)}
    for k in range(CONV_WIDTH):
        e = CONV_WIDTH - 1 - k
        taps[e % SUBLANES].append((e // SUBLANES, k))
    return taps


def _bwd1_body(n_tiles, dh_ref, proj_ref, h1_ref, z_ref, ob_ref, dpool_ref, wout_ref, pww_ref, poolw_ref, pools_ref, dw_ref,
               lg_ref, lb_ref,
               dproj_ref, gwout_ref, gpww_ref, gpoolw_ref, gvec_ref, gdw_ref,
               vext, gext, gsh, y_s):
    b = pl.program_id(0)
    i = pl.program_id(1)
    tm = TILE_M
    t0 = (n_tiles - 1 - i) * tm

    @pl.when(i == 0)
    def _():
        vext[tm:, :] = jnp.zeros((POOL_HALO, W_MIX), F32)
        gext[tm:, :] = jnp.zeros((CONV_HALO, W_MIX), F32)

    @pl.when((b == 0) & (i == 0))
    def _():
        gwout_ref[...] = jnp.zeros_like(gwout_ref)
        gpww_ref[...] = jnp.zeros_like(gpww_ref)
        gpoolw_ref[...] = jnp.zeros_like(gpoolw_ref)
        gvec_ref[...] = jnp.zeros_like(gvec_ref)
        gdw_ref[...] = jnp.zeros_like(gdw_ref)

    dhb = dh_ref[0].astype(BF16)
    dy_a = lax.dot_general(dhb, wout_ref[0:W_MIX, :], NT_DIMS, preferred_element_type=F32)
    dy_b = lax.dot_general(dhb, wout_ref[W_MIX:, :], NT_DIMS, preferred_element_type=F32)

    a_gate = proj_ref[0, :, W_MIX:2 * W_MIX]
    sig_a = jax.nn.sigmoid(a_gate)
    silu_a = a_gate * sig_a
    z = z_ref[0]
    out_a = z * pools_ref[...]
    y_s[:, 0:W_MIX] = (out_a * silu_a).astype(BF16)
    d_out_a = dy_a * silu_a
    d_ag = dy_a * out_a * (sig_a * (1.0 + a_gate * (1.0 - sig_a)))
    dproj_ref[0, :, W_MIX:2 * W_MIX] = d_ag.astype(BF16)
    gvec_ref[V_POOL_SCALE] += _colsum8(d_out_a * z)
    dz = d_out_a * pools_ref[...]
    gvec_ref[V_POOL_B] += _colsum8(dz)
    t = _row_index((tm, POOL_GC), t0)
    n_ext = tm + POOL_HALO
    for g, w in enumerate(POOL_WINDOWS):
        lanes = slice(g * POOL_GC, (g + 1) * POOL_GC)
        dzg = dz[:, lanes].astype(BF16)
        gpoolw_ref[g] += lax.dot_general(dpool_ref[0, :, lanes], dzg, TN_DIMS, preferred_element_type=F32)
        dd = lax.dot_general(dzg, poolw_ref[g].astype(BF16), NT_DIMS, preferred_element_type=F32)
        vext[0:tm, lanes] = dd * (1.0 / jnp.minimum(t + 1, w).astype(F32))
        s = vext[:, lanes]
        sh = 1
        while sh < w:
            s = s + pltpu.roll(s, n_ext - sh, 0)
            sh *= 2
        dproj_ref[0, :, lanes] = (s[0:tm] - dd).astype(BF16)
    vext[tm:, :] = vext[0:POOL_HALO, :]

    b_gate = proj_ref[0, :, 4 * W_MIX:5 * W_MIX]
    sig_b = jax.nn.sigmoid(b_gate)
    silu_b = b_gate * sig_b
    ob = ob_ref[0]
    y_s[:, W_MIX:] = (ob * silu_b).astype(BF16)
    d_ob = dy_b * silu_b
    dproj_ref[0, :, 4 * W_MIX:5 * W_MIX] = (dy_b * ob * (sig_b * (1.0 + b_gate * (1.0 - sig_b)))).astype(BF16)
    gvec_ref[V_PW_B] += _colsum8(d_ob)
    gwout_ref[...] += lax.dot_general(y_s[...], dhb, TN_DIMS, preferred_element_type=F32)

    h1 = h1_ref[0]
    mu = jnp.mean(h1, axis=-1, keepdims=True)
    xc = h1 - mu
    rstd = lax.rsqrt(jnp.mean(xc * xc, axis=-1, keepdims=True) + LN_EPS)
    xhat = xc * rstd
    h2 = xhat * lg_ref[...] + lb_ref[...]
    sig2 = jax.nn.sigmoid(h2)
    d_obb = d_ob.astype(BF16)
    gpww_ref[...] += lax.dot_general((h2 * sig2).astype(BF16), d_obb, TN_DIMS, preferred_element_type=F32)
    dh2 = lax.dot_general(d_obb, pww_ref[...], NT_DIMS, preferred_element_type=F32) * (sig2 * (1.0 + h2 * (1.0 - sig2)))
    gvec_ref[V_LN_G] += _colsum8(dh2 * xhat)
    gvec_ref[V_LN_B] += _colsum8(dh2)
    dxh = dh2 * lg_ref[...]
    dh1 = rstd * (dxh - jnp.mean(dxh, axis=-1, keepdims=True) - xhat * jnp.mean(dxh * xhat, axis=-1, keepdims=True))
    gvec_ref[V_CONV_B] += _colsum8(dh1)
    gext[0:tm, :] = dh1

    taps = _bwd_taps()
    win = CONV_ROWS + CONV_HALO - SUBLANES
    for c0 in range(0, tm, CONV_ROWS):
        for l0 in range(0, W_MIX, LANES):
            lanes = slice(l0, l0 + LANES)
            b_val = proj_ref[0, c0:c0 + CONV_ROWS, 2 * W_MIX + l0:2 * W_MIX + l0 + LANES]
            sg = jax.nn.sigmoid(proj_ref[0, c0:c0 + CONV_ROWS, 3 * W_MIX + l0:3 * W_MIX + l0 + LANES])
            h0c = b_val * sg
            acc = jnp.zeros((CONV_ROWS, LANES), F32)
            for s in range(SUBLANES):
                gsh[...] = gext[c0 + s:c0 + s + win, lanes]
                for q, k in taps[s]:
                    gq = gsh[SUBLANES * q:SUBLANES * q + CONV_ROWS, :]
                    acc = acc + gq * dw_ref[k:k + 1, lanes]
                    gdw_ref[k, :, lanes] += _colsum8(h0c * gq)
            d_bval = acc * sg
            dproj_ref[0, c0:c0 + CONV_ROWS, 2 * W_MIX + l0:2 * W_MIX + l0 + LANES] = d_bval.astype(BF16)
            dproj_ref[0, c0:c0 + CONV_ROWS, 3 * W_MIX + l0:3 * W_MIX + l0 + LANES] = (
                d_bval * b_val * (1.0 - sg)).astype(BF16)
    gext[tm:, :] = gext[0:CONV_HALO, :]


def _bwd1(dh, proj, h1, z, ob, dpool, w_out_b, pw_w_b, pool_w, pool_scale, conv_dw_f, ln_g, ln_b):
    bl, s, _ = dh.shape
    tm = TILE_M
    nt = s // tm
    tok = lambda n: pl.BlockSpec((1, tm, n), lambda b, i: (b, nt - 1 - i, 0))
    res = lambda shape: pl.BlockSpec(shape, lambda b, i: (0,) * len(shape))
    out_shapes = [(D_MODEL, D_MODEL), (W_MIX, W_MIX), (len(POOL_WINDOWS), POOL_GC, POOL_GC), (6, SUBLANES, W_MIX),
                  (CONV_WIDTH, SUBLANES, W_MIX)]
    return pl.pallas_call(
        functools.partial(_bwd1_body, nt),
        name="bwd1",
        grid=(bl, nt),
        in_specs=[tok(D_MODEL), tok(D_IN), tok(W_MIX), tok(W_MIX), tok(W_MIX), tok(W_MIX)] + [VMEM] * 7,
        out_specs=[tok(D_IN)] + [res(sh) for sh in out_shapes],
        out_shape=[jax.ShapeDtypeStruct((bl, s, D_IN), BF16)] + [jax.ShapeDtypeStruct(sh, F32) for sh in out_shapes],
        scratch_shapes=[pltpu.VMEM((tm + POOL_HALO, W_MIX), F32), pltpu.VMEM((tm + CONV_HALO, W_MIX), F32),
                        pltpu.VMEM((CONV_ROWS + CONV_HALO - SUBLANES, LANES), F32), pltpu.VMEM((tm, D_MODEL), BF16)],
        compiler_params=pltpu.CompilerParams(dimension_semantics=("arbitrary", "arbitrary"), vmem_limit_bytes=VMEM_LIMIT),
    )(dh, proj, h1, z, ob, dpool, w_out_b, pw_w_b, pool_w, _row(pool_scale), conv_dw_f, _row(ln_g), _row(ln_b))


MESH = pl.DeviceIdType.MESH
W_IN_BLK = D_IN // N_DEV
W_OUT_BLK = D_MODEL // N_DEV
PW_BLK = W_MIX // N_DEV
DW_BLK = W_MIX // N_DEV
N_PEER = N_DEV - 1


def _place():
    x, y, c = lax.axis_index("x"), lax.axis_index("y"), lax.axis_index("c")
    other_chips = [(1 - x, y), (x, 1 - y), (1 - x, 1 - y)]
    return x, y, c, other_chips


def _remote(srcs, dsts, send_sems, recv_sems, k, to):
    return [pltpu.make_async_remote_copy(src_ref=s, dst_ref=d, send_sem=send_sems.at[a, k], recv_sem=recv_sems.at[a, k],
                                         device_id=to, device_id_type=MESH)
            for a, (s, d) in enumerate(zip(srcs, dsts))]


SLAB = 2 * W_IN_BLK
F_SIB, F_X, F_Y, F_ON, F_PASS_X, F_PASS_Y, F_PASS_D = range(7)
W_IN_ONLY, OTHERS = (0,), (1, 2, 3)


def _inproj_body(n_tiles, x_ref, ng_ref, wint_ref, wout_ref, pww_ref, dw_ref,
                 proj_hbm, wt_ref, woutf_ref, pwwf_ref, dwf_ref,
                 hn_s, wt_s, wout_s, pww_s, dw_blk, stage, out_sems, send_sems, recv_sems):
    s = pl.program_id(0)
    i = pl.program_id(1)
    tm = TILE_IN
    x, y, c, _ = _place()
    sibling = (x, y, 1 - c)
    xn, yn, dg = (1 - x, y), (x, 1 - y), (1 - x, 1 - y)
    pick = lambda a, b: tuple(jnp.where(c == 1, p, q) for p, q in zip(a, b))
    on_from, on_to = pick(xn, yn), pick(yn, xn)

    def blocks(px, py, pc):
        k = 4 * px + 2 * py + pc
        rows = lambda ref, n: ref.at[pl.ds(pl.multiple_of(k * n, n), n), :]
        return (rows(wt_s, W_IN_BLK), rows(wout_s, W_OUT_BLK), rows(pww_s, PW_BLK), dw_blk.at[k])

    def copies(arrays, flow, block, to):
        refs = blocks(*block)
        return [pltpu.make_async_remote_copy(src_ref=refs[a], dst_ref=refs[a], send_sem=send_sems.at[a, flow],
                                             recv_sem=recv_sems.at[a, flow], device_id=to, device_id_type=MESH)
                for a in arrays]

    def start(arrays):
        for flow, to in ((F_SIB, sibling), (F_X, (*xn, c)), (F_Y, (*yn, c))):
            for cp in copies(arrays, flow, (x, y, c), to):
                cp.start()

    def direct(arrays):
        for flow, chip in ((F_X, xn), (F_Y, yn)):
            for cp in copies(arrays, flow, (*chip, c), sibling):
                cp.wait_recv()
        for cp in copies(arrays, F_ON, (*on_from, c), (*on_to, c)):
            cp.start()
        for flow, chip in ((F_PASS_X, xn), (F_PASS_Y, yn)):
            for cp in copies(arrays, flow, (*chip, c), sibling):
                cp.start()

    def diagonal(arrays):
        for cp in copies(arrays, F_ON, (*dg, c), sibling):
            cp.wait_recv()
        for cp in copies(arrays, F_PASS_D, (*dg, c), sibling):
            cp.start()

    def from_sibling(arrays, flow, chip):
        for cp in copies(arrays, flow, (*chip, 1 - c), sibling):
            cp.wait_recv()

    def finish(arrays):
        mine = (x, y, c)
        for flow, block in ((F_SIB, mine), (F_X, mine), (F_Y, mine), (F_ON, (*on_from, c)), (F_PASS_X, (*xn, c)),
                            (F_PASS_Y, (*yn, c)), (F_PASS_D, (*dg, c))):
            for cp in copies(arrays, flow, block, sibling):
                cp.wait_send()

    step = s * n_tiles + i
    par = step % 2
    first = step == 0

    @pl.when(first)
    def _():
        mine = blocks(x, y, c)
        mine[0][...] = wint_ref[...].astype(BF16)
        mine[1][...] = wout_ref[...].astype(BF16)
        mine[2][...] = pww_ref[...].astype(BF16)
        mine[3][...] = dw_ref[...]
        start(W_IN_ONLY)

    rows_i = pl.ds(pl.multiple_of(i * tm, tm), tm)

    @pl.when(s == 0)
    def _():
        xt = x_ref[0]
        r1 = lax.rsqrt(jnp.mean(xt * xt, axis=-1, keepdims=True) + RMS_EPS)
        hn_s[rows_i, :] = (xt * r1 * ng_ref[...]).astype(BF16)

    @pl.when(first)
    def _():
        from_sibling(W_IN_ONLY, F_SIB, (x, y))

    @pl.when((s == 1) & (i == 0))
    def _():
        direct(W_IN_ONLY)
        start(OTHERS)
        from_sibling(W_IN_ONLY, F_PASS_Y, yn)

    @pl.when((s == 2) & (i == 0))
    def _():
        from_sibling(W_IN_ONLY, F_PASS_X, xn)
        diagonal(W_IN_ONLY)
        direct(OTHERS)
        from_sibling(W_IN_ONLY, F_PASS_D, dg)

    tiles_per_seq = proj_hbm.shape[1] // tm

    def out_copy(unit, col0):
        dst = proj_hbm.at[i // tiles_per_seq, pl.ds(pl.multiple_of((i % tiles_per_seq) * tm, tm), tm),
                          pl.ds(pl.multiple_of(col0, LANES), SLAB)]
        return pltpu.make_async_copy(stage.at[par, :, pl.ds(unit * SLAB, SLAB)], dst, out_sems.at[par, unit])

    @pl.when(step >= 2)
    def _():
        out_copy(0, 0).wait()

    @pl.when(step >= 2 * n_tiles + 2)
    def _():
        out_copy(1, 0).wait()

    hn = hn_s[rows_i, :]
    for phase, chip in ((0, 2 * x + y), (1, 2 * x + (1 - y))):
        @pl.when(s == phase)
        def _():
            slab = wt_s[pl.ds(pl.multiple_of(chip * SLAB, SLAB), SLAB), :]
            stage[par, :, 0:SLAB] = lax.dot_general(hn, slab, NT_DIMS, preferred_element_type=F32)
            out_copy(0, chip * SLAB).start()

    @pl.when(s == 2)
    def _():
        col0 = (1 - x) * 2 * SLAB
        slab = wt_s[pl.ds(pl.multiple_of(col0, 2 * SLAB), 2 * SLAB), :]
        stage[par] = lax.dot_general(hn, slab, NT_DIMS, preferred_element_type=F32)
        out_copy(0, col0).start()
        out_copy(1, col0 + SLAB).start()

    @pl.when(step == 3 * n_tiles - 1)
    def _():
        out_copy(0, 0).wait()
        out_copy(1, 0).wait()
        pltpu.make_async_copy(stage.at[1 - par, :, pl.ds(0, SLAB)], proj_hbm.at[0, pl.ds(0, tm), pl.ds(0, SLAB)],
                              out_sems.at[1 - par, 0]).wait()
        pltpu.make_async_copy(stage.at[1 - par, :, pl.ds(SLAB, SLAB)], proj_hbm.at[0, pl.ds(0, tm), pl.ds(0, SLAB)],
                              out_sems.at[1 - par, 1]).wait()
        from_sibling(OTHERS, F_SIB, (x, y))
        from_sibling(OTHERS, F_PASS_X, xn)
        from_sibling(OTHERS, F_PASS_Y, yn)
        diagonal(OTHERS)
        from_sibling(OTHERS, F_PASS_D, dg)
        finish(W_IN_ONLY)
        finish(OTHERS)
        wt_ref[...] = wt_s[...]
        woutf_ref[...] = wout_s[...]
        pwwf_ref[...] = pww_s[...]
        for k in range(N_DEV):
            dwf_ref[:, k * DW_BLK:(k + 1) * DW_BLK] = dw_blk[k]


def _inproj(x, norm_g, w_in_t, w_out, pw_w, conv_dw):
    bl, s, _ = x.shape
    tm = TILE_IN
    nt = s // tm
    n_tiles = bl * nt

    def x_map(phase, i):
        i = jnp.where(phase == 0, i, n_tiles - 1)
        return (i // nt, i % nt, 0)

    dma = pltpu.SemaphoreType.DMA
    return pl.pallas_call(
        functools.partial(_inproj_body, n_tiles),
        name="inproj",
        grid=(3, n_tiles),
        in_specs=[pl.BlockSpec((1, tm, D_MODEL), x_map)] + [VMEM] * 5,
        out_specs=[pl.BlockSpec(memory_space=pl.ANY)] + [VMEM] * 4,
        out_shape=[jax.ShapeDtypeStruct((bl, s, D_IN), F32), jax.ShapeDtypeStruct((D_IN, D_MODEL), BF16),
                   jax.ShapeDtypeStruct((D_MODEL, D_MODEL), BF16), jax.ShapeDtypeStruct((W_MIX, W_MIX), BF16),
                   jax.ShapeDtypeStruct((CONV_WIDTH, W_MIX), F32)],
        scratch_shapes=[pltpu.VMEM((n_tiles * tm, D_MODEL), BF16), pltpu.VMEM((D_IN, D_MODEL), BF16),
                        pltpu.VMEM((D_MODEL, D_MODEL), BF16), pltpu.VMEM((W_MIX, W_MIX), BF16),
                        pltpu.VMEM((N_DEV, CONV_WIDTH, DW_BLK), F32), pltpu.VMEM((2, tm, 2 * SLAB), F32),
                        dma((2, 2)), dma((4, 7)), dma((4, 7))],
        compiler_params=pltpu.CompilerParams(dimension_semantics=("arbitrary", "arbitrary"), vmem_limit_bytes=VMEM_LIMIT),
    )(x, _row(norm_g), w_in_t, w_out, pw_w, conv_dw)


R_LOSS, R_FINAL_G, R_POOL, R_CONV, R_LN = range(5)
VEC_ROWS = 16
DW_ROWS = 32


def _ag_start(slots, send_sems, recv_sems):
    x, y, c, other_chips = _place()
    mine = slots(x, y, c)
    for k, to in enumerate([(x, y, 1 - c)] + [(*chip, c) for chip in other_chips]):
        for cp in _remote(mine, mine, send_sems, recv_sems, k, to):
            cp.start()


def _ag_forward(slots, send_sems, recv_sems):
    x, y, c, other_chips = _place()
    for j, chip in enumerate(other_chips):
        refs = slots(*chip, c)
        for cp in _remote(refs, refs, send_sems, recv_sems, 1 + j, (x, y, 1 - c)):
            cp.wait_recv()
        for cp in _remote(refs, refs, send_sems, recv_sems, 4 + j, (x, y, 1 - c)):
            cp.start()


def _ag_finish(slots, send_sems, recv_sems):
    x, y, c, other_chips = _place()
    sibling = (x, y, 1 - c)
    for k, block in enumerate([(x, y, 1 - c)] + [None] * 3 + [(*chip, 1 - c) for chip in other_chips]):
        if block is not None:
            refs = slots(*block)
            for cp in _remote(refs, refs, send_sems, recv_sems, k, sibling):
                cp.wait_recv()
    mine = slots(x, y, c)
    for k, block in enumerate([mine] * 4 + [slots(*chip, c) for chip in other_chips]):
        for cp in _remote(block, block, send_sems, recv_sems, k, sibling):
            cp.wait_send()


def _rs_send_a(to_sibling, landed, send_a, recv_a):
    x, y, c, _ = _place()
    for chip in range(4):
        for cp in _remote(to_sibling(2 * chip + (1 - c)), landed(chip), send_a, recv_a, chip, (x, y, 1 - c)):
            cp.start()


def _rs_chip_sum(mine, landed, send_a, recv_a, chip):
    x, y, c, _ = _place()
    for cp in _remote(landed(chip), landed(chip), send_a, recv_a, chip, (x, y, 1 - c)):
        cp.wait_recv()
    return [m[...] + r[...] for m, r in zip(mine(chip), landed(chip))]


def _rs_send_b(mine, landed, staged, arrived, send_a, recv_a, send_b, recv_b):
    x, y, c, other_chips = _place()
    for j, (px, py) in enumerate(other_chips):
        for o, v in zip(staged(j), _rs_chip_sum(mine, landed, send_a, recv_a, 2 * px + py)):
            o[...] = v.astype(BF16)
        for cp in _remote(staged(j), arrived(j), send_b, recv_b, j, (px, py, c)):
            cp.start()


def _rs_finish(to_sibling, mine, landed, staged, arrived, send_a, recv_a, send_b, recv_b):
    x, y, c, _ = _place()
    sibling = (x, y, 1 - c)
    total = _rs_chip_sum(mine, landed, send_a, recv_a, 2 * x + y)
    for j in range(3):
        for cp in _remote(arrived(j), arrived(j), send_b, recv_b, j, sibling):
            cp.wait_recv()
        total = [t + g[...].astype(F32) for t, g in zip(total, arrived(j))]
    for chip in range(4):
        for cp in _remote(to_sibling(2 * chip + (1 - c)), landed(chip), send_a, recv_a, chip, sibling):
            cp.wait_send()
    for j in range(3):
        for cp in _remote(staged(j), arrived(j), send_b, recv_b, j, sibling):
            cp.wait_send()
    return total


EXCHANGE_STEP = 2


def _bwd2a_body(n_steps, x_ref, dproj_ref, ng_ref, gwout_ref, gpww_ref, loss_ref, gfg_ref, gvec_ref, gdw_ref, gpoolw_ref,
                stwin_ref, ownwin_ref, owout_ref, opww_ref, ovec_ref, odw_ref, opoolw_ref,
                acc, slab4, a_win, a_wout, a_pww, sb_wout, sb_pww, rb_wout, rb_pww, all_vec, all_dw, all_poolw,
                send_a, recv_a, send_b, recv_b, send_g, recv_g, send_w, recv_w):
    step = pl.program_id(0) * pl.num_programs(1) + pl.program_id(1)
    x, y, c, other_chips = _place()

    def rows(ref, k, n):
        return ref.at[pl.ds(pl.multiple_of(k * n, n), n), :]

    to_sibling = lambda k: (rows(gwout_ref, k, W_OUT_BLK), rows(gpww_ref, k, PW_BLK))
    mine = lambda chip: to_sibling(2 * chip + c)
    landed = lambda chip: (a_wout.at[chip], a_pww.at[chip])
    staged = lambda j: (sb_wout.at[j], sb_pww.at[j])
    arrived = lambda j: (rb_wout.at[j], rb_pww.at[j])

    def slots(px, py, pc):
        k = 4 * px + 2 * py + pc
        return (all_vec.at[k], all_dw.at[k], all_poolw.at[k])

    @pl.when(step == 0)
    def _():
        acc[...] = jnp.zeros_like(acc)
        sum8 = lambda v: jnp.sum(v, axis=0, keepdims=True)
        vec, dwp, pwp = slots(x, y, c)
        vec[...] = jnp.zeros((VEC_ROWS, D_MODEL), F32)
        vec[R_LOSS:R_LOSS + 1, :] = jnp.broadcast_to(loss_ref[0:1, 0:1], (1, D_MODEL))
        vec[R_FINAL_G:R_FINAL_G + 1, :] = sum8(gfg_ref[...])
        for row, (lo, hi) in ((R_POOL, (V_POOL_B, V_POOL_SCALE)), (R_CONV, (V_CONV_B, V_LN_G)), (R_LN, (V_LN_B, V_PW_B))):
            vec[row:row + 1, 0:W_MIX] = sum8(gvec_ref[lo])
            vec[row:row + 1, W_MIX:] = sum8(gvec_ref[hi])
        dwp[0:CONV_WIDTH, :] = jnp.sum(gdw_ref[...], axis=1)
        dwp[CONV_WIDTH:, :] = jnp.zeros((DW_ROWS - CONV_WIDTH, W_MIX), F32)
        pwp[...] = gpoolw_ref[...].reshape(len(POOL_WINDOWS) * POOL_GC, POOL_GC)
        _rs_send_a(to_sibling, landed, send_a, recv_a)
        _ag_start(slots, send_g, recv_g)

    x_t = x_ref[0]
    r1 = lax.rsqrt(jnp.mean(x_t * x_t, axis=-1, keepdims=True) + RMS_EPS)
    hn = (x_t * r1 * ng_ref[...]).astype(BF16)
    acc[...] += lax.dot_general(hn, dproj_ref[0], TN_DIMS, preferred_element_type=F32)

    @pl.when(step == EXCHANGE_STEP)
    def _():
        _rs_send_b(mine, landed, staged, arrived, send_a, recv_a, send_b, recv_b)
        _ag_forward(slots, send_g, recv_g)

    @pl.when(step == n_steps - 1)
    def _():
        slab = lambda k: acc[:, k * W_IN_BLK:(k + 1) * W_IN_BLK]
        w_copy = lambda chip: _remote((slab4.at[chip],), (a_win.at[chip],), send_w, recv_w, chip, (x, y, 1 - c))[0]
        for cc in range(2):
            @pl.when(c == cc)
            def _():
                for chip in range(4):
                    slab4[chip] = slab(2 * chip + (1 - cc))
        for chip in range(4):
            w_copy(chip).start()

        owout_ref[...], opww_ref[...] = _rs_finish(to_sibling, mine, landed, staged, arrived, send_a, recv_a, send_b, recv_b)
        _ag_finish(slots, send_g, recv_g)
        for o, a in ((ovec_ref, all_vec), (odw_ref, all_dw), (opoolw_ref, all_poolw)):
            s = a[0]
            for k in range(1, N_DEV):
                s = s + a[k]
            o[...] = s

        for chip in range(4):
            w_copy(chip).wait()
        for cc in range(2):
            @pl.when(c == cc)
            def _():
                for chip in range(4):
                    slab4[chip] = slab(2 * chip + cc) + a_win[chip]
        for j, (px, py) in enumerate(other_chips):
            stwin_ref[j] = slab4[2 * px + py].astype(BF16)
        ownwin_ref[...] = slab4[2 * x + y]


def _bwd2a(x, dproj, norm_g, gwout, gpww, loss8, gfg8, gvec, gdw8, gpoolw):
    bl, s, _ = x.shape
    tm = TILE_M
    nt = s // tm
    tok = lambda n: pl.BlockSpec((1, tm, n), lambda b, i: (b, i, 0))
    n_pw = len(POOL_WINDOWS) * POOL_GC
    win = (D_MODEL, W_IN_BLK)
    blk = ((W_OUT_BLK, D_MODEL), (PW_BLK, W_MIX))
    rep = ((VEC_ROWS, D_MODEL), (DW_ROWS, W_MIX), (n_pw, POOL_GC))
    dma = pltpu.SemaphoreType.DMA
    return pl.pallas_call(
        functools.partial(_bwd2a_body, bl * nt),
        name="bwd2a",
        grid=(bl, nt),
        in_specs=[tok(D_MODEL), tok(D_IN)] + [VMEM] * 8,
        out_specs=[VMEM] * 7,
        out_shape=[jax.ShapeDtypeStruct((3,) + win, BF16), jax.ShapeDtypeStruct(win, F32)]
        + [jax.ShapeDtypeStruct(sh, F32) for sh in blk + rep],
        scratch_shapes=[pltpu.VMEM((D_MODEL, D_IN), F32), pltpu.VMEM((4,) + win, F32), pltpu.VMEM((4,) + win, F32)]
        + [pltpu.VMEM((4,) + sh, F32) for sh in blk]
        + [pltpu.VMEM((3,) + sh, BF16) for sh in blk] * 2
        + [pltpu.VMEM((N_DEV,) + sh, F32) for sh in rep]
        + [dma((2, 4)), dma((2, 4)), dma((2, 3)), dma((2, 3)), dma((3, N_PEER)), dma((3, N_PEER)), dma((1, 4)), dma((1, 4))],
        compiler_params=pltpu.CompilerParams(dimension_semantics=("arbitrary", "arbitrary"), vmem_limit_bytes=VMEM_LIMIT),
    )(x, dproj, _row(norm_g), gwout, gpww, loss8, gfg8, gvec, gdw8, gpoolw)


def _bwd2b_body(n_steps, x_ref, dh_ref, dproj_ref, ng_ref, wint_ref, stwin_hbm, ownwin_ref,
                gx_ref, owin_ref, ong_ref,
                gng, rb_win, all_ng, send_b, recv_b, send_n, recv_n):
    step = pl.program_id(0) * pl.num_programs(1) + pl.program_id(1)
    x, y, c, other_chips = _place()
    sibling = (x, y, 1 - c)
    b_copy = lambda j: pltpu.make_async_remote_copy(src_ref=stwin_hbm.at[j], dst_ref=rb_win.at[j], send_sem=send_b.at[j],
                                                    recv_sem=recv_b.at[j], device_id=(*other_chips[j], c), device_id_type=MESH)

    @pl.when(step == 0)
    def _():
        gng[...] = jnp.zeros_like(gng)
        for j in range(3):
            b_copy(j).start()

    x_t = x_ref[0]
    r1 = lax.rsqrt(jnp.mean(x_t * x_t, axis=-1, keepdims=True) + RMS_EPS)
    xr = x_t * r1
    dhn = jnp.dot(dproj_ref[0], wint_ref[...], preferred_element_type=F32)
    gng[...] += _colsum8(dhn * xr)
    gy = dhn * ng_ref[...]
    gx_ref[0] = dh_ref[0] + r1 * (gy - xr * jnp.mean(gy * xr, axis=-1, keepdims=True))

    k_me = 4 * x + 2 * y + c
    peers = [sibling] + [(*chip, c) for chip in other_chips] + [(*chip, 1 - c) for chip in other_chips]
    ng_copy = lambda r: pltpu.make_async_remote_copy(src_ref=all_ng.at[k_me], dst_ref=all_ng.at[k_me], send_sem=send_n.at[r],
                                                     recv_sem=recv_n.at[r], device_id=peers[r], device_id_type=MESH)

    @pl.when(step == n_steps - 1)
    def _():
        all_ng[k_me] = gng[...]
        for r in range(N_PEER):
            ng_copy(r).start()
        total = ownwin_ref[...]
        for j in range(3):
            b_copy(j).wait()
            total = total + rb_win[j].astype(F32)
        owin_ref[...] = total
        for r in range(N_PEER):
            ng_copy(r).wait()
        s = all_ng[0]
        for k in range(1, N_DEV):
            s = s + all_ng[k]
        ong_ref[...] = jnp.broadcast_to(jnp.sum(s, axis=0, keepdims=True), ong_ref.shape)


def _bwd2b(x, dh, dproj, norm_g, w_in_t, st_win, own_win):
    bl, s, _ = x.shape
    tm = TILE_M
    nt = s // tm
    tok = lambda n: pl.BlockSpec((1, tm, n), lambda b, i: (b, i, 0))
    blk = (D_MODEL, W_IN_BLK)
    dma = pltpu.SemaphoreType.DMA
    return pl.pallas_call(
        functools.partial(_bwd2b_body, bl * nt),
        name="bwd2b",
        grid=(bl, nt),
        in_specs=[tok(D_MODEL), tok(D_MODEL), tok(D_IN), VMEM, VMEM, pl.BlockSpec(memory_space=pl.ANY), VMEM],
        out_specs=[tok(D_MODEL), VMEM, VMEM],
        out_shape=[jax.ShapeDtypeStruct((bl, s, D_MODEL), F32), jax.ShapeDtypeStruct(blk, F32),
                   jax.ShapeDtypeStruct((SUBLANES, D_MODEL), F32)],
        scratch_shapes=[pltpu.VMEM((SUBLANES, D_MODEL), F32), pltpu.VMEM((3,) + blk, BF16),
                        pltpu.VMEM((N_DEV, SUBLANES, D_MODEL), F32), dma((3,)), dma((3,)), dma((N_PEER,)), dma((N_PEER,))],
        compiler_params=pltpu.CompilerParams(dimension_semantics=("arbitrary", "arbitrary"), vmem_limit_bytes=VMEM_LIMIT),
    )(x, dh, dproj, _row(norm_g), w_in_t, st_win, own_win)


ADAM_LR = 0.001
ADAM_B1 = 0.9
ADAM_B2 = 0.999
ADAM_EPS = 1e-08
ADAM_WD = 0.01
ADAM_STEP = 10


def _adamw_body(n, *refs):
    ws, gs, ms, vs = (refs[k * n:(k + 1) * n] for k in range(4))
    ds, m2s, v2s = (refs[(4 + k) * n:(5 + k) * n] for k in range(3))
    for w_ref, g_ref, m_ref, v_ref, d_ref, m2_ref, v2_ref in zip(ws, gs, ms, vs, ds, m2s, v2s):
        g = g_ref[...]
        m = ADAM_B1 * m_ref[...] + (1.0 - ADAM_B1) * g
        v = ADAM_B2 * v_ref[...] + (1.0 - ADAM_B2) * (g * g)
        m_hat = m / (1.0 - ADAM_B1 ** ADAM_STEP)
        v_hat = v / (1.0 - ADAM_B2 ** ADAM_STEP)
        d_ref[...] = -ADAM_LR * (m_hat / (jnp.sqrt(v_hat) + ADAM_EPS) + ADAM_WD * w_ref[...])
        m2_ref[...] = m
        v2_ref[...] = v


def _adamw(ws, gs, ms, vs):
    n = len(ws)
    shapes = [w.shape for w in ws]
    flat = lambda a: a.reshape(1, -1) if a.ndim == 1 else a
    args = [flat(a) for a in (*ws, *gs, *ms, *vs)]
    outs = pl.pallas_call(
        functools.partial(_adamw_body, n),
        name="adamw",
        in_specs=[VMEM] * (4 * n),
        out_specs=[VMEM] * (3 * n),
        out_shape=[jax.ShapeDtypeStruct(a.shape, F32) for a in args[:n]] * 3,
        compiler_params=pltpu.CompilerParams(vmem_limit_bytes=VMEM_LIMIT),
    )(*args)
    outs = [o.reshape(shapes[k % n]) for k, o in enumerate(outs)]
    return outs[:n], outs[n:2 * n], outs[2 * n:]


def kernel(x, norm_g, w_in, pool_w, pool_b, pool_scale, conv_dw, conv_b, ln_g, ln_b, pw_w, pw_b, w_out, final_g, loss_target, m_norm_g, m_w_in, m_pool_w, m_pool_b, m_pool_scale, m_conv_dw, m_conv_b, m_ln_g, m_ln_b, m_pw_w, m_pw_b, m_w_out, m_final_g, v_norm_g, v_w_in, v_pool_w, v_pool_b, v_pool_scale, v_conv_dw, v_conv_b, v_ln_g, v_ln_b, v_pw_w, v_pw_b, v_w_out, v_final_g):
    proj, w_in_t, w_out_b, pw_w_b, conv_dw_f = _inproj(x, norm_g, w_in.T, w_out, pw_w, conv_dw)
    h1, z, ob, dpool, dh, loss8, gfg8 = _fwd(x, loss_target, proj, pool_w, pool_b, pool_scale, conv_dw_f,
                                             conv_b, ln_g, ln_b, pw_w_b, pw_b, w_out_b, final_g)
    dproj, gwout, gpww, gpoolw, gvec, gdw8 = _bwd1(dh, proj, h1, z, ob, dpool, w_out_b, pw_w_b, pool_w, pool_scale,
                                                   conv_dw_f, ln_g, ln_b)
    st_win, own_win, g_w_out, g_pw_w, vec, dw_sum, poolw_sum = _bwd2a(x, dproj, norm_g, gwout, gpww, loss8, gfg8, gvec, gdw8,
                                                                      gpoolw)
    grad_x, g_w_in, g_norm_g = _bwd2b(x, dh, dproj, norm_g, w_in_t, st_win, own_win)

    me = 4 * lax.axis_index("x") + 2 * lax.axis_index("y") + lax.axis_index("c")
    n_g = len(POOL_WINDOWS)
    grads = dict(
        norm_g=g_norm_g[0], w_in=g_w_in, pool_w=poolw_sum.reshape(n_g, POOL_GC, POOL_GC),
        pool_b=vec[R_POOL, :W_MIX].reshape(n_g, POOL_GC), pool_scale=vec[R_POOL, W_MIX:],
        conv_dw=lax.dynamic_slice(dw_sum, (0, me * DW_BLK), (CONV_WIDTH, DW_BLK)), conv_b=vec[R_CONV, :W_MIX],
        ln_g=vec[R_CONV, W_MIX:], ln_b=vec[R_LN, :W_MIX], pw_w=g_pw_w, pw_b=vec[R_LN, W_MIX:], w_out=g_w_out,
        final_g=vec[R_FINAL_G])
    weights = dict(norm_g=norm_g, w_in=w_in, pool_w=pool_w, pool_b=pool_b, pool_scale=pool_scale, conv_dw=conv_dw, conv_b=conv_b,
                   ln_g=ln_g, ln_b=ln_b, pw_w=pw_w, pw_b=pw_b, w_out=w_out, final_g=final_g)
    m_in = dict(norm_g=m_norm_g, w_in=m_w_in, pool_w=m_pool_w, pool_b=m_pool_b, pool_scale=m_pool_scale, conv_dw=m_conv_dw,
                conv_b=m_conv_b, ln_g=m_ln_g, ln_b=m_ln_b, pw_w=m_pw_w, pw_b=m_pw_b, w_out=m_w_out, final_g=m_final_g)
    v_in = dict(norm_g=v_norm_g, w_in=v_w_in, pool_w=v_pool_w, pool_b=v_pool_b, pool_scale=v_pool_scale, conv_dw=v_conv_dw,
                conv_b=v_conv_b, ln_g=v_ln_g, ln_b=v_ln_b, pw_w=v_pw_w, pw_b=v_pw_b, w_out=v_w_out, final_g=v_final_g)
    names = list(weights)
    delta, new_m, new_v = _adamw([weights[n] for n in names], [grads[n] for n in names], [m_in[n] for n in names],
                                 [v_in[n] for n in names])
    return (vec[R_LOSS, 0], grad_x, *[grads[n] for n in names], *delta, *new_m, *new_v)
```

```python
import functools

import jax
import jax.numpy as jnp
from jax import lax
from jax.experimental import pallas as pl
from jax.experimental.pallas import tpu as pltpu

F32 = jnp.float32
BF16 = jnp.bfloat16

D_MODEL = 1024
W_MIX = 512
D_IN = 5 * W_MIX
POOL_WINDOWS = (2, 4, 8, 16)
POOL_GC = 128
CONV_WIDTH = 31
RMS_EPS = 1e-6
LN_EPS = 1e-5
N_DEV = 8

LANES = 128
SUBLANES = 8
TILE_M = 512
TILE_IN = 1024
POOL_HALO = 16
CONV_HALO = 32
CONV_ROWS = 128
VMEM_LIMIT = 56 * 1024 * 1024

VMEM = pl.BlockSpec(memory_space=pltpu.VMEM)


def _silu(v):
    return v * jax.nn.sigmoid(v)


def _colsum8(v):
    m, n = v.shape
    return jnp.sum(v.reshape(m // SUBLANES, SUBLANES, n), axis=0)


def _row_index(shape, t0):
    return lax.broadcasted_iota(jnp.int32, shape, 0) + t0


def _conv_taps():
    taps = {s: [] for s in range(SUBLANES)}
    for k in range(CONV_WIDTH):
        e = k + CONV_HALO - (CONV_WIDTH - 1)
        taps[e % SUBLANES].append((e // SUBLANES, k))
    return taps


def _fwd_body(x_ref, tgt_ref, proj_ref, poolw_ref, poolb_ref, pools_ref, dw_ref, cb_ref, lg_ref, lb_ref,
              pww_ref, pwb_ref, wout_ref, fg_ref,
              h1_ref, z_ref, ob_ref, dpool_ref, dh_ref, loss_ref, gfg_ref,
              uext, hext, y_s):
    b = pl.program_id(0)
    i = pl.program_id(1)
    tm = TILE_M

    @pl.when(i == 0)
    def _():
        uext[0:POOL_HALO, :] = jnp.zeros((POOL_HALO, W_MIX), F32)
        hext[0:CONV_HALO, :] = jnp.zeros((CONV_HALO, W_MIX), F32)
        hext[CONV_HALO + tm:, :] = jnp.zeros((SUBLANES, W_MIX), F32)

    @pl.when((b == 0) & (i == 0))
    def _():
        loss_ref[...] = jnp.zeros_like(loss_ref)
        gfg_ref[...] = jnp.zeros_like(gfg_ref)

    uext[POOL_HALO:, :] = proj_ref[0, :, 0:W_MIX]
    t = _row_index((tm, POOL_GC), i * tm)
    for g, w in enumerate(POOL_WINDOWS):
        lanes = slice(g * POOL_GC, (g + 1) * POOL_GC)
        e = uext[:, lanes]
        s = e
        sh = 1
        while sh < w:
            s = s + pltpu.roll(s, sh, 0)
            sh *= 2
        inv_cnt = 1.0 / jnp.minimum(t + 1, w).astype(F32)
        d = (s[POOL_HALO:] * inv_cnt - e[POOL_HALO:]).astype(BF16)
        dpool_ref[0, :, lanes] = d
        z = jnp.dot(d, poolw_ref[g].astype(BF16), preferred_element_type=F32) + poolb_ref[:, lanes]
        z_ref[0, :, lanes] = z
        a_gate = proj_ref[0, :, W_MIX + g * POOL_GC:W_MIX + (g + 1) * POOL_GC]
        y_s[:, lanes] = (z * pools_ref[:, lanes] * _silu(a_gate)).astype(BF16)

    hext[CONV_HALO:CONV_HALO + tm, :] = proj_ref[0, :, 2 * W_MIX:3 * W_MIX] * jax.nn.sigmoid(proj_ref[0, :, 3 * W_MIX:4 * W_MIX])
    taps = _conv_taps()
    for c0 in range(0, tm, CONV_ROWS):
        for l0 in range(0, W_MIX, LANES):
            lanes = slice(l0, l0 + LANES)
            acc = jnp.zeros((CONV_ROWS, LANES), F32) + cb_ref[:, lanes]
            for s in range(SUBLANES):
                part = jnp.zeros((CONV_ROWS + SUBLANES, LANES), F32)
                for q, k in taps[s]:
                    r0 = c0 + SUBLANES * q
                    part = part + hext[r0:r0 + CONV_ROWS + SUBLANES, lanes] * dw_ref[k:k + 1, lanes]
                acc = acc + part[s:s + CONV_ROWS]
            h1_ref[0, c0:c0 + CONV_ROWS, lanes] = acc
    h1 = h1_ref[0]
    mu = jnp.mean(h1, axis=-1, keepdims=True)
    xc = h1 - mu
    rstd = lax.rsqrt(jnp.mean(xc * xc, axis=-1, keepdims=True) + LN_EPS)
    h3 = _silu(xc * rstd * lg_ref[...] + lb_ref[...])
    ob = jnp.dot(h3.astype(BF16), pww_ref[...], preferred_element_type=F32) + pwb_ref[...]
    ob_ref[0] = ob
    y_s[:, W_MIX:] = (ob * _silu(proj_ref[0, :, 4 * W_MIX:5 * W_MIX])).astype(BF16)

    h = x_ref[0] + jnp.dot(y_s[...], wout_ref[...], preferred_element_type=F32)
    r2 = lax.rsqrt(jnp.mean(h * h, axis=-1, keepdims=True) + RMS_EPS)
    hr = h * r2
    err = hr * fg_ref[...] - tgt_ref[0]
    loss_ref[...] += jnp.sum(err * err) * (0.5 / D_MODEL)
    dout = err * (1.0 / D_MODEL)
    gfg_ref[...] += _colsum8(dout * hr)
    gy = dout * fg_ref[...]
    dh_ref[0] = r2 * (gy - hr * jnp.mean(gy * hr, axis=-1, keepdims=True))

    uext[0:POOL_HALO, :] = uext[tm:tm + POOL_HALO, :]
    hext[0:CONV_HALO, :] = hext[tm:tm + CONV_HALO, :]


def _row(v):
    return v.reshape(1, -1)


def _fwd(x, tgt, proj, pool_w, pool_b, pool_scale, conv_dw_f, conv_b, ln_g, ln_b, pw_w_b, pw_b, w_out_b, final_g):
    bl, s, _ = x.shape
    tm = TILE_M
    tok = lambda n: pl.BlockSpec((1, tm, n), lambda b, i: (b, i, 0))
    acc = lambda n: pl.BlockSpec((SUBLANES, n), lambda b, i: (0, 0))
    act = lambda n: jax.ShapeDtypeStruct((bl, s, n), F32)
    return pl.pallas_call(
        functools.partial(_fwd_body),
        name="fwd",
        grid=(bl, s // tm),
        in_specs=[tok(D_MODEL), tok(D_MODEL), tok(D_IN)] + [VMEM] * 11,
        out_specs=[tok(W_MIX), tok(W_MIX), tok(W_MIX), tok(W_MIX), tok(D_MODEL), acc(LANES), acc(D_MODEL)],
        out_shape=[act(W_MIX), act(W_MIX), act(W_MIX), jax.ShapeDtypeStruct((bl, s, W_MIX), BF16), act(D_MODEL),
                   jax.ShapeDtypeStruct((SUBLANES, LANES), F32), jax.ShapeDtypeStruct((SUBLANES, D_MODEL), F32)],
        scratch_shapes=[pltpu.VMEM((POOL_HALO + tm, W_MIX), F32), pltpu.VMEM((CONV_HALO + tm + SUBLANES, W_MIX), F32),
                        pltpu.VMEM((tm, D_MODEL), BF16)],
        compiler_params=pltpu.CompilerParams(dimension_semantics=("arbitrary", "arbitrary"), vmem_limit_bytes=VMEM_LIMIT),
    )(x, tgt, proj, pool_w, pool_b.reshape(1, W_MIX), _row(pool_scale), conv_dw_f, _row(conv_b),
      _row(ln_g), _row(ln_b), pw_w_b, _row(pw_b), w_out_b, _row(final_g))


NT_DIMS = (((1,), (1,)), ((), ()))
TN_DIMS = (((0,), (0,)), ((), ()))
V_POOL_B, V_POOL_SCALE, V_CONV_B, V_LN_G, V_LN_B, V_PW_B = range(6)


def _bwd_taps():
    taps = {s: [] for s in range(SUBLANES)}
    for k in range(CONV_WIDTH):
        e = CONV_WIDTH - 1 - k
        taps[e % SUBLANES].append((e // SUBLANES, k))
    return taps


def _bwd1_body(n_tiles, dh_ref, proj_ref, h1_ref, z_ref, ob_ref, dpool_ref, wout_ref, pww_ref, poolw_ref, pools_ref, dw_ref,
               lg_ref, lb_ref,
               dproj_ref, gwout_ref, gpww_ref, gpoolw_ref, gvec_ref, gdw_ref,
               vext, gext, gsh, y_s):
    b = pl.program_id(0)
    i = pl.program_id(1)
    tm = TILE_M
    t0 = (n_tiles - 1 - i) * tm

    @pl.when(i == 0)
    def _():
        vext[tm:, :] = jnp.zeros((POOL_HALO, W_MIX), F32)
        gext[tm:, :] = jnp.zeros((CONV_HALO, W_MIX), F32)

    @pl.when((b == 0) & (i == 0))
    def _():
        gwout_ref[...] = jnp.zeros_like(gwout_ref)
        gpww_ref[...] = jnp.zeros_like(gpww_ref)
        gpoolw_ref[...] = jnp.zeros_like(gpoolw_ref)
        gvec_ref[...] = jnp.zeros_like(gvec_ref)
        gdw_ref[...] = jnp.zeros_like(gdw_ref)

    dhb = dh_ref[0].astype(BF16)
    dy_a = lax.dot_general(dhb, wout_ref[0:W_MIX, :], NT_DIMS, preferred_element_type=F32)
    dy_b = lax.dot_general(dhb, wout_ref[W_MIX:, :], NT_DIMS, preferred_element_type=F32)

    a_gate = proj_ref[0, :, W_MIX:2 * W_MIX]
    sig_a = jax.nn.sigmoid(a_gate)
    silu_a = a_gate * sig_a
    z = z_ref[0]
    out_a = z * pools_ref[...]
    y_s[:, 0:W_MIX] = (out_a * silu_a).astype(BF16)
    d_out_a = dy_a * silu_a
    d_ag = dy_a * out_a * (sig_a * (1.0 + a_gate * (1.0 - sig_a)))
    dproj_ref[0, :, W_MIX:2 * W_MIX] = d_ag.astype(BF16)
    gvec_ref[V_POOL_SCALE] += _colsum8(d_out_a * z)
    dz = d_out_a * pools_ref[...]
    gvec_ref[V_POOL_B] += _colsum8(dz)
    t = _row_index((tm, POOL_GC), t0)
    n_ext = tm + POOL_HALO
    for g, w in enumerate(POOL_WINDOWS):
        lanes = slice(g * POOL_GC, (g + 1) * POOL_GC)
        dzg = dz[:, lanes].astype(BF16)
        gpoolw_ref[g] += lax.dot_general(dpool_ref[0, :, lanes], dzg, TN_DIMS, preferred_element_type=F32)
        dd = lax.dot_general(dzg, poolw_ref[g].astype(BF16), NT_DIMS, preferred_element_type=F32)
        vext[0:tm, lanes] = dd * (1.0 / jnp.minimum(t + 1, w).astype(F32))
        s = vext[:, lanes]
        sh = 1
        while sh < w:
            s = s + pltpu.roll(s, n_ext - sh, 0)
            sh *= 2
        dproj_ref[0, :, lanes] = (s[0:tm] - dd).astype(BF16)
    vext[tm:, :] = vext[0:POOL_HALO, :]

    b_gate = proj_ref[0, :, 4 * W_MIX:5 * W_MIX]
    sig_b = jax.nn.sigmoid(b_gate)
    silu_b = b_gate * sig_b
    ob = ob_ref[0]
    y_s[:, W_MIX:] = (ob * silu_b).astype(BF16)
    d_ob = dy_b * silu_b
    dproj_ref[0, :, 4 * W_MIX:5 * W_MIX] = (dy_b * ob * (sig_b * (1.0 + b_gate * (1.0 - sig_b)))).astype(BF16)
    gvec_ref[V_PW_B] += _colsum8(d_ob)
    gwout_ref[...] += lax.dot_general(y_s[...], dhb, TN_DIMS, preferred_element_type=F32)

    h1 = h1_ref[0]
    mu = jnp.mean(h1, axis=-1, keepdims=True)
    xc = h1 - mu
    rstd = lax.rsqrt(jnp.mean(xc * xc, axis=-1, keepdims=True) + LN_EPS)
    xhat = xc * rstd
    h2 = xhat * lg_ref[...] + lb_ref[...]
    sig2 = jax.nn.sigmoid(h2)
    d_obb = d_ob.astype(BF16)
    gpww_ref[...] += lax.dot_general((h2 * sig2).astype(BF16), d_obb, TN_DIMS, preferred_element_type=F32)
    dh2 = lax.dot_general(d_obb, pww_ref[...], NT_DIMS, preferred_element_type=F32) * (sig2 * (1.0 + h2 * (1.0 - sig2)))
    gvec_ref[V_LN_G] += _colsum8(dh2 * xhat)
    gvec_ref[V_LN_B] += _colsum8(dh2)
    dxh = dh2 * lg_ref[...]
    dh1 = rstd * (dxh - jnp.mean(dxh, axis=-1, keepdims=True) - xhat * jnp.mean(dxh * xhat, axis=-1, keepdims=True))
    gvec_ref[V_CONV_B] += _colsum8(dh1)
    gext[0:tm, :] = dh1

    taps = _bwd_taps()
    win = CONV_ROWS + CONV_HALO - SUBLANES
    for c0 in range(0, tm, CONV_ROWS):
        for l0 in range(0, W_MIX, LANES):
            lanes = slice(l0, l0 + LANES)
            b_val = proj_ref[0, c0:c0 + CONV_ROWS, 2 * W_MIX + l0:2 * W_MIX + l0 + LANES]
            sg = jax.nn.sigmoid(proj_ref[0, c0:c0 + CONV_ROWS, 3 * W_MIX + l0:3 * W_MIX + l0 + LANES])
            h0c = b_val * sg
            acc = jnp.zeros((CONV_ROWS, LANES), F32)
            for s in range(SUBLANES):
                gsh[...] = gext[c0 + s:c0 + s + win, lanes]
                for q, k in taps[s]:
                    gq = gsh[SUBLANES * q:SUBLANES * q + CONV_ROWS, :]
                    acc = acc + gq * dw_ref[k:k + 1, lanes]
                    gdw_ref[k, :, lanes] += _colsum8(h0c * gq)
            d_bval = acc * sg
            dproj_ref[0, c0:c0 + CONV_ROWS, 2 * W_MIX + l0:2 * W_MIX + l0 + LANES] = d_bval.astype(BF16)
            dproj_ref[0, c0:c0 + CONV_ROWS, 3 * W_MIX + l0:3 * W_MIX + l0 + LANES] = (
                d_bval * b_val * (1.0 - sg)).astype(BF16)
    gext[tm:, :] = gext[0:CONV_HALO, :]


def _bwd1(dh, proj, h1, z, ob, dpool, w_out_b, pw_w_b, pool_w, pool_scale, conv_dw_f, ln_g, ln_b):
    bl, s, _ = dh.shape
    tm = TILE_M
    nt = s // tm
    tok = lambda n: pl.BlockSpec((1, tm, n), lambda b, i: (b, nt - 1 - i, 0))
    res = lambda shape: pl.BlockSpec(shape, lambda b, i: (0,) * len(shape))
    out_shapes = [(D_MODEL, D_MODEL), (W_MIX, W_MIX), (len(POOL_WINDOWS), POOL_GC, POOL_GC), (6, SUBLANES, W_MIX),
                  (CONV_WIDTH, SUBLANES, W_MIX)]
    return pl.pallas_call(
        functools.partial(_bwd1_body, nt),
        name="bwd1",
        grid=(bl, nt),
        in_specs=[tok(D_MODEL), tok(D_IN), tok(W_MIX), tok(W_MIX), tok(W_MIX), tok(W_MIX)] + [VMEM] * 7,
        out_specs=[tok(D_IN)] + [res(sh) for sh in out_shapes],
        out_shape=[jax.ShapeDtypeStruct((bl, s, D_IN), BF16)] + [jax.ShapeDtypeStruct(sh, F32) for sh in out_shapes],
        scratch_shapes=[pltpu.VMEM((tm + POOL_HALO, W_MIX), F32), pltpu.VMEM((tm + CONV_HALO, W_MIX), F32),
                        pltpu.VMEM((CONV_ROWS + CONV_HALO - SUBLANES, LANES), F32), pltpu.VMEM((tm, D_MODEL), BF16)],
        compiler_params=pltpu.CompilerParams(dimension_semantics=("arbitrary", "arbitrary"), vmem_limit_bytes=VMEM_LIMIT),
    )(dh, proj, h1, z, ob, dpool, w_out_b, pw_w_b, pool_w, _row(pool_scale), conv_dw_f, _row(ln_g), _row(ln_b))


MESH = pl.DeviceIdType.MESH
W_IN_BLK = D_IN // N_DEV
W_OUT_BLK = D_MODEL // N_DEV
PW_BLK = W_MIX // N_DEV
DW_BLK = W_MIX // N_DEV
N_PEER = N_DEV - 1


SLAB = 2 * W_IN_BLK
F_SIB, F_X, F_Y, F_ON, F_PASS_X, F_PASS_Y, F_PASS_D = range(7)
W_IN_ONLY, OTHERS = (0,), (1, 2, 3)


def _inproj_body(n_tiles, x_ref, ng_ref, wint_ref, wout_ref, pww_ref, dw_ref,
                 proj_hbm, wt_ref, woutf_ref, pwwf_ref, dwf_ref,
                 hn_s, wt_s, wout_s, pww_s, dw_blk, stage, out_sems, send_sems, recv_sems):
    s = pl.program_id(0)
    i = pl.program_id(1)
    tm = TILE_IN
    x, y, c = lax.axis_index("x"), lax.axis_index("y"), lax.axis_index("c")
    sibling = (x, y, 1 - c)
    xn, yn, dg = (1 - x, y), (x, 1 - y), (1 - x, 1 - y)
    pick = lambda a, b: tuple(jnp.where(c == 1, p, q) for p, q in zip(a, b))
    on_from, on_to = pick(xn, yn), pick(yn, xn)

    def blocks(px, py, pc):
        k = 4 * px + 2 * py + pc
        rows = lambda ref, n: ref.at[pl.ds(pl.multiple_of(k * n, n), n), :]
        return (rows(wt_s, W_IN_BLK), rows(wout_s, W_OUT_BLK), rows(pww_s, PW_BLK), dw_blk.at[k])

    def copies(arrays, flow, block, to):
        refs = blocks(*block)
        return [pltpu.make_async_remote_copy(src_ref=refs[a], dst_ref=refs[a], send_sem=send_sems.at[a, flow],
                                             recv_sem=recv_sems.at[a, flow], device_id=to, device_id_type=MESH)
                for a in arrays]

    def start(arrays):
        for flow, to in ((F_SIB, sibling), (F_X, (*xn, c)), (F_Y, (*yn, c))):
            for cp in copies(arrays, flow, (x, y, c), to):
                cp.start()

    def direct(arrays):
        for flow, chip in ((F_X, xn), (F_Y, yn)):
            for cp in copies(arrays, flow, (*chip, c), sibling):
                cp.wait_recv()
        for cp in copies(arrays, F_ON, (*on_from, c), (*on_to, c)):
            cp.start()
        for flow, chip in ((F_PASS_X, xn), (F_PASS_Y, yn)):
            for cp in copies(arrays, flow, (*chip, c), sibling):
                cp.start()

    def diagonal(arrays):
        for cp in copies(arrays, F_ON, (*dg, c), sibling):
            cp.wait_recv()
        for cp in copies(arrays, F_PASS_D, (*dg, c), sibling):
            cp.start()

    def from_sibling(arrays, flow, chip):
        for cp in copies(arrays, flow, (*chip, 1 - c), sibling):
            cp.wait_recv()

    def finish(arrays):
        mine = (x, y, c)
        for flow, block in ((F_SIB, mine), (F_X, mine), (F_Y, mine), (F_ON, (*on_from, c)), (F_PASS_X, (*xn, c)),
                            (F_PASS_Y, (*yn, c)), (F_PASS_D, (*dg, c))):
            for cp in copies(arrays, flow, block, sibling):
                cp.wait_send()

    step = s * n_tiles + i
    par = step % 2
    first = step == 0

    @pl.when(first)
    def _():
        mine = blocks(x, y, c)
        mine[0][...] = wint_ref[...].astype(BF16)
        mine[1][...] = wout_ref[...].astype(BF16)
        mine[2][...] = pww_ref[...].astype(BF16)
        mine[3][...] = dw_ref[...]
        start(W_IN_ONLY)

    rows_i = pl.ds(pl.multiple_of(i * tm, tm), tm)

    @pl.when(s == 0)
    def _():
        xt = x_ref[0]
        r1 = lax.rsqrt(jnp.mean(xt * xt, axis=-1, keepdims=True) + RMS_EPS)
        hn_s[rows_i, :] = (xt * r1 * ng_ref[...]).astype(BF16)

    @pl.when(first)
    def _():
        from_sibling(W_IN_ONLY, F_SIB, (x, y))

    @pl.when((s == 1) & (i == 0))
    def _():
        direct(W_IN_ONLY)
        start(OTHERS)
        from_sibling(W_IN_ONLY, F_PASS_Y, yn)

    @pl.when((s == 2) & (i == 0))
    def _():
        from_sibling(W_IN_ONLY, F_PASS_X, xn)
        diagonal(W_IN_ONLY)
        direct(OTHERS)
        from_sibling(W_IN_ONLY, F_PASS_D, dg)

    tiles_per_seq = proj_hbm.shape[1] // tm

    def out_copy(unit, col0):
        dst = proj_hbm.at[i // tiles_per_seq, pl.ds(pl.multiple_of((i % tiles_per_seq) * tm, tm), tm),
                          pl.ds(pl.multiple_of(col0, LANES), SLAB)]
        return pltpu.make_async_copy(stage.at[par, :, pl.ds(unit * SLAB, SLAB)], dst, out_sems.at[par, unit])

    @pl.when(step >= 2)
    def _():
        out_copy(0, 0).wait()

    @pl.when(step >= 2 * n_tiles + 2)
    def _():
        out_copy(1, 0).wait()

    hn = hn_s[rows_i, :]
    for phase, chip in ((0, 2 * x + y), (1, 2 * x + (1 - y))):
        @pl.when(s == phase)
        def _():
            slab = wt_s[pl.ds(pl.multiple_of(chip * SLAB, SLAB), SLAB), :]
            stage[par, :, 0:SLAB] = lax.dot_general(hn, slab, NT_DIMS, preferred_element_type=F32)
            out_copy(0, chip * SLAB).start()

    @pl.when(s == 2)
    def _():
        col0 = (1 - x) * 2 * SLAB
        slab = wt_s[pl.ds(pl.multiple_of(col0, 2 * SLAB), 2 * SLAB), :]
        stage[par] = lax.dot_general(hn, slab, NT_DIMS, preferred_element_type=F32)
        out_copy(0, col0).start()
        out_copy(1, col0 + SLAB).start()

    @pl.when(step == 3 * n_tiles - 1)
    def _():
        out_copy(0, 0).wait()
        out_copy(1, 0).wait()
        pltpu.make_async_copy(stage.at[1 - par, :, pl.ds(0, SLAB)], proj_hbm.at[0, pl.ds(0, tm), pl.ds(0, SLAB)],
                              out_sems.at[1 - par, 0]).wait()
        pltpu.make_async_copy(stage.at[1 - par, :, pl.ds(SLAB, SLAB)], proj_hbm.at[0, pl.ds(0, tm), pl.ds(0, SLAB)],
                              out_sems.at[1 - par, 1]).wait()
        from_sibling(OTHERS, F_SIB, (x, y))
        from_sibling(OTHERS, F_PASS_X, xn)
        from_sibling(OTHERS, F_PASS_Y, yn)
        diagonal(OTHERS)
        from_sibling(OTHERS, F_PASS_D, dg)
        finish(W_IN_ONLY)
        finish(OTHERS)
        wt_ref[...] = wt_s[...]
        woutf_ref[...] = wout_s[...]
        pwwf_ref[...] = pww_s[...]
        for k in range(N_DEV):
            dwf_ref[:, k * DW_BLK:(k + 1) * DW_BLK] = dw_blk[k]


def _inproj(x, norm_g, w_in_t, w_out, pw_w, conv_dw):
    bl, s, _ = x.shape
    tm = TILE_IN
    nt = s // tm
    n_tiles = bl * nt

    def x_map(phase, i):
        i = jnp.where(phase == 0, i, n_tiles - 1)
        return (i // nt, i % nt, 0)

    dma = pltpu.SemaphoreType.DMA
    return pl.pallas_call(
        functools.partial(_inproj_body, n_tiles),
        name="inproj",
        grid=(3, n_tiles),
        in_specs=[pl.BlockSpec((1, tm, D_MODEL), x_map)] + [VMEM] * 5,
        out_specs=[pl.BlockSpec(memory_space=pl.ANY)] + [VMEM] * 4,
        out_shape=[jax.ShapeDtypeStruct((bl, s, D_IN), F32), jax.ShapeDtypeStruct((D_IN, D_MODEL), BF16),
                   jax.ShapeDtypeStruct((D_MODEL, D_MODEL), BF16), jax.ShapeDtypeStruct((W_MIX, W_MIX), BF16),
                   jax.ShapeDtypeStruct((CONV_WIDTH, W_MIX), F32)],
        scratch_shapes=[pltpu.VMEM((n_tiles * tm, D_MODEL), BF16), pltpu.VMEM((D_IN, D_MODEL), BF16),
                        pltpu.VMEM((D_MODEL, D_MODEL), BF16), pltpu.VMEM((W_MIX, W_MIX), BF16),
                        pltpu.VMEM((N_DEV, CONV_WIDTH, DW_BLK), F32), pltpu.VMEM((2, tm, 2 * SLAB), F32),
                        dma((2, 2)), dma((4, 7)), dma((4, 7))],
        compiler_params=pltpu.CompilerParams(dimension_semantics=("arbitrary", "arbitrary"), vmem_limit_bytes=VMEM_LIMIT),
    )(x, _row(norm_g), w_in_t, w_out, pw_w, conv_dw)


B_PLAIN, B_ON, B_MERGED = range(3)


def _routes():
    x, y, c = lax.axis_index("x"), lax.axis_index("y"), lax.axis_index("c")
    xn, yn = (1 - x, y), (x, 1 - y)
    pick = lambda a, b: tuple(jnp.where(c == 1, p, q) for p, q in zip(a, b))
    return pick(xn, yn), pick(yn, xn), (1 - x, 1 - y), (x, y), c


def _chip_index(chip):
    return 2 * chip[0] + chip[1]


def _copy(src, dst, send_sems, recv_sems, a, k, to):
    return pltpu.make_async_remote_copy(src_ref=src, dst_ref=dst, send_sem=send_sems.at[a, k], recv_sem=recv_sems.at[a, k],
                                        device_id=to, device_id_type=MESH)


def _rsb_send(plain, on, rb_plain, rb_on, send_b, recv_b):
    first, _, _, _, c = _routes()
    for k, srcs, dsts in ((B_ON, on, rb_on), (B_PLAIN, plain, rb_plain)):
        for a, (s, d) in enumerate(zip(srcs, dsts)):
            _copy(s, d, send_b, recv_b, a, k, (*first, c)).start()


def _rsb_merge(second_sums, merged, rb_on, rb_merged, send_b, recv_b):
    _, second, _, _, c = _routes()
    for a, (own, m, got, dst) in enumerate(zip(second_sums, merged, rb_on, rb_merged)):
        _copy(got, got, send_b, recv_b, a, B_ON, (*second, c)).wait_recv()
        m[...] = (own[...] + got[...].astype(F32)).astype(BF16)
        _copy(m, dst, send_b, recv_b, a, B_MERGED, (*second, c)).start()


def _rsb_total(own_sums, plain, on, merged, rb_plain, rb_merged, send_b, recv_b):
    _, second, _, _, c = _routes()
    out = []
    for a, (own, p, g) in enumerate(zip(own_sums, rb_plain, rb_merged)):
        _copy(p, p, send_b, recv_b, a, B_PLAIN, (*second, c)).wait_recv()
        _copy(g, g, send_b, recv_b, a, B_MERGED, (*second, c)).wait_recv()
        out.append(own[...] + p[...].astype(F32) + g[...].astype(F32))
    for k, srcs in ((B_PLAIN, plain), (B_ON, on), (B_MERGED, merged)):
        for a, s in enumerate(srcs):
            _copy(s, s, send_b, recv_b, a, k, (*second, c)).wait_send()
    return out


R_LOSS, R_FINAL_G, R_POOL, R_CONV, R_LN = range(5)
VEC_ROWS = 16
DW_ROWS = 32
A_PART, A_X, A_Y, A_ON, A_HALF = range(5)
STEP_SUMS, STEP_MERGE, STEP_HALF = 1, 3, 5


def _bwd2a_body(n_steps, x_ref, dproj_ref, ng_ref, gwout_ref, gpww_ref, loss_ref, gfg_ref, gvec_ref, gdw_ref, gpoolw_ref,
                win_own_ref, win_second_ref, win_plain_ref, win_on_ref, owout_ref, opww_ref, ovec_ref, odw_ref, opoolw_ref,
                acc, a_win, a_wout, a_pww, st_wout, st_pww, rb_wout, rb_pww, part_vec, part_dw, part_pw,
                sib_vec, sib_dw, sib_pw, res_vec, res_dw, res_pw, half_vec, half_dw, half_pw,
                send_a, recv_a, send_b, recv_b, send_r, recv_r, send_w, recv_w):
    step = pl.program_id(0) * pl.num_programs(1) + pl.program_id(1)
    x, y, c = lax.axis_index("x"), lax.axis_index("y"), lax.axis_index("c")
    sibling = (x, y, 1 - c)
    xn, yn = (1 - x, y), (x, 1 - y)
    first, second, diag, own, _ = _routes()
    rows = lambda ref, k, n: ref.at[pl.ds(pl.multiple_of(k * n, n), n), :]

    blocks = lambda k: (rows(gwout_ref, k, W_OUT_BLK), rows(gpww_ref, k, PW_BLK))
    landed = lambda chip: (a_wout.at[chip], a_pww.at[chip])
    staged = lambda k: (st_wout.at[k], st_pww.at[k])
    arrived = lambda k: (rb_wout.at[k], rb_pww.at[k])

    parts, sibs, halves = (part_vec, part_dw, part_pw), (sib_vec, sib_dw, sib_pw), (half_vec, half_dw, half_pw)
    results = (res_vec, res_dw, res_pw)
    half_rows = lambda ref: pl.ds(pl.multiple_of(c * (ref.shape[0] // 2), SUBLANES), ref.shape[0] // 2)
    slot = lambda chip: tuple(h.at[_chip_index(chip)] for h in halves)

    @pl.when(step == 0)
    def _():
        acc[...] = jnp.zeros_like(acc)
        sum8 = lambda v: jnp.sum(v, axis=0, keepdims=True)
        part_vec[...] = jnp.zeros((VEC_ROWS, D_MODEL), F32)
        part_vec[R_LOSS:R_LOSS + 1, :] = jnp.broadcast_to(loss_ref[0:1, 0:1], (1, D_MODEL))
        part_vec[R_FINAL_G:R_FINAL_G + 1, :] = sum8(gfg_ref[...])
        for row, (lo, hi) in ((R_POOL, (V_POOL_B, V_POOL_SCALE)), (R_CONV, (V_CONV_B, V_LN_G)), (R_LN, (V_LN_B, V_PW_B))):
            part_vec[row:row + 1, 0:W_MIX] = sum8(gvec_ref[lo])
            part_vec[row:row + 1, W_MIX:] = sum8(gvec_ref[hi])
        part_dw[0:CONV_WIDTH, :] = jnp.sum(gdw_ref[...], axis=1)
        part_dw[CONV_WIDTH:, :] = jnp.zeros((DW_ROWS - CONV_WIDTH, W_MIX), F32)
        part_pw[...] = gpoolw_ref[...].reshape(len(POOL_WINDOWS) * POOL_GC, POOL_GC)
        for a, (p, s) in enumerate(zip(parts, sibs)):
            _copy(p, s, send_r, recv_r, a, A_PART, sibling).start()
        for chip in range(4):
            for a, (s, d) in enumerate(zip(blocks(2 * chip + (1 - c)), landed(chip))):
                _copy(s, d, send_a, recv_a, a, chip, sibling).start()

    x_t = x_ref[0]
    r1 = lax.rsqrt(jnp.mean(x_t * x_t, axis=-1, keepdims=True) + RMS_EPS)
    hn = (x_t * r1 * ng_ref[...]).astype(BF16)
    acc[...] += lax.dot_general(dproj_ref[0], hn, TN_DIMS, preferred_element_type=F32)

    @pl.when(step == STEP_SUMS)
    def _():
        for a, (p, s, h) in enumerate(zip(parts, sibs, slot(own))):
            _copy(p, s, send_r, recv_r, a, A_PART, sibling).wait_recv()
            h[...] = p[half_rows(p), :] + s[half_rows(s), :]
            for k, to in ((A_X, xn), (A_Y, yn)):
                _copy(h, h, send_r, recv_r, a, k, (*to, c)).start()
        for chip in range(4):
            for a, (m, l) in enumerate(zip(blocks(2 * chip + c), landed(chip))):
                _copy(l, l, send_a, recv_a, a, chip, sibling).wait_recv()
                l[...] = m[...] + l[...]
        for k, chip in ((B_ON, diag), (B_PLAIN, first)):
            for s, l in zip(staged(k), landed(_chip_index(chip))):
                s[...] = l[...].astype(BF16)
        _rsb_send(staged(B_PLAIN), staged(B_ON), arrived(B_PLAIN), arrived(B_ON), send_b, recv_b)

    @pl.when(step == STEP_MERGE)
    def _():
        _rsb_merge(landed(_chip_index(second)), staged(B_MERGED), arrived(B_ON), arrived(B_MERGED), send_b, recv_b)
        for a in range(3):
            for k, chip in ((A_X, xn), (A_Y, yn)):
                h = slot(chip)[a]
                _copy(h, h, send_r, recv_r, a, k, sibling).wait_recv()
            h = slot(first)[a]
            _copy(h, h, send_r, recv_r, a, A_ON, (*second, c)).start()

    @pl.when(step == STEP_HALF)
    def _():
        for a, (r, hs) in enumerate(zip(results, halves)):
            h = slot(diag)[a]
            _copy(h, h, send_r, recv_r, a, A_ON, sibling).wait_recv()
            mine = r.at[half_rows(r), :]
            mine[...] = (hs[0] + hs[1]) + (hs[2] + hs[3])
            _copy(mine, mine, send_r, recv_r, a, A_HALF, sibling).start()

    @pl.when(step == n_steps - 1)
    def _():
        block = lambda k: rows(acc, k, W_IN_BLK)
        w_copy = lambda chip: _copy(block(2 * chip + (1 - c)), a_win.at[chip], send_w, recv_w, 0, chip, sibling)
        for chip in range(4):
            w_copy(chip).start()

        owout_ref[...], opww_ref[...] = _rsb_total(landed(_chip_index(own)), staged(B_PLAIN), staged(B_ON), staged(B_MERGED),
                                                   arrived(B_PLAIN), arrived(B_MERGED), send_b, recv_b)
        for a, (o, r, p, s) in enumerate(zip((ovec_ref, odw_ref, opoolw_ref), results, parts, sibs)):
            theirs = r.at[pl.ds(pl.multiple_of((1 - c) * (r.shape[0] // 2), SUBLANES), r.shape[0] // 2), :]
            _copy(theirs, theirs, send_r, recv_r, a, A_HALF, sibling).wait_recv()
            o[...] = r[...]
            _copy(p, s, send_r, recv_r, a, A_PART, sibling).wait_send()
            for k, h in ((A_X, slot(own)[a]), (A_Y, slot(own)[a]), (A_ON, slot(first)[a])):
                _copy(h, h, send_r, recv_r, a, k, sibling).wait_send()
            mine = r.at[half_rows(r), :]
            _copy(mine, mine, send_r, recv_r, a, A_HALF, sibling).wait_send()
        for chip in range(4):
            for a, (s, d) in enumerate(zip(blocks(2 * chip + (1 - c)), landed(chip))):
                _copy(s, d, send_a, recv_a, a, chip, sibling).wait_send()

        for chip in range(4):
            w_copy(chip).wait()
        chip_sum = lambda chip: block(2 * _chip_index(chip) + c)[...] + a_win[_chip_index(chip)]
        win_own_ref[...] = chip_sum(own)
        win_second_ref[...] = chip_sum(second)
        win_plain_ref[...] = chip_sum(first).astype(BF16)
        win_on_ref[...] = chip_sum(diag).astype(BF16)


def _bwd2a(x, dproj, norm_g, gwout, gpww, loss8, gfg8, gvec, gdw8, gpoolw):
    bl, s, _ = x.shape
    tm = TILE_M
    nt = s // tm
    tok = lambda n: pl.BlockSpec((1, tm, n), lambda b, i: (b, i, 0))
    n_pw = len(POOL_WINDOWS) * POOL_GC
    win = (W_IN_BLK, D_MODEL)
    blk = ((W_OUT_BLK, D_MODEL), (PW_BLK, W_MIX))
    rep = ((VEC_ROWS, D_MODEL), (DW_ROWS, W_MIX), (n_pw, POOL_GC))
    half = lambda sh: (sh[0] // 2,) + sh[1:]
    dma = pltpu.SemaphoreType.DMA
    assert bl * nt > STEP_HALF + 1, "the exchanges' phases need their grid steps"
    return pl.pallas_call(
        functools.partial(_bwd2a_body, bl * nt),
        name="bwd2a",
        grid=(bl, nt),
        in_specs=[tok(D_MODEL), tok(D_IN)] + [VMEM] * 8,
        out_specs=[VMEM] * 9,
        out_shape=[jax.ShapeDtypeStruct(win, F32), jax.ShapeDtypeStruct(win, F32), jax.ShapeDtypeStruct(win, BF16),
                   jax.ShapeDtypeStruct(win, BF16)] + [jax.ShapeDtypeStruct(sh, F32) for sh in blk + rep],
        scratch_shapes=[pltpu.VMEM((D_IN, D_MODEL), F32), pltpu.VMEM((4,) + win, F32)]
        + [pltpu.VMEM((4,) + sh, F32) for sh in blk]
        + [pltpu.VMEM((3,) + sh, BF16) for sh in blk] * 2
        + [pltpu.VMEM(sh, F32) for sh in rep] * 3
        + [pltpu.VMEM((4,) + half(sh), F32) for sh in rep]
        + [dma((2, 4)), dma((2, 4)), dma((2, 3)), dma((2, 3)), dma((3, 5)), dma((3, 5)), dma((1, 4)), dma((1, 4))],
        compiler_params=pltpu.CompilerParams(dimension_semantics=("arbitrary", "arbitrary"), vmem_limit_bytes=VMEM_LIMIT),
    )(x, dproj, _row(norm_g), gwout, gpww, loss8, gfg8, gvec, gdw8, gpoolw)


def _bwd2b_body(n_steps, x_ref, dh_ref, dproj_ref, ng_ref, wint_ref, win_own_ref, win_second_ref, win_plain_hbm, win_on_hbm,
                gx_ref, owin_ref, ong_ref,
                gng, merged, rb_win, all_ng, send_b, recv_b, send_n, recv_n):
    step = pl.program_id(0) * pl.num_programs(1) + pl.program_id(1)
    x, y, c = lax.axis_index("x"), lax.axis_index("y"), lax.axis_index("c")
    sibling = (x, y, 1 - c)
    other_chips = [(1 - x, y), (x, 1 - y), (1 - x, 1 - y)]
    arrived = lambda k: (rb_win.at[k],)

    @pl.when(step == 0)
    def _():
        gng[...] = jnp.zeros_like(gng)
        _rsb_send((win_plain_hbm,), (win_on_hbm,), arrived(B_PLAIN), arrived(B_ON), send_b, recv_b)

    x_t = x_ref[0]
    r1 = lax.rsqrt(jnp.mean(x_t * x_t, axis=-1, keepdims=True) + RMS_EPS)
    xr = x_t * r1
    dhn = jnp.dot(dproj_ref[0], wint_ref[...], preferred_element_type=F32)
    gng[...] += _colsum8(dhn * xr)
    gy = dhn * ng_ref[...]
    gx_ref[0] = dh_ref[0] + r1 * (gy - xr * jnp.mean(gy * xr, axis=-1, keepdims=True))

    @pl.when(step == STEP_MERGE)
    def _():
        _rsb_merge((win_second_ref,), (merged,), arrived(B_ON), arrived(B_MERGED), send_b, recv_b)

    k_me = 4 * x + 2 * y + c
    peers = [sibling] + [(*chip, c) for chip in other_chips] + [(*chip, 1 - c) for chip in other_chips]
    ng_copy = lambda r: pltpu.make_async_remote_copy(src_ref=all_ng.at[k_me], dst_ref=all_ng.at[k_me], send_sem=send_n.at[r],
                                                     recv_sem=recv_n.at[r], device_id=peers[r], device_id_type=MESH)

    @pl.when(step == n_steps - 1)
    def _():
        all_ng[k_me] = gng[...]
        for r in range(N_PEER):
            ng_copy(r).start()
        (owin_ref[...],) = _rsb_total((win_own_ref,), (win_plain_hbm,), (win_on_hbm,), (merged,), arrived(B_PLAIN),
                                      arrived(B_MERGED), send_b, recv_b)
        for r in range(N_PEER):
            ng_copy(r).wait()
        s = all_ng[0]
        for k in range(1, N_DEV):
            s = s + all_ng[k]
        ong_ref[...] = jnp.broadcast_to(jnp.sum(s, axis=0, keepdims=True), ong_ref.shape)


def _bwd2b(x, dh, dproj, norm_g, w_in_t, win_own, win_second, win_plain, win_on):
    bl, s, _ = x.shape
    tm = TILE_M
    nt = s // tm
    tok = lambda n: pl.BlockSpec((1, tm, n), lambda b, i: (b, i, 0))
    win = (W_IN_BLK, D_MODEL)
    hbm = pl.BlockSpec(memory_space=pl.ANY)
    dma = pltpu.SemaphoreType.DMA
    return pl.pallas_call(
        functools.partial(_bwd2b_body, bl * nt),
        name="bwd2b",
        grid=(bl, nt),
        in_specs=[tok(D_MODEL), tok(D_MODEL), tok(D_IN), VMEM, VMEM, VMEM, VMEM, hbm, hbm],
        out_specs=[tok(D_MODEL), VMEM, VMEM],
        out_shape=[jax.ShapeDtypeStruct((bl, s, D_MODEL), F32), jax.ShapeDtypeStruct(win, F32),
                   jax.ShapeDtypeStruct((SUBLANES, D_MODEL), F32)],
        scratch_shapes=[pltpu.VMEM((SUBLANES, D_MODEL), F32), pltpu.VMEM(win, BF16), pltpu.VMEM((3,) + win, BF16),
                        pltpu.VMEM((N_DEV, SUBLANES, D_MODEL), F32), dma((1, 3)), dma((1, 3)), dma((N_PEER,)), dma((N_PEER,))],
        compiler_params=pltpu.CompilerParams(dimension_semantics=("arbitrary", "arbitrary"), vmem_limit_bytes=VMEM_LIMIT),
    )(x, dh, dproj, _row(norm_g), w_in_t, win_own, win_second, win_plain, win_on)


ADAM_LR = 0.001
ADAM_B1 = 0.9
ADAM_B2 = 0.999
ADAM_EPS = 1e-08
ADAM_WD = 0.01
ADAM_STEP = 10


def _adamw_body(n, *refs):
    ws, gs, ms, vs = (refs[k * n:(k + 1) * n] for k in range(4))
    ds, m2s, v2s = (refs[(4 + k) * n:(5 + k) * n] for k in range(3))
    for w_ref, g_ref, m_ref, v_ref, d_ref, m2_ref, v2_ref in zip(ws, gs, ms, vs, ds, m2s, v2s):
        g = g_ref[...]
        m = ADAM_B1 * m_ref[...] + (1.0 - ADAM_B1) * g
        v = ADAM_B2 * v_ref[...] + (1.0 - ADAM_B2) * (g * g)
        m_hat = m / (1.0 - ADAM_B1 ** ADAM_STEP)
        v_hat = v / (1.0 - ADAM_B2 ** ADAM_STEP)
        d_ref[...] = -ADAM_LR * (m_hat / (jnp.sqrt(v_hat) + ADAM_EPS) + ADAM_WD * w_ref[...])
        m2_ref[...] = m
        v2_ref[...] = v


def _adamw(ws, gs, ms, vs):
    n = len(ws)
    shapes = [w.shape for w in ws]
    flat = lambda a: a.reshape(1, -1) if a.ndim == 1 else a
    args = [flat(a) for a in (*ws, *gs, *ms, *vs)]
    outs = pl.pallas_call(
        functools.partial(_adamw_body, n),
        name="adamw",
        in_specs=[VMEM] * (4 * n),
        out_specs=[VMEM] * (3 * n),
        out_shape=[jax.ShapeDtypeStruct(a.shape, F32) for a in args[:n]] * 3,
        compiler_params=pltpu.CompilerParams(vmem_limit_bytes=VMEM_LIMIT),
    )(*args)
    outs = [o.reshape(shapes[k % n]) for k, o in enumerate(outs)]
    return outs[:n], outs[n:2 * n], outs[2 * n:]


def kernel(x, norm_g, w_in, pool_w, pool_b, pool_scale, conv_dw, conv_b, ln_g, ln_b, pw_w, pw_b, w_out, final_g, loss_target, m_norm_g, m_w_in, m_pool_w, m_pool_b, m_pool_scale, m_conv_dw, m_conv_b, m_ln_g, m_ln_b, m_pw_w, m_pw_b, m_w_out, m_final_g, v_norm_g, v_w_in, v_pool_w, v_pool_b, v_pool_scale, v_conv_dw, v_conv_b, v_ln_g, v_ln_b, v_pw_w, v_pw_b, v_w_out, v_final_g):
    proj, w_in_t, w_out_b, pw_w_b, conv_dw_f = _inproj(x, norm_g, w_in.T, w_out, pw_w, conv_dw)
    h1, z, ob, dpool, dh, loss8, gfg8 = _fwd(x, loss_target, proj, pool_w, pool_b, pool_scale, conv_dw_f,
                                             conv_b, ln_g, ln_b, pw_w_b, pw_b, w_out_b, final_g)
    dproj, gwout, gpww, gpoolw, gvec, gdw8 = _bwd1(dh, proj, h1, z, ob, dpool, w_out_b, pw_w_b, pool_w, pool_scale,
                                                   conv_dw_f, ln_g, ln_b)
    win_sums = _bwd2a(x, dproj, norm_g, gwout, gpww, loss8, gfg8, gvec, gdw8, gpoolw)
    g_w_out, g_pw_w, vec, dw_sum, poolw_sum = win_sums[4:]
    grad_x, g_w_in_t, g_norm_g = _bwd2b(x, dh, dproj, norm_g, w_in_t, *win_sums[:4])

    me = 4 * lax.axis_index("x") + 2 * lax.axis_index("y") + lax.axis_index("c")
    n_g = len(POOL_WINDOWS)
    grads = dict(
        norm_g=g_norm_g[0], w_in=g_w_in_t, pool_w=poolw_sum.reshape(n_g, POOL_GC, POOL_GC),
        pool_b=vec[R_POOL, :W_MIX].reshape(n_g, POOL_GC), pool_scale=vec[R_POOL, W_MIX:],
        conv_dw=lax.dynamic_slice(dw_sum, (0, me * DW_BLK), (CONV_WIDTH, DW_BLK)), conv_b=vec[R_CONV, :W_MIX],
        ln_g=vec[R_CONV, W_MIX:], ln_b=vec[R_LN, :W_MIX], pw_w=g_pw_w, pw_b=vec[R_LN, W_MIX:], w_out=g_w_out,
        final_g=vec[R_FINAL_G])
    weights = dict(norm_g=norm_g, w_in=w_in.T, pool_w=pool_w, pool_b=pool_b, pool_scale=pool_scale, conv_dw=conv_dw, conv_b=conv_b,
                   ln_g=ln_g, ln_b=ln_b, pw_w=pw_w, pw_b=pw_b, w_out=w_out, final_g=final_g)
    m_in = dict(norm_g=m_norm_g, w_in=m_w_in.T, pool_w=m_pool_w, pool_b=m_pool_b, pool_scale=m_pool_scale, conv_dw=m_conv_dw,
                conv_b=m_conv_b, ln_g=m_ln_g, ln_b=m_ln_b, pw_w=m_pw_w, pw_b=m_pw_b, w_out=m_w_out, final_g=m_final_g)
    v_in = dict(norm_g=v_norm_g, w_in=v_w_in.T, pool_w=v_pool_w, pool_b=v_pool_b, pool_scale=v_pool_scale, conv_dw=v_conv_dw,
                conv_b=v_conv_b, ln_g=v_ln_g, ln_b=v_ln_b, pw_w=v_pw_w, pw_b=v_pw_b, w_out=v_w_out, final_g=v_final_g)
    names = list(weights)
    delta, new_m, new_v = _adamw([weights[n] for n in names], [grads[n] for n in names], [m_in[n] for n in names],
                                 [v_in[n] for n in names])
    k_in = names.index("w_in")
    for group in (delta, new_m, new_v):
        group[k_in] = group[k_in].T
    grads["w_in"] = g_w_in_t.T
    return (vec[R_LOSS, 0], grad_x, *[grads[n] for n in names], *delta, *new_m, *new_v)
```

```python
import functools

import jax
import jax.numpy as jnp
from jax import lax
from jax.experimental import pallas as pl
from jax.experimental.pallas import tpu as pltpu

F32 = jnp.float32
BF16 = jnp.bfloat16

D_MODEL = 1024
W_MIX = 512
D_IN = 5 * W_MIX
POOL_WINDOWS = (2, 4, 8, 16)
POOL_GC = 128
CONV_WIDTH = 31
RMS_EPS = 1e-6
LN_EPS = 1e-5
N_DEV = 8

LANES = 128
SUBLANES = 8
TILE_M = 512
TILE_IN = 1024
POOL_HALO = 16
CONV_HALO = 32
CONV_ROWS = 128
BWD_CONV_ROWS = 64
BWD_SUBTILES = 2
VMEM_LIMIT = 56 * 1024 * 1024

VMEM = pl.BlockSpec(memory_space=pltpu.VMEM)


def _silu(v):
    return v * jax.nn.sigmoid(v)


def _colsum8(v):
    m, n = v.shape
    return jnp.sum(v.reshape(m // SUBLANES, SUBLANES, n), axis=0)


def _row_index(shape, t0):
    return lax.broadcasted_iota(jnp.int32, shape, 0) + t0


def _conv_taps():
    taps = {s: [] for s in range(SUBLANES)}
    for k in range(CONV_WIDTH):
        e = k + CONV_HALO - (CONV_WIDTH - 1)
        taps[e % SUBLANES].append((e // SUBLANES, k))
    return taps


def _fwd_body(x_ref, tgt_ref, proj_ref, poolw_ref, poolb_ref, pools_ref, dw_ref, cb_ref, lg_ref, lb_ref,
              pww_ref, pwb_ref, wout_ref, fg_ref,
              h1_ref, z_ref, ob_ref, dpool_ref, dh_ref, loss_ref, gfg_ref,
              uext, hext, y_s):
    b = pl.program_id(0)
    i = pl.program_id(1)
    tm = TILE_M

    @pl.when(i == 0)
    def _():
        uext[0:POOL_HALO, :] = jnp.zeros((POOL_HALO, W_MIX), F32)
        hext[0:CONV_HALO, :] = jnp.zeros((CONV_HALO, W_MIX), F32)
        hext[CONV_HALO + tm:, :] = jnp.zeros((SUBLANES, W_MIX), F32)

    @pl.when((b == 0) & (i == 0))
    def _():
        loss_ref[...] = jnp.zeros_like(loss_ref)
        gfg_ref[...] = jnp.zeros_like(gfg_ref)

    uext[POOL_HALO:, :] = proj_ref[0, :, 0:W_MIX]
    t = _row_index((tm, POOL_GC), i * tm)
    for g, w in enumerate(POOL_WINDOWS):
        lanes = slice(g * POOL_GC, (g + 1) * POOL_GC)
        e = uext[:, lanes]
        s = e
        sh = 1
        while sh < w:
            s = s + pltpu.roll(s, sh, 0)
            sh *= 2
        inv_cnt = 1.0 / jnp.minimum(t + 1, w).astype(F32)
        d = (s[POOL_HALO:] * inv_cnt - e[POOL_HALO:]).astype(BF16)
        dpool_ref[0, :, lanes] = d
        z = jnp.dot(d, poolw_ref[g].astype(BF16), preferred_element_type=F32) + poolb_ref[:, lanes]
        z_ref[0, :, lanes] = z
        a_gate = proj_ref[0, :, W_MIX + g * POOL_GC:W_MIX + (g + 1) * POOL_GC]
        y_s[:, lanes] = (z * pools_ref[:, lanes] * _silu(a_gate)).astype(BF16)

    hext[CONV_HALO:CONV_HALO + tm, :] = proj_ref[0, :, 2 * W_MIX:3 * W_MIX] * jax.nn.sigmoid(proj_ref[0, :, 3 * W_MIX:4 * W_MIX])
    taps = _conv_taps()
    for c0 in range(0, tm, CONV_ROWS):
        for l0 in range(0, W_MIX, LANES):
            lanes = slice(l0, l0 + LANES)
            acc = jnp.zeros((CONV_ROWS, LANES), F32) + cb_ref[:, lanes]
            for s in range(SUBLANES):
                part = jnp.zeros((CONV_ROWS + SUBLANES, LANES), F32)
                for q, k in taps[s]:
                    r0 = c0 + SUBLANES * q
                    part = part + hext[r0:r0 + CONV_ROWS + SUBLANES, lanes] * dw_ref[k:k + 1, lanes]
                acc = acc + part[s:s + CONV_ROWS]
            h1_ref[0, c0:c0 + CONV_ROWS, lanes] = acc
    h1 = h1_ref[0]
    mu = jnp.mean(h1, axis=-1, keepdims=True)
    xc = h1 - mu
    rstd = lax.rsqrt(jnp.mean(xc * xc, axis=-1, keepdims=True) + LN_EPS)
    h3 = _silu(xc * rstd * lg_ref[...] + lb_ref[...])
    ob = jnp.dot(h3.astype(BF16), pww_ref[...], preferred_element_type=F32) + pwb_ref[...]
    ob_ref[0] = ob
    y_s[:, W_MIX:] = (ob * _silu(proj_ref[0, :, 4 * W_MIX:5 * W_MIX])).astype(BF16)

    h = x_ref[0] + jnp.dot(y_s[...], wout_ref[...], preferred_element_type=F32)
    r2 = lax.rsqrt(jnp.mean(h * h, axis=-1, keepdims=True) + RMS_EPS)
    hr = h * r2
    err = hr * fg_ref[...] - tgt_ref[0]
    loss_ref[...] += jnp.sum(err * err) * (0.5 / D_MODEL)
    dout = err * (1.0 / D_MODEL)
    gfg_ref[...] += _colsum8(dout * hr)
    gy = dout * fg_ref[...]
    dh_ref[0] = r2 * (gy - hr * jnp.mean(gy * hr, axis=-1, keepdims=True))

    uext[0:POOL_HALO, :] = uext[tm:tm + POOL_HALO, :]
    hext[0:CONV_HALO, :] = hext[tm:tm + CONV_HALO, :]


def _row(v):
    return v.reshape(1, -1)


def _fwd(x, tgt, proj, pool_w, pool_b, pool_scale, conv_dw_f, conv_b, ln_g, ln_b, pw_w_b, pw_b, w_out_b, final_g):
    bl, s, _ = x.shape
    tm = TILE_M
    tok = lambda n: pl.BlockSpec((1, tm, n), lambda b, i: (b, i, 0))
    acc = lambda n: pl.BlockSpec((SUBLANES, n), lambda b, i: (0, 0))
    act = lambda n: jax.ShapeDtypeStruct((bl, s, n), F32)
    return pl.pallas_call(
        functools.partial(_fwd_body),
        name="fwd",
        grid=(bl, s // tm),
        in_specs=[tok(D_MODEL), tok(D_MODEL), tok(D_IN)] + [VMEM] * 11,
        out_specs=[tok(W_MIX), tok(W_MIX), tok(W_MIX), tok(W_MIX), tok(D_MODEL), acc(LANES), acc(D_MODEL)],
        out_shape=[act(W_MIX), act(W_MIX), act(W_MIX), jax.ShapeDtypeStruct((bl, s, W_MIX), BF16), act(D_MODEL),
                   jax.ShapeDtypeStruct((SUBLANES, LANES), F32), jax.ShapeDtypeStruct((SUBLANES, D_MODEL), F32)],
        scratch_shapes=[pltpu.VMEM((POOL_HALO + tm, W_MIX), F32), pltpu.VMEM((CONV_HALO + tm + SUBLANES, W_MIX), F32),
                        pltpu.VMEM((tm, D_MODEL), BF16)],
        compiler_params=pltpu.CompilerParams(dimension_semantics=("arbitrary", "arbitrary"), vmem_limit_bytes=VMEM_LIMIT),
    )(x, tgt, proj, pool_w, pool_b.reshape(1, W_MIX), _row(pool_scale), conv_dw_f, _row(conv_b),
      _row(ln_g), _row(ln_b), pw_w_b, _row(pw_b), w_out_b, _row(final_g))


NT_DIMS = (((1,), (1,)), ((), ()))
TN_DIMS = (((0,), (0,)), ((), ()))
V_POOL_B, V_POOL_SCALE, V_CONV_B, V_LN_G, V_LN_B, V_PW_B = range(6)


def _bwd_taps():
    taps = {s: [] for s in range(SUBLANES)}
    for k in range(CONV_WIDTH):
        e = CONV_WIDTH - 1 - k
        taps[e % SUBLANES].append((e // SUBLANES, k))
    return taps


def _bwd1_body(n_tiles, dh_ref, proj_ref, h1_ref, z_ref, ob_ref, dpool_ref, wout_ref, pww_ref, poolw_ref, pools_ref, dw_ref,
               lg_ref, lb_ref,
               dproj_ref, gwout_ref, gpww_ref, gpoolw_ref, gvec_ref, gdw_ref,
               vext, gext, gsh, y_s):
    b = pl.program_id(0)
    i = pl.program_id(1)
    tm = TILE_M
    t0 = (n_tiles - 1 - i) * tm

    @pl.when(i == 0)
    def _():
        vext[tm:, :] = jnp.zeros((POOL_HALO, W_MIX), F32)
        gext[tm:, :] = jnp.zeros((CONV_HALO, W_MIX), F32)

    @pl.when((b == 0) & (i == 0))
    def _():
        gwout_ref[...] = jnp.zeros_like(gwout_ref)
        gpww_ref[...] = jnp.zeros_like(gpww_ref)
        gpoolw_ref[...] = jnp.zeros_like(gpoolw_ref)
        gvec_ref[...] = jnp.zeros_like(gvec_ref)
        gdw_ref[...] = jnp.zeros_like(gdw_ref)

    sub = tm // BWD_SUBTILES

    def dense_phase(r0):
        rs = slice(r0, r0 + sub)
        dhb = dh_ref[0, rs, :].astype(BF16)
        dy_a = lax.dot_general(dhb, wout_ref[0:W_MIX, :], NT_DIMS, preferred_element_type=F32)
        dy_b = lax.dot_general(dhb, wout_ref[W_MIX:, :], NT_DIMS, preferred_element_type=F32)

        a_gate = proj_ref[0, rs, W_MIX:2 * W_MIX]
        sig_a = jax.nn.sigmoid(a_gate)
        silu_a = a_gate * sig_a
        z = z_ref[0, rs, :]
        out_a = z * pools_ref[...]
        y_s[rs, 0:W_MIX] = (out_a * silu_a).astype(BF16)
        d_out_a = dy_a * silu_a
        d_ag = dy_a * out_a * (sig_a * (1.0 + a_gate * (1.0 - sig_a)))
        dproj_ref[0, rs, W_MIX:2 * W_MIX] = d_ag.astype(BF16)
        gvec_ref[V_POOL_SCALE] += _colsum8(d_out_a * z)
        dz = d_out_a * pools_ref[...]
        gvec_ref[V_POOL_B] += _colsum8(dz)
        t = _row_index((sub, POOL_GC), t0 + r0)
        n_ext = sub + POOL_HALO
        for g, w in enumerate(POOL_WINDOWS):
            lanes = slice(g * POOL_GC, (g + 1) * POOL_GC)
            dzg = dz[:, lanes].astype(BF16)
            gpoolw_ref[g] += lax.dot_general(dpool_ref[0, rs, lanes], dzg, TN_DIMS, preferred_element_type=F32)
            dd = lax.dot_general(dzg, poolw_ref[g].astype(BF16), NT_DIMS, preferred_element_type=F32)
            vext[rs, lanes] = dd * (1.0 / jnp.minimum(t + 1, w).astype(F32))
            s = vext[r0:r0 + n_ext, lanes]
            sh = 1
            while sh < w:
                s = s + pltpu.roll(s, n_ext - sh, 0)
                sh *= 2
            dproj_ref[0, rs, lanes] = (s[0:sub] - dd).astype(BF16)

        b_gate = proj_ref[0, rs, 4 * W_MIX:5 * W_MIX]
        sig_b = jax.nn.sigmoid(b_gate)
        silu_b = b_gate * sig_b
        ob = ob_ref[0, rs, :]
        y_s[rs, W_MIX:] = (ob * silu_b).astype(BF16)
        d_ob = dy_b * silu_b
        dproj_ref[0, rs, 4 * W_MIX:5 * W_MIX] = (dy_b * ob * (sig_b * (1.0 + b_gate * (1.0 - sig_b)))).astype(BF16)
        gvec_ref[V_PW_B] += _colsum8(d_ob)
        gwout_ref[...] += lax.dot_general(y_s[rs, :], dhb, TN_DIMS, preferred_element_type=F32)

        h1 = h1_ref[0, rs, :]
        mu = jnp.mean(h1, axis=-1, keepdims=True)
        xc = h1 - mu
        rstd = lax.rsqrt(jnp.mean(xc * xc, axis=-1, keepdims=True) + LN_EPS)
        xhat = xc * rstd
        h2 = xhat * lg_ref[...] + lb_ref[...]
        sig2 = jax.nn.sigmoid(h2)
        d_obb = d_ob.astype(BF16)
        gpww_ref[...] += lax.dot_general((h2 * sig2).astype(BF16), d_obb, TN_DIMS, preferred_element_type=F32)
        dh2 = lax.dot_general(d_obb, pww_ref[...], NT_DIMS, preferred_element_type=F32) * (sig2 * (1.0 + h2 * (1.0 - sig2)))
        gvec_ref[V_LN_G] += _colsum8(dh2 * xhat)
        gvec_ref[V_LN_B] += _colsum8(dh2)
        dxh = dh2 * lg_ref[...]
        dh1 = rstd * (dxh - jnp.mean(dxh, axis=-1, keepdims=True) - xhat * jnp.mean(dxh * xhat, axis=-1, keepdims=True))
        gvec_ref[V_CONV_B] += _colsum8(dh1)
        gext[rs, :] = dh1

    taps = _bwd_taps()
    cr = BWD_CONV_ROWS
    win = cr + CONV_HALO - SUBLANES

    def conv_phase(r0):
        for c0 in range(r0, r0 + sub, cr):
            for l0 in range(0, W_MIX, LANES):
                lanes = slice(l0, l0 + LANES)
                b_val = proj_ref[0, c0:c0 + cr, 2 * W_MIX + l0:2 * W_MIX + l0 + LANES]
                sg = jax.nn.sigmoid(proj_ref[0, c0:c0 + cr, 3 * W_MIX + l0:3 * W_MIX + l0 + LANES])
                h0c = b_val * sg
                acc = jnp.zeros((cr, LANES), F32)
                for s in range(SUBLANES):
                    gsh[...] = gext[c0 + s:c0 + s + win, lanes]
                    for q, k in taps[s]:
                        gq = gsh[SUBLANES * q:SUBLANES * q + cr, :]
                        acc = acc + gq * dw_ref[k:k + 1, lanes]
                        gdw_ref[k, :, lanes] += _colsum8(h0c * gq)
                d_bval = acc * sg
                dproj_ref[0, c0:c0 + cr, 2 * W_MIX + l0:2 * W_MIX + l0 + LANES] = d_bval.astype(BF16)
                dproj_ref[0, c0:c0 + cr, 3 * W_MIX + l0:3 * W_MIX + l0 + LANES] = (d_bval * b_val * (1.0 - sg)).astype(BF16)

    starts = [k * sub for k in reversed(range(BWD_SUBTILES))]
    for r0 in starts:
        dense_phase(r0)
    for r0 in starts:
        conv_phase(r0)
    vext[tm:, :] = vext[0:POOL_HALO, :]
    gext[tm:, :] = gext[0:CONV_HALO, :]


def _bwd1(dh, proj, h1, z, ob, dpool, w_out_b, pw_w_b, pool_w, pool_scale, conv_dw_f, ln_g, ln_b):
    bl, s, _ = dh.shape
    tm = TILE_M
    nt = s // tm
    tok = lambda n: pl.BlockSpec((1, tm, n), lambda b, i: (b, nt - 1 - i, 0))
    res = lambda shape: pl.BlockSpec(shape, lambda b, i: (0,) * len(shape))
    out_shapes = [(D_MODEL, D_MODEL), (W_MIX, W_MIX), (len(POOL_WINDOWS), POOL_GC, POOL_GC), (6, SUBLANES, W_MIX),
                  (CONV_WIDTH, SUBLANES, W_MIX)]
    return pl.pallas_call(
        functools.partial(_bwd1_body, nt),
        name="bwd1",
        grid=(bl, nt),
        in_specs=[tok(D_MODEL), tok(D_IN), tok(W_MIX), tok(W_MIX), tok(W_MIX), tok(W_MIX)] + [VMEM] * 7,
        out_specs=[tok(D_IN)] + [res(sh) for sh in out_shapes],
        out_shape=[jax.ShapeDtypeStruct((bl, s, D_IN), BF16)] + [jax.ShapeDtypeStruct(sh, F32) for sh in out_shapes],
        scratch_shapes=[pltpu.VMEM((tm + POOL_HALO, W_MIX), F32), pltpu.VMEM((tm + CONV_HALO, W_MIX), F32),
                        pltpu.VMEM((BWD_CONV_ROWS + CONV_HALO - SUBLANES, LANES), F32), pltpu.VMEM((tm, D_MODEL), BF16)],
        compiler_params=pltpu.CompilerParams(dimension_semantics=("arbitrary", "arbitrary"), vmem_limit_bytes=VMEM_LIMIT),
    )(dh, proj, h1, z, ob, dpool, w_out_b, pw_w_b, pool_w, _row(pool_scale), conv_dw_f, _row(ln_g), _row(ln_b))


MESH = pl.DeviceIdType.MESH
W_IN_BLK = D_IN // N_DEV
W_OUT_BLK = D_MODEL // N_DEV
PW_BLK = W_MIX // N_DEV
DW_BLK = W_MIX // N_DEV
N_PEER = N_DEV - 1


SLAB = 2 * W_IN_BLK
F_SIB, F_X, F_Y, F_ON, F_PASS_X, F_PASS_Y, F_PASS_D = range(7)
W_IN_ONLY, OTHERS = (0,), (1, 2, 3)


def _inproj_body(n_tiles, x_ref, ng_ref, wint_ref, wout_ref, pww_ref, dw_ref,
                 proj_hbm, wt_ref, woutf_ref, pwwf_ref, dwf_ref,
                 hn_s, wt_s, wout_s, pww_s, dw_blk, stage, out_sems, send_sems, recv_sems):
    s = pl.program_id(0)
    i = pl.program_id(1)
    tm = TILE_IN
    x, y, c = lax.axis_index("x"), lax.axis_index("y"), lax.axis_index("c")
    sibling = (x, y, 1 - c)
    xn, yn, dg = (1 - x, y), (x, 1 - y), (1 - x, 1 - y)
    pick = lambda a, b: tuple(jnp.where(c == 1, p, q) for p, q in zip(a, b))
    on_from, on_to = pick(xn, yn), pick(yn, xn)

    def blocks(px, py, pc):
        k = 4 * px + 2 * py + pc
        rows = lambda ref, n: ref.at[pl.ds(pl.multiple_of(k * n, n), n), :]
        return (rows(wt_s, W_IN_BLK), rows(wout_s, W_OUT_BLK), rows(pww_s, PW_BLK), dw_blk.at[k])

    def copies(arrays, flow, block, to):
        refs = blocks(*block)
        return [pltpu.make_async_remote_copy(src_ref=refs[a], dst_ref=refs[a], send_sem=send_sems.at[a, flow],
                                             recv_sem=recv_sems.at[a, flow], device_id=to, device_id_type=MESH)
                for a in arrays]

    def start(arrays):
        for flow, to in ((F_SIB, sibling), (F_X, (*xn, c)), (F_Y, (*yn, c))):
            for cp in copies(arrays, flow, (x, y, c), to):
                cp.start()

    def direct(arrays):
        for flow, chip in ((F_X, xn), (F_Y, yn)):
            for cp in copies(arrays, flow, (*chip, c), sibling):
                cp.wait_recv()
        for cp in copies(arrays, F_ON, (*on_from, c), (*on_to, c)):
            cp.start()
        for flow, chip in ((F_PASS_X, xn), (F_PASS_Y, yn)):
            for cp in copies(arrays, flow, (*chip, c), sibling):
                cp.start()

    def diagonal(arrays):
        for cp in copies(arrays, F_ON, (*dg, c), sibling):
            cp.wait_recv()
        for cp in copies(arrays, F_PASS_D, (*dg, c), sibling):
            cp.start()

    def from_sibling(arrays, flow, chip):
        for cp in copies(arrays, flow, (*chip, 1 - c), sibling):
            cp.wait_recv()

    def finish(arrays):
        mine = (x, y, c)
        for flow, block in ((F_SIB, mine), (F_X, mine), (F_Y, mine), (F_ON, (*on_from, c)), (F_PASS_X, (*xn, c)),
                            (F_PASS_Y, (*yn, c)), (F_PASS_D, (*dg, c))):
            for cp in copies(arrays, flow, block, sibling):
                cp.wait_send()

    step = s * n_tiles + i
    par = step % 2
    first = step == 0

    @pl.when(first)
    def _():
        mine = blocks(x, y, c)
        mine[0][...] = wint_ref[...].astype(BF16)
        mine[1][...] = wout_ref[...].astype(BF16)
        mine[2][...] = pww_ref[...].astype(BF16)
        mine[3][...] = dw_ref[...]
        start(W_IN_ONLY)

    rows_i = pl.ds(pl.multiple_of(i * tm, tm), tm)

    @pl.when(s == 0)
    def _():
        xt = x_ref[0]
        r1 = lax.rsqrt(jnp.mean(xt * xt, axis=-1, keepdims=True) + RMS_EPS)
        hn_s[rows_i, :] = (xt * r1 * ng_ref[...]).astype(BF16)

    @pl.when(first)
    def _():
        from_sibling(W_IN_ONLY, F_SIB, (x, y))

    @pl.when((s == 1) & (i == 0))
    def _():
        direct(W_IN_ONLY)
        start(OTHERS)
        from_sibling(W_IN_ONLY, F_PASS_Y, yn)

    @pl.when((s == 2) & (i == 0))
    def _():
        from_sibling(W_IN_ONLY, F_PASS_X, xn)
        diagonal(W_IN_ONLY)
        direct(OTHERS)
        from_sibling(W_IN_ONLY, F_PASS_D, dg)

    tiles_per_seq = proj_hbm.shape[1] // tm

    def out_copy(unit, col0):
        dst = proj_hbm.at[i // tiles_per_seq, pl.ds(pl.multiple_of((i % tiles_per_seq) * tm, tm), tm),
                          pl.ds(pl.multiple_of(col0, LANES), SLAB)]
        return pltpu.make_async_copy(stage.at[par, :, pl.ds(unit * SLAB, SLAB)], dst, out_sems.at[par, unit])

    @pl.when(step >= 2)
    def _():
        out_copy(0, 0).wait()

    @pl.when(step >= 2 * n_tiles + 2)
    def _():
        out_copy(1, 0).wait()

    hn = hn_s[rows_i, :]
    for phase, chip in ((0, 2 * x + y), (1, 2 * x + (1 - y))):
        @pl.when(s == phase)
        def _():
            slab = wt_s[pl.ds(pl.multiple_of(chip * SLAB, SLAB), SLAB), :]
            stage[par, :, 0:SLAB] = lax.dot_general(hn, slab, NT_DIMS, preferred_element_type=F32)
            out_copy(0, chip * SLAB).start()

    @pl.when(s == 2)
    def _():
        col0 = (1 - x) * 2 * SLAB
        slab = wt_s[pl.ds(pl.multiple_of(col0, 2 * SLAB), 2 * SLAB), :]
        stage[par] = lax.dot_general(hn, slab, NT_DIMS, preferred_element_type=F32)
        out_copy(0, col0).start()
        out_copy(1, col0 + SLAB).start()

    @pl.when(step == 3 * n_tiles - 1)
    def _():
        out_copy(0, 0).wait()
        out_copy(1, 0).wait()
        pltpu.make_async_copy(stage.at[1 - par, :, pl.ds(0, SLAB)], proj_hbm.at[0, pl.ds(0, tm), pl.ds(0, SLAB)],
                              out_sems.at[1 - par, 0]).wait()
        pltpu.make_async_copy(stage.at[1 - par, :, pl.ds(SLAB, SLAB)], proj_hbm.at[0, pl.ds(0, tm), pl.ds(0, SLAB)],
                              out_sems.at[1 - par, 1]).wait()
        from_sibling(OTHERS, F_SIB, (x, y))
        from_sibling(OTHERS, F_PASS_X, xn)
        from_sibling(OTHERS, F_PASS_Y, yn)
        diagonal(OTHERS)
        from_sibling(OTHERS, F_PASS_D, dg)
        finish(W_IN_ONLY)
        finish(OTHERS)
        wt_ref[...] = wt_s[...]
        woutf_ref[...] = wout_s[...]
        pwwf_ref[...] = pww_s[...]
        for k in range(N_DEV):
            dwf_ref[:, k * DW_BLK:(k + 1) * DW_BLK] = dw_blk[k]


def _inproj(x, norm_g, w_in_t, w_out, pw_w, conv_dw):
    bl, s, _ = x.shape
    tm = TILE_IN
    nt = s // tm
    n_tiles = bl * nt

    def x_map(phase, i):
        i = jnp.where(phase == 0, i, n_tiles - 1)
        return (i // nt, i % nt, 0)

    dma = pltpu.SemaphoreType.DMA
    return pl.pallas_call(
        functools.partial(_inproj_body, n_tiles),
        name="inproj",
        grid=(3, n_tiles),
        in_specs=[pl.BlockSpec((1, tm, D_MODEL), x_map)] + [VMEM] * 5,
        out_specs=[pl.BlockSpec(memory_space=pl.ANY)] + [VMEM] * 4,
        out_shape=[jax.ShapeDtypeStruct((bl, s, D_IN), F32), jax.ShapeDtypeStruct((D_IN, D_MODEL), BF16),
                   jax.ShapeDtypeStruct((D_MODEL, D_MODEL), BF16), jax.ShapeDtypeStruct((W_MIX, W_MIX), BF16),
                   jax.ShapeDtypeStruct((CONV_WIDTH, W_MIX), F32)],
        scratch_shapes=[pltpu.VMEM((n_tiles * tm, D_MODEL), BF16), pltpu.VMEM((D_IN, D_MODEL), BF16),
                        pltpu.VMEM((D_MODEL, D_MODEL), BF16), pltpu.VMEM((W_MIX, W_MIX), BF16),
                        pltpu.VMEM((N_DEV, CONV_WIDTH, DW_BLK), F32), pltpu.VMEM((2, tm, 2 * SLAB), F32),
                        dma((2, 2)), dma((4, 7)), dma((4, 7))],
        compiler_params=pltpu.CompilerParams(dimension_semantics=("arbitrary", "arbitrary"), vmem_limit_bytes=VMEM_LIMIT),
    )(x, _row(norm_g), w_in_t, w_out, pw_w, conv_dw)


B_PLAIN, B_ON, B_MERGED = range(3)


def _routes():
    x, y, c = lax.axis_index("x"), lax.axis_index("y"), lax.axis_index("c")
    xn, yn = (1 - x, y), (x, 1 - y)
    pick = lambda a, b: tuple(jnp.where(c == 1, p, q) for p, q in zip(a, b))
    return pick(xn, yn), pick(yn, xn), (1 - x, 1 - y), (x, y), c


def _chip_index(chip):
    return 2 * chip[0] + chip[1]


def _copy(src, dst, send_sems, recv_sems, a, k, to):
    return pltpu.make_async_remote_copy(src_ref=src, dst_ref=dst, send_sem=send_sems.at[a, k], recv_sem=recv_sems.at[a, k],
                                        device_id=to, device_id_type=MESH)


def _rsb_send(plain, on, rb_plain, rb_on, send_b, recv_b):
    first, _, _, _, c = _routes()
    for k, srcs, dsts in ((B_ON, on, rb_on), (B_PLAIN, plain, rb_plain)):
        for a, (s, d) in enumerate(zip(srcs, dsts)):
            _copy(s, d, send_b, recv_b, a, k, (*first, c)).start()


def _rsb_merge(second_sums, merged, rb_on, rb_merged, send_b, recv_b):
    _, second, _, _, c = _routes()
    for a, (own, m, got, dst) in enumerate(zip(second_sums, merged, rb_on, rb_merged)):
        _copy(got, got, send_b, recv_b, a, B_ON, (*second, c)).wait_recv()
        m[...] = (own[...] + got[...].astype(F32)).astype(BF16)
        _copy(m, dst, send_b, recv_b, a, B_MERGED, (*second, c)).start()


def _rsb_total(own_sums, plain, on, merged, rb_plain, rb_merged, send_b, recv_b):
    _, second, _, _, c = _routes()
    out = []
    for a, (own, p, g) in enumerate(zip(own_sums, rb_plain, rb_merged)):
        _copy(p, p, send_b, recv_b, a, B_PLAIN, (*second, c)).wait_recv()
        _copy(g, g, send_b, recv_b, a, B_MERGED, (*second, c)).wait_recv()
        out.append(own[...] + p[...].astype(F32) + g[...].astype(F32))
    for k, srcs in ((B_PLAIN, plain), (B_ON, on), (B_MERGED, merged)):
        for a, s in enumerate(srcs):
            _copy(s, s, send_b, recv_b, a, k, (*second, c)).wait_send()
    return out


R_LOSS, R_FINAL_G, R_POOL, R_CONV, R_LN = range(5)
VEC_ROWS = 16
DW_ROWS = 32
A_PART, A_X, A_Y, A_ON, A_HALF = range(5)
STEP_SUMS, STEP_MERGE, STEP_HALF = 1, 3, 5


def _bwd2a_body(n_steps, x_ref, dproj_ref, ng_ref, gwout_ref, gpww_ref, loss_ref, gfg_ref, gvec_ref, gdw_ref, gpoolw_ref,
                win_own_ref, win_second_ref, win_plain_ref, win_on_ref, owout_ref, opww_ref, ovec_ref, odw_ref, opoolw_ref,
                acc, a_win, a_wout, a_pww, st_wout, st_pww, rb_wout, rb_pww, part_vec, part_dw, part_pw,
                sib_vec, sib_dw, sib_pw, res_vec, res_dw, res_pw, half_vec, half_dw, half_pw,
                send_a, recv_a, send_b, recv_b, send_r, recv_r, send_w, recv_w):
    step = pl.program_id(0) * pl.num_programs(1) + pl.program_id(1)
    x, y, c = lax.axis_index("x"), lax.axis_index("y"), lax.axis_index("c")
    sibling = (x, y, 1 - c)
    xn, yn = (1 - x, y), (x, 1 - y)
    first, second, diag, own, _ = _routes()
    rows = lambda ref, k, n: ref.at[pl.ds(pl.multiple_of(k * n, n), n), :]

    blocks = lambda k: (rows(gwout_ref, k, W_OUT_BLK), rows(gpww_ref, k, PW_BLK))
    landed = lambda chip: (a_wout.at[chip], a_pww.at[chip])
    staged = lambda k: (st_wout.at[k], st_pww.at[k])
    arrived = lambda k: (rb_wout.at[k], rb_pww.at[k])

    parts, sibs, halves = (part_vec, part_dw, part_pw), (sib_vec, sib_dw, sib_pw), (half_vec, half_dw, half_pw)
    results = (res_vec, res_dw, res_pw)
    half_rows = lambda ref: pl.ds(pl.multiple_of(c * (ref.shape[0] // 2), SUBLANES), ref.shape[0] // 2)
    slot = lambda chip: tuple(h.at[_chip_index(chip)] for h in halves)

    @pl.when(step == 0)
    def _():
        acc[...] = jnp.zeros_like(acc)
        sum8 = lambda v: jnp.sum(v, axis=0, keepdims=True)
        part_vec[...] = jnp.zeros((VEC_ROWS, D_MODEL), F32)
        part_vec[R_LOSS:R_LOSS + 1, :] = jnp.broadcast_to(loss_ref[0:1, 0:1], (1, D_MODEL))
        part_vec[R_FINAL_G:R_FINAL_G + 1, :] = sum8(gfg_ref[...])
        for row, (lo, hi) in ((R_POOL, (V_POOL_B, V_POOL_SCALE)), (R_CONV, (V_CONV_B, V_LN_G)), (R_LN, (V_LN_B, V_PW_B))):
            part_vec[row:row + 1, 0:W_MIX] = sum8(gvec_ref[lo])
            part_vec[row:row + 1, W_MIX:] = sum8(gvec_ref[hi])
        part_dw[0:CONV_WIDTH, :] = jnp.sum(gdw_ref[...], axis=1)
        part_dw[CONV_WIDTH:, :] = jnp.zeros((DW_ROWS - CONV_WIDTH, W_MIX), F32)
        part_pw[...] = gpoolw_ref[...].reshape(len(POOL_WINDOWS) * POOL_GC, POOL_GC)
        for a, (p, s) in enumerate(zip(parts, sibs)):
            _copy(p, s, send_r, recv_r, a, A_PART, sibling).start()
        for chip in range(4):
            for a, (s, d) in enumerate(zip(blocks(2 * chip + (1 - c)), landed(chip))):
                _copy(s, d, send_a, recv_a, a, chip, sibling).start()

    x_t = x_ref[0]
    r1 = lax.rsqrt(jnp.mean(x_t * x_t, axis=-1, keepdims=True) + RMS_EPS)
    hn = (x_t * r1 * ng_ref[...]).astype(BF16)
    acc[...] += lax.dot_general(dproj_ref[0], hn, TN_DIMS, preferred_element_type=F32)

    @pl.when(step == STEP_SUMS)
    def _():
        for a, (p, s, h) in enumerate(zip(parts, sibs, slot(own))):
            _copy(p, s, send_r, recv_r, a, A_PART, sibling).wait_recv()
            h[...] = p[half_rows(p), :] + s[half_rows(s), :]
            for k, to in ((A_X, xn), (A_Y, yn)):
                _copy(h, h, send_r, recv_r, a, k, (*to, c)).start()
        for chip in range(4):
            for a, (m, l) in enumerate(zip(blocks(2 * chip + c), landed(chip))):
                _copy(l, l, send_a, recv_a, a, chip, sibling).wait_recv()
                l[...] = m[...] + l[...]
        for k, chip in ((B_ON, diag), (B_PLAIN, first)):
            for s, l in zip(staged(k), landed(_chip_index(chip))):
                s[...] = l[...].astype(BF16)
        _rsb_send(staged(B_PLAIN), staged(B_ON), arrived(B_PLAIN), arrived(B_ON), send_b, recv_b)

    @pl.when(step == STEP_MERGE)
    def _():
        _rsb_merge(landed(_chip_index(second)), staged(B_MERGED), arrived(B_ON), arrived(B_MERGED), send_b, recv_b)
        for a in range(3):
            for k, chip in ((A_X, xn), (A_Y, yn)):
                h = slot(chip)[a]
                _copy(h, h, send_r, recv_r, a, k, sibling).wait_recv()
            h = slot(first)[a]
            _copy(h, h, send_r, recv_r, a, A_ON, (*second, c)).start()

    @pl.when(step == STEP_HALF)
    def _():
        for a, (r, hs) in enumerate(zip(results, halves)):
            h = slot(diag)[a]
            _copy(h, h, send_r, recv_r, a, A_ON, sibling).wait_recv()
            mine = r.at[half_rows(r), :]
            mine[...] = (hs[0] + hs[1]) + (hs[2] + hs[3])
            _copy(mine, mine, send_r, recv_r, a, A_HALF, sibling).start()

    @pl.when(step == n_steps - 1)
    def _():
        block = lambda k: rows(acc, k, W_IN_BLK)
        w_copy = lambda chip: _copy(block(2 * chip + (1 - c)), a_win.at[chip], send_w, recv_w, 0, chip, sibling)
        for chip in range(4):
            w_copy(chip).start()

        owout_ref[...], opww_ref[...] = _rsb_total(landed(_chip_index(own)), staged(B_PLAIN), staged(B_ON), staged(B_MERGED),
                                                   arrived(B_PLAIN), arrived(B_MERGED), send_b, recv_b)
        for a, (r, p, s) in enumerate(zip(results, parts, sibs)):
            theirs = r.at[pl.ds(pl.multiple_of((1 - c) * (r.shape[0] // 2), SUBLANES), r.shape[0] // 2), :]
            _copy(theirs, theirs, send_r, recv_r, a, A_HALF, sibling).wait_recv()
            _copy(p, s, send_r, recv_r, a, A_PART, sibling).wait_send()
            for k, h in ((A_X, slot(own)[a]), (A_Y, slot(own)[a]), (A_ON, slot(first)[a])):
                _copy(h, h, send_r, recv_r, a, k, sibling).wait_send()
            mine = r.at[half_rows(r), :]
            _copy(mine, mine, send_r, recv_r, a, A_HALF, sibling).wait_send()
        for chip in range(4):
            for a, (s, d) in enumerate(zip(blocks(2 * chip + (1 - c)), landed(chip))):
                _copy(s, d, send_a, recv_a, a, chip, sibling).wait_send()
        ovec_ref[...] = res_vec[...]
        opoolw_ref[...] = res_pw[...]
        for k in range(N_DEV):
            @pl.when(4 * x + 2 * y + c == k)
            def _():
                odw_ref[...] = res_dw[0:CONV_WIDTH, k * DW_BLK:(k + 1) * DW_BLK]

        for chip in range(4):
            w_copy(chip).wait()
        chip_sum = lambda chip: block(2 * _chip_index(chip) + c)[...] + a_win[_chip_index(chip)]
        win_own_ref[...] = chip_sum(own)
        win_second_ref[...] = chip_sum(second)
        win_plain_ref[...] = chip_sum(first).astype(BF16)
        win_on_ref[...] = chip_sum(diag).astype(BF16)


def _bwd2a(x, dproj, norm_g, gwout, gpww, loss8, gfg8, gvec, gdw8, gpoolw):
    bl, s, _ = x.shape
    tm = TILE_M
    nt = s // tm
    tok = lambda n: pl.BlockSpec((1, tm, n), lambda b, i: (b, i, 0))
    n_pw = len(POOL_WINDOWS) * POOL_GC
    win = (W_IN_BLK, D_MODEL)
    blk = ((W_OUT_BLK, D_MODEL), (PW_BLK, W_MIX))
    rep = ((VEC_ROWS, D_MODEL), (DW_ROWS, W_MIX), (n_pw, POOL_GC))
    half = lambda sh: (sh[0] // 2,) + sh[1:]
    dma = pltpu.SemaphoreType.DMA
    assert bl * nt > STEP_HALF + 1, "the exchanges' phases need their grid steps"
    return pl.pallas_call(
        functools.partial(_bwd2a_body, bl * nt),
        name="bwd2a",
        grid=(bl, nt),
        in_specs=[tok(D_MODEL), tok(D_IN)] + [VMEM] * 8,
        out_specs=[VMEM] * 9,
        out_shape=[jax.ShapeDtypeStruct(win, F32), jax.ShapeDtypeStruct(win, F32), jax.ShapeDtypeStruct(win, BF16),
                   jax.ShapeDtypeStruct(win, BF16)]
        + [jax.ShapeDtypeStruct(sh, F32) for sh in blk + (rep[0], (CONV_WIDTH, DW_BLK), rep[2])],
        scratch_shapes=[pltpu.VMEM((D_IN, D_MODEL), F32), pltpu.VMEM((4,) + win, F32)]
        + [pltpu.VMEM((4,) + sh, F32) for sh in blk]
        + [pltpu.VMEM((3,) + sh, BF16) for sh in blk] * 2
        + [pltpu.VMEM(sh, F32) for sh in rep] * 3
        + [pltpu.VMEM((4,) + half(sh), F32) for sh in rep]
        + [dma((2, 4)), dma((2, 4)), dma((2, 3)), dma((2, 3)), dma((3, 5)), dma((3, 5)), dma((1, 4)), dma((1, 4))],
        compiler_params=pltpu.CompilerParams(dimension_semantics=("arbitrary", "arbitrary"), vmem_limit_bytes=VMEM_LIMIT),
    )(x, dproj, _row(norm_g), gwout, gpww, loss8, gfg8, gvec, gdw8, gpoolw)


def _bwd2b_body(n_steps, x_ref, dh_ref, dproj_ref, ng_ref, wint_ref, win_own_ref, win_second_ref, win_plain_hbm, win_on_hbm,
                gx_ref, owin_ref, ong_ref,
                gng, merged, rb_win, all_ng, send_b, recv_b, send_n, recv_n):
    step = pl.program_id(0) * pl.num_programs(1) + pl.program_id(1)
    x, y, c = lax.axis_index("x"), lax.axis_index("y"), lax.axis_index("c")
    sibling = (x, y, 1 - c)
    other_chips = [(1 - x, y), (x, 1 - y), (1 - x, 1 - y)]
    arrived = lambda k: (rb_win.at[k],)

    @pl.when(step == 0)
    def _():
        gng[...] = jnp.zeros_like(gng)
        _rsb_send((win_plain_hbm,), (win_on_hbm,), arrived(B_PLAIN), arrived(B_ON), send_b, recv_b)

    x_t = x_ref[0]
    r1 = lax.rsqrt(jnp.mean(x_t * x_t, axis=-1, keepdims=True) + RMS_EPS)
    xr = x_t * r1
    dhn = jnp.dot(dproj_ref[0], wint_ref[...], preferred_element_type=F32)
    gng[...] += _colsum8(dhn * xr)
    gy = dhn * ng_ref[...]
    gx_ref[0] = dh_ref[0] + r1 * (gy - xr * jnp.mean(gy * xr, axis=-1, keepdims=True))

    @pl.when(step == STEP_MERGE)
    def _():
        _rsb_merge((win_second_ref,), (merged,), arrived(B_ON), arrived(B_MERGED), send_b, recv_b)

    k_me = 4 * x + 2 * y + c
    peers = [sibling] + [(*chip, c) for chip in other_chips] + [(*chip, 1 - c) for chip in other_chips]
    ng_copy = lambda r: pltpu.make_async_remote_copy(src_ref=all_ng.at[k_me], dst_ref=all_ng.at[k_me], send_sem=send_n.at[r],
                                                     recv_sem=recv_n.at[r], device_id=peers[r], device_id_type=MESH)

    @pl.when(step == n_steps - 1)
    def _():
        all_ng[k_me] = gng[...]
        for r in range(N_PEER):
            ng_copy(r).start()
        (owin_ref[...],) = _rsb_total((win_own_ref,), (win_plain_hbm,), (win_on_hbm,), (merged,), arrived(B_PLAIN),
                                      arrived(B_MERGED), send_b, recv_b)
        for r in range(N_PEER):
            ng_copy(r).wait()
        s = all_ng[0]
        for k in range(1, N_DEV):
            s = s + all_ng[k]
        ong_ref[...] = jnp.broadcast_to(jnp.sum(s, axis=0, keepdims=True), ong_ref.shape)


def _bwd2b(x, dh, dproj, norm_g, w_in_t, win_own, win_second, win_plain, win_on):
    bl, s, _ = x.shape
    tm = TILE_M
    nt = s // tm
    tok = lambda n: pl.BlockSpec((1, tm, n), lambda b, i: (b, i, 0))
    win = (W_IN_BLK, D_MODEL)
    hbm = pl.BlockSpec(memory_space=pl.ANY)
    dma = pltpu.SemaphoreType.DMA
    return pl.pallas_call(
        functools.partial(_bwd2b_body, bl * nt),
        name="bwd2b",
        grid=(bl, nt),
        in_specs=[tok(D_MODEL), tok(D_MODEL), tok(D_IN), VMEM, VMEM, VMEM, VMEM, hbm, hbm],
        out_specs=[tok(D_MODEL), VMEM, VMEM],
        out_shape=[jax.ShapeDtypeStruct((bl, s, D_MODEL), F32), jax.ShapeDtypeStruct(win, F32),
                   jax.ShapeDtypeStruct((SUBLANES, D_MODEL), F32)],
        scratch_shapes=[pltpu.VMEM((SUBLANES, D_MODEL), F32), pltpu.VMEM(win, BF16), pltpu.VMEM((3,) + win, BF16),
                        pltpu.VMEM((N_DEV, SUBLANES, D_MODEL), F32), dma((1, 3)), dma((1, 3)), dma((N_PEER,)), dma((N_PEER,))],
        compiler_params=pltpu.CompilerParams(dimension_semantics=("arbitrary", "arbitrary"), vmem_limit_bytes=VMEM_LIMIT),
    )(x, dh, dproj, _row(norm_g), w_in_t, win_own, win_second, win_plain, win_on)


ADAM_LR = 0.001
ADAM_B1 = 0.9
ADAM_B2 = 0.999
ADAM_EPS = 1e-08
ADAM_WD = 0.01
ADAM_STEP = 10


PARAMS = ("norm_g", "w_in", "pool_w", "pool_b", "pool_scale", "conv_dw", "conv_b", "ln_g", "ln_b", "pw_w", "pw_b", "w_out",
          "final_g")


def _adamw_body(*refs):
    n = len(PARAMS)
    ws, ms, vs = (dict(zip(PARAMS, refs[k * n:(k + 1) * n])) for k in range(3))
    gwin_ref, gwout_ref, gpww_ref, vec_ref, gdw_ref, gpoolw_ref, gng_ref = refs[3 * n:3 * n + 7]
    outs = refs[3 * n + 7:]
    gs, ds, m2s, v2s = (dict(zip(PARAMS, outs[k * n:(k + 1) * n])) for k in range(4))
    loss_ref = outs[4 * n]

    def update(name, g, at=slice(None)):
        m = ADAM_B1 * ms[name][at] + (1.0 - ADAM_B1) * g
        v = ADAM_B2 * vs[name][at] + (1.0 - ADAM_B2) * (g * g)
        m_hat = m / (1.0 - ADAM_B1 ** ADAM_STEP)
        v_hat = v / (1.0 - ADAM_B2 ** ADAM_STEP)
        gs[name][at] = g
        ds[name][at] = -ADAM_LR * (m_hat / (jnp.sqrt(v_hat) + ADAM_EPS) + ADAM_WD * ws[name][at])
        m2s[name][at] = m
        v2s[name][at] = v

    row = lambda r, lo, hi: vec_ref[r:r + 1, lo:hi]
    loss_ref[...] = row(R_LOSS, 0, 1)
    update("norm_g", gng_ref[0:1, :])
    update("w_in", gwin_ref[...])
    update("pool_w", gpoolw_ref[...].reshape(ws["pool_w"].shape))
    for g in range(len(POOL_WINDOWS)):
        update("pool_b", row(R_POOL, g * POOL_GC, (g + 1) * POOL_GC), at=slice(g, g + 1))
    update("pool_scale", row(R_POOL, W_MIX, 2 * W_MIX))
    update("conv_dw", gdw_ref[...])
    update("conv_b", row(R_CONV, 0, W_MIX))
    update("ln_g", row(R_CONV, W_MIX, 2 * W_MIX))
    update("ln_b", row(R_LN, 0, W_MIX))
    update("pw_w", gpww_ref[...])
    update("pw_b", row(R_LN, W_MIX, 2 * W_MIX))
    update("w_out", gwout_ref[...])
    update("final_g", row(R_FINAL_G, 0, D_MODEL))


def _adamw(ws, ms, vs, grad_sources):
    n = len(PARAMS)
    shapes = [ws[p].shape for p in PARAMS]
    flat = lambda a: a.reshape(1, -1) if a.ndim == 1 else a
    args = [flat(d[p]) for d in (ws, ms, vs) for p in PARAMS]
    outs = pl.pallas_call(
        functools.partial(_adamw_body),
        name="adamw",
        in_specs=[VMEM] * (3 * n + len(grad_sources)),
        out_specs=[VMEM] * (4 * n + 1),
        out_shape=[jax.ShapeDtypeStruct(a.shape, F32) for a in args[:n]] * 4 + [jax.ShapeDtypeStruct((1, 1), F32)],
        compiler_params=pltpu.CompilerParams(vmem_limit_bytes=VMEM_LIMIT),
    )(*args, *grad_sources)
    loss = outs[4 * n].reshape(())
    outs = [o.reshape(shapes[k % n]) for k, o in enumerate(outs[:4 * n])]
    return loss, outs[:n], outs[n:2 * n], outs[2 * n:3 * n], outs[3 * n:]


def kernel(x, norm_g, w_in, pool_w, pool_b, pool_scale, conv_dw, conv_b, ln_g, ln_b, pw_w, pw_b, w_out, final_g, loss_target, m_norm_g, m_w_in, m_pool_w, m_pool_b, m_pool_scale, m_conv_dw, m_conv_b, m_ln_g, m_ln_b, m_pw_w, m_pw_b, m_w_out, m_final_g, v_norm_g, v_w_in, v_pool_w, v_pool_b, v_pool_scale, v_conv_dw, v_conv_b, v_ln_g, v_ln_b, v_pw_w, v_pw_b, v_w_out, v_final_g):
    proj, w_in_t, w_out_b, pw_w_b, conv_dw_f = _inproj(x, norm_g, w_in.T, w_out, pw_w, conv_dw)
    h1, z, ob, dpool, dh, loss8, gfg8 = _fwd(x, loss_target, proj, pool_w, pool_b, pool_scale, conv_dw_f,
                                             conv_b, ln_g, ln_b, pw_w_b, pw_b, w_out_b, final_g)
    dproj, gwout, gpww, gpoolw, gvec, gdw8 = _bwd1(dh, proj, h1, z, ob, dpool, w_out_b, pw_w_b, pool_w, pool_scale,
                                                   conv_dw_f, ln_g, ln_b)
    win_sums = _bwd2a(x, dproj, norm_g, gwout, gpww, loss8, gfg8, gvec, gdw8, gpoolw)
    g_w_out, g_pw_w, vec, g_conv_dw, poolw_sum = win_sums[4:]
    grad_x, g_w_in_t, g_norm_g = _bwd2b(x, dh, dproj, norm_g, w_in_t, *win_sums[:4])

    weights = dict(norm_g=norm_g, w_in=w_in.T, pool_w=pool_w, pool_b=pool_b, pool_scale=pool_scale, conv_dw=conv_dw, conv_b=conv_b,
                   ln_g=ln_g, ln_b=ln_b, pw_w=pw_w, pw_b=pw_b, w_out=w_out, final_g=final_g)
    m_in = dict(norm_g=m_norm_g, w_in=m_w_in.T, pool_w=m_pool_w, pool_b=m_pool_b, pool_scale=m_pool_scale, conv_dw=m_conv_dw,
                conv_b=m_conv_b, ln_g=m_ln_g, ln_b=m_ln_b, pw_w=m_pw_w, pw_b=m_pw_b, w_out=m_w_out, final_g=m_final_g)
    v_in = dict(norm_g=v_norm_g, w_in=v_w_in.T, pool_w=v_pool_w, pool_b=v_pool_b, pool_scale=v_pool_scale, conv_dw=v_conv_dw,
                conv_b=v_conv_b, ln_g=v_ln_g, ln_b=v_ln_b, pw_w=v_pw_w, pw_b=v_pw_b, w_out=v_w_out, final_g=v_final_g)
    loss, *groups = _adamw(weights, m_in, v_in, (g_w_in_t, g_w_out, g_pw_w, vec, g_conv_dw, poolw_sum, g_norm_g))
    k_in = PARAMS.index("w_in")
    for group in groups:
        group[k_in] = group[k_in].T
    grads, delta, new_m, new_v = groups
    return (loss, grad_x, *grads, *delta, *new_m, *new_v)
```

```python
import functools

import jax
import jax.numpy as jnp
from jax import lax
from jax.experimental import pallas as pl
from jax.experimental.pallas import tpu as pltpu

F32 = jnp.float32
BF16 = jnp.bfloat16

D_MODEL = 1024
W_MIX = 512
D_IN = 5 * W_MIX
POOL_WINDOWS = (2, 4, 8, 16)
POOL_GC = 128
CONV_WIDTH = 31
RMS_EPS = 1e-6
LN_EPS = 1e-5
N_DEV = 8

LANES = 128
SUBLANES = 8
TILE_M = 512
TILE_IN = 1024
POOL_HALO = 16
CONV_HALO = 32
CONV_ROWS = 128
BWD_CONV_ROWS = 64
BWD_SUBTILES = 2
VMEM_LIMIT = 56 * 1024 * 1024

VMEM = pl.BlockSpec(memory_space=pltpu.VMEM)


def _silu(v):
    return v * jax.nn.sigmoid(v)


def _colsum8(v):
    m, n = v.shape
    return jnp.sum(v.reshape(m // SUBLANES, SUBLANES, n), axis=0)


def _row_index(shape, t0):
    return lax.broadcasted_iota(jnp.int32, shape, 0) + t0


def _conv_taps():
    taps = {s: [] for s in range(SUBLANES)}
    for k in range(CONV_WIDTH):
        e = k + CONV_HALO - (CONV_WIDTH - 1)
        taps[e % SUBLANES].append((e // SUBLANES, k))
    return taps


def _fwd_body(x_ref, tgt_ref, proj_ref, poolw_ref, poolb_ref, pools_ref, dw_ref, cb_ref, lg_ref, lb_ref,
              pww_ref, pwb_ref, wout_ref, fg_ref,
              h1_ref, z_ref, ob_ref, dpool_ref, dh_ref, loss_ref, gfg_ref,
              uext, hext, y_s):
    b = pl.program_id(0)
    i = pl.program_id(1)
    tm = TILE_M

    @pl.when(i == 0)
    def _():
        uext[0:POOL_HALO, :] = jnp.zeros((POOL_HALO, W_MIX), F32)
        hext[0:CONV_HALO, :] = jnp.zeros((CONV_HALO, W_MIX), F32)
        hext[CONV_HALO + tm:, :] = jnp.zeros((SUBLANES, W_MIX), F32)

    @pl.when((b == 0) & (i == 0))
    def _():
        loss_ref[...] = jnp.zeros_like(loss_ref)
        gfg_ref[...] = jnp.zeros_like(gfg_ref)

    uext[POOL_HALO:, :] = proj_ref[0, :, 0:W_MIX]
    t = _row_index((tm, POOL_GC), i * tm)
    for g, w in enumerate(POOL_WINDOWS):
        lanes = slice(g * POOL_GC, (g + 1) * POOL_GC)
        e = uext[:, lanes]
        s = e
        sh = 1
        while sh < w:
            s = s + pltpu.roll(s, sh, 0)
            sh *= 2
        inv_cnt = 1.0 / jnp.minimum(t + 1, w).astype(F32)
        d = (s[POOL_HALO:] * inv_cnt - e[POOL_HALO:]).astype(BF16)
        dpool_ref[0, :, lanes] = d
        z = jnp.dot(d, poolw_ref[g].astype(BF16), preferred_element_type=F32) + poolb_ref[:, lanes]
        z_ref[0, :, lanes] = z
        a_gate = proj_ref[0, :, W_MIX + g * POOL_GC:W_MIX + (g + 1) * POOL_GC]
        y_s[:, lanes] = (z * pools_ref[:, lanes] * _silu(a_gate)).astype(BF16)

    hext[CONV_HALO:CONV_HALO + tm, :] = proj_ref[0, :, 2 * W_MIX:3 * W_MIX] * jax.nn.sigmoid(proj_ref[0, :, 3 * W_MIX:4 * W_MIX])
    taps = _conv_taps()
    for c0 in range(0, tm, CONV_ROWS):
        for l0 in range(0, W_MIX, LANES):
            lanes = slice(l0, l0 + LANES)
            acc = jnp.zeros((CONV_ROWS, LANES), F32) + cb_ref[:, lanes]
            for s in range(SUBLANES):
                part = jnp.zeros((CONV_ROWS + SUBLANES, LANES), F32)
                for q, k in taps[s]:
                    r0 = c0 + SUBLANES * q
                    part = part + hext[r0:r0 + CONV_ROWS + SUBLANES, lanes] * dw_ref[k:k + 1, lanes]
                acc = acc + part[s:s + CONV_ROWS]
            h1_ref[0, c0:c0 + CONV_ROWS, lanes] = acc
    h1 = h1_ref[0]
    mu = jnp.mean(h1, axis=-1, keepdims=True)
    xc = h1 - mu
    rstd = lax.rsqrt(jnp.mean(xc * xc, axis=-1, keepdims=True) + LN_EPS)
    h3 = _silu(xc * rstd * lg_ref[...] + lb_ref[...])
    ob = jnp.dot(h3.astype(BF16), pww_ref[...], preferred_element_type=F32) + pwb_ref[...]
    ob_ref[0] = ob
    y_s[:, W_MIX:] = (ob * _silu(proj_ref[0, :, 4 * W_MIX:5 * W_MIX])).astype(BF16)

    h = x_ref[0] + jnp.dot(y_s[...], wout_ref[...], preferred_element_type=F32)
    r2 = lax.rsqrt(jnp.mean(h * h, axis=-1, keepdims=True) + RMS_EPS)
    hr = h * r2
    err = hr * fg_ref[...] - tgt_ref[0]
    loss_ref[...] += jnp.sum(err * err) * (0.5 / D_MODEL)
    dout = err * (1.0 / D_MODEL)
    gfg_ref[...] += _colsum8(dout * hr)
    gy = dout * fg_ref[...]
    dh_ref[0] = r2 * (gy - hr * jnp.mean(gy * hr, axis=-1, keepdims=True))

    uext[0:POOL_HALO, :] = uext[tm:tm + POOL_HALO, :]
    hext[0:CONV_HALO, :] = hext[tm:tm + CONV_HALO, :]


def _row(v):
    return v.reshape(1, -1)


def _fwd(x, tgt, proj, pool_w, pool_b, pool_scale, conv_dw_f, conv_b, ln_g, ln_b, pw_w_b, pw_b, w_out_b, final_g):
    bl, s, _ = x.shape
    tm = TILE_M
    tok = lambda n: pl.BlockSpec((1, tm, n), lambda b, i: (b, i, 0))
    acc = lambda n: pl.BlockSpec((SUBLANES, n), lambda b, i: (0, 0))
    act = lambda n: jax.ShapeDtypeStruct((bl, s, n), F32)
    return pl.pallas_call(
        functools.partial(_fwd_body),
        name="fwd",
        grid=(bl, s // tm),
        in_specs=[tok(D_MODEL), tok(D_MODEL), tok(D_IN)] + [VMEM] * 11,
        out_specs=[tok(W_MIX), tok(W_MIX), tok(W_MIX), tok(W_MIX), tok(D_MODEL), acc(LANES), acc(D_MODEL)],
        out_shape=[act(W_MIX), act(W_MIX), act(W_MIX), jax.ShapeDtypeStruct((bl, s, W_MIX), BF16), act(D_MODEL),
                   jax.ShapeDtypeStruct((SUBLANES, LANES), F32), jax.ShapeDtypeStruct((SUBLANES, D_MODEL), F32)],
        scratch_shapes=[pltpu.VMEM((POOL_HALO + tm, W_MIX), F32), pltpu.VMEM((CONV_HALO + tm + SUBLANES, W_MIX), F32),
                        pltpu.VMEM((tm, D_MODEL), BF16)],
        compiler_params=pltpu.CompilerParams(dimension_semantics=("arbitrary", "arbitrary"), vmem_limit_bytes=VMEM_LIMIT),
    )(x, tgt, proj, pool_w, pool_b.reshape(1, W_MIX), _row(pool_scale), conv_dw_f, _row(conv_b),
      _row(ln_g), _row(ln_b), pw_w_b, _row(pw_b), w_out_b, _row(final_g))


NT_DIMS = (((1,), (1,)), ((), ()))
TN_DIMS = (((0,), (0,)), ((), ()))
V_POOL_B, V_POOL_SCALE, V_CONV_B, V_LN_G, V_LN_B, V_PW_B = range(6)


def _bwd_taps():
    taps = {s: [] for s in range(SUBLANES)}
    for k in range(CONV_WIDTH):
        e = CONV_WIDTH - 1 - k
        taps[e % SUBLANES].append((e // SUBLANES, k))
    return taps


def _bwd1_body(n_tiles, dh_ref, proj_ref, h1_ref, z_ref, ob_ref, dpool_ref, wout_ref, pww_ref, poolw_ref, pools_ref, dw_ref,
               lg_ref, lb_ref,
               dproj_ref, gwout_ref, gpww_ref, gpoolw_ref, gvec_ref, gdw_ref,
               vext, gext, gsh, y_s):
    b = pl.program_id(0)
    i = pl.program_id(1)
    tm = TILE_M
    t0 = (n_tiles - 1 - i) * tm

    @pl.when(i == 0)
    def _():
        vext[tm:, :] = jnp.zeros((POOL_HALO, W_MIX), F32)
        gext[tm:, :] = jnp.zeros((CONV_HALO, W_MIX), F32)

    @pl.when((b == 0) & (i == 0))
    def _():
        gwout_ref[...] = jnp.zeros_like(gwout_ref)
        gpww_ref[...] = jnp.zeros_like(gpww_ref)
        gpoolw_ref[...] = jnp.zeros_like(gpoolw_ref)
        gvec_ref[...] = jnp.zeros_like(gvec_ref)
        gdw_ref[...] = jnp.zeros_like(gdw_ref)

    sub = tm // BWD_SUBTILES

    def dense_phase(r0):
        rs = slice(r0, r0 + sub)
        dhb = dh_ref[0, rs, :].astype(BF16)
        dy_a = lax.dot_general(dhb, wout_ref[0:W_MIX, :], NT_DIMS, preferred_element_type=F32)
        dy_b = lax.dot_general(dhb, wout_ref[W_MIX:, :], NT_DIMS, preferred_element_type=F32)

        a_gate = proj_ref[0, rs, W_MIX:2 * W_MIX]
        sig_a = jax.nn.sigmoid(a_gate)
        silu_a = a_gate * sig_a
        z = z_ref[0, rs, :]
        out_a = z * pools_ref[...]
        y_s[rs, 0:W_MIX] = (out_a * silu_a).astype(BF16)
        d_out_a = dy_a * silu_a
        d_ag = dy_a * out_a * (sig_a * (1.0 + a_gate * (1.0 - sig_a)))
        dproj_ref[0, rs, W_MIX:2 * W_MIX] = d_ag.astype(BF16)
        gvec_ref[V_POOL_SCALE] += _colsum8(d_out_a * z)
        dz = d_out_a * pools_ref[...]
        gvec_ref[V_POOL_B] += _colsum8(dz)
        t = _row_index((sub, POOL_GC), t0 + r0)
        n_ext = sub + POOL_HALO
        for g, w in enumerate(POOL_WINDOWS):
            lanes = slice(g * POOL_GC, (g + 1) * POOL_GC)
            dzg = dz[:, lanes].astype(BF16)
            gpoolw_ref[g] += lax.dot_general(dpool_ref[0, rs, lanes], dzg, TN_DIMS, preferred_element_type=F32)
            dd = lax.dot_general(dzg, poolw_ref[g].astype(BF16), NT_DIMS, preferred_element_type=F32)
            vext[rs, lanes] = dd * (1.0 / jnp.minimum(t + 1, w).astype(F32))
            s = vext[r0:r0 + n_ext, lanes]
            sh = 1
            while sh < w:
                s = s + pltpu.roll(s, n_ext - sh, 0)
                sh *= 2
            dproj_ref[0, rs, lanes] = (s[0:sub] - dd).astype(BF16)

        b_gate = proj_ref[0, rs, 4 * W_MIX:5 * W_MIX]
        sig_b = jax.nn.sigmoid(b_gate)
        silu_b = b_gate * sig_b
        ob = ob_ref[0, rs, :]
        y_s[rs, W_MIX:] = (ob * silu_b).astype(BF16)
        d_ob = dy_b * silu_b
        dproj_ref[0, rs, 4 * W_MIX:5 * W_MIX] = (dy_b * ob * (sig_b * (1.0 + b_gate * (1.0 - sig_b)))).astype(BF16)
        gvec_ref[V_PW_B] += _colsum8(d_ob)
        gwout_ref[...] += lax.dot_general(y_s[rs, :], dhb, TN_DIMS, preferred_element_type=F32)

        h1 = h1_ref[0, rs, :]
        mu = jnp.mean(h1, axis=-1, keepdims=True)
        xc = h1 - mu
        rstd = lax.rsqrt(jnp.mean(xc * xc, axis=-1, keepdims=True) + LN_EPS)
        xhat = xc * rstd
        h2 = xhat * lg_ref[...] + lb_ref[...]
        sig2 = jax.nn.sigmoid(h2)
        d_obb = d_ob.astype(BF16)
        gpww_ref[...] += lax.dot_general((h2 * sig2).astype(BF16), d_obb, TN_DIMS, preferred_element_type=F32)
        dh2 = lax.dot_general(d_obb, pww_ref[...], NT_DIMS, preferred_element_type=F32) * (sig2 * (1.0 + h2 * (1.0 - sig2)))
        gvec_ref[V_LN_G] += _colsum8(dh2 * xhat)
        gvec_ref[V_LN_B] += _colsum8(dh2)
        dxh = dh2 * lg_ref[...]
        dh1 = rstd * (dxh - jnp.mean(dxh, axis=-1, keepdims=True) - xhat * jnp.mean(dxh * xhat, axis=-1, keepdims=True))
        gvec_ref[V_CONV_B] += _colsum8(dh1)
        gext[rs, :] = dh1

    taps = _bwd_taps()
    cr = BWD_CONV_ROWS
    win = cr + CONV_HALO - SUBLANES

    def conv_phase(r0):
        for c0 in range(r0, r0 + sub, cr):
            for l0 in range(0, W_MIX, LANES):
                lanes = slice(l0, l0 + LANES)
                b_val = proj_ref[0, c0:c0 + cr, 2 * W_MIX + l0:2 * W_MIX + l0 + LANES]
                sg = jax.nn.sigmoid(proj_ref[0, c0:c0 + cr, 3 * W_MIX + l0:3 * W_MIX + l0 + LANES])
                h0c = b_val * sg
                acc = jnp.zeros((cr, LANES), F32)
                for s in range(SUBLANES):
                    gsh[...] = gext[c0 + s:c0 + s + win, lanes]
                    for q, k in taps[s]:
                        gq = gsh[SUBLANES * q:SUBLANES * q + cr, :]
                        acc = acc + gq * dw_ref[k:k + 1, lanes]
                        gdw_ref[k, :, lanes] += _colsum8(h0c * gq)
                d_bval = acc * sg
                dproj_ref[0, c0:c0 + cr, 2 * W_MIX + l0:2 * W_MIX + l0 + LANES] = d_bval.astype(BF16)
                dproj_ref[0, c0:c0 + cr, 3 * W_MIX + l0:3 * W_MIX + l0 + LANES] = (d_bval * b_val * (1.0 - sg)).astype(BF16)

    starts = [k * sub for k in reversed(range(BWD_SUBTILES))]
    for r0 in starts:
        dense_phase(r0)
    for r0 in starts:
        conv_phase(r0)
    vext[tm:, :] = vext[0:POOL_HALO, :]
    gext[tm:, :] = gext[0:CONV_HALO, :]


def _bwd1(dh, proj, h1, z, ob, dpool, w_out_b, pw_w_b, pool_w, pool_scale, conv_dw_f, ln_g, ln_b):
    bl, s, _ = dh.shape
    tm = TILE_M
    nt = s // tm
    tok = lambda n: pl.BlockSpec((1, tm, n), lambda b, i: (b, nt - 1 - i, 0))
    res = lambda shape: pl.BlockSpec(shape, lambda b, i: (0,) * len(shape))
    out_shapes = [(D_MODEL, D_MODEL), (W_MIX, W_MIX), (len(POOL_WINDOWS), POOL_GC, POOL_GC), (6, SUBLANES, W_MIX),
                  (CONV_WIDTH, SUBLANES, W_MIX)]
    return pl.pallas_call(
        functools.partial(_bwd1_body, nt),
        name="bwd1",
        grid=(bl, nt),
        in_specs=[tok(D_MODEL), tok(D_IN), tok(W_MIX), tok(W_MIX), tok(W_MIX), tok(W_MIX)] + [VMEM] * 7,
        out_specs=[tok(D_IN)] + [res(sh) for sh in out_shapes],
        out_shape=[jax.ShapeDtypeStruct((bl, s, D_IN), BF16)] + [jax.ShapeDtypeStruct(sh, F32) for sh in out_shapes],
        scratch_shapes=[pltpu.VMEM((tm + POOL_HALO, W_MIX), F32), pltpu.VMEM((tm + CONV_HALO, W_MIX), F32),
                        pltpu.VMEM((BWD_CONV_ROWS + CONV_HALO - SUBLANES, LANES), F32), pltpu.VMEM((tm, D_MODEL), BF16)],
        compiler_params=pltpu.CompilerParams(dimension_semantics=("arbitrary", "arbitrary"), vmem_limit_bytes=VMEM_LIMIT),
    )(dh, proj, h1, z, ob, dpool, w_out_b, pw_w_b, pool_w, _row(pool_scale), conv_dw_f, _row(ln_g), _row(ln_b))


MESH = pl.DeviceIdType.MESH
W_IN_BLK = D_IN // N_DEV
W_OUT_BLK = D_MODEL // N_DEV
PW_BLK = W_MIX // N_DEV
DW_BLK = W_MIX // N_DEV
N_PEER = N_DEV - 1


SLAB = 2 * W_IN_BLK
F_SIB, F_X, F_Y, F_ON, F_PASS_X, F_PASS_Y, F_PASS_D = range(7)
W_IN_ONLY, OTHERS = (0,), (1, 2, 3)


def _inproj_body(n_tiles, x_ref, ng_ref, wint_ref, wout_ref, pww_ref, dw_ref,
                 proj_hbm, wt_ref, woutf_ref, pwwf_ref, dwf_ref,
                 hn_s, wt_s, wout_s, pww_s, dw_blk, stage, out_sems, send_sems, recv_sems):
    s = pl.program_id(0)
    i = pl.program_id(1)
    tm = TILE_IN
    x, y, c = lax.axis_index("x"), lax.axis_index("y"), lax.axis_index("c")
    sibling = (x, y, 1 - c)
    xn, yn, dg = (1 - x, y), (x, 1 - y), (1 - x, 1 - y)
    pick = lambda a, b: tuple(jnp.where(c == 1, p, q) for p, q in zip(a, b))
    on_from, on_to = pick(xn, yn), pick(yn, xn)

    def blocks(px, py, pc):
        k = 4 * px + 2 * py + pc
        rows = lambda ref, n: ref.at[pl.ds(pl.multiple_of(k * n, n), n), :]
        return (rows(wt_s, W_IN_BLK), rows(wout_s, W_OUT_BLK), rows(pww_s, PW_BLK), dw_blk.at[k])

    def copies(arrays, flow, block, to):
        refs = blocks(*block)
        return [pltpu.make_async_remote_copy(src_ref=refs[a], dst_ref=refs[a], send_sem=send_sems.at[a, flow],
                                             recv_sem=recv_sems.at[a, flow], device_id=to, device_id_type=MESH)
                for a in arrays]

    def start(arrays):
        for flow, to in ((F_SIB, sibling), (F_X, (*xn, c)), (F_Y, (*yn, c))):
            for cp in copies(arrays, flow, (x, y, c), to):
                cp.start()

    def direct(arrays):
        for flow, chip in ((F_X, xn), (F_Y, yn)):
            for cp in copies(arrays, flow, (*chip, c), sibling):
                cp.wait_recv()
        for cp in copies(arrays, F_ON, (*on_from, c), (*on_to, c)):
            cp.start()
        for flow, chip in ((F_PASS_X, xn), (F_PASS_Y, yn)):
            for cp in copies(arrays, flow, (*chip, c), sibling):
                cp.start()

    def diagonal(arrays):
        for cp in copies(arrays, F_ON, (*dg, c), sibling):
            cp.wait_recv()
        for cp in copies(arrays, F_PASS_D, (*dg, c), sibling):
            cp.start()

    def from_sibling(arrays, flow, chip):
        for cp in copies(arrays, flow, (*chip, 1 - c), sibling):
            cp.wait_recv()

    def finish(arrays):
        mine = (x, y, c)
        for flow, block in ((F_SIB, mine), (F_X, mine), (F_Y, mine), (F_ON, (*on_from, c)), (F_PASS_X, (*xn, c)),
                            (F_PASS_Y, (*yn, c)), (F_PASS_D, (*dg, c))):
            for cp in copies(arrays, flow, block, sibling):
                cp.wait_send()

    step = s * n_tiles + i
    par = step % 2
    first = step == 0

    @pl.when(first)
    def _():
        mine = blocks(x, y, c)
        mine[0][...] = wint_ref[...].astype(BF16)
        mine[1][...] = wout_ref[...].astype(BF16)
        mine[2][...] = pww_ref[...].astype(BF16)
        mine[3][...] = dw_ref[...]
        start(W_IN_ONLY)

    rows_i = pl.ds(pl.multiple_of(i * tm, tm), tm)

    @pl.when(s == 0)
    def _():
        xt = x_ref[0]
        r1 = lax.rsqrt(jnp.mean(xt * xt, axis=-1, keepdims=True) + RMS_EPS)
        hn_s[rows_i, :] = (xt * r1 * ng_ref[...]).astype(BF16)

    @pl.when(first)
    def _():
        from_sibling(W_IN_ONLY, F_SIB, (x, y))

    @pl.when((s == 1) & (i == 0))
    def _():
        direct(W_IN_ONLY)
        start(OTHERS)
        from_sibling(W_IN_ONLY, F_PASS_Y, yn)

    @pl.when((s == 2) & (i == 0))
    def _():
        from_sibling(W_IN_ONLY, F_PASS_X, xn)
        diagonal(W_IN_ONLY)
        direct(OTHERS)
        from_sibling(W_IN_ONLY, F_PASS_D, dg)

    tiles_per_seq = proj_hbm.shape[1] // tm

    def out_copy(unit, col0):
        dst = proj_hbm.at[i // tiles_per_seq, pl.ds(pl.multiple_of((i % tiles_per_seq) * tm, tm), tm),
                          pl.ds(pl.multiple_of(col0, LANES), SLAB)]
        return pltpu.make_async_copy(stage.at[par, :, pl.ds(unit * SLAB, SLAB)], dst, out_sems.at[par, unit])

    @pl.when(step >= 2)
    def _():
        out_copy(0, 0).wait()

    @pl.when(step >= 2 * n_tiles + 2)
    def _():
        out_copy(1, 0).wait()

    hn = hn_s[rows_i, :]
    for phase, chip in ((0, 2 * x + y), (1, 2 * x + (1 - y))):
        @pl.when(s == phase)
        def _():
            slab = wt_s[pl.ds(pl.multiple_of(chip * SLAB, SLAB), SLAB), :]
            stage[par, :, 0:SLAB] = lax.dot_general(hn, slab, NT_DIMS, preferred_element_type=F32)
            out_copy(0, chip * SLAB).start()

    @pl.when(s == 2)
    def _():
        col0 = (1 - x) * 2 * SLAB
        slab = wt_s[pl.ds(pl.multiple_of(col0, 2 * SLAB), 2 * SLAB), :]
        stage[par] = lax.dot_general(hn, slab, NT_DIMS, preferred_element_type=F32)
        out_copy(0, col0).start()
        out_copy(1, col0 + SLAB).start()

    @pl.when(step == 3 * n_tiles - 1)
    def _():
        out_copy(0, 0).wait()
        out_copy(1, 0).wait()
        pltpu.make_async_copy(stage.at[1 - par, :, pl.ds(0, SLAB)], proj_hbm.at[0, pl.ds(0, tm), pl.ds(0, SLAB)],
                              out_sems.at[1 - par, 0]).wait()
        pltpu.make_async_copy(stage.at[1 - par, :, pl.ds(SLAB, SLAB)], proj_hbm.at[0, pl.ds(0, tm), pl.ds(0, SLAB)],
                              out_sems.at[1 - par, 1]).wait()
        from_sibling(OTHERS, F_SIB, (x, y))
        from_sibling(OTHERS, F_PASS_X, xn)
        from_sibling(OTHERS, F_PASS_Y, yn)
        diagonal(OTHERS)
        from_sibling(OTHERS, F_PASS_D, dg)
        finish(W_IN_ONLY)
        finish(OTHERS)
        wt_ref[...] = wt_s[...]
        woutf_ref[...] = wout_s[...]
        pwwf_ref[...] = pww_s[...]
        for k in range(N_DEV):
            dwf_ref[:, k * DW_BLK:(k + 1) * DW_BLK] = dw_blk[k]


def _inproj(x, norm_g, w_in_t, w_out, pw_w, conv_dw):
    bl, s, _ = x.shape
    tm = TILE_IN
    nt = s // tm
    n_tiles = bl * nt

    def x_map(phase, i):
        i = jnp.where(phase == 0, i, n_tiles - 1)
        return (i // nt, i % nt, 0)

    dma = pltpu.SemaphoreType.DMA
    return pl.pallas_call(
        functools.partial(_inproj_body, n_tiles),
        name="inproj",
        grid=(3, n_tiles),
        in_specs=[pl.BlockSpec((1, tm, D_MODEL), x_map)] + [VMEM] * 5,
        out_specs=[pl.BlockSpec(memory_space=pl.ANY)] + [VMEM] * 4,
        out_shape=[jax.ShapeDtypeStruct((bl, s, D_IN), F32), jax.ShapeDtypeStruct((D_IN, D_MODEL), BF16),
                   jax.ShapeDtypeStruct((D_MODEL, D_MODEL), BF16), jax.ShapeDtypeStruct((W_MIX, W_MIX), BF16),
                   jax.ShapeDtypeStruct((CONV_WIDTH, W_MIX), F32)],
        scratch_shapes=[pltpu.VMEM((n_tiles * tm, D_MODEL), BF16), pltpu.VMEM((D_IN, D_MODEL), BF16),
                        pltpu.VMEM((D_MODEL, D_MODEL), BF16), pltpu.VMEM((W_MIX, W_MIX), BF16),
                        pltpu.VMEM((N_DEV, CONV_WIDTH, DW_BLK), F32), pltpu.VMEM((2, tm, 2 * SLAB), F32),
                        dma((2, 2)), dma((4, 7)), dma((4, 7))],
        compiler_params=pltpu.CompilerParams(dimension_semantics=("arbitrary", "arbitrary"), vmem_limit_bytes=VMEM_LIMIT),
    )(x, _row(norm_g), w_in_t, w_out, pw_w, conv_dw)


B_PLAIN, B_ON, B_MERGED = range(3)


def _routes():
    x, y, c = lax.axis_index("x"), lax.axis_index("y"), lax.axis_index("c")
    xn, yn = (1 - x, y), (x, 1 - y)
    pick = lambda a, b: tuple(jnp.where(c == 1, p, q) for p, q in zip(a, b))
    return pick(xn, yn), pick(yn, xn), (1 - x, 1 - y), (x, y), c


def _chip_index(chip):
    return 2 * chip[0] + chip[1]


def _copy(src, dst, send_sems, recv_sems, a, k, to):
    return pltpu.make_async_remote_copy(src_ref=src, dst_ref=dst, send_sem=send_sems.at[a, k], recv_sem=recv_sems.at[a, k],
                                        device_id=to, device_id_type=MESH)


def _rsb_send(plain, on, rb_plain, rb_on, send_b, recv_b):
    first, _, _, _, c = _routes()
    for k, srcs, dsts in ((B_ON, on, rb_on), (B_PLAIN, plain, rb_plain)):
        for a, (s, d) in enumerate(zip(srcs, dsts)):
            _copy(s, d, send_b, recv_b, a, k, (*first, c)).start()


def _rsb_merge(second_sums, merged, rb_on, rb_merged, send_b, recv_b):
    _, second, _, _, c = _routes()
    for a, (own, m, got, dst) in enumerate(zip(second_sums, merged, rb_on, rb_merged)):
        _copy(got, got, send_b, recv_b, a, B_ON, (*second, c)).wait_recv()
        m[...] = (own[...] + got[...].astype(F32)).astype(BF16)
        _copy(m, dst, send_b, recv_b, a, B_MERGED, (*second, c)).start()


def _rsb_total(own_sums, plain, on, merged, rb_plain, rb_merged, send_b, recv_b):
    _, second, _, _, c = _routes()
    out = []
    for a, (own, p, g) in enumerate(zip(own_sums, rb_plain, rb_merged)):
        _copy(p, p, send_b, recv_b, a, B_PLAIN, (*second, c)).wait_recv()
        _copy(g, g, send_b, recv_b, a, B_MERGED, (*second, c)).wait_recv()
        out.append(own[...] + p[...].astype(F32) + g[...].astype(F32))
    for k, srcs in ((B_PLAIN, plain), (B_ON, on), (B_MERGED, merged)):
        for a, s in enumerate(srcs):
            _copy(s, s, send_b, recv_b, a, k, (*second, c)).wait_send()
    return out


R_LOSS, R_FINAL_G, R_POOL, R_CONV, R_LN = range(5)
VEC_ROWS = 16
DW_ROWS = 32
A_PART, A_X, A_Y, A_ON, A_HALF = range(5)
STEP_SUMS, STEP_MERGE, STEP_HALF = 1, 3, 5


def _bwd2a_body(n_steps, x_ref, dproj_ref, ng_ref, gwout_ref, gpww_ref, loss_ref, gfg_ref, gvec_ref, gdw_ref, gpoolw_ref,
                win_own_ref, win_staged_ref, owout_ref, opww_ref, ovec_ref, odw_ref, opoolw_ref,
                acc, a_win, a_wout, a_pww, st_wout, st_pww, rb_wout, rb_pww, part_vec, part_dw, part_pw,
                sib_vec, sib_dw, sib_pw, res_vec, res_dw, res_pw, half_vec, half_dw, half_pw,
                send_a, recv_a, send_b, recv_b, send_r, recv_r, send_w, recv_w):
    step = pl.program_id(0) * pl.num_programs(1) + pl.program_id(1)
    x, y, c = lax.axis_index("x"), lax.axis_index("y"), lax.axis_index("c")
    sibling = (x, y, 1 - c)
    xn, yn = (1 - x, y), (x, 1 - y)
    first, second, diag, own, _ = _routes()
    rows = lambda ref, k, n: ref.at[pl.ds(pl.multiple_of(k * n, n), n), :]

    blocks = lambda k: (rows(gwout_ref, k, W_OUT_BLK), rows(gpww_ref, k, PW_BLK))
    landed = lambda chip: (a_wout.at[chip], a_pww.at[chip])
    staged = lambda k: (st_wout.at[k], st_pww.at[k])
    arrived = lambda k: (rb_wout.at[k], rb_pww.at[k])

    parts, sibs, halves = (part_vec, part_dw, part_pw), (sib_vec, sib_dw, sib_pw), (half_vec, half_dw, half_pw)
    results = (res_vec, res_dw, res_pw)
    half_rows = lambda ref: pl.ds(pl.multiple_of(c * (ref.shape[0] // 2), SUBLANES), ref.shape[0] // 2)
    slot = lambda chip: tuple(h.at[_chip_index(chip)] for h in halves)

    @pl.when(step == 0)
    def _():
        acc[...] = jnp.zeros_like(acc)
        sum8 = lambda v: jnp.sum(v, axis=0, keepdims=True)
        part_vec[...] = jnp.zeros((VEC_ROWS, D_MODEL), F32)
        part_vec[R_LOSS:R_LOSS + 1, :] = jnp.broadcast_to(loss_ref[0:1, 0:1], (1, D_MODEL))
        part_vec[R_FINAL_G:R_FINAL_G + 1, :] = sum8(gfg_ref[...])
        for row, (lo, hi) in ((R_POOL, (V_POOL_B, V_POOL_SCALE)), (R_CONV, (V_CONV_B, V_LN_G)), (R_LN, (V_LN_B, V_PW_B))):
            part_vec[row:row + 1, 0:W_MIX] = sum8(gvec_ref[lo])
            part_vec[row:row + 1, W_MIX:] = sum8(gvec_ref[hi])
        part_dw[0:CONV_WIDTH, :] = jnp.sum(gdw_ref[...], axis=1)
        part_dw[CONV_WIDTH:, :] = jnp.zeros((DW_ROWS - CONV_WIDTH, W_MIX), F32)
        part_pw[...] = gpoolw_ref[...].reshape(len(POOL_WINDOWS) * POOL_GC, POOL_GC)
        for a, (p, s) in enumerate(zip(parts, sibs)):
            _copy(p, s, send_r, recv_r, a, A_PART, sibling).start()
        for chip in range(4):
            for a, (s, d) in enumerate(zip(blocks(2 * chip + (1 - c)), landed(chip))):
                _copy(s, d, send_a, recv_a, a, chip, sibling).start()

    x_t = x_ref[0]
    r1 = lax.rsqrt(jnp.mean(x_t * x_t, axis=-1, keepdims=True) + RMS_EPS)
    hn = (x_t * r1 * ng_ref[...]).astype(BF16)
    acc[...] += lax.dot_general(dproj_ref[0], hn, TN_DIMS, preferred_element_type=F32)

    @pl.when(step == STEP_SUMS)
    def _():
        for a, (p, s, h) in enumerate(zip(parts, sibs, slot(own))):
            _copy(p, s, send_r, recv_r, a, A_PART, sibling).wait_recv()
            h[...] = p[half_rows(p), :] + s[half_rows(s), :]
            for k, to in ((A_X, xn), (A_Y, yn)):
                _copy(h, h, send_r, recv_r, a, k, (*to, c)).start()
        for chip in range(4):
            for a, (m, l) in enumerate(zip(blocks(2 * chip + c), landed(chip))):
                _copy(l, l, send_a, recv_a, a, chip, sibling).wait_recv()
                l[...] = m[...] + l[...]
        for k, chip in ((B_ON, diag), (B_PLAIN, first)):
            for s, l in zip(staged(k), landed(_chip_index(chip))):
                s[...] = l[...].astype(BF16)
        _rsb_send(staged(B_PLAIN), staged(B_ON), arrived(B_PLAIN), arrived(B_ON), send_b, recv_b)

    @pl.when(step == STEP_MERGE)
    def _():
        _rsb_merge(landed(_chip_index(second)), staged(B_MERGED), arrived(B_ON), arrived(B_MERGED), send_b, recv_b)
        for a in range(3):
            for k, chip in ((A_X, xn), (A_Y, yn)):
                h = slot(chip)[a]
                _copy(h, h, send_r, recv_r, a, k, sibling).wait_recv()
            h = slot(first)[a]
            _copy(h, h, send_r, recv_r, a, A_ON, (*second, c)).start()

    @pl.when(step == STEP_HALF)
    def _():
        for a, (r, hs) in enumerate(zip(results, halves)):
            h = slot(diag)[a]
            _copy(h, h, send_r, recv_r, a, A_ON, sibling).wait_recv()
            mine = r.at[half_rows(r), :]
            mine[...] = (hs[0] + hs[1]) + (hs[2] + hs[3])
            _copy(mine, mine, send_r, recv_r, a, A_HALF, sibling).start()

    @pl.when(step == n_steps - 1)
    def _():
        block = lambda k: rows(acc, k, W_IN_BLK)
        w_copy = lambda chip: _copy(block(2 * chip + (1 - c)), a_win.at[chip], send_w, recv_w, 0, chip, sibling)
        for chip in range(4):
            w_copy(chip).start()

        owout_ref[...], opww_ref[...] = _rsb_total(landed(_chip_index(own)), staged(B_PLAIN), staged(B_ON), staged(B_MERGED),
                                                   arrived(B_PLAIN), arrived(B_MERGED), send_b, recv_b)
        for a, (r, p, s) in enumerate(zip(results, parts, sibs)):
            theirs = r.at[pl.ds(pl.multiple_of((1 - c) * (r.shape[0] // 2), SUBLANES), r.shape[0] // 2), :]
            _copy(theirs, theirs, send_r, recv_r, a, A_HALF, sibling).wait_recv()
            _copy(p, s, send_r, recv_r, a, A_PART, sibling).wait_send()
            for k, h in ((A_X, slot(own)[a]), (A_Y, slot(own)[a]), (A_ON, slot(first)[a])):
                _copy(h, h, send_r, recv_r, a, k, sibling).wait_send()
            mine = r.at[half_rows(r), :]
            _copy(mine, mine, send_r, recv_r, a, A_HALF, sibling).wait_send()
        for chip in range(4):
            for a, (s, d) in enumerate(zip(blocks(2 * chip + (1 - c)), landed(chip))):
                _copy(s, d, send_a, recv_a, a, chip, sibling).wait_send()
        ovec_ref[...] = res_vec[...]
        opoolw_ref[...] = res_pw[...]
        for k in range(N_DEV):
            @pl.when(4 * x + 2 * y + c == k)
            def _():
                odw_ref[...] = res_dw[0:CONV_WIDTH, k * DW_BLK:(k + 1) * DW_BLK]

        for chip in range(4):
            w_copy(chip).wait()
        chip_sum = lambda chip: block(2 * _chip_index(chip) + c)[...] + a_win[_chip_index(chip)]
        win_own_ref[...] = chip_sum(own)
        for j, chip in enumerate((xn, yn, diag)):
            win_staged_ref[j] = chip_sum(chip).astype(BF16)


def _bwd2a(x, dproj, norm_g, gwout, gpww, loss8, gfg8, gvec, gdw8, gpoolw):
    bl, s, _ = x.shape
    tm = TILE_M
    nt = s // tm
    tok = lambda n: pl.BlockSpec((1, tm, n), lambda b, i: (b, i, 0))
    n_pw = len(POOL_WINDOWS) * POOL_GC
    win = (W_IN_BLK, D_MODEL)
    blk = ((W_OUT_BLK, D_MODEL), (PW_BLK, W_MIX))
    rep = ((VEC_ROWS, D_MODEL), (DW_ROWS, W_MIX), (n_pw, POOL_GC))
    half = lambda sh: (sh[0] // 2,) + sh[1:]
    dma = pltpu.SemaphoreType.DMA
    assert bl * nt > STEP_HALF + 1, "the exchanges' phases need their grid steps"
    return pl.pallas_call(
        functools.partial(_bwd2a_body, bl * nt),
        name="bwd2a",
        grid=(bl, nt),
        in_specs=[tok(D_MODEL), tok(D_IN)] + [VMEM] * 8,
        out_specs=[VMEM] * 7,
        out_shape=[jax.ShapeDtypeStruct(win, F32), jax.ShapeDtypeStruct((3,) + win, BF16)]
        + [jax.ShapeDtypeStruct(sh, F32) for sh in blk + (rep[0], (CONV_WIDTH, DW_BLK), rep[2])],
        scratch_shapes=[pltpu.VMEM((D_IN, D_MODEL), F32), pltpu.VMEM((4,) + win, F32)]
        + [pltpu.VMEM((4,) + sh, F32) for sh in blk]
        + [pltpu.VMEM((3,) + sh, BF16) for sh in blk] * 2
        + [pltpu.VMEM(sh, F32) for sh in rep] * 3
        + [pltpu.VMEM((4,) + half(sh), F32) for sh in rep]
        + [dma((2, 4)), dma((2, 4)), dma((2, 3)), dma((2, 3)), dma((3, 5)), dma((3, 5)), dma((1, 4)), dma((1, 4))],
        compiler_params=pltpu.CompilerParams(dimension_semantics=("arbitrary", "arbitrary"), vmem_limit_bytes=VMEM_LIMIT),
    )(x, dproj, _row(norm_g), gwout, gpww, loss8, gfg8, gvec, gdw8, gpoolw)


HBM = pl.BlockSpec(memory_space=pltpu.HBM)
SEM = pl.BlockSpec(memory_space=pltpu.SEMAPHORE)
EFFECT = pltpu.SideEffectType.DATAFLOW_SIDE_EFFECTING


def _owner_copies(src_ref, land_ref, send_sems, recv_sems):
    x, y, c = lax.axis_index("x"), lax.axis_index("y"), lax.axis_index("c")
    owners = ((1 - x, y, c), (x, 1 - y, c), (1 - x, 1 - y, c))
    return [pltpu.make_async_remote_copy(src_ref=src_ref.at[j], dst_ref=land_ref.at[j], send_sem=send_sems.at[j],
                                         recv_sem=recv_sems.at[j], device_id=owners[j], device_id_type=MESH)
            for j in range(3)]


def _rs_start(staged):
    def body(src_ref, land_ref, send_sems, recv_sems, src_thru, land_thru, token):
        for cp in _owner_copies(src_ref, land_ref, send_sems, recv_sems):
            cp.start()
        token[...] = jnp.zeros_like(token)

    dma = pltpu.SemaphoreType.DMA
    return pl.pallas_call(
        body, name="rs_start",
        out_shape=(dma((3,)), dma((3,)), pltpu.HBM(staged.shape, staged.dtype), pltpu.HBM(staged.shape, staged.dtype),
                   jax.ShapeDtypeStruct((SUBLANES, LANES), F32)),
        in_specs=(HBM, HBM), out_specs=(SEM, SEM, HBM, HBM, VMEM), input_output_aliases={0: 2, 1: 3},
        compiler_params=pltpu.CompilerParams(has_side_effects=EFFECT),
    )(pltpu.with_memory_space_constraint(staged, pltpu.HBM),
      pltpu.with_memory_space_constraint(lax.empty(staged.shape, staged.dtype), pltpu.HBM))


def _rs_wait(send_sems, recv_sems, src_thru, land_thru, after):
    def body(src_ref, land_ref, send_sems, recv_sems, after_ref, src_dead, got_ref):
        for cp in _owner_copies(src_ref, land_ref, send_sems, recv_sems):
            cp.wait_send()
            cp.wait_recv()

    return pl.pallas_call(
        body, name="rs_wait",
        out_shape=(pltpu.HBM(src_thru.shape, src_thru.dtype), pltpu.HBM(land_thru.shape, land_thru.dtype)),
        in_specs=(HBM, HBM, SEM, SEM, pl.BlockSpec(memory_space=pl.ANY)), out_specs=(HBM, HBM),
        input_output_aliases={0: 0, 1: 1},
        compiler_params=pltpu.CompilerParams(has_side_effects=EFFECT),
    )(src_thru, land_thru, send_sems, recv_sems, after)[1]


def _bwd2b_body(x_ref, dh_ref, dproj_ref, ng_ref, wint_ref, token_ref, gx_ref, gng_ref):
    @pl.when((pl.program_id(0) == 0) & (pl.program_id(1) == 0))
    def _():
        gng_ref[...] = jnp.zeros_like(gng_ref) + token_ref[0:1, 0:1]

    x_t = x_ref[0]
    r1 = lax.rsqrt(jnp.mean(x_t * x_t, axis=-1, keepdims=True) + RMS_EPS)
    xr = x_t * r1
    dhn = jnp.dot(dproj_ref[0], wint_ref[...], preferred_element_type=F32)
    gng_ref[...] += _colsum8(dhn * xr)
    gy = dhn * ng_ref[...]
    gx_ref[0] = dh_ref[0] + r1 * (gy - xr * jnp.mean(gy * xr, axis=-1, keepdims=True))


def _bwd2b(x, dh, dproj, norm_g, w_in_t, token):
    bl, s, _ = x.shape
    tm = TILE_M
    tok = lambda n: pl.BlockSpec((1, tm, n), lambda b, i: (b, i, 0))
    return pl.pallas_call(
        functools.partial(_bwd2b_body),
        name="bwd2b",
        grid=(bl, s // tm),
        in_specs=[tok(D_MODEL), tok(D_MODEL), tok(D_IN), VMEM, VMEM, VMEM],
        out_specs=[tok(D_MODEL), pl.BlockSpec((SUBLANES, D_MODEL), lambda b, i: (0, 0))],
        out_shape=[jax.ShapeDtypeStruct((bl, s, D_MODEL), F32), jax.ShapeDtypeStruct((SUBLANES, D_MODEL), F32)],
        compiler_params=pltpu.CompilerParams(dimension_semantics=("arbitrary", "arbitrary"), vmem_limit_bytes=VMEM_LIMIT),
    )(x, dh, dproj, _row(norm_g), w_in_t, token)


def _norm_g_body(part_ref, out_ref, all_ng, send_n, recv_n):
    x, y, c = lax.axis_index("x"), lax.axis_index("y"), lax.axis_index("c")
    other_chips = [(1 - x, y), (x, 1 - y), (1 - x, 1 - y)]
    k_me = 4 * x + 2 * y + c
    peers = [(x, y, 1 - c)] + [(*chip, c) for chip in other_chips] + [(*chip, 1 - c) for chip in other_chips]
    copies = [pltpu.make_async_remote_copy(src_ref=all_ng.at[k_me], dst_ref=all_ng.at[k_me], send_sem=send_n.at[r],
                                           recv_sem=recv_n.at[r], device_id=peers[r], device_id_type=MESH)
              for r in range(N_PEER)]
    all_ng[k_me] = part_ref[...]
    for cp in copies:
        cp.start()
    for cp in copies:
        cp.wait()
    s = all_ng[0]
    for k in range(1, N_DEV):
        s = s + all_ng[k]
    out_ref[...] = jnp.broadcast_to(jnp.sum(s, axis=0, keepdims=True), out_ref.shape)


def _norm_g_sum(part):
    dma = pltpu.SemaphoreType.DMA
    return pl.pallas_call(
        functools.partial(_norm_g_body),
        name="norm_g_sum",
        in_specs=[VMEM],
        out_specs=VMEM,
        out_shape=jax.ShapeDtypeStruct(part.shape, F32),
        scratch_shapes=[pltpu.VMEM((N_DEV,) + part.shape, F32), dma((N_PEER,)), dma((N_PEER,))],
    )(part)


ADAM_LR = 0.001
ADAM_B1 = 0.9
ADAM_B2 = 0.999
ADAM_EPS = 1e-08
ADAM_WD = 0.01
ADAM_STEP = 10


PARAMS = ("norm_g", "w_in", "pool_w", "pool_b", "pool_scale", "conv_dw", "conv_b", "ln_g", "ln_b", "pw_w", "pw_b", "w_out",
          "final_g")


def _adamw_body(*refs):
    n = len(PARAMS)
    ws, ms, vs = (dict(zip(PARAMS, refs[k * n:(k + 1) * n])) for k in range(3))
    gwin_ref, gwin_others_ref, gwout_ref, gpww_ref, vec_ref, gdw_ref, gpoolw_ref, gng_ref = refs[3 * n:3 * n + 8]
    outs = refs[3 * n + 8:]
    gs, ds, m2s, v2s = (dict(zip(PARAMS, outs[k * n:(k + 1) * n])) for k in range(4))
    loss_ref = outs[4 * n]

    def update(name, g, at=slice(None)):
        m = ADAM_B1 * ms[name][at] + (1.0 - ADAM_B1) * g
        v = ADAM_B2 * vs[name][at] + (1.0 - ADAM_B2) * (g * g)
        m_hat = m / (1.0 - ADAM_B1 ** ADAM_STEP)
        v_hat = v / (1.0 - ADAM_B2 ** ADAM_STEP)
        gs[name][at] = g
        ds[name][at] = -ADAM_LR * (m_hat / (jnp.sqrt(v_hat) + ADAM_EPS) + ADAM_WD * ws[name][at])
        m2s[name][at] = m
        v2s[name][at] = v

    row = lambda r, lo, hi: vec_ref[r:r + 1, lo:hi]
    loss_ref[...] = row(R_LOSS, 0, 1)
    update("norm_g", gng_ref[0:1, :])
    g_w_in = gwin_ref[...]
    for j in range(3):
        g_w_in = g_w_in + gwin_others_ref[j].astype(F32)
    update("w_in", g_w_in)
    update("pool_w", gpoolw_ref[...].reshape(ws["pool_w"].shape))
    for g in range(len(POOL_WINDOWS)):
        update("pool_b", row(R_POOL, g * POOL_GC, (g + 1) * POOL_GC), at=slice(g, g + 1))
    update("pool_scale", row(R_POOL, W_MIX, 2 * W_MIX))
    update("conv_dw", gdw_ref[...])
    update("conv_b", row(R_CONV, 0, W_MIX))
    update("ln_g", row(R_CONV, W_MIX, 2 * W_MIX))
    update("ln_b", row(R_LN, 0, W_MIX))
    update("pw_w", gpww_ref[...])
    update("pw_b", row(R_LN, W_MIX, 2 * W_MIX))
    update("w_out", gwout_ref[...])
    update("final_g", row(R_FINAL_G, 0, D_MODEL))


def _adamw(ws, ms, vs, grad_sources):
    n = len(PARAMS)
    shapes = [ws[p].shape for p in PARAMS]
    flat = lambda a: a.reshape(1, -1) if a.ndim == 1 else a
    args = [flat(d[p]) for d in (ws, ms, vs) for p in PARAMS]
    outs = pl.pallas_call(
        functools.partial(_adamw_body),
        name="adamw",
        in_specs=[VMEM] * (3 * n + len(grad_sources)),
        out_specs=[VMEM] * (4 * n + 1),
        out_shape=[jax.ShapeDtypeStruct(a.shape, F32) for a in args[:n]] * 4 + [jax.ShapeDtypeStruct((1, 1), F32)],
        compiler_params=pltpu.CompilerParams(vmem_limit_bytes=VMEM_LIMIT),
    )(*args, *grad_sources)
    loss = outs[4 * n].reshape(())
    outs = [o.reshape(shapes[k % n]) for k, o in enumerate(outs[:4 * n])]
    return loss, outs[:n], outs[n:2 * n], outs[2 * n:3 * n], outs[3 * n:]


def kernel(x, norm_g, w_in, pool_w, pool_b, pool_scale, conv_dw, conv_b, ln_g, ln_b, pw_w, pw_b, w_out, final_g, loss_target, m_norm_g, m_w_in, m_pool_w, m_pool_b, m_pool_scale, m_conv_dw, m_conv_b, m_ln_g, m_ln_b, m_pw_w, m_pw_b, m_w_out, m_final_g, v_norm_g, v_w_in, v_pool_w, v_pool_b, v_pool_scale, v_conv_dw, v_conv_b, v_ln_g, v_ln_b, v_pw_w, v_pw_b, v_w_out, v_final_g):
    proj, w_in_t, w_out_b, pw_w_b, conv_dw_f = _inproj(x, norm_g, w_in.T, w_out, pw_w, conv_dw)
    h1, z, ob, dpool, dh, loss8, gfg8 = _fwd(x, loss_target, proj, pool_w, pool_b, pool_scale, conv_dw_f,
                                             conv_b, ln_g, ln_b, pw_w_b, pw_b, w_out_b, final_g)
    dproj, gwout, gpww, gpoolw, gvec, gdw8 = _bwd1(dh, proj, h1, z, ob, dpool, w_out_b, pw_w_b, pool_w, pool_scale,
                                                   conv_dw_f, ln_g, ln_b)
    win_own, win_staged, g_w_out, g_pw_w, vec, g_conv_dw, poolw_sum = _bwd2a(x, dproj, norm_g, gwout, gpww, loss8, gfg8, gvec,
                                                                             gdw8, gpoolw)
    send_sems, recv_sems, staged_thru, land_thru, token = _rs_start(win_staged)
    grad_x, gng8 = _bwd2b(x, dh, dproj, norm_g, w_in_t, token)
    g_norm_g = _norm_g_sum(gng8)
    win_others = _rs_wait(send_sems, recv_sems, staged_thru, land_thru, gng8)

    weights = dict(norm_g=norm_g, w_in=w_in.T, pool_w=pool_w, pool_b=pool_b, pool_scale=pool_scale, conv_dw=conv_dw, conv_b=conv_b,
                   ln_g=ln_g, ln_b=ln_b, pw_w=pw_w, pw_b=pw_b, w_out=w_out, final_g=final_g)
    m_in = dict(norm_g=m_norm_g, w_in=m_w_in.T, pool_w=m_pool_w, pool_b=m_pool_b, pool_scale=m_pool_scale, conv_dw=m_conv_dw,
                conv_b=m_conv_b, ln_g=m_ln_g, ln_b=m_ln_b, pw_w=m_pw_w, pw_b=m_pw_b, w_out=m_w_out, final_g=m_final_g)
    v_in = dict(norm_g=v_norm_g, w_in=v_w_in.T, pool_w=v_pool_w, pool_b=v_pool_b, pool_scale=v_pool_scale, conv_dw=v_conv_dw,
                conv_b=v_conv_b, ln_g=v_ln_g, ln_b=v_ln_b, pw_w=v_pw_w, pw_b=v_pw_b, w_out=v_w_out, final_g=v_final_g)
    loss, *groups = _adamw(weights, m_in, v_in, (win_own, win_others, g_w_out, g_pw_w, vec, g_conv_dw, poolw_sum, g_norm_g))
    k_in = PARAMS.index("w_in")
    for group in groups:
        group[k_in] = group[k_in].T
    grads, delta, new_m, new_v = groups
    return (loss, grad_x, *grads, *delta, *new_m, *new_v)
```

```python
import functools

import jax
import jax.numpy as jnp
from jax import lax
from jax.experimental import pallas as pl
from jax.experimental.pallas import tpu as pltpu

F32 = jnp.float32
BF16 = jnp.bfloat16

D_MODEL = 1024
W_MIX = 512
D_IN = 5 * W_MIX
POOL_WINDOWS = (2, 4, 8, 16)
POOL_GC = 128
CONV_WIDTH = 31
RMS_EPS = 1e-6
LN_EPS = 1e-5
N_DEV = 8

LANES = 128
SUBLANES = 8
TILE_M = 512
TILE_IN = 1024
POOL_HALO = 16
CONV_HALO = 32
CONV_ROWS = 128
BWD_CONV_ROWS = 64
BWD_SUBTILES = 2
VMEM_LIMIT = 56 * 1024 * 1024

VMEM = pl.BlockSpec(memory_space=pltpu.VMEM)


def _silu(v):
    return v * jax.nn.sigmoid(v)


def _colsum8(v):
    m, n = v.shape
    return jnp.sum(v.reshape(m // SUBLANES, SUBLANES, n), axis=0)


def _row_index(shape, t0):
    return lax.broadcasted_iota(jnp.int32, shape, 0) + t0


def _conv_taps():
    taps = {s: [] for s in range(SUBLANES)}
    for k in range(CONV_WIDTH):
        e = k + CONV_HALO - (CONV_WIDTH - 1)
        taps[e % SUBLANES].append((e // SUBLANES, k))
    return taps


def _fwd_body(x_ref, tgt_ref, proj_ref, poolw_ref, poolb_ref, pools_ref, dw_ref, cb_ref, lg_ref, lb_ref,
              pww_ref, pwb_ref, wout_ref, fg_ref,
              h1_ref, z_ref, ob_ref, dpool_ref, dh_ref, loss_ref, gfg_ref,
              uext, hext, y_s):
    b = pl.program_id(0)
    i = pl.program_id(1)
    tm = TILE_M

    @pl.when(i == 0)
    def _():
        uext[0:POOL_HALO, :] = jnp.zeros((POOL_HALO, W_MIX), F32)
        hext[0:CONV_HALO, :] = jnp.zeros((CONV_HALO, W_MIX), F32)
        hext[CONV_HALO + tm:, :] = jnp.zeros((SUBLANES, W_MIX), F32)

    @pl.when((b == 0) & (i == 0))
    def _():
        loss_ref[...] = jnp.zeros_like(loss_ref)
        gfg_ref[...] = jnp.zeros_like(gfg_ref)

    uext[POOL_HALO:, :] = proj_ref[0, :, 0:W_MIX]
    t = _row_index((tm, POOL_GC), i * tm)
    for g, w in enumerate(POOL_WINDOWS):
        lanes = slice(g * POOL_GC, (g + 1) * POOL_GC)
        e = uext[:, lanes]
        s = e
        sh = 1
        while sh < w:
            s = s + pltpu.roll(s, sh, 0)
            sh *= 2
        inv_cnt = 1.0 / jnp.minimum(t + 1, w).astype(F32)
        d = (s[POOL_HALO:] * inv_cnt - e[POOL_HALO:]).astype(BF16)
        dpool_ref[0, :, lanes] = d
        z = jnp.dot(d, poolw_ref[g].astype(BF16), preferred_element_type=F32) + poolb_ref[:, lanes]
        z_ref[0, :, lanes] = z
        a_gate = proj_ref[0, :, W_MIX + g * POOL_GC:W_MIX + (g + 1) * POOL_GC]
        y_s[:, lanes] = (z * pools_ref[:, lanes] * _silu(a_gate)).astype(BF16)

    hext[CONV_HALO:CONV_HALO + tm, :] = proj_ref[0, :, 2 * W_MIX:3 * W_MIX] * jax.nn.sigmoid(proj_ref[0, :, 3 * W_MIX:4 * W_MIX])
    taps = _conv_taps()
    for c0 in range(0, tm, CONV_ROWS):
        for l0 in range(0, W_MIX, LANES):
            lanes = slice(l0, l0 + LANES)
            acc = jnp.zeros((CONV_ROWS, LANES), F32) + cb_ref[:, lanes]
            for s in range(SUBLANES):
                part = jnp.zeros((CONV_ROWS + SUBLANES, LANES), F32)
                for q, k in taps[s]:
                    r0 = c0 + SUBLANES * q
                    part = part + hext[r0:r0 + CONV_ROWS + SUBLANES, lanes] * dw_ref[k:k + 1, lanes]
                acc = acc + part[s:s + CONV_ROWS]
            h1_ref[0, c0:c0 + CONV_ROWS, lanes] = acc
    h1 = h1_ref[0]
    mu = jnp.mean(h1, axis=-1, keepdims=True)
    xc = h1 - mu
    rstd = lax.rsqrt(jnp.mean(xc * xc, axis=-1, keepdims=True) + LN_EPS)
    h3 = _silu(xc * rstd * lg_ref[...] + lb_ref[...])
    ob = jnp.dot(h3.astype(BF16), pww_ref[...], preferred_element_type=F32) + pwb_ref[...]
    ob_ref[0] = ob
    y_s[:, W_MIX:] = (ob * _silu(proj_ref[0, :, 4 * W_MIX:5 * W_MIX])).astype(BF16)

    h = x_ref[0] + jnp.dot(y_s[...], wout_ref[...], preferred_element_type=F32)
    r2 = lax.rsqrt(jnp.mean(h * h, axis=-1, keepdims=True) + RMS_EPS)
    hr = h * r2
    err = hr * fg_ref[...] - tgt_ref[0]
    loss_ref[...] += jnp.sum(err * err) * (0.5 / D_MODEL)
    dout = err * (1.0 / D_MODEL)
    gfg_ref[...] += _colsum8(dout * hr)
    gy = dout * fg_ref[...]
    dh_ref[0] = r2 * (gy - hr * jnp.mean(gy * hr, axis=-1, keepdims=True))

    uext[0:POOL_HALO, :] = uext[tm:tm + POOL_HALO, :]
    hext[0:CONV_HALO, :] = hext[tm:tm + CONV_HALO, :]


def _row(v):
    return v.reshape(1, -1)


def _fwd(x, tgt, proj, pool_w, pool_b, pool_scale, conv_dw_f, conv_b, ln_g, ln_b, pw_w_b, pw_b, w_out_b, final_g):
    bl, s, _ = x.shape
    tm = TILE_M
    tok = lambda n: pl.BlockSpec((1, tm, n), lambda b, i: (b, i, 0))
    acc = lambda n: pl.BlockSpec((SUBLANES, n), lambda b, i: (0, 0))
    act = lambda n: jax.ShapeDtypeStruct((bl, s, n), F32)
    return pl.pallas_call(
        functools.partial(_fwd_body),
        name="fwd",
        grid=(bl, s // tm),
        in_specs=[tok(D_MODEL), tok(D_MODEL), tok(D_IN)] + [VMEM] * 11,
        out_specs=[tok(W_MIX), tok(W_MIX), tok(W_MIX), tok(W_MIX), tok(D_MODEL), acc(LANES), acc(D_MODEL)],
        out_shape=[act(W_MIX), act(W_MIX), act(W_MIX), jax.ShapeDtypeStruct((bl, s, W_MIX), BF16), act(D_MODEL),
                   jax.ShapeDtypeStruct((SUBLANES, LANES), F32), jax.ShapeDtypeStruct((SUBLANES, D_MODEL), F32)],
        scratch_shapes=[pltpu.VMEM((POOL_HALO + tm, W_MIX), F32), pltpu.VMEM((CONV_HALO + tm + SUBLANES, W_MIX), F32),
                        pltpu.VMEM((tm, D_MODEL), BF16)],
        compiler_params=pltpu.CompilerParams(dimension_semantics=("arbitrary", "arbitrary"), vmem_limit_bytes=VMEM_LIMIT),
    )(x, tgt, proj, pool_w, pool_b.reshape(1, W_MIX), _row(pool_scale), conv_dw_f, _row(conv_b),
      _row(ln_g), _row(ln_b), pw_w_b, _row(pw_b), w_out_b, _row(final_g))


NT_DIMS = (((1,), (1,)), ((), ()))
TN_DIMS = (((0,), (0,)), ((), ()))
V_POOL_B, V_POOL_SCALE, V_CONV_B, V_LN_G, V_LN_B, V_PW_B = range(6)


def _bwd_taps():
    taps = {s: [] for s in range(SUBLANES)}
    for k in range(CONV_WIDTH):
        e = CONV_WIDTH - 1 - k
        taps[e % SUBLANES].append((e // SUBLANES, k))
    return taps


def _bwd1_body(n_tiles, dh_ref, proj_ref, h1_ref, z_ref, ob_ref, dpool_ref, wout_ref, pww_ref, poolw_ref, pools_ref, dw_ref,
               lg_ref, lb_ref,
               dproj_ref, gwout_ref, gpww_ref, gpoolw_ref, gvec_ref, gdw_ref,
               vext, gext, gsh, y_s):
    b = pl.program_id(0)
    i = pl.program_id(1)
    tm = TILE_M
    t0 = (n_tiles - 1 - i) * tm

    @pl.when(i == 0)
    def _():
        vext[tm:, :] = jnp.zeros((POOL_HALO, W_MIX), F32)
        gext[tm:, :] = jnp.zeros((CONV_HALO, W_MIX), F32)

    @pl.when((b == 0) & (i == 0))
    def _():
        gwout_ref[...] = jnp.zeros_like(gwout_ref)
        gpww_ref[...] = jnp.zeros_like(gpww_ref)
        gpoolw_ref[...] = jnp.zeros_like(gpoolw_ref)
        gvec_ref[...] = jnp.zeros_like(gvec_ref)
        gdw_ref[...] = jnp.zeros_like(gdw_ref)

    sub = tm // BWD_SUBTILES

    def dense_phase(r0):
        rs = slice(r0, r0 + sub)
        dhb = dh_ref[0, rs, :].astype(BF16)
        dy_a = lax.dot_general(dhb, wout_ref[0:W_MIX, :], NT_DIMS, preferred_element_type=F32)
        dy_b = lax.dot_general(dhb, wout_ref[W_MIX:, :], NT_DIMS, preferred_element_type=F32)

        a_gate = proj_ref[0, rs, W_MIX:2 * W_MIX]
        sig_a = jax.nn.sigmoid(a_gate)
        silu_a = a_gate * sig_a
        z = z_ref[0, rs, :]
        out_a = z * pools_ref[...]
        y_s[rs, 0:W_MIX] = (out_a * silu_a).astype(BF16)
        d_out_a = dy_a * silu_a
        d_ag = dy_a * out_a * (sig_a * (1.0 + a_gate * (1.0 - sig_a)))
        dproj_ref[0, rs, W_MIX:2 * W_MIX] = d_ag.astype(BF16)
        gvec_ref[V_POOL_SCALE] += _colsum8(d_out_a * z)
        dz = d_out_a * pools_ref[...]
        gvec_ref[V_POOL_B] += _colsum8(dz)
        t = _row_index((sub, POOL_GC), t0 + r0)
        n_ext = sub + POOL_HALO
        for g, w in enumerate(POOL_WINDOWS):
            lanes = slice(g * POOL_GC, (g + 1) * POOL_GC)
            dzg = dz[:, lanes].astype(BF16)
            gpoolw_ref[g] += lax.dot_general(dpool_ref[0, rs, lanes], dzg, TN_DIMS, preferred_element_type=F32)
            dd = lax.dot_general(dzg, poolw_ref[g].astype(BF16), NT_DIMS, preferred_element_type=F32)
            vext[rs, lanes] = dd * (1.0 / jnp.minimum(t + 1, w).astype(F32))
            s = vext[r0:r0 + n_ext, lanes]
            sh = 1
            while sh < w:
                s = s + pltpu.roll(s, n_ext - sh, 0)
                sh *= 2
            dproj_ref[0, rs, lanes] = (s[0:sub] - dd).astype(BF16)

        b_gate = proj_ref[0, rs, 4 * W_MIX:5 * W_MIX]
        sig_b = jax.nn.sigmoid(b_gate)
        silu_b = b_gate * sig_b
        ob = ob_ref[0, rs, :]
        y_s[rs, W_MIX:] = (ob * silu_b).astype(BF16)
        d_ob = dy_b * silu_b
        dproj_ref[0, rs, 4 * W_MIX:5 * W_MIX] = (dy_b * ob * (sig_b * (1.0 + b_gate * (1.0 - sig_b)))).astype(BF16)
        gvec_ref[V_PW_B] += _colsum8(d_ob)
        gwout_ref[...] += lax.dot_general(y_s[rs, :], dhb, TN_DIMS, preferred_element_type=F32)

        h1 = h1_ref[0, rs, :]
        mu = jnp.mean(h1, axis=-1, keepdims=True)
        xc = h1 - mu
        rstd = lax.rsqrt(jnp.mean(xc * xc, axis=-1, keepdims=True) + LN_EPS)
        xhat = xc * rstd
        h2 = xhat * lg_ref[...] + lb_ref[...]
        sig2 = jax.nn.sigmoid(h2)
        d_obb = d_ob.astype(BF16)
        gpww_ref[...] += lax.dot_general((h2 * sig2).astype(BF16), d_obb, TN_DIMS, preferred_element_type=F32)
        dh2 = lax.dot_general(d_obb, pww_ref[...], NT_DIMS, preferred_element_type=F32) * (sig2 * (1.0 + h2 * (1.0 - sig2)))
        gvec_ref[V_LN_G] += _colsum8(dh2 * xhat)
        gvec_ref[V_LN_B] += _colsum8(dh2)
        dxh = dh2 * lg_ref[...]
        dh1 = rstd * (dxh - jnp.mean(dxh, axis=-1, keepdims=True) - xhat * jnp.mean(dxh * xhat, axis=-1, keepdims=True))
        gvec_ref[V_CONV_B] += _colsum8(dh1)
        gext[rs, :] = dh1

    taps = _bwd_taps()
    cr = BWD_CONV_ROWS
    win = cr + CONV_HALO - SUBLANES

    def conv_phase(r0):
        for c0 in range(r0, r0 + sub, cr):
            for l0 in range(0, W_MIX, LANES):
                lanes = slice(l0, l0 + LANES)
                b_val = proj_ref[0, c0:c0 + cr, 2 * W_MIX + l0:2 * W_MIX + l0 + LANES]
                sg = jax.nn.sigmoid(proj_ref[0, c0:c0 + cr, 3 * W_MIX + l0:3 * W_MIX + l0 + LANES])
                h0c = b_val * sg
                acc = jnp.zeros((cr, LANES), F32)
                for s in range(SUBLANES):
                    gsh[...] = gext[c0 + s:c0 + s + win, lanes]
                    for q, k in taps[s]:
                        gq = gsh[SUBLANES * q:SUBLANES * q + cr, :]
                        acc = acc + gq * dw_ref[k:k + 1, lanes]
                        gdw_ref[k, :, lanes] += _colsum8(h0c * gq)
                d_bval = acc * sg
                dproj_ref[0, c0:c0 + cr, 2 * W_MIX + l0:2 * W_MIX + l0 + LANES] = d_bval.astype(BF16)
                dproj_ref[0, c0:c0 + cr, 3 * W_MIX + l0:3 * W_MIX + l0 + LANES] = (d_bval * b_val * (1.0 - sg)).astype(BF16)

    starts = [k * sub for k in reversed(range(BWD_SUBTILES))]
    for r0 in starts:
        dense_phase(r0)
    for r0 in starts:
        conv_phase(r0)
    vext[tm:, :] = vext[0:POOL_HALO, :]
    gext[tm:, :] = gext[0:CONV_HALO, :]


def _bwd1(dh, proj, h1, z, ob, dpool, w_out_b, pw_w_b, pool_w, pool_scale, conv_dw_f, ln_g, ln_b):
    bl, s, _ = dh.shape
    tm = TILE_M
    nt = s // tm
    tok = lambda n: pl.BlockSpec((1, tm, n), lambda b, i: (b, nt - 1 - i, 0))
    res = lambda shape: pl.BlockSpec(shape, lambda b, i: (0,) * len(shape))
    out_shapes = [(D_MODEL, D_MODEL), (W_MIX, W_MIX), (len(POOL_WINDOWS), POOL_GC, POOL_GC), (6, SUBLANES, W_MIX),
                  (CONV_WIDTH, SUBLANES, W_MIX)]
    return pl.pallas_call(
        functools.partial(_bwd1_body, nt),
        name="bwd1",
        grid=(bl, nt),
        in_specs=[tok(D_MODEL), tok(D_IN), tok(W_MIX), tok(W_MIX), tok(W_MIX), tok(W_MIX)] + [VMEM] * 7,
        out_specs=[tok(D_IN)] + [res(sh) for sh in out_shapes],
        out_shape=[jax.ShapeDtypeStruct((bl, s, D_IN), BF16)] + [jax.ShapeDtypeStruct(sh, F32) for sh in out_shapes],
        scratch_shapes=[pltpu.VMEM((tm + POOL_HALO, W_MIX), F32), pltpu.VMEM((tm + CONV_HALO, W_MIX), F32),
                        pltpu.VMEM((BWD_CONV_ROWS + CONV_HALO - SUBLANES, LANES), F32), pltpu.VMEM((tm, D_MODEL), BF16)],
        compiler_params=pltpu.CompilerParams(dimension_semantics=("arbitrary", "arbitrary"), vmem_limit_bytes=VMEM_LIMIT),
    )(dh, proj, h1, z, ob, dpool, w_out_b, pw_w_b, pool_w, _row(pool_scale), conv_dw_f, _row(ln_g), _row(ln_b))


MESH = pl.DeviceIdType.MESH
W_IN_BLK = D_IN // N_DEV
W_OUT_BLK = D_MODEL // N_DEV
PW_BLK = W_MIX // N_DEV
DW_BLK = W_MIX // N_DEV
N_PEER = N_DEV - 1


SLAB = 2 * W_IN_BLK
F_SIB, F_X, F_Y, F_ON, F_PASS_X, F_PASS_Y, F_PASS_D = range(7)
W_IN_ONLY = (0,)


def _inproj_body(n_tiles, x_ref, ng_ref, wint_ref, token_ref, proj_hbm, wt_ref,
                 hn_s, wt_s, stage, out_sems, send_sems, recv_sems):
    s = pl.program_id(0)
    i = pl.program_id(1)
    tm = TILE_IN
    x, y, c = lax.axis_index("x"), lax.axis_index("y"), lax.axis_index("c")
    sibling = (x, y, 1 - c)
    xn, yn, dg = (1 - x, y), (x, 1 - y), (1 - x, 1 - y)
    pick = lambda a, b: tuple(jnp.where(c == 1, p, q) for p, q in zip(a, b))
    on_from, on_to = pick(xn, yn), pick(yn, xn)

    def blocks(px, py, pc):
        k = 4 * px + 2 * py + pc
        return (wt_s.at[pl.ds(pl.multiple_of(k * W_IN_BLK, W_IN_BLK), W_IN_BLK), :],)

    def copies(arrays, flow, block, to):
        refs = blocks(*block)
        return [pltpu.make_async_remote_copy(src_ref=refs[a], dst_ref=refs[a], send_sem=send_sems.at[a, flow],
                                             recv_sem=recv_sems.at[a, flow], device_id=to, device_id_type=MESH)
                for a in arrays]

    def start(arrays):
        for flow, to in ((F_SIB, sibling), (F_X, (*xn, c)), (F_Y, (*yn, c))):
            for cp in copies(arrays, flow, (x, y, c), to):
                cp.start()

    def direct(arrays):
        for flow, chip in ((F_X, xn), (F_Y, yn)):
            for cp in copies(arrays, flow, (*chip, c), sibling):
                cp.wait_recv()
        for cp in copies(arrays, F_ON, (*on_from, c), (*on_to, c)):
            cp.start()
        for flow, chip in ((F_PASS_X, xn), (F_PASS_Y, yn)):
            for cp in copies(arrays, flow, (*chip, c), sibling):
                cp.start()

    def diagonal(arrays):
        for cp in copies(arrays, F_ON, (*dg, c), sibling):
            cp.wait_recv()
        for cp in copies(arrays, F_PASS_D, (*dg, c), sibling):
            cp.start()

    def from_sibling(arrays, flow, chip):
        for cp in copies(arrays, flow, (*chip, 1 - c), sibling):
            cp.wait_recv()

    def finish(arrays):
        mine = (x, y, c)
        for flow, block in ((F_SIB, mine), (F_X, mine), (F_Y, mine), (F_ON, (*on_from, c)), (F_PASS_X, (*xn, c)),
                            (F_PASS_Y, (*yn, c)), (F_PASS_D, (*dg, c))):
            for cp in copies(arrays, flow, block, sibling):
                cp.wait_send()

    step = s * n_tiles + i
    par = step % 2
    first = step == 0

    @pl.when(first)
    def _():
        blocks(x, y, c)[0][...] = (wint_ref[...] + token_ref[0:1, 0:1]).astype(BF16)
        start(W_IN_ONLY)

    rows_i = pl.ds(pl.multiple_of(i * tm, tm), tm)

    @pl.when(s == 0)
    def _():
        xt = x_ref[0]
        r1 = lax.rsqrt(jnp.mean(xt * xt, axis=-1, keepdims=True) + RMS_EPS)
        hn_s[rows_i, :] = (xt * r1 * ng_ref[...]).astype(BF16)

    @pl.when(first)
    def _():
        from_sibling(W_IN_ONLY, F_SIB, (x, y))

    @pl.when((s == 1) & (i == 0))
    def _():
        direct(W_IN_ONLY)
        from_sibling(W_IN_ONLY, F_PASS_Y, yn)

    @pl.when((s == 2) & (i == 0))
    def _():
        from_sibling(W_IN_ONLY, F_PASS_X, xn)
        diagonal(W_IN_ONLY)
        from_sibling(W_IN_ONLY, F_PASS_D, dg)

    tiles_per_seq = proj_hbm.shape[1] // tm

    def out_copy(unit, col0):
        dst = proj_hbm.at[i // tiles_per_seq, pl.ds(pl.multiple_of((i % tiles_per_seq) * tm, tm), tm),
                          pl.ds(pl.multiple_of(col0, LANES), SLAB)]
        return pltpu.make_async_copy(stage.at[par, :, pl.ds(unit * SLAB, SLAB)], dst, out_sems.at[par, unit])

    @pl.when(step >= 2)
    def _():
        out_copy(0, 0).wait()

    @pl.when(step >= 2 * n_tiles + 2)
    def _():
        out_copy(1, 0).wait()

    hn = hn_s[rows_i, :]
    for phase, chip in ((0, 2 * x + y), (1, 2 * x + (1 - y))):
        @pl.when(s == phase)
        def _():
            slab = wt_s[pl.ds(pl.multiple_of(chip * SLAB, SLAB), SLAB), :]
            stage[par, :, 0:SLAB] = lax.dot_general(hn, slab, NT_DIMS, preferred_element_type=F32)
            out_copy(0, chip * SLAB).start()

    @pl.when(s == 2)
    def _():
        col0 = (1 - x) * 2 * SLAB
        slab = wt_s[pl.ds(pl.multiple_of(col0, 2 * SLAB), 2 * SLAB), :]
        stage[par] = lax.dot_general(hn, slab, NT_DIMS, preferred_element_type=F32)
        out_copy(0, col0).start()
        out_copy(1, col0 + SLAB).start()

    @pl.when(step == 3 * n_tiles - 1)
    def _():
        out_copy(0, 0).wait()
        out_copy(1, 0).wait()
        pltpu.make_async_copy(stage.at[1 - par, :, pl.ds(0, SLAB)], proj_hbm.at[0, pl.ds(0, tm), pl.ds(0, SLAB)],
                              out_sems.at[1 - par, 0]).wait()
        pltpu.make_async_copy(stage.at[1 - par, :, pl.ds(SLAB, SLAB)], proj_hbm.at[0, pl.ds(0, tm), pl.ds(0, SLAB)],
                              out_sems.at[1 - par, 1]).wait()
        finish(W_IN_ONLY)
        wt_ref[...] = wt_s[...]


def _inproj(x, norm_g, w_in_t, token):
    bl, s, _ = x.shape
    tm = TILE_IN
    nt = s // tm
    n_tiles = bl * nt

    def x_map(phase, i):
        i = jnp.where(phase == 0, i, n_tiles - 1)
        return (i // nt, i % nt, 0)

    dma = pltpu.SemaphoreType.DMA
    return pl.pallas_call(
        functools.partial(_inproj_body, n_tiles),
        name="inproj",
        grid=(3, n_tiles),
        in_specs=[pl.BlockSpec((1, tm, D_MODEL), x_map)] + [VMEM] * 3,
        out_specs=[pl.BlockSpec(memory_space=pl.ANY), VMEM],
        out_shape=[jax.ShapeDtypeStruct((bl, s, D_IN), F32), jax.ShapeDtypeStruct((D_IN, D_MODEL), BF16)],
        scratch_shapes=[pltpu.VMEM((n_tiles * tm, D_MODEL), BF16), pltpu.VMEM((D_IN, D_MODEL), BF16),
                        pltpu.VMEM((2, tm, 2 * SLAB), F32), dma((2, 2)), dma((1, 7)), dma((1, 7))],
        compiler_params=pltpu.CompilerParams(dimension_semantics=("arbitrary", "arbitrary"), vmem_limit_bytes=VMEM_LIMIT),
    )(x, _row(norm_g), w_in_t, token)


B_PLAIN, B_ON, B_MERGED = range(3)


def _routes():
    x, y, c = lax.axis_index("x"), lax.axis_index("y"), lax.axis_index("c")
    xn, yn = (1 - x, y), (x, 1 - y)
    pick = lambda a, b: tuple(jnp.where(c == 1, p, q) for p, q in zip(a, b))
    return pick(xn, yn), pick(yn, xn), (1 - x, 1 - y), (x, y), c


def _chip_index(chip):
    return 2 * chip[0] + chip[1]


def _copy(src, dst, send_sems, recv_sems, a, k, to):
    return pltpu.make_async_remote_copy(src_ref=src, dst_ref=dst, send_sem=send_sems.at[a, k], recv_sem=recv_sems.at[a, k],
                                        device_id=to, device_id_type=MESH)


def _rsb_send(plain, on, rb_plain, rb_on, send_b, recv_b):
    first, _, _, _, c = _routes()
    for k, srcs, dsts in ((B_ON, on, rb_on), (B_PLAIN, plain, rb_plain)):
        for a, (s, d) in enumerate(zip(srcs, dsts)):
            _copy(s, d, send_b, recv_b, a, k, (*first, c)).start()


def _rsb_merge(second_sums, merged, rb_on, rb_merged, send_b, recv_b):
    _, second, _, _, c = _routes()
    for a, (own, m, got, dst) in enumerate(zip(second_sums, merged, rb_on, rb_merged)):
        _copy(got, got, send_b, recv_b, a, B_ON, (*second, c)).wait_recv()
        m[...] = (own[...] + got[...].astype(F32)).astype(BF16)
        _copy(m, dst, send_b, recv_b, a, B_MERGED, (*second, c)).start()


def _rsb_total(own_sums, plain, on, merged, rb_plain, rb_merged, send_b, recv_b):
    _, second, _, _, c = _routes()
    out = []
    for a, (own, p, g) in enumerate(zip(own_sums, rb_plain, rb_merged)):
        _copy(p, p, send_b, recv_b, a, B_PLAIN, (*second, c)).wait_recv()
        _copy(g, g, send_b, recv_b, a, B_MERGED, (*second, c)).wait_recv()
        out.append(own[...] + p[...].astype(F32) + g[...].astype(F32))
    for k, srcs in ((B_PLAIN, plain), (B_ON, on), (B_MERGED, merged)):
        for a, s in enumerate(srcs):
            _copy(s, s, send_b, recv_b, a, k, (*second, c)).wait_send()
    return out


R_LOSS, R_FINAL_G, R_POOL, R_CONV, R_LN = range(5)
VEC_ROWS = 16
DW_ROWS = 32
A_PART, A_X, A_Y, A_ON, A_HALF = range(5)
STEP_SUMS, STEP_MERGE, STEP_HALF = 1, 3, 5


def _bwd2a_body(n_steps, x_ref, dproj_ref, ng_ref, gwout_ref, gpww_ref, loss_ref, gfg_ref, gvec_ref, gdw_ref, gpoolw_ref,
                win_own_ref, win_staged_ref, owout_ref, opww_ref, ovec_ref, odw_ref, opoolw_ref,
                acc, a_win, a_wout, a_pww, st_wout, st_pww, rb_wout, rb_pww, part_vec, part_dw, part_pw,
                sib_vec, sib_dw, sib_pw, res_vec, res_dw, res_pw, half_vec, half_dw, half_pw,
                send_a, recv_a, send_b, recv_b, send_r, recv_r, send_w, recv_w):
    step = pl.program_id(0) * pl.num_programs(1) + pl.program_id(1)
    x, y, c = lax.axis_index("x"), lax.axis_index("y"), lax.axis_index("c")
    sibling = (x, y, 1 - c)
    xn, yn = (1 - x, y), (x, 1 - y)
    first, second, diag, own, _ = _routes()
    rows = lambda ref, k, n: ref.at[pl.ds(pl.multiple_of(k * n, n), n), :]

    blocks = lambda k: (rows(gwout_ref, k, W_OUT_BLK), rows(gpww_ref, k, PW_BLK))
    landed = lambda chip: (a_wout.at[chip], a_pww.at[chip])
    staged = lambda k: (st_wout.at[k], st_pww.at[k])
    arrived = lambda k: (rb_wout.at[k], rb_pww.at[k])

    parts, sibs, halves = (part_vec, part_dw, part_pw), (sib_vec, sib_dw, sib_pw), (half_vec, half_dw, half_pw)
    results = (res_vec, res_dw, res_pw)
    half_rows = lambda ref: pl.ds(pl.multiple_of(c * (ref.shape[0] // 2), SUBLANES), ref.shape[0] // 2)
    slot = lambda chip: tuple(h.at[_chip_index(chip)] for h in halves)

    @pl.when(step == 0)
    def _():
        acc[...] = jnp.zeros_like(acc)
        sum8 = lambda v: jnp.sum(v, axis=0, keepdims=True)
        part_vec[...] = jnp.zeros((VEC_ROWS, D_MODEL), F32)
        part_vec[R_LOSS:R_LOSS + 1, :] = jnp.broadcast_to(loss_ref[0:1, 0:1], (1, D_MODEL))
        part_vec[R_FINAL_G:R_FINAL_G + 1, :] = sum8(gfg_ref[...])
        for row, (lo, hi) in ((R_POOL, (V_POOL_B, V_POOL_SCALE)), (R_CONV, (V_CONV_B, V_LN_G)), (R_LN, (V_LN_B, V_PW_B))):
            part_vec[row:row + 1, 0:W_MIX] = sum8(gvec_ref[lo])
            part_vec[row:row + 1, W_MIX:] = sum8(gvec_ref[hi])
        part_dw[0:CONV_WIDTH, :] = jnp.sum(gdw_ref[...], axis=1)
        part_dw[CONV_WIDTH:, :] = jnp.zeros((DW_ROWS - CONV_WIDTH, W_MIX), F32)
        part_pw[...] = gpoolw_ref[...].reshape(len(POOL_WINDOWS) * POOL_GC, POOL_GC)
        for a, (p, s) in enumerate(zip(parts, sibs)):
            _copy(p, s, send_r, recv_r, a, A_PART, sibling).start()
        for chip in range(4):
            for a, (s, d) in enumerate(zip(blocks(2 * chip + (1 - c)), landed(chip))):
                _copy(s, d, send_a, recv_a, a, chip, sibling).start()

    x_t = x_ref[0]
    r1 = lax.rsqrt(jnp.mean(x_t * x_t, axis=-1, keepdims=True) + RMS_EPS)
    hn = (x_t * r1 * ng_ref[...]).astype(BF16)
    acc[...] += lax.dot_general(dproj_ref[0], hn, TN_DIMS, preferred_element_type=F32)

    @pl.when(step == STEP_SUMS)
    def _():
        for a, (p, s, h) in enumerate(zip(parts, sibs, slot(own))):
            _copy(p, s, send_r, recv_r, a, A_PART, sibling).wait_recv()
            h[...] = p[half_rows(p), :] + s[half_rows(s), :]
            for k, to in ((A_X, xn), (A_Y, yn)):
                _copy(h, h, send_r, recv_r, a, k, (*to, c)).start()
        for chip in range(4):
            for a, (m, l) in enumerate(zip(blocks(2 * chip + c), landed(chip))):
                _copy(l, l, send_a, recv_a, a, chip, sibling).wait_recv()
                l[...] = m[...] + l[...]
        for k, chip in ((B_ON, diag), (B_PLAIN, first)):
            for s, l in zip(staged(k), landed(_chip_index(chip))):
                s[...] = l[...].astype(BF16)
        _rsb_send(staged(B_PLAIN), staged(B_ON), arrived(B_PLAIN), arrived(B_ON), send_b, recv_b)

    @pl.when(step == STEP_MERGE)
    def _():
        _rsb_merge(landed(_chip_index(second)), staged(B_MERGED), arrived(B_ON), arrived(B_MERGED), send_b, recv_b)
        for a in range(3):
            for k, chip in ((A_X, xn), (A_Y, yn)):
                h = slot(chip)[a]
                _copy(h, h, send_r, recv_r, a, k, sibling).wait_recv()
            h = slot(first)[a]
            _copy(h, h, send_r, recv_r, a, A_ON, (*second, c)).start()

    @pl.when(step == STEP_HALF)
    def _():
        for a, (r, hs) in enumerate(zip(results, halves)):
            h = slot(diag)[a]
            _copy(h, h, send_r, recv_r, a, A_ON, sibling).wait_recv()
            mine = r.at[half_rows(r), :]
            mine[...] = (hs[0] + hs[1]) + (hs[2] + hs[3])
            _copy(mine, mine, send_r, recv_r, a, A_HALF, sibling).start()

    @pl.when(step == n_steps - 1)
    def _():
        block = lambda k: rows(acc, k, W_IN_BLK)
        w_copy = lambda chip: _copy(block(2 * chip + (1 - c)), a_win.at[chip], send_w, recv_w, 0, chip, sibling)
        for chip in range(4):
            w_copy(chip).start()

        owout_ref[...], opww_ref[...] = _rsb_total(landed(_chip_index(own)), staged(B_PLAIN), staged(B_ON), staged(B_MERGED),
                                                   arrived(B_PLAIN), arrived(B_MERGED), send_b, recv_b)
        for a, (r, p, s) in enumerate(zip(results, parts, sibs)):
            theirs = r.at[pl.ds(pl.multiple_of((1 - c) * (r.shape[0] // 2), SUBLANES), r.shape[0] // 2), :]
            _copy(theirs, theirs, send_r, recv_r, a, A_HALF, sibling).wait_recv()
            _copy(p, s, send_r, recv_r, a, A_PART, sibling).wait_send()
            for k, h in ((A_X, slot(own)[a]), (A_Y, slot(own)[a]), (A_ON, slot(first)[a])):
                _copy(h, h, send_r, recv_r, a, k, sibling).wait_send()
            mine = r.at[half_rows(r), :]
            _copy(mine, mine, send_r, recv_r, a, A_HALF, sibling).wait_send()
        for chip in range(4):
            for a, (s, d) in enumerate(zip(blocks(2 * chip + (1 - c)), landed(chip))):
                _copy(s, d, send_a, recv_a, a, chip, sibling).wait_send()
        ovec_ref[...] = res_vec[...]
        opoolw_ref[...] = res_pw[...]
        for k in range(N_DEV):
            @pl.when(4 * x + 2 * y + c == k)
            def _():
                odw_ref[...] = res_dw[0:CONV_WIDTH, k * DW_BLK:(k + 1) * DW_BLK]

        for chip in range(4):
            w_copy(chip).wait()
        chip_sum = lambda chip: block(2 * _chip_index(chip) + c)[...] + a_win[_chip_index(chip)]
        win_own_ref[...] = chip_sum(own)
        for j, chip in enumerate((xn, yn, diag)):
            win_staged_ref[j] = chip_sum(chip).astype(BF16)


def _bwd2a(x, dproj, norm_g, gwout, gpww, loss8, gfg8, gvec, gdw8, gpoolw):
    bl, s, _ = x.shape
    tm = TILE_M
    nt = s // tm
    tok = lambda n: pl.BlockSpec((1, tm, n), lambda b, i: (b, i, 0))
    n_pw = len(POOL_WINDOWS) * POOL_GC
    win = (W_IN_BLK, D_MODEL)
    blk = ((W_OUT_BLK, D_MODEL), (PW_BLK, W_MIX))
    rep = ((VEC_ROWS, D_MODEL), (DW_ROWS, W_MIX), (n_pw, POOL_GC))
    half = lambda sh: (sh[0] // 2,) + sh[1:]
    dma = pltpu.SemaphoreType.DMA
    assert bl * nt > STEP_HALF + 1, "the exchanges' phases need their grid steps"
    return pl.pallas_call(
        functools.partial(_bwd2a_body, bl * nt),
        name="bwd2a",
        grid=(bl, nt),
        in_specs=[tok(D_MODEL), tok(D_IN)] + [VMEM] * 8,
        out_specs=[VMEM] * 7,
        out_shape=[jax.ShapeDtypeStruct(win, F32), jax.ShapeDtypeStruct((3,) + win, BF16)]
        + [jax.ShapeDtypeStruct(sh, F32) for sh in blk + (rep[0], (CONV_WIDTH, DW_BLK), rep[2])],
        scratch_shapes=[pltpu.VMEM((D_IN, D_MODEL), F32), pltpu.VMEM((4,) + win, F32)]
        + [pltpu.VMEM((4,) + sh, F32) for sh in blk]
        + [pltpu.VMEM((3,) + sh, BF16) for sh in blk] * 2
        + [pltpu.VMEM(sh, F32) for sh in rep] * 3
        + [pltpu.VMEM((4,) + half(sh), F32) for sh in rep]
        + [dma((2, 4)), dma((2, 4)), dma((2, 3)), dma((2, 3)), dma((3, 5)), dma((3, 5)), dma((1, 4)), dma((1, 4))],
        compiler_params=pltpu.CompilerParams(dimension_semantics=("arbitrary", "arbitrary"), vmem_limit_bytes=VMEM_LIMIT),
    )(x, dproj, _row(norm_g), gwout, gpww, loss8, gfg8, gvec, gdw8, gpoolw)


HBM = pl.BlockSpec(memory_space=pltpu.HBM)
SEM = pl.BlockSpec(memory_space=pltpu.SEMAPHORE)
EFFECT = pltpu.SideEffectType.DATAFLOW_SIDE_EFFECTING


def _owner_copies(src_ref, land_ref, send_sems, recv_sems):
    x, y, c = lax.axis_index("x"), lax.axis_index("y"), lax.axis_index("c")
    owners = ((1 - x, y, c), (x, 1 - y, c), (1 - x, 1 - y, c))
    return [pltpu.make_async_remote_copy(src_ref=src_ref.at[j], dst_ref=land_ref.at[j], send_sem=send_sems.at[j],
                                         recv_sem=recv_sems.at[j], device_id=owners[j], device_id_type=MESH)
            for j in range(3)]


def _rs_start(staged):
    def body(src_ref, land_ref, send_sems, recv_sems, src_thru, land_thru, token):
        for cp in _owner_copies(src_ref, land_ref, send_sems, recv_sems):
            cp.start()
        token[...] = jnp.zeros_like(token)

    dma = pltpu.SemaphoreType.DMA
    return pl.pallas_call(
        body, name="rs_start",
        out_shape=(dma((3,)), dma((3,)), pltpu.HBM(staged.shape, staged.dtype), pltpu.HBM(staged.shape, staged.dtype),
                   jax.ShapeDtypeStruct((SUBLANES, LANES), F32)),
        in_specs=(HBM, HBM), out_specs=(SEM, SEM, HBM, HBM, VMEM), input_output_aliases={0: 2, 1: 3},
        compiler_params=pltpu.CompilerParams(has_side_effects=EFFECT),
    )(pltpu.with_memory_space_constraint(staged, pltpu.HBM),
      pltpu.with_memory_space_constraint(lax.empty(staged.shape, staged.dtype), pltpu.HBM))


def _rs_wait(send_sems, recv_sems, src_thru, land_thru, after):
    def body(src_ref, land_ref, send_sems, recv_sems, after_ref, src_dead, got_ref):
        for cp in _owner_copies(src_ref, land_ref, send_sems, recv_sems):
            cp.wait_send()
            cp.wait_recv()

    return pl.pallas_call(
        body, name="rs_wait",
        out_shape=(pltpu.HBM(src_thru.shape, src_thru.dtype), pltpu.HBM(land_thru.shape, land_thru.dtype)),
        in_specs=(HBM, HBM, SEM, SEM, pl.BlockSpec(memory_space=pl.ANY)), out_specs=(HBM, HBM),
        input_output_aliases={0: 0, 1: 1},
        compiler_params=pltpu.CompilerParams(has_side_effects=EFFECT),
    )(src_thru, land_thru, send_sems, recv_sems, after)[1]


def _peer_copies(lands, send_sems, recv_sems):
    x, y, c = lax.axis_index("x"), lax.axis_index("y"), lax.axis_index("c")
    k = 4 * x + 2 * y + c
    other_chips = [(1 - x, y), (x, 1 - y), (1 - x, 1 - y)]
    peers = [(x, y, 1 - c)] + [(*chip, c) for chip in other_chips] + [(*chip, 1 - c) for chip in other_chips]
    mine = (lands[0].at[pl.ds(pl.multiple_of(k * W_OUT_BLK, W_OUT_BLK), W_OUT_BLK), :],
            lands[1].at[pl.ds(pl.multiple_of(k * PW_BLK, PW_BLK), PW_BLK), :], lands[2].at[k])
    return mine, [pltpu.make_async_remote_copy(src_ref=m, dst_ref=m, send_sem=send_sems.at[a * N_PEER + r],
                                               recv_sem=recv_sems.at[a * N_PEER + r], device_id=peers[r], device_id_type=MESH)
                  for r in range(N_PEER) for a, m in enumerate(mine)]


def _weights_start(w_out, pw_w, conv_dw):
    def body(wout_ref, pww_ref, dw_ref, l0, l1, l2, send_sems, recv_sems, t0, t1, t2, token, wout_b, pww_b):
        mine, copies = _peer_copies((l0, l1, l2), send_sems, recv_sems)
        wout_b[...] = wout_ref[...].astype(BF16)
        pww_b[...] = pww_ref[...].astype(BF16)
        for src, dst in zip((wout_b, pww_b, dw_ref), mine):
            pltpu.sync_copy(src, dst)
        for cp in copies:
            cp.start()
        token[...] = jnp.zeros_like(token)

    dma = pltpu.SemaphoreType.DMA
    shapes = (((D_MODEL, D_MODEL), BF16), ((W_MIX, W_MIX), BF16), ((N_DEV, CONV_WIDTH, DW_BLK), F32))
    lands = [pltpu.with_memory_space_constraint(lax.empty(sh, dt), pltpu.HBM) for sh, dt in shapes]
    return pl.pallas_call(
        body, name="weights_start",
        out_shape=(dma((3 * N_PEER,)), dma((3 * N_PEER,))) + tuple(pltpu.HBM(sh, dt) for sh, dt in shapes)
        + (jax.ShapeDtypeStruct((SUBLANES, LANES), F32),),
        in_specs=(VMEM, VMEM, VMEM, HBM, HBM, HBM), out_specs=(SEM, SEM, HBM, HBM, HBM, VMEM),
        input_output_aliases={3: 2, 4: 3, 5: 4},
        scratch_shapes=[pltpu.VMEM((W_OUT_BLK, D_MODEL), BF16), pltpu.VMEM((PW_BLK, W_MIX), BF16)],
        compiler_params=pltpu.CompilerParams(has_side_effects=EFFECT),
    )(w_out, pw_w, conv_dw, *lands)


def _weights_wait(send_sems, recv_sems, lands, after):
    def body(l0, l1, l2, send_sems, recv_sems, after_ref, o0, o1, o2):
        _, copies = _peer_copies((l0, l1, l2), send_sems, recv_sems)
        for cp in copies:
            cp.wait_send()
            cp.wait_recv()

    return pl.pallas_call(
        body, name="weights_wait",
        out_shape=tuple(pltpu.HBM(l.shape, l.dtype) for l in lands),
        in_specs=(HBM, HBM, HBM, SEM, SEM, pl.BlockSpec(memory_space=pl.ANY)), out_specs=(HBM, HBM, HBM),
        input_output_aliases={0: 0, 1: 1, 2: 2},
        compiler_params=pltpu.CompilerParams(has_side_effects=EFFECT),
    )(*lands, send_sems, recv_sems, after)


def _bwd2b_body(x_ref, dh_ref, dproj_ref, ng_ref, wint_ref, token_ref, gx_ref, gng_ref):
    @pl.when((pl.program_id(0) == 0) & (pl.program_id(1) == 0))
    def _():
        gng_ref[...] = jnp.zeros_like(gng_ref) + token_ref[0:1, 0:1]

    x_t = x_ref[0]
    r1 = lax.rsqrt(jnp.mean(x_t * x_t, axis=-1, keepdims=True) + RMS_EPS)
    xr = x_t * r1
    dhn = jnp.dot(dproj_ref[0], wint_ref[...], preferred_element_type=F32)
    gng_ref[...] += _colsum8(dhn * xr)
    gy = dhn * ng_ref[...]
    gx_ref[0] = dh_ref[0] + r1 * (gy - xr * jnp.mean(gy * xr, axis=-1, keepdims=True))


def _bwd2b(x, dh, dproj, norm_g, w_in_t, token):
    bl, s, _ = x.shape
    tm = TILE_M
    tok = lambda n: pl.BlockSpec((1, tm, n), lambda b, i: (b, i, 0))
    return pl.pallas_call(
        functools.partial(_bwd2b_body),
        name="bwd2b",
        grid=(bl, s // tm),
        in_specs=[tok(D_MODEL), tok(D_MODEL), tok(D_IN), VMEM, VMEM, VMEM],
        out_specs=[tok(D_MODEL), pl.BlockSpec((SUBLANES, D_MODEL), lambda b, i: (0, 0))],
        out_shape=[jax.ShapeDtypeStruct((bl, s, D_MODEL), F32), jax.ShapeDtypeStruct((SUBLANES, D_MODEL), F32)],
        compiler_params=pltpu.CompilerParams(dimension_semantics=("arbitrary", "arbitrary"), vmem_limit_bytes=VMEM_LIMIT),
    )(x, dh, dproj, _row(norm_g), w_in_t, token)


def _norm_g_body(part_ref, out_ref, all_ng, send_n, recv_n):
    x, y, c = lax.axis_index("x"), lax.axis_index("y"), lax.axis_index("c")
    other_chips = [(1 - x, y), (x, 1 - y), (1 - x, 1 - y)]
    k_me = 4 * x + 2 * y + c
    peers = [(x, y, 1 - c)] + [(*chip, c) for chip in other_chips] + [(*chip, 1 - c) for chip in other_chips]
    copies = [pltpu.make_async_remote_copy(src_ref=all_ng.at[k_me], dst_ref=all_ng.at[k_me], send_sem=send_n.at[r],
                                           recv_sem=recv_n.at[r], device_id=peers[r], device_id_type=MESH)
              for r in range(N_PEER)]
    all_ng[k_me] = part_ref[...]
    for cp in copies:
        cp.start()
    for cp in copies:
        cp.wait()
    s = all_ng[0]
    for k in range(1, N_DEV):
        s = s + all_ng[k]
    out_ref[...] = jnp.broadcast_to(jnp.sum(s, axis=0, keepdims=True), out_ref.shape)


def _norm_g_sum(part):
    dma = pltpu.SemaphoreType.DMA
    return pl.pallas_call(
        functools.partial(_norm_g_body),
        name="norm_g_sum",
        in_specs=[VMEM],
        out_specs=VMEM,
        out_shape=jax.ShapeDtypeStruct(part.shape, F32),
        scratch_shapes=[pltpu.VMEM((N_DEV,) + part.shape, F32), dma((N_PEER,)), dma((N_PEER,))],
    )(part)


ADAM_LR = 0.001
ADAM_B1 = 0.9
ADAM_B2 = 0.999
ADAM_EPS = 1e-08
ADAM_WD = 0.01
ADAM_STEP = 10


PARAMS = ("norm_g", "w_in", "pool_w", "pool_b", "pool_scale", "conv_dw", "conv_b", "ln_g", "ln_b", "pw_w", "pw_b", "w_out",
          "final_g")


def _adamw_body(*refs):
    n = len(PARAMS)
    ws, ms, vs = (dict(zip(PARAMS, refs[k * n:(k + 1) * n])) for k in range(3))
    gwin_ref, gwin_others_ref, gwout_ref, gpww_ref, vec_ref, gdw_ref, gpoolw_ref, gng_ref = refs[3 * n:3 * n + 8]
    outs = refs[3 * n + 8:]
    gs, ds, m2s, v2s = (dict(zip(PARAMS, outs[k * n:(k + 1) * n])) for k in range(4))
    loss_ref = outs[4 * n]

    def update(name, g, at=slice(None)):
        m = ADAM_B1 * ms[name][at] + (1.0 - ADAM_B1) * g
        v = ADAM_B2 * vs[name][at] + (1.0 - ADAM_B2) * (g * g)
        m_hat = m / (1.0 - ADAM_B1 ** ADAM_STEP)
        v_hat = v / (1.0 - ADAM_B2 ** ADAM_STEP)
        gs[name][at] = g
        ds[name][at] = -ADAM_LR * (m_hat / (jnp.sqrt(v_hat) + ADAM_EPS) + ADAM_WD * ws[name][at])
        m2s[name][at] = m
        v2s[name][at] = v

    row = lambda r, lo, hi: vec_ref[r:r + 1, lo:hi]
    loss_ref[...] = row(R_LOSS, 0, 1)
    update("norm_g", gng_ref[0:1, :])
    g_w_in = gwin_ref[...]
    for j in range(3):
        g_w_in = g_w_in + gwin_others_ref[j].astype(F32)
    update("w_in", g_w_in)
    update("pool_w", gpoolw_ref[...].reshape(ws["pool_w"].shape))
    for g in range(len(POOL_WINDOWS)):
        update("pool_b", row(R_POOL, g * POOL_GC, (g + 1) * POOL_GC), at=slice(g, g + 1))
    update("pool_scale", row(R_POOL, W_MIX, 2 * W_MIX))
    update("conv_dw", gdw_ref[...])
    update("conv_b", row(R_CONV, 0, W_MIX))
    update("ln_g", row(R_CONV, W_MIX, 2 * W_MIX))
    update("ln_b", row(R_LN, 0, W_MIX))
    update("pw_w", gpww_ref[...])
    update("pw_b", row(R_LN, W_MIX, 2 * W_MIX))
    update("w_out", gwout_ref[...])
    update("final_g", row(R_FINAL_G, 0, D_MODEL))


def _adamw(ws, ms, vs, grad_sources):
    n = len(PARAMS)
    shapes = [ws[p].shape for p in PARAMS]
    flat = lambda a: a.reshape(1, -1) if a.ndim == 1 else a
    args = [flat(d[p]) for d in (ws, ms, vs) for p in PARAMS]
    outs = pl.pallas_call(
        functools.partial(_adamw_body),
        name="adamw",
        in_specs=[VMEM] * (3 * n + len(grad_sources)),
        out_specs=[VMEM] * (4 * n + 1),
        out_shape=[jax.ShapeDtypeStruct(a.shape, F32) for a in args[:n]] * 4 + [jax.ShapeDtypeStruct((1, 1), F32)],
        compiler_params=pltpu.CompilerParams(vmem_limit_bytes=VMEM_LIMIT),
    )(*args, *grad_sources)
    loss = outs[4 * n].reshape(())
    outs = [o.reshape(shapes[k % n]) for k, o in enumerate(outs[:4 * n])]
    return loss, outs[:n], outs[n:2 * n], outs[2 * n:3 * n], outs[3 * n:]


def kernel(x, norm_g, w_in, pool_w, pool_b, pool_scale, conv_dw, conv_b, ln_g, ln_b, pw_w, pw_b, w_out, final_g, loss_target, m_norm_g, m_w_in, m_pool_w, m_pool_b, m_pool_scale, m_conv_dw, m_conv_b, m_ln_g, m_ln_b, m_pw_w, m_pw_b, m_w_out, m_final_g, v_norm_g, v_w_in, v_pool_w, v_pool_b, v_pool_scale, v_conv_dw, v_conv_b, v_ln_g, v_ln_b, v_pw_w, v_pw_b, v_w_out, v_final_g):
    ws_send, ws_recv, *lands, token = _weights_start(w_out, pw_w, conv_dw)
    proj, w_in_t = _inproj(x, norm_g, w_in.T, token)
    w_out_b, pw_w_b, dw_blocks = _weights_wait(ws_send, ws_recv, lands, w_in_t)
    conv_dw_f = jnp.transpose(dw_blocks, (1, 0, 2)).reshape(CONV_WIDTH, W_MIX)
    h1, z, ob, dpool, dh, loss8, gfg8 = _fwd(x, loss_target, proj, pool_w, pool_b, pool_scale, conv_dw_f,
                                             conv_b, ln_g, ln_b, pw_w_b, pw_b, w_out_b, final_g)
    dproj, gwout, gpww, gpoolw, gvec, gdw8 = _bwd1(dh, proj, h1, z, ob, dpool, w_out_b, pw_w_b, pool_w, pool_scale,
                                                   conv_dw_f, ln_g, ln_b)
    win_own, win_staged, g_w_out, g_pw_w, vec, g_conv_dw, poolw_sum = _bwd2a(x, dproj, norm_g, gwout, gpww, loss8, gfg8, gvec,
                                                                             gdw8, gpoolw)
    send_sems, recv_sems, staged_thru, land_thru, token = _rs_start(win_staged)
    grad_x, gng8 = _bwd2b(x, dh, dproj, norm_g, w_in_t, token)
    g_norm_g = _norm_g_sum(gng8)
    win_others = _rs_wait(send_sems, recv_sems, staged_thru, land_thru, gng8)

    weights = dict(norm_g=norm_g, w_in=w_in.T, pool_w=pool_w, pool_b=pool_b, pool_scale=pool_scale, conv_dw=conv_dw, conv_b=conv_b,
                   ln_g=ln_g, ln_b=ln_b, pw_w=pw_w, pw_b=pw_b, w_out=w_out, final_g=final_g)
    m_in = dict(norm_g=m_norm_g, w_in=m_w_in.T, pool_w=m_pool_w, pool_b=m_pool_b, pool_scale=m_pool_scale, conv_dw=m_conv_dw,
                conv_b=m_conv_b, ln_g=m_ln_g, ln_b=m_ln_b, pw_w=m_pw_w, pw_b=m_pw_b, w_out=m_w_out, final_g=m_final_g)
    v_in = dict(norm_g=v_norm_g, w_in=v_w_in.T, pool_w=v_pool_w, pool_b=v_pool_b, pool_scale=v_pool_scale, conv_dw=v_conv_dw,
                conv_b=v_conv_b, ln_g=v_ln_g, ln_b=v_ln_b, pw_w=v_pw_w, pw_b=v_pw_b, w_out=v_w_out, final_g=v_final_g)
    loss, *groups = _adamw(weights, m_in, v_in, (win_own, win_others, g_w_out, g_pw_w, vec, g_conv_dw, poolw_sum, g_norm_g))
    k_in = PARAMS.index("w_in")
    for group in groups:
        group[k_in] = group[k_in].T
    grads, delta, new_m, new_v = groups
    return (loss, grad_x, *grads, *delta, *new_m, *new_v)
```

```python
import functools

import jax
import jax.numpy as jnp
from jax import lax
from jax.experimental import pallas as pl
from jax.experimental.pallas import tpu as pltpu

F32 = jnp.float32
BF16 = jnp.bfloat16

D_MODEL = 1024
W_MIX = 512
D_IN = 5 * W_MIX
POOL_WINDOWS = (2, 4, 8, 16)
POOL_GC = 128
CONV_WIDTH = 31
RMS_EPS = 1e-6
LN_EPS = 1e-5
N_DEV = 8

LANES = 128
SUBLANES = 8
TILE_M = 512
TILE_IN = 1024
POOL_HALO = 16
CONV_HALO = 32
CONV_ROWS = 128
BWD_CONV_ROWS = 64
BWD_SUBTILES = 2
VMEM_LIMIT = 56 * 1024 * 1024

VMEM = pl.BlockSpec(memory_space=pltpu.VMEM)


def _silu(v):
    return v * jax.nn.sigmoid(v)


def _colsum8(v):
    m, n = v.shape
    return jnp.sum(v.reshape(m // SUBLANES, SUBLANES, n), axis=0)


def _row_index(shape, t0):
    return lax.broadcasted_iota(jnp.int32, shape, 0) + t0


def _conv_taps():
    taps = {s: [] for s in range(SUBLANES)}
    for k in range(CONV_WIDTH):
        e = k + CONV_HALO - (CONV_WIDTH - 1)
        taps[e % SUBLANES].append((e // SUBLANES, k))
    return taps


def _fwd_body(x_ref, tgt_ref, proj_ref, poolw_ref, poolb_ref, pools_ref, dw_ref, cb_ref, lg_ref, lb_ref,
              pww_ref, pwb_ref, wout_ref, fg_ref,
              h1_ref, z_ref, ob_ref, dpool_ref, dh_ref, loss_ref, gfg_ref,
              uext, hext, y_s):
    b = pl.program_id(0)
    i = pl.program_id(1)
    tm = TILE_M

    @pl.when(i == 0)
    def _():
        uext[0:POOL_HALO, :] = jnp.zeros((POOL_HALO, W_MIX), F32)
        hext[0:CONV_HALO, :] = jnp.zeros((CONV_HALO, W_MIX), F32)
        hext[CONV_HALO + tm:, :] = jnp.zeros((SUBLANES, W_MIX), F32)

    @pl.when((b == 0) & (i == 0))
    def _():
        loss_ref[...] = jnp.zeros_like(loss_ref)
        gfg_ref[...] = jnp.zeros_like(gfg_ref)

    uext[POOL_HALO:, :] = proj_ref[0, :, 0:W_MIX]
    t = _row_index((tm, POOL_GC), i * tm)
    for g, w in enumerate(POOL_WINDOWS):
        lanes = slice(g * POOL_GC, (g + 1) * POOL_GC)
        e = uext[:, lanes]
        s = e
        sh = 1
        while sh < w:
            s = s + pltpu.roll(s, sh, 0)
            sh *= 2
        inv_cnt = 1.0 / jnp.minimum(t + 1, w).astype(F32)
        d = (s[POOL_HALO:] * inv_cnt - e[POOL_HALO:]).astype(BF16)
        dpool_ref[0, :, lanes] = d
        z = jnp.dot(d, poolw_ref[g].astype(BF16), preferred_element_type=F32) + poolb_ref[:, lanes]
        z_ref[0, :, lanes] = z
        a_gate = proj_ref[0, :, W_MIX + g * POOL_GC:W_MIX + (g + 1) * POOL_GC]
        y_s[:, lanes] = (z * pools_ref[:, lanes] * _silu(a_gate)).astype(BF16)

    hext[CONV_HALO:CONV_HALO + tm, :] = proj_ref[0, :, 2 * W_MIX:3 * W_MIX] * jax.nn.sigmoid(proj_ref[0, :, 3 * W_MIX:4 * W_MIX])
    taps = _conv_taps()
    for c0 in range(0, tm, CONV_ROWS):
        for l0 in range(0, W_MIX, LANES):
            lanes = slice(l0, l0 + LANES)
            acc = jnp.zeros((CONV_ROWS, LANES), F32) + cb_ref[:, lanes]
            for s in range(SUBLANES):
                part = jnp.zeros((CONV_ROWS + SUBLANES, LANES), F32)
                for q, k in taps[s]:
                    r0 = c0 + SUBLANES * q
                    part = part + hext[r0:r0 + CONV_ROWS + SUBLANES, lanes] * dw_ref[k:k + 1, lanes]
                acc = acc + part[s:s + CONV_ROWS]
            h1_ref[0, c0:c0 + CONV_ROWS, lanes] = acc
    h1 = h1_ref[0]
    mu = jnp.mean(h1, axis=-1, keepdims=True)
    xc = h1 - mu
    rstd = lax.rsqrt(jnp.mean(xc * xc, axis=-1, keepdims=True) + LN_EPS)
    h3 = _silu(xc * rstd * lg_ref[...] + lb_ref[...])
    ob = jnp.dot(h3.astype(BF16), pww_ref[...], preferred_element_type=F32) + pwb_ref[...]
    ob_ref[0] = ob
    y_s[:, W_MIX:] = (ob * _silu(proj_ref[0, :, 4 * W_MIX:5 * W_MIX])).astype(BF16)

    h = x_ref[0] + jnp.dot(y_s[...], wout_ref[...], preferred_element_type=F32)
    r2 = lax.rsqrt(jnp.mean(h * h, axis=-1, keepdims=True) + RMS_EPS)
    hr = h * r2
    err = hr * fg_ref[...] - tgt_ref[0]
    loss_ref[...] += jnp.sum(err * err) * (0.5 / D_MODEL)
    dout = err * (1.0 / D_MODEL)
    gfg_ref[...] += _colsum8(dout * hr)
    gy = dout * fg_ref[...]
    dh_ref[0] = r2 * (gy - hr * jnp.mean(gy * hr, axis=-1, keepdims=True))

    uext[0:POOL_HALO, :] = uext[tm:tm + POOL_HALO, :]
    hext[0:CONV_HALO, :] = hext[tm:tm + CONV_HALO, :]


def _row(v):
    return v.reshape(1, -1)


def _fwd(x, tgt, proj, pool_w, pool_b, pool_scale, conv_dw_f, conv_b, ln_g, ln_b, pw_w_b, pw_b, w_out_b, final_g):
    bl, s, _ = x.shape
    tm = TILE_M
    tok = lambda n: pl.BlockSpec((1, tm, n), lambda b, i: (b, i, 0))
    acc = lambda n: pl.BlockSpec((SUBLANES, n), lambda b, i: (0, 0))
    act = lambda n: jax.ShapeDtypeStruct((bl, s, n), F32)
    return pl.pallas_call(
        functools.partial(_fwd_body),
        name="fwd",
        grid=(bl, s // tm),
        in_specs=[tok(D_MODEL), tok(D_MODEL), tok(D_IN)] + [VMEM] * 11,
        out_specs=[tok(W_MIX), tok(W_MIX), tok(W_MIX), tok(W_MIX), tok(D_MODEL), acc(LANES), acc(D_MODEL)],
        out_shape=[act(W_MIX), act(W_MIX), act(W_MIX), jax.ShapeDtypeStruct((bl, s, W_MIX), BF16), act(D_MODEL),
                   jax.ShapeDtypeStruct((SUBLANES, LANES), F32), jax.ShapeDtypeStruct((SUBLANES, D_MODEL), F32)],
        scratch_shapes=[pltpu.VMEM((POOL_HALO + tm, W_MIX), F32), pltpu.VMEM((CONV_HALO + tm + SUBLANES, W_MIX), F32),
                        pltpu.VMEM((tm, D_MODEL), BF16)],
        compiler_params=pltpu.CompilerParams(dimension_semantics=("arbitrary", "arbitrary"), vmem_limit_bytes=VMEM_LIMIT),
    )(x, tgt, proj, pool_w, pool_b.reshape(1, W_MIX), _row(pool_scale), conv_dw_f, _row(conv_b),
      _row(ln_g), _row(ln_b), pw_w_b, _row(pw_b), w_out_b, _row(final_g))


NT_DIMS = (((1,), (1,)), ((), ()))
TN_DIMS = (((0,), (0,)), ((), ()))
V_POOL_B, V_POOL_SCALE, V_CONV_B, V_LN_G, V_LN_B, V_PW_B = range(6)


def _bwd_taps():
    taps = {s: [] for s in range(SUBLANES)}
    for k in range(CONV_WIDTH):
        e = CONV_WIDTH - 1 - k
        taps[e % SUBLANES].append((e // SUBLANES, k))
    return taps


def _bwd1_body(n_tiles, dh_ref, proj_ref, h1_ref, z_ref, ob_ref, dpool_ref, wout_ref, pww_ref, poolw_ref, pools_ref, dw_ref,
               lg_ref, lb_ref,
               dproj_ref, gwout_ref, gpww_ref, gpoolw_ref, gvec_ref, gdw_ref,
               vext, gext, gsh, y_s):
    b = pl.program_id(0)
    i = pl.program_id(1)
    tm = TILE_M
    t0 = (n_tiles - 1 - i) * tm

    @pl.when(i == 0)
    def _():
        vext[tm:, :] = jnp.zeros((POOL_HALO, W_MIX), F32)
        gext[tm:, :] = jnp.zeros((CONV_HALO, W_MIX), F32)

    @pl.when((b == 0) & (i == 0))
    def _():
        gwout_ref[...] = jnp.zeros_like(gwout_ref)
        gpww_ref[...] = jnp.zeros_like(gpww_ref)
        gpoolw_ref[...] = jnp.zeros_like(gpoolw_ref)
        gvec_ref[...] = jnp.zeros_like(gvec_ref)
        gdw_ref[...] = jnp.zeros_like(gdw_ref)

    sub = tm // BWD_SUBTILES

    def dense_phase(r0):
        rs = slice(r0, r0 + sub)
        dhb = dh_ref[0, rs, :].astype(BF16)
        dy_a = lax.dot_general(dhb, wout_ref[0:W_MIX, :], NT_DIMS, preferred_element_type=F32)
        dy_b = lax.dot_general(dhb, wout_ref[W_MIX:, :], NT_DIMS, preferred_element_type=F32)

        a_gate = proj_ref[0, rs, W_MIX:2 * W_MIX]
        sig_a = jax.nn.sigmoid(a_gate)
        silu_a = a_gate * sig_a
        z = z_ref[0, rs, :]
        out_a = z * pools_ref[...]
        y_s[rs, 0:W_MIX] = (out_a * silu_a).astype(BF16)
        d_out_a = dy_a * silu_a
        d_ag = dy_a * out_a * (sig_a * (1.0 + a_gate * (1.0 - sig_a)))
        dproj_ref[0, rs, W_MIX:2 * W_MIX] = d_ag.astype(BF16)
        gvec_ref[V_POOL_SCALE] += _colsum8(d_out_a * z)
        dz = d_out_a * pools_ref[...]
        gvec_ref[V_POOL_B] += _colsum8(dz)
        t = _row_index((sub, POOL_GC), t0 + r0)
        n_ext = sub + POOL_HALO
        for g, w in enumerate(POOL_WINDOWS):
            lanes = slice(g * POOL_GC, (g + 1) * POOL_GC)
            dzg = dz[:, lanes].astype(BF16)
            gpoolw_ref[g] += lax.dot_general(dpool_ref[0, rs, lanes], dzg, TN_DIMS, preferred_element_type=F32)
            dd = lax.dot_general(dzg, poolw_ref[g].astype(BF16), NT_DIMS, preferred_element_type=F32)
            vext[rs, lanes] = dd * (1.0 / jnp.minimum(t + 1, w).astype(F32))
            s = vext[r0:r0 + n_ext, lanes]
            sh = 1
            while sh < w:
                s = s + pltpu.roll(s, n_ext - sh, 0)
                sh *= 2
            dproj_ref[0, rs, lanes] = (s[0:sub] - dd).astype(BF16)

        b_gate = proj_ref[0, rs, 4 * W_MIX:5 * W_MIX]
        sig_b = jax.nn.sigmoid(b_gate)
        silu_b = b_gate * sig_b
        ob = ob_ref[0, rs, :]
        y_s[rs, W_MIX:] = (ob * silu_b).astype(BF16)
        d_ob = dy_b * silu_b
        dproj_ref[0, rs, 4 * W_MIX:5 * W_MIX] = (dy_b * ob * (sig_b * (1.0 + b_gate * (1.0 - sig_b)))).astype(BF16)
        gvec_ref[V_PW_B] += _colsum8(d_ob)
        gwout_ref[...] += lax.dot_general(y_s[rs, :], dhb, TN_DIMS, preferred_element_type=F32)

        h1 = h1_ref[0, rs, :]
        mu = jnp.mean(h1, axis=-1, keepdims=True)
        xc = h1 - mu
        rstd = lax.rsqrt(jnp.mean(xc * xc, axis=-1, keepdims=True) + LN_EPS)
        xhat = xc * rstd
        h2 = xhat * lg_ref[...] + lb_ref[...]
        sig2 = jax.nn.sigmoid(h2)
        d_obb = d_ob.astype(BF16)
        gpww_ref[...] += lax.dot_general((h2 * sig2).astype(BF16), d_obb, TN_DIMS, preferred_element_type=F32)
        dh2 = lax.dot_general(d_obb, pww_ref[...], NT_DIMS, preferred_element_type=F32) * (sig2 * (1.0 + h2 * (1.0 - sig2)))
        gvec_ref[V_LN_G] += _colsum8(dh2 * xhat)
        gvec_ref[V_LN_B] += _colsum8(dh2)
        dxh = dh2 * lg_ref[...]
        dh1 = rstd * (dxh - jnp.mean(dxh, axis=-1, keepdims=True) - xhat * jnp.mean(dxh * xhat, axis=-1, keepdims=True))
        gvec_ref[V_CONV_B] += _colsum8(dh1)
        gext[rs, :] = dh1

    taps = _bwd_taps()
    cr = BWD_CONV_ROWS
    win = cr + CONV_HALO - SUBLANES

    def conv_phase(r0):
        for c0 in range(r0, r0 + sub, cr):
            for l0 in range(0, W_MIX, LANES):
                lanes = slice(l0, l0 + LANES)
                b_val = proj_ref[0, c0:c0 + cr, 2 * W_MIX + l0:2 * W_MIX + l0 + LANES]
                sg = jax.nn.sigmoid(proj_ref[0, c0:c0 + cr, 3 * W_MIX + l0:3 * W_MIX + l0 + LANES])
                h0c = b_val * sg
                acc = jnp.zeros((cr, LANES), F32)
                for s in range(SUBLANES):
                    gsh[...] = gext[c0 + s:c0 + s + win, lanes]
                    for q, k in taps[s]:
                        gq = gsh[SUBLANES * q:SUBLANES * q + cr, :]
                        acc = acc + gq * dw_ref[k:k + 1, lanes]
                        gdw_ref[k, :, lanes] += _colsum8(h0c * gq)
                d_bval = acc * sg
                dproj_ref[0, c0:c0 + cr, 2 * W_MIX + l0:2 * W_MIX + l0 + LANES] = d_bval.astype(BF16)
                dproj_ref[0, c0:c0 + cr, 3 * W_MIX + l0:3 * W_MIX + l0 + LANES] = (d_bval * b_val * (1.0 - sg)).astype(BF16)

    starts = [k * sub for k in reversed(range(BWD_SUBTILES))]
    for r0 in starts:
        dense_phase(r0)
    for r0 in starts:
        conv_phase(r0)
    vext[tm:, :] = vext[0:POOL_HALO, :]
    gext[tm:, :] = gext[0:CONV_HALO, :]


def _bwd1(dh, proj, h1, z, ob, dpool, w_out_b, pw_w_b, pool_w, pool_scale, conv_dw_f, ln_g, ln_b):
    bl, s, _ = dh.shape
    tm = TILE_M
    nt = s // tm
    tok = lambda n: pl.BlockSpec((1, tm, n), lambda b, i: (b, nt - 1 - i, 0))
    res = lambda shape: pl.BlockSpec(shape, lambda b, i: (0,) * len(shape))
    out_shapes = [(D_MODEL, D_MODEL), (W_MIX, W_MIX), (len(POOL_WINDOWS), POOL_GC, POOL_GC), (6, SUBLANES, W_MIX),
                  (CONV_WIDTH, SUBLANES, W_MIX)]
    return pl.pallas_call(
        functools.partial(_bwd1_body, nt),
        name="bwd1",
        grid=(bl, nt),
        in_specs=[tok(D_MODEL), tok(D_IN), tok(W_MIX), tok(W_MIX), tok(W_MIX), tok(W_MIX)] + [VMEM] * 7,
        out_specs=[tok(D_IN)] + [res(sh) for sh in out_shapes],
        out_shape=[jax.ShapeDtypeStruct((bl, s, D_IN), BF16)] + [jax.ShapeDtypeStruct(sh, F32) for sh in out_shapes],
        scratch_shapes=[pltpu.VMEM((tm + POOL_HALO, W_MIX), F32), pltpu.VMEM((tm + CONV_HALO, W_MIX), F32),
                        pltpu.VMEM((BWD_CONV_ROWS + CONV_HALO - SUBLANES, LANES), F32), pltpu.VMEM((tm, D_MODEL), BF16)],
        compiler_params=pltpu.CompilerParams(dimension_semantics=("arbitrary", "arbitrary"), vmem_limit_bytes=VMEM_LIMIT),
    )(dh, proj, h1, z, ob, dpool, w_out_b, pw_w_b, pool_w, _row(pool_scale), conv_dw_f, _row(ln_g), _row(ln_b))


MESH = pl.DeviceIdType.MESH
W_IN_BLK = D_IN // N_DEV
W_OUT_BLK = D_MODEL // N_DEV
PW_BLK = W_MIX // N_DEV
DW_BLK = W_MIX // N_DEV
N_PEER = N_DEV - 1


SLAB = 2 * W_IN_BLK
F_SIB, F_X, F_Y, F_ON, F_PASS_X, F_PASS_Y, F_PASS_D = range(7)
W_IN_ONLY, OTHERS = (0,), (1, 2, 3)


def _inproj_body(n_tiles, x_ref, ng_ref, wint_ref, wout_ref, pww_ref, dw_ref,
                 proj_hbm, wt_ref, woutf_ref, pwwf_ref, dwf_ref,
                 hn_s, wt_s, wout_s, pww_s, dw_blk, stage, out_sems, send_sems, recv_sems):
    s = pl.program_id(0)
    i = pl.program_id(1)
    tm = TILE_IN
    x, y, c = lax.axis_index("x"), lax.axis_index("y"), lax.axis_index("c")
    sibling = (x, y, 1 - c)
    xn, yn, dg = (1 - x, y), (x, 1 - y), (1 - x, 1 - y)
    pick = lambda a, b: tuple(jnp.where(c == 1, p, q) for p, q in zip(a, b))
    on_from, on_to = pick(xn, yn), pick(yn, xn)

    def blocks(px, py, pc):
        k = 4 * px + 2 * py + pc
        rows = lambda ref, n: ref.at[pl.ds(pl.multiple_of(k * n, n), n), :]
        return (rows(wt_s, W_IN_BLK), rows(wout_s, W_OUT_BLK), rows(pww_s, PW_BLK), dw_blk.at[k])

    def copies(arrays, flow, block, to):
        refs = blocks(*block)
        return [pltpu.make_async_remote_copy(src_ref=refs[a], dst_ref=refs[a], send_sem=send_sems.at[a, flow],
                                             recv_sem=recv_sems.at[a, flow], device_id=to, device_id_type=MESH)
                for a in arrays]

    def start(arrays):
        for flow, to in ((F_SIB, sibling), (F_X, (*xn, c)), (F_Y, (*yn, c))):
            for cp in copies(arrays, flow, (x, y, c), to):
                cp.start()

    def direct(arrays):
        for flow, chip in ((F_X, xn), (F_Y, yn)):
            for cp in copies(arrays, flow, (*chip, c), sibling):
                cp.wait_recv()
        for cp in copies(arrays, F_ON, (*on_from, c), (*on_to, c)):
            cp.start()
        for flow, chip in ((F_PASS_X, xn), (F_PASS_Y, yn)):
            for cp in copies(arrays, flow, (*chip, c), sibling):
                cp.start()

    def diagonal(arrays):
        for cp in copies(arrays, F_ON, (*dg, c), sibling):
            cp.wait_recv()
        for cp in copies(arrays, F_PASS_D, (*dg, c), sibling):
            cp.start()

    def from_sibling(arrays, flow, chip):
        for cp in copies(arrays, flow, (*chip, 1 - c), sibling):
            cp.wait_recv()

    def finish(arrays):
        mine = (x, y, c)
        for flow, block in ((F_SIB, mine), (F_X, mine), (F_Y, mine), (F_ON, (*on_from, c)), (F_PASS_X, (*xn, c)),
                            (F_PASS_Y, (*yn, c)), (F_PASS_D, (*dg, c))):
            for cp in copies(arrays, flow, block, sibling):
                cp.wait_send()

    step = s * n_tiles + i
    par = step % 2
    first = step == 0

    @pl.when(first)
    def _():
        mine = blocks(x, y, c)
        mine[0][...] = wint_ref[...].astype(BF16)
        mine[1][...] = wout_ref[...].astype(BF16)
        mine[2][...] = pww_ref[...].astype(BF16)
        mine[3][...] = dw_ref[...]
        start(W_IN_ONLY)
        start(OTHERS)

    rows_i = pl.ds(pl.multiple_of(i * tm, tm), tm)

    @pl.when(s == 0)
    def _():
        xt = x_ref[0]
        r1 = lax.rsqrt(jnp.mean(xt * xt, axis=-1, keepdims=True) + RMS_EPS)
        hn_s[rows_i, :] = (xt * r1 * ng_ref[...]).astype(BF16)

    @pl.when(first)
    def _():
        from_sibling(W_IN_ONLY, F_SIB, (x, y))

    @pl.when((s == 1) & (i == 0))
    def _():
        direct(W_IN_ONLY)
        from_sibling(W_IN_ONLY, F_PASS_Y, yn)

    @pl.when((s == 2) & (i == 0))
    def _():
        from_sibling(W_IN_ONLY, F_PASS_X, xn)
        diagonal(W_IN_ONLY)
        direct(OTHERS)
        from_sibling(W_IN_ONLY, F_PASS_D, dg)

    tiles_per_seq = proj_hbm.shape[1] // tm

    def out_copy(unit, col0):
        dst = proj_hbm.at[i // tiles_per_seq, pl.ds(pl.multiple_of((i % tiles_per_seq) * tm, tm), tm),
                          pl.ds(pl.multiple_of(col0, LANES), SLAB)]
        return pltpu.make_async_copy(stage.at[par, :, pl.ds(unit * SLAB, SLAB)], dst, out_sems.at[par, unit])

    @pl.when(step >= 2)
    def _():
        out_copy(0, 0).wait()

    @pl.when(step >= 2 * n_tiles + 2)
    def _():
        out_copy(1, 0).wait()

    hn = hn_s[rows_i, :]
    for phase, chip in ((0, 2 * x + y), (1, 2 * x + (1 - y))):
        @pl.when(s == phase)
        def _():
            slab = wt_s[pl.ds(pl.multiple_of(chip * SLAB, SLAB), SLAB), :]
            stage[par, :, 0:SLAB] = lax.dot_general(hn, slab, NT_DIMS, preferred_element_type=F32)
            out_copy(0, chip * SLAB).start()

    @pl.when(s == 2)
    def _():
        col0 = (1 - x) * 2 * SLAB
        slab = wt_s[pl.ds(pl.multiple_of(col0, 2 * SLAB), 2 * SLAB), :]
        stage[par] = lax.dot_general(hn, slab, NT_DIMS, preferred_element_type=F32)
        out_copy(0, col0).start()
        out_copy(1, col0 + SLAB).start()

    @pl.when(step == 3 * n_tiles - 1)
    def _():
        out_copy(0, 0).wait()
        out_copy(1, 0).wait()
        pltpu.make_async_copy(stage.at[1 - par, :, pl.ds(0, SLAB)], proj_hbm.at[0, pl.ds(0, tm), pl.ds(0, SLAB)],
                              out_sems.at[1 - par, 0]).wait()
        pltpu.make_async_copy(stage.at[1 - par, :, pl.ds(SLAB, SLAB)], proj_hbm.at[0, pl.ds(0, tm), pl.ds(0, SLAB)],
                              out_sems.at[1 - par, 1]).wait()
        from_sibling(OTHERS, F_SIB, (x, y))
        from_sibling(OTHERS, F_PASS_X, xn)
        from_sibling(OTHERS, F_PASS_Y, yn)
        diagonal(OTHERS)
        from_sibling(OTHERS, F_PASS_D, dg)
        finish(W_IN_ONLY)
        finish(OTHERS)
        wt_ref[...] = wt_s[...]
        woutf_ref[...] = wout_s[...]
        pwwf_ref[...] = pww_s[...]
        for k in range(N_DEV):
            dwf_ref[:, k * DW_BLK:(k + 1) * DW_BLK] = dw_blk[k]


def _inproj(x, norm_g, w_in_t, w_out, pw_w, conv_dw):
    bl, s, _ = x.shape
    tm = TILE_IN
    nt = s // tm
    n_tiles = bl * nt

    def x_map(phase, i):
        i = jnp.where(phase == 0, i, n_tiles - 1)
        return (i // nt, i % nt, 0)

    dma = pltpu.SemaphoreType.DMA
    return pl.pallas_call(
        functools.partial(_inproj_body, n_tiles),
        name="inproj",
        grid=(3, n_tiles),
        in_specs=[pl.BlockSpec((1, tm, D_MODEL), x_map)] + [VMEM] * 5,
        out_specs=[pl.BlockSpec(memory_space=pl.ANY)] + [VMEM] * 4,
        out_shape=[jax.ShapeDtypeStruct((bl, s, D_IN), F32), jax.ShapeDtypeStruct((D_IN, D_MODEL), BF16),
                   jax.ShapeDtypeStruct((D_MODEL, D_MODEL), BF16), jax.ShapeDtypeStruct((W_MIX, W_MIX), BF16),
                   jax.ShapeDtypeStruct((CONV_WIDTH, W_MIX), F32)],
        scratch_shapes=[pltpu.VMEM((n_tiles * tm, D_MODEL), BF16), pltpu.VMEM((D_IN, D_MODEL), BF16),
                        pltpu.VMEM((D_MODEL, D_MODEL), BF16), pltpu.VMEM((W_MIX, W_MIX), BF16),
                        pltpu.VMEM((N_DEV, CONV_WIDTH, DW_BLK), F32), pltpu.VMEM((2, tm, 2 * SLAB), F32),
                        dma((2, 2)), dma((4, 7)), dma((4, 7))],
        compiler_params=pltpu.CompilerParams(dimension_semantics=("arbitrary", "arbitrary"), vmem_limit_bytes=VMEM_LIMIT),
    )(x, _row(norm_g), w_in_t, w_out, pw_w, conv_dw)


B_PLAIN, B_ON, B_MERGED = range(3)


def _routes():
    x, y, c = lax.axis_index("x"), lax.axis_index("y"), lax.axis_index("c")
    xn, yn = (1 - x, y), (x, 1 - y)
    pick = lambda a, b: tuple(jnp.where(c == 1, p, q) for p, q in zip(a, b))
    return pick(xn, yn), pick(yn, xn), (1 - x, 1 - y), (x, y), c


def _chip_index(chip):
    return 2 * chip[0] + chip[1]


def _copy(src, dst, send_sems, recv_sems, a, k, to):
    return pltpu.make_async_remote_copy(src_ref=src, dst_ref=dst, send_sem=send_sems.at[a, k], recv_sem=recv_sems.at[a, k],
                                        device_id=to, device_id_type=MESH)


def _rsb_send(plain, on, rb_plain, rb_on, send_b, recv_b):
    first, _, _, _, c = _routes()
    for k, srcs, dsts in ((B_ON, on, rb_on), (B_PLAIN, plain, rb_plain)):
        for a, (s, d) in enumerate(zip(srcs, dsts)):
            _copy(s, d, send_b, recv_b, a, k, (*first, c)).start()


def _rsb_merge(second_sums, merged, rb_on, rb_merged, send_b, recv_b):
    _, second, _, _, c = _routes()
    for a, (own, m, got, dst) in enumerate(zip(second_sums, merged, rb_on, rb_merged)):
        _copy(got, got, send_b, recv_b, a, B_ON, (*second, c)).wait_recv()
        m[...] = (own[...] + got[...].astype(F32)).astype(BF16)
        _copy(m, dst, send_b, recv_b, a, B_MERGED, (*second, c)).start()


def _rsb_total(own_sums, plain, on, merged, rb_plain, rb_merged, send_b, recv_b):
    _, second, _, _, c = _routes()
    out = []
    for a, (own, p, g) in enumerate(zip(own_sums, rb_plain, rb_merged)):
        _copy(p, p, send_b, recv_b, a, B_PLAIN, (*second, c)).wait_recv()
        _copy(g, g, send_b, recv_b, a, B_MERGED, (*second, c)).wait_recv()
        out.append(own[...] + p[...].astype(F32) + g[...].astype(F32))
    for k, srcs in ((B_PLAIN, plain), (B_ON, on), (B_MERGED, merged)):
        for a, s in enumerate(srcs):
            _copy(s, s, send_b, recv_b, a, k, (*second, c)).wait_send()
    return out


R_LOSS, R_FINAL_G, R_POOL, R_CONV, R_LN = range(5)
VEC_ROWS = 16
DW_ROWS = 32
A_PART, A_X, A_Y, A_ON, A_HALF = range(5)
STEP_SUMS, STEP_MERGE, STEP_HALF = 1, 3, 5


def _bwd2a_body(n_steps, x_ref, dproj_ref, ng_ref, gwout_ref, gpww_ref, loss_ref, gfg_ref, gvec_ref, gdw_ref, gpoolw_ref,
                win_own_ref, win_staged_ref, owout_ref, opww_ref, ovec_ref, odw_ref, opoolw_ref,
                acc, st_win, a_win, a_wout, a_pww, st_wout, st_pww, rb_wout, rb_pww, part_vec, part_dw, part_pw,
                sib_vec, sib_dw, sib_pw, res_vec, res_dw, res_pw, half_vec, half_dw, half_pw,
                send_a, recv_a, send_b, recv_b, send_r, recv_r, send_w, recv_w):
    step = pl.program_id(0) * pl.num_programs(1) + pl.program_id(1)
    x, y, c = lax.axis_index("x"), lax.axis_index("y"), lax.axis_index("c")
    sibling = (x, y, 1 - c)
    xn, yn = (1 - x, y), (x, 1 - y)
    first, second, diag, own, _ = _routes()
    rows = lambda ref, k, n: ref.at[pl.ds(pl.multiple_of(k * n, n), n), :]

    blocks = lambda k: (rows(gwout_ref, k, W_OUT_BLK), rows(gpww_ref, k, PW_BLK))
    landed = lambda chip: (a_wout.at[chip], a_pww.at[chip])
    staged = lambda k: (st_wout.at[k], st_pww.at[k])
    arrived = lambda k: (rb_wout.at[k], rb_pww.at[k])

    parts, sibs, halves = (part_vec, part_dw, part_pw), (sib_vec, sib_dw, sib_pw), (half_vec, half_dw, half_pw)
    results = (res_vec, res_dw, res_pw)
    half_rows = lambda ref: pl.ds(pl.multiple_of(c * (ref.shape[0] // 2), SUBLANES), ref.shape[0] // 2)
    slot = lambda chip: tuple(h.at[_chip_index(chip)] for h in halves)

    @pl.when(step == 0)
    def _():
        acc[...] = jnp.zeros_like(acc)
        sum8 = lambda v: jnp.sum(v, axis=0, keepdims=True)
        part_vec[...] = jnp.zeros((VEC_ROWS, D_MODEL), F32)
        part_vec[R_LOSS:R_LOSS + 1, :] = jnp.broadcast_to(loss_ref[0:1, 0:1], (1, D_MODEL))
        part_vec[R_FINAL_G:R_FINAL_G + 1, :] = sum8(gfg_ref[...])
        for row, (lo, hi) in ((R_POOL, (V_POOL_B, V_POOL_SCALE)), (R_CONV, (V_CONV_B, V_LN_G)), (R_LN, (V_LN_B, V_PW_B))):
            part_vec[row:row + 1, 0:W_MIX] = sum8(gvec_ref[lo])
            part_vec[row:row + 1, W_MIX:] = sum8(gvec_ref[hi])
        part_dw[0:CONV_WIDTH, :] = jnp.sum(gdw_ref[...], axis=1)
        part_dw[CONV_WIDTH:, :] = jnp.zeros((DW_ROWS - CONV_WIDTH, W_MIX), F32)
        part_pw[...] = gpoolw_ref[...].reshape(len(POOL_WINDOWS) * POOL_GC, POOL_GC)
        for a, (p, s) in enumerate(zip(parts, sibs)):
            _copy(p, s, send_r, recv_r, a, A_PART, sibling).start()
        for chip in range(4):
            for a, (s, d) in enumerate(zip(blocks(2 * chip + (1 - c)), landed(chip))):
                _copy(s, d, send_a, recv_a, a, chip, sibling).start()

    x_t = x_ref[0]
    r1 = lax.rsqrt(jnp.mean(x_t * x_t, axis=-1, keepdims=True) + RMS_EPS)
    hn = (x_t * r1 * ng_ref[...]).astype(BF16)
    acc[...] += lax.dot_general(dproj_ref[0], hn, TN_DIMS, preferred_element_type=F32)

    @pl.when(step == STEP_SUMS)
    def _():
        for a, (p, s, h) in enumerate(zip(parts, sibs, slot(own))):
            _copy(p, s, send_r, recv_r, a, A_PART, sibling).wait_recv()
            h[...] = p[half_rows(p), :] + s[half_rows(s), :]
            for k, to in ((A_X, xn), (A_Y, yn)):
                _copy(h, h, send_r, recv_r, a, k, (*to, c)).start()
        for chip in range(4):
            for a, (m, l) in enumerate(zip(blocks(2 * chip + c), landed(chip))):
                _copy(l, l, send_a, recv_a, a, chip, sibling).wait_recv()
                l[...] = m[...] + l[...]
        for k, chip in ((B_ON, diag), (B_PLAIN, first)):
            for s, l in zip(staged(k), landed(_chip_index(chip))):
                s[...] = l[...].astype(BF16)
        _rsb_send(staged(B_PLAIN), staged(B_ON), arrived(B_PLAIN), arrived(B_ON), send_b, recv_b)

    @pl.when(step == STEP_MERGE)
    def _():
        _rsb_merge(landed(_chip_index(second)), staged(B_MERGED), arrived(B_ON), arrived(B_MERGED), send_b, recv_b)
        for a in range(3):
            for k, chip in ((A_X, xn), (A_Y, yn)):
                h = slot(chip)[a]
                _copy(h, h, send_r, recv_r, a, k, sibling).wait_recv()
            h = slot(first)[a]
            _copy(h, h, send_r, recv_r, a, A_ON, (*second, c)).start()

    @pl.when(step == STEP_HALF)
    def _():
        for a, (r, hs) in enumerate(zip(results, halves)):
            h = slot(diag)[a]
            _copy(h, h, send_r, recv_r, a, A_ON, sibling).wait_recv()
            mine = r.at[half_rows(r), :]
            mine[...] = (hs[0] + hs[1]) + (hs[2] + hs[3])
            _copy(mine, mine, send_r, recv_r, a, A_HALF, sibling).start()

    @pl.when(step == n_steps - 1)
    def _():
        block = lambda k: rows(acc, k, W_IN_BLK)
        w_copy = lambda chip: _copy(st_win.at[chip], a_win.at[chip], send_w, recv_w, 0, chip, sibling)
        for chip in range(4):
            st_win[chip] = block(2 * chip + (1 - c))[...].astype(BF16)
            w_copy(chip).start()

        owout_ref[...], opww_ref[...] = _rsb_total(landed(_chip_index(own)), staged(B_PLAIN), staged(B_ON), staged(B_MERGED),
                                                   arrived(B_PLAIN), arrived(B_MERGED), send_b, recv_b)
        for a, (r, p, s) in enumerate(zip(results, parts, sibs)):
            theirs = r.at[pl.ds(pl.multiple_of((1 - c) * (r.shape[0] // 2), SUBLANES), r.shape[0] // 2), :]
            _copy(theirs, theirs, send_r, recv_r, a, A_HALF, sibling).wait_recv()
            _copy(p, s, send_r, recv_r, a, A_PART, sibling).wait_send()
            for k, h in ((A_X, slot(own)[a]), (A_Y, slot(own)[a]), (A_ON, slot(first)[a])):
                _copy(h, h, send_r, recv_r, a, k, sibling).wait_send()
            mine = r.at[half_rows(r), :]
            _copy(mine, mine, send_r, recv_r, a, A_HALF, sibling).wait_send()
        for chip in range(4):
            for a, (s, d) in enumerate(zip(blocks(2 * chip + (1 - c)), landed(chip))):
                _copy(s, d, send_a, recv_a, a, chip, sibling).wait_send()
        ovec_ref[...] = res_vec[...]
        opoolw_ref[...] = res_pw[...]
        for k in range(N_DEV):
            @pl.when(4 * x + 2 * y + c == k)
            def _():
                odw_ref[...] = res_dw[0:CONV_WIDTH, k * DW_BLK:(k + 1) * DW_BLK]

        for chip in range(4):
            w_copy(chip).wait()
        chip_sum = lambda chip: block(2 * _chip_index(chip) + c)[...] + a_win[_chip_index(chip)].astype(F32)
        win_own_ref[...] = chip_sum(own)
        for j, chip in enumerate((xn, yn, diag)):
            win_staged_ref[j] = chip_sum(chip).astype(BF16)


def _bwd2a(x, dproj, norm_g, gwout, gpww, loss8, gfg8, gvec, gdw8, gpoolw):
    bl, s, _ = x.shape
    tm = TILE_M
    nt = s // tm
    tok = lambda n: pl.BlockSpec((1, tm, n), lambda b, i: (b, i, 0))
    n_pw = len(POOL_WINDOWS) * POOL_GC
    win = (W_IN_BLK, D_MODEL)
    blk = ((W_OUT_BLK, D_MODEL), (PW_BLK, W_MIX))
    rep = ((VEC_ROWS, D_MODEL), (DW_ROWS, W_MIX), (n_pw, POOL_GC))
    half = lambda sh: (sh[0] // 2,) + sh[1:]
    dma = pltpu.SemaphoreType.DMA
    assert bl * nt > STEP_HALF + 1, "the exchanges' phases need their grid steps"
    return pl.pallas_call(
        functools.partial(_bwd2a_body, bl * nt),
        name="bwd2a",
        grid=(bl, nt),
        in_specs=[tok(D_MODEL), tok(D_IN)] + [VMEM] * 8,
        out_specs=[VMEM] * 7,
        out_shape=[jax.ShapeDtypeStruct(win, F32), jax.ShapeDtypeStruct((3,) + win, BF16)]
        + [jax.ShapeDtypeStruct(sh, F32) for sh in blk + (rep[0], (CONV_WIDTH, DW_BLK), rep[2])],
        scratch_shapes=[pltpu.VMEM((D_IN, D_MODEL), F32), pltpu.VMEM((4,) + win, BF16), pltpu.VMEM((4,) + win, BF16)]
        + [pltpu.VMEM((4,) + sh, F32) for sh in blk]
        + [pltpu.VMEM((3,) + sh, BF16) for sh in blk] * 2
        + [pltpu.VMEM(sh, F32) for sh in rep] * 3
        + [pltpu.VMEM((4,) + half(sh), F32) for sh in rep]
        + [dma((2, 4)), dma((2, 4)), dma((2, 3)), dma((2, 3)), dma((3, 5)), dma((3, 5)), dma((1, 4)), dma((1, 4))],
        compiler_params=pltpu.CompilerParams(dimension_semantics=("arbitrary", "arbitrary"), vmem_limit_bytes=VMEM_LIMIT),
    )(x, dproj, _row(norm_g), gwout, gpww, loss8, gfg8, gvec, gdw8, gpoolw)


HBM = pl.BlockSpec(memory_space=pltpu.HBM)
SEM = pl.BlockSpec(memory_space=pltpu.SEMAPHORE)
EFFECT = pltpu.SideEffectType.DATAFLOW_SIDE_EFFECTING


def _owner_copies(src_ref, land_ref, send_sems, recv_sems):
    x, y, c = lax.axis_index("x"), lax.axis_index("y"), lax.axis_index("c")
    owners = ((1 - x, y, c), (x, 1 - y, c), (1 - x, 1 - y, c))
    return [pltpu.make_async_remote_copy(src_ref=src_ref.at[j], dst_ref=land_ref.at[j], send_sem=send_sems.at[j],
                                         recv_sem=recv_sems.at[j], device_id=owners[j], device_id_type=MESH)
            for j in range(3)]


def _rs_start(staged):
    def body(src_ref, land_ref, send_sems, recv_sems, src_thru, land_thru, token):
        for cp in _owner_copies(src_ref, land_ref, send_sems, recv_sems):
            cp.start()
        token[...] = jnp.zeros_like(token)

    dma = pltpu.SemaphoreType.DMA
    return pl.pallas_call(
        body, name="rs_start",
        out_shape=(dma((3,)), dma((3,)), pltpu.HBM(staged.shape, staged.dtype), pltpu.HBM(staged.shape, staged.dtype),
                   jax.ShapeDtypeStruct((SUBLANES, LANES), F32)),
        in_specs=(HBM, HBM), out_specs=(SEM, SEM, HBM, HBM, VMEM), input_output_aliases={0: 2, 1: 3},
        compiler_params=pltpu.CompilerParams(has_side_effects=EFFECT),
    )(pltpu.with_memory_space_constraint(staged, pltpu.HBM),
      pltpu.with_memory_space_constraint(lax.empty(staged.shape, staged.dtype), pltpu.HBM))


def _rs_wait(send_sems, recv_sems, src_thru, land_thru, after):
    def body(src_ref, land_ref, send_sems, recv_sems, after_ref, src_dead, got_ref):
        for cp in _owner_copies(src_ref, land_ref, send_sems, recv_sems):
            cp.wait_send()
            cp.wait_recv()

    return pl.pallas_call(
        body, name="rs_wait",
        out_shape=(pltpu.HBM(src_thru.shape, src_thru.dtype), pltpu.HBM(land_thru.shape, land_thru.dtype)),
        in_specs=(HBM, HBM, SEM, SEM, pl.BlockSpec(memory_space=pl.ANY)), out_specs=(HBM, HBM),
        input_output_aliases={0: 0, 1: 1},
        compiler_params=pltpu.CompilerParams(has_side_effects=EFFECT),
    )(src_thru, land_thru, send_sems, recv_sems, after)[1]


def _bwd2b_body(x_ref, dh_ref, dproj_ref, ng_ref, wint_ref, token_ref, gx_ref, gng_ref):
    @pl.when((pl.program_id(0) == 0) & (pl.program_id(1) == 0))
    def _():
        gng_ref[...] = jnp.zeros_like(gng_ref) + token_ref[0:1, 0:1]

    x_t = x_ref[0]
    r1 = lax.rsqrt(jnp.mean(x_t * x_t, axis=-1, keepdims=True) + RMS_EPS)
    xr = x_t * r1
    dhn = jnp.dot(dproj_ref[0], wint_ref[...], preferred_element_type=F32)
    gng_ref[...] += _colsum8(dhn * xr)
    gy = dhn * ng_ref[...]
    gx_ref[0] = dh_ref[0] + r1 * (gy - xr * jnp.mean(gy * xr, axis=-1, keepdims=True))


def _bwd2b(x, dh, dproj, norm_g, w_in_t, token):
    bl, s, _ = x.shape
    tm = TILE_M
    tok = lambda n: pl.BlockSpec((1, tm, n), lambda b, i: (b, i, 0))
    return pl.pallas_call(
        functools.partial(_bwd2b_body),
        name="bwd2b",
        grid=(bl, s // tm),
        in_specs=[tok(D_MODEL), tok(D_MODEL), tok(D_IN), VMEM, VMEM, VMEM],
        out_specs=[tok(D_MODEL), pl.BlockSpec((SUBLANES, D_MODEL), lambda b, i: (0, 0))],
        out_shape=[jax.ShapeDtypeStruct((bl, s, D_MODEL), F32), jax.ShapeDtypeStruct((SUBLANES, D_MODEL), F32)],
        compiler_params=pltpu.CompilerParams(dimension_semantics=("arbitrary", "arbitrary"), vmem_limit_bytes=VMEM_LIMIT),
    )(x, dh, dproj, _row(norm_g), w_in_t, token)


def _norm_g_body(part_ref, out_ref, all_ng, send_n, recv_n):
    x, y, c = lax.axis_index("x"), lax.axis_index("y"), lax.axis_index("c")
    other_chips = [(1 - x, y), (x, 1 - y), (1 - x, 1 - y)]
    k_me = 4 * x + 2 * y + c
    peers = [(x, y, 1 - c)] + [(*chip, c) for chip in other_chips] + [(*chip, 1 - c) for chip in other_chips]
    copies = [pltpu.make_async_remote_copy(src_ref=all_ng.at[k_me], dst_ref=all_ng.at[k_me], send_sem=send_n.at[r],
                                           recv_sem=recv_n.at[r], device_id=peers[r], device_id_type=MESH)
              for r in range(N_PEER)]
    all_ng[k_me] = part_ref[...]
    for cp in copies:
        cp.start()
    for cp in copies:
        cp.wait()
    s = all_ng[0]
    for k in range(1, N_DEV):
        s = s + all_ng[k]
    out_ref[...] = jnp.broadcast_to(jnp.sum(s, axis=0, keepdims=True), out_ref.shape)


def _norm_g_sum(part):
    dma = pltpu.SemaphoreType.DMA
    return pl.pallas_call(
        functools.partial(_norm_g_body),
        name="norm_g_sum",
        in_specs=[VMEM],
        out_specs=VMEM,
        out_shape=jax.ShapeDtypeStruct(part.shape, F32),
        scratch_shapes=[pltpu.VMEM((N_DEV,) + part.shape, F32), dma((N_PEER,)), dma((N_PEER,))],
    )(part)


ADAM_LR = 0.001
ADAM_B1 = 0.9
ADAM_B2 = 0.999
ADAM_EPS = 1e-08
ADAM_WD = 0.01
ADAM_STEP = 10


PARAMS = ("norm_g", "w_in", "pool_w", "pool_b", "pool_scale", "conv_dw", "conv_b", "ln_g", "ln_b", "pw_w", "pw_b", "w_out",
          "final_g")


def _adamw_body(*refs):
    n = len(PARAMS)
    ws, ms, vs = (dict(zip(PARAMS, refs[k * n:(k + 1) * n])) for k in range(3))
    gwin_ref, gwin_others_ref, gwout_ref, gpww_ref, vec_ref, gdw_ref, gpoolw_ref, gng_ref = refs[3 * n:3 * n + 8]
    outs = refs[3 * n + 8:]
    gs, ds, m2s, v2s = (dict(zip(PARAMS, outs[k * n:(k + 1) * n])) for k in range(4))
    loss_ref = outs[4 * n]

    def update(name, g, at=slice(None)):
        m = ADAM_B1 * ms[name][at] + (1.0 - ADAM_B1) * g
        v = ADAM_B2 * vs[name][at] + (1.0 - ADAM_B2) * (g * g)
        m_hat = m / (1.0 - ADAM_B1 ** ADAM_STEP)
        v_hat = v / (1.0 - ADAM_B2 ** ADAM_STEP)
        gs[name][at] = g
        ds[name][at] = -ADAM_LR * (m_hat / (jnp.sqrt(v_hat) + ADAM_EPS) + ADAM_WD * ws[name][at])
        m2s[name][at] = m
        v2s[name][at] = v

    row = lambda r, lo, hi: vec_ref[r:r + 1, lo:hi]
    loss_ref[...] = row(R_LOSS, 0, 1)
    update("norm_g", gng_ref[0:1, :])
    g_w_in = gwin_ref[...]
    for j in range(3):
        g_w_in = g_w_in + gwin_others_ref[j].astype(F32)
    update("w_in", g_w_in)
    update("pool_w", gpoolw_ref[...].reshape(ws["pool_w"].shape))
    for g in range(len(POOL_WINDOWS)):
        update("pool_b", row(R_POOL, g * POOL_GC, (g + 1) * POOL_GC), at=slice(g, g + 1))
    update("pool_scale", row(R_POOL, W_MIX, 2 * W_MIX))
    update("conv_dw", gdw_ref[...])
    update("conv_b", row(R_CONV, 0, W_MIX))
    update("ln_g", row(R_CONV, W_MIX, 2 * W_MIX))
    update("ln_b", row(R_LN, 0, W_MIX))
    update("pw_w", gpww_ref[...])
    update("pw_b", row(R_LN, W_MIX, 2 * W_MIX))
    update("w_out", gwout_ref[...])
    update("final_g", row(R_FINAL_G, 0, D_MODEL))


def _adamw(ws, ms, vs, grad_sources):
    n = len(PARAMS)
    shapes = [ws[p].shape for p in PARAMS]
    flat = lambda a: a.reshape(1, -1) if a.ndim == 1 else a
    args = [flat(d[p]) for d in (ws, ms, vs) for p in PARAMS]
    outs = pl.pallas_call(
        functools.partial(_adamw_body),
        name="adamw",
        in_specs=[VMEM] * (3 * n + len(grad_sources)),
        out_specs=[VMEM] * (4 * n + 1),
        out_shape=[jax.ShapeDtypeStruct(a.shape, F32) for a in args[:n]] * 4 + [jax.ShapeDtypeStruct((1, 1), F32)],
        compiler_params=pltpu.CompilerParams(vmem_limit_bytes=VMEM_LIMIT),
    )(*args, *grad_sources)
    loss = outs[4 * n].reshape(())
    outs = [o.reshape(shapes[k % n]) for k, o in enumerate(outs[:4 * n])]
    return loss, outs[:n], outs[n:2 * n], outs[2 * n:3 * n], outs[3 * n:]


def kernel(x, norm_g, w_in, pool_w, pool_b, pool_scale, conv_dw, conv_b, ln_g, ln_b, pw_w, pw_b, w_out, final_g, loss_target, m_norm_g, m_w_in, m_pool_w, m_pool_b, m_pool_scale, m_conv_dw, m_conv_b, m_ln_g, m_ln_b, m_pw_w, m_pw_b, m_w_out, m_final_g, v_norm_g, v_w_in, v_pool_w, v_pool_b, v_pool_scale, v_conv_dw, v_conv_b, v_ln_g, v_ln_b, v_pw_w, v_pw_b, v_w_out, v_final_g):
    proj, w_in_t, w_out_b, pw_w_b, conv_dw_f = _inproj(x, norm_g, w_in.T, w_out, pw_w, conv_dw)
    h1, z, ob, dpool, dh, loss8, gfg8 = _fwd(x, loss_target, proj, pool_w, pool_b, pool_scale, conv_dw_f,
                                             conv_b, ln_g, ln_b, pw_w_b, pw_b, w_out_b, final_g)
    dproj, gwout, gpww, gpoolw, gvec, gdw8 = _bwd1(dh, proj, h1, z, ob, dpool, w_out_b, pw_w_b, pool_w, pool_scale,
                                                   conv_dw_f, ln_g, ln_b)
    win_own, win_staged, g_w_out, g_pw_w, vec, g_conv_dw, poolw_sum = _bwd2a(x, dproj, norm_g, gwout, gpww, loss8, gfg8, gvec,
                                                                             gdw8, gpoolw)
    send_sems, recv_sems, staged_thru, land_thru, token = _rs_start(win_staged)
    grad_x, gng8 = _bwd2b(x, dh, dproj, norm_g, w_in_t, token)
    g_norm_g = _norm_g_sum(gng8)
    win_others = _rs_wait(send_sems, recv_sems, staged_thru, land_thru, gng8)

    weights = dict(norm_g=norm_g, w_in=w_in.T, pool_w=pool_w, pool_b=pool_b, pool_scale=pool_scale, conv_dw=conv_dw, conv_b=conv_b,
                   ln_g=ln_g, ln_b=ln_b, pw_w=pw_w, pw_b=pw_b, w_out=w_out, final_g=final_g)
    m_in = dict(norm_g=m_norm_g, w_in=m_w_in.T, pool_w=m_pool_w, pool_b=m_pool_b, pool_scale=m_pool_scale, conv_dw=m_conv_dw,
                conv_b=m_conv_b, ln_g=m_ln_g, ln_b=m_ln_b, pw_w=m_pw_w, pw_b=m_pw_b, w_out=m_w_out, final_g=m_final_g)
    v_in = dict(norm_g=v_norm_g, w_in=v_w_in.T, pool_w=v_pool_w, pool_b=v_pool_b, pool_scale=v_pool_scale, conv_dw=v_conv_dw,
                conv_b=v_conv_b, ln_g=v_ln_g, ln_b=v_ln_b, pw_w=v_pw_w, pw_b=v_pw_b, w_out=v_w_out, final_g=v_final_g)
    loss, *groups = _adamw(weights, m_in, v_in, (win_own, win_others, g_w_out, g_pw_w, vec, g_conv_dw, poolw_sum, g_norm_g))
    k_in = PARAMS.index("w_in")
    for group in groups:
        group[k_in] = group[k_in].T
    grads, delta, new_m, new_v = groups
    return (loss, grad_x, *grads, *delta, *new_m, *new_v)
```

```python
import functools

import jax
import jax.numpy as jnp
from jax import lax
from jax.experimental import pallas as pl
from jax.experimental.pallas import tpu as pltpu

F32 = jnp.float32
BF16 = jnp.bfloat16

D_MODEL = 1024
W_MIX = 512
D_IN = 5 * W_MIX
POOL_WINDOWS = (2, 4, 8, 16)
POOL_GC = 128
CONV_WIDTH = 31
RMS_EPS = 1e-6
LN_EPS = 1e-5
N_DEV = 8

LANES = 128
SUBLANES = 8
TILE_M = 512
TILE_IN = 1024
POOL_HALO = 16
CONV_HALO = 32
CONV_ROWS = 128
BWD_CONV_ROWS = 64
BWD_SUBTILES = 2
VMEM_LIMIT = 56 * 1024 * 1024

VMEM = pl.BlockSpec(memory_space=pltpu.VMEM)


def _silu(v):
    return v * jax.nn.sigmoid(v)


def _colsum8(v):
    m, n = v.shape
    return jnp.sum(v.reshape(m // SUBLANES, SUBLANES, n), axis=0)


def _row_index(shape, t0):
    return lax.broadcasted_iota(jnp.int32, shape, 0) + t0


def _conv_taps():
    taps = {s: [] for s in range(SUBLANES)}
    for k in range(CONV_WIDTH):
        e = k + CONV_HALO - (CONV_WIDTH - 1)
        taps[e % SUBLANES].append((e // SUBLANES, k))
    return taps


def _fwd_body(x_ref, tgt_ref, proj_ref, poolw_ref, poolb_ref, pools_ref, dw_ref, cb_ref, lg_ref, lb_ref,
              pww_ref, pwb_ref, wout_ref, fg_ref,
              h1_ref, z_ref, ob_ref, dpool_ref, dh_ref, loss_ref, gfg_ref,
              uext, hext, y_s):
    b = pl.program_id(0)
    i = pl.program_id(1)
    tm = TILE_M

    @pl.when(i == 0)
    def _():
        uext[0:POOL_HALO, :] = jnp.zeros((POOL_HALO, W_MIX), F32)
        hext[0:CONV_HALO, :] = jnp.zeros((CONV_HALO, W_MIX), F32)
        hext[CONV_HALO + tm:, :] = jnp.zeros((SUBLANES, W_MIX), F32)

    @pl.when((b == 0) & (i == 0))
    def _():
        loss_ref[...] = jnp.zeros_like(loss_ref)
        gfg_ref[...] = jnp.zeros_like(gfg_ref)

    uext[POOL_HALO:, :] = proj_ref[0, :, 0:W_MIX]
    t = _row_index((tm, POOL_GC), i * tm)
    for g, w in enumerate(POOL_WINDOWS):
        lanes = slice(g * POOL_GC, (g + 1) * POOL_GC)
        e = uext[:, lanes]
        s = e
        sh = 1
        while sh < w:
            s = s + pltpu.roll(s, sh, 0)
            sh *= 2
        inv_cnt = 1.0 / jnp.minimum(t + 1, w).astype(F32)
        d = (s[POOL_HALO:] * inv_cnt - e[POOL_HALO:]).astype(BF16)
        dpool_ref[0, :, lanes] = d
        z = jnp.dot(d, poolw_ref[g].astype(BF16), preferred_element_type=F32) + poolb_ref[:, lanes]
        z_ref[0, :, lanes] = z
        a_gate = proj_ref[0, :, W_MIX + g * POOL_GC:W_MIX + (g + 1) * POOL_GC]
        y_s[:, lanes] = (z * pools_ref[:, lanes] * _silu(a_gate)).astype(BF16)

    hext[CONV_HALO:CONV_HALO + tm, :] = proj_ref[0, :, 2 * W_MIX:3 * W_MIX] * jax.nn.sigmoid(proj_ref[0, :, 3 * W_MIX:4 * W_MIX])
    taps = _conv_taps()
    for c0 in range(0, tm, CONV_ROWS):
        for l0 in range(0, W_MIX, LANES):
            lanes = slice(l0, l0 + LANES)
            acc = jnp.zeros((CONV_ROWS, LANES), F32) + cb_ref[:, lanes]
            for s in range(SUBLANES):
                part = jnp.zeros((CONV_ROWS + SUBLANES, LANES), F32)
                for q, k in taps[s]:
                    r0 = c0 + SUBLANES * q
                    part = part + hext[r0:r0 + CONV_ROWS + SUBLANES, lanes] * dw_ref[k:k + 1, lanes]
                acc = acc + part[s:s + CONV_ROWS]
            h1_ref[0, c0:c0 + CONV_ROWS, lanes] = acc
    h1 = h1_ref[0]
    mu = jnp.mean(h1, axis=-1, keepdims=True)
    xc = h1 - mu
    rstd = lax.rsqrt(jnp.mean(xc * xc, axis=-1, keepdims=True) + LN_EPS)
    h3 = _silu(xc * rstd * lg_ref[...] + lb_ref[...])
    ob = jnp.dot(h3.astype(BF16), pww_ref[...], preferred_element_type=F32) + pwb_ref[...]
    ob_ref[0] = ob
    y_s[:, W_MIX:] = (ob * _silu(proj_ref[0, :, 4 * W_MIX:5 * W_MIX])).astype(BF16)

    h = x_ref[0] + jnp.dot(y_s[...], wout_ref[...], preferred_element_type=F32)
    r2 = lax.rsqrt(jnp.mean(h * h, axis=-1, keepdims=True) + RMS_EPS)
    hr = h * r2
    err = hr * fg_ref[...] - tgt_ref[0]
    loss_ref[...] += jnp.sum(err * err) * (0.5 / D_MODEL)
    dout = err * (1.0 / D_MODEL)
    gfg_ref[...] += _colsum8(dout * hr)
    gy = dout * fg_ref[...]
    dh_ref[0] = r2 * (gy - hr * jnp.mean(gy * hr, axis=-1, keepdims=True))

    uext[0:POOL_HALO, :] = uext[tm:tm + POOL_HALO, :]
    hext[0:CONV_HALO, :] = hext[tm:tm + CONV_HALO, :]


def _row(v):
    return v.reshape(1, -1)


def _fwd(x, tgt, proj, pool_w, pool_b, pool_scale, conv_dw_f, conv_b, ln_g, ln_b, pw_w_b, pw_b, w_out_b, final_g):
    bl, s, _ = x.shape
    tm = TILE_M
    tok = lambda n: pl.BlockSpec((1, tm, n), lambda b, i: (b, i, 0))
    acc = lambda n: pl.BlockSpec((SUBLANES, n), lambda b, i: (0, 0))
    act = lambda n: jax.ShapeDtypeStruct((bl, s, n), F32)
    return pl.pallas_call(
        functools.partial(_fwd_body),
        name="fwd",
        grid=(bl, s // tm),
        in_specs=[tok(D_MODEL), tok(D_MODEL), tok(D_IN)] + [VMEM] * 11,
        out_specs=[tok(W_MIX), tok(W_MIX), tok(W_MIX), tok(W_MIX), tok(D_MODEL), acc(LANES), acc(D_MODEL)],
        out_shape=[act(W_MIX), act(W_MIX), act(W_MIX), jax.ShapeDtypeStruct((bl, s, W_MIX), BF16), act(D_MODEL),
                   jax.ShapeDtypeStruct((SUBLANES, LANES), F32), jax.ShapeDtypeStruct((SUBLANES, D_MODEL), F32)],
        scratch_shapes=[pltpu.VMEM((POOL_HALO + tm, W_MIX), F32), pltpu.VMEM((CONV_HALO + tm + SUBLANES, W_MIX), F32),
                        pltpu.VMEM((tm, D_MODEL), BF16)],
        compiler_params=pltpu.CompilerParams(dimension_semantics=("arbitrary", "arbitrary"), vmem_limit_bytes=VMEM_LIMIT),
    )(x, tgt, proj, pool_w, pool_b.reshape(1, W_MIX), _row(pool_scale), conv_dw_f, _row(conv_b),
      _row(ln_g), _row(ln_b), pw_w_b, _row(pw_b), w_out_b, _row(final_g))


NT_DIMS = (((1,), (1,)), ((), ()))
TN_DIMS = (((0,), (0,)), ((), ()))
V_POOL_B, V_POOL_SCALE, V_CONV_B, V_LN_G, V_LN_B, V_PW_B = range(6)


def _bwd_taps():
    taps = {s: [] for s in range(SUBLANES)}
    for k in range(CONV_WIDTH):
        e = CONV_WIDTH - 1 - k
        taps[e % SUBLANES].append((e // SUBLANES, k))
    return taps


def _bwd1_body(n_tiles, dh_ref, proj_ref, h1_ref, z_ref, ob_ref, dpool_ref, wout_ref, pww_ref, poolw_ref, pools_ref, dw_ref,
               lg_ref, lb_ref,
               dproj_ref, gwout_ref, gpww_ref, gpoolw_ref, gvec_ref, gdw_ref,
               vext, gext, gsh, y_s):
    b = pl.program_id(0)
    i = pl.program_id(1)
    tm = TILE_M
    t0 = (n_tiles - 1 - i) * tm

    @pl.when(i == 0)
    def _():
        vext[tm:, :] = jnp.zeros((POOL_HALO, W_MIX), F32)
        gext[tm:, :] = jnp.zeros((CONV_HALO, W_MIX), F32)

    @pl.when((b == 0) & (i == 0))
    def _():
        gwout_ref[...] = jnp.zeros_like(gwout_ref)
        gpww_ref[...] = jnp.zeros_like(gpww_ref)
        gpoolw_ref[...] = jnp.zeros_like(gpoolw_ref)
        gvec_ref[...] = jnp.zeros_like(gvec_ref)
        gdw_ref[...] = jnp.zeros_like(gdw_ref)

    sub = tm // BWD_SUBTILES

    def dense_phase(r0):
        rs = slice(r0, r0 + sub)
        dhb = dh_ref[0, rs, :].astype(BF16)
        dy_a = lax.dot_general(dhb, wout_ref[0:W_MIX, :], NT_DIMS, preferred_element_type=F32)
        dy_b = lax.dot_general(dhb, wout_ref[W_MIX:, :], NT_DIMS, preferred_element_type=F32)

        a_gate = proj_ref[0, rs, W_MIX:2 * W_MIX]
        sig_a = jax.nn.sigmoid(a_gate)
        silu_a = a_gate * sig_a
        z = z_ref[0, rs, :]
        out_a = z * pools_ref[...]
        y_s[rs, 0:W_MIX] = (out_a * silu_a).astype(BF16)
        d_out_a = dy_a * silu_a
        d_ag = dy_a * out_a * (sig_a * (1.0 + a_gate * (1.0 - sig_a)))
        dproj_ref[0, rs, W_MIX:2 * W_MIX] = d_ag.astype(BF16)
        gvec_ref[V_POOL_SCALE] += _colsum8(d_out_a * z)
        dz = d_out_a * pools_ref[...]
        gvec_ref[V_POOL_B] += _colsum8(dz)
        t = _row_index((sub, POOL_GC), t0 + r0)
        n_ext = sub + POOL_HALO
        for g, w in enumerate(POOL_WINDOWS):
            lanes = slice(g * POOL_GC, (g + 1) * POOL_GC)
            dzg = dz[:, lanes].astype(BF16)
            gpoolw_ref[g] += lax.dot_general(dpool_ref[0, rs, lanes], dzg, TN_DIMS, preferred_element_type=F32)
            dd = lax.dot_general(dzg, poolw_ref[g].astype(BF16), NT_DIMS, preferred_element_type=F32)
            vext[rs, lanes] = dd * (1.0 / jnp.minimum(t + 1, w).astype(F32))
            s = vext[r0:r0 + n_ext, lanes]
            sh = 1
            while sh < w:
                s = s + pltpu.roll(s, n_ext - sh, 0)
                sh *= 2
            dproj_ref[0, rs, lanes] = (s[0:sub] - dd).astype(BF16)

        b_gate = proj_ref[0, rs, 4 * W_MIX:5 * W_MIX]
        sig_b = jax.nn.sigmoid(b_gate)
        silu_b = b_gate * sig_b
        ob = ob_ref[0, rs, :]
        y_s[rs, W_MIX:] = (ob * silu_b).astype(BF16)
        d_ob = dy_b * silu_b
        dproj_ref[0, rs, 4 * W_MIX:5 * W_MIX] = (dy_b * ob * (sig_b * (1.0 + b_gate * (1.0 - sig_b)))).astype(BF16)
        gvec_ref[V_PW_B] += _colsum8(d_ob)
        gwout_ref[...] += lax.dot_general(y_s[rs, :], dhb, TN_DIMS, preferred_element_type=F32)

        h1 = h1_ref[0, rs, :]
        mu = jnp.mean(h1, axis=-1, keepdims=True)
        xc = h1 - mu
        rstd = lax.rsqrt(jnp.mean(xc * xc, axis=-1, keepdims=True) + LN_EPS)
        xhat = xc * rstd
        h2 = xhat * lg_ref[...] + lb_ref[...]
        sig2 = jax.nn.sigmoid(h2)
        d_obb = d_ob.astype(BF16)
        gpww_ref[...] += lax.dot_general((h2 * sig2).astype(BF16), d_obb, TN_DIMS, preferred_element_type=F32)
        dh2 = lax.dot_general(d_obb, pww_ref[...], NT_DIMS, preferred_element_type=F32) * (sig2 * (1.0 + h2 * (1.0 - sig2)))
        gvec_ref[V_LN_G] += _colsum8(dh2 * xhat)
        gvec_ref[V_LN_B] += _colsum8(dh2)
        dxh = dh2 * lg_ref[...]
        dh1 = rstd * (dxh - jnp.mean(dxh, axis=-1, keepdims=True) - xhat * jnp.mean(dxh * xhat, axis=-1, keepdims=True))
        gvec_ref[V_CONV_B] += _colsum8(dh1)
        gext[rs, :] = dh1

    taps = _bwd_taps()
    cr = BWD_CONV_ROWS
    win = cr + CONV_HALO - SUBLANES

    def conv_phase(r0):
        for c0 in range(r0, r0 + sub, cr):
            for l0 in range(0, W_MIX, LANES):
                lanes = slice(l0, l0 + LANES)
                b_val = proj_ref[0, c0:c0 + cr, 2 * W_MIX + l0:2 * W_MIX + l0 + LANES]
                sg = jax.nn.sigmoid(proj_ref[0, c0:c0 + cr, 3 * W_MIX + l0:3 * W_MIX + l0 + LANES])
                h0c = b_val * sg
                acc = jnp.zeros((cr, LANES), F32)
                for s in range(SUBLANES):
                    gsh[...] = gext[c0 + s:c0 + s + win, lanes]
                    for q, k in taps[s]:
                        gq = gsh[SUBLANES * q:SUBLANES * q + cr, :]
                        acc = acc + gq * dw_ref[k:k + 1, lanes]
                        gdw_ref[k, :, lanes] += _colsum8(h0c * gq)
                d_bval = acc * sg
                dproj_ref[0, c0:c0 + cr, 2 * W_MIX + l0:2 * W_MIX + l0 + LANES] = d_bval.astype(BF16)
                dproj_ref[0, c0:c0 + cr, 3 * W_MIX + l0:3 * W_MIX + l0 + LANES] = (d_bval * b_val * (1.0 - sg)).astype(BF16)

    starts = [k * sub for k in reversed(range(BWD_SUBTILES))]
    for r0 in starts:
        dense_phase(r0)
    for r0 in starts:
        conv_phase(r0)
    vext[tm:, :] = vext[0:POOL_HALO, :]
    gext[tm:, :] = gext[0:CONV_HALO, :]


def _bwd1(dh, proj, h1, z, ob, dpool, w_out_b, pw_w_b, pool_w, pool_scale, conv_dw_f, ln_g, ln_b):
    bl, s, _ = dh.shape
    tm = TILE_M
    nt = s // tm
    tok = lambda n: pl.BlockSpec((1, tm, n), lambda b, i: (b, nt - 1 - i, 0))
    res = lambda shape: pl.BlockSpec(shape, lambda b, i: (0,) * len(shape))
    out_shapes = [(D_MODEL, D_MODEL), (W_MIX, W_MIX), (len(POOL_WINDOWS), POOL_GC, POOL_GC), (6, SUBLANES, W_MIX),
                  (CONV_WIDTH, SUBLANES, W_MIX)]
    return pl.pallas_call(
        functools.partial(_bwd1_body, nt),
        name="bwd1",
        grid=(bl, nt),
        in_specs=[tok(D_MODEL), tok(D_IN), tok(W_MIX), tok(W_MIX), tok(W_MIX), tok(W_MIX)] + [VMEM] * 7,
        out_specs=[tok(D_IN)] + [res(sh) for sh in out_shapes],
        out_shape=[jax.ShapeDtypeStruct((bl, s, D_IN), BF16)] + [jax.ShapeDtypeStruct(sh, F32) for sh in out_shapes],
        scratch_shapes=[pltpu.VMEM((tm + POOL_HALO, W_MIX), F32), pltpu.VMEM((tm + CONV_HALO, W_MIX), F32),
                        pltpu.VMEM((BWD_CONV_ROWS + CONV_HALO - SUBLANES, LANES), F32), pltpu.VMEM((tm, D_MODEL), BF16)],
        compiler_params=pltpu.CompilerParams(dimension_semantics=("arbitrary", "arbitrary"), vmem_limit_bytes=VMEM_LIMIT),
    )(dh, proj, h1, z, ob, dpool, w_out_b, pw_w_b, pool_w, _row(pool_scale), conv_dw_f, _row(ln_g), _row(ln_b))


MESH = pl.DeviceIdType.MESH
W_IN_BLK = D_IN // N_DEV
W_OUT_BLK = D_MODEL // N_DEV
PW_BLK = W_MIX // N_DEV
DW_BLK = W_MIX // N_DEV
N_PEER = N_DEV - 1


SLAB = 2 * W_IN_BLK
N_SLABS = 4
F_SIB, F_X, F_Y, F_ON, F_PASS_X, F_PASS_Y, F_PASS_D = range(7)
W_IN_ONLY, OTHERS = (0,), (1, 2, 3)


def _inproj_body(n_tiles, x_ref, ng_ref, wint_ref, wout_ref, pww_ref, dw_ref,
                 proj_hbm, wt_ref, woutf_ref, pwwf_ref, dwf_ref,
                 hn_s, wt_s, wout_s, pww_s, dw_blk, stage, out_sems, send_sems, recv_sems):
    s = pl.program_id(0)
    i = pl.program_id(1)
    tm = TILE_IN
    x, y, c = lax.axis_index("x"), lax.axis_index("y"), lax.axis_index("c")
    sibling = (x, y, 1 - c)
    xn, yn, dg = (1 - x, y), (x, 1 - y), (1 - x, 1 - y)
    pick = lambda a, b: tuple(jnp.where(c == 1, p, q) for p, q in zip(a, b))
    on_from, on_to = pick(xn, yn), pick(yn, xn)

    def blocks(px, py, pc):
        k = 4 * px + 2 * py + pc
        rows = lambda ref, n: ref.at[pl.ds(pl.multiple_of(k * n, n), n), :]
        return (rows(wt_s, W_IN_BLK), rows(wout_s, W_OUT_BLK), rows(pww_s, PW_BLK), dw_blk.at[k])

    def copies(arrays, flow, block, to):
        refs = blocks(*block)
        return [pltpu.make_async_remote_copy(src_ref=refs[a], dst_ref=refs[a], send_sem=send_sems.at[a, flow],
                                             recv_sem=recv_sems.at[a, flow], device_id=to, device_id_type=MESH)
                for a in arrays]

    def start(arrays):
        for flow, to in ((F_SIB, sibling), (F_X, (*xn, c)), (F_Y, (*yn, c))):
            for cp in copies(arrays, flow, (x, y, c), to):
                cp.start()

    def direct(arrays):
        for flow, chip in ((F_X, xn), (F_Y, yn)):
            for cp in copies(arrays, flow, (*chip, c), sibling):
                cp.wait_recv()
        for cp in copies(arrays, F_ON, (*on_from, c), (*on_to, c)):
            cp.start()
        for flow, chip in ((F_PASS_X, xn), (F_PASS_Y, yn)):
            for cp in copies(arrays, flow, (*chip, c), sibling):
                cp.start()

    def diagonal(arrays):
        for cp in copies(arrays, F_ON, (*dg, c), sibling):
            cp.wait_recv()
        for cp in copies(arrays, F_PASS_D, (*dg, c), sibling):
            cp.start()

    def from_sibling(arrays, flow, chip):
        for cp in copies(arrays, flow, (*chip, 1 - c), sibling):
            cp.wait_recv()

    def finish(arrays):
        mine = (x, y, c)
        for flow, block in ((F_SIB, mine), (F_X, mine), (F_Y, mine), (F_ON, (*on_from, c)), (F_PASS_X, (*xn, c)),
                            (F_PASS_Y, (*yn, c)), (F_PASS_D, (*dg, c))):
            for cp in copies(arrays, flow, block, sibling):
                cp.wait_send()

    step = s * n_tiles + i
    par = step % 2
    first = step == 0

    @pl.when(first)
    def _():
        mine = blocks(x, y, c)
        mine[0][...] = wint_ref[...].astype(BF16)
        mine[1][...] = wout_ref[...].astype(BF16)
        mine[2][...] = pww_ref[...].astype(BF16)
        mine[3][...] = dw_ref[...]
        start(W_IN_ONLY)
        start(OTHERS)

    rows_i = pl.ds(pl.multiple_of(i * tm, tm), tm)

    @pl.when(s == 0)
    def _():
        xt = x_ref[0]
        r1 = lax.rsqrt(jnp.mean(xt * xt, axis=-1, keepdims=True) + RMS_EPS)
        hn_s[rows_i, :] = (xt * r1 * ng_ref[...]).astype(BF16)

    @pl.when(first)
    def _():
        from_sibling(W_IN_ONLY, F_SIB, (x, y))

    @pl.when((s == 1) & (i == 0))
    def _():
        direct(W_IN_ONLY)
        from_sibling(W_IN_ONLY, F_PASS_Y, yn)

    @pl.when((s == 2) & (i == 0))
    def _():
        from_sibling(W_IN_ONLY, F_PASS_X, xn)

    @pl.when((s == 3) & (i == 0))
    def _():
        diagonal(W_IN_ONLY)
        direct(OTHERS)
        from_sibling(W_IN_ONLY, F_PASS_D, dg)

    tiles_per_seq = proj_hbm.shape[1] // tm

    def out_copy(parity, col0):
        dst = proj_hbm.at[i // tiles_per_seq, pl.ds(pl.multiple_of((i % tiles_per_seq) * tm, tm), tm),
                          pl.ds(pl.multiple_of(col0, LANES), SLAB)]
        return pltpu.make_async_copy(stage.at[parity], dst, out_sems.at[parity])

    @pl.when(step >= 2)
    def _():
        out_copy(par, 0).wait()

    chip = jnp.where(s < 2, 2 * x, 2 * (1 - x)) + jnp.where((s == 0) | (s == 2), y, 1 - y)
    slab = wt_s[pl.ds(pl.multiple_of(chip * SLAB, SLAB), SLAB), :]
    stage[par] = lax.dot_general(hn_s[rows_i, :], slab, NT_DIMS, preferred_element_type=F32)
    out_copy(par, chip * SLAB).start()

    @pl.when(step == N_SLABS * n_tiles - 1)
    def _():
        out_copy(par, 0).wait()
        out_copy(1 - par, 0).wait()
        from_sibling(OTHERS, F_SIB, (x, y))
        from_sibling(OTHERS, F_PASS_X, xn)
        from_sibling(OTHERS, F_PASS_Y, yn)
        diagonal(OTHERS)
        from_sibling(OTHERS, F_PASS_D, dg)
        finish(W_IN_ONLY)
        finish(OTHERS)
        wt_ref[...] = wt_s[...]
        woutf_ref[...] = wout_s[...]
        pwwf_ref[...] = pww_s[...]
        for k in range(N_DEV):
            dwf_ref[:, k * DW_BLK:(k + 1) * DW_BLK] = dw_blk[k]


def _inproj(x, norm_g, w_in_t, w_out, pw_w, conv_dw):
    bl, s, _ = x.shape
    tm = TILE_IN
    nt = s // tm
    n_tiles = bl * nt

    def x_map(phase, i):
        i = jnp.where(phase == 0, i, n_tiles - 1)
        return (i // nt, i % nt, 0)

    dma = pltpu.SemaphoreType.DMA
    return pl.pallas_call(
        functools.partial(_inproj_body, n_tiles),
        name="inproj",
        grid=(N_SLABS, n_tiles),
        in_specs=[pl.BlockSpec((1, tm, D_MODEL), x_map)] + [VMEM] * 5,
        out_specs=[pl.BlockSpec(memory_space=pl.ANY)] + [VMEM] * 4,
        out_shape=[jax.ShapeDtypeStruct((bl, s, D_IN), F32), jax.ShapeDtypeStruct((D_IN, D_MODEL), BF16),
                   jax.ShapeDtypeStruct((D_MODEL, D_MODEL), BF16), jax.ShapeDtypeStruct((W_MIX, W_MIX), BF16),
                   jax.ShapeDtypeStruct((CONV_WIDTH, W_MIX), F32)],
        scratch_shapes=[pltpu.VMEM((n_tiles * tm, D_MODEL), BF16), pltpu.VMEM((D_IN, D_MODEL), BF16),
                        pltpu.VMEM((D_MODEL, D_MODEL), BF16), pltpu.VMEM((W_MIX, W_MIX), BF16),
                        pltpu.VMEM((N_DEV, CONV_WIDTH, DW_BLK), F32), pltpu.VMEM((2, tm, SLAB), F32),
                        dma((2,)), dma((4, 7)), dma((4, 7))],
        compiler_params=pltpu.CompilerParams(dimension_semantics=("arbitrary", "arbitrary"), vmem_limit_bytes=VMEM_LIMIT),
    )(x, _row(norm_g), w_in_t, w_out, pw_w, conv_dw)


B_PLAIN, B_ON, B_MERGED = range(3)


def _routes():
    x, y, c = lax.axis_index("x"), lax.axis_index("y"), lax.axis_index("c")
    xn, yn = (1 - x, y), (x, 1 - y)
    pick = lambda a, b: tuple(jnp.where(c == 1, p, q) for p, q in zip(a, b))
    return pick(xn, yn), pick(yn, xn), (1 - x, 1 - y), (x, y), c


def _chip_index(chip):
    return 2 * chip[0] + chip[1]


def _copy(src, dst, send_sems, recv_sems, a, k, to):
    return pltpu.make_async_remote_copy(src_ref=src, dst_ref=dst, send_sem=send_sems.at[a, k], recv_sem=recv_sems.at[a, k],
                                        device_id=to, device_id_type=MESH)


def _rsb_send(plain, on, rb_plain, rb_on, send_b, recv_b):
    first, _, _, _, c = _routes()
    for k, srcs, dsts in ((B_ON, on, rb_on), (B_PLAIN, plain, rb_plain)):
        for a, (s, d) in enumerate(zip(srcs, dsts)):
            _copy(s, d, send_b, recv_b, a, k, (*first, c)).start()


def _rsb_merge(second_sums, merged, rb_on, rb_merged, send_b, recv_b):
    _, second, _, _, c = _routes()
    for a, (own, m, got, dst) in enumerate(zip(second_sums, merged, rb_on, rb_merged)):
        _copy(got, got, send_b, recv_b, a, B_ON, (*second, c)).wait_recv()
        m[...] = (own[...] + got[...].astype(F32)).astype(BF16)
        _copy(m, dst, send_b, recv_b, a, B_MERGED, (*second, c)).start()


def _rsb_total(own_sums, plain, on, merged, rb_plain, rb_merged, send_b, recv_b):
    _, second, _, _, c = _routes()
    out = []
    for a, (own, p, g) in enumerate(zip(own_sums, rb_plain, rb_merged)):
        _copy(p, p, send_b, recv_b, a, B_PLAIN, (*second, c)).wait_recv()
        _copy(g, g, send_b, recv_b, a, B_MERGED, (*second, c)).wait_recv()
        out.append(own[...] + p[...].astype(F32) + g[...].astype(F32))
    for k, srcs in ((B_PLAIN, plain), (B_ON, on), (B_MERGED, merged)):
        for a, s in enumerate(srcs):
            _copy(s, s, send_b, recv_b, a, k, (*second, c)).wait_send()
    return out


R_LOSS, R_FINAL_G, R_POOL, R_CONV, R_LN = range(5)
VEC_ROWS = 16
DW_ROWS = 32
A_PART, A_X, A_Y, A_ON, A_HALF = range(5)
STEP_SUMS, STEP_MERGE, STEP_HALF = 1, 3, 5


def _bwd2a_body(n_steps, x_ref, dproj_ref, ng_ref, gwout_ref, gpww_ref, loss_ref, gfg_ref, gvec_ref, gdw_ref, gpoolw_ref,
                win_own_ref, win_staged_ref, owout_ref, opww_ref, ovec_ref, odw_ref, opoolw_ref,
                acc, st_win, a_win, a_wout, a_pww, st_wout, st_pww, rb_wout, rb_pww, part_vec, part_dw, part_pw,
                sib_vec, sib_dw, sib_pw, res_vec, res_dw, res_pw, half_vec, half_dw, half_pw,
                send_a, recv_a, send_b, recv_b, send_r, recv_r, send_w, recv_w):
    step = pl.program_id(0) * pl.num_programs(1) + pl.program_id(1)
    x, y, c = lax.axis_index("x"), lax.axis_index("y"), lax.axis_index("c")
    sibling = (x, y, 1 - c)
    xn, yn = (1 - x, y), (x, 1 - y)
    first, second, diag, own, _ = _routes()
    rows = lambda ref, k, n: ref.at[pl.ds(pl.multiple_of(k * n, n), n), :]

    blocks = lambda k: (rows(gwout_ref, k, W_OUT_BLK), rows(gpww_ref, k, PW_BLK))
    landed = lambda chip: (a_wout.at[chip], a_pww.at[chip])
    staged = lambda k: (st_wout.at[k], st_pww.at[k])
    arrived = lambda k: (rb_wout.at[k], rb_pww.at[k])

    parts, sibs, halves = (part_vec, part_dw, part_pw), (sib_vec, sib_dw, sib_pw), (half_vec, half_dw, half_pw)
    results = (res_vec, res_dw, res_pw)
    half_rows = lambda ref: pl.ds(pl.multiple_of(c * (ref.shape[0] // 2), SUBLANES), ref.shape[0] // 2)
    slot = lambda chip: tuple(h.at[_chip_index(chip)] for h in halves)

    @pl.when(step == 0)
    def _():
        acc[...] = jnp.zeros_like(acc)
        sum8 = lambda v: jnp.sum(v, axis=0, keepdims=True)
        part_vec[...] = jnp.zeros((VEC_ROWS, D_MODEL), F32)
        part_vec[R_LOSS:R_LOSS + 1, :] = jnp.broadcast_to(loss_ref[0:1, 0:1], (1, D_MODEL))
        part_vec[R_FINAL_G:R_FINAL_G + 1, :] = sum8(gfg_ref[...])
        for row, (lo, hi) in ((R_POOL, (V_POOL_B, V_POOL_SCALE)), (R_CONV, (V_CONV_B, V_LN_G)), (R_LN, (V_LN_B, V_PW_B))):
            part_vec[row:row + 1, 0:W_MIX] = sum8(gvec_ref[lo])
            part_vec[row:row + 1, W_MIX:] = sum8(gvec_ref[hi])
        part_dw[0:CONV_WIDTH, :] = jnp.sum(gdw_ref[...], axis=1)
        part_dw[CONV_WIDTH:, :] = jnp.zeros((DW_ROWS - CONV_WIDTH, W_MIX), F32)
        part_pw[...] = gpoolw_ref[...].reshape(len(POOL_WINDOWS) * POOL_GC, POOL_GC)
        for a, (p, s) in enumerate(zip(parts, sibs)):
            _copy(p, s, send_r, recv_r, a, A_PART, sibling).start()
        for chip in range(4):
            for a, (s, d) in enumerate(zip(blocks(2 * chip + (1 - c)), landed(chip))):
                _copy(s, d, send_a, recv_a, a, chip, sibling).start()

    x_t = x_ref[0]
    r1 = lax.rsqrt(jnp.mean(x_t * x_t, axis=-1, keepdims=True) + RMS_EPS)
    hn = (x_t * r1 * ng_ref[...]).astype(BF16)
    acc[...] += lax.dot_general(dproj_ref[0], hn, TN_DIMS, preferred_element_type=F32)

    @pl.when(step == STEP_SUMS)
    def _():
        for a, (p, s, h) in enumerate(zip(parts, sibs, slot(own))):
            _copy(p, s, send_r, recv_r, a, A_PART, sibling).wait_recv()
            h[...] = p[half_rows(p), :] + s[half_rows(s), :]
            for k, to in ((A_X, xn), (A_Y, yn)):
                _copy(h, h, send_r, recv_r, a, k, (*to, c)).start()
        for chip in range(4):
            for a, (m, l) in enumerate(zip(blocks(2 * chip + c), landed(chip))):
                _copy(l, l, send_a, recv_a, a, chip, sibling).wait_recv()
                l[...] = m[...] + l[...]
        for k, chip in ((B_ON, diag), (B_PLAIN, first)):
            for s, l in zip(staged(k), landed(_chip_index(chip))):
                s[...] = l[...].astype(BF16)
        _rsb_send(staged(B_PLAIN), staged(B_ON), arrived(B_PLAIN), arrived(B_ON), send_b, recv_b)

    @pl.when(step == STEP_MERGE)
    def _():
        _rsb_merge(landed(_chip_index(second)), staged(B_MERGED), arrived(B_ON), arrived(B_MERGED), send_b, recv_b)
        for a in range(3):
            for k, chip in ((A_X, xn), (A_Y, yn)):
                h = slot(chip)[a]
                _copy(h, h, send_r, recv_r, a, k, sibling).wait_recv()
            h = slot(first)[a]
            _copy(h, h, send_r, recv_r, a, A_ON, (*second, c)).start()

    @pl.when(step == STEP_HALF)
    def _():
        for a, (r, hs) in enumerate(zip(results, halves)):
            h = slot(diag)[a]
            _copy(h, h, send_r, recv_r, a, A_ON, sibling).wait_recv()
            mine = r.at[half_rows(r), :]
            mine[...] = (hs[0] + hs[1]) + (hs[2] + hs[3])
            _copy(mine, mine, send_r, recv_r, a, A_HALF, sibling).start()

    @pl.when(step == n_steps - 1)
    def _():
        block = lambda k: rows(acc, k, W_IN_BLK)
        w_copy = lambda chip: _copy(st_win.at[chip], a_win.at[chip], send_w, recv_w, 0, chip, sibling)
        for chip in range(4):
            st_win[chip] = block(2 * chip + (1 - c))[...].astype(BF16)
            w_copy(chip).start()

        owout_ref[...], opww_ref[...] = _rsb_total(landed(_chip_index(own)), staged(B_PLAIN), staged(B_ON), staged(B_MERGED),
                                                   arrived(B_PLAIN), arrived(B_MERGED), send_b, recv_b)
        for a, (r, p, s) in enumerate(zip(results, parts, sibs)):
            theirs = r.at[pl.ds(pl.multiple_of((1 - c) * (r.shape[0] // 2), SUBLANES), r.shape[0] // 2), :]
            _copy(theirs, theirs, send_r, recv_r, a, A_HALF, sibling).wait_recv()
            _copy(p, s, send_r, recv_r, a, A_PART, sibling).wait_send()
            for k, h in ((A_X, slot(own)[a]), (A_Y, slot(own)[a]), (A_ON, slot(first)[a])):
                _copy(h, h, send_r, recv_r, a, k, sibling).wait_send()
            mine = r.at[half_rows(r), :]
            _copy(mine, mine, send_r, recv_r, a, A_HALF, sibling).wait_send()
        for chip in range(4):
            for a, (s, d) in enumerate(zip(blocks(2 * chip + (1 - c)), landed(chip))):
                _copy(s, d, send_a, recv_a, a, chip, sibling).wait_send()
        ovec_ref[...] = res_vec[...]
        opoolw_ref[...] = res_pw[...]
        for k in range(N_DEV):
            @pl.when(4 * x + 2 * y + c == k)
            def _():
                odw_ref[...] = res_dw[0:CONV_WIDTH, k * DW_BLK:(k + 1) * DW_BLK]

        for chip in range(4):
            w_copy(chip).wait()
        chip_sum = lambda chip: block(2 * _chip_index(chip) + c)[...] + a_win[_chip_index(chip)].astype(F32)
        win_own_ref[...] = chip_sum(own)
        for j, chip in enumerate((xn, yn, diag)):
            win_staged_ref[j] = chip_sum(chip).astype(BF16)


def _bwd2a(x, dproj, norm_g, gwout, gpww, loss8, gfg8, gvec, gdw8, gpoolw):
    bl, s, _ = x.shape
    tm = TILE_M
    nt = s // tm
    tok = lambda n: pl.BlockSpec((1, tm, n), lambda b, i: (b, i, 0))
    n_pw = len(POOL_WINDOWS) * POOL_GC
    win = (W_IN_BLK, D_MODEL)
    blk = ((W_OUT_BLK, D_MODEL), (PW_BLK, W_MIX))
    rep = ((VEC_ROWS, D_MODEL), (DW_ROWS, W_MIX), (n_pw, POOL_GC))
    half = lambda sh: (sh[0] // 2,) + sh[1:]
    dma = pltpu.SemaphoreType.DMA
    assert bl * nt > STEP_HALF + 1, "the exchanges' phases need their grid steps"
    return pl.pallas_call(
        functools.partial(_bwd2a_body, bl * nt),
        name="bwd2a",
        grid=(bl, nt),
        in_specs=[tok(D_MODEL), tok(D_IN)] + [VMEM] * 8,
        out_specs=[VMEM] * 7,
        out_shape=[jax.ShapeDtypeStruct(win, F32), jax.ShapeDtypeStruct((3,) + win, BF16)]
        + [jax.ShapeDtypeStruct(sh, F32) for sh in blk + (rep[0], (CONV_WIDTH, DW_BLK), rep[2])],
        scratch_shapes=[pltpu.VMEM((D_IN, D_MODEL), F32), pltpu.VMEM((4,) + win, BF16), pltpu.VMEM((4,) + win, BF16)]
        + [pltpu.VMEM((4,) + sh, F32) for sh in blk]
        + [pltpu.VMEM((3,) + sh, BF16) for sh in blk] * 2
        + [pltpu.VMEM(sh, F32) for sh in rep] * 3
        + [pltpu.VMEM((4,) + half(sh), F32) for sh in rep]
        + [dma((2, 4)), dma((2, 4)), dma((2, 3)), dma((2, 3)), dma((3, 5)), dma((3, 5)), dma((1, 4)), dma((1, 4))],
        compiler_params=pltpu.CompilerParams(dimension_semantics=("arbitrary", "arbitrary"), vmem_limit_bytes=VMEM_LIMIT),
    )(x, dproj, _row(norm_g), gwout, gpww, loss8, gfg8, gvec, gdw8, gpoolw)


HBM = pl.BlockSpec(memory_space=pltpu.HBM)
SEM = pl.BlockSpec(memory_space=pltpu.SEMAPHORE)
EFFECT = pltpu.SideEffectType.DATAFLOW_SIDE_EFFECTING


def _owner_copies(src_ref, land_ref, send_sems, recv_sems):
    x, y, c = lax.axis_index("x"), lax.axis_index("y"), lax.axis_index("c")
    owners = ((1 - x, y, c), (x, 1 - y, c), (1 - x, 1 - y, c))
    return [pltpu.make_async_remote_copy(src_ref=src_ref.at[j], dst_ref=land_ref.at[j], send_sem=send_sems.at[j],
                                         recv_sem=recv_sems.at[j], device_id=owners[j], device_id_type=MESH)
            for j in range(3)]


def _rs_start(staged):
    def body(src_ref, land_ref, send_sems, recv_sems, src_thru, land_thru, token):
        for cp in _owner_copies(src_ref, land_ref, send_sems, recv_sems):
            cp.start()
        token[...] = jnp.zeros_like(token)

    dma = pltpu.SemaphoreType.DMA
    return pl.pallas_call(
        body, name="rs_start",
        out_shape=(dma((3,)), dma((3,)), pltpu.HBM(staged.shape, staged.dtype), pltpu.HBM(staged.shape, staged.dtype),
                   jax.ShapeDtypeStruct((SUBLANES, LANES), F32)),
        in_specs=(HBM, HBM), out_specs=(SEM, SEM, HBM, HBM, VMEM), input_output_aliases={0: 2, 1: 3},
        compiler_params=pltpu.CompilerParams(has_side_effects=EFFECT),
    )(pltpu.with_memory_space_constraint(staged, pltpu.HBM),
      pltpu.with_memory_space_constraint(lax.empty(staged.shape, staged.dtype), pltpu.HBM))


def _rs_wait(send_sems, recv_sems, src_thru, land_thru, after):
    def body(src_ref, land_ref, send_sems, recv_sems, after_ref, src_dead, got_ref):
        for cp in _owner_copies(src_ref, land_ref, send_sems, recv_sems):
            cp.wait_send()
            cp.wait_recv()

    return pl.pallas_call(
        body, name="rs_wait",
        out_shape=(pltpu.HBM(src_thru.shape, src_thru.dtype), pltpu.HBM(land_thru.shape, land_thru.dtype)),
        in_specs=(HBM, HBM, SEM, SEM, pl.BlockSpec(memory_space=pl.ANY)), out_specs=(HBM, HBM),
        input_output_aliases={0: 0, 1: 1},
        compiler_params=pltpu.CompilerParams(has_side_effects=EFFECT),
    )(src_thru, land_thru, send_sems, recv_sems, after)[1]


def _bwd2b_body(x_ref, dh_ref, dproj_ref, ng_ref, wint_ref, token_ref, gx_ref, gng_ref):
    @pl.when((pl.program_id(0) == 0) & (pl.program_id(1) == 0))
    def _():
        gng_ref[...] = jnp.zeros_like(gng_ref) + token_ref[0:1, 0:1]

    x_t = x_ref[0]
    r1 = lax.rsqrt(jnp.mean(x_t * x_t, axis=-1, keepdims=True) + RMS_EPS)
    xr = x_t * r1
    dhn = jnp.dot(dproj_ref[0], wint_ref[...], preferred_element_type=F32)
    gng_ref[...] += _colsum8(dhn * xr)
    gy = dhn * ng_ref[...]
    gx_ref[0] = dh_ref[0] + r1 * (gy - xr * jnp.mean(gy * xr, axis=-1, keepdims=True))


def _bwd2b(x, dh, dproj, norm_g, w_in_t, token):
    bl, s, _ = x.shape
    tm = TILE_M
    tok = lambda n: pl.BlockSpec((1, tm, n), lambda b, i: (b, i, 0))
    return pl.pallas_call(
        functools.partial(_bwd2b_body),
        name="bwd2b",
        grid=(bl, s // tm),
        in_specs=[tok(D_MODEL), tok(D_MODEL), tok(D_IN), VMEM, VMEM, VMEM],
        out_specs=[tok(D_MODEL), pl.BlockSpec((SUBLANES, D_MODEL), lambda b, i: (0, 0))],
        out_shape=[jax.ShapeDtypeStruct((bl, s, D_MODEL), F32), jax.ShapeDtypeStruct((SUBLANES, D_MODEL), F32)],
        compiler_params=pltpu.CompilerParams(dimension_semantics=("arbitrary", "arbitrary"), vmem_limit_bytes=VMEM_LIMIT),
    )(x, dh, dproj, _row(norm_g), w_in_t, token)


def _norm_g_body(part_ref, out_ref, all_ng, send_n, recv_n):
    x, y, c = lax.axis_index("x"), lax.axis_index("y"), lax.axis_index("c")
    other_chips = [(1 - x, y), (x, 1 - y), (1 - x, 1 - y)]
    k_me = 4 * x + 2 * y + c
    peers = [(x, y, 1 - c)] + [(*chip, c) for chip in other_chips] + [(*chip, 1 - c) for chip in other_chips]
    copies = [pltpu.make_async_remote_copy(src_ref=all_ng.at[k_me], dst_ref=all_ng.at[k_me], send_sem=send_n.at[r],
                                           recv_sem=recv_n.at[r], device_id=peers[r], device_id_type=MESH)
              for r in range(N_PEER)]
    all_ng[k_me] = part_ref[...]
    for cp in copies:
        cp.start()
    for cp in copies:
        cp.wait()
    s = all_ng[0]
    for k in range(1, N_DEV):
        s = s + all_ng[k]
    out_ref[...] = jnp.broadcast_to(jnp.sum(s, axis=0, keepdims=True), out_ref.shape)


def _norm_g_sum(part):
    dma = pltpu.SemaphoreType.DMA
    return pl.pallas_call(
        functools.partial(_norm_g_body),
        name="norm_g_sum",
        in_specs=[VMEM],
        out_specs=VMEM,
        out_shape=jax.ShapeDtypeStruct(part.shape, F32),
        scratch_shapes=[pltpu.VMEM((N_DEV,) + part.shape, F32), dma((N_PEER,)), dma((N_PEER,))],
    )(part)


ADAM_LR = 0.001
ADAM_B1 = 0.9
ADAM_B2 = 0.999
ADAM_EPS = 1e-08
ADAM_WD = 0.01
ADAM_STEP = 10


PARAMS = ("norm_g", "w_in", "pool_w", "pool_b", "pool_scale", "conv_dw", "conv_b", "ln_g", "ln_b", "pw_w", "pw_b", "w_out",
          "final_g")


def _adamw_body(*refs):
    n = len(PARAMS)
    ws, ms, vs = (dict(zip(PARAMS, refs[k * n:(k + 1) * n])) for k in range(3))
    gwin_ref, gwin_others_ref, gwout_ref, gpww_ref, vec_ref, gdw_ref, gpoolw_ref, gng_ref = refs[3 * n:3 * n + 8]
    outs = refs[3 * n + 8:]
    gs, ds, m2s, v2s = (dict(zip(PARAMS, outs[k * n:(k + 1) * n])) for k in range(4))
    loss_ref = outs[4 * n]

    def update(name, g, at=slice(None)):
        m = ADAM_B1 * ms[name][at] + (1.0 - ADAM_B1) * g
        v = ADAM_B2 * vs[name][at] + (1.0 - ADAM_B2) * (g * g)
        m_hat = m / (1.0 - ADAM_B1 ** ADAM_STEP)
        v_hat = v / (1.0 - ADAM_B2 ** ADAM_STEP)
        gs[name][at] = g
        ds[name][at] = -ADAM_LR * (m_hat / (jnp.sqrt(v_hat) + ADAM_EPS) + ADAM_WD * ws[name][at])
        m2s[name][at] = m
        v2s[name][at] = v

    row = lambda r, lo, hi: vec_ref[r:r + 1, lo:hi]
    loss_ref[...] = row(R_LOSS, 0, 1)
    update("norm_g", gng_ref[0:1, :])
    g_w_in = gwin_ref[...]
    for j in range(3):
        g_w_in = g_w_in + gwin_others_ref[j].astype(F32)
    update("w_in", g_w_in)
    update("pool_w", gpoolw_ref[...].reshape(ws["pool_w"].shape))
    for g in range(len(POOL_WINDOWS)):
        update("pool_b", row(R_POOL, g * POOL_GC, (g + 1) * POOL_GC), at=slice(g, g + 1))
    update("pool_scale", row(R_POOL, W_MIX, 2 * W_MIX))
    update("conv_dw", gdw_ref[...])
    update("conv_b", row(R_CONV, 0, W_MIX))
    update("ln_g", row(R_CONV, W_MIX, 2 * W_MIX))
    update("ln_b", row(R_LN, 0, W_MIX))
    update("pw_w", gpww_ref[...])
    update("pw_b", row(R_LN, W_MIX, 2 * W_MIX))
    update("w_out", gwout_ref[...])
    update("final_g", row(R_FINAL_G, 0, D_MODEL))


def _adamw(ws, ms, vs, grad_sources):
    n = len(PARAMS)
    shapes = [ws[p].shape for p in PARAMS]
    flat = lambda a: a.reshape(1, -1) if a.ndim == 1 else a
    args = [flat(d[p]) for d in (ws, ms, vs) for p in PARAMS]
    outs = pl.pallas_call(
        functools.partial(_adamw_body),
        name="adamw",
        in_specs=[VMEM] * (3 * n + len(grad_sources)),
        out_specs=[VMEM] * (4 * n + 1),
        out_shape=[jax.ShapeDtypeStruct(a.shape, F32) for a in args[:n]] * 4 + [jax.ShapeDtypeStruct((1, 1), F32)],
        compiler_params=pltpu.CompilerParams(vmem_limit_bytes=VMEM_LIMIT),
    )(*args, *grad_sources)
    loss = outs[4 * n].reshape(())
    outs = [o.reshape(shapes[k % n]) for k, o in enumerate(outs[:4 * n])]
    return loss, outs[:n], outs[n:2 * n], outs[2 * n:3 * n], outs[3 * n:]


def kernel(x, norm_g, w_in, pool_w, pool_b, pool_scale, conv_dw, conv_b, ln_g, ln_b, pw_w, pw_b, w_out, final_g, loss_target, m_norm_g, m_w_in, m_pool_w, m_pool_b, m_pool_scale, m_conv_dw, m_conv_b, m_ln_g, m_ln_b, m_pw_w, m_pw_b, m_w_out, m_final_g, v_norm_g, v_w_in, v_pool_w, v_pool_b, v_pool_scale, v_conv_dw, v_conv_b, v_ln_g, v_ln_b, v_pw_w, v_pw_b, v_w_out, v_final_g):
    proj, w_in_t, w_out_b, pw_w_b, conv_dw_f = _inproj(x, norm_g, w_in.T, w_out, pw_w, conv_dw)
    h1, z, ob, dpool, dh, loss8, gfg8 = _fwd(x, loss_target, proj, pool_w, pool_b, pool_scale, conv_dw_f,
                                             conv_b, ln_g, ln_b, pw_w_b, pw_b, w_out_b, final_g)
    dproj, gwout, gpww, gpoolw, gvec, gdw8 = _bwd1(dh, proj, h1, z, ob, dpool, w_out_b, pw_w_b, pool_w, pool_scale,
                                                   conv_dw_f, ln_g, ln_b)
    win_own, win_staged, g_w_out, g_pw_w, vec, g_conv_dw, poolw_sum = _bwd2a(x, dproj, norm_g, gwout, gpww, loss8, gfg8, gvec,
                                                                             gdw8, gpoolw)
    send_sems, recv_sems, staged_thru, land_thru, token = _rs_start(win_staged)
    grad_x, gng8 = _bwd2b(x, dh, dproj, norm_g, w_in_t, token)
    g_norm_g = _norm_g_sum(gng8)
    win_others = _rs_wait(send_sems, recv_sems, staged_thru, land_thru, gng8)

    weights = dict(norm_g=norm_g, w_in=w_in.T, pool_w=pool_w, pool_b=pool_b, pool_scale=pool_scale, conv_dw=conv_dw, conv_b=conv_b,
                   ln_g=ln_g, ln_b=ln_b, pw_w=pw_w, pw_b=pw_b, w_out=w_out, final_g=final_g)
    m_in = dict(norm_g=m_norm_g, w_in=m_w_in.T, pool_w=m_pool_w, pool_b=m_pool_b, pool_scale=m_pool_scale, conv_dw=m_conv_dw,
                conv_b=m_conv_b, ln_g=m_ln_g, ln_b=m_ln_b, pw_w=m_pw_w, pw_b=m_pw_b, w_out=m_w_out, final_g=m_final_g)
    v_in = dict(norm_g=v_norm_g, w_in=v_w_in.T, pool_w=v_pool_w, pool_b=v_pool_b, pool_scale=v_pool_scale, conv_dw=v_conv_dw,
                conv_b=v_conv_b, ln_g=v_ln_g, ln_b=v_ln_b, pw_w=v_pw_w, pw_b=v_pw_b, w_out=v_w_out, final_g=v_final_g)
    loss, *groups = _adamw(weights, m_in, v_in, (win_own, win_others, g_w_out, g_pw_w, vec, g_conv_dw, poolw_sum, g_norm_g))
    k_in = PARAMS.index("w_in")
    for group in groups:
        group[k_in] = group[k_in].T
    grads, delta, new_m, new_v = groups
    return (loss, grad_x, *grads, *delta, *new_m, *new_v)
```

```python
import functools

import jax
import jax.numpy as jnp
from jax import lax
from jax.experimental import pallas as pl
from jax.experimental.pallas import tpu as pltpu

F32 = jnp.float32
BF16 = jnp.bfloat16

D_MODEL = 1024
W_MIX = 512
D_IN = 5 * W_MIX
POOL_WINDOWS = (2, 4, 8, 16)
POOL_GC = 128
CONV_WIDTH = 31
RMS_EPS = 1e-6
LN_EPS = 1e-5
N_DEV = 8

LANES = 128
SUBLANES = 8
TILE_M = 512
TILE_IN = 1024
POOL_HALO = 16
CONV_HALO = 32
CONV_ROWS = 128
BWD_CONV_ROWS = 64
BWD_SUBTILES = 2
VMEM_LIMIT = 56 * 1024 * 1024

VMEM = pl.BlockSpec(memory_space=pltpu.VMEM)


def _silu(v):
    return v * jax.nn.sigmoid(v)


def _colsum8(v):
    m, n = v.shape
    return jnp.sum(v.reshape(m // SUBLANES, SUBLANES, n), axis=0)


def _row_index(shape, t0):
    return lax.broadcasted_iota(jnp.int32, shape, 0) + t0


def _conv_taps():
    taps = {s: [] for s in range(SUBLANES)}
    for k in range(CONV_WIDTH):
        e = k + CONV_HALO - (CONV_WIDTH - 1)
        taps[e % SUBLANES].append((e // SUBLANES, k))
    return taps


def _fwd_body(x_ref, tgt_ref, proj_ref, poolw_ref, poolb_ref, pools_ref, dw_ref, cb_ref, lg_ref, lb_ref,
              pww_ref, pwb_ref, wout_ref, fg_ref,
              h1_ref, z_ref, ob_ref, dpool_ref, dh_ref, loss_ref, gfg_ref,
              uext, hext, y_s):
    b = pl.program_id(0)
    i = pl.program_id(1)
    tm = TILE_M

    @pl.when(i == 0)
    def _():
        uext[0:POOL_HALO, :] = jnp.zeros((POOL_HALO, W_MIX), F32)
        hext[0:CONV_HALO, :] = jnp.zeros((CONV_HALO, W_MIX), F32)
        hext[CONV_HALO + tm:, :] = jnp.zeros((SUBLANES, W_MIX), F32)

    @pl.when((b == 0) & (i == 0))
    def _():
        loss_ref[...] = jnp.zeros_like(loss_ref)
        gfg_ref[...] = jnp.zeros_like(gfg_ref)

    uext[POOL_HALO:, :] = proj_ref[0, :, 0:W_MIX]
    t = _row_index((tm, POOL_GC), i * tm)
    for g, w in enumerate(POOL_WINDOWS):
        lanes = slice(g * POOL_GC, (g + 1) * POOL_GC)
        e = uext[:, lanes]
        s = e
        sh = 1
        while sh < w:
            s = s + pltpu.roll(s, sh, 0)
            sh *= 2
        inv_cnt = 1.0 / jnp.minimum(t + 1, w).astype(F32)
        d = (s[POOL_HALO:] * inv_cnt - e[POOL_HALO:]).astype(BF16)
        dpool_ref[0, :, lanes] = d
        z = jnp.dot(d, poolw_ref[g].astype(BF16), preferred_element_type=F32) + poolb_ref[:, lanes]
        z_ref[0, :, lanes] = z
        a_gate = proj_ref[0, :, W_MIX + g * POOL_GC:W_MIX + (g + 1) * POOL_GC]
        y_s[:, lanes] = (z * pools_ref[:, lanes] * _silu(a_gate)).astype(BF16)

    hext[CONV_HALO:CONV_HALO + tm, :] = proj_ref[0, :, 2 * W_MIX:3 * W_MIX] * jax.nn.sigmoid(proj_ref[0, :, 3 * W_MIX:4 * W_MIX])
    taps = _conv_taps()
    for c0 in range(0, tm, CONV_ROWS):
        for l0 in range(0, W_MIX, LANES):
            lanes = slice(l0, l0 + LANES)
            acc = jnp.zeros((CONV_ROWS, LANES), F32) + cb_ref[:, lanes]
            for s in range(SUBLANES):
                part = jnp.zeros((CONV_ROWS + SUBLANES, LANES), F32)
                for q, k in taps[s]:
                    r0 = c0 + SUBLANES * q
                    part = part + hext[r0:r0 + CONV_ROWS + SUBLANES, lanes] * dw_ref[k:k + 1, lanes]
                acc = acc + part[s:s + CONV_ROWS]
            h1_ref[0, c0:c0 + CONV_ROWS, lanes] = acc
    h1 = h1_ref[0]
    mu = jnp.mean(h1, axis=-1, keepdims=True)
    xc = h1 - mu
    rstd = lax.rsqrt(jnp.mean(xc * xc, axis=-1, keepdims=True) + LN_EPS)
    h3 = _silu(xc * rstd * lg_ref[...] + lb_ref[...])
    ob = jnp.dot(h3.astype(BF16), pww_ref[...], preferred_element_type=F32) + pwb_ref[...]
    ob_ref[0] = ob
    y_s[:, W_MIX:] = (ob * _silu(proj_ref[0, :, 4 * W_MIX:5 * W_MIX])).astype(BF16)

    h = x_ref[0] + jnp.dot(y_s[...], wout_ref[...], preferred_element_type=F32)
    r2 = lax.rsqrt(jnp.mean(h * h, axis=-1, keepdims=True) + RMS_EPS)
    hr = h * r2
    err = hr * fg_ref[...] - tgt_ref[0]
    loss_ref[...] += jnp.sum(err * err) * (0.5 / D_MODEL)
    dout = err * (1.0 / D_MODEL)
    gfg_ref[...] += _colsum8(dout * hr)
    gy = dout * fg_ref[...]
    dh_ref[0] = r2 * (gy - hr * jnp.mean(gy * hr, axis=-1, keepdims=True))

    uext[0:POOL_HALO, :] = uext[tm:tm + POOL_HALO, :]
    hext[0:CONV_HALO, :] = hext[tm:tm + CONV_HALO, :]


def _row(v):
    return v.reshape(1, -1)


def _fwd(x, tgt, proj, pool_w, pool_b, pool_scale, conv_dw_f, conv_b, ln_g, ln_b, pw_w_b, pw_b, w_out_b, final_g):
    bl, s, _ = x.shape
    tm = TILE_M
    tok = lambda n: pl.BlockSpec((1, tm, n), lambda b, i: (b, i, 0))
    acc = lambda n: pl.BlockSpec((SUBLANES, n), lambda b, i: (0, 0))
    act = lambda n: jax.ShapeDtypeStruct((bl, s, n), F32)
    return pl.pallas_call(
        functools.partial(_fwd_body),
        name="fwd",
        grid=(bl, s // tm),
        in_specs=[tok(D_MODEL), tok(D_MODEL), tok(D_IN)] + [VMEM] * 11,
        out_specs=[tok(W_MIX), tok(W_MIX), tok(W_MIX), tok(W_MIX), tok(D_MODEL), acc(LANES), acc(D_MODEL)],
        out_shape=[act(W_MIX), act(W_MIX), act(W_MIX), jax.ShapeDtypeStruct((bl, s, W_MIX), BF16), act(D_MODEL),
                   jax.ShapeDtypeStruct((SUBLANES, LANES), F32), jax.ShapeDtypeStruct((SUBLANES, D_MODEL), F32)],
        scratch_shapes=[pltpu.VMEM((POOL_HALO + tm, W_MIX), F32), pltpu.VMEM((CONV_HALO + tm + SUBLANES, W_MIX), F32),
                        pltpu.VMEM((tm, D_MODEL), BF16)],
        compiler_params=pltpu.CompilerParams(dimension_semantics=("arbitrary", "arbitrary"), vmem_limit_bytes=VMEM_LIMIT),
    )(x, tgt, proj, pool_w, pool_b.reshape(1, W_MIX), _row(pool_scale), conv_dw_f, _row(conv_b),
      _row(ln_g), _row(ln_b), pw_w_b, _row(pw_b), w_out_b, _row(final_g))


NT_DIMS = (((1,), (1,)), ((), ()))
TN_DIMS = (((0,), (0,)), ((), ()))
V_POOL_B, V_POOL_SCALE, V_CONV_B, V_LN_G, V_LN_B, V_PW_B = range(6)


def _bwd_taps():
    taps = {s: [] for s in range(SUBLANES)}
    for k in range(CONV_WIDTH):
        e = CONV_WIDTH - 1 - k
        taps[e % SUBLANES].append((e // SUBLANES, k))
    return taps


def _bwd1_body(n_tiles, dh_ref, proj_ref, h1_ref, z_ref, ob_ref, dpool_ref, wout_ref, pww_ref, poolw_ref, pools_ref, dw_ref,
               lg_ref, lb_ref,
               dproj_ref, gwout_ref, gpww_ref, gpoolw_ref, gvec_ref, gdw_ref,
               vext, gext, gsh, y_s):
    b = pl.program_id(0)
    i = pl.program_id(1)
    tm = TILE_M
    t0 = (n_tiles - 1 - i) * tm

    @pl.when(i == 0)
    def _():
        vext[tm:, :] = jnp.zeros((POOL_HALO, W_MIX), F32)
        gext[tm:, :] = jnp.zeros((CONV_HALO, W_MIX), F32)

    @pl.when((b == 0) & (i == 0))
    def _():
        gwout_ref[...] = jnp.zeros_like(gwout_ref)
        gpww_ref[...] = jnp.zeros_like(gpww_ref)
        gpoolw_ref[...] = jnp.zeros_like(gpoolw_ref)
        gvec_ref[...] = jnp.zeros_like(gvec_ref)
        gdw_ref[...] = jnp.zeros_like(gdw_ref)

    sub = tm // BWD_SUBTILES

    def dense_phase(r0):
        rs = slice(r0, r0 + sub)
        dhb = dh_ref[0, rs, :].astype(BF16)
        dy_a = lax.dot_general(dhb, wout_ref[0:W_MIX, :], NT_DIMS, preferred_element_type=F32)
        dy_b = lax.dot_general(dhb, wout_ref[W_MIX:, :], NT_DIMS, preferred_element_type=F32)

        a_gate = proj_ref[0, rs, W_MIX:2 * W_MIX]
        sig_a = jax.nn.sigmoid(a_gate)
        silu_a = a_gate * sig_a
        z = z_ref[0, rs, :]
        out_a = z * pools_ref[...]
        y_s[rs, 0:W_MIX] = (out_a * silu_a).astype(BF16)
        d_out_a = dy_a * silu_a
        d_ag = dy_a * out_a * (sig_a * (1.0 + a_gate * (1.0 - sig_a)))
        dproj_ref[0, rs, W_MIX:2 * W_MIX] = d_ag.astype(BF16)
        gvec_ref[V_POOL_SCALE] += _colsum8(d_out_a * z)
        dz = d_out_a * pools_ref[...]
        gvec_ref[V_POOL_B] += _colsum8(dz)
        t = _row_index((sub, POOL_GC), t0 + r0)
        n_ext = sub + POOL_HALO
        for g, w in enumerate(POOL_WINDOWS):
            lanes = slice(g * POOL_GC, (g + 1) * POOL_GC)
            dzg = dz[:, lanes].astype(BF16)
            gpoolw_ref[g] += lax.dot_general(dpool_ref[0, rs, lanes], dzg, TN_DIMS, preferred_element_type=F32)
            dd = lax.dot_general(dzg, poolw_ref[g].astype(BF16), NT_DIMS, preferred_element_type=F32)
            vext[rs, lanes] = dd * (1.0 / jnp.minimum(t + 1, w).astype(F32))
            s = vext[r0:r0 + n_ext, lanes]
            sh = 1
            while sh < w:
                s = s + pltpu.roll(s, n_ext - sh, 0)
                sh *= 2
            dproj_ref[0, rs, lanes] = (s[0:sub] - dd).astype(BF16)

        b_gate = proj_ref[0, rs, 4 * W_MIX:5 * W_MIX]
        sig_b = jax.nn.sigmoid(b_gate)
        silu_b = b_gate * sig_b
        ob = ob_ref[0, rs, :]
        y_s[rs, W_MIX:] = (ob * silu_b).astype(BF16)
        d_ob = dy_b * silu_b
        dproj_ref[0, rs, 4 * W_MIX:5 * W_MIX] = (dy_b * ob * (sig_b * (1.0 + b_gate * (1.0 - sig_b)))).astype(BF16)
        gvec_ref[V_PW_B] += _colsum8(d_ob)
        gwout_ref[...] += lax.dot_general(y_s[rs, :], dhb, TN_DIMS, preferred_element_type=F32)

        h1 = h1_ref[0, rs, :]
        mu = jnp.mean(h1, axis=-1, keepdims=True)
        xc = h1 - mu
        rstd = lax.rsqrt(jnp.mean(xc * xc, axis=-1, keepdims=True) + LN_EPS)
        xhat = xc * rstd
        h2 = xhat * lg_ref[...] + lb_ref[...]
        sig2 = jax.nn.sigmoid(h2)
        d_obb = d_ob.astype(BF16)
        gpww_ref[...] += lax.dot_general((h2 * sig2).astype(BF16), d_obb, TN_DIMS, preferred_element_type=F32)
        dh2 = lax.dot_general(d_obb, pww_ref[...], NT_DIMS, preferred_element_type=F32) * (sig2 * (1.0 + h2 * (1.0 - sig2)))
        gvec_ref[V_LN_G] += _colsum8(dh2 * xhat)
        gvec_ref[V_LN_B] += _colsum8(dh2)
        dxh = dh2 * lg_ref[...]
        dh1 = rstd * (dxh - jnp.mean(dxh, axis=-1, keepdims=True) - xhat * jnp.mean(dxh * xhat, axis=-1, keepdims=True))
        gvec_ref[V_CONV_B] += _colsum8(dh1)
        gext[rs, :] = dh1

    taps = _bwd_taps()
    cr = BWD_CONV_ROWS
    win = cr + CONV_HALO - SUBLANES

    def conv_phase(r0):
        for c0 in range(r0, r0 + sub, cr):
            for l0 in range(0, W_MIX, LANES):
                lanes = slice(l0, l0 + LANES)
                b_val = proj_ref[0, c0:c0 + cr, 2 * W_MIX + l0:2 * W_MIX + l0 + LANES]
                sg = jax.nn.sigmoid(proj_ref[0, c0:c0 + cr, 3 * W_MIX + l0:3 * W_MIX + l0 + LANES])
                h0c = b_val * sg
                acc = jnp.zeros((cr, LANES), F32)
                for s in range(SUBLANES):
                    gsh[...] = gext[c0 + s:c0 + s + win, lanes]
                    for q, k in taps[s]:
                        gq = gsh[SUBLANES * q:SUBLANES * q + cr, :]
                        acc = acc + gq * dw_ref[k:k + 1, lanes]
                        gdw_ref[k, :, lanes] += _colsum8(h0c * gq)
                d_bval = acc * sg
                dproj_ref[0, c0:c0 + cr, 2 * W_MIX + l0:2 * W_MIX + l0 + LANES] = d_bval.astype(BF16)
                dproj_ref[0, c0:c0 + cr, 3 * W_MIX + l0:3 * W_MIX + l0 + LANES] = (d_bval * b_val * (1.0 - sg)).astype(BF16)

    starts = [k * sub for k in reversed(range(BWD_SUBTILES))]
    for r0 in starts:
        dense_phase(r0)
    for r0 in starts:
        conv_phase(r0)
    vext[tm:, :] = vext[0:POOL_HALO, :]
    gext[tm:, :] = gext[0:CONV_HALO, :]


def _bwd1(dh, proj, h1, z, ob, dpool, w_out_b, pw_w_b, pool_w, pool_scale, conv_dw_f, ln_g, ln_b):
    bl, s, _ = dh.shape
    tm = TILE_M
    nt = s // tm
    tok = lambda n: pl.BlockSpec((1, tm, n), lambda b, i: (b, nt - 1 - i, 0))
    res = lambda shape: pl.BlockSpec(shape, lambda b, i: (0,) * len(shape))
    out_shapes = [(D_MODEL, D_MODEL), (W_MIX, W_MIX), (len(POOL_WINDOWS), POOL_GC, POOL_GC), (6, SUBLANES, W_MIX),
                  (CONV_WIDTH, SUBLANES, W_MIX)]
    return pl.pallas_call(
        functools.partial(_bwd1_body, nt),
        name="bwd1",
        grid=(bl, nt),
        in_specs=[tok(D_MODEL), tok(D_IN), tok(W_MIX), tok(W_MIX), tok(W_MIX), tok(W_MIX)] + [VMEM] * 7,
        out_specs=[tok(D_IN)] + [res(sh) for sh in out_shapes],
        out_shape=[jax.ShapeDtypeStruct((bl, s, D_IN), BF16)] + [jax.ShapeDtypeStruct(sh, F32) for sh in out_shapes],
        scratch_shapes=[pltpu.VMEM((tm + POOL_HALO, W_MIX), F32), pltpu.VMEM((tm + CONV_HALO, W_MIX), F32),
                        pltpu.VMEM((BWD_CONV_ROWS + CONV_HALO - SUBLANES, LANES), F32), pltpu.VMEM((tm, D_MODEL), BF16)],
        compiler_params=pltpu.CompilerParams(dimension_semantics=("arbitrary", "arbitrary"), vmem_limit_bytes=VMEM_LIMIT),
    )(dh, proj, h1, z, ob, dpool, w_out_b, pw_w_b, pool_w, _row(pool_scale), conv_dw_f, _row(ln_g), _row(ln_b))


MESH = pl.DeviceIdType.MESH
W_IN_BLK = D_IN // N_DEV
W_OUT_BLK = D_MODEL // N_DEV
PW_BLK = W_MIX // N_DEV
DW_BLK = W_MIX // N_DEV
N_PEER = N_DEV - 1


SLAB = 2 * W_IN_BLK
N_SLABS = 4
F_SIB, F_X, F_Y, F_ON, F_PASS_X, F_PASS_Y, F_PASS_D = range(7)
W_IN_ONLY, OTHERS = (0,), (1, 2, 3)


def _inproj_body(n_tiles, x_ref, ng_ref, wint_ref, wout_ref, pww_ref, dw_ref,
                 proj_hbm, wt_ref, woutf_ref, pwwf_ref, dwf_ref,
                 hn_s, wt_s, wout_s, pww_s, dw_blk, stage, out_sems, send_sems, recv_sems):
    s = pl.program_id(0)
    i = pl.program_id(1)
    tm = TILE_IN
    x, y, c = lax.axis_index("x"), lax.axis_index("y"), lax.axis_index("c")
    sibling = (x, y, 1 - c)
    xn, yn, dg = (1 - x, y), (x, 1 - y), (1 - x, 1 - y)
    pick = lambda a, b: tuple(jnp.where(c == 1, p, q) for p, q in zip(a, b))
    on_from, on_to = pick(xn, yn), pick(yn, xn)

    def blocks(px, py, pc):
        k = 4 * px + 2 * py + pc
        rows = lambda ref, n: ref.at[pl.ds(pl.multiple_of(k * n, n), n), :]
        return (rows(wt_s, W_IN_BLK), rows(wout_s, W_OUT_BLK), rows(pww_s, PW_BLK), dw_blk.at[k])

    def copies(arrays, flow, block, to):
        refs = blocks(*block)
        return [pltpu.make_async_remote_copy(src_ref=refs[a], dst_ref=refs[a], send_sem=send_sems.at[a, flow],
                                             recv_sem=recv_sems.at[a, flow], device_id=to, device_id_type=MESH)
                for a in arrays]

    def start(arrays):
        for flow, to in ((F_SIB, sibling), (F_X, (*xn, c)), (F_Y, (*yn, c))):
            for cp in copies(arrays, flow, (x, y, c), to):
                cp.start()

    def direct(arrays):
        for flow, chip in ((F_X, xn), (F_Y, yn)):
            for cp in copies(arrays, flow, (*chip, c), sibling):
                cp.wait_recv()
        for cp in copies(arrays, F_ON, (*on_from, c), (*on_to, c)):
            cp.start()
        for flow, chip in ((F_PASS_X, xn), (F_PASS_Y, yn)):
            for cp in copies(arrays, flow, (*chip, c), sibling):
                cp.start()

    def diagonal(arrays):
        for cp in copies(arrays, F_ON, (*dg, c), sibling):
            cp.wait_recv()
        for cp in copies(arrays, F_PASS_D, (*dg, c), sibling):
            cp.start()

    def from_sibling(arrays, flow, chip):
        for cp in copies(arrays, flow, (*chip, 1 - c), sibling):
            cp.wait_recv()

    def finish(arrays):
        mine = (x, y, c)
        for flow, block in ((F_SIB, mine), (F_X, mine), (F_Y, mine), (F_ON, (*on_from, c)), (F_PASS_X, (*xn, c)),
                            (F_PASS_Y, (*yn, c)), (F_PASS_D, (*dg, c))):
            for cp in copies(arrays, flow, block, sibling):
                cp.wait_send()

    step = s * n_tiles + i
    par = step % 2
    first = step == 0

    @pl.when(first)
    def _():
        mine = blocks(x, y, c)
        mine[0][...] = wint_ref[...].astype(BF16)
        mine[1][...] = wout_ref[...].astype(BF16)
        mine[2][...] = pww_ref[...].astype(BF16)
        mine[3][...] = dw_ref[...]
        start(W_IN_ONLY)
        start(OTHERS)

    rows_i = pl.ds(pl.multiple_of(i * tm, tm), tm)

    @pl.when(s == 0)
    def _():
        xt = x_ref[0]
        r1 = lax.rsqrt(jnp.mean(xt * xt, axis=-1, keepdims=True) + RMS_EPS)
        hn_s[rows_i, :] = (xt * r1 * ng_ref[...]).astype(BF16)

    @pl.when(first)
    def _():
        from_sibling(W_IN_ONLY, F_SIB, (x, y))

    @pl.when((s == 1) & (i == 0))
    def _():
        direct(W_IN_ONLY)
        from_sibling(W_IN_ONLY, F_PASS_Y, yn)

    @pl.when((s == 2) & (i == 0))
    def _():
        from_sibling(W_IN_ONLY, F_PASS_X, xn)

    @pl.when((s == 3) & (i == 0))
    def _():
        diagonal(W_IN_ONLY)
        direct(OTHERS)
        from_sibling(W_IN_ONLY, F_PASS_D, dg)

    tiles_per_seq = proj_hbm.shape[1] // tm

    def out_copy(parity, col0):
        dst = proj_hbm.at[i // tiles_per_seq, pl.ds(pl.multiple_of((i % tiles_per_seq) * tm, tm), tm),
                          pl.ds(pl.multiple_of(col0, LANES), SLAB)]
        return pltpu.make_async_copy(stage.at[parity], dst, out_sems.at[parity])

    @pl.when(step >= 2)
    def _():
        out_copy(par, 0).wait()

    chip = jnp.where(s < 2, 2 * x, 2 * (1 - x)) + jnp.where((s == 0) | (s == 2), y, 1 - y)
    slab = wt_s[pl.ds(pl.multiple_of(chip * SLAB, SLAB), SLAB), :]
    stage[par] = lax.dot_general(hn_s[rows_i, :], slab, NT_DIMS, preferred_element_type=F32)
    out_copy(par, chip * SLAB).start()

    @pl.when(step == N_SLABS * n_tiles - 1)
    def _():
        out_copy(par, 0).wait()
        out_copy(1 - par, 0).wait()
        from_sibling(OTHERS, F_SIB, (x, y))
        from_sibling(OTHERS, F_PASS_X, xn)
        from_sibling(OTHERS, F_PASS_Y, yn)
        diagonal(OTHERS)
        from_sibling(OTHERS, F_PASS_D, dg)
        finish(W_IN_ONLY)
        finish(OTHERS)
        wt_ref[...] = wt_s[...]
        woutf_ref[...] = wout_s[...]
        pwwf_ref[...] = pww_s[...]
        for k in range(N_DEV):
            dwf_ref[:, k * DW_BLK:(k + 1) * DW_BLK] = dw_blk[k]


def _inproj(x, norm_g, w_in_t, w_out, pw_w, conv_dw):
    bl, s, _ = x.shape
    tm = TILE_IN
    nt = s // tm
    n_tiles = bl * nt

    def x_map(phase, i):
        i = jnp.where(phase == 0, i, n_tiles - 1)
        return (i // nt, i % nt, 0)

    dma = pltpu.SemaphoreType.DMA
    return pl.pallas_call(
        functools.partial(_inproj_body, n_tiles),
        name="inproj",
        grid=(N_SLABS, n_tiles),
        in_specs=[pl.BlockSpec((1, tm, D_MODEL), x_map)] + [VMEM] * 5,
        out_specs=[pl.BlockSpec(memory_space=pl.ANY)] + [VMEM] * 4,
        out_shape=[jax.ShapeDtypeStruct((bl, s, D_IN), F32), jax.ShapeDtypeStruct((D_IN, D_MODEL), BF16),
                   jax.ShapeDtypeStruct((D_MODEL, D_MODEL), BF16), jax.ShapeDtypeStruct((W_MIX, W_MIX), BF16),
                   jax.ShapeDtypeStruct((CONV_WIDTH, W_MIX), F32)],
        scratch_shapes=[pltpu.VMEM((n_tiles * tm, D_MODEL), BF16), pltpu.VMEM((D_IN, D_MODEL), BF16),
                        pltpu.VMEM((D_MODEL, D_MODEL), BF16), pltpu.VMEM((W_MIX, W_MIX), BF16),
                        pltpu.VMEM((N_DEV, CONV_WIDTH, DW_BLK), F32), pltpu.VMEM((2, tm, SLAB), F32),
                        dma((2,)), dma((4, 7)), dma((4, 7))],
        compiler_params=pltpu.CompilerParams(dimension_semantics=("arbitrary", "arbitrary"), vmem_limit_bytes=VMEM_LIMIT),
    )(x, _row(norm_g), w_in_t, w_out, pw_w, conv_dw)


B_PLAIN, B_ON, B_MERGED = range(3)


def _routes():
    x, y, c = lax.axis_index("x"), lax.axis_index("y"), lax.axis_index("c")
    xn, yn = (1 - x, y), (x, 1 - y)
    pick = lambda a, b: tuple(jnp.where(c == 1, p, q) for p, q in zip(a, b))
    return pick(xn, yn), pick(yn, xn), (1 - x, 1 - y), (x, y), c


def _chip_index(chip):
    return 2 * chip[0] + chip[1]


def _copy(src, dst, send_sems, recv_sems, a, k, to):
    return pltpu.make_async_remote_copy(src_ref=src, dst_ref=dst, send_sem=send_sems.at[a, k], recv_sem=recv_sems.at[a, k],
                                        device_id=to, device_id_type=MESH)


def _rsb_send(plain, on, rb_plain, rb_on, send_b, recv_b):
    first, _, _, _, c = _routes()
    for k, srcs, dsts in ((B_ON, on, rb_on), (B_PLAIN, plain, rb_plain)):
        for a, (s, d) in enumerate(zip(srcs, dsts)):
            _copy(s, d, send_b, recv_b, a, k, (*first, c)).start()


def _rsb_merge(second_sums, merged, rb_on, rb_merged, send_b, recv_b):
    _, second, _, _, c = _routes()
    for a, (own, m, got, dst) in enumerate(zip(second_sums, merged, rb_on, rb_merged)):
        _copy(got, got, send_b, recv_b, a, B_ON, (*second, c)).wait_recv()
        m[...] = (own[...] + got[...].astype(F32)).astype(BF16)
        _copy(m, dst, send_b, recv_b, a, B_MERGED, (*second, c)).start()


def _rsb_total(own_sums, plain, on, merged, rb_plain, rb_merged, send_b, recv_b):
    _, second, _, _, c = _routes()
    out = []
    for a, (own, p, g) in enumerate(zip(own_sums, rb_plain, rb_merged)):
        _copy(p, p, send_b, recv_b, a, B_PLAIN, (*second, c)).wait_recv()
        _copy(g, g, send_b, recv_b, a, B_MERGED, (*second, c)).wait_recv()
        out.append(own[...] + p[...].astype(F32) + g[...].astype(F32))
    for k, srcs in ((B_PLAIN, plain), (B_ON, on), (B_MERGED, merged)):
        for a, s in enumerate(srcs):
            _copy(s, s, send_b, recv_b, a, k, (*second, c)).wait_send()
    return out


R_LOSS, R_FINAL_G, R_POOL, R_CONV, R_LN = range(5)
VEC_ROWS = 16
DW_ROWS = 32
A_PART, A_X, A_Y, A_ON, A_HALF = range(5)
STEP_SUMS, STEP_MERGE, STEP_HALF = 1, 3, 5


def _bwd2a_body(n_steps, x_ref, dproj_ref, ng_ref, gwout_ref, gpww_ref, loss_ref, gfg_ref, gvec_ref, gdw_ref, gpoolw_ref,
                win_own_ref, win_staged_ref, owout_ref, opww_ref, ovec_ref, odw_ref, opoolw_ref,
                acc, st_win, a_win, a_wout, a_pww, st_wout, st_pww, rb_wout, rb_pww, part_vec, part_dw, part_pw,
                sib_vec, sib_dw, sib_pw, res_vec, res_dw, res_pw, half_vec, half_dw, half_pw,
                send_a, recv_a, send_b, recv_b, send_r, recv_r, send_w, recv_w):
    step = pl.program_id(0) * pl.num_programs(1) + pl.program_id(1)
    x, y, c = lax.axis_index("x"), lax.axis_index("y"), lax.axis_index("c")
    sibling = (x, y, 1 - c)
    xn, yn = (1 - x, y), (x, 1 - y)
    first, second, diag, own, _ = _routes()
    rows = lambda ref, k, n: ref.at[pl.ds(pl.multiple_of(k * n, n), n), :]

    blocks = lambda k: (rows(gwout_ref, k, W_OUT_BLK), rows(gpww_ref, k, PW_BLK))
    landed = lambda chip: (a_wout.at[chip], a_pww.at[chip])
    staged = lambda k: (st_wout.at[k], st_pww.at[k])
    arrived = lambda k: (rb_wout.at[k], rb_pww.at[k])

    parts, sibs, halves = (part_vec, part_dw, part_pw), (sib_vec, sib_dw, sib_pw), (half_vec, half_dw, half_pw)
    results = (res_vec, res_dw, res_pw)
    half_rows = lambda ref: pl.ds(pl.multiple_of(c * (ref.shape[0] // 2), SUBLANES), ref.shape[0] // 2)
    slot = lambda chip: tuple(h.at[_chip_index(chip)] for h in halves)

    @pl.when(step == 0)
    def _():
        acc[...] = jnp.zeros_like(acc)
        sum8 = lambda v: jnp.sum(v, axis=0, keepdims=True)
        part_vec[...] = jnp.zeros((VEC_ROWS, D_MODEL), F32)
        part_vec[R_LOSS:R_LOSS + 1, :] = jnp.broadcast_to(loss_ref[0:1, 0:1], (1, D_MODEL))
        part_vec[R_FINAL_G:R_FINAL_G + 1, :] = sum8(gfg_ref[...])
        for row, (lo, hi) in ((R_POOL, (V_POOL_B, V_POOL_SCALE)), (R_CONV, (V_CONV_B, V_LN_G)), (R_LN, (V_LN_B, V_PW_B))):
            part_vec[row:row + 1, 0:W_MIX] = sum8(gvec_ref[lo])
            part_vec[row:row + 1, W_MIX:] = sum8(gvec_ref[hi])
        part_dw[0:CONV_WIDTH, :] = jnp.sum(gdw_ref[...], axis=1)
        part_dw[CONV_WIDTH:, :] = jnp.zeros((DW_ROWS - CONV_WIDTH, W_MIX), F32)
        part_pw[...] = gpoolw_ref[...].reshape(len(POOL_WINDOWS) * POOL_GC, POOL_GC)
        for a, (p, s) in enumerate(zip(parts, sibs)):
            _copy(p, s, send_r, recv_r, a, A_PART, sibling).start()
        for chip in range(4):
            for a, (s, d) in enumerate(zip(blocks(2 * chip + (1 - c)), landed(chip))):
                _copy(s, d, send_a, recv_a, a, chip, sibling).start()

    x_t = x_ref[0]
    r1 = lax.rsqrt(jnp.mean(x_t * x_t, axis=-1, keepdims=True) + RMS_EPS)
    hn = (x_t * r1 * ng_ref[...]).astype(BF16)
    acc[...] += lax.dot_general(dproj_ref[0], hn, TN_DIMS, preferred_element_type=F32)

    @pl.when(step == STEP_SUMS)
    def _():
        for a, (p, s, h) in enumerate(zip(parts, sibs, slot(own))):
            _copy(p, s, send_r, recv_r, a, A_PART, sibling).wait_recv()
            h[...] = p[half_rows(p), :] + s[half_rows(s), :]
            for k, to in ((A_X, xn), (A_Y, yn)):
                _copy(h, h, send_r, recv_r, a, k, (*to, c)).start()
        for chip in range(4):
            for a, (m, l) in enumerate(zip(blocks(2 * chip + c), landed(chip))):
                _copy(l, l, send_a, recv_a, a, chip, sibling).wait_recv()
                l[...] = m[...] + l[...]
        for k, chip in ((B_ON, diag), (B_PLAIN, first)):
            for s, l in zip(staged(k), landed(_chip_index(chip))):
                s[...] = l[...].astype(BF16)
        _rsb_send(staged(B_PLAIN), staged(B_ON), arrived(B_PLAIN), arrived(B_ON), send_b, recv_b)

    @pl.when(step == STEP_MERGE)
    def _():
        _rsb_merge(landed(_chip_index(second)), staged(B_MERGED), arrived(B_ON), arrived(B_MERGED), send_b, recv_b)
        for a in range(3):
            for k, chip in ((A_X, xn), (A_Y, yn)):
                h = slot(chip)[a]
                _copy(h, h, send_r, recv_r, a, k, sibling).wait_recv()
            h = slot(first)[a]
            _copy(h, h, send_r, recv_r, a, A_ON, (*second, c)).start()

    @pl.when(step == STEP_HALF)
    def _():
        for a, (r, hs) in enumerate(zip(results, halves)):
            h = slot(diag)[a]
            _copy(h, h, send_r, recv_r, a, A_ON, sibling).wait_recv()
            mine = r.at[half_rows(r), :]
            mine[...] = (hs[0] + hs[1]) + (hs[2] + hs[3])
            _copy(mine, mine, send_r, recv_r, a, A_HALF, sibling).start()

    @pl.when(step == n_steps - 1)
    def _():
        block = lambda k: rows(acc, k, W_IN_BLK)
        w_copy = lambda chip: _copy(st_win.at[chip], a_win.at[chip], send_w, recv_w, 0, chip, sibling)
        for chip in range(4):
            st_win[chip] = block(2 * chip + (1 - c))[...].astype(BF16)
            w_copy(chip).start()

        owout_ref[...], opww_ref[...] = _rsb_total(landed(_chip_index(own)), staged(B_PLAIN), staged(B_ON), staged(B_MERGED),
                                                   arrived(B_PLAIN), arrived(B_MERGED), send_b, recv_b)
        for a, (r, p, s) in enumerate(zip(results, parts, sibs)):
            theirs = r.at[pl.ds(pl.multiple_of((1 - c) * (r.shape[0] // 2), SUBLANES), r.shape[0] // 2), :]
            _copy(theirs, theirs, send_r, recv_r, a, A_HALF, sibling).wait_recv()
            _copy(p, s, send_r, recv_r, a, A_PART, sibling).wait_send()
            for k, h in ((A_X, slot(own)[a]), (A_Y, slot(own)[a]), (A_ON, slot(first)[a])):
                _copy(h, h, send_r, recv_r, a, k, sibling).wait_send()
            mine = r.at[half_rows(r), :]
            _copy(mine, mine, send_r, recv_r, a, A_HALF, sibling).wait_send()
        for chip in range(4):
            for a, (s, d) in enumerate(zip(blocks(2 * chip + (1 - c)), landed(chip))):
                _copy(s, d, send_a, recv_a, a, chip, sibling).wait_send()
        ovec_ref[...] = res_vec[...]
        opoolw_ref[...] = res_pw[...]
        for k in range(N_DEV):
            @pl.when(4 * x + 2 * y + c == k)
            def _():
                odw_ref[...] = res_dw[0:CONV_WIDTH, k * DW_BLK:(k + 1) * DW_BLK]

        for chip in range(4):
            w_copy(chip).wait()
        chip_sum = lambda chip: block(2 * _chip_index(chip) + c)[...] + a_win[_chip_index(chip)].astype(F32)
        win_own_ref[...] = chip_sum(own)
        for j, chip in enumerate((xn, yn, diag)):
            win_staged_ref[j] = chip_sum(chip).astype(BF16)


def _bwd2a(x, dproj, norm_g, gwout, gpww, loss8, gfg8, gvec, gdw8, gpoolw):
    bl, s, _ = x.shape
    tm = TILE_M
    nt = s // tm
    tok = lambda n: pl.BlockSpec((1, tm, n), lambda b, i: (b, i, 0))
    n_pw = len(POOL_WINDOWS) * POOL_GC
    win = (W_IN_BLK, D_MODEL)
    blk = ((W_OUT_BLK, D_MODEL), (PW_BLK, W_MIX))
    rep = ((VEC_ROWS, D_MODEL), (DW_ROWS, W_MIX), (n_pw, POOL_GC))
    half = lambda sh: (sh[0] // 2,) + sh[1:]
    dma = pltpu.SemaphoreType.DMA
    assert bl * nt > STEP_HALF + 1, "the exchanges' phases need their grid steps"
    return pl.pallas_call(
        functools.partial(_bwd2a_body, bl * nt),
        name="bwd2a",
        grid=(bl, nt),
        in_specs=[tok(D_MODEL), tok(D_IN)] + [VMEM] * 8,
        out_specs=[VMEM] * 7,
        out_shape=[jax.ShapeDtypeStruct(win, F32), jax.ShapeDtypeStruct((3,) + win, BF16)]
        + [jax.ShapeDtypeStruct(sh, F32) for sh in blk + (rep[0], (CONV_WIDTH, DW_BLK), rep[2])],
        scratch_shapes=[pltpu.VMEM((D_IN, D_MODEL), F32), pltpu.VMEM((4,) + win, BF16), pltpu.VMEM((4,) + win, BF16)]
        + [pltpu.VMEM((4,) + sh, F32) for sh in blk]
        + [pltpu.VMEM((3,) + sh, BF16) for sh in blk] * 2
        + [pltpu.VMEM(sh, F32) for sh in rep] * 3
        + [pltpu.VMEM((4,) + half(sh), F32) for sh in rep]
        + [dma((2, 4)), dma((2, 4)), dma((2, 3)), dma((2, 3)), dma((3, 5)), dma((3, 5)), dma((1, 4)), dma((1, 4))],
        compiler_params=pltpu.CompilerParams(dimension_semantics=("arbitrary", "arbitrary"), vmem_limit_bytes=VMEM_LIMIT),
    )(x, dproj, _row(norm_g), gwout, gpww, loss8, gfg8, gvec, gdw8, gpoolw)


HBM = pl.BlockSpec(memory_space=pltpu.HBM)
SEM = pl.BlockSpec(memory_space=pltpu.SEMAPHORE)
EFFECT = pltpu.SideEffectType.DATAFLOW_SIDE_EFFECTING


def _owner_copies(src_ref, land_ref, send_sems, recv_sems):
    x, y, c = lax.axis_index("x"), lax.axis_index("y"), lax.axis_index("c")
    owners = ((1 - x, y, c), (x, 1 - y, c), (1 - x, 1 - y, c))
    return [pltpu.make_async_remote_copy(src_ref=src_ref.at[j], dst_ref=land_ref.at[j], send_sem=send_sems.at[j],
                                         recv_sem=recv_sems.at[j], device_id=owners[j], device_id_type=MESH)
            for j in range(3)]


def _rs_start(staged):
    def body(src_ref, land_ref, send_sems, recv_sems, src_thru, land_thru, token):
        for cp in _owner_copies(src_ref, land_ref, send_sems, recv_sems):
            cp.start()
        token[...] = jnp.zeros_like(token)

    dma = pltpu.SemaphoreType.DMA
    return pl.pallas_call(
        body, name="rs_start",
        out_shape=(dma((3,)), dma((3,)), pltpu.HBM(staged.shape, staged.dtype), pltpu.HBM(staged.shape, staged.dtype),
                   jax.ShapeDtypeStruct((SUBLANES, LANES), F32)),
        in_specs=(HBM, HBM), out_specs=(SEM, SEM, HBM, HBM, VMEM), input_output_aliases={0: 2, 1: 3},
        compiler_params=pltpu.CompilerParams(has_side_effects=EFFECT),
    )(pltpu.with_memory_space_constraint(staged, pltpu.HBM),
      pltpu.with_memory_space_constraint(lax.empty(staged.shape, staged.dtype), pltpu.HBM))


def _rs_wait(send_sems, recv_sems, src_thru, land_thru, after):
    def body(src_ref, land_ref, send_sems, recv_sems, after_ref, src_dead, got_ref):
        for cp in _owner_copies(src_ref, land_ref, send_sems, recv_sems):
            cp.wait_send()
            cp.wait_recv()

    return pl.pallas_call(
        body, name="rs_wait",
        out_shape=(pltpu.HBM(src_thru.shape, src_thru.dtype), pltpu.HBM(land_thru.shape, land_thru.dtype)),
        in_specs=(HBM, HBM, SEM, SEM, pl.BlockSpec(memory_space=pl.ANY)), out_specs=(HBM, HBM),
        input_output_aliases={0: 0, 1: 1},
        compiler_params=pltpu.CompilerParams(has_side_effects=EFFECT),
    )(src_thru, land_thru, send_sems, recv_sems, after)[1]


def _bwd2b_body(x_ref, dh_ref, dproj_ref, ng_ref, wint_ref, token_ref, gx_ref, gng_ref):
    @pl.when((pl.program_id(0) == 0) & (pl.program_id(1) == 0))
    def _():
        gng_ref[...] = jnp.zeros_like(gng_ref) + token_ref[0:1, 0:1]

    x_t = x_ref[0]
    r1 = lax.rsqrt(jnp.mean(x_t * x_t, axis=-1, keepdims=True) + RMS_EPS)
    xr = x_t * r1
    dhn = jnp.dot(dproj_ref[0], wint_ref[...], preferred_element_type=F32)
    gng_ref[...] += _colsum8(dhn * xr)
    gy = dhn * ng_ref[...]
    gx_ref[0] = dh_ref[0] + r1 * (gy - xr * jnp.mean(gy * xr, axis=-1, keepdims=True))


def _bwd2b(x, dh, dproj, norm_g, w_in_t, token):
    bl, s, _ = x.shape
    tm = TILE_M
    tok = lambda n: pl.BlockSpec((1, tm, n), lambda b, i: (b, i, 0))
    return pl.pallas_call(
        functools.partial(_bwd2b_body),
        name="bwd2b",
        grid=(bl, s // tm),
        in_specs=[tok(D_MODEL), tok(D_MODEL), tok(D_IN), VMEM, VMEM, VMEM],
        out_specs=[tok(D_MODEL), pl.BlockSpec((SUBLANES, D_MODEL), lambda b, i: (0, 0))],
        out_shape=[jax.ShapeDtypeStruct((bl, s, D_MODEL), F32), jax.ShapeDtypeStruct((SUBLANES, D_MODEL), F32)],
        compiler_params=pltpu.CompilerParams(dimension_semantics=("arbitrary", "arbitrary"), vmem_limit_bytes=VMEM_LIMIT),
    )(x, dh, dproj, _row(norm_g), w_in_t, token)


def _partial_copies(part_ref, land_ref, send_sems, recv_sems):
    x, y, c = lax.axis_index("x"), lax.axis_index("y"), lax.axis_index("c")
    other_chips = [(1 - x, y), (x, 1 - y), (1 - x, 1 - y)]
    peers = [(x, y, 1 - c)] + [(*chip, c) for chip in other_chips] + [(*chip, 1 - c) for chip in other_chips]
    return [pltpu.make_async_remote_copy(src_ref=part_ref, dst_ref=land_ref.at[4 * x + 2 * y + c], send_sem=send_sems.at[r],
                                         recv_sem=recv_sems.at[r], device_id=peers[r], device_id_type=MESH)
            for r in range(N_PEER)]


def _partials_start(part):
    def body(part_ref, land_ref, send_sems, recv_sems, part_thru, land_thru):
        for cp in _partial_copies(part_ref, land_ref, send_sems, recv_sems):
            cp.start()

    dma = pltpu.SemaphoreType.DMA
    land = (N_DEV,) + part.shape
    return pl.pallas_call(
        body, name="partials_start",
        out_shape=(dma((N_PEER,)), dma((N_PEER,)), pltpu.HBM(part.shape, part.dtype), pltpu.HBM(land, part.dtype)),
        in_specs=(HBM, HBM), out_specs=(SEM, SEM, HBM, HBM), input_output_aliases={0: 2, 1: 3},
        compiler_params=pltpu.CompilerParams(has_side_effects=EFFECT),
    )(pltpu.with_memory_space_constraint(part, pltpu.HBM),
      pltpu.with_memory_space_constraint(lax.empty(land, part.dtype), pltpu.HBM))


def _partials_wait(send_sems, recv_sems, part_thru, land_thru):
    def body(part_ref, land_ref, send_sems, recv_sems, part_out, land_out):
        for cp in _partial_copies(part_ref, land_ref, send_sems, recv_sems):
            cp.wait_send()
            cp.wait_recv()

    return pl.pallas_call(
        body, name="partials_wait",
        out_shape=(pltpu.HBM(part_thru.shape, part_thru.dtype), pltpu.HBM(land_thru.shape, land_thru.dtype)),
        in_specs=(HBM, HBM, SEM, SEM), out_specs=(HBM, HBM), input_output_aliases={0: 0, 1: 1},
        compiler_params=pltpu.CompilerParams(has_side_effects=EFFECT),
    )(part_thru, land_thru, send_sems, recv_sems)


ADAM_LR = 0.001
ADAM_B1 = 0.9
ADAM_B2 = 0.999
ADAM_EPS = 1e-08
ADAM_WD = 0.01
ADAM_STEP = 10


PARAMS = ("norm_g", "w_in", "pool_w", "pool_b", "pool_scale", "conv_dw", "conv_b", "ln_g", "ln_b", "pw_w", "pw_b", "w_out",
          "final_g")


def _adamw_body(*refs):
    n = len(PARAMS)
    ws, ms, vs = (dict(zip(PARAMS, refs[k * n:(k + 1) * n])) for k in range(3))
    gwin_ref, gwin_others_ref, gwout_ref, gpww_ref, vec_ref, gdw_ref, gpoolw_ref, gng_ref, gng_others_ref = refs[3 * n:3 * n + 9]
    outs = refs[3 * n + 9:]
    gs, ds, m2s, v2s = (dict(zip(PARAMS, outs[k * n:(k + 1) * n])) for k in range(4))
    loss_ref = outs[4 * n]

    def update(name, g, at=slice(None)):
        m = ADAM_B1 * ms[name][at] + (1.0 - ADAM_B1) * g
        v = ADAM_B2 * vs[name][at] + (1.0 - ADAM_B2) * (g * g)
        m_hat = m / (1.0 - ADAM_B1 ** ADAM_STEP)
        v_hat = v / (1.0 - ADAM_B2 ** ADAM_STEP)
        gs[name][at] = g
        ds[name][at] = -ADAM_LR * (m_hat / (jnp.sqrt(v_hat) + ADAM_EPS) + ADAM_WD * ws[name][at])
        m2s[name][at] = m
        v2s[name][at] = v

    row = lambda r, lo, hi: vec_ref[r:r + 1, lo:hi]
    loss_ref[...] = row(R_LOSS, 0, 1)
    me = 4 * lax.axis_index("x") + 2 * lax.axis_index("y") + lax.axis_index("c")
    g_norm = jnp.zeros(gng_ref.shape, F32)
    for k in range(N_DEV):
        g_norm = g_norm + jnp.where(me == k, gng_ref[...], gng_others_ref[k])
    update("norm_g", jnp.sum(g_norm, axis=0, keepdims=True))
    g_w_in = gwin_ref[...]
    for j in range(3):
        g_w_in = g_w_in + gwin_others_ref[j].astype(F32)
    update("w_in", g_w_in)
    update("pool_w", gpoolw_ref[...].reshape(ws["pool_w"].shape))
    for g in range(len(POOL_WINDOWS)):
        update("pool_b", row(R_POOL, g * POOL_GC, (g + 1) * POOL_GC), at=slice(g, g + 1))
    update("pool_scale", row(R_POOL, W_MIX, 2 * W_MIX))
    update("conv_dw", gdw_ref[...])
    update("conv_b", row(R_CONV, 0, W_MIX))
    update("ln_g", row(R_CONV, W_MIX, 2 * W_MIX))
    update("ln_b", row(R_LN, 0, W_MIX))
    update("pw_w", gpww_ref[...])
    update("pw_b", row(R_LN, W_MIX, 2 * W_MIX))
    update("w_out", gwout_ref[...])
    update("final_g", row(R_FINAL_G, 0, D_MODEL))


def _adamw(ws, ms, vs, grad_sources):
    n = len(PARAMS)
    shapes = [ws[p].shape for p in PARAMS]
    flat = lambda a: a.reshape(1, -1) if a.ndim == 1 else a
    args = [flat(d[p]) for d in (ws, ms, vs) for p in PARAMS]
    outs = pl.pallas_call(
        functools.partial(_adamw_body),
        name="adamw",
        in_specs=[VMEM] * (3 * n + len(grad_sources)),
        out_specs=[VMEM] * (4 * n + 1),
        out_shape=[jax.ShapeDtypeStruct(a.shape, F32) for a in args[:n]] * 4 + [jax.ShapeDtypeStruct((1, 1), F32)],
        compiler_params=pltpu.CompilerParams(vmem_limit_bytes=VMEM_LIMIT),
    )(*args, *grad_sources)
    loss = outs[4 * n].reshape(())
    outs = [o.reshape(shapes[k % n]) for k, o in enumerate(outs[:4 * n])]
    return loss, outs[:n], outs[n:2 * n], outs[2 * n:3 * n], outs[3 * n:]


def kernel(x, norm_g, w_in, pool_w, pool_b, pool_scale, conv_dw, conv_b, ln_g, ln_b, pw_w, pw_b, w_out, final_g, loss_target, m_norm_g, m_w_in, m_pool_w, m_pool_b, m_pool_scale, m_conv_dw, m_conv_b, m_ln_g, m_ln_b, m_pw_w, m_pw_b, m_w_out, m_final_g, v_norm_g, v_w_in, v_pool_w, v_pool_b, v_pool_scale, v_conv_dw, v_conv_b, v_ln_g, v_ln_b, v_pw_w, v_pw_b, v_w_out, v_final_g):
    proj, w_in_t, w_out_b, pw_w_b, conv_dw_f = _inproj(x, norm_g, w_in.T, w_out, pw_w, conv_dw)
    h1, z, ob, dpool, dh, loss8, gfg8 = _fwd(x, loss_target, proj, pool_w, pool_b, pool_scale, conv_dw_f,
                                             conv_b, ln_g, ln_b, pw_w_b, pw_b, w_out_b, final_g)
    dproj, gwout, gpww, gpoolw, gvec, gdw8 = _bwd1(dh, proj, h1, z, ob, dpool, w_out_b, pw_w_b, pool_w, pool_scale,
                                                   conv_dw_f, ln_g, ln_b)
    win_own, win_staged, g_w_out, g_pw_w, vec, g_conv_dw, poolw_sum = _bwd2a(x, dproj, norm_g, gwout, gpww, loss8, gfg8, gvec,
                                                                             gdw8, gpoolw)
    send_sems, recv_sems, staged_thru, land_thru, token = _rs_start(win_staged)
    grad_x, gng8 = _bwd2b(x, dh, dproj, norm_g, w_in_t, token)
    ng_send, ng_recv, gng8, gng_land = _partials_start(gng8)
    win_others = _rs_wait(send_sems, recv_sems, staged_thru, land_thru, gng8)
    gng8, gng_others = _partials_wait(ng_send, ng_recv, gng8, gng_land)

    weights = dict(norm_g=norm_g, w_in=w_in.T, pool_w=pool_w, pool_b=pool_b, pool_scale=pool_scale, conv_dw=conv_dw, conv_b=conv_b,
                   ln_g=ln_g, ln_b=ln_b, pw_w=pw_w, pw_b=pw_b, w_out=w_out, final_g=final_g)
    m_in = dict(norm_g=m_norm_g, w_in=m_w_in.T, pool_w=m_pool_w, pool_b=m_pool_b, pool_scale=m_pool_scale, conv_dw=m_conv_dw,
                conv_b=m_conv_b, ln_g=m_ln_g, ln_b=m_ln_b, pw_w=m_pw_w, pw_b=m_pw_b, w_out=m_w_out, final_g=m_final_g)
    v_in = dict(norm_g=v_norm_g, w_in=v_w_in.T, pool_w=v_pool_w, pool_b=v_pool_b, pool_scale=v_pool_scale, conv_dw=v_conv_dw,
                conv_b=v_conv_b, ln_g=v_ln_g, ln_b=v_ln_b, pw_w=v_pw_w, pw_b=v_pw_b, w_out=v_w_out, final_g=v_final_g)
    loss, *groups = _adamw(weights, m_in, v_in, (win_own, win_others, g_w_out, g_pw_w, vec, g_conv_dw, poolw_sum, gng8,
                                                 gng_others))
    k_in = PARAMS.index("w_in")
    for group in groups:
        group[k_in] = group[k_in].T
    grads, delta, new_m, new_v = groups
    return (loss, grad_x, *grads, *delta, *new_m, *new_v)
```

```python
import functools

import jax
import jax.numpy as jnp
from jax import lax
from jax.experimental import pallas as pl
from jax.experimental.pallas import tpu as pltpu

F32 = jnp.float32
BF16 = jnp.bfloat16

D_MODEL = 1024
W_MIX = 512
D_IN = 5 * W_MIX
POOL_WINDOWS = (2, 4, 8, 16)
POOL_GC = 128
CONV_WIDTH = 31
RMS_EPS = 1e-6
LN_EPS = 1e-5
N_DEV = 8

LANES = 128
SUBLANES = 8
TILE_M = 512
POOL_HALO = 16
CONV_HALO = 32
CONV_ROWS = 128
BWD_CONV_ROWS = 64
BWD_SUBTILES = 2
VMEM_LIMIT = 56 * 1024 * 1024

VMEM = pl.BlockSpec(memory_space=pltpu.VMEM)


def _silu(v):
    return v * jax.nn.sigmoid(v)


def _colsum8(v):
    m, n = v.shape
    return jnp.sum(v.reshape(m // SUBLANES, SUBLANES, n), axis=0)


def _row_index(shape, t0):
    return lax.broadcasted_iota(jnp.int32, shape, 0) + t0


def _conv_taps():
    taps = {s: [] for s in range(SUBLANES)}
    for k in range(CONV_WIDTH):
        e = k + CONV_HALO - (CONV_WIDTH - 1)
        taps[e % SUBLANES].append((e // SUBLANES, k))
    return taps


def _row(v):
    return v.reshape(1, -1)


def _rest_body(x_ref, tgt_ref, bgate_ref, h1_ref, ya_ref, lg_ref, lb_ref, pww_ref, pwb_ref, wout_ref, fg_ref,
               ob_ref, dh_ref, loss_ref, gfg_ref, y_s):
    @pl.when((pl.program_id(0) == 0) & (pl.program_id(1) == 0))
    def _():
        loss_ref[...] = jnp.zeros_like(loss_ref)
        gfg_ref[...] = jnp.zeros_like(gfg_ref)

    h1 = h1_ref[0]
    mu = jnp.mean(h1, axis=-1, keepdims=True)
    xc = h1 - mu
    rstd = lax.rsqrt(jnp.mean(xc * xc, axis=-1, keepdims=True) + LN_EPS)
    h3 = _silu(xc * rstd * lg_ref[...] + lb_ref[...])
    ob = jnp.dot(h3.astype(BF16), pww_ref[...], preferred_element_type=F32) + pwb_ref[...]
    ob_ref[0] = ob
    y_s[:, 0:W_MIX] = ya_ref[0]
    y_s[:, W_MIX:] = (ob * _silu(bgate_ref[0])).astype(BF16)

    h = x_ref[0] + jnp.dot(y_s[...], wout_ref[...], preferred_element_type=F32)
    r2 = lax.rsqrt(jnp.mean(h * h, axis=-1, keepdims=True) + RMS_EPS)
    hr = h * r2
    err = hr * fg_ref[...] - tgt_ref[0]
    loss_ref[...] += jnp.sum(err * err) * (0.5 / D_MODEL)
    dout = err * (1.0 / D_MODEL)
    gfg_ref[...] += _colsum8(dout * hr)
    gy = dout * fg_ref[...]
    dh_ref[0] = r2 * (gy - hr * jnp.mean(gy * hr, axis=-1, keepdims=True))


def _rest(x, tgt, proj, h1, ya, ln_g, ln_b, pw_w_b, pw_b, w_out_b, final_g):
    bl, s, _ = x.shape
    tm = TILE_M
    tok = lambda n: pl.BlockSpec((1, tm, n), lambda b, i: (b, i, 0))
    acc = lambda n: pl.BlockSpec((SUBLANES, n), lambda b, i: (0, 0))
    b_gate = pl.BlockSpec((1, tm, W_MIX), lambda b, i: (b, i, 4))
    return pl.pallas_call(
        functools.partial(_rest_body),
        name="fwd",
        grid=(bl, s // tm),
        in_specs=[tok(D_MODEL), tok(D_MODEL), b_gate, tok(W_MIX), tok(W_MIX)] + [VMEM] * 6,
        out_specs=[tok(W_MIX), tok(D_MODEL), acc(LANES), acc(D_MODEL)],
        out_shape=[jax.ShapeDtypeStruct((bl, s, W_MIX), F32), jax.ShapeDtypeStruct((bl, s, D_MODEL), F32),
                   jax.ShapeDtypeStruct((SUBLANES, LANES), F32), jax.ShapeDtypeStruct((SUBLANES, D_MODEL), F32)],
        scratch_shapes=[pltpu.VMEM((tm, D_MODEL), BF16)],
        compiler_params=pltpu.CompilerParams(dimension_semantics=("arbitrary", "arbitrary"), vmem_limit_bytes=VMEM_LIMIT),
    )(x, tgt, proj, h1, ya, _row(ln_g), _row(ln_b), pw_w_b, _row(pw_b), w_out_b, _row(final_g))


NT_DIMS = (((1,), (1,)), ((), ()))
TN_DIMS = (((0,), (0,)), ((), ()))
V_POOL_B, V_POOL_SCALE, V_CONV_B, V_LN_G, V_LN_B, V_PW_B = range(6)


def _bwd_taps():
    taps = {s: [] for s in range(SUBLANES)}
    for k in range(CONV_WIDTH):
        e = CONV_WIDTH - 1 - k
        taps[e % SUBLANES].append((e // SUBLANES, k))
    return taps


def _bwd1_body(n_tiles, dh_ref, proj_ref, h1_ref, z_ref, ob_ref, dpool_ref, wout_ref, pww_ref, poolw_ref, pools_ref, dw_ref,
               lg_ref, lb_ref,
               dproj_ref, gwout_ref, gpww_ref, gpoolw_ref, gvec_ref, gdw_ref,
               vext, gext, gsh, y_s):
    b = pl.program_id(0)
    i = pl.program_id(1)
    tm = TILE_M
    t0 = (n_tiles - 1 - i) * tm

    @pl.when(i == 0)
    def _():
        vext[tm:, :] = jnp.zeros((POOL_HALO, W_MIX), F32)
        gext[tm:, :] = jnp.zeros((CONV_HALO, W_MIX), F32)

    @pl.when((b == 0) & (i == 0))
    def _():
        gwout_ref[...] = jnp.zeros_like(gwout_ref)
        gpww_ref[...] = jnp.zeros_like(gpww_ref)
        gpoolw_ref[...] = jnp.zeros_like(gpoolw_ref)
        gvec_ref[...] = jnp.zeros_like(gvec_ref)
        gdw_ref[...] = jnp.zeros_like(gdw_ref)

    sub = tm // BWD_SUBTILES

    def dense_phase(r0):
        rs = slice(r0, r0 + sub)
        dhb = dh_ref[0, rs, :].astype(BF16)
        dy_a = lax.dot_general(dhb, wout_ref[0:W_MIX, :], NT_DIMS, preferred_element_type=F32)
        dy_b = lax.dot_general(dhb, wout_ref[W_MIX:, :], NT_DIMS, preferred_element_type=F32)

        a_gate = proj_ref[0, rs, W_MIX:2 * W_MIX]
        sig_a = jax.nn.sigmoid(a_gate)
        silu_a = a_gate * sig_a
        z = z_ref[0, rs, :]
        out_a = z * pools_ref[...]
        y_s[rs, 0:W_MIX] = (out_a * silu_a).astype(BF16)
        d_out_a = dy_a * silu_a
        d_ag = dy_a * out_a * (sig_a * (1.0 + a_gate * (1.0 - sig_a)))
        dproj_ref[0, rs, W_MIX:2 * W_MIX] = d_ag.astype(BF16)
        gvec_ref[V_POOL_SCALE] += _colsum8(d_out_a * z)
        dz = d_out_a * pools_ref[...]
        gvec_ref[V_POOL_B] += _colsum8(dz)
        t = _row_index((sub, POOL_GC), t0 + r0)
        n_ext = sub + POOL_HALO
        for g, w in enumerate(POOL_WINDOWS):
            lanes = slice(g * POOL_GC, (g + 1) * POOL_GC)
            dzg = dz[:, lanes].astype(BF16)
            gpoolw_ref[g] += lax.dot_general(dpool_ref[0, rs, lanes], dzg, TN_DIMS, preferred_element_type=F32)
            dd = lax.dot_general(dzg, poolw_ref[g].astype(BF16), NT_DIMS, preferred_element_type=F32)
            vext[rs, lanes] = dd * (1.0 / jnp.minimum(t + 1, w).astype(F32))
            s = vext[r0:r0 + n_ext, lanes]
            sh = 1
            while sh < w:
                s = s + pltpu.roll(s, n_ext - sh, 0)
                sh *= 2
            dproj_ref[0, rs, lanes] = (s[0:sub] - dd).astype(BF16)

        b_gate = proj_ref[0, rs, 4 * W_MIX:5 * W_MIX]
        sig_b = jax.nn.sigmoid(b_gate)
        silu_b = b_gate * sig_b
        ob = ob_ref[0, rs, :]
        y_s[rs, W_MIX:] = (ob * silu_b).astype(BF16)
        d_ob = dy_b * silu_b
        dproj_ref[0, rs, 4 * W_MIX:5 * W_MIX] = (dy_b * ob * (sig_b * (1.0 + b_gate * (1.0 - sig_b)))).astype(BF16)
        gvec_ref[V_PW_B] += _colsum8(d_ob)
        gwout_ref[...] += lax.dot_general(y_s[rs, :], dhb, TN_DIMS, preferred_element_type=F32)

        h1 = h1_ref[0, rs, :]
        mu = jnp.mean(h1, axis=-1, keepdims=True)
        xc = h1 - mu
        rstd = lax.rsqrt(jnp.mean(xc * xc, axis=-1, keepdims=True) + LN_EPS)
        xhat = xc * rstd
        h2 = xhat * lg_ref[...] + lb_ref[...]
        sig2 = jax.nn.sigmoid(h2)
        d_obb = d_ob.astype(BF16)
        gpww_ref[...] += lax.dot_general((h2 * sig2).astype(BF16), d_obb, TN_DIMS, preferred_element_type=F32)
        dh2 = lax.dot_general(d_obb, pww_ref[...], NT_DIMS, preferred_element_type=F32) * (sig2 * (1.0 + h2 * (1.0 - sig2)))
        gvec_ref[V_LN_G] += _colsum8(dh2 * xhat)
        gvec_ref[V_LN_B] += _colsum8(dh2)
        dxh = dh2 * lg_ref[...]
        dh1 = rstd * (dxh - jnp.mean(dxh, axis=-1, keepdims=True) - xhat * jnp.mean(dxh * xhat, axis=-1, keepdims=True))
        gvec_ref[V_CONV_B] += _colsum8(dh1)
        gext[rs, :] = dh1

    taps = _bwd_taps()
    cr = BWD_CONV_ROWS
    win = cr + CONV_HALO - SUBLANES

    def conv_phase(r0):
        for c0 in range(r0, r0 + sub, cr):
            for l0 in range(0, W_MIX, LANES):
                lanes = slice(l0, l0 + LANES)
                b_val = proj_ref[0, c0:c0 + cr, 2 * W_MIX + l0:2 * W_MIX + l0 + LANES]
                sg = jax.nn.sigmoid(proj_ref[0, c0:c0 + cr, 3 * W_MIX + l0:3 * W_MIX + l0 + LANES])
                h0c = b_val * sg
                acc = jnp.zeros((cr, LANES), F32)
                for s in range(SUBLANES):
                    gsh[...] = gext[c0 + s:c0 + s + win, lanes]
                    for q, k in taps[s]:
                        gq = gsh[SUBLANES * q:SUBLANES * q + cr, :]
                        acc = acc + gq * dw_ref[k:k + 1, lanes]
                        gdw_ref[k, :, lanes] += _colsum8(h0c * gq)
                d_bval = acc * sg
                dproj_ref[0, c0:c0 + cr, 2 * W_MIX + l0:2 * W_MIX + l0 + LANES] = d_bval.astype(BF16)
                dproj_ref[0, c0:c0 + cr, 3 * W_MIX + l0:3 * W_MIX + l0 + LANES] = (d_bval * b_val * (1.0 - sg)).astype(BF16)

    starts = [k * sub for k in reversed(range(BWD_SUBTILES))]
    for r0 in starts:
        dense_phase(r0)
    for r0 in starts:
        conv_phase(r0)
    vext[tm:, :] = vext[0:POOL_HALO, :]
    gext[tm:, :] = gext[0:CONV_HALO, :]


def _bwd1(dh, proj, h1, z, ob, dpool, w_out_b, pw_w_b, pool_w, pool_scale, conv_dw_f, ln_g, ln_b):
    bl, s, _ = dh.shape
    tm = TILE_M
    nt = s // tm
    tok = lambda n: pl.BlockSpec((1, tm, n), lambda b, i: (b, nt - 1 - i, 0))
    res = lambda shape: pl.BlockSpec(shape, lambda b, i: (0,) * len(shape))
    out_shapes = [(D_MODEL, D_MODEL), (W_MIX, W_MIX), (len(POOL_WINDOWS), POOL_GC, POOL_GC), (6, SUBLANES, W_MIX),
                  (CONV_WIDTH, SUBLANES, W_MIX)]
    return pl.pallas_call(
        functools.partial(_bwd1_body, nt),
        name="bwd1",
        grid=(bl, nt),
        in_specs=[tok(D_MODEL), tok(D_IN), tok(W_MIX), tok(W_MIX), tok(W_MIX), tok(W_MIX)] + [VMEM] * 7,
        out_specs=[tok(D_IN)] + [res(sh) for sh in out_shapes],
        out_shape=[jax.ShapeDtypeStruct((bl, s, D_IN), BF16)] + [jax.ShapeDtypeStruct(sh, F32) for sh in out_shapes],
        scratch_shapes=[pltpu.VMEM((tm + POOL_HALO, W_MIX), F32), pltpu.VMEM((tm + CONV_HALO, W_MIX), F32),
                        pltpu.VMEM((BWD_CONV_ROWS + CONV_HALO - SUBLANES, LANES), F32), pltpu.VMEM((tm, D_MODEL), BF16)],
        compiler_params=pltpu.CompilerParams(dimension_semantics=("arbitrary", "arbitrary"), vmem_limit_bytes=VMEM_LIMIT),
    )(dh, proj, h1, z, ob, dpool, w_out_b, pw_w_b, pool_w, _row(pool_scale), conv_dw_f, _row(ln_g), _row(ln_b))


MESH = pl.DeviceIdType.MESH
W_IN_BLK = D_IN // N_DEV
W_OUT_BLK = D_MODEL // N_DEV
PW_BLK = W_MIX // N_DEV
DW_BLK = W_MIX // N_DEV
N_PEER = N_DEV - 1


SLAB = 2 * W_IN_BLK
N_SLABS = 4
F_SIB, F_X, F_Y, F_ON, F_PASS_X, F_PASS_Y, F_PASS_D = range(7)
FIRST_WAVE, SECOND_WAVE = (0, 3), (1, 2)
MIX_PHASE = N_SLABS


def _front_body(n_tiles, tiles_per_seq, x_ref, ng_ref, wint_ref, wout_ref, pww_ref, dw_ref, poolw_ref, poolb_ref, pools_ref,
                cb_ref,
                proj_hbm, wt_hbm, woutf_hbm, pwwf_hbm, dwf_ref, h1_ref, z_ref, dpool_ref, ya_ref,
                hn_s, wt_s, wout_s, pww_s, dw_blk, stage, pin, uext, hext,
                out_sems, in_sems, w_sems, send_sems, recv_sems):
    s = pl.program_id(0)
    i = pl.program_id(1)
    tm = TILE_M
    x, y, c = lax.axis_index("x"), lax.axis_index("y"), lax.axis_index("c")
    sibling = (x, y, 1 - c)
    xn, yn, dg = (1 - x, y), (x, 1 - y), (1 - x, 1 - y)
    pick = lambda a, b: tuple(jnp.where(c == 1, p, q) for p, q in zip(a, b))
    on_from, on_to = pick(xn, yn), pick(yn, xn)

    def blocks(px, py, pc):
        k = 4 * px + 2 * py + pc
        rows = lambda ref, n: ref.at[pl.ds(pl.multiple_of(k * n, n), n), :]
        return (rows(wt_s, W_IN_BLK), rows(wout_s, W_OUT_BLK), rows(pww_s, PW_BLK), dw_blk.at[k])

    def copies(arrays, flow, block, to):
        refs = blocks(*block)
        return [pltpu.make_async_remote_copy(src_ref=refs[a], dst_ref=refs[a], send_sem=send_sems.at[a, flow],
                                             recv_sem=recv_sems.at[a, flow], device_id=to, device_id_type=MESH)
                for a in arrays]

    def start(arrays):
        for flow, to in ((F_SIB, sibling), (F_X, (*xn, c)), (F_Y, (*yn, c))):
            for cp in copies(arrays, flow, (x, y, c), to):
                cp.start()

    def direct(arrays):
        for flow, chip in ((F_X, xn), (F_Y, yn)):
            for cp in copies(arrays, flow, (*chip, c), sibling):
                cp.wait_recv()
        for cp in copies(arrays, F_ON, (*on_from, c), (*on_to, c)):
            cp.start()
        for flow, chip in ((F_PASS_X, xn), (F_PASS_Y, yn)):
            for cp in copies(arrays, flow, (*chip, c), sibling):
                cp.start()

    def diagonal(arrays):
        for cp in copies(arrays, F_ON, (*dg, c), sibling):
            cp.wait_recv()
        for cp in copies(arrays, F_PASS_D, (*dg, c), sibling):
            cp.start()

    def from_sibling(arrays, flow, chip):
        for cp in copies(arrays, flow, (*chip, 1 - c), sibling):
            cp.wait_recv()

    def finish(arrays):
        mine = (x, y, c)
        for flow, block in ((F_SIB, mine), (F_X, mine), (F_Y, mine), (F_ON, (*on_from, c)), (F_PASS_X, (*xn, c)),
                            (F_PASS_Y, (*yn, c)), (F_PASS_D, (*dg, c))):
            for cp in copies(arrays, flow, block, sibling):
                cp.wait_send()

    step = s * n_tiles + i
    par = step % 2
    first = step == 0
    seq, tile = i // tiles_per_seq, i % tiles_per_seq
    rows_i = pl.ds(pl.multiple_of(i * tm, tm), tm)

    @pl.when(first)
    def _():
        mine = blocks(x, y, c)
        mine[0][...] = wint_ref[...].astype(BF16)
        mine[1][...] = wout_ref[...].astype(BF16)
        mine[2][...] = pww_ref[...].astype(BF16)
        mine[3][...] = dw_ref[...]
        start(FIRST_WAVE)

    @pl.when(s == 0)
    def _():
        xt = x_ref[0]
        r1 = lax.rsqrt(jnp.mean(xt * xt, axis=-1, keepdims=True) + RMS_EPS)
        hn_s[rows_i, :] = (xt * r1 * ng_ref[...]).astype(BF16)

    @pl.when(first)
    def _():
        from_sibling(FIRST_WAVE, F_SIB, (x, y))

    @pl.when((s == 1) & (i == 0))
    def _():
        direct(FIRST_WAVE)
        start(SECOND_WAVE)
        from_sibling(FIRST_WAVE, F_PASS_Y, yn)

    @pl.when((s == 2) & (i == 0))
    def _():
        from_sibling(FIRST_WAVE, F_PASS_X, xn)

    @pl.when((s == 3) & (i == 0))
    def _():
        diagonal(FIRST_WAVE)
        direct(SECOND_WAVE)
        from_sibling(FIRST_WAVE, F_PASS_D, dg)

    tok_rows = pl.ds(pl.multiple_of(tile * tm, tm), tm)

    def out_copy(parity, col0):
        dst = proj_hbm.at[seq, tok_rows, pl.ds(pl.multiple_of(col0, LANES), SLAB)]
        return pltpu.make_async_copy(stage.at[parity], dst, out_sems.at[parity])

    @pl.when(s < MIX_PHASE)
    def _():
        @pl.when(step >= 2)
        def _():
            out_copy(par, 0).wait()

        chip = jnp.where(s < 2, 2 * x, 2 * (1 - x)) + jnp.where((s == 0) | (s == 2), y, 1 - y)
        slab = wt_s[pl.ds(pl.multiple_of(chip * SLAB, SLAB), SLAB), :]
        stage[par] = lax.dot_general(hn_s[rows_i, :], slab, NT_DIMS, preferred_element_type=F32)
        out_copy(par, chip * SLAB).start()

    def in_copy(parity, k):
        src = proj_hbm.at[k // tiles_per_seq, pl.ds(pl.multiple_of((k % tiles_per_seq) * tm, tm), tm), pl.ds(0, 4 * W_MIX)]
        return pltpu.make_async_copy(src, pin.at[parity], in_sems.at[parity])

    @pl.when((s == MIX_PHASE) & (i == 0))
    def _():
        out_copy(par, 0).wait()
        out_copy(1 - par, 0).wait()
        in_copy(par, 0).start()
        for k in range(N_DEV):
            dwf_ref[:, k * DW_BLK:(k + 1) * DW_BLK] = dw_blk[k]

    @pl.when(s == MIX_PHASE)
    def _():
        in_copy(par, i).wait()

        @pl.when(i + 1 < n_tiles)
        def _():
            in_copy(1 - par, i + 1).start()

        @pl.when(tile == 0)
        def _():
            uext[0:POOL_HALO, :] = jnp.zeros((POOL_HALO, W_MIX), F32)
            hext[0:CONV_HALO, :] = jnp.zeros((CONV_HALO, W_MIX), F32)
            hext[CONV_HALO + tm:, :] = jnp.zeros((SUBLANES, W_MIX), F32)

        p = pin.at[par]
        uext[POOL_HALO:, :] = p[:, 0:W_MIX]
        t = _row_index((tm, POOL_GC), tile * tm)
        for g, w in enumerate(POOL_WINDOWS):
            lanes = slice(g * POOL_GC, (g + 1) * POOL_GC)
            e = uext[:, lanes]
            acc = e
            sh = 1
            while sh < w:
                acc = acc + pltpu.roll(acc, sh, 0)
                sh *= 2
            inv_cnt = 1.0 / jnp.minimum(t + 1, w).astype(F32)
            d = (acc[POOL_HALO:] * inv_cnt - e[POOL_HALO:]).astype(BF16)
            dpool_ref[0, :, lanes] = d
            z = jnp.dot(d, poolw_ref[g].astype(BF16), preferred_element_type=F32) + poolb_ref[:, lanes]
            z_ref[0, :, lanes] = z
            a_gate = p[:, W_MIX + g * POOL_GC:W_MIX + (g + 1) * POOL_GC]
            ya_ref[0, :, lanes] = (z * pools_ref[:, lanes] * _silu(a_gate)).astype(BF16)

        hext[CONV_HALO:CONV_HALO + tm, :] = p[:, 2 * W_MIX:3 * W_MIX] * jax.nn.sigmoid(p[:, 3 * W_MIX:4 * W_MIX])
        taps = _conv_taps()
        for c0 in range(0, tm, CONV_ROWS):
            for l0 in range(0, W_MIX, LANES):
                lanes = slice(l0, l0 + LANES)
                acc = jnp.zeros((CONV_ROWS, LANES), F32) + cb_ref[:, lanes]
                for sft in range(SUBLANES):
                    part = jnp.zeros((CONV_ROWS + SUBLANES, LANES), F32)
                    for q, k in taps[sft]:
                        r0 = c0 + SUBLANES * q
                        part = part + hext[r0:r0 + CONV_ROWS + SUBLANES, lanes] * dwf_ref[k:k + 1, lanes]
                    acc = acc + part[sft:sft + CONV_ROWS]
                h1_ref[0, c0:c0 + CONV_ROWS, lanes] = acc
        uext[0:POOL_HALO, :] = uext[tm:tm + POOL_HALO, :]
        hext[0:CONV_HALO, :] = hext[tm:tm + CONV_HALO, :]

    @pl.when(step == (MIX_PHASE + 1) * n_tiles - 1)
    def _():
        from_sibling(SECOND_WAVE, F_SIB, (x, y))
        from_sibling(SECOND_WAVE, F_PASS_X, xn)
        from_sibling(SECOND_WAVE, F_PASS_Y, yn)
        diagonal(SECOND_WAVE)
        from_sibling(SECOND_WAVE, F_PASS_D, dg)
        finish(FIRST_WAVE)
        finish(SECOND_WAVE)
        outs = [pltpu.make_async_copy(src, dst, w_sems.at[k])
                for k, (src, dst) in enumerate(((wt_s, wt_hbm), (wout_s, woutf_hbm), (pww_s, pwwf_hbm)))]
        for cp in outs:
            cp.start()
        for cp in outs:
            cp.wait()


def _front(x, norm_g, w_in_t, w_out, pw_w, conv_dw, pool_w, pool_b, pool_scale, conv_b):
    bl, s, _ = x.shape
    tm = TILE_M
    nt = s // tm
    n_tiles = bl * nt

    def x_map(phase, i):
        i = jnp.where(phase == 0, i, n_tiles - 1)
        return (i // nt, i % nt, 0)

    def mix_map(phase, i):
        i = jnp.where(phase == MIX_PHASE, i, 0)
        return (i // nt, i % nt, 0)

    dma = pltpu.SemaphoreType.DMA
    hbm = pl.BlockSpec(memory_space=pl.ANY)
    mix = pl.BlockSpec((1, tm, W_MIX), mix_map)
    act = lambda dt: jax.ShapeDtypeStruct((bl, s, W_MIX), dt)
    return pl.pallas_call(
        functools.partial(_front_body, n_tiles, nt),
        name="front",
        grid=(MIX_PHASE + 1, n_tiles),
        in_specs=[pl.BlockSpec((1, tm, D_MODEL), x_map)] + [VMEM] * 9,
        out_specs=[hbm, hbm, hbm, hbm, VMEM, mix, mix, mix, mix],
        out_shape=[jax.ShapeDtypeStruct((bl, s, D_IN), F32), jax.ShapeDtypeStruct((D_IN, D_MODEL), BF16),
                   jax.ShapeDtypeStruct((D_MODEL, D_MODEL), BF16), jax.ShapeDtypeStruct((W_MIX, W_MIX), BF16),
                   jax.ShapeDtypeStruct((CONV_WIDTH, W_MIX), F32), act(F32), act(F32), act(BF16), act(BF16)],
        scratch_shapes=[pltpu.VMEM((n_tiles * tm, D_MODEL), BF16), pltpu.VMEM((D_IN, D_MODEL), BF16),
                        pltpu.VMEM((D_MODEL, D_MODEL), BF16), pltpu.VMEM((W_MIX, W_MIX), BF16),
                        pltpu.VMEM((N_DEV, CONV_WIDTH, DW_BLK), F32), pltpu.VMEM((2, tm, SLAB), F32),
                        pltpu.VMEM((2, tm, 4 * W_MIX), F32), pltpu.VMEM((POOL_HALO + tm, W_MIX), F32),
                        pltpu.VMEM((CONV_HALO + tm + SUBLANES, W_MIX), F32),
                        dma((2,)), dma((2,)), dma((3,)), dma((4, 7)), dma((4, 7))],
        compiler_params=pltpu.CompilerParams(dimension_semantics=("arbitrary", "arbitrary"), vmem_limit_bytes=VMEM_LIMIT),
    )(x, _row(norm_g), w_in_t, w_out, pw_w, conv_dw, pool_w, pool_b.reshape(1, W_MIX), _row(pool_scale), _row(conv_b))


B_PLAIN, B_ON, B_MERGED = range(3)


def _routes():
    x, y, c = lax.axis_index("x"), lax.axis_index("y"), lax.axis_index("c")
    xn, yn = (1 - x, y), (x, 1 - y)
    pick = lambda a, b: tuple(jnp.where(c == 1, p, q) for p, q in zip(a, b))
    return pick(xn, yn), pick(yn, xn), (1 - x, 1 - y), (x, y), c


def _chip_index(chip):
    return 2 * chip[0] + chip[1]


def _copy(src, dst, send_sems, recv_sems, a, k, to):
    return pltpu.make_async_remote_copy(src_ref=src, dst_ref=dst, send_sem=send_sems.at[a, k], recv_sem=recv_sems.at[a, k],
                                        device_id=to, device_id_type=MESH)


def _rsb_send(plain, on, rb_plain, rb_on, send_b, recv_b):
    first, _, _, _, c = _routes()
    for k, srcs, dsts in ((B_ON, on, rb_on), (B_PLAIN, plain, rb_plain)):
        for a, (s, d) in enumerate(zip(srcs, dsts)):
            _copy(s, d, send_b, recv_b, a, k, (*first, c)).start()


def _rsb_merge(second_sums, merged, rb_on, rb_merged, send_b, recv_b):
    _, second, _, _, c = _routes()
    for a, (own, m, got, dst) in enumerate(zip(second_sums, merged, rb_on, rb_merged)):
        _copy(got, got, send_b, recv_b, a, B_ON, (*second, c)).wait_recv()
        m[...] = (own[...] + got[...].astype(F32)).astype(BF16)
        _copy(m, dst, send_b, recv_b, a, B_MERGED, (*second, c)).start()


def _rsb_total(own_sums, plain, on, merged, rb_plain, rb_merged, send_b, recv_b):
    _, second, _, _, c = _routes()
    out = []
    for a, (own, p, g) in enumerate(zip(own_sums, rb_plain, rb_merged)):
        _copy(p, p, send_b, recv_b, a, B_PLAIN, (*second, c)).wait_recv()
        _copy(g, g, send_b, recv_b, a, B_MERGED, (*second, c)).wait_recv()
        out.append(own[...] + p[...].astype(F32) + g[...].astype(F32))
    for k, srcs in ((B_PLAIN, plain), (B_ON, on), (B_MERGED, merged)):
        for a, s in enumerate(srcs):
            _copy(s, s, send_b, recv_b, a, k, (*second, c)).wait_send()
    return out


R_LOSS, R_FINAL_G, R_POOL, R_CONV, R_LN = range(5)
VEC_ROWS = 16
DW_ROWS = 32
A_PART, A_X, A_Y, A_ON, A_HALF = range(5)
STEP_SUMS, STEP_MERGE, STEP_HALF = 1, 3, 5


def _bwd2a_body(n_steps, x_ref, dproj_ref, ng_ref, gwout_ref, gpww_ref, loss_ref, gfg_ref, gvec_ref, gdw_ref, gpoolw_ref,
                win_own_ref, win_staged_ref, owout_ref, opww_ref, ovec_ref, odw_ref, opoolw_ref,
                acc, st_win, a_win, a_wout, a_pww, st_wout, st_pww, rb_wout, rb_pww, part_vec, part_dw, part_pw,
                sib_vec, sib_dw, sib_pw, res_vec, res_dw, res_pw, half_vec, half_dw, half_pw,
                send_a, recv_a, send_b, recv_b, send_r, recv_r, send_w, recv_w):
    step = pl.program_id(0) * pl.num_programs(1) + pl.program_id(1)
    x, y, c = lax.axis_index("x"), lax.axis_index("y"), lax.axis_index("c")
    sibling = (x, y, 1 - c)
    xn, yn = (1 - x, y), (x, 1 - y)
    first, second, diag, own, _ = _routes()
    rows = lambda ref, k, n: ref.at[pl.ds(pl.multiple_of(k * n, n), n), :]

    blocks = lambda k: (rows(gwout_ref, k, W_OUT_BLK), rows(gpww_ref, k, PW_BLK))
    landed = lambda chip: (a_wout.at[chip], a_pww.at[chip])
    staged = lambda k: (st_wout.at[k], st_pww.at[k])
    arrived = lambda k: (rb_wout.at[k], rb_pww.at[k])

    parts, sibs, halves = (part_vec, part_dw, part_pw), (sib_vec, sib_dw, sib_pw), (half_vec, half_dw, half_pw)
    results = (res_vec, res_dw, res_pw)
    half_rows = lambda ref: pl.ds(pl.multiple_of(c * (ref.shape[0] // 2), SUBLANES), ref.shape[0] // 2)
    slot = lambda chip: tuple(h.at[_chip_index(chip)] for h in halves)

    @pl.when(step == 0)
    def _():
        acc[...] = jnp.zeros_like(acc)
        sum8 = lambda v: jnp.sum(v, axis=0, keepdims=True)
        part_vec[...] = jnp.zeros((VEC_ROWS, D_MODEL), F32)
        part_vec[R_LOSS:R_LOSS + 1, :] = jnp.broadcast_to(loss_ref[0:1, 0:1], (1, D_MODEL))
        part_vec[R_FINAL_G:R_FINAL_G + 1, :] = sum8(gfg_ref[...])
        for row, (lo, hi) in ((R_POOL, (V_POOL_B, V_POOL_SCALE)), (R_CONV, (V_CONV_B, V_LN_G)), (R_LN, (V_LN_B, V_PW_B))):
            part_vec[row:row + 1, 0:W_MIX] = sum8(gvec_ref[lo])
            part_vec[row:row + 1, W_MIX:] = sum8(gvec_ref[hi])
        part_dw[0:CONV_WIDTH, :] = jnp.sum(gdw_ref[...], axis=1)
        part_dw[CONV_WIDTH:, :] = jnp.zeros((DW_ROWS - CONV_WIDTH, W_MIX), F32)
        part_pw[...] = gpoolw_ref[...].reshape(len(POOL_WINDOWS) * POOL_GC, POOL_GC)
        for a, (p, s) in enumerate(zip(parts, sibs)):
            _copy(p, s, send_r, recv_r, a, A_PART, sibling).start()
        for chip in range(4):
            for a, (s, d) in enumerate(zip(blocks(2 * chip + (1 - c)), landed(chip))):
                _copy(s, d, send_a, recv_a, a, chip, sibling).start()

    x_t = x_ref[0]
    r1 = lax.rsqrt(jnp.mean(x_t * x_t, axis=-1, keepdims=True) + RMS_EPS)
    hn = (x_t * r1 * ng_ref[...]).astype(BF16)
    acc[...] += lax.dot_general(dproj_ref[0], hn, TN_DIMS, preferred_element_type=F32)

    @pl.when(step == STEP_SUMS)
    def _():
        for a, (p, s, h) in enumerate(zip(parts, sibs, slot(own))):
            _copy(p, s, send_r, recv_r, a, A_PART, sibling).wait_recv()
            h[...] = p[half_rows(p), :] + s[half_rows(s), :]
            for k, to in ((A_X, xn), (A_Y, yn)):
                _copy(h, h, send_r, recv_r, a, k, (*to, c)).start()
        for chip in range(4):
            for a, (m, l) in enumerate(zip(blocks(2 * chip + c), landed(chip))):
                _copy(l, l, send_a, recv_a, a, chip, sibling).wait_recv()
                l[...] = m[...] + l[...]
        for k, chip in ((B_ON, diag), (B_PLAIN, first)):
            for s, l in zip(staged(k), landed(_chip_index(chip))):
                s[...] = l[...].astype(BF16)
        _rsb_send(staged(B_PLAIN), staged(B_ON), arrived(B_PLAIN), arrived(B_ON), send_b, recv_b)

    @pl.when(step == STEP_MERGE)
    def _():
        _rsb_merge(landed(_chip_index(second)), staged(B_MERGED), arrived(B_ON), arrived(B_MERGED), send_b, recv_b)
        for a in range(3):
            for k, chip in ((A_X, xn), (A_Y, yn)):
                h = slot(chip)[a]
                _copy(h, h, send_r, recv_r, a, k, sibling).wait_recv()
            h = slot(first)[a]
            _copy(h, h, send_r, recv_r, a, A_ON, (*second, c)).start()

    @pl.when(step == STEP_HALF)
    def _():
        for a, (r, hs) in enumerate(zip(results, halves)):
            h = slot(diag)[a]
            _copy(h, h, send_r, recv_r, a, A_ON, sibling).wait_recv()
            mine = r.at[half_rows(r), :]
            mine[...] = (hs[0] + hs[1]) + (hs[2] + hs[3])
            _copy(mine, mine, send_r, recv_r, a, A_HALF, sibling).start()

    @pl.when(step == n_steps - 1)
    def _():
        block = lambda k: rows(acc, k, W_IN_BLK)
        w_copy = lambda chip: _copy(st_win.at[chip], a_win.at[chip], send_w, recv_w, 0, chip, sibling)
        for chip in range(4):
            st_win[chip] = block(2 * chip + (1 - c))[...].astype(BF16)
            w_copy(chip).start()

        owout_ref[...], opww_ref[...] = _rsb_total(landed(_chip_index(own)), staged(B_PLAIN), staged(B_ON), staged(B_MERGED),
                                                   arrived(B_PLAIN), arrived(B_MERGED), send_b, recv_b)
        for a, (r, p, s) in enumerate(zip(results, parts, sibs)):
            theirs = r.at[pl.ds(pl.multiple_of((1 - c) * (r.shape[0] // 2), SUBLANES), r.shape[0] // 2), :]
            _copy(theirs, theirs, send_r, recv_r, a, A_HALF, sibling).wait_recv()
            _copy(p, s, send_r, recv_r, a, A_PART, sibling).wait_send()
            for k, h in ((A_X, slot(own)[a]), (A_Y, slot(own)[a]), (A_ON, slot(first)[a])):
                _copy(h, h, send_r, recv_r, a, k, sibling).wait_send()
            mine = r.at[half_rows(r), :]
            _copy(mine, mine, send_r, recv_r, a, A_HALF, sibling).wait_send()
        for chip in range(4):
            for a, (s, d) in enumerate(zip(blocks(2 * chip + (1 - c)), landed(chip))):
                _copy(s, d, send_a, recv_a, a, chip, sibling).wait_send()
        ovec_ref[...] = res_vec[...]
        opoolw_ref[...] = res_pw[...]
        for k in range(N_DEV):
            @pl.when(4 * x + 2 * y + c == k)
            def _():
                odw_ref[...] = res_dw[0:CONV_WIDTH, k * DW_BLK:(k + 1) * DW_BLK]

        for chip in range(4):
            w_copy(chip).wait()
        chip_sum = lambda chip: block(2 * _chip_index(chip) + c)[...] + a_win[_chip_index(chip)].astype(F32)
        win_own_ref[...] = chip_sum(own)
        for j, chip in enumerate((xn, yn, diag)):
            win_staged_ref[j] = chip_sum(chip).astype(BF16)


def _bwd2a(x, dproj, norm_g, gwout, gpww, loss8, gfg8, gvec, gdw8, gpoolw):
    bl, s, _ = x.shape
    tm = TILE_M
    nt = s // tm
    tok = lambda n: pl.BlockSpec((1, tm, n), lambda b, i: (b, i, 0))
    n_pw = len(POOL_WINDOWS) * POOL_GC
    win = (W_IN_BLK, D_MODEL)
    blk = ((W_OUT_BLK, D_MODEL), (PW_BLK, W_MIX))
    rep = ((VEC_ROWS, D_MODEL), (DW_ROWS, W_MIX), (n_pw, POOL_GC))
    half = lambda sh: (sh[0] // 2,) + sh[1:]
    dma = pltpu.SemaphoreType.DMA
    assert bl * nt > STEP_HALF + 1, "the exchanges' phases need their grid steps"
    return pl.pallas_call(
        functools.partial(_bwd2a_body, bl * nt),
        name="bwd2a",
        grid=(bl, nt),
        in_specs=[tok(D_MODEL), tok(D_IN)] + [VMEM] * 8,
        out_specs=[VMEM] * 7,
        out_shape=[jax.ShapeDtypeStruct(win, F32), jax.ShapeDtypeStruct((3,) + win, BF16)]
        + [jax.ShapeDtypeStruct(sh, F32) for sh in blk + (rep[0], (CONV_WIDTH, DW_BLK), rep[2])],
        scratch_shapes=[pltpu.VMEM((D_IN, D_MODEL), F32), pltpu.VMEM((4,) + win, BF16), pltpu.VMEM((4,) + win, BF16)]
        + [pltpu.VMEM((4,) + sh, F32) for sh in blk]
        + [pltpu.VMEM((3,) + sh, BF16) for sh in blk] * 2
        + [pltpu.VMEM(sh, F32) for sh in rep] * 3
        + [pltpu.VMEM((4,) + half(sh), F32) for sh in rep]
        + [dma((2, 4)), dma((2, 4)), dma((2, 3)), dma((2, 3)), dma((3, 5)), dma((3, 5)), dma((1, 4)), dma((1, 4))],
        compiler_params=pltpu.CompilerParams(dimension_semantics=("arbitrary", "arbitrary"), vmem_limit_bytes=VMEM_LIMIT),
    )(x, dproj, _row(norm_g), gwout, gpww, loss8, gfg8, gvec, gdw8, gpoolw)


HBM = pl.BlockSpec(memory_space=pltpu.HBM)
SEM = pl.BlockSpec(memory_space=pltpu.SEMAPHORE)
EFFECT = pltpu.SideEffectType.DATAFLOW_SIDE_EFFECTING


def _owner_copies(src_ref, land_ref, send_sems, recv_sems):
    x, y, c = lax.axis_index("x"), lax.axis_index("y"), lax.axis_index("c")
    owners = ((1 - x, y, c), (x, 1 - y, c), (1 - x, 1 - y, c))
    return [pltpu.make_async_remote_copy(src_ref=src_ref.at[j], dst_ref=land_ref.at[j], send_sem=send_sems.at[j],
                                         recv_sem=recv_sems.at[j], device_id=owners[j], device_id_type=MESH)
            for j in range(3)]


def _rs_start(staged):
    def body(src_ref, land_ref, send_sems, recv_sems, src_thru, land_thru, token):
        for cp in _owner_copies(src_ref, land_ref, send_sems, recv_sems):
            cp.start()
        token[...] = jnp.zeros_like(token)

    dma = pltpu.SemaphoreType.DMA
    return pl.pallas_call(
        body, name="rs_start",
        out_shape=(dma((3,)), dma((3,)), pltpu.HBM(staged.shape, staged.dtype), pltpu.HBM(staged.shape, staged.dtype),
                   jax.ShapeDtypeStruct((SUBLANES, LANES), F32)),
        in_specs=(HBM, HBM), out_specs=(SEM, SEM, HBM, HBM, VMEM), input_output_aliases={0: 2, 1: 3},
        compiler_params=pltpu.CompilerParams(has_side_effects=EFFECT),
    )(pltpu.with_memory_space_constraint(staged, pltpu.HBM),
      pltpu.with_memory_space_constraint(lax.empty(staged.shape, staged.dtype), pltpu.HBM))


def _rs_wait(send_sems, recv_sems, src_thru, land_thru, after):
    def body(src_ref, land_ref, send_sems, recv_sems, after_ref, src_dead, got_ref):
        for cp in _owner_copies(src_ref, land_ref, send_sems, recv_sems):
            cp.wait_send()
            cp.wait_recv()

    return pl.pallas_call(
        body, name="rs_wait",
        out_shape=(pltpu.HBM(src_thru.shape, src_thru.dtype), pltpu.HBM(land_thru.shape, land_thru.dtype)),
        in_specs=(HBM, HBM, SEM, SEM, pl.BlockSpec(memory_space=pl.ANY)), out_specs=(HBM, HBM),
        input_output_aliases={0: 0, 1: 1},
        compiler_params=pltpu.CompilerParams(has_side_effects=EFFECT),
    )(src_thru, land_thru, send_sems, recv_sems, after)[1]


def _bwd2b_body(x_ref, dh_ref, dproj_ref, ng_ref, wint_ref, token_ref, gx_ref, gng_ref):
    @pl.when((pl.program_id(0) == 0) & (pl.program_id(1) == 0))
    def _():
        gng_ref[...] = jnp.zeros_like(gng_ref) + token_ref[0:1, 0:1]

    x_t = x_ref[0]
    r1 = lax.rsqrt(jnp.mean(x_t * x_t, axis=-1, keepdims=True) + RMS_EPS)
    xr = x_t * r1
    dhn = jnp.dot(dproj_ref[0], wint_ref[...], preferred_element_type=F32)
    gng_ref[...] += _colsum8(dhn * xr)
    gy = dhn * ng_ref[...]
    gx_ref[0] = dh_ref[0] + r1 * (gy - xr * jnp.mean(gy * xr, axis=-1, keepdims=True))


def _bwd2b(x, dh, dproj, norm_g, w_in_t, token):
    bl, s, _ = x.shape
    tm = TILE_M
    tok = lambda n: pl.BlockSpec((1, tm, n), lambda b, i: (b, i, 0))
    return pl.pallas_call(
        functools.partial(_bwd2b_body),
        name="bwd2b",
        grid=(bl, s // tm),
        in_specs=[tok(D_MODEL), tok(D_MODEL), tok(D_IN), VMEM, VMEM, VMEM],
        out_specs=[tok(D_MODEL), pl.BlockSpec((SUBLANES, D_MODEL), lambda b, i: (0, 0))],
        out_shape=[jax.ShapeDtypeStruct((bl, s, D_MODEL), F32), jax.ShapeDtypeStruct((SUBLANES, D_MODEL), F32)],
        compiler_params=pltpu.CompilerParams(dimension_semantics=("arbitrary", "arbitrary"), vmem_limit_bytes=VMEM_LIMIT),
    )(x, dh, dproj, _row(norm_g), w_in_t, token)


def _partial_copies(part_ref, land_ref, send_sems, recv_sems):
    x, y, c = lax.axis_index("x"), lax.axis_index("y"), lax.axis_index("c")
    other_chips = [(1 - x, y), (x, 1 - y), (1 - x, 1 - y)]
    peers = [(x, y, 1 - c)] + [(*chip, c) for chip in other_chips] + [(*chip, 1 - c) for chip in other_chips]
    return [pltpu.make_async_remote_copy(src_ref=part_ref, dst_ref=land_ref.at[4 * x + 2 * y + c], send_sem=send_sems.at[r],
                                         recv_sem=recv_sems.at[r], device_id=peers[r], device_id_type=MESH)
            for r in range(N_PEER)]


def _partials_start(part):
    def body(part_ref, land_ref, send_sems, recv_sems, part_thru, land_thru):
        for cp in _partial_copies(part_ref, land_ref, send_sems, recv_sems):
            cp.start()

    dma = pltpu.SemaphoreType.DMA
    land = (N_DEV,) + part.shape
    return pl.pallas_call(
        body, name="partials_start",
        out_shape=(dma((N_PEER,)), dma((N_PEER,)), pltpu.HBM(part.shape, part.dtype), pltpu.HBM(land, part.dtype)),
        in_specs=(HBM, HBM), out_specs=(SEM, SEM, HBM, HBM), input_output_aliases={0: 2, 1: 3},
        compiler_params=pltpu.CompilerParams(has_side_effects=EFFECT),
    )(pltpu.with_memory_space_constraint(part, pltpu.HBM),
      pltpu.with_memory_space_constraint(lax.empty(land, part.dtype), pltpu.HBM))


def _partials_wait(send_sems, recv_sems, part_thru, land_thru):
    def body(part_ref, land_ref, send_sems, recv_sems, part_out, land_out):
        for cp in _partial_copies(part_ref, land_ref, send_sems, recv_sems):
            cp.wait_send()
            cp.wait_recv()

    return pl.pallas_call(
        body, name="partials_wait",
        out_shape=(pltpu.HBM(part_thru.shape, part_thru.dtype), pltpu.HBM(land_thru.shape, land_thru.dtype)),
        in_specs=(HBM, HBM, SEM, SEM), out_specs=(HBM, HBM), input_output_aliases={0: 0, 1: 1},
        compiler_params=pltpu.CompilerParams(has_side_effects=EFFECT),
    )(part_thru, land_thru, send_sems, recv_sems)


ADAM_LR = 0.001
ADAM_B1 = 0.9
ADAM_B2 = 0.999
ADAM_EPS = 1e-08
ADAM_WD = 0.01
ADAM_STEP = 10


PARAMS = ("norm_g", "w_in", "pool_w", "pool_b", "pool_scale", "conv_dw", "conv_b", "ln_g", "ln_b", "pw_w", "pw_b", "w_out",
          "final_g")


def _adamw_body(*refs):
    n = len(PARAMS)
    ws, ms, vs = (dict(zip(PARAMS, refs[k * n:(k + 1) * n])) for k in range(3))
    gwin_ref, gwin_others_ref, gwout_ref, gpww_ref, vec_ref, gdw_ref, gpoolw_ref, gng_ref, gng_others_ref = refs[3 * n:3 * n + 9]
    outs = refs[3 * n + 9:]
    gs, ds, m2s, v2s = (dict(zip(PARAMS, outs[k * n:(k + 1) * n])) for k in range(4))
    loss_ref = outs[4 * n]

    def update(name, g, at=slice(None)):
        m = ADAM_B1 * ms[name][at] + (1.0 - ADAM_B1) * g
        v = ADAM_B2 * vs[name][at] + (1.0 - ADAM_B2) * (g * g)
        m_hat = m / (1.0 - ADAM_B1 ** ADAM_STEP)
        v_hat = v / (1.0 - ADAM_B2 ** ADAM_STEP)
        gs[name][at] = g
        ds[name][at] = -ADAM_LR * (m_hat / (jnp.sqrt(v_hat) + ADAM_EPS) + ADAM_WD * ws[name][at])
        m2s[name][at] = m
        v2s[name][at] = v

    row = lambda r, lo, hi: vec_ref[r:r + 1, lo:hi]
    loss_ref[...] = row(R_LOSS, 0, 1)
    me = 4 * lax.axis_index("x") + 2 * lax.axis_index("y") + lax.axis_index("c")
    g_norm = jnp.zeros(gng_ref.shape, F32)
    for k in range(N_DEV):
        g_norm = g_norm + jnp.where(me == k, gng_ref[...], gng_others_ref[k])
    update("norm_g", jnp.sum(g_norm, axis=0, keepdims=True))
    g_w_in = gwin_ref[...]
    for j in range(3):
        g_w_in = g_w_in + gwin_others_ref[j].astype(F32)
    update("w_in", g_w_in)
    update("pool_w", gpoolw_ref[...].reshape(ws["pool_w"].shape))
    for g in range(len(POOL_WINDOWS)):
        update("pool_b", row(R_POOL, g * POOL_GC, (g + 1) * POOL_GC), at=slice(g, g + 1))
    update("pool_scale", row(R_POOL, W_MIX, 2 * W_MIX))
    update("conv_dw", gdw_ref[...])
    update("conv_b", row(R_CONV, 0, W_MIX))
    update("ln_g", row(R_CONV, W_MIX, 2 * W_MIX))
    update("ln_b", row(R_LN, 0, W_MIX))
    update("pw_w", gpww_ref[...])
    update("pw_b", row(R_LN, W_MIX, 2 * W_MIX))
    update("w_out", gwout_ref[...])
    update("final_g", row(R_FINAL_G, 0, D_MODEL))


def _adamw(ws, ms, vs, grad_sources):
    n = len(PARAMS)
    shapes = [ws[p].shape for p in PARAMS]
    flat = lambda a: a.reshape(1, -1) if a.ndim == 1 else a
    args = [flat(d[p]) for d in (ws, ms, vs) for p in PARAMS]
    outs = pl.pallas_call(
        functools.partial(_adamw_body),
        name="adamw",
        in_specs=[VMEM] * (3 * n + len(grad_sources)),
        out_specs=[VMEM] * (4 * n + 1),
        out_shape=[jax.ShapeDtypeStruct(a.shape, F32) for a in args[:n]] * 4 + [jax.ShapeDtypeStruct((1, 1), F32)],
        compiler_params=pltpu.CompilerParams(vmem_limit_bytes=VMEM_LIMIT),
    )(*args, *grad_sources)
    loss = outs[4 * n].reshape(())
    outs = [o.reshape(shapes[k % n]) for k, o in enumerate(outs[:4 * n])]
    return loss, outs[:n], outs[n:2 * n], outs[2 * n:3 * n], outs[3 * n:]


def kernel(x, norm_g, w_in, pool_w, pool_b, pool_scale, conv_dw, conv_b, ln_g, ln_b, pw_w, pw_b, w_out, final_g, loss_target, m_norm_g, m_w_in, m_pool_w, m_pool_b, m_pool_scale, m_conv_dw, m_conv_b, m_ln_g, m_ln_b, m_pw_w, m_pw_b, m_w_out, m_final_g, v_norm_g, v_w_in, v_pool_w, v_pool_b, v_pool_scale, v_conv_dw, v_conv_b, v_ln_g, v_ln_b, v_pw_w, v_pw_b, v_w_out, v_final_g):
    proj, w_in_t, w_out_b, pw_w_b, conv_dw_f, h1, z, dpool, ya = _front(x, norm_g, w_in.T, w_out, pw_w, conv_dw, pool_w, pool_b,
                                                                       pool_scale, conv_b)
    ob, dh, loss8, gfg8 = _rest(x, loss_target, proj, h1, ya, ln_g, ln_b, pw_w_b, pw_b, w_out_b, final_g)
    dproj, gwout, gpww, gpoolw, gvec, gdw8 = _bwd1(dh, proj, h1, z, ob, dpool, w_out_b, pw_w_b, pool_w, pool_scale,
                                                   conv_dw_f, ln_g, ln_b)
    win_own, win_staged, g_w_out, g_pw_w, vec, g_conv_dw, poolw_sum = _bwd2a(x, dproj, norm_g, gwout, gpww, loss8, gfg8, gvec,
                                                                             gdw8, gpoolw)
    send_sems, recv_sems, staged_thru, land_thru, token = _rs_start(win_staged)
    grad_x, gng8 = _bwd2b(x, dh, dproj, norm_g, w_in_t, token)
    ng_send, ng_recv, gng8, gng_land = _partials_start(gng8)
    win_others = _rs_wait(send_sems, recv_sems, staged_thru, land_thru, gng8)
    gng8, gng_others = _partials_wait(ng_send, ng_recv, gng8, gng_land)

    weights = dict(norm_g=norm_g, w_in=w_in.T, pool_w=pool_w, pool_b=pool_b, pool_scale=pool_scale, conv_dw=conv_dw, conv_b=conv_b,
                   ln_g=ln_g, ln_b=ln_b, pw_w=pw_w, pw_b=pw_b, w_out=w_out, final_g=final_g)
    m_in = dict(norm_g=m_norm_g, w_in=m_w_in.T, pool_w=m_pool_w, pool_b=m_pool_b, pool_scale=m_pool_scale, conv_dw=m_conv_dw,
                conv_b=m_conv_b, ln_g=m_ln_g, ln_b=m_ln_b, pw_w=m_pw_w, pw_b=m_pw_b, w_out=m_w_out, final_g=m_final_g)
    v_in = dict(norm_g=v_norm_g, w_in=v_w_in.T, pool_w=v_pool_w, pool_b=v_pool_b, pool_scale=v_pool_scale, conv_dw=v_conv_dw,
                conv_b=v_conv_b, ln_g=v_ln_g, ln_b=v_ln_b, pw_w=v_pw_w, pw_b=v_pw_b, w_out=v_w_out, final_g=v_final_g)
    loss, *groups = _adamw(weights, m_in, v_in, (win_own, win_others, g_w_out, g_pw_w, vec, g_conv_dw, poolw_sum, gng8,
                                                 gng_others))
    k_in = PARAMS.index("w_in")
    for group in groups:
        group[k_in] = group[k_in].T
    grads, delta, new_m, new_v = groups
    return (loss, grad_x, *grads, *delta, *new_m, *new_v)
```

```python
import functools

import jax
import jax.numpy as jnp
from jax import lax
from jax.experimental import pallas as pl
from jax.experimental.pallas import tpu as pltpu

F32 = jnp.float32
BF16 = jnp.bfloat16

D_MODEL = 1024
W_MIX = 512
D_IN = 5 * W_MIX
POOL_WINDOWS = (2, 4, 8, 16)
POOL_GC = 128
CONV_WIDTH = 31
RMS_EPS = 1e-6
LN_EPS = 1e-5
N_DEV = 8

LANES = 128
SUBLANES = 8
TILE_M = 512
TILE_IN = 1024
POOL_HALO = 16
CONV_HALO = 32
CONV_ROWS = 128
BWD_CONV_ROWS = 64
BWD_SUBTILES = 2
VMEM_LIMIT = 56 * 1024 * 1024

VMEM = pl.BlockSpec(memory_space=pltpu.VMEM)


def _silu(v):
    return v * jax.nn.sigmoid(v)


def _colsum8(v):
    m, n = v.shape
    return jnp.sum(v.reshape(m // SUBLANES, SUBLANES, n), axis=0)


def _row_index(shape, t0):
    return lax.broadcasted_iota(jnp.int32, shape, 0) + t0


def _conv_taps():
    taps = {s: [] for s in range(SUBLANES)}
    for k in range(CONV_WIDTH):
        e = k + CONV_HALO - (CONV_WIDTH - 1)
        taps[e % SUBLANES].append((e // SUBLANES, k))
    return taps


def _row(v):
    return v.reshape(1, -1)


def _rest_body(x_ref, tgt_ref, bgate_ref, h1_ref, ya_ref, lg_ref, lb_ref, pww_ref, pwb_ref, wout_ref, fg_ref,
               ob_ref, dh_ref, loss_ref, gfg_ref, y_s):
    @pl.when((pl.program_id(0) == 0) & (pl.program_id(1) == 0))
    def _():
        loss_ref[...] = jnp.zeros_like(loss_ref)
        gfg_ref[...] = jnp.zeros_like(gfg_ref)

    h1 = h1_ref[0]
    mu = jnp.mean(h1, axis=-1, keepdims=True)
    xc = h1 - mu
    rstd = lax.rsqrt(jnp.mean(xc * xc, axis=-1, keepdims=True) + LN_EPS)
    h3 = _silu(xc * rstd * lg_ref[...] + lb_ref[...])
    ob = jnp.dot(h3.astype(BF16), pww_ref[...], preferred_element_type=F32) + pwb_ref[...]
    ob_ref[0] = ob
    y_s[:, 0:W_MIX] = ya_ref[0]
    y_s[:, W_MIX:] = (ob * _silu(bgate_ref[0])).astype(BF16)

    h = x_ref[0] + jnp.dot(y_s[...], wout_ref[...], preferred_element_type=F32)
    r2 = lax.rsqrt(jnp.mean(h * h, axis=-1, keepdims=True) + RMS_EPS)
    hr = h * r2
    err = hr * fg_ref[...] - tgt_ref[0]
    loss_ref[...] += jnp.sum(err * err) * (0.5 / D_MODEL)
    dout = err * (1.0 / D_MODEL)
    gfg_ref[...] += _colsum8(dout * hr)
    gy = dout * fg_ref[...]
    dh_ref[0] = r2 * (gy - hr * jnp.mean(gy * hr, axis=-1, keepdims=True))


def _rest(x, tgt, proj, h1, ya, ln_g, ln_b, pw_w_b, pw_b, w_out_b, final_g):
    bl, s, _ = x.shape
    tm = TILE_M
    tok = lambda n: pl.BlockSpec((1, tm, n), lambda b, i: (b, i, 0))
    acc = lambda n: pl.BlockSpec((SUBLANES, n), lambda b, i: (0, 0))
    b_gate = pl.BlockSpec((1, tm, W_MIX), lambda b, i: (b, i, 4))
    return pl.pallas_call(
        functools.partial(_rest_body),
        name="fwd",
        grid=(bl, s // tm),
        in_specs=[tok(D_MODEL), tok(D_MODEL), b_gate, tok(W_MIX), tok(W_MIX)] + [VMEM] * 6,
        out_specs=[tok(W_MIX), tok(D_MODEL), acc(LANES), acc(D_MODEL)],
        out_shape=[jax.ShapeDtypeStruct((bl, s, W_MIX), F32), jax.ShapeDtypeStruct((bl, s, D_MODEL), F32),
                   jax.ShapeDtypeStruct((SUBLANES, LANES), F32), jax.ShapeDtypeStruct((SUBLANES, D_MODEL), F32)],
        scratch_shapes=[pltpu.VMEM((tm, D_MODEL), BF16)],
        compiler_params=pltpu.CompilerParams(dimension_semantics=("arbitrary", "arbitrary"), vmem_limit_bytes=VMEM_LIMIT),
    )(x, tgt, proj, h1, ya, _row(ln_g), _row(ln_b), pw_w_b, _row(pw_b), w_out_b, _row(final_g))


NT_DIMS = (((1,), (1,)), ((), ()))
TN_DIMS = (((0,), (0,)), ((), ()))
V_POOL_B, V_POOL_SCALE, V_CONV_B, V_LN_G, V_LN_B, V_PW_B = range(6)


def _bwd_taps():
    taps = {s: [] for s in range(SUBLANES)}
    for k in range(CONV_WIDTH):
        e = CONV_WIDTH - 1 - k
        taps[e % SUBLANES].append((e // SUBLANES, k))
    return taps


def _bwd1_body(n_tiles, dh_ref, proj_ref, h1_ref, z_ref, ob_ref, dpool_ref, wout_ref, pww_ref, poolw_ref, pools_ref, dw_ref,
               lg_ref, lb_ref,
               dproj_ref, gwout_ref, gpww_ref, gpoolw_ref, gvec_ref, gdw_ref,
               vext, gext, gsh, y_s):
    b = pl.program_id(0)
    i = pl.program_id(1)
    tm = TILE_M
    t0 = (n_tiles - 1 - i) * tm

    @pl.when(i == 0)
    def _():
        vext[tm:, :] = jnp.zeros((POOL_HALO, W_MIX), F32)
        gext[tm:, :] = jnp.zeros((CONV_HALO, W_MIX), F32)

    @pl.when((b == 0) & (i == 0))
    def _():
        gwout_ref[...] = jnp.zeros_like(gwout_ref)
        gpww_ref[...] = jnp.zeros_like(gpww_ref)
        gpoolw_ref[...] = jnp.zeros_like(gpoolw_ref)
        gvec_ref[...] = jnp.zeros_like(gvec_ref)
        gdw_ref[...] = jnp.zeros_like(gdw_ref)

    sub = tm // BWD_SUBTILES

    def dense_phase(r0):
        rs = slice(r0, r0 + sub)
        dhb = dh_ref[0, rs, :].astype(BF16)
        dy_a = lax.dot_general(dhb, wout_ref[0:W_MIX, :], NT_DIMS, preferred_element_type=F32)
        dy_b = lax.dot_general(dhb, wout_ref[W_MIX:, :], NT_DIMS, preferred_element_type=F32)

        a_gate = proj_ref[0, rs, W_MIX:2 * W_MIX]
        sig_a = jax.nn.sigmoid(a_gate)
        silu_a = a_gate * sig_a
        z = z_ref[0, rs, :]
        out_a = z * pools_ref[...]
        y_s[rs, 0:W_MIX] = (out_a * silu_a).astype(BF16)
        d_out_a = dy_a * silu_a
        d_ag = dy_a * out_a * (sig_a * (1.0 + a_gate * (1.0 - sig_a)))
        dproj_ref[0, rs, W_MIX:2 * W_MIX] = d_ag.astype(BF16)
        gvec_ref[V_POOL_SCALE] += _colsum8(d_out_a * z)
        dz = d_out_a * pools_ref[...]
        gvec_ref[V_POOL_B] += _colsum8(dz)
        t = _row_index((sub, POOL_GC), t0 + r0)
        n_ext = sub + POOL_HALO
        for g, w in enumerate(POOL_WINDOWS):
            lanes = slice(g * POOL_GC, (g + 1) * POOL_GC)
            dzg = dz[:, lanes].astype(BF16)
            gpoolw_ref[g] += lax.dot_general(dpool_ref[0, rs, lanes], dzg, TN_DIMS, preferred_element_type=F32)
            dd = lax.dot_general(dzg, poolw_ref[g].astype(BF16), NT_DIMS, preferred_element_type=F32)
            vext[rs, lanes] = dd * (1.0 / jnp.minimum(t + 1, w).astype(F32))
            s = vext[r0:r0 + n_ext, lanes]
            sh = 1
            while sh < w:
                s = s + pltpu.roll(s, n_ext - sh, 0)
                sh *= 2
            dproj_ref[0, rs, lanes] = (s[0:sub] - dd).astype(BF16)

        b_gate = proj_ref[0, rs, 4 * W_MIX:5 * W_MIX]
        sig_b = jax.nn.sigmoid(b_gate)
        silu_b = b_gate * sig_b
        ob = ob_ref[0, rs, :]
        y_s[rs, W_MIX:] = (ob * silu_b).astype(BF16)
        d_ob = dy_b * silu_b
        dproj_ref[0, rs, 4 * W_MIX:5 * W_MIX] = (dy_b * ob * (sig_b * (1.0 + b_gate * (1.0 - sig_b)))).astype(BF16)
        gvec_ref[V_PW_B] += _colsum8(d_ob)
        gwout_ref[...] += lax.dot_general(y_s[rs, :], dhb, TN_DIMS, preferred_element_type=F32)

        h1 = h1_ref[0, rs, :]
        mu = jnp.mean(h1, axis=-1, keepdims=True)
        xc = h1 - mu
        rstd = lax.rsqrt(jnp.mean(xc * xc, axis=-1, keepdims=True) + LN_EPS)
        xhat = xc * rstd
        h2 = xhat * lg_ref[...] + lb_ref[...]
        sig2 = jax.nn.sigmoid(h2)
        d_obb = d_ob.astype(BF16)
        gpww_ref[...] += lax.dot_general((h2 * sig2).astype(BF16), d_obb, TN_DIMS, preferred_element_type=F32)
        dh2 = lax.dot_general(d_obb, pww_ref[...], NT_DIMS, preferred_element_type=F32) * (sig2 * (1.0 + h2 * (1.0 - sig2)))
        gvec_ref[V_LN_G] += _colsum8(dh2 * xhat)
        gvec_ref[V_LN_B] += _colsum8(dh2)
        dxh = dh2 * lg_ref[...]
        dh1 = rstd * (dxh - jnp.mean(dxh, axis=-1, keepdims=True) - xhat * jnp.mean(dxh * xhat, axis=-1, keepdims=True))
        gvec_ref[V_CONV_B] += _colsum8(dh1)
        gext[rs, :] = dh1

    taps = _bwd_taps()
    cr = BWD_CONV_ROWS
    win = cr + CONV_HALO - SUBLANES

    def conv_phase(r0):
        for c0 in range(r0, r0 + sub, cr):
            for l0 in range(0, W_MIX, LANES):
                lanes = slice(l0, l0 + LANES)
                b_val = proj_ref[0, c0:c0 + cr, 2 * W_MIX + l0:2 * W_MIX + l0 + LANES]
                sg = jax.nn.sigmoid(proj_ref[0, c0:c0 + cr, 3 * W_MIX + l0:3 * W_MIX + l0 + LANES])
                h0c = b_val * sg
                acc = jnp.zeros((cr, LANES), F32)
                for s in range(SUBLANES):
                    gsh[...] = gext[c0 + s:c0 + s + win, lanes]
                    for q, k in taps[s]:
                        gq = gsh[SUBLANES * q:SUBLANES * q + cr, :]
                        acc = acc + gq * dw_ref[k:k + 1, lanes]
                        gdw_ref[k, :, lanes] += _colsum8(h0c * gq)
                d_bval = acc * sg
                dproj_ref[0, c0:c0 + cr, 2 * W_MIX + l0:2 * W_MIX + l0 + LANES] = d_bval.astype(BF16)
                dproj_ref[0, c0:c0 + cr, 3 * W_MIX + l0:3 * W_MIX + l0 + LANES] = (d_bval * b_val * (1.0 - sg)).astype(BF16)

    starts = [k * sub for k in reversed(range(BWD_SUBTILES))]
    for r0 in starts:
        dense_phase(r0)
    for r0 in starts:
        conv_phase(r0)
    vext[tm:, :] = vext[0:POOL_HALO, :]
    gext[tm:, :] = gext[0:CONV_HALO, :]


def _bwd1(dh, proj, h1, z, ob, dpool, w_out_b, pw_w_b, pool_w, pool_scale, conv_dw_f, ln_g, ln_b):
    bl, s, _ = dh.shape
    tm = TILE_M
    nt = s // tm
    tok = lambda n: pl.BlockSpec((1, tm, n), lambda b, i: (b, nt - 1 - i, 0))
    res = lambda shape: pl.BlockSpec(shape, lambda b, i: (0,) * len(shape))
    out_shapes = [(D_MODEL, D_MODEL), (W_MIX, W_MIX), (len(POOL_WINDOWS), POOL_GC, POOL_GC), (6, SUBLANES, W_MIX),
                  (CONV_WIDTH, SUBLANES, W_MIX)]
    return pl.pallas_call(
        functools.partial(_bwd1_body, nt),
        name="bwd1",
        grid=(bl, nt),
        in_specs=[tok(D_MODEL), tok(D_IN), tok(W_MIX), tok(W_MIX), tok(W_MIX), tok(W_MIX)] + [VMEM] * 7,
        out_specs=[tok(D_IN)] + [res(sh) for sh in out_shapes],
        out_shape=[jax.ShapeDtypeStruct((bl, s, D_IN), BF16)] + [jax.ShapeDtypeStruct(sh, F32) for sh in out_shapes],
        scratch_shapes=[pltpu.VMEM((tm + POOL_HALO, W_MIX), F32), pltpu.VMEM((tm + CONV_HALO, W_MIX), F32),
                        pltpu.VMEM((BWD_CONV_ROWS + CONV_HALO - SUBLANES, LANES), F32), pltpu.VMEM((tm, D_MODEL), BF16)],
        compiler_params=pltpu.CompilerParams(dimension_semantics=("arbitrary", "arbitrary"), vmem_limit_bytes=VMEM_LIMIT),
    )(dh, proj, h1, z, ob, dpool, w_out_b, pw_w_b, pool_w, _row(pool_scale), conv_dw_f, _row(ln_g), _row(ln_b))


MESH = pl.DeviceIdType.MESH
W_IN_BLK = D_IN // N_DEV
W_OUT_BLK = D_MODEL // N_DEV
PW_BLK = W_MIX // N_DEV
DW_BLK = W_MIX // N_DEV
N_PEER = N_DEV - 1


SLAB = 2 * W_IN_BLK
N_SLABS = 4
F_SIB, F_X, F_Y, F_ON, F_PASS_X, F_PASS_Y, F_PASS_D = range(7)
FIRST_WAVE, SECOND_WAVE = (0, 3), (1, 2)


def _front_body(n_in, n_mix, x_ref, ng_ref, wint_ref, wout_ref, pww_ref, dw_ref, poolw_ref, poolb_ref, pools_ref,
                cb_ref,
                proj_hbm, wt_hbm, woutf_hbm, pwwf_hbm, dwf_ref, h1_ref, z_ref, dpool_ref, ya_ref,
                hn_s, wt_s, wout_s, pww_s, dw_blk, stage, pin, uext, hext,
                out_sems, in_sems, w_sems, send_sems, recv_sems):
    step = pl.program_id(0)
    tin, tm = TILE_IN, TILE_M
    slab_steps = N_SLABS * n_in
    s, i = step // n_in, step % n_in
    x, y, c = lax.axis_index("x"), lax.axis_index("y"), lax.axis_index("c")
    sibling = (x, y, 1 - c)
    xn, yn, dg = (1 - x, y), (x, 1 - y), (1 - x, 1 - y)
    pick = lambda a, b: tuple(jnp.where(c == 1, p, q) for p, q in zip(a, b))
    on_from, on_to = pick(xn, yn), pick(yn, xn)

    def blocks(px, py, pc):
        k = 4 * px + 2 * py + pc
        rows = lambda ref, n: ref.at[pl.ds(pl.multiple_of(k * n, n), n), :]
        return (rows(wt_s, W_IN_BLK), rows(wout_s, W_OUT_BLK), rows(pww_s, PW_BLK), dw_blk.at[k])

    def copies(arrays, flow, block, to):
        refs = blocks(*block)
        return [pltpu.make_async_remote_copy(src_ref=refs[a], dst_ref=refs[a], send_sem=send_sems.at[a, flow],
                                             recv_sem=recv_sems.at[a, flow], device_id=to, device_id_type=MESH)
                for a in arrays]

    def start(arrays):
        for flow, to in ((F_SIB, sibling), (F_X, (*xn, c)), (F_Y, (*yn, c))):
            for cp in copies(arrays, flow, (x, y, c), to):
                cp.start()

    def direct(arrays):
        for flow, chip in ((F_X, xn), (F_Y, yn)):
            for cp in copies(arrays, flow, (*chip, c), sibling):
                cp.wait_recv()
        for cp in copies(arrays, F_ON, (*on_from, c), (*on_to, c)):
            cp.start()
        for flow, chip in ((F_PASS_X, xn), (F_PASS_Y, yn)):
            for cp in copies(arrays, flow, (*chip, c), sibling):
                cp.start()

    def diagonal(arrays):
        for cp in copies(arrays, F_ON, (*dg, c), sibling):
            cp.wait_recv()
        for cp in copies(arrays, F_PASS_D, (*dg, c), sibling):
            cp.start()

    def from_sibling(arrays, flow, chip):
        for cp in copies(arrays, flow, (*chip, 1 - c), sibling):
            cp.wait_recv()

    def finish(arrays):
        mine = (x, y, c)
        for flow, block in ((F_SIB, mine), (F_X, mine), (F_Y, mine), (F_ON, (*on_from, c)), (F_PASS_X, (*xn, c)),
                            (F_PASS_Y, (*yn, c)), (F_PASS_D, (*dg, c))):
            for cp in copies(arrays, flow, block, sibling):
                cp.wait_send()

    par = step % 2
    first = step == 0
    in_per_seq, mix_per_seq = proj_hbm.shape[1] // tin, proj_hbm.shape[1] // tm
    rows_i = pl.ds(pl.multiple_of(i * tin, tin), tin)

    @pl.when(first)
    def _():
        mine = blocks(x, y, c)
        mine[0][...] = wint_ref[...].astype(BF16)
        mine[1][...] = wout_ref[...].astype(BF16)
        mine[2][...] = pww_ref[...].astype(BF16)
        mine[3][...] = dw_ref[...]
        start(FIRST_WAVE)

    @pl.when(s == 0)
    def _():
        xt = x_ref[0]
        r1 = lax.rsqrt(jnp.mean(xt * xt, axis=-1, keepdims=True) + RMS_EPS)
        hn_s[rows_i, :] = (xt * r1 * ng_ref[...]).astype(BF16)

    @pl.when(first)
    def _():
        from_sibling(FIRST_WAVE, F_SIB, (x, y))

    @pl.when((s == 1) & (i == 0))
    def _():
        direct(FIRST_WAVE)
        start(SECOND_WAVE)
        from_sibling(FIRST_WAVE, F_PASS_Y, yn)

    @pl.when((s == 2) & (i == 0))
    def _():
        from_sibling(FIRST_WAVE, F_PASS_X, xn)

    @pl.when((s == 3) & (i == 0))
    def _():
        diagonal(FIRST_WAVE)
        direct(SECOND_WAVE)
        from_sibling(FIRST_WAVE, F_PASS_D, dg)

    def out_copy(parity, col0):
        dst = proj_hbm.at[i // in_per_seq, pl.ds(pl.multiple_of((i % in_per_seq) * tin, tin), tin),
                          pl.ds(pl.multiple_of(col0, LANES), SLAB)]
        return pltpu.make_async_copy(stage.at[parity], dst, out_sems.at[parity])

    weights_out = [pltpu.make_async_copy(src, dst, w_sems.at[k])
                   for k, (src, dst) in enumerate(((wt_s, wt_hbm), (wout_s, woutf_hbm), (pww_s, pwwf_hbm)))]

    @pl.when(step < slab_steps)
    def _():
        @pl.when(step >= 2)
        def _():
            out_copy(par, 0).wait()

        chip = jnp.where(s < 2, 2 * x, 2 * (1 - x)) + jnp.where((s == 0) | (s == 2), y, 1 - y)
        slab = wt_s[pl.ds(pl.multiple_of(chip * SLAB, SLAB), SLAB), :]
        stage[par] = lax.dot_general(hn_s[rows_i, :], slab, NT_DIMS, preferred_element_type=F32)
        out_copy(par, chip * SLAB).start()

    m = step - slab_steps
    tile = m % mix_per_seq

    def in_copy(parity, k):
        src = proj_hbm.at[k // mix_per_seq, pl.ds(pl.multiple_of((k % mix_per_seq) * tm, tm), tm), pl.ds(0, 4 * W_MIX)]
        return pltpu.make_async_copy(src, pin.at[parity], in_sems.at[parity])

    @pl.when(m == 0)
    def _():
        out_copy(par, 0).wait()
        out_copy(1 - par, 0).wait()
        in_copy(par, 0).start()
        weights_out[0].start()
        for k in range(N_DEV):
            dwf_ref[:, k * DW_BLK:(k + 1) * DW_BLK] = dw_blk[k]

    @pl.when(m >= 0)
    def _():
        in_copy(par, m).wait()

        @pl.when(m + 1 < n_mix)
        def _():
            in_copy(1 - par, m + 1).start()

        @pl.when(tile == 0)
        def _():
            uext[0:POOL_HALO, :] = jnp.zeros((POOL_HALO, W_MIX), F32)
            hext[0:CONV_HALO, :] = jnp.zeros((CONV_HALO, W_MIX), F32)
            hext[CONV_HALO + tm:, :] = jnp.zeros((SUBLANES, W_MIX), F32)

        p = pin.at[par]
        uext[POOL_HALO:, :] = p[:, 0:W_MIX]
        t = _row_index((tm, POOL_GC), tile * tm)
        for g, w in enumerate(POOL_WINDOWS):
            lanes = slice(g * POOL_GC, (g + 1) * POOL_GC)
            e = uext[:, lanes]
            acc = e
            sh = 1
            while sh < w:
                acc = acc + pltpu.roll(acc, sh, 0)
                sh *= 2
            inv_cnt = 1.0 / jnp.minimum(t + 1, w).astype(F32)
            d = (acc[POOL_HALO:] * inv_cnt - e[POOL_HALO:]).astype(BF16)
            dpool_ref[0, :, lanes] = d
            z = jnp.dot(d, poolw_ref[g].astype(BF16), preferred_element_type=F32) + poolb_ref[:, lanes]
            z_ref[0, :, lanes] = z
            a_gate = p[:, W_MIX + g * POOL_GC:W_MIX + (g + 1) * POOL_GC]
            ya_ref[0, :, lanes] = (z * pools_ref[:, lanes] * _silu(a_gate)).astype(BF16)

        hext[CONV_HALO:CONV_HALO + tm, :] = p[:, 2 * W_MIX:3 * W_MIX] * jax.nn.sigmoid(p[:, 3 * W_MIX:4 * W_MIX])
        taps = _conv_taps()
        for c0 in range(0, tm, CONV_ROWS):
            for l0 in range(0, W_MIX, LANES):
                lanes = slice(l0, l0 + LANES)
                acc = jnp.zeros((CONV_ROWS, LANES), F32) + cb_ref[:, lanes]
                for sft in range(SUBLANES):
                    part = jnp.zeros((CONV_ROWS + SUBLANES, LANES), F32)
                    for q, k in taps[sft]:
                        r0 = c0 + SUBLANES * q
                        part = part + hext[r0:r0 + CONV_ROWS + SUBLANES, lanes] * dwf_ref[k:k + 1, lanes]
                    acc = acc + part[sft:sft + CONV_ROWS]
                h1_ref[0, c0:c0 + CONV_ROWS, lanes] = acc
        uext[0:POOL_HALO, :] = uext[tm:tm + POOL_HALO, :]
        hext[0:CONV_HALO, :] = hext[tm:tm + CONV_HALO, :]

    @pl.when(m == n_mix - 2)
    def _():
        diagonal(SECOND_WAVE)

    @pl.when(m == n_mix - 1)
    def _():
        from_sibling(SECOND_WAVE, F_SIB, (x, y))
        from_sibling(SECOND_WAVE, F_PASS_X, xn)
        from_sibling(SECOND_WAVE, F_PASS_Y, yn)
        from_sibling(SECOND_WAVE, F_PASS_D, dg)
        for cp in weights_out[1:]:
            cp.start()
        finish(FIRST_WAVE)
        finish(SECOND_WAVE)
        for cp in weights_out:
            cp.wait()


def _front(x, norm_g, w_in_t, w_out, pw_w, conv_dw, pool_w, pool_b, pool_scale, conv_b):
    bl, s, _ = x.shape
    tin, tm = TILE_IN, TILE_M
    in_per_seq, mix_per_seq = s // tin, s // tm
    n_in, n_mix = bl * in_per_seq, bl * mix_per_seq
    slab_steps = N_SLABS * n_in
    assert n_mix >= 2

    def x_map(step):
        i = jnp.minimum(step, n_in - 1)
        return (i // in_per_seq, i % in_per_seq, 0)

    def mix_map(step):
        k = jnp.maximum(step - slab_steps, 0)
        return (k // mix_per_seq, k % mix_per_seq, 0)

    dma = pltpu.SemaphoreType.DMA
    hbm = pl.BlockSpec(memory_space=pl.ANY)
    mix = pl.BlockSpec((1, tm, W_MIX), mix_map)
    act = lambda dt: jax.ShapeDtypeStruct((bl, s, W_MIX), dt)
    return pl.pallas_call(
        functools.partial(_front_body, n_in, n_mix),
        name="front",
        grid=(slab_steps + n_mix,),
        in_specs=[pl.BlockSpec((1, tin, D_MODEL), x_map)] + [VMEM] * 9,
        out_specs=[hbm, hbm, hbm, hbm, VMEM, mix, mix, mix, mix],
        out_shape=[jax.ShapeDtypeStruct((bl, s, D_IN), F32), jax.ShapeDtypeStruct((D_IN, D_MODEL), BF16),
                   jax.ShapeDtypeStruct((D_MODEL, D_MODEL), BF16), jax.ShapeDtypeStruct((W_MIX, W_MIX), BF16),
                   jax.ShapeDtypeStruct((CONV_WIDTH, W_MIX), F32), act(F32), act(F32), act(BF16), act(BF16)],
        scratch_shapes=[pltpu.VMEM((bl * s, D_MODEL), BF16), pltpu.VMEM((D_IN, D_MODEL), BF16),
                        pltpu.VMEM((D_MODEL, D_MODEL), BF16), pltpu.VMEM((W_MIX, W_MIX), BF16),
                        pltpu.VMEM((N_DEV, CONV_WIDTH, DW_BLK), F32), pltpu.VMEM((2, tin, SLAB), F32),
                        pltpu.VMEM((2, tm, 4 * W_MIX), F32), pltpu.VMEM((POOL_HALO + tm, W_MIX), F32),
                        pltpu.VMEM((CONV_HALO + tm + SUBLANES, W_MIX), F32),
                        dma((2,)), dma((2,)), dma((3,)), dma((4, 7)), dma((4, 7))],
        compiler_params=pltpu.CompilerParams(dimension_semantics=("arbitrary",), vmem_limit_bytes=VMEM_LIMIT),
    )(x, _row(norm_g), w_in_t, w_out, pw_w, conv_dw, pool_w, pool_b.reshape(1, W_MIX), _row(pool_scale), _row(conv_b))


B_PLAIN, B_ON, B_MERGED = range(3)


def _routes():
    x, y, c = lax.axis_index("x"), lax.axis_index("y"), lax.axis_index("c")
    xn, yn = (1 - x, y), (x, 1 - y)
    pick = lambda a, b: tuple(jnp.where(c == 1, p, q) for p, q in zip(a, b))
    return pick(xn, yn), pick(yn, xn), (1 - x, 1 - y), (x, y), c


def _chip_index(chip):
    return 2 * chip[0] + chip[1]


def _copy(src, dst, send_sems, recv_sems, a, k, to):
    return pltpu.make_async_remote_copy(src_ref=src, dst_ref=dst, send_sem=send_sems.at[a, k], recv_sem=recv_sems.at[a, k],
                                        device_id=to, device_id_type=MESH)


def _rsb_send(plain, on, rb_plain, rb_on, send_b, recv_b):
    first, _, _, _, c = _routes()
    for k, srcs, dsts in ((B_ON, on, rb_on), (B_PLAIN, plain, rb_plain)):
        for a, (s, d) in enumerate(zip(srcs, dsts)):
            _copy(s, d, send_b, recv_b, a, k, (*first, c)).start()


def _rsb_merge(second_sums, merged, rb_on, rb_merged, send_b, recv_b):
    _, second, _, _, c = _routes()
    for a, (own, m, got, dst) in enumerate(zip(second_sums, merged, rb_on, rb_merged)):
        _copy(got, got, send_b, recv_b, a, B_ON, (*second, c)).wait_recv()
        m[...] = (own[...] + got[...].astype(F32)).astype(BF16)
        _copy(m, dst, send_b, recv_b, a, B_MERGED, (*second, c)).start()


def _rsb_total(own_sums, plain, on, merged, rb_plain, rb_merged, send_b, recv_b):
    _, second, _, _, c = _routes()
    out = []
    for a, (own, p, g) in enumerate(zip(own_sums, rb_plain, rb_merged)):
        _copy(p, p, send_b, recv_b, a, B_PLAIN, (*second, c)).wait_recv()
        _copy(g, g, send_b, recv_b, a, B_MERGED, (*second, c)).wait_recv()
        out.append(own[...] + p[...].astype(F32) + g[...].astype(F32))
    for k, srcs in ((B_PLAIN, plain), (B_ON, on), (B_MERGED, merged)):
        for a, s in enumerate(srcs):
            _copy(s, s, send_b, recv_b, a, k, (*second, c)).wait_send()
    return out


R_LOSS, R_FINAL_G, R_POOL, R_CONV, R_LN = range(5)
VEC_ROWS = 16
DW_ROWS = 32
A_PART, A_X, A_Y, A_ON, A_HALF = range(5)
STEP_SUMS, STEP_MERGE, STEP_HALF = 1, 3, 5


def _bwd2a_body(n_steps, x_ref, dproj_ref, ng_ref, gwout_ref, gpww_ref, loss_ref, gfg_ref, gvec_ref, gdw_ref, gpoolw_ref,
                win_own_ref, win_staged_ref, owout_ref, opww_ref, ovec_ref, odw_ref, opoolw_ref,
                acc, st_win, a_win, a_wout, a_pww, st_wout, st_pww, rb_wout, rb_pww, part_vec, part_dw, part_pw,
                sib_vec, sib_dw, sib_pw, res_vec, res_dw, res_pw, half_vec, half_dw, half_pw,
                send_a, recv_a, send_b, recv_b, send_r, recv_r, send_w, recv_w):
    step = pl.program_id(0) * pl.num_programs(1) + pl.program_id(1)
    x, y, c = lax.axis_index("x"), lax.axis_index("y"), lax.axis_index("c")
    sibling = (x, y, 1 - c)
    xn, yn = (1 - x, y), (x, 1 - y)
    first, second, diag, own, _ = _routes()
    rows = lambda ref, k, n: ref.at[pl.ds(pl.multiple_of(k * n, n), n), :]

    blocks = lambda k: (rows(gwout_ref, k, W_OUT_BLK), rows(gpww_ref, k, PW_BLK))
    landed = lambda chip: (a_wout.at[chip], a_pww.at[chip])
    staged = lambda k: (st_wout.at[k], st_pww.at[k])
    arrived = lambda k: (rb_wout.at[k], rb_pww.at[k])

    parts, sibs, halves = (part_vec, part_dw, part_pw), (sib_vec, sib_dw, sib_pw), (half_vec, half_dw, half_pw)
    results = (res_vec, res_dw, res_pw)
    half_rows = lambda ref: pl.ds(pl.multiple_of(c * (ref.shape[0] // 2), SUBLANES), ref.shape[0] // 2)
    slot = lambda chip: tuple(h.at[_chip_index(chip)] for h in halves)

    @pl.when(step == 0)
    def _():
        acc[...] = jnp.zeros_like(acc)
        sum8 = lambda v: jnp.sum(v, axis=0, keepdims=True)
        part_vec[...] = jnp.zeros((VEC_ROWS, D_MODEL), F32)
        part_vec[R_LOSS:R_LOSS + 1, :] = jnp.broadcast_to(loss_ref[0:1, 0:1], (1, D_MODEL))
        part_vec[R_FINAL_G:R_FINAL_G + 1, :] = sum8(gfg_ref[...])
        for row, (lo, hi) in ((R_POOL, (V_POOL_B, V_POOL_SCALE)), (R_CONV, (V_CONV_B, V_LN_G)), (R_LN, (V_LN_B, V_PW_B))):
            part_vec[row:row + 1, 0:W_MIX] = sum8(gvec_ref[lo])
            part_vec[row:row + 1, W_MIX:] = sum8(gvec_ref[hi])
        part_dw[0:CONV_WIDTH, :] = jnp.sum(gdw_ref[...], axis=1)
        part_dw[CONV_WIDTH:, :] = jnp.zeros((DW_ROWS - CONV_WIDTH, W_MIX), F32)
        part_pw[...] = gpoolw_ref[...].reshape(len(POOL_WINDOWS) * POOL_GC, POOL_GC)
        for a, (p, s) in enumerate(zip(parts, sibs)):
            _copy(p, s, send_r, recv_r, a, A_PART, sibling).start()
        for chip in range(4):
            for a, (s, d) in enumerate(zip(blocks(2 * chip + (1 - c)), landed(chip))):
                _copy(s, d, send_a, recv_a, a, chip, sibling).start()

    x_t = x_ref[0]
    r1 = lax.rsqrt(jnp.mean(x_t * x_t, axis=-1, keepdims=True) + RMS_EPS)
    hn = (x_t * r1 * ng_ref[...]).astype(BF16)
    acc[...] += lax.dot_general(dproj_ref[0], hn, TN_DIMS, preferred_element_type=F32)

    @pl.when(step == STEP_SUMS)
    def _():
        for a, (p, s, h) in enumerate(zip(parts, sibs, slot(own))):
            _copy(p, s, send_r, recv_r, a, A_PART, sibling).wait_recv()
            h[...] = p[half_rows(p), :] + s[half_rows(s), :]
            for k, to in ((A_X, xn), (A_Y, yn)):
                _copy(h, h, send_r, recv_r, a, k, (*to, c)).start()
        for chip in range(4):
            for a, (m, l) in enumerate(zip(blocks(2 * chip + c), landed(chip))):
                _copy(l, l, send_a, recv_a, a, chip, sibling).wait_recv()
                l[...] = m[...] + l[...]
        for k, chip in ((B_ON, diag), (B_PLAIN, first)):
            for s, l in zip(staged(k), landed(_chip_index(chip))):
                s[...] = l[...].astype(BF16)
        _rsb_send(staged(B_PLAIN), staged(B_ON), arrived(B_PLAIN), arrived(B_ON), send_b, recv_b)

    @pl.when(step == STEP_MERGE)
    def _():
        _rsb_merge(landed(_chip_index(second)), staged(B_MERGED), arrived(B_ON), arrived(B_MERGED), send_b, recv_b)
        for a in range(3):
            for k, chip in ((A_X, xn), (A_Y, yn)):
                h = slot(chip)[a]
                _copy(h, h, send_r, recv_r, a, k, sibling).wait_recv()
            h = slot(first)[a]
            _copy(h, h, send_r, recv_r, a, A_ON, (*second, c)).start()

    @pl.when(step == STEP_HALF)
    def _():
        for a, (r, hs) in enumerate(zip(results, halves)):
            h = slot(diag)[a]
            _copy(h, h, send_r, recv_r, a, A_ON, sibling).wait_recv()
            mine = r.at[half_rows(r), :]
            mine[...] = (hs[0] + hs[1]) + (hs[2] + hs[3])
            _copy(mine, mine, send_r, recv_r, a, A_HALF, sibling).start()

    @pl.when(step == n_steps - 1)
    def _():
        block = lambda k: rows(acc, k, W_IN_BLK)
        w_copy = lambda chip: _copy(st_win.at[chip], a_win.at[chip], send_w, recv_w, 0, chip, sibling)
        for chip in range(4):
            st_win[chip] = block(2 * chip + (1 - c))[...].astype(BF16)
            w_copy(chip).start()

        owout_ref[...], opww_ref[...] = _rsb_total(landed(_chip_index(own)), staged(B_PLAIN), staged(B_ON), staged(B_MERGED),
                                                   arrived(B_PLAIN), arrived(B_MERGED), send_b, recv_b)
        for a, (r, p, s) in enumerate(zip(results, parts, sibs)):
            theirs = r.at[pl.ds(pl.multiple_of((1 - c) * (r.shape[0] // 2), SUBLANES), r.shape[0] // 2), :]
            _copy(theirs, theirs, send_r, recv_r, a, A_HALF, sibling).wait_recv()
            _copy(p, s, send_r, recv_r, a, A_PART, sibling).wait_send()
            for k, h in ((A_X, slot(own)[a]), (A_Y, slot(own)[a]), (A_ON, slot(first)[a])):
                _copy(h, h, send_r, recv_r, a, k, sibling).wait_send()
            mine = r.at[half_rows(r), :]
            _copy(mine, mine, send_r, recv_r, a, A_HALF, sibling).wait_send()
        for chip in range(4):
            for a, (s, d) in enumerate(zip(blocks(2 * chip + (1 - c)), landed(chip))):
                _copy(s, d, send_a, recv_a, a, chip, sibling).wait_send()
        ovec_ref[...] = res_vec[...]
        opoolw_ref[...] = res_pw[...]
        for k in range(N_DEV):
            @pl.when(4 * x + 2 * y + c == k)
            def _():
                odw_ref[...] = res_dw[0:CONV_WIDTH, k * DW_BLK:(k + 1) * DW_BLK]

        for chip in range(4):
            w_copy(chip).wait()
        chip_sum = lambda chip: block(2 * _chip_index(chip) + c)[...] + a_win[_chip_index(chip)].astype(F32)
        win_own_ref[...] = chip_sum(own)
        for j, chip in enumerate((xn, yn, diag)):
            win_staged_ref[j] = chip_sum(chip).astype(BF16)


def _bwd2a(x, dproj, norm_g, gwout, gpww, loss8, gfg8, gvec, gdw8, gpoolw):
    bl, s, _ = x.shape
    tm = TILE_M
    nt = s // tm
    tok = lambda n: pl.BlockSpec((1, tm, n), lambda b, i: (b, i, 0))
    n_pw = len(POOL_WINDOWS) * POOL_GC
    win = (W_IN_BLK, D_MODEL)
    blk = ((W_OUT_BLK, D_MODEL), (PW_BLK, W_MIX))
    rep = ((VEC_ROWS, D_MODEL), (DW_ROWS, W_MIX), (n_pw, POOL_GC))
    half = lambda sh: (sh[0] // 2,) + sh[1:]
    dma = pltpu.SemaphoreType.DMA
    assert bl * nt > STEP_HALF + 1, "the exchanges' phases need their grid steps"
    return pl.pallas_call(
        functools.partial(_bwd2a_body, bl * nt),
        name="bwd2a",
        grid=(bl, nt),
        in_specs=[tok(D_MODEL), tok(D_IN)] + [VMEM] * 8,
        out_specs=[VMEM] * 7,
        out_shape=[jax.ShapeDtypeStruct(win, F32), jax.ShapeDtypeStruct((3,) + win, BF16)]
        + [jax.ShapeDtypeStruct(sh, F32) for sh in blk + (rep[0], (CONV_WIDTH, DW_BLK), rep[2])],
        scratch_shapes=[pltpu.VMEM((D_IN, D_MODEL), F32), pltpu.VMEM((4,) + win, BF16), pltpu.VMEM((4,) + win, BF16)]
        + [pltpu.VMEM((4,) + sh, F32) for sh in blk]
        + [pltpu.VMEM((3,) + sh, BF16) for sh in blk] * 2
        + [pltpu.VMEM(sh, F32) for sh in rep] * 3
        + [pltpu.VMEM((4,) + half(sh), F32) for sh in rep]
        + [dma((2, 4)), dma((2, 4)), dma((2, 3)), dma((2, 3)), dma((3, 5)), dma((3, 5)), dma((1, 4)), dma((1, 4))],
        compiler_params=pltpu.CompilerParams(dimension_semantics=("arbitrary", "arbitrary"), vmem_limit_bytes=VMEM_LIMIT),
    )(x, dproj, _row(norm_g), gwout, gpww, loss8, gfg8, gvec, gdw8, gpoolw)


HBM = pl.BlockSpec(memory_space=pltpu.HBM)
SEM = pl.BlockSpec(memory_space=pltpu.SEMAPHORE)
EFFECT = pltpu.SideEffectType.DATAFLOW_SIDE_EFFECTING


def _owner_copies(src_ref, land_ref, send_sems, recv_sems):
    x, y, c = lax.axis_index("x"), lax.axis_index("y"), lax.axis_index("c")
    owners = ((1 - x, y, c), (x, 1 - y, c), (1 - x, 1 - y, c))
    return [pltpu.make_async_remote_copy(src_ref=src_ref.at[j], dst_ref=land_ref.at[j], send_sem=send_sems.at[j],
                                         recv_sem=recv_sems.at[j], device_id=owners[j], device_id_type=MESH)
            for j in range(3)]


def _rs_start(staged):
    def body(src_ref, land_ref, send_sems, recv_sems, src_thru, land_thru, token):
        for cp in _owner_copies(src_ref, land_ref, send_sems, recv_sems):
            cp.start()
        token[...] = jnp.zeros_like(token)

    dma = pltpu.SemaphoreType.DMA
    return pl.pallas_call(
        body, name="rs_start",
        out_shape=(dma((3,)), dma((3,)), pltpu.HBM(staged.shape, staged.dtype), pltpu.HBM(staged.shape, staged.dtype),
                   jax.ShapeDtypeStruct((SUBLANES, LANES), F32)),
        in_specs=(HBM, HBM), out_specs=(SEM, SEM, HBM, HBM, VMEM), input_output_aliases={0: 2, 1: 3},
        compiler_params=pltpu.CompilerParams(has_side_effects=EFFECT),
    )(pltpu.with_memory_space_constraint(staged, pltpu.HBM),
      pltpu.with_memory_space_constraint(lax.empty(staged.shape, staged.dtype), pltpu.HBM))


def _rs_wait(send_sems, recv_sems, src_thru, land_thru, after):
    def body(src_ref, land_ref, send_sems, recv_sems, after_ref, src_dead, got_ref):
        for cp in _owner_copies(src_ref, land_ref, send_sems, recv_sems):
            cp.wait_send()
            cp.wait_recv()

    return pl.pallas_call(
        body, name="rs_wait",
        out_shape=(pltpu.HBM(src_thru.shape, src_thru.dtype), pltpu.HBM(land_thru.shape, land_thru.dtype)),
        in_specs=(HBM, HBM, SEM, SEM, pl.BlockSpec(memory_space=pl.ANY)), out_specs=(HBM, HBM),
        input_output_aliases={0: 0, 1: 1},
        compiler_params=pltpu.CompilerParams(has_side_effects=EFFECT),
    )(src_thru, land_thru, send_sems, recv_sems, after)[1]


def _bwd2b_body(x_ref, dh_ref, dproj_ref, ng_ref, wint_ref, token_ref, gx_ref, gng_ref):
    @pl.when((pl.program_id(0) == 0) & (pl.program_id(1) == 0))
    def _():
        gng_ref[...] = jnp.zeros_like(gng_ref) + token_ref[0:1, 0:1]

    x_t = x_ref[0]
    r1 = lax.rsqrt(jnp.mean(x_t * x_t, axis=-1, keepdims=True) + RMS_EPS)
    xr = x_t * r1
    dhn = jnp.dot(dproj_ref[0], wint_ref[...], preferred_element_type=F32)
    gng_ref[...] += _colsum8(dhn * xr)
    gy = dhn * ng_ref[...]
    gx_ref[0] = dh_ref[0] + r1 * (gy - xr * jnp.mean(gy * xr, axis=-1, keepdims=True))


def _bwd2b(x, dh, dproj, norm_g, w_in_t, token):
    bl, s, _ = x.shape
    tm = TILE_M
    tok = lambda n: pl.BlockSpec((1, tm, n), lambda b, i: (b, i, 0))
    return pl.pallas_call(
        functools.partial(_bwd2b_body),
        name="bwd2b",
        grid=(bl, s // tm),
        in_specs=[tok(D_MODEL), tok(D_MODEL), tok(D_IN), VMEM, VMEM, VMEM],
        out_specs=[tok(D_MODEL), pl.BlockSpec((SUBLANES, D_MODEL), lambda b, i: (0, 0))],
        out_shape=[jax.ShapeDtypeStruct((bl, s, D_MODEL), F32), jax.ShapeDtypeStruct((SUBLANES, D_MODEL), F32)],
        compiler_params=pltpu.CompilerParams(dimension_semantics=("arbitrary", "arbitrary"), vmem_limit_bytes=VMEM_LIMIT),
    )(x, dh, dproj, _row(norm_g), w_in_t, token)


def _partial_copies(part_ref, land_ref, send_sems, recv_sems):
    x, y, c = lax.axis_index("x"), lax.axis_index("y"), lax.axis_index("c")
    other_chips = [(1 - x, y), (x, 1 - y), (1 - x, 1 - y)]
    peers = [(x, y, 1 - c)] + [(*chip, c) for chip in other_chips] + [(*chip, 1 - c) for chip in other_chips]
    return [pltpu.make_async_remote_copy(src_ref=part_ref, dst_ref=land_ref.at[4 * x + 2 * y + c], send_sem=send_sems.at[r],
                                         recv_sem=recv_sems.at[r], device_id=peers[r], device_id_type=MESH)
            for r in range(N_PEER)]


def _partials_start(part):
    def body(part_ref, land_ref, send_sems, recv_sems, part_thru, land_thru):
        for cp in _partial_copies(part_ref, land_ref, send_sems, recv_sems):
            cp.start()

    dma = pltpu.SemaphoreType.DMA
    land = (N_DEV,) + part.shape
    return pl.pallas_call(
        body, name="partials_start",
        out_shape=(dma((N_PEER,)), dma((N_PEER,)), pltpu.HBM(part.shape, part.dtype), pltpu.HBM(land, part.dtype)),
        in_specs=(HBM, HBM), out_specs=(SEM, SEM, HBM, HBM), input_output_aliases={0: 2, 1: 3},
        compiler_params=pltpu.CompilerParams(has_side_effects=EFFECT),
    )(pltpu.with_memory_space_constraint(part, pltpu.HBM),
      pltpu.with_memory_space_constraint(lax.empty(land, part.dtype), pltpu.HBM))


def _partials_wait(send_sems, recv_sems, part_thru, land_thru):
    def body(part_ref, land_ref, send_sems, recv_sems, part_out, land_out):
        for cp in _partial_copies(part_ref, land_ref, send_sems, recv_sems):
            cp.wait_send()
            cp.wait_recv()

    return pl.pallas_call(
        body, name="partials_wait",
        out_shape=(pltpu.HBM(part_thru.shape, part_thru.dtype), pltpu.HBM(land_thru.shape, land_thru.dtype)),
        in_specs=(HBM, HBM, SEM, SEM), out_specs=(HBM, HBM), input_output_aliases={0: 0, 1: 1},
        compiler_params=pltpu.CompilerParams(has_side_effects=EFFECT),
    )(part_thru, land_thru, send_sems, recv_sems)


ADAM_LR = 0.001
ADAM_B1 = 0.9
ADAM_B2 = 0.999
ADAM_EPS = 1e-08
ADAM_WD = 0.01
ADAM_STEP = 10


PARAMS = ("norm_g", "w_in", "pool_w", "pool_b", "pool_scale", "conv_dw", "conv_b", "ln_g", "ln_b", "pw_w", "pw_b", "w_out",
          "final_g")


def _adamw_body(*refs):
    n = len(PARAMS)
    ws, ms, vs = (dict(zip(PARAMS, refs[k * n:(k + 1) * n])) for k in range(3))
    gwin_ref, gwin_others_ref, gwout_ref, gpww_ref, vec_ref, gdw_ref, gpoolw_ref, gng_ref, gng_others_ref = refs[3 * n:3 * n + 9]
    outs = refs[3 * n + 9:]
    gs, ds, m2s, v2s = (dict(zip(PARAMS, outs[k * n:(k + 1) * n])) for k in range(4))
    loss_ref = outs[4 * n]

    def update(name, g, at=slice(None)):
        m = ADAM_B1 * ms[name][at] + (1.0 - ADAM_B1) * g
        v = ADAM_B2 * vs[name][at] + (1.0 - ADAM_B2) * (g * g)
        m_hat = m / (1.0 - ADAM_B1 ** ADAM_STEP)
        v_hat = v / (1.0 - ADAM_B2 ** ADAM_STEP)
        gs[name][at] = g
        ds[name][at] = -ADAM_LR * (m_hat / (jnp.sqrt(v_hat) + ADAM_EPS) + ADAM_WD * ws[name][at])
        m2s[name][at] = m
        v2s[name][at] = v

    row = lambda r, lo, hi: vec_ref[r:r + 1, lo:hi]
    loss_ref[...] = row(R_LOSS, 0, 1)
    me = 4 * lax.axis_index("x") + 2 * lax.axis_index("y") + lax.axis_index("c")
    g_norm = jnp.zeros(gng_ref.shape, F32)
    for k in range(N_DEV):
        g_norm = g_norm + jnp.where(me == k, gng_ref[...], gng_others_ref[k])
    update("norm_g", jnp.sum(g_norm, axis=0, keepdims=True))
    g_w_in = gwin_ref[...]
    for j in range(3):
        g_w_in = g_w_in + gwin_others_ref[j].astype(F32)
    update("w_in", g_w_in)
    update("pool_w", gpoolw_ref[...].reshape(ws["pool_w"].shape))
    for g in range(len(POOL_WINDOWS)):
        update("pool_b", row(R_POOL, g * POOL_GC, (g + 1) * POOL_GC), at=slice(g, g + 1))
    update("pool_scale", row(R_POOL, W_MIX, 2 * W_MIX))
    update("conv_dw", gdw_ref[...])
    update("conv_b", row(R_CONV, 0, W_MIX))
    update("ln_g", row(R_CONV, W_MIX, 2 * W_MIX))
    update("ln_b", row(R_LN, 0, W_MIX))
    update("pw_w", gpww_ref[...])
    update("pw_b", row(R_LN, W_MIX, 2 * W_MIX))
    update("w_out", gwout_ref[...])
    update("final_g", row(R_FINAL_G, 0, D_MODEL))


def _adamw(ws, ms, vs, grad_sources):
    n = len(PARAMS)
    shapes = [ws[p].shape for p in PARAMS]
    flat = lambda a: a.reshape(1, -1) if a.ndim == 1 else a
    args = [flat(d[p]) for d in (ws, ms, vs) for p in PARAMS]
    outs = pl.pallas_call(
        functools.partial(_adamw_body),
        name="adamw",
        in_specs=[VMEM] * (3 * n + len(grad_sources)),
        out_specs=[VMEM] * (4 * n + 1),
        out_shape=[jax.ShapeDtypeStruct(a.shape, F32) for a in args[:n]] * 4 + [jax.ShapeDtypeStruct((1, 1), F32)],
        compiler_params=pltpu.CompilerParams(vmem_limit_bytes=VMEM_LIMIT),
    )(*args, *grad_sources)
    loss = outs[4 * n].reshape(())
    outs = [o.reshape(shapes[k % n]) for k, o in enumerate(outs[:4 * n])]
    return loss, outs[:n], outs[n:2 * n], outs[2 * n:3 * n], outs[3 * n:]


def kernel(x, norm_g, w_in, pool_w, pool_b, pool_scale, conv_dw, conv_b, ln_g, ln_b, pw_w, pw_b, w_out, final_g, loss_target, m_norm_g, m_w_in, m_pool_w, m_pool_b, m_pool_scale, m_conv_dw, m_conv_b, m_ln_g, m_ln_b, m_pw_w, m_pw_b, m_w_out, m_final_g, v_norm_g, v_w_in, v_pool_w, v_pool_b, v_pool_scale, v_conv_dw, v_conv_b, v_ln_g, v_ln_b, v_pw_w, v_pw_b, v_w_out, v_final_g):
    proj, w_in_t, w_out_b, pw_w_b, conv_dw_f, h1, z, dpool, ya = _front(x, norm_g, w_in.T, w_out, pw_w, conv_dw, pool_w, pool_b,
                                                                       pool_scale, conv_b)
    ob, dh, loss8, gfg8 = _rest(x, loss_target, proj, h1, ya, ln_g, ln_b, pw_w_b, pw_b, w_out_b, final_g)
    dproj, gwout, gpww, gpoolw, gvec, gdw8 = _bwd1(dh, proj, h1, z, ob, dpool, w_out_b, pw_w_b, pool_w, pool_scale,
                                                   conv_dw_f, ln_g, ln_b)
    win_own, win_staged, g_w_out, g_pw_w, vec, g_conv_dw, poolw_sum = _bwd2a(x, dproj, norm_g, gwout, gpww, loss8, gfg8, gvec,
                                                                             gdw8, gpoolw)
    send_sems, recv_sems, staged_thru, land_thru, token = _rs_start(win_staged)
    grad_x, gng8 = _bwd2b(x, dh, dproj, norm_g, w_in_t, token)
    ng_send, ng_recv, gng8, gng_land = _partials_start(gng8)
    win_others = _rs_wait(send_sems, recv_sems, staged_thru, land_thru, gng8)
    gng8, gng_others = _partials_wait(ng_send, ng_recv, gng8, gng_land)

    weights = dict(norm_g=norm_g, w_in=w_in.T, pool_w=pool_w, pool_b=pool_b, pool_scale=pool_scale, conv_dw=conv_dw, conv_b=conv_b,
                   ln_g=ln_g, ln_b=ln_b, pw_w=pw_w, pw_b=pw_b, w_out=w_out, final_g=final_g)
    m_in = dict(norm_g=m_norm_g, w_in=m_w_in.T, pool_w=m_pool_w, pool_b=m_pool_b, pool_scale=m_pool_scale, conv_dw=m_conv_dw,
                conv_b=m_conv_b, ln_g=m_ln_g, ln_b=m_ln_b, pw_w=m_pw_w, pw_b=m_pw_b, w_out=m_w_out, final_g=m_final_g)
    v_in = dict(norm_g=v_norm_g, w_in=v_w_in.T, pool_w=v_pool_w, pool_b=v_pool_b, pool_scale=v_pool_scale, conv_dw=v_conv_dw,
                conv_b=v_conv_b, ln_g=v_ln_g, ln_b=v_ln_b, pw_w=v_pw_w, pw_b=v_pw_b, w_out=v_w_out, final_g=v_final_g)
    loss, *groups = _adamw(weights, m_in, v_in, (win_own, win_others, g_w_out, g_pw_w, vec, g_conv_dw, poolw_sum, gng8,
                                                 gng_others))
    k_in = PARAMS.index("w_in")
    for group in groups:
        group[k_in] = group[k_in].T
    grads, delta, new_m, new_v = groups
    return (loss, grad_x, *grads, *delta, *new_m, *new_v)
```

```python
import functools

import jax
import jax.numpy as jnp
from jax import lax
from jax.experimental import pallas as pl
from jax.experimental.pallas import tpu as pltpu

F32 = jnp.float32
BF16 = jnp.bfloat16

D_MODEL = 1024
W_MIX = 512
D_IN = 5 * W_MIX
POOL_WINDOWS = (2, 4, 8, 16)
POOL_GC = 128
CONV_WIDTH = 31
RMS_EPS = 1e-6
LN_EPS = 1e-5
N_DEV = 8

LANES = 128
SUBLANES = 8
TILE_M = 512
TILE_IN = 1024
POOL_HALO = 16
CONV_HALO = 32
CONV_ROWS = 128
BWD_CONV_ROWS = 64
BWD_SUBTILES = 2
VMEM_LIMIT = 56 * 1024 * 1024

VMEM = pl.BlockSpec(memory_space=pltpu.VMEM)


def _silu(v):
    return v * jax.nn.sigmoid(v)


def _colsum8(v):
    m, n = v.shape
    return jnp.sum(v.reshape(m // SUBLANES, SUBLANES, n), axis=0)


def _row_index(shape, t0):
    return lax.broadcasted_iota(jnp.int32, shape, 0) + t0


def _conv_taps():
    taps = {s: [] for s in range(SUBLANES)}
    for k in range(CONV_WIDTH):
        e = k + CONV_HALO - (CONV_WIDTH - 1)
        taps[e % SUBLANES].append((e // SUBLANES, k))
    return taps


def _fwd_body(x_ref, tgt_ref, proj_ref, poolw_ref, poolb_ref, pools_ref, dw_ref, cb_ref, lg_ref, lb_ref,
              pww_ref, pwb_ref, wout_ref, fg_ref,
              h1_ref, z_ref, ob_ref, dpool_ref, dh_ref, loss_ref, gfg_ref,
              uext, hext, y_s):
    b = pl.program_id(0)
    i = pl.program_id(1)
    tm = TILE_M

    @pl.when(i == 0)
    def _():
        uext[0:POOL_HALO, :] = jnp.zeros((POOL_HALO, W_MIX), F32)
        hext[0:CONV_HALO, :] = jnp.zeros((CONV_HALO, W_MIX), F32)
        hext[CONV_HALO + tm:, :] = jnp.zeros((SUBLANES, W_MIX), F32)

    @pl.when((b == 0) & (i == 0))
    def _():
        loss_ref[...] = jnp.zeros_like(loss_ref)
        gfg_ref[...] = jnp.zeros_like(gfg_ref)

    uext[POOL_HALO:, :] = proj_ref[0, :, 0:W_MIX]
    t = _row_index((tm, POOL_GC), i * tm)
    for g, w in enumerate(POOL_WINDOWS):
        lanes = slice(g * POOL_GC, (g + 1) * POOL_GC)
        e = uext[:, lanes]
        s = e
        sh = 1
        while sh < w:
            s = s + pltpu.roll(s, sh, 0)
            sh *= 2
        inv_cnt = 1.0 / jnp.minimum(t + 1, w).astype(F32)
        d = (s[POOL_HALO:] * inv_cnt - e[POOL_HALO:]).astype(BF16)
        dpool_ref[0, :, lanes] = d
        z = jnp.dot(d, poolw_ref[g].astype(BF16), preferred_element_type=F32) + poolb_ref[:, lanes]
        z_ref[0, :, lanes] = z
        a_gate = proj_ref[0, :, W_MIX + g * POOL_GC:W_MIX + (g + 1) * POOL_GC]
        y_s[:, lanes] = (z * pools_ref[:, lanes] * _silu(a_gate)).astype(BF16)

    hext[CONV_HALO:CONV_HALO + tm, :] = proj_ref[0, :, 2 * W_MIX:3 * W_MIX] * jax.nn.sigmoid(proj_ref[0, :, 3 * W_MIX:4 * W_MIX])
    taps = _conv_taps()
    for c0 in range(0, tm, CONV_ROWS):
        for l0 in range(0, W_MIX, LANES):
            lanes = slice(l0, l0 + LANES)
            acc = jnp.zeros((CONV_ROWS, LANES), F32) + cb_ref[:, lanes]
            for s in range(SUBLANES):
                part = jnp.zeros((CONV_ROWS + SUBLANES, LANES), F32)
                for q, k in taps[s]:
                    r0 = c0 + SUBLANES * q
                    part = part + hext[r0:r0 + CONV_ROWS + SUBLANES, lanes] * dw_ref[k:k + 1, lanes]
                acc = acc + part[s:s + CONV_ROWS]
            h1_ref[0, c0:c0 + CONV_ROWS, lanes] = acc
    h1 = h1_ref[0]
    mu = jnp.mean(h1, axis=-1, keepdims=True)
    xc = h1 - mu
    rstd = lax.rsqrt(jnp.mean(xc * xc, axis=-1, keepdims=True) + LN_EPS)
    h3 = _silu(xc * rstd * lg_ref[...] + lb_ref[...])
    ob = jnp.dot(h3.astype(BF16), pww_ref[...], preferred_element_type=F32) + pwb_ref[...]
    ob_ref[0] = ob
    y_s[:, W_MIX:] = (ob * _silu(proj_ref[0, :, 4 * W_MIX:5 * W_MIX])).astype(BF16)

    h = x_ref[0] + jnp.dot(y_s[...], wout_ref[...], preferred_element_type=F32)
    r2 = lax.rsqrt(jnp.mean(h * h, axis=-1, keepdims=True) + RMS_EPS)
    hr = h * r2
    err = hr * fg_ref[...] - tgt_ref[0]
    loss_ref[...] += jnp.sum(err * err) * (0.5 / D_MODEL)
    dout = err * (1.0 / D_MODEL)
    gfg_ref[...] += _colsum8(dout * hr)
    gy = dout * fg_ref[...]
    dh_ref[0] = r2 * (gy - hr * jnp.mean(gy * hr, axis=-1, keepdims=True))

    uext[0:POOL_HALO, :] = uext[tm:tm + POOL_HALO, :]
    hext[0:CONV_HALO, :] = hext[tm:tm + CONV_HALO, :]


def _row(v):
    return v.reshape(1, -1)


def _fwd(x, tgt, proj, pool_w, pool_b, pool_scale, conv_dw_f, conv_b, ln_g, ln_b, pw_w_b, pw_b, w_out_b, final_g):
    bl, s, _ = x.shape
    tm = TILE_M
    tok = lambda n: pl.BlockSpec((1, tm, n), lambda b, i: (b, i, 0))
    acc = lambda n: pl.BlockSpec((SUBLANES, n), lambda b, i: (0, 0))
    act = lambda n: jax.ShapeDtypeStruct((bl, s, n), F32)
    return pl.pallas_call(
        functools.partial(_fwd_body),
        name="fwd",
        grid=(bl, s // tm),
        in_specs=[tok(D_MODEL), tok(D_MODEL), tok(D_IN)] + [VMEM] * 11,
        out_specs=[tok(W_MIX), tok(W_MIX), tok(W_MIX), tok(W_MIX), tok(D_MODEL), acc(LANES), acc(D_MODEL)],
        out_shape=[act(W_MIX), act(W_MIX), act(W_MIX), jax.ShapeDtypeStruct((bl, s, W_MIX), BF16), act(D_MODEL),
                   jax.ShapeDtypeStruct((SUBLANES, LANES), F32), jax.ShapeDtypeStruct((SUBLANES, D_MODEL), F32)],
        scratch_shapes=[pltpu.VMEM((POOL_HALO + tm, W_MIX), F32), pltpu.VMEM((CONV_HALO + tm + SUBLANES, W_MIX), F32),
                        pltpu.VMEM((tm, D_MODEL), BF16)],
        compiler_params=pltpu.CompilerParams(dimension_semantics=("arbitrary", "arbitrary"), vmem_limit_bytes=VMEM_LIMIT),
    )(x, tgt, proj, pool_w, pool_b.reshape(1, W_MIX), _row(pool_scale), conv_dw_f, _row(conv_b),
      _row(ln_g), _row(ln_b), pw_w_b, _row(pw_b), w_out_b, _row(final_g))


NT_DIMS = (((1,), (1,)), ((), ()))
TN_DIMS = (((0,), (0,)), ((), ()))
V_POOL_B, V_POOL_SCALE, V_CONV_B, V_LN_G, V_LN_B, V_PW_B = range(6)


def _bwd_taps():
    taps = {s: [] for s in range(SUBLANES)}
    for k in range(CONV_WIDTH):
        e = CONV_WIDTH - 1 - k
        taps[e % SUBLANES].append((e // SUBLANES, k))
    return taps


def _bwd1_body(n_tiles, dh_ref, proj_ref, h1_ref, z_ref, ob_ref, dpool_ref, wout_ref, pww_ref, poolw_ref, pools_ref, dw_ref,
               lg_ref, lb_ref,
               dproj_ref, gwout_ref, gpww_ref, gpoolw_ref, gvec_ref, gdw_ref,
               vext, gext, gsh, y_s):
    b = pl.program_id(0)
    i = pl.program_id(1)
    tm = TILE_M
    t0 = (n_tiles - 1 - i) * tm

    @pl.when(i == 0)
    def _():
        vext[tm:, :] = jnp.zeros((POOL_HALO, W_MIX), F32)
        gext[tm:, :] = jnp.zeros((CONV_HALO, W_MIX), F32)

    @pl.when((b == 0) & (i == 0))
    def _():
        gwout_ref[...] = jnp.zeros_like(gwout_ref)
        gpww_ref[...] = jnp.zeros_like(gpww_ref)
        gpoolw_ref[...] = jnp.zeros_like(gpoolw_ref)
        gvec_ref[...] = jnp.zeros_like(gvec_ref)
        gdw_ref[...] = jnp.zeros_like(gdw_ref)

    sub = tm // BWD_SUBTILES

    def dense_phase(r0):
        rs = slice(r0, r0 + sub)
        dhb = dh_ref[0, rs, :].astype(BF16)
        dy_a = lax.dot_general(dhb, wout_ref[0:W_MIX, :], NT_DIMS, preferred_element_type=F32)
        dy_b = lax.dot_general(dhb, wout_ref[W_MIX:, :], NT_DIMS, preferred_element_type=F32)

        a_gate = proj_ref[0, rs, W_MIX:2 * W_MIX]
        sig_a = jax.nn.sigmoid(a_gate)
        silu_a = a_gate * sig_a
        z = z_ref[0, rs, :]
        out_a = z * pools_ref[...]
        y_s[rs, 0:W_MIX] = (out_a * silu_a).astype(BF16)
        d_out_a = dy_a * silu_a
        d_ag = dy_a * out_a * (sig_a * (1.0 + a_gate * (1.0 - sig_a)))
        dproj_ref[0, rs, W_MIX:2 * W_MIX] = d_ag.astype(BF16)
        gvec_ref[V_POOL_SCALE] += _colsum8(d_out_a * z)
        dz = d_out_a * pools_ref[...]
        gvec_ref[V_POOL_B] += _colsum8(dz)
        t = _row_index((sub, POOL_GC), t0 + r0)
        n_ext = sub + POOL_HALO
        for g, w in enumerate(POOL_WINDOWS):
            lanes = slice(g * POOL_GC, (g + 1) * POOL_GC)
            dzg = dz[:, lanes].astype(BF16)
            gpoolw_ref[g] += lax.dot_general(dpool_ref[0, rs, lanes], dzg, TN_DIMS, preferred_element_type=F32)
            dd = lax.dot_general(dzg, poolw_ref[g].astype(BF16), NT_DIMS, preferred_element_type=F32)
            vext[rs, lanes] = dd * (1.0 / jnp.minimum(t + 1, w).astype(F32))
            s = vext[r0:r0 + n_ext, lanes]
            sh = 1
            while sh < w:
                s = s + pltpu.roll(s, n_ext - sh, 0)
                sh *= 2
            dproj_ref[0, rs, lanes] = (s[0:sub] - dd).astype(BF16)

        b_gate = proj_ref[0, rs, 4 * W_MIX:5 * W_MIX]
        sig_b = jax.nn.sigmoid(b_gate)
        silu_b = b_gate * sig_b
        ob = ob_ref[0, rs, :]
        y_s[rs, W_MIX:] = (ob * silu_b).astype(BF16)
        d_ob = dy_b * silu_b
        dproj_ref[0, rs, 4 * W_MIX:5 * W_MIX] = (dy_b * ob * (sig_b * (1.0 + b_gate * (1.0 - sig_b)))).astype(BF16)
        gvec_ref[V_PW_B] += _colsum8(d_ob)
        gwout_ref[...] += lax.dot_general(y_s[rs, :], dhb, TN_DIMS, preferred_element_type=F32)

        h1 = h1_ref[0, rs, :]
        mu = jnp.mean(h1, axis=-1, keepdims=True)
        xc = h1 - mu
        rstd = lax.rsqrt(jnp.mean(xc * xc, axis=-1, keepdims=True) + LN_EPS)
        xhat = xc * rstd
        h2 = xhat * lg_ref[...] + lb_ref[...]
        sig2 = jax.nn.sigmoid(h2)
        d_obb = d_ob.astype(BF16)
        gpww_ref[...] += lax.dot_general((h2 * sig2).astype(BF16), d_obb, TN_DIMS, preferred_element_type=F32)
        dh2 = lax.dot_general(d_obb, pww_ref[...], NT_DIMS, preferred_element_type=F32) * (sig2 * (1.0 + h2 * (1.0 - sig2)))
        gvec_ref[V_LN_G] += _colsum8(dh2 * xhat)
        gvec_ref[V_LN_B] += _colsum8(dh2)
        dxh = dh2 * lg_ref[...]
        dh1 = rstd * (dxh - jnp.mean(dxh, axis=-1, keepdims=True) - xhat * jnp.mean(dxh * xhat, axis=-1, keepdims=True))
        gvec_ref[V_CONV_B] += _colsum8(dh1)
        gext[rs, :] = dh1

    taps = _bwd_taps()
    cr = BWD_CONV_ROWS
    win = cr + CONV_HALO - SUBLANES

    def conv_phase(r0):
        for c0 in range(r0, r0 + sub, cr):
            for l0 in range(0, W_MIX, LANES):
                lanes = slice(l0, l0 + LANES)
                b_val = proj_ref[0, c0:c0 + cr, 2 * W_MIX + l0:2 * W_MIX + l0 + LANES]
                sg = jax.nn.sigmoid(proj_ref[0, c0:c0 + cr, 3 * W_MIX + l0:3 * W_MIX + l0 + LANES])
                h0c = b_val * sg
                acc = jnp.zeros((cr, LANES), F32)
                for s in range(SUBLANES):
                    gsh[...] = gext[c0 + s:c0 + s + win, lanes]
                    for q, k in taps[s]:
                        gq = gsh[SUBLANES * q:SUBLANES * q + cr, :]
                        acc = acc + gq * dw_ref[k:k + 1, lanes]
                        gdw_ref[k, :, lanes] += _colsum8(h0c * gq)
                d_bval = acc * sg
                dproj_ref[0, c0:c0 + cr, 2 * W_MIX + l0:2 * W_MIX + l0 + LANES] = d_bval.astype(BF16)
                dproj_ref[0, c0:c0 + cr, 3 * W_MIX + l0:3 * W_MIX + l0 + LANES] = (d_bval * b_val * (1.0 - sg)).astype(BF16)

    starts = [k * sub for k in reversed(range(BWD_SUBTILES))]
    for r0 in starts:
        dense_phase(r0)
    for r0 in starts:
        conv_phase(r0)
    vext[tm:, :] = vext[0:POOL_HALO, :]
    gext[tm:, :] = gext[0:CONV_HALO, :]


def _bwd1(dh, proj, h1, z, ob, dpool, w_out_b, pw_w_b, pool_w, pool_scale, conv_dw_f, ln_g, ln_b):
    bl, s, _ = dh.shape
    tm = TILE_M
    nt = s // tm
    tok = lambda n: pl.BlockSpec((1, tm, n), lambda b, i: (b, nt - 1 - i, 0))
    res = lambda shape: pl.BlockSpec(shape, lambda b, i: (0,) * len(shape))
    out_shapes = [(D_MODEL, D_MODEL), (W_MIX, W_MIX), (len(POOL_WINDOWS), POOL_GC, POOL_GC), (6, SUBLANES, W_MIX),
                  (CONV_WIDTH, SUBLANES, W_MIX)]
    return pl.pallas_call(
        functools.partial(_bwd1_body, nt),
        name="bwd1",
        grid=(bl, nt),
        in_specs=[tok(D_MODEL), tok(D_IN), tok(W_MIX), tok(W_MIX), tok(W_MIX), tok(W_MIX)] + [VMEM] * 7,
        out_specs=[tok(D_IN)] + [res(sh) for sh in out_shapes],
        out_shape=[jax.ShapeDtypeStruct((bl, s, D_IN), BF16)] + [jax.ShapeDtypeStruct(sh, F32) for sh in out_shapes],
        scratch_shapes=[pltpu.VMEM((tm + POOL_HALO, W_MIX), F32), pltpu.VMEM((tm + CONV_HALO, W_MIX), F32),
                        pltpu.VMEM((BWD_CONV_ROWS + CONV_HALO - SUBLANES, LANES), F32), pltpu.VMEM((tm, D_MODEL), BF16)],
        compiler_params=pltpu.CompilerParams(dimension_semantics=("arbitrary", "arbitrary"), vmem_limit_bytes=VMEM_LIMIT),
    )(dh, proj, h1, z, ob, dpool, w_out_b, pw_w_b, pool_w, _row(pool_scale), conv_dw_f, _row(ln_g), _row(ln_b))


MESH = pl.DeviceIdType.MESH
W_IN_BLK = D_IN // N_DEV
W_OUT_BLK = D_MODEL // N_DEV
PW_BLK = W_MIX // N_DEV
DW_BLK = W_MIX // N_DEV
N_PEER = N_DEV - 1


SLAB = 2 * W_IN_BLK
N_SLABS = 4
F_SIB, F_X, F_Y, F_ON, F_PASS_X, F_PASS_Y, F_PASS_D = range(7)
W_IN_ONLY, OTHERS = (0,), (1, 2, 3)


def _inproj_body(n_tiles, x_ref, ng_ref, wint_ref, wout_ref, pww_ref, dw_ref,
                 proj_hbm, wt_ref, woutf_ref, pwwf_ref, dwf_ref,
                 hn_s, wt_s, wout_s, pww_s, dw_blk, stage, out_sems, send_sems, recv_sems):
    s = pl.program_id(0)
    i = pl.program_id(1)
    tm = TILE_IN
    x, y, c = lax.axis_index("x"), lax.axis_index("y"), lax.axis_index("c")
    sibling = (x, y, 1 - c)
    xn, yn, dg = (1 - x, y), (x, 1 - y), (1 - x, 1 - y)
    pick = lambda a, b: tuple(jnp.where(c == 1, p, q) for p, q in zip(a, b))
    on_from, on_to = pick(xn, yn), pick(yn, xn)

    def blocks(px, py, pc):
        k = 4 * px + 2 * py + pc
        rows = lambda ref, n: ref.at[pl.ds(pl.multiple_of(k * n, n), n), :]
        return (rows(wt_s, W_IN_BLK), rows(wout_s, W_OUT_BLK), rows(pww_s, PW_BLK), dw_blk.at[k])

    def copies(arrays, flow, block, to):
        refs = blocks(*block)
        return [pltpu.make_async_remote_copy(src_ref=refs[a], dst_ref=refs[a], send_sem=send_sems.at[a, flow],
                                             recv_sem=recv_sems.at[a, flow], device_id=to, device_id_type=MESH)
                for a in arrays]

    def start(arrays):
        for flow, to in ((F_SIB, sibling), (F_X, (*xn, c)), (F_Y, (*yn, c))):
            for cp in copies(arrays, flow, (x, y, c), to):
                cp.start()

    def direct(arrays):
        for flow, chip in ((F_X, xn), (F_Y, yn)):
            for cp in copies(arrays, flow, (*chip, c), sibling):
                cp.wait_recv()
        for cp in copies(arrays, F_ON, (*on_from, c), (*on_to, c)):
            cp.start()
        for flow, chip in ((F_PASS_X, xn), (F_PASS_Y, yn)):
            for cp in copies(arrays, flow, (*chip, c), sibling):
                cp.start()

    def diagonal(arrays):
        for cp in copies(arrays, F_ON, (*dg, c), sibling):
            cp.wait_recv()
        for cp in copies(arrays, F_PASS_D, (*dg, c), sibling):
            cp.start()

    def from_sibling(arrays, flow, chip):
        for cp in copies(arrays, flow, (*chip, 1 - c), sibling):
            cp.wait_recv()

    def finish(arrays):
        mine = (x, y, c)
        for flow, block in ((F_SIB, mine), (F_X, mine), (F_Y, mine), (F_ON, (*on_from, c)), (F_PASS_X, (*xn, c)),
                            (F_PASS_Y, (*yn, c)), (F_PASS_D, (*dg, c))):
            for cp in copies(arrays, flow, block, sibling):
                cp.wait_send()

    step = s * n_tiles + i
    par = step % 2
    first = step == 0

    @pl.when(first)
    def _():
        mine = blocks(x, y, c)
        mine[0][...] = wint_ref[...].astype(BF16)
        mine[1][...] = wout_ref[...].astype(BF16)
        mine[2][...] = pww_ref[...].astype(BF16)
        mine[3][...] = dw_ref[...]
        start(W_IN_ONLY)
        start(OTHERS)

    rows_i = pl.ds(pl.multiple_of(i * tm, tm), tm)

    @pl.when(s == 0)
    def _():
        xt = x_ref[0]
        r1 = lax.rsqrt(jnp.mean(xt * xt, axis=-1, keepdims=True) + RMS_EPS)
        hn_s[rows_i, :] = (xt * r1 * ng_ref[...]).astype(BF16)

    @pl.when(first)
    def _():
        from_sibling(W_IN_ONLY, F_SIB, (x, y))

    @pl.when((s == 1) & (i == 0))
    def _():
        direct(W_IN_ONLY)
        from_sibling(W_IN_ONLY, F_PASS_Y, yn)

    @pl.when((s == 2) & (i == 0))
    def _():
        from_sibling(W_IN_ONLY, F_PASS_X, xn)

    @pl.when((s == 3) & (i == 0))
    def _():
        diagonal(W_IN_ONLY)
        direct(OTHERS)
        from_sibling(W_IN_ONLY, F_PASS_D, dg)

    tiles_per_seq = proj_hbm.shape[1] // tm

    def out_copy(parity, col0):
        dst = proj_hbm.at[i // tiles_per_seq, pl.ds(pl.multiple_of((i % tiles_per_seq) * tm, tm), tm),
                          pl.ds(pl.multiple_of(col0, LANES), SLAB)]
        return pltpu.make_async_copy(stage.at[parity], dst, out_sems.at[parity])

    @pl.when(step >= 2)
    def _():
        out_copy(par, 0).wait()

    chip = jnp.where(s < 2, 2 * x, 2 * (1 - x)) + jnp.where((s == 0) | (s == 2), y, 1 - y)
    slab = wt_s[pl.ds(pl.multiple_of(chip * SLAB, SLAB), SLAB), :]
    stage[par] = lax.dot_general(hn_s[rows_i, :], slab, NT_DIMS, preferred_element_type=F32)
    out_copy(par, chip * SLAB).start()

    @pl.when(step == N_SLABS * n_tiles - 1)
    def _():
        out_copy(par, 0).wait()
        out_copy(1 - par, 0).wait()
        from_sibling(OTHERS, F_SIB, (x, y))
        from_sibling(OTHERS, F_PASS_X, xn)
        from_sibling(OTHERS, F_PASS_Y, yn)
        diagonal(OTHERS)
        from_sibling(OTHERS, F_PASS_D, dg)
        finish(W_IN_ONLY)
        finish(OTHERS)
        wt_ref[...] = wt_s[...]
        woutf_ref[...] = wout_s[...]
        pwwf_ref[...] = pww_s[...]
        for k in range(N_DEV):
            dwf_ref[:, k * DW_BLK:(k + 1) * DW_BLK] = dw_blk[k]


def _inproj(x, norm_g, w_in_t, w_out, pw_w, conv_dw):
    bl, s, _ = x.shape
    tm = TILE_IN
    nt = s // tm
    n_tiles = bl * nt

    def x_map(phase, i):
        i = jnp.where(phase == 0, i, n_tiles - 1)
        return (i // nt, i % nt, 0)

    dma = pltpu.SemaphoreType.DMA
    return pl.pallas_call(
        functools.partial(_inproj_body, n_tiles),
        name="inproj",
        grid=(N_SLABS, n_tiles),
        in_specs=[pl.BlockSpec((1, tm, D_MODEL), x_map)] + [VMEM] * 5,
        out_specs=[pl.BlockSpec(memory_space=pl.ANY)] + [VMEM] * 4,
        out_shape=[jax.ShapeDtypeStruct((bl, s, D_IN), F32), jax.ShapeDtypeStruct((D_IN, D_MODEL), BF16),
                   jax.ShapeDtypeStruct((D_MODEL, D_MODEL), BF16), jax.ShapeDtypeStruct((W_MIX, W_MIX), BF16),
                   jax.ShapeDtypeStruct((CONV_WIDTH, W_MIX), F32)],
        scratch_shapes=[pltpu.VMEM((n_tiles * tm, D_MODEL), BF16), pltpu.VMEM((D_IN, D_MODEL), BF16),
                        pltpu.VMEM((D_MODEL, D_MODEL), BF16), pltpu.VMEM((W_MIX, W_MIX), BF16),
                        pltpu.VMEM((N_DEV, CONV_WIDTH, DW_BLK), F32), pltpu.VMEM((2, tm, SLAB), F32),
                        dma((2,)), dma((4, 7)), dma((4, 7))],
        compiler_params=pltpu.CompilerParams(dimension_semantics=("arbitrary", "arbitrary"), vmem_limit_bytes=VMEM_LIMIT),
    )(x, _row(norm_g), w_in_t, w_out, pw_w, conv_dw)


B_PLAIN, B_ON, B_MERGED = range(3)


def _routes():
    x, y, c = lax.axis_index("x"), lax.axis_index("y"), lax.axis_index("c")
    xn, yn = (1 - x, y), (x, 1 - y)
    pick = lambda a, b: tuple(jnp.where(c == 1, p, q) for p, q in zip(a, b))
    return pick(xn, yn), pick(yn, xn), (1 - x, 1 - y), (x, y), c


def _chip_index(chip):
    return 2 * chip[0] + chip[1]


def _copy(src, dst, send_sems, recv_sems, a, k, to):
    return pltpu.make_async_remote_copy(src_ref=src, dst_ref=dst, send_sem=send_sems.at[a, k], recv_sem=recv_sems.at[a, k],
                                        device_id=to, device_id_type=MESH)


def _rsb_send(plain, on, rb_plain, rb_on, send_b, recv_b):
    first, _, _, _, c = _routes()
    for k, srcs, dsts in ((B_ON, on, rb_on), (B_PLAIN, plain, rb_plain)):
        for a, (s, d) in enumerate(zip(srcs, dsts)):
            _copy(s, d, send_b, recv_b, a, k, (*first, c)).start()


def _rsb_merge(second_sums, merged, rb_on, rb_merged, send_b, recv_b):
    _, second, _, _, c = _routes()
    for a, (own, m, got, dst) in enumerate(zip(second_sums, merged, rb_on, rb_merged)):
        _copy(got, got, send_b, recv_b, a, B_ON, (*second, c)).wait_recv()
        m[...] = (own[...] + got[...].astype(F32)).astype(BF16)
        _copy(m, dst, send_b, recv_b, a, B_MERGED, (*second, c)).start()


def _rsb_total(own_sums, plain, on, merged, rb_plain, rb_merged, send_b, recv_b):
    _, second, _, _, c = _routes()
    out = []
    for a, (own, p, g) in enumerate(zip(own_sums, rb_plain, rb_merged)):
        _copy(p, p, send_b, recv_b, a, B_PLAIN, (*second, c)).wait_recv()
        _copy(g, g, send_b, recv_b, a, B_MERGED, (*second, c)).wait_recv()
        out.append(own[...] + p[...].astype(F32) + g[...].astype(F32))
    for k, srcs in ((B_PLAIN, plain), (B_ON, on), (B_MERGED, merged)):
        for a, s in enumerate(srcs):
            _copy(s, s, send_b, recv_b, a, k, (*second, c)).wait_send()
    return out


R_LOSS, R_FINAL_G, R_POOL, R_CONV, R_LN = range(5)
VEC_ROWS = 16
DW_ROWS = 32
A_PART, A_X, A_Y, A_ON, A_HALF = range(5)
HALF_DTYPES = (F32, BF16, BF16)
STEP_SUMS, STEP_MERGE, STEP_HALF = 1, 3, 5


def _bwd2a_body(n_steps, x_ref, dproj_ref, ng_ref, gwout_ref, gpww_ref, loss_ref, gfg_ref, gvec_ref, gdw_ref, gpoolw_ref,
                win_own_ref, win_staged_ref, owout_ref, opww_ref, ovec_ref, odw_ref, opoolw_ref,
                acc, st_win, a_win, a_wout, a_pww, st_wout, st_pww, rb_wout, rb_pww, part_vec, part_dw, part_pw,
                sib_vec, sib_dw, sib_pw, res_vec, res_dw, res_pw, half_vec, half_dw, half_pw,
                send_a, recv_a, send_b, recv_b, send_r, recv_r, send_w, recv_w):
    step = pl.program_id(0) * pl.num_programs(1) + pl.program_id(1)
    x, y, c = lax.axis_index("x"), lax.axis_index("y"), lax.axis_index("c")
    sibling = (x, y, 1 - c)
    xn, yn = (1 - x, y), (x, 1 - y)
    first, second, diag, own, _ = _routes()
    rows = lambda ref, k, n: ref.at[pl.ds(pl.multiple_of(k * n, n), n), :]

    blocks = lambda k: (rows(gwout_ref, k, W_OUT_BLK), rows(gpww_ref, k, PW_BLK))
    landed = lambda chip: (a_wout.at[chip], a_pww.at[chip])
    staged = lambda k: (st_wout.at[k], st_pww.at[k])
    arrived = lambda k: (rb_wout.at[k], rb_pww.at[k])

    parts, sibs, halves = (part_vec, part_dw, part_pw), (sib_vec, sib_dw, sib_pw), (half_vec, half_dw, half_pw)
    results = (res_vec, res_dw, res_pw)
    half_rows = lambda ref: pl.ds(pl.multiple_of(c * (ref.shape[0] // 2), SUBLANES), ref.shape[0] // 2)
    slot = lambda chip: tuple(h.at[_chip_index(chip)] for h in halves)

    @pl.when(step == 0)
    def _():
        acc[...] = jnp.zeros_like(acc)
        sum8 = lambda v: jnp.sum(v, axis=0, keepdims=True)
        part_vec[...] = jnp.zeros((VEC_ROWS, D_MODEL), F32)
        part_vec[R_LOSS:R_LOSS + 1, :] = jnp.broadcast_to(loss_ref[0:1, 0:1], (1, D_MODEL))
        part_vec[R_FINAL_G:R_FINAL_G + 1, :] = sum8(gfg_ref[...])
        for row, (lo, hi) in ((R_POOL, (V_POOL_B, V_POOL_SCALE)), (R_CONV, (V_CONV_B, V_LN_G)), (R_LN, (V_LN_B, V_PW_B))):
            part_vec[row:row + 1, 0:W_MIX] = sum8(gvec_ref[lo])
            part_vec[row:row + 1, W_MIX:] = sum8(gvec_ref[hi])
        part_dw[0:CONV_WIDTH, :] = jnp.sum(gdw_ref[...], axis=1)
        part_dw[CONV_WIDTH:, :] = jnp.zeros((DW_ROWS - CONV_WIDTH, W_MIX), F32)
        part_pw[...] = gpoolw_ref[...].reshape(len(POOL_WINDOWS) * POOL_GC, POOL_GC)
        for a, (p, s) in enumerate(zip(parts, sibs)):
            _copy(p, s, send_r, recv_r, a, A_PART, sibling).start()
        for chip in range(4):
            for a, (s, d) in enumerate(zip(blocks(2 * chip + (1 - c)), landed(chip))):
                _copy(s, d, send_a, recv_a, a, chip, sibling).start()

    x_t = x_ref[0]
    r1 = lax.rsqrt(jnp.mean(x_t * x_t, axis=-1, keepdims=True) + RMS_EPS)
    hn = (x_t * r1 * ng_ref[...]).astype(BF16)
    acc[...] += lax.dot_general(dproj_ref[0], hn, TN_DIMS, preferred_element_type=F32)

    @pl.when(step == STEP_SUMS)
    def _():
        for a, (p, s, h) in enumerate(zip(parts, sibs, slot(own))):
            _copy(p, s, send_r, recv_r, a, A_PART, sibling).wait_recv()
            h[...] = (p[half_rows(p), :] + s[half_rows(s), :]).astype(h.dtype)
            for k, to in ((A_X, xn), (A_Y, yn)):
                _copy(h, h, send_r, recv_r, a, k, (*to, c)).start()
        for chip in range(4):
            for a, (m, l) in enumerate(zip(blocks(2 * chip + c), landed(chip))):
                _copy(l, l, send_a, recv_a, a, chip, sibling).wait_recv()
                l[...] = m[...] + l[...]
        for k, chip in ((B_ON, diag), (B_PLAIN, first)):
            for s, l in zip(staged(k), landed(_chip_index(chip))):
                s[...] = l[...].astype(BF16)
        _rsb_send(staged(B_PLAIN), staged(B_ON), arrived(B_PLAIN), arrived(B_ON), send_b, recv_b)

    @pl.when(step == STEP_MERGE)
    def _():
        _rsb_merge(landed(_chip_index(second)), staged(B_MERGED), arrived(B_ON), arrived(B_MERGED), send_b, recv_b)
        for a in range(3):
            for k, chip in ((A_X, xn), (A_Y, yn)):
                h = slot(chip)[a]
                _copy(h, h, send_r, recv_r, a, k, sibling).wait_recv()
            h = slot(first)[a]
            _copy(h, h, send_r, recv_r, a, A_ON, (*second, c)).start()

    @pl.when(step == STEP_HALF)
    def _():
        for a, (r, hs) in enumerate(zip(results, halves)):
            h = slot(diag)[a]
            _copy(h, h, send_r, recv_r, a, A_ON, sibling).wait_recv()
            mine = r.at[half_rows(r), :]
            chip_sums = [hs[k].astype(F32) for k in range(4)]
            mine[...] = (chip_sums[0] + chip_sums[1]) + (chip_sums[2] + chip_sums[3])
            _copy(mine, mine, send_r, recv_r, a, A_HALF, sibling).start()

    @pl.when(step == n_steps - 1)
    def _():
        block = lambda k: rows(acc, k, W_IN_BLK)
        w_copy = lambda chip: _copy(st_win.at[chip], a_win.at[chip], send_w, recv_w, 0, chip, sibling)
        for chip in range(4):
            st_win[chip] = block(2 * chip + (1 - c))[...].astype(BF16)
            w_copy(chip).start()

        owout_ref[...], opww_ref[...] = _rsb_total(landed(_chip_index(own)), staged(B_PLAIN), staged(B_ON), staged(B_MERGED),
                                                   arrived(B_PLAIN), arrived(B_MERGED), send_b, recv_b)
        for a, (r, p, s) in enumerate(zip(results, parts, sibs)):
            theirs = r.at[pl.ds(pl.multiple_of((1 - c) * (r.shape[0] // 2), SUBLANES), r.shape[0] // 2), :]
            _copy(theirs, theirs, send_r, recv_r, a, A_HALF, sibling).wait_recv()
            _copy(p, s, send_r, recv_r, a, A_PART, sibling).wait_send()
            for k, h in ((A_X, slot(own)[a]), (A_Y, slot(own)[a]), (A_ON, slot(first)[a])):
                _copy(h, h, send_r, recv_r, a, k, sibling).wait_send()
            mine = r.at[half_rows(r), :]
            _copy(mine, mine, send_r, recv_r, a, A_HALF, sibling).wait_send()
        for chip in range(4):
            for a, (s, d) in enumerate(zip(blocks(2 * chip + (1 - c)), landed(chip))):
                _copy(s, d, send_a, recv_a, a, chip, sibling).wait_send()
        ovec_ref[...] = res_vec[...]
        opoolw_ref[...] = res_pw[...]
        for k in range(N_DEV):
            @pl.when(4 * x + 2 * y + c == k)
            def _():
                odw_ref[...] = res_dw[0:CONV_WIDTH, k * DW_BLK:(k + 1) * DW_BLK]

        for chip in range(4):
            w_copy(chip).wait()
        chip_sum = lambda chip: block(2 * _chip_index(chip) + c)[...] + a_win[_chip_index(chip)].astype(F32)
        win_own_ref[...] = chip_sum(own)
        for j, chip in enumerate((xn, yn, diag)):
            win_staged_ref[j] = chip_sum(chip).astype(BF16)


def _bwd2a(x, dproj, norm_g, gwout, gpww, loss8, gfg8, gvec, gdw8, gpoolw):
    bl, s, _ = x.shape
    tm = TILE_M
    nt = s // tm
    tok = lambda n: pl.BlockSpec((1, tm, n), lambda b, i: (b, i, 0))
    n_pw = len(POOL_WINDOWS) * POOL_GC
    win = (W_IN_BLK, D_MODEL)
    blk = ((W_OUT_BLK, D_MODEL), (PW_BLK, W_MIX))
    rep = ((VEC_ROWS, D_MODEL), (DW_ROWS, W_MIX), (n_pw, POOL_GC))
    half = lambda sh: (sh[0] // 2,) + sh[1:]
    dma = pltpu.SemaphoreType.DMA
    assert bl * nt > STEP_HALF + 1, "the exchanges' phases need their grid steps"
    return pl.pallas_call(
        functools.partial(_bwd2a_body, bl * nt),
        name="bwd2a",
        grid=(bl, nt),
        in_specs=[tok(D_MODEL), tok(D_IN)] + [VMEM] * 8,
        out_specs=[VMEM] * 7,
        out_shape=[jax.ShapeDtypeStruct(win, F32), jax.ShapeDtypeStruct((3,) + win, BF16)]
        + [jax.ShapeDtypeStruct(sh, F32) for sh in blk + (rep[0], (CONV_WIDTH, DW_BLK), rep[2])],
        scratch_shapes=[pltpu.VMEM((D_IN, D_MODEL), F32), pltpu.VMEM((4,) + win, BF16), pltpu.VMEM((4,) + win, BF16)]
        + [pltpu.VMEM((4,) + sh, F32) for sh in blk]
        + [pltpu.VMEM((3,) + sh, BF16) for sh in blk] * 2
        + [pltpu.VMEM(sh, F32) for sh in rep] * 3
        + [pltpu.VMEM((4,) + half(sh), dt) for sh, dt in zip(rep, HALF_DTYPES)]
        + [dma((2, 4)), dma((2, 4)), dma((2, 3)), dma((2, 3)), dma((3, 5)), dma((3, 5)), dma((1, 4)), dma((1, 4))],
        compiler_params=pltpu.CompilerParams(dimension_semantics=("arbitrary", "arbitrary"), vmem_limit_bytes=VMEM_LIMIT),
    )(x, dproj, _row(norm_g), gwout, gpww, loss8, gfg8, gvec, gdw8, gpoolw)


HBM = pl.BlockSpec(memory_space=pltpu.HBM)
SEM = pl.BlockSpec(memory_space=pltpu.SEMAPHORE)
EFFECT = pltpu.SideEffectType.DATAFLOW_SIDE_EFFECTING


def _owner_copies(src_ref, land_ref, send_sems, recv_sems):
    x, y, c = lax.axis_index("x"), lax.axis_index("y"), lax.axis_index("c")
    owners = ((1 - x, y, c), (x, 1 - y, c), (1 - x, 1 - y, c))
    return [pltpu.make_async_remote_copy(src_ref=src_ref.at[j], dst_ref=land_ref.at[j], send_sem=send_sems.at[j],
                                         recv_sem=recv_sems.at[j], device_id=owners[j], device_id_type=MESH)
            for j in range(3)]


def _rs_start(staged):
    def body(src_ref, land_ref, send_sems, recv_sems, src_thru, land_thru, token):
        for cp in _owner_copies(src_ref, land_ref, send_sems, recv_sems):
            cp.start()
        token[...] = jnp.zeros_like(token)

    dma = pltpu.SemaphoreType.DMA
    return pl.pallas_call(
        body, name="rs_start",
        out_shape=(dma((3,)), dma((3,)), pltpu.HBM(staged.shape, staged.dtype), pltpu.HBM(staged.shape, staged.dtype),
                   jax.ShapeDtypeStruct((SUBLANES, LANES), F32)),
        in_specs=(HBM, HBM), out_specs=(SEM, SEM, HBM, HBM, VMEM), input_output_aliases={0: 2, 1: 3},
        compiler_params=pltpu.CompilerParams(has_side_effects=EFFECT),
    )(pltpu.with_memory_space_constraint(staged, pltpu.HBM),
      pltpu.with_memory_space_constraint(lax.empty(staged.shape, staged.dtype), pltpu.HBM))


def _rs_wait(send_sems, recv_sems, src_thru, land_thru, after):
    def body(src_ref, land_ref, send_sems, recv_sems, after_ref, src_dead, got_ref):
        for cp in _owner_copies(src_ref, land_ref, send_sems, recv_sems):
            cp.wait_send()
            cp.wait_recv()

    return pl.pallas_call(
        body, name="rs_wait",
        out_shape=(pltpu.HBM(src_thru.shape, src_thru.dtype), pltpu.HBM(land_thru.shape, land_thru.dtype)),
        in_specs=(HBM, HBM, SEM, SEM, pl.BlockSpec(memory_space=pl.ANY)), out_specs=(HBM, HBM),
        input_output_aliases={0: 0, 1: 1},
        compiler_params=pltpu.CompilerParams(has_side_effects=EFFECT),
    )(src_thru, land_thru, send_sems, recv_sems, after)[1]


def _bwd2b_body(x_ref, dh_ref, dproj_ref, ng_ref, wint_ref, token_ref, gx_ref, gng_ref):
    @pl.when((pl.program_id(0) == 0) & (pl.program_id(1) == 0))
    def _():
        gng_ref[...] = jnp.zeros_like(gng_ref) + token_ref[0:1, 0:1]

    x_t = x_ref[0]
    r1 = lax.rsqrt(jnp.mean(x_t * x_t, axis=-1, keepdims=True) + RMS_EPS)
    xr = x_t * r1
    dhn = jnp.dot(dproj_ref[0], wint_ref[...], preferred_element_type=F32)
    gng_ref[...] += _colsum8(dhn * xr)
    gy = dhn * ng_ref[...]
    gx_ref[0] = dh_ref[0] + r1 * (gy - xr * jnp.mean(gy * xr, axis=-1, keepdims=True))


def _bwd2b(x, dh, dproj, norm_g, w_in_t, token):
    bl, s, _ = x.shape
    tm = TILE_M
    tok = lambda n: pl.BlockSpec((1, tm, n), lambda b, i: (b, i, 0))
    return pl.pallas_call(
        functools.partial(_bwd2b_body),
        name="bwd2b",
        grid=(bl, s // tm),
        in_specs=[tok(D_MODEL), tok(D_MODEL), tok(D_IN), VMEM, VMEM, VMEM],
        out_specs=[tok(D_MODEL), pl.BlockSpec((SUBLANES, D_MODEL), lambda b, i: (0, 0))],
        out_shape=[jax.ShapeDtypeStruct((bl, s, D_MODEL), F32), jax.ShapeDtypeStruct((SUBLANES, D_MODEL), F32)],
        compiler_params=pltpu.CompilerParams(dimension_semantics=("arbitrary", "arbitrary"), vmem_limit_bytes=VMEM_LIMIT),
    )(x, dh, dproj, _row(norm_g), w_in_t, token)


def _partial_copies(part_ref, land_ref, send_sems, recv_sems):
    x, y, c = lax.axis_index("x"), lax.axis_index("y"), lax.axis_index("c")
    other_chips = [(1 - x, y), (x, 1 - y), (1 - x, 1 - y)]
    peers = [(x, y, 1 - c)] + [(*chip, c) for chip in other_chips] + [(*chip, 1 - c) for chip in other_chips]
    return [pltpu.make_async_remote_copy(src_ref=part_ref, dst_ref=land_ref.at[4 * x + 2 * y + c], send_sem=send_sems.at[r],
                                         recv_sem=recv_sems.at[r], device_id=peers[r], device_id_type=MESH)
            for r in range(N_PEER)]


def _partials_start(part):
    def body(part_ref, land_ref, send_sems, recv_sems, part_thru, land_thru):
        for cp in _partial_copies(part_ref, land_ref, send_sems, recv_sems):
            cp.start()

    dma = pltpu.SemaphoreType.DMA
    land = (N_DEV,) + part.shape
    return pl.pallas_call(
        body, name="partials_start",
        out_shape=(dma((N_PEER,)), dma((N_PEER,)), pltpu.HBM(part.shape, part.dtype), pltpu.HBM(land, part.dtype)),
        in_specs=(HBM, HBM), out_specs=(SEM, SEM, HBM, HBM), input_output_aliases={0: 2, 1: 3},
        compiler_params=pltpu.CompilerParams(has_side_effects=EFFECT),
    )(pltpu.with_memory_space_constraint(part, pltpu.HBM),
      pltpu.with_memory_space_constraint(lax.empty(land, part.dtype), pltpu.HBM))


def _partials_wait(send_sems, recv_sems, part_thru, land_thru):
    def body(part_ref, land_ref, send_sems, recv_sems, part_out, land_out):
        for cp in _partial_copies(part_ref, land_ref, send_sems, recv_sems):
            cp.wait_send()
            cp.wait_recv()

    return pl.pallas_call(
        body, name="partials_wait",
        out_shape=(pltpu.HBM(part_thru.shape, part_thru.dtype), pltpu.HBM(land_thru.shape, land_thru.dtype)),
        in_specs=(HBM, HBM, SEM, SEM), out_specs=(HBM, HBM), input_output_aliases={0: 0, 1: 1},
        compiler_params=pltpu.CompilerParams(has_side_effects=EFFECT),
    )(part_thru, land_thru, send_sems, recv_sems)


ADAM_LR = 0.001
ADAM_B1 = 0.9
ADAM_B2 = 0.999
ADAM_EPS = 1e-08
ADAM_WD = 0.01
ADAM_STEP = 10


PARAMS = ("norm_g", "w_in", "pool_w", "pool_b", "pool_scale", "conv_dw", "conv_b", "ln_g", "ln_b", "pw_w", "pw_b", "w_out",
          "final_g")


def _adamw_body(*refs):
    n = len(PARAMS)
    ws, ms, vs = (dict(zip(PARAMS, refs[k * n:(k + 1) * n])) for k in range(3))
    gwin_ref, gwin_others_ref, gwout_ref, gpww_ref, vec_ref, gdw_ref, gpoolw_ref, gng_ref, gng_others_ref = refs[3 * n:3 * n + 9]
    outs = refs[3 * n + 9:]
    gs, ds, m2s, v2s = (dict(zip(PARAMS, outs[k * n:(k + 1) * n])) for k in range(4))
    loss_ref = outs[4 * n]

    def update(name, g, at=slice(None)):
        m = ADAM_B1 * ms[name][at] + (1.0 - ADAM_B1) * g
        v = ADAM_B2 * vs[name][at] + (1.0 - ADAM_B2) * (g * g)
        m_hat = m / (1.0 - ADAM_B1 ** ADAM_STEP)
        v_hat = v / (1.0 - ADAM_B2 ** ADAM_STEP)
        gs[name][at] = g
        ds[name][at] = -ADAM_LR * (m_hat / (jnp.sqrt(v_hat) + ADAM_EPS) + ADAM_WD * ws[name][at])
        m2s[name][at] = m
        v2s[name][at] = v

    row = lambda r, lo, hi: vec_ref[r:r + 1, lo:hi]
    loss_ref[...] = row(R_LOSS, 0, 1)
    me = 4 * lax.axis_index("x") + 2 * lax.axis_index("y") + lax.axis_index("c")
    g_norm = jnp.zeros(gng_ref.shape, F32)
    for k in range(N_DEV):
        g_norm = g_norm + jnp.where(me == k, gng_ref[...], gng_others_ref[k])
    update("norm_g", jnp.sum(g_norm, axis=0, keepdims=True))
    g_w_in = gwin_ref[...]
    for j in range(3):
        g_w_in = g_w_in + gwin_others_ref[j].astype(F32)
    update("w_in", g_w_in)
    update("pool_w", gpoolw_ref[...].reshape(ws["pool_w"].shape))
    for g in range(len(POOL_WINDOWS)):
        update("pool_b", row(R_POOL, g * POOL_GC, (g + 1) * POOL_GC), at=slice(g, g + 1))
    update("pool_scale", row(R_POOL, W_MIX, 2 * W_MIX))
    update("conv_dw", gdw_ref[...])
    update("conv_b", row(R_CONV, 0, W_MIX))
    update("ln_g", row(R_CONV, W_MIX, 2 * W_MIX))
    update("ln_b", row(R_LN, 0, W_MIX))
    update("pw_w", gpww_ref[...])
    update("pw_b", row(R_LN, W_MIX, 2 * W_MIX))
    update("w_out", gwout_ref[...])
    update("final_g", row(R_FINAL_G, 0, D_MODEL))


def _adamw(ws, ms, vs, grad_sources):
    n = len(PARAMS)
    shapes = [ws[p].shape for p in PARAMS]
    flat = lambda a: a.reshape(1, -1) if a.ndim == 1 else a
    args = [flat(d[p]) for d in (ws, ms, vs) for p in PARAMS]
    outs = pl.pallas_call(
        functools.partial(_adamw_body),
        name="adamw",
        in_specs=[VMEM] * (3 * n + len(grad_sources)),
        out_specs=[VMEM] * (4 * n + 1),
        out_shape=[jax.ShapeDtypeStruct(a.shape, F32) for a in args[:n]] * 4 + [jax.ShapeDtypeStruct((1, 1), F32)],
        compiler_params=pltpu.CompilerParams(vmem_limit_bytes=VMEM_LIMIT),
    )(*args, *grad_sources)
    loss = outs[4 * n].reshape(())
    outs = [o.reshape(shapes[k % n]) for k, o in enumerate(outs[:4 * n])]
    return loss, outs[:n], outs[n:2 * n], outs[2 * n:3 * n], outs[3 * n:]


def kernel(x, norm_g, w_in, pool_w, pool_b, pool_scale, conv_dw, conv_b, ln_g, ln_b, pw_w, pw_b, w_out, final_g, loss_target, m_norm_g, m_w_in, m_pool_w, m_pool_b, m_pool_scale, m_conv_dw, m_conv_b, m_ln_g, m_ln_b, m_pw_w, m_pw_b, m_w_out, m_final_g, v_norm_g, v_w_in, v_pool_w, v_pool_b, v_pool_scale, v_conv_dw, v_conv_b, v_ln_g, v_ln_b, v_pw_w, v_pw_b, v_w_out, v_final_g):
    proj, w_in_t, w_out_b, pw_w_b, conv_dw_f = _inproj(x, norm_g, w_in.T, w_out, pw_w, conv_dw)
    h1, z, ob, dpool, dh, loss8, gfg8 = _fwd(x, loss_target, proj, pool_w, pool_b, pool_scale, conv_dw_f,
                                             conv_b, ln_g, ln_b, pw_w_b, pw_b, w_out_b, final_g)
    dproj, gwout, gpww, gpoolw, gvec, gdw8 = _bwd1(dh, proj, h1, z, ob, dpool, w_out_b, pw_w_b, pool_w, pool_scale,
                                                   conv_dw_f, ln_g, ln_b)
    win_own, win_staged, g_w_out, g_pw_w, vec, g_conv_dw, poolw_sum = _bwd2a(x, dproj, norm_g, gwout, gpww, loss8, gfg8, gvec,
                                                                             gdw8, gpoolw)
    send_sems, recv_sems, staged_thru, land_thru, token = _rs_start(win_staged)
    grad_x, gng8 = _bwd2b(x, dh, dproj, norm_g, w_in_t, token)
    ng_send, ng_recv, gng8, gng_land = _partials_start(gng8)
    win_others = _rs_wait(send_sems, recv_sems, staged_thru, land_thru, gng8)
    gng8, gng_others = _partials_wait(ng_send, ng_recv, gng8, gng_land)

    weights = dict(norm_g=norm_g, w_in=w_in.T, pool_w=pool_w, pool_b=pool_b, pool_scale=pool_scale, conv_dw=conv_dw, conv_b=conv_b,
                   ln_g=ln_g, ln_b=ln_b, pw_w=pw_w, pw_b=pw_b, w_out=w_out, final_g=final_g)
    m_in = dict(norm_g=m_norm_g, w_in=m_w_in.T, pool_w=m_pool_w, pool_b=m_pool_b, pool_scale=m_pool_scale, conv_dw=m_conv_dw,
                conv_b=m_conv_b, ln_g=m_ln_g, ln_b=m_ln_b, pw_w=m_pw_w, pw_b=m_pw_b, w_out=m_w_out, final_g=m_final_g)
    v_in = dict(norm_g=v_norm_g, w_in=v_w_in.T, pool_w=v_pool_w, pool_b=v_pool_b, pool_scale=v_pool_scale, conv_dw=v_conv_dw,
                conv_b=v_conv_b, ln_g=v_ln_g, ln_b=v_ln_b, pw_w=v_pw_w, pw_b=v_pw_b, w_out=v_w_out, final_g=v_final_g)
    loss, *groups = _adamw(weights, m_in, v_in, (win_own, win_others, g_w_out, g_pw_w, vec, g_conv_dw, poolw_sum, gng8,
                                                 gng_others))
    k_in = PARAMS.index("w_in")
    for group in groups:
        group[k_in] = group[k_in].T
    grads, delta, new_m, new_v = groups
    return (loss, grad_x, *grads, *delta, *new_m, *new_v)
```

```python
import functools

import jax
import jax.numpy as jnp
from jax import lax
from jax.experimental import pallas as pl
from jax.experimental.pallas import tpu as pltpu

F32 = jnp.float32
BF16 = jnp.bfloat16

D_MODEL = 1024
W_MIX = 512
D_IN = 5 * W_MIX
POOL_WINDOWS = (2, 4, 8, 16)
POOL_GC = 128
CONV_WIDTH = 31
RMS_EPS = 1e-6
LN_EPS = 1e-5
N_DEV = 8

LANES = 128
SUBLANES = 8
TILE_M = 512
TILE_IN = 1024
POOL_HALO = 16
CONV_HALO = 32
CONV_ROWS = 128
BWD_CONV_ROWS = 64
BWD_SUBTILES = 2
VMEM_LIMIT = 56 * 1024 * 1024

VMEM = pl.BlockSpec(memory_space=pltpu.VMEM)


def _silu(v):
    return v * jax.nn.sigmoid(v)


def _colsum8(v):
    m, n = v.shape
    return jnp.sum(v.reshape(m // SUBLANES, SUBLANES, n), axis=0)


def _row_index(shape, t0):
    return lax.broadcasted_iota(jnp.int32, shape, 0) + t0


def _conv_taps():
    taps = {s: [] for s in range(SUBLANES)}
    for k in range(CONV_WIDTH):
        e = k + CONV_HALO - (CONV_WIDTH - 1)
        taps[e % SUBLANES].append((e // SUBLANES, k))
    return taps


def _fwd_body(x_ref, tgt_ref, proj_ref, poolw_ref, poolb_ref, pools_ref, dw_ref, cb_ref, lg_ref, lb_ref,
              pww_ref, pwb_ref, wout_ref, fg_ref,
              h1_ref, z_ref, ob_ref, dpool_ref, dh_ref, loss_ref, gfg_ref,
              uext, hext, y_s):
    b = pl.program_id(0)
    i = pl.program_id(1)
    tm = TILE_M

    @pl.when(i == 0)
    def _():
        uext[0:POOL_HALO, :] = jnp.zeros((POOL_HALO, W_MIX), F32)
        hext[0:CONV_HALO, :] = jnp.zeros((CONV_HALO, W_MIX), F32)
        hext[CONV_HALO + tm:, :] = jnp.zeros((SUBLANES, W_MIX), F32)

    @pl.when((b == 0) & (i == 0))
    def _():
        loss_ref[...] = jnp.zeros_like(loss_ref)
        gfg_ref[...] = jnp.zeros_like(gfg_ref)

    uext[POOL_HALO:, :] = proj_ref[0, :, 0:W_MIX]
    t = _row_index((tm, POOL_GC), i * tm)
    for g, w in enumerate(POOL_WINDOWS):
        lanes = slice(g * POOL_GC, (g + 1) * POOL_GC)
        e = uext[:, lanes]
        s = e
        sh = 1
        while sh < w:
            s = s + pltpu.roll(s, sh, 0)
            sh *= 2
        inv_cnt = 1.0 / jnp.minimum(t + 1, w).astype(F32)
        d = (s[POOL_HALO:] * inv_cnt - e[POOL_HALO:]).astype(BF16)
        dpool_ref[0, :, lanes] = d
        z = jnp.dot(d, poolw_ref[g].astype(BF16), preferred_element_type=F32) + poolb_ref[:, lanes]
        z_ref[0, :, lanes] = z
        a_gate = proj_ref[0, :, W_MIX + g * POOL_GC:W_MIX + (g + 1) * POOL_GC]
        y_s[:, lanes] = (z * pools_ref[:, lanes] * _silu(a_gate)).astype(BF16)

    hext[CONV_HALO:CONV_HALO + tm, :] = proj_ref[0, :, 2 * W_MIX:3 * W_MIX] * jax.nn.sigmoid(proj_ref[0, :, 3 * W_MIX:4 * W_MIX])
    taps = _conv_taps()
    for c0 in range(0, tm, CONV_ROWS):
        for l0 in range(0, W_MIX, LANES):
            lanes = slice(l0, l0 + LANES)
            acc = jnp.zeros((CONV_ROWS, LANES), F32) + cb_ref[:, lanes]
            for s in range(SUBLANES):
                part = jnp.zeros((CONV_ROWS + SUBLANES, LANES), F32)
                for q, k in taps[s]:
                    r0 = c0 + SUBLANES * q
                    part = part + hext[r0:r0 + CONV_ROWS + SUBLANES, lanes] * dw_ref[k:k + 1, lanes]
                acc = acc + part[s:s + CONV_ROWS]
            h1_ref[0, c0:c0 + CONV_ROWS, lanes] = acc
    h1 = h1_ref[0]
    mu = jnp.mean(h1, axis=-1, keepdims=True)
    xc = h1 - mu
    rstd = lax.rsqrt(jnp.mean(xc * xc, axis=-1, keepdims=True) + LN_EPS)
    h3 = _silu(xc * rstd * lg_ref[...] + lb_ref[...])
    ob = jnp.dot(h3.astype(BF16), pww_ref[...], preferred_element_type=F32) + pwb_ref[...]
    ob_ref[0] = ob
    y_s[:, W_MIX:] = (ob * _silu(proj_ref[0, :, 4 * W_MIX:5 * W_MIX])).astype(BF16)

    h = x_ref[0] + jnp.dot(y_s[...], wout_ref[...], preferred_element_type=F32)
    r2 = lax.rsqrt(jnp.mean(h * h, axis=-1, keepdims=True) + RMS_EPS)
    hr = h * r2
    err = hr * fg_ref[...] - tgt_ref[0]
    loss_ref[...] += jnp.sum(err * err) * (0.5 / D_MODEL)
    dout = err * (1.0 / D_MODEL)
    gfg_ref[...] += _colsum8(dout * hr)
    gy = dout * fg_ref[...]
    dh_ref[0] = r2 * (gy - hr * jnp.mean(gy * hr, axis=-1, keepdims=True))

    uext[0:POOL_HALO, :] = uext[tm:tm + POOL_HALO, :]
    hext[0:CONV_HALO, :] = hext[tm:tm + CONV_HALO, :]


def _row(v):
    return v.reshape(1, -1)


def _fwd(x, tgt, proj, pool_w, pool_b, pool_scale, conv_dw_f, conv_b, ln_g, ln_b, pw_w_b, pw_b, w_out_b, final_g):
    bl, s, _ = x.shape
    tm = TILE_M
    tok = lambda n: pl.BlockSpec((1, tm, n), lambda b, i: (b, i, 0))
    acc = lambda n: pl.BlockSpec((SUBLANES, n), lambda b, i: (0, 0))
    act = lambda n: jax.ShapeDtypeStruct((bl, s, n), F32)
    return pl.pallas_call(
        functools.partial(_fwd_body),
        name="fwd",
        grid=(bl, s // tm),
        in_specs=[tok(D_MODEL), tok(D_MODEL), tok(D_IN)] + [VMEM] * 11,
        out_specs=[tok(W_MIX), tok(W_MIX), tok(W_MIX), tok(W_MIX), tok(D_MODEL), acc(LANES), acc(D_MODEL)],
        out_shape=[act(W_MIX), act(W_MIX), act(W_MIX), jax.ShapeDtypeStruct((bl, s, W_MIX), BF16), act(D_MODEL),
                   jax.ShapeDtypeStruct((SUBLANES, LANES), F32), jax.ShapeDtypeStruct((SUBLANES, D_MODEL), F32)],
        scratch_shapes=[pltpu.VMEM((POOL_HALO + tm, W_MIX), F32), pltpu.VMEM((CONV_HALO + tm + SUBLANES, W_MIX), F32),
                        pltpu.VMEM((tm, D_MODEL), BF16)],
        compiler_params=pltpu.CompilerParams(dimension_semantics=("arbitrary", "arbitrary"), vmem_limit_bytes=VMEM_LIMIT),
    )(x, tgt, proj, pool_w, pool_b.reshape(1, W_MIX), _row(pool_scale), conv_dw_f, _row(conv_b),
      _row(ln_g), _row(ln_b), pw_w_b, _row(pw_b), w_out_b, _row(final_g))


NT_DIMS = (((1,), (1,)), ((), ()))
TN_DIMS = (((0,), (0,)), ((), ()))
V_POOL_B, V_POOL_SCALE, V_CONV_B, V_LN_G, V_LN_B, V_PW_B = range(6)


def _bwd_taps():
    taps = {s: [] for s in range(SUBLANES)}
    for k in range(CONV_WIDTH):
        e = CONV_WIDTH - 1 - k
        taps[e % SUBLANES].append((e // SUBLANES, k))
    return taps


def _bwd1_body(n_tiles, dh_ref, proj_ref, h1_ref, z_ref, ob_ref, dpool_ref, wout_ref, pww_ref, poolw_ref, pools_ref, dw_ref,
               lg_ref, lb_ref,
               dproj_ref, gwout_ref, gpww_ref, gpoolw_ref, gvec_ref, gdw_ref,
               vext, gext, gsh, y_s):
    b = pl.program_id(0)
    i = pl.program_id(1)
    tm = TILE_M
    t0 = (n_tiles - 1 - i) * tm

    @pl.when(i == 0)
    def _():
        vext[tm:, :] = jnp.zeros((POOL_HALO, W_MIX), F32)
        gext[tm:, :] = jnp.zeros((CONV_HALO, W_MIX), F32)

    @pl.when((b == 0) & (i == 0))
    def _():
        gwout_ref[...] = jnp.zeros_like(gwout_ref)
        gpww_ref[...] = jnp.zeros_like(gpww_ref)
        gpoolw_ref[...] = jnp.zeros_like(gpoolw_ref)
        gvec_ref[...] = jnp.zeros_like(gvec_ref)
        gdw_ref[...] = jnp.zeros_like(gdw_ref)

    sub = tm // BWD_SUBTILES

    def dense_phase(r0):
        rs = slice(r0, r0 + sub)
        dhb = dh_ref[0, rs, :].astype(BF16)
        dy_a = lax.dot_general(dhb, wout_ref[0:W_MIX, :], NT_DIMS, preferred_element_type=F32)
        dy_b = lax.dot_general(dhb, wout_ref[W_MIX:, :], NT_DIMS, preferred_element_type=F32)

        a_gate = proj_ref[0, rs, W_MIX:2 * W_MIX]
        sig_a = jax.nn.sigmoid(a_gate)
        silu_a = a_gate * sig_a
        z = z_ref[0, rs, :]
        out_a = z * pools_ref[...]
        y_s[rs, 0:W_MIX] = (out_a * silu_a).astype(BF16)
        d_out_a = dy_a * silu_a
        d_ag = dy_a * out_a * (sig_a * (1.0 + a_gate * (1.0 - sig_a)))
        dproj_ref[0, rs, W_MIX:2 * W_MIX] = d_ag.astype(BF16)
        gvec_ref[V_POOL_SCALE] += _colsum8(d_out_a * z)
        dz = d_out_a * pools_ref[...]
        gvec_ref[V_POOL_B] += _colsum8(dz)
        t = _row_index((sub, POOL_GC), t0 + r0)
        n_ext = sub + POOL_HALO
        for g, w in enumerate(POOL_WINDOWS):
            lanes = slice(g * POOL_GC, (g + 1) * POOL_GC)
            dzg = dz[:, lanes].astype(BF16)
            gpoolw_ref[g] += lax.dot_general(dpool_ref[0, rs, lanes], dzg, TN_DIMS, preferred_element_type=F32)
            dd = lax.dot_general(dzg, poolw_ref[g].astype(BF16), NT_DIMS, preferred_element_type=F32)
            vext[rs, lanes] = dd * (1.0 / jnp.minimum(t + 1, w).astype(F32))
            s = vext[r0:r0 + n_ext, lanes]
            sh = 1
            while sh < w:
                s = s + pltpu.roll(s, n_ext - sh, 0)
                sh *= 2
            dproj_ref[0, rs, lanes] = (s[0:sub] - dd).astype(BF16)

        b_gate = proj_ref[0, rs, 4 * W_MIX:5 * W_MIX]
        sig_b = jax.nn.sigmoid(b_gate)
        silu_b = b_gate * sig_b
        ob = ob_ref[0, rs, :]
        y_s[rs, W_MIX:] = (ob * silu_b).astype(BF16)
        d_ob = dy_b * silu_b
        dproj_ref[0, rs, 4 * W_MIX:5 * W_MIX] = (dy_b * ob * (sig_b * (1.0 + b_gate * (1.0 - sig_b)))).astype(BF16)
        gvec_ref[V_PW_B] += _colsum8(d_ob)
        gwout_ref[...] += lax.dot_general(y_s[rs, :], dhb, TN_DIMS, preferred_element_type=F32)

        h1 = h1_ref[0, rs, :]
        mu = jnp.mean(h1, axis=-1, keepdims=True)
        xc = h1 - mu
        rstd = lax.rsqrt(jnp.mean(xc * xc, axis=-1, keepdims=True) + LN_EPS)
        xhat = xc * rstd
        h2 = xhat * lg_ref[...] + lb_ref[...]
        sig2 = jax.nn.sigmoid(h2)
        d_obb = d_ob.astype(BF16)
        gpww_ref[...] += lax.dot_general((h2 * sig2).astype(BF16), d_obb, TN_DIMS, preferred_element_type=F32)
        dh2 = lax.dot_general(d_obb, pww_ref[...], NT_DIMS, preferred_element_type=F32) * (sig2 * (1.0 + h2 * (1.0 - sig2)))
        gvec_ref[V_LN_G] += _colsum8(dh2 * xhat)
        gvec_ref[V_LN_B] += _colsum8(dh2)
        dxh = dh2 * lg_ref[...]
        dh1 = rstd * (dxh - jnp.mean(dxh, axis=-1, keepdims=True) - xhat * jnp.mean(dxh * xhat, axis=-1, keepdims=True))
        gvec_ref[V_CONV_B] += _colsum8(dh1)
        gext[rs, :] = dh1

    taps = _bwd_taps()
    cr = BWD_CONV_ROWS
    win = cr + CONV_HALO - SUBLANES

    def conv_phase(r0):
        for c0 in range(r0, r0 + sub, cr):
            for l0 in range(0, W_MIX, LANES):
                lanes = slice(l0, l0 + LANES)
                b_val = proj_ref[0, c0:c0 + cr, 2 * W_MIX + l0:2 * W_MIX + l0 + LANES]
                sg = jax.nn.sigmoid(proj_ref[0, c0:c0 + cr, 3 * W_MIX + l0:3 * W_MIX + l0 + LANES])
                h0c = b_val * sg
                acc = jnp.zeros((cr, LANES), F32)
                for s in range(SUBLANES):
                    gsh[...] = gext[c0 + s:c0 + s + win, lanes]
                    for q, k in taps[s]:
                        gq = gsh[SUBLANES * q:SUBLANES * q + cr, :]
                        acc = acc + gq * dw_ref[k:k + 1, lanes]
                        gdw_ref[k, :, lanes] += _colsum8(h0c * gq)
                d_bval = acc * sg
                dproj_ref[0, c0:c0 + cr, 2 * W_MIX + l0:2 * W_MIX + l0 + LANES] = d_bval.astype(BF16)
                dproj_ref[0, c0:c0 + cr, 3 * W_MIX + l0:3 * W_MIX + l0 + LANES] = (d_bval * b_val * (1.0 - sg)).astype(BF16)

    starts = [k * sub for k in reversed(range(BWD_SUBTILES))]
    for r0 in starts:
        dense_phase(r0)
    for r0 in starts:
        conv_phase(r0)
    vext[tm:, :] = vext[0:POOL_HALO, :]
    gext[tm:, :] = gext[0:CONV_HALO, :]


def _bwd1(dh, proj, h1, z, ob, dpool, w_out_b, pw_w_b, pool_w, pool_scale, conv_dw_f, ln_g, ln_b):
    bl, s, _ = dh.shape
    tm = TILE_M
    nt = s // tm
    tok = lambda n: pl.BlockSpec((1, tm, n), lambda b, i: (b, nt - 1 - i, 0))
    res = lambda shape: pl.BlockSpec(shape, lambda b, i: (0,) * len(shape))
    out_shapes = [(D_MODEL, D_MODEL), (W_MIX, W_MIX), (len(POOL_WINDOWS), POOL_GC, POOL_GC), (6, SUBLANES, W_MIX),
                  (CONV_WIDTH, SUBLANES, W_MIX)]
    return pl.pallas_call(
        functools.partial(_bwd1_body, nt),
        name="bwd1",
        grid=(bl, nt),
        in_specs=[tok(D_MODEL), tok(D_IN), tok(W_MIX), tok(W_MIX), tok(W_MIX), tok(W_MIX)] + [VMEM] * 7,
        out_specs=[tok(D_IN)] + [res(sh) for sh in out_shapes],
        out_shape=[jax.ShapeDtypeStruct((bl, s, D_IN), BF16)] + [jax.ShapeDtypeStruct(sh, F32) for sh in out_shapes],
        scratch_shapes=[pltpu.VMEM((tm + POOL_HALO, W_MIX), F32), pltpu.VMEM((tm + CONV_HALO, W_MIX), F32),
                        pltpu.VMEM((BWD_CONV_ROWS + CONV_HALO - SUBLANES, LANES), F32), pltpu.VMEM((tm, D_MODEL), BF16)],
        compiler_params=pltpu.CompilerParams(dimension_semantics=("arbitrary", "arbitrary"), vmem_limit_bytes=VMEM_LIMIT),
    )(dh, proj, h1, z, ob, dpool, w_out_b, pw_w_b, pool_w, _row(pool_scale), conv_dw_f, _row(ln_g), _row(ln_b))


MESH = pl.DeviceIdType.MESH
W_IN_BLK = D_IN // N_DEV
W_OUT_BLK = D_MODEL // N_DEV
PW_BLK = W_MIX // N_DEV
DW_BLK = W_MIX // N_DEV
N_PEER = N_DEV - 1


SLAB = 2 * W_IN_BLK
N_SLABS = 4
F_SIB, F_X, F_Y, F_ON, F_PASS_X, F_PASS_Y, F_PASS_D = range(7)
W_IN_ONLY, OTHERS = (0,), (1, 2, 3)


def _inproj_body(n_tiles, x_ref, ng_ref, wint_ref, wout_ref, pww_ref, dw_ref,
                 proj_hbm, wt_ref, woutf_ref, pwwf_ref, dwf_ref,
                 hn_s, wt_s, wout_s, pww_s, dw_blk, stage, out_sems, send_sems, recv_sems):
    s = pl.program_id(0)
    i = pl.program_id(1)
    tm = TILE_IN
    x, y, c = lax.axis_index("x"), lax.axis_index("y"), lax.axis_index("c")
    sibling = (x, y, 1 - c)
    xn, yn, dg = (1 - x, y), (x, 1 - y), (1 - x, 1 - y)
    pick = lambda a, b: tuple(jnp.where(c == 1, p, q) for p, q in zip(a, b))
    on_from, on_to = pick(xn, yn), pick(yn, xn)

    def blocks(px, py, pc):
        k = 4 * px + 2 * py + pc
        rows = lambda ref, n: ref.at[pl.ds(pl.multiple_of(k * n, n), n), :]
        return (rows(wt_s, W_IN_BLK), rows(wout_s, W_OUT_BLK), rows(pww_s, PW_BLK), dw_blk.at[k])

    def copies(arrays, flow, block, to):
        refs = blocks(*block)
        return [pltpu.make_async_remote_copy(src_ref=refs[a], dst_ref=refs[a], send_sem=send_sems.at[a, flow],
                                             recv_sem=recv_sems.at[a, flow], device_id=to, device_id_type=MESH)
                for a in arrays]

    def start(arrays):
        for flow, to in ((F_SIB, sibling), (F_X, (*xn, c)), (F_Y, (*yn, c))):
            for cp in copies(arrays, flow, (x, y, c), to):
                cp.start()

    def direct(arrays):
        for flow, chip in ((F_X, xn), (F_Y, yn)):
            for cp in copies(arrays, flow, (*chip, c), sibling):
                cp.wait_recv()
        for cp in copies(arrays, F_ON, (*on_from, c), (*on_to, c)):
            cp.start()
        for flow, chip in ((F_PASS_X, xn), (F_PASS_Y, yn)):
            for cp in copies(arrays, flow, (*chip, c), sibling):
                cp.start()

    def diagonal(arrays):
        for cp in copies(arrays, F_ON, (*dg, c), sibling):
            cp.wait_recv()
        for cp in copies(arrays, F_PASS_D, (*dg, c), sibling):
            cp.start()

    def from_sibling(arrays, flow, chip):
        for cp in copies(arrays, flow, (*chip, 1 - c), sibling):
            cp.wait_recv()

    def finish(arrays):
        mine = (x, y, c)
        for flow, block in ((F_SIB, mine), (F_X, mine), (F_Y, mine), (F_ON, (*on_from, c)), (F_PASS_X, (*xn, c)),
                            (F_PASS_Y, (*yn, c)), (F_PASS_D, (*dg, c))):
            for cp in copies(arrays, flow, block, sibling):
                cp.wait_send()

    step = s * n_tiles + i
    par = step % 2
    first = step == 0

    @pl.when(first)
    def _():
        mine = blocks(x, y, c)
        mine[0][...] = wint_ref[...].astype(BF16)
        mine[1][...] = wout_ref[...].astype(BF16)
        mine[2][...] = pww_ref[...].astype(BF16)
        mine[3][...] = dw_ref[...]
        start(W_IN_ONLY)
        start(OTHERS)

    rows_i = pl.ds(pl.multiple_of(i * tm, tm), tm)

    @pl.when(s == 0)
    def _():
        xt = x_ref[0]
        r1 = lax.rsqrt(jnp.mean(xt * xt, axis=-1, keepdims=True) + RMS_EPS)
        hn_s[rows_i, :] = (xt * r1 * ng_ref[...]).astype(BF16)

    @pl.when(first)
    def _():
        from_sibling(W_IN_ONLY, F_SIB, (x, y))

    @pl.when((s == 1) & (i == 0))
    def _():
        direct(W_IN_ONLY)
        from_sibling(W_IN_ONLY, F_PASS_Y, yn)

    @pl.when((s == 2) & (i == 0))
    def _():
        from_sibling(W_IN_ONLY, F_PASS_X, xn)

    @pl.when((s == 3) & (i == 0))
    def _():
        diagonal(W_IN_ONLY)
        direct(OTHERS)
        from_sibling(W_IN_ONLY, F_PASS_D, dg)

    tiles_per_seq = proj_hbm.shape[1] // tm

    def out_copy(parity, col0):
        dst = proj_hbm.at[i // tiles_per_seq, pl.ds(pl.multiple_of((i % tiles_per_seq) * tm, tm), tm),
                          pl.ds(pl.multiple_of(col0, LANES), SLAB)]
        return pltpu.make_async_copy(stage.at[parity], dst, out_sems.at[parity])

    @pl.when(step >= 2)
    def _():
        out_copy(par, 0).wait()

    chip = jnp.where(s < 2, 2 * x, 2 * (1 - x)) + jnp.where((s == 0) | (s == 2), y, 1 - y)
    slab = wt_s[pl.ds(pl.multiple_of(chip * SLAB, SLAB), SLAB), :]
    stage[par] = lax.dot_general(hn_s[rows_i, :], slab, NT_DIMS, preferred_element_type=F32)
    out_copy(par, chip * SLAB).start()

    @pl.when(step == N_SLABS * n_tiles - 1)
    def _():
        out_copy(par, 0).wait()
        out_copy(1 - par, 0).wait()
        from_sibling(OTHERS, F_SIB, (x, y))
        from_sibling(OTHERS, F_PASS_X, xn)
        from_sibling(OTHERS, F_PASS_Y, yn)
        diagonal(OTHERS)
        from_sibling(OTHERS, F_PASS_D, dg)
        finish(W_IN_ONLY)
        finish(OTHERS)
        wt_ref[...] = wt_s[...]
        woutf_ref[...] = wout_s[...]
        pwwf_ref[...] = pww_s[...]
        for k in range(N_DEV):
            dwf_ref[:, k * DW_BLK:(k + 1) * DW_BLK] = dw_blk[k]


def _inproj(x, norm_g, w_in_t, w_out, pw_w, conv_dw):
    bl, s, _ = x.shape
    tm = TILE_IN
    nt = s // tm
    n_tiles = bl * nt

    def x_map(phase, i):
        i = jnp.where(phase == 0, i, n_tiles - 1)
        return (i // nt, i % nt, 0)

    dma = pltpu.SemaphoreType.DMA
    return pl.pallas_call(
        functools.partial(_inproj_body, n_tiles),
        name="inproj",
        grid=(N_SLABS, n_tiles),
        in_specs=[pl.BlockSpec((1, tm, D_MODEL), x_map)] + [VMEM] * 5,
        out_specs=[pl.BlockSpec(memory_space=pl.ANY)] + [VMEM] * 4,
        out_shape=[jax.ShapeDtypeStruct((bl, s, D_IN), F32), jax.ShapeDtypeStruct((D_IN, D_MODEL), BF16),
                   jax.ShapeDtypeStruct((D_MODEL, D_MODEL), BF16), jax.ShapeDtypeStruct((W_MIX, W_MIX), BF16),
                   jax.ShapeDtypeStruct((CONV_WIDTH, W_MIX), F32)],
        scratch_shapes=[pltpu.VMEM((n_tiles * tm, D_MODEL), BF16), pltpu.VMEM((D_IN, D_MODEL), BF16),
                        pltpu.VMEM((D_MODEL, D_MODEL), BF16), pltpu.VMEM((W_MIX, W_MIX), BF16),
                        pltpu.VMEM((N_DEV, CONV_WIDTH, DW_BLK), F32), pltpu.VMEM((2, tm, SLAB), F32),
                        dma((2,)), dma((4, 7)), dma((4, 7))],
        compiler_params=pltpu.CompilerParams(dimension_semantics=("arbitrary", "arbitrary"), vmem_limit_bytes=VMEM_LIMIT),
    )(x, _row(norm_g), w_in_t, w_out, pw_w, conv_dw)


B_PLAIN, B_ON, B_MERGED = range(3)


def _routes():
    x, y, c = lax.axis_index("x"), lax.axis_index("y"), lax.axis_index("c")
    xn, yn = (1 - x, y), (x, 1 - y)
    pick = lambda a, b: tuple(jnp.where(c == 1, p, q) for p, q in zip(a, b))
    return pick(xn, yn), pick(yn, xn), (1 - x, 1 - y), (x, y), c


def _chip_index(chip):
    return 2 * chip[0] + chip[1]


def _copy(src, dst, send_sems, recv_sems, a, k, to):
    return pltpu.make_async_remote_copy(src_ref=src, dst_ref=dst, send_sem=send_sems.at[a, k], recv_sem=recv_sems.at[a, k],
                                        device_id=to, device_id_type=MESH)


def _rsb_send(plain, on, rb_plain, rb_on, send_b, recv_b):
    first, _, _, _, c = _routes()
    for k, srcs, dsts in ((B_ON, on, rb_on), (B_PLAIN, plain, rb_plain)):
        for a, (s, d) in enumerate(zip(srcs, dsts)):
            _copy(s, d, send_b, recv_b, a, k, (*first, c)).start()


def _rsb_merge(second_sums, merged, rb_on, rb_merged, send_b, recv_b):
    _, second, _, _, c = _routes()
    for a, (own, m, got, dst) in enumerate(zip(second_sums, merged, rb_on, rb_merged)):
        _copy(got, got, send_b, recv_b, a, B_ON, (*second, c)).wait_recv()
        m[...] = (own[...] + got[...].astype(F32)).astype(BF16)
        _copy(m, dst, send_b, recv_b, a, B_MERGED, (*second, c)).start()


def _rsb_total(own_sums, plain, on, merged, rb_plain, rb_merged, send_b, recv_b):
    _, second, _, _, c = _routes()
    out = []
    for a, (own, p, g) in enumerate(zip(own_sums, rb_plain, rb_merged)):
        _copy(p, p, send_b, recv_b, a, B_PLAIN, (*second, c)).wait_recv()
        _copy(g, g, send_b, recv_b, a, B_MERGED, (*second, c)).wait_recv()
        out.append(own[...] + p[...].astype(F32) + g[...].astype(F32))
    for k, srcs in ((B_PLAIN, plain), (B_ON, on), (B_MERGED, merged)):
        for a, s in enumerate(srcs):
            _copy(s, s, send_b, recv_b, a, k, (*second, c)).wait_send()
    return out


R_LOSS, R_FINAL_G, R_POOL, R_CONV, R_LN = range(5)
VEC_ROWS = 16
DW_ROWS = 32
A_PART, A_X, A_Y, A_ON, A_HALF = range(5)
HALF_DTYPES = (F32, BF16, BF16)
STEP_SUMS, STEP_MERGE, STEP_HALF = 1, 3, 5


def _bwd2a_body(n_steps, x_ref, dproj_ref, ng_ref, gwout_ref, gpww_ref, loss_ref, gfg_ref, gvec_ref, gdw_ref, gpoolw_ref,
                win_own_ref, win_staged_ref, owout_ref, opww_ref, ovec_ref, odw_ref, opoolw_ref,
                acc, st_win, a_win, a_wout, a_pww, st_wout, st_pww, rb_wout, rb_pww, part_vec, part_dw, part_pw,
                sib_vec, sib_dw, sib_pw, res_vec, res_dw, res_pw, half_vec, half_dw, half_pw,
                send_a, recv_a, send_b, recv_b, send_r, recv_r, send_w, recv_w):
    step = pl.program_id(0) * pl.num_programs(1) + pl.program_id(1)
    x, y, c = lax.axis_index("x"), lax.axis_index("y"), lax.axis_index("c")
    sibling = (x, y, 1 - c)
    xn, yn = (1 - x, y), (x, 1 - y)
    first, second, diag, own, _ = _routes()
    rows = lambda ref, k, n: ref.at[pl.ds(pl.multiple_of(k * n, n), n), :]

    blocks = lambda k: (rows(gwout_ref, k, W_OUT_BLK), rows(gpww_ref, k, PW_BLK))
    landed = lambda chip: (a_wout.at[chip], a_pww.at[chip])
    staged = lambda k: (st_wout.at[k], st_pww.at[k])
    arrived = lambda k: (rb_wout.at[k], rb_pww.at[k])

    parts, sibs, halves = (part_vec, part_dw, part_pw), (sib_vec, sib_dw, sib_pw), (half_vec, half_dw, half_pw)
    results = (res_vec, res_dw, res_pw)
    half_rows = lambda ref: pl.ds(pl.multiple_of(c * (ref.shape[0] // 2), SUBLANES), ref.shape[0] // 2)
    slot = lambda chip: tuple(h.at[_chip_index(chip)] for h in halves)

    @pl.when(step == 0)
    def _():
        acc[...] = jnp.zeros_like(acc)
        sum8 = lambda v: jnp.sum(v, axis=0, keepdims=True)
        part_vec[...] = jnp.zeros((VEC_ROWS, D_MODEL), F32)
        part_vec[R_LOSS:R_LOSS + 1, :] = jnp.broadcast_to(loss_ref[0:1, 0:1], (1, D_MODEL))
        part_vec[R_FINAL_G:R_FINAL_G + 1, :] = sum8(gfg_ref[...])
        for row, (lo, hi) in ((R_POOL, (V_POOL_B, V_POOL_SCALE)), (R_CONV, (V_CONV_B, V_LN_G)), (R_LN, (V_LN_B, V_PW_B))):
            part_vec[row:row + 1, 0:W_MIX] = sum8(gvec_ref[lo])
            part_vec[row:row + 1, W_MIX:] = sum8(gvec_ref[hi])
        part_dw[0:CONV_WIDTH, :] = jnp.sum(gdw_ref[...], axis=1)
        part_dw[CONV_WIDTH:, :] = jnp.zeros((DW_ROWS - CONV_WIDTH, W_MIX), F32)
        part_pw[...] = gpoolw_ref[...].reshape(len(POOL_WINDOWS) * POOL_GC, POOL_GC)
        for a, (p, s) in enumerate(zip(parts, sibs)):
            _copy(p, s, send_r, recv_r, a, A_PART, sibling).start()
        for chip in range(4):
            for a, (s, d) in enumerate(zip(blocks(2 * chip + (1 - c)), landed(chip))):
                _copy(s, d, send_a, recv_a, a, chip, sibling).start()

    x_t = x_ref[0]
    r1 = lax.rsqrt(jnp.mean(x_t * x_t, axis=-1, keepdims=True) + RMS_EPS)
    hn = (x_t * r1 * ng_ref[...]).astype(BF16)
    acc[...] += lax.dot_general(dproj_ref[0], hn, TN_DIMS, preferred_element_type=F32)

    @pl.when(step == STEP_SUMS)
    def _():
        for a, (p, s, h) in enumerate(zip(parts, sibs, slot(own))):
            _copy(p, s, send_r, recv_r, a, A_PART, sibling).wait_recv()
            h[...] = (p[half_rows(p), :] + s[half_rows(s), :]).astype(h.dtype)
            for k, to in ((A_X, xn), (A_Y, yn)):
                _copy(h, h, send_r, recv_r, a, k, (*to, c)).start()
        for chip in range(4):
            for a, (m, l) in enumerate(zip(blocks(2 * chip + c), landed(chip))):
                _copy(l, l, send_a, recv_a, a, chip, sibling).wait_recv()
                l[...] = m[...] + l[...]
        for k, chip in ((B_ON, diag), (B_PLAIN, first)):
            for s, l in zip(staged(k), landed(_chip_index(chip))):
                s[...] = l[...].astype(BF16)
        _rsb_send(staged(B_PLAIN), staged(B_ON), arrived(B_PLAIN), arrived(B_ON), send_b, recv_b)

    @pl.when(step == STEP_MERGE)
    def _():
        _rsb_merge(landed(_chip_index(second)), staged(B_MERGED), arrived(B_ON), arrived(B_MERGED), send_b, recv_b)
        for a in range(3):
            for k, chip in ((A_X, xn), (A_Y, yn)):
                h = slot(chip)[a]
                _copy(h, h, send_r, recv_r, a, k, sibling).wait_recv()
            h = slot(first)[a]
            _copy(h, h, send_r, recv_r, a, A_ON, (*second, c)).start()

    @pl.when(step == STEP_HALF)
    def _():
        for a, (r, hs) in enumerate(zip(results, halves)):
            h = slot(diag)[a]
            _copy(h, h, send_r, recv_r, a, A_ON, sibling).wait_recv()
            mine = r.at[half_rows(r), :]
            chip_sums = [hs[k].astype(F32) for k in range(4)]
            mine[...] = (chip_sums[0] + chip_sums[1]) + (chip_sums[2] + chip_sums[3])
            _copy(mine, mine, send_r, recv_r, a, A_HALF, sibling).start()

    @pl.when(step == n_steps - 1)
    def _():
        block = lambda k: rows(acc, k, W_IN_BLK)
        w_copy = lambda chip: _copy(st_win.at[chip], a_win.at[chip], send_w, recv_w, 0, chip, sibling)
        for chip in range(4):
            st_win[chip] = block(2 * chip + (1 - c))[...].astype(BF16)
            w_copy(chip).start()

        owout_ref[...], opww_ref[...] = _rsb_total(landed(_chip_index(own)), staged(B_PLAIN), staged(B_ON), staged(B_MERGED),
                                                   arrived(B_PLAIN), arrived(B_MERGED), send_b, recv_b)
        for a, (r, p, s) in enumerate(zip(results, parts, sibs)):
            theirs = r.at[pl.ds(pl.multiple_of((1 - c) * (r.shape[0] // 2), SUBLANES), r.shape[0] // 2), :]
            _copy(theirs, theirs, send_r, recv_r, a, A_HALF, sibling).wait_recv()
            _copy(p, s, send_r, recv_r, a, A_PART, sibling).wait_send()
            for k, h in ((A_X, slot(own)[a]), (A_Y, slot(own)[a]), (A_ON, slot(first)[a])):
                _copy(h, h, send_r, recv_r, a, k, sibling).wait_send()
            mine = r.at[half_rows(r), :]
            _copy(mine, mine, send_r, recv_r, a, A_HALF, sibling).wait_send()
        for chip in range(4):
            for a, (s, d) in enumerate(zip(blocks(2 * chip + (1 - c)), landed(chip))):
                _copy(s, d, send_a, recv_a, a, chip, sibling).wait_send()
        ovec_ref[...] = res_vec[...]
        opoolw_ref[...] = res_pw[...]
        for k in range(N_DEV):
            @pl.when(4 * x + 2 * y + c == k)
            def _():
                odw_ref[...] = res_dw[0:CONV_WIDTH, k * DW_BLK:(k + 1) * DW_BLK]

        for chip in range(4):
            w_copy(chip).wait()
        chip_sum = lambda chip: block(2 * _chip_index(chip) + c)[...] + a_win[_chip_index(chip)].astype(F32)
        win_own_ref[...] = chip_sum(own)
        for j, chip in enumerate((xn, yn, diag)):
            win_staged_ref[j] = chip_sum(chip).astype(BF16)


def _bwd2a(x, dproj, norm_g, gwout, gpww, loss8, gfg8, gvec, gdw8, gpoolw):
    bl, s, _ = x.shape
    tm = TILE_M
    nt = s // tm
    tok = lambda n: pl.BlockSpec((1, tm, n), lambda b, i: (b, i, 0))
    n_pw = len(POOL_WINDOWS) * POOL_GC
    win = (W_IN_BLK, D_MODEL)
    blk = ((W_OUT_BLK, D_MODEL), (PW_BLK, W_MIX))
    rep = ((VEC_ROWS, D_MODEL), (DW_ROWS, W_MIX), (n_pw, POOL_GC))
    half = lambda sh: (sh[0] // 2,) + sh[1:]
    dma = pltpu.SemaphoreType.DMA
    assert bl * nt > STEP_HALF + 1, "the exchanges' phases need their grid steps"
    return pl.pallas_call(
        functools.partial(_bwd2a_body, bl * nt),
        name="bwd2a",
        grid=(bl, nt),
        in_specs=[tok(D_MODEL), tok(D_IN)] + [VMEM] * 8,
        out_specs=[VMEM] * 7,
        out_shape=[jax.ShapeDtypeStruct(win, F32), jax.ShapeDtypeStruct((3,) + win, BF16)]
        + [jax.ShapeDtypeStruct(sh, F32) for sh in blk + (rep[0], (CONV_WIDTH, DW_BLK), rep[2])],
        scratch_shapes=[pltpu.VMEM((D_IN, D_MODEL), F32), pltpu.VMEM((4,) + win, BF16), pltpu.VMEM((4,) + win, BF16)]
        + [pltpu.VMEM((4,) + sh, F32) for sh in blk]
        + [pltpu.VMEM((3,) + sh, BF16) for sh in blk] * 2
        + [pltpu.VMEM(sh, F32) for sh in rep] * 3
        + [pltpu.VMEM((4,) + half(sh), dt) for sh, dt in zip(rep, HALF_DTYPES)]
        + [dma((2, 4)), dma((2, 4)), dma((2, 3)), dma((2, 3)), dma((3, 5)), dma((3, 5)), dma((1, 4)), dma((1, 4))],
        compiler_params=pltpu.CompilerParams(dimension_semantics=("arbitrary", "arbitrary"), vmem_limit_bytes=VMEM_LIMIT),
    )(x, dproj, _row(norm_g), gwout, gpww, loss8, gfg8, gvec, gdw8, gpoolw)


HBM = pl.BlockSpec(memory_space=pltpu.HBM)
SEM = pl.BlockSpec(memory_space=pltpu.SEMAPHORE)
EFFECT = pltpu.SideEffectType.DATAFLOW_SIDE_EFFECTING


def _owner_copies(src_ref, land_ref, send_sems, recv_sems):
    x, y, c = lax.axis_index("x"), lax.axis_index("y"), lax.axis_index("c")
    owners = ((1 - x, y, c), (x, 1 - y, c), (1 - x, 1 - y, c))
    return [pltpu.make_async_remote_copy(src_ref=src_ref.at[j], dst_ref=land_ref.at[j], send_sem=send_sems.at[j],
                                         recv_sem=recv_sems.at[j], device_id=owners[j], device_id_type=MESH)
            for j in range(3)]


def _rs_start(staged):
    def body(src_ref, land_ref, send_sems, recv_sems, src_thru, land_thru, token):
        for cp in _owner_copies(src_ref, land_ref, send_sems, recv_sems):
            cp.start()
        token[...] = jnp.zeros_like(token)

    dma = pltpu.SemaphoreType.DMA
    return pl.pallas_call(
        body, name="rs_start",
        out_shape=(dma((3,)), dma((3,)), pltpu.HBM(staged.shape, staged.dtype), pltpu.HBM(staged.shape, staged.dtype),
                   jax.ShapeDtypeStruct((SUBLANES, LANES), F32)),
        in_specs=(HBM, HBM), out_specs=(SEM, SEM, HBM, HBM, VMEM), input_output_aliases={0: 2, 1: 3},
        compiler_params=pltpu.CompilerParams(has_side_effects=EFFECT),
    )(pltpu.with_memory_space_constraint(staged, pltpu.HBM),
      pltpu.with_memory_space_constraint(lax.empty(staged.shape, staged.dtype), pltpu.HBM))


def _rs_wait(send_sems, recv_sems, src_thru, land_thru, after):
    def body(src_ref, land_ref, send_sems, recv_sems, after_ref, src_dead, got_ref):
        for cp in _owner_copies(src_ref, land_ref, send_sems, recv_sems):
            cp.wait_send()
            cp.wait_recv()

    return pl.pallas_call(
        body, name="rs_wait",
        out_shape=(pltpu.HBM(src_thru.shape, src_thru.dtype), pltpu.HBM(land_thru.shape, land_thru.dtype)),
        in_specs=(HBM, HBM, SEM, SEM, pl.BlockSpec(memory_space=pl.ANY)), out_specs=(HBM, HBM),
        input_output_aliases={0: 0, 1: 1},
        compiler_params=pltpu.CompilerParams(has_side_effects=EFFECT),
    )(src_thru, land_thru, send_sems, recv_sems, after)[1]


def _bwd2b_body(x_ref, dh_ref, dproj_ref, ng_ref, wint_ref, token_ref, gx_ref, gng_ref):
    @pl.when((pl.program_id(0) == 0) & (pl.program_id(1) == 0))
    def _():
        gng_ref[...] = jnp.zeros_like(gng_ref) + token_ref[0:1, 0:1]

    x_t = x_ref[0]
    r1 = lax.rsqrt(jnp.mean(x_t * x_t, axis=-1, keepdims=True) + RMS_EPS)
    xr = x_t * r1
    dhn = jnp.dot(dproj_ref[0], wint_ref[...], preferred_element_type=F32)
    gng_ref[...] += _colsum8(dhn * xr)
    gy = dhn * ng_ref[...]
    gx_ref[0] = dh_ref[0] + r1 * (gy - xr * jnp.mean(gy * xr, axis=-1, keepdims=True))


def _bwd2b(x, dh, dproj, norm_g, w_in_t, token):
    bl, s, _ = x.shape
    tm = TILE_M
    tok = lambda n: pl.BlockSpec((1, tm, n), lambda b, i: (b, i, 0))
    return pl.pallas_call(
        functools.partial(_bwd2b_body),
        name="bwd2b",
        grid=(bl, s // tm),
        in_specs=[tok(D_MODEL), tok(D_MODEL), tok(D_IN), VMEM, VMEM, VMEM],
        out_specs=[tok(D_MODEL), pl.BlockSpec((SUBLANES, D_MODEL), lambda b, i: (0, 0))],
        out_shape=[jax.ShapeDtypeStruct((bl, s, D_MODEL), F32), jax.ShapeDtypeStruct((SUBLANES, D_MODEL), F32)],
        compiler_params=pltpu.CompilerParams(dimension_semantics=("arbitrary", "arbitrary"), vmem_limit_bytes=VMEM_LIMIT),
    )(x, dh, dproj, _row(norm_g), w_in_t, token)


def _partial_copies(part_ref, land_ref, send_sems, recv_sems):
    x, y, c = lax.axis_index("x"), lax.axis_index("y"), lax.axis_index("c")
    other_chips = [(1 - x, y), (x, 1 - y), (1 - x, 1 - y)]
    peers = [(x, y, 1 - c)] + [(*chip, c) for chip in other_chips] + [(*chip, 1 - c) for chip in other_chips]
    return [pltpu.make_async_remote_copy(src_ref=part_ref, dst_ref=land_ref.at[4 * x + 2 * y + c], send_sem=send_sems.at[r],
                                         recv_sem=recv_sems.at[r], device_id=peers[r], device_id_type=MESH)
            for r in range(N_PEER)]


def _partials_start(part):
    def body(part_ref, land_ref, send_sems, recv_sems, part_thru, land_thru):
        for cp in _partial_copies(part_ref, land_ref, send_sems, recv_sems):
            cp.start()

    dma = pltpu.SemaphoreType.DMA
    land = (N_DEV,) + part.shape
    return pl.pallas_call(
        body, name="partials_start",
        out_shape=(dma((N_PEER,)), dma((N_PEER,)), pltpu.HBM(part.shape, part.dtype), pltpu.HBM(land, part.dtype)),
        in_specs=(HBM, HBM), out_specs=(SEM, SEM, HBM, HBM), input_output_aliases={0: 2, 1: 3},
        compiler_params=pltpu.CompilerParams(has_side_effects=EFFECT),
    )(pltpu.with_memory_space_constraint(part, pltpu.HBM),
      pltpu.with_memory_space_constraint(lax.empty(land, part.dtype), pltpu.HBM))


def _partials_wait(send_sems, recv_sems, part_thru, land_thru, after):
    def body(part_ref, land_ref, send_sems, recv_sems, after_ref, part_out, land_out):
        for cp in _partial_copies(part_ref, land_ref, send_sems, recv_sems):
            cp.wait_send()
            cp.wait_recv()

    return pl.pallas_call(
        body, name="partials_wait",
        out_shape=(pltpu.HBM(part_thru.shape, part_thru.dtype), pltpu.HBM(land_thru.shape, land_thru.dtype)),
        in_specs=(HBM, HBM, SEM, SEM, pl.BlockSpec(memory_space=pl.ANY)), out_specs=(HBM, HBM),
        input_output_aliases={0: 0, 1: 1},
        compiler_params=pltpu.CompilerParams(has_side_effects=EFFECT),
    )(part_thru, land_thru, send_sems, recv_sems, after)


ADAM_LR = 0.001
ADAM_B1 = 0.9
ADAM_B2 = 0.999
ADAM_EPS = 1e-08
ADAM_WD = 0.01
ADAM_STEP = 10


PARAMS = ("norm_g", "w_in", "pool_w", "pool_b", "pool_scale", "conv_dw", "conv_b", "ln_g", "ln_b", "pw_w", "pw_b", "w_out",
          "final_g")


LATE_PARAMS = ("norm_g",)
MAIN_PARAMS = tuple(p for p in PARAMS if p not in LATE_PARAMS)


def _adamw_refs(params, n_sources, refs):
    n = len(params)
    ws, ms, vs = (dict(zip(params, refs[k * n:(k + 1) * n])) for k in range(3))
    outs = refs[3 * n + n_sources:]
    gs, ds, m2s, v2s = (dict(zip(params, outs[k * n:(k + 1) * n])) for k in range(4))

    def update(name, g, at=slice(None)):
        m = ADAM_B1 * ms[name][at] + (1.0 - ADAM_B1) * g
        v = ADAM_B2 * vs[name][at] + (1.0 - ADAM_B2) * (g * g)
        m_hat = m / (1.0 - ADAM_B1 ** ADAM_STEP)
        v_hat = v / (1.0 - ADAM_B2 ** ADAM_STEP)
        gs[name][at] = g
        ds[name][at] = -ADAM_LR * (m_hat / (jnp.sqrt(v_hat) + ADAM_EPS) + ADAM_WD * ws[name][at])
        m2s[name][at] = m
        v2s[name][at] = v

    return update, refs[3 * n:3 * n + n_sources], outs[4 * n:]


def _adamw_late_body(*refs):
    update, (gng_ref, gng_others_ref), _ = _adamw_refs(LATE_PARAMS, 2, refs)
    me = 4 * lax.axis_index("x") + 2 * lax.axis_index("y") + lax.axis_index("c")
    g_norm = jnp.zeros(gng_ref.shape, F32)
    for k in range(N_DEV):
        g_norm = g_norm + jnp.where(me == k, gng_ref[...], gng_others_ref[k])
    update("norm_g", jnp.sum(g_norm, axis=0, keepdims=True))


def _adamw_body(*refs):
    update, sources, (loss_ref,) = _adamw_refs(MAIN_PARAMS, 7, refs)
    gwin_ref, gwin_others_ref, gwout_ref, gpww_ref, vec_ref, gdw_ref, gpoolw_ref = sources
    ws = dict(zip(MAIN_PARAMS, refs))
    row = lambda r, lo, hi: vec_ref[r:r + 1, lo:hi]
    loss_ref[...] = row(R_LOSS, 0, 1)
    g_w_in = gwin_ref[...]
    for j in range(3):
        g_w_in = g_w_in + gwin_others_ref[j].astype(F32)
    update("w_in", g_w_in)
    update("pool_w", gpoolw_ref[...].reshape(ws["pool_w"].shape))
    for g in range(len(POOL_WINDOWS)):
        update("pool_b", row(R_POOL, g * POOL_GC, (g + 1) * POOL_GC), at=slice(g, g + 1))
    update("pool_scale", row(R_POOL, W_MIX, 2 * W_MIX))
    update("conv_dw", gdw_ref[...])
    update("conv_b", row(R_CONV, 0, W_MIX))
    update("ln_g", row(R_CONV, W_MIX, 2 * W_MIX))
    update("ln_b", row(R_LN, 0, W_MIX))
    update("pw_w", gpww_ref[...])
    update("pw_b", row(R_LN, W_MIX, 2 * W_MIX))
    update("w_out", gwout_ref[...])
    update("final_g", row(R_FINAL_G, 0, D_MODEL))


def _adamw(body, name, params, ws, ms, vs, grad_sources, extra_out_shapes=()):
    n = len(params)
    flat = lambda a: a.reshape(1, -1) if a.ndim == 1 else a
    args = [flat(d[p]) for d in (ws, ms, vs) for p in params]
    outs = pl.pallas_call(
        body,
        name=name,
        in_specs=[VMEM] * (3 * n + len(grad_sources)),
        out_specs=[VMEM] * (4 * n + len(extra_out_shapes)),
        out_shape=[jax.ShapeDtypeStruct(a.shape, F32) for a in args[:n]] * 4 + list(extra_out_shapes),
        compiler_params=pltpu.CompilerParams(vmem_limit_bytes=VMEM_LIMIT),
    )(*args, *grad_sources)
    groups = tuple({p: outs[k * n + j].reshape(ws[p].shape) for j, p in enumerate(params)} for k in range(4))
    return groups, outs[4 * n:]


def kernel(x, norm_g, w_in, pool_w, pool_b, pool_scale, conv_dw, conv_b, ln_g, ln_b, pw_w, pw_b, w_out, final_g, loss_target, m_norm_g, m_w_in, m_pool_w, m_pool_b, m_pool_scale, m_conv_dw, m_conv_b, m_ln_g, m_ln_b, m_pw_w, m_pw_b, m_w_out, m_final_g, v_norm_g, v_w_in, v_pool_w, v_pool_b, v_pool_scale, v_conv_dw, v_conv_b, v_ln_g, v_ln_b, v_pw_w, v_pw_b, v_w_out, v_final_g):
    proj, w_in_t, w_out_b, pw_w_b, conv_dw_f = _inproj(x, norm_g, w_in.T, w_out, pw_w, conv_dw)
    h1, z, ob, dpool, dh, loss8, gfg8 = _fwd(x, loss_target, proj, pool_w, pool_b, pool_scale, conv_dw_f,
                                             conv_b, ln_g, ln_b, pw_w_b, pw_b, w_out_b, final_g)
    dproj, gwout, gpww, gpoolw, gvec, gdw8 = _bwd1(dh, proj, h1, z, ob, dpool, w_out_b, pw_w_b, pool_w, pool_scale,
                                                   conv_dw_f, ln_g, ln_b)
    win_own, win_staged, g_w_out, g_pw_w, vec, g_conv_dw, poolw_sum = _bwd2a(x, dproj, norm_g, gwout, gpww, loss8, gfg8, gvec,
                                                                             gdw8, gpoolw)
    send_sems, recv_sems, staged_thru, land_thru, token = _rs_start(win_staged)
    grad_x, gng8 = _bwd2b(x, dh, dproj, norm_g, w_in_t, token)
    ng_send, ng_recv, gng8, gng_land = _partials_start(gng8)
    win_others = _rs_wait(send_sems, recv_sems, staged_thru, land_thru, gng8)

    weights = dict(norm_g=norm_g, w_in=w_in.T, pool_w=pool_w, pool_b=pool_b, pool_scale=pool_scale, conv_dw=conv_dw, conv_b=conv_b,
                   ln_g=ln_g, ln_b=ln_b, pw_w=pw_w, pw_b=pw_b, w_out=w_out, final_g=final_g)
    m_in = dict(norm_g=m_norm_g, w_in=m_w_in.T, pool_w=m_pool_w, pool_b=m_pool_b, pool_scale=m_pool_scale, conv_dw=m_conv_dw,
                conv_b=m_conv_b, ln_g=m_ln_g, ln_b=m_ln_b, pw_w=m_pw_w, pw_b=m_pw_b, w_out=m_w_out, final_g=m_final_g)
    v_in = dict(norm_g=v_norm_g, w_in=v_w_in.T, pool_w=v_pool_w, pool_b=v_pool_b, pool_scale=v_pool_scale, conv_dw=v_conv_dw,
                conv_b=v_conv_b, ln_g=v_ln_g, ln_b=v_ln_b, pw_w=v_pw_w, pw_b=v_pw_b, w_out=v_w_out, final_g=v_final_g)
    main, (loss,) = _adamw(functools.partial(_adamw_body), "adamw", MAIN_PARAMS, weights, m_in, v_in,
                           (win_own, win_others, g_w_out, g_pw_w, vec, g_conv_dw, poolw_sum), [jax.ShapeDtypeStruct((1, 1), F32)])
    gng8, gng_others = _partials_wait(ng_send, ng_recv, gng8, gng_land, main[0]["w_out"])
    late, _ = _adamw(functools.partial(_adamw_late_body), "adamw_norm_g", LATE_PARAMS, weights, m_in, v_in, (gng8, gng_others))
    groups = [[(late[k] if p in LATE_PARAMS else main[k])[p] for p in PARAMS] for k in range(4)]
    k_in = PARAMS.index("w_in")
    for group in groups:
        group[k_in] = group[k_in].T
    grads, delta, new_m, new_v = groups
    return (loss.reshape(()), grad_x, *grads, *delta, *new_m, *new_v)
```

```python
import functools

import jax
import jax.numpy as jnp
from jax import lax
from jax.experimental import pallas as pl
from jax.experimental.pallas import tpu as pltpu

F32 = jnp.float32
BF16 = jnp.bfloat16

D_MODEL = 1024
W_MIX = 512
D_IN = 5 * W_MIX
POOL_WINDOWS = (2, 4, 8, 16)
POOL_GC = 128
CONV_WIDTH = 31
RMS_EPS = 1e-6
LN_EPS = 1e-5
N_DEV = 8

LANES = 128
SUBLANES = 8
TILE_M = 512
TILE_IN = 1024
POOL_HALO = 16
CONV_HALO = 32
CONV_ROWS = 128
BWD_CONV_ROWS = 64
BWD_SUBTILES = 2
VMEM_LIMIT = 56 * 1024 * 1024

VMEM = pl.BlockSpec(memory_space=pltpu.VMEM)


def _silu(v):
    return v * jax.nn.sigmoid(v)


def _colsum8(v):
    m, n = v.shape
    return jnp.sum(v.reshape(m // SUBLANES, SUBLANES, n), axis=0)


def _row_index(shape, t0):
    return lax.broadcasted_iota(jnp.int32, shape, 0) + t0


def _conv_taps():
    taps = {s: [] for s in range(SUBLANES)}
    for k in range(CONV_WIDTH):
        e = k + CONV_HALO - (CONV_WIDTH - 1)
        taps[e % SUBLANES].append((e // SUBLANES, k))
    return taps


def _fwd_body(x_ref, tgt_ref, proj_ref, poolw_ref, poolb_ref, pools_ref, dw_ref, cb_ref, lg_ref, lb_ref,
              pww_ref, pwb_ref, wout_ref, fg_ref,
              h1_ref, z_ref, ob_ref, dpool_ref, dh_ref, loss_ref, gfg_ref,
              uext, hext, y_s):
    b = pl.program_id(0)
    i = pl.program_id(1)
    tm = TILE_M

    @pl.when(i == 0)
    def _():
        uext[0:POOL_HALO, :] = jnp.zeros((POOL_HALO, W_MIX), F32)
        hext[0:CONV_HALO, :] = jnp.zeros((CONV_HALO, W_MIX), F32)
        hext[CONV_HALO + tm:, :] = jnp.zeros((SUBLANES, W_MIX), F32)

    @pl.when((b == 0) & (i == 0))
    def _():
        loss_ref[...] = jnp.zeros_like(loss_ref)
        gfg_ref[...] = jnp.zeros_like(gfg_ref)

    uext[POOL_HALO:, :] = proj_ref[0, :, 0:W_MIX]
    t = _row_index((tm, POOL_GC), i * tm)
    for g, w in enumerate(POOL_WINDOWS):
        lanes = slice(g * POOL_GC, (g + 1) * POOL_GC)
        e = uext[:, lanes]
        s = e
        sh = 1
        while sh < w:
            s = s + pltpu.roll(s, sh, 0)
            sh *= 2
        inv_cnt = 1.0 / jnp.minimum(t + 1, w).astype(F32)
        d = (s[POOL_HALO:] * inv_cnt - e[POOL_HALO:]).astype(BF16)
        dpool_ref[0, :, lanes] = d
        z = jnp.dot(d, poolw_ref[g].astype(BF16), preferred_element_type=F32) + poolb_ref[:, lanes]
        z_ref[0, :, lanes] = z
        a_gate = proj_ref[0, :, W_MIX + g * POOL_GC:W_MIX + (g + 1) * POOL_GC]
        y_s[:, lanes] = (z * pools_ref[:, lanes] * _silu(a_gate)).astype(BF16)

    hext[CONV_HALO:CONV_HALO + tm, :] = proj_ref[0, :, 2 * W_MIX:3 * W_MIX] * jax.nn.sigmoid(proj_ref[0, :, 3 * W_MIX:4 * W_MIX])
    taps = _conv_taps()
    for c0 in range(0, tm, CONV_ROWS):
        for l0 in range(0, W_MIX, LANES):
            lanes = slice(l0, l0 + LANES)
            acc = jnp.zeros((CONV_ROWS, LANES), F32) + cb_ref[:, lanes]
            for s in range(SUBLANES):
                part = jnp.zeros((CONV_ROWS + SUBLANES, LANES), F32)
                for q, k in taps[s]:
                    r0 = c0 + SUBLANES * q
                    part = part + hext[r0:r0 + CONV_ROWS + SUBLANES, lanes] * dw_ref[k:k + 1, lanes]
                acc = acc + part[s:s + CONV_ROWS]
            h1_ref[0, c0:c0 + CONV_ROWS, lanes] = acc
    h1 = h1_ref[0]
    mu = jnp.mean(h1, axis=-1, keepdims=True)
    xc = h1 - mu
    rstd = lax.rsqrt(jnp.mean(xc * xc, axis=-1, keepdims=True) + LN_EPS)
    h3 = _silu(xc * rstd * lg_ref[...] + lb_ref[...])
    ob = jnp.dot(h3.astype(BF16), pww_ref[...], preferred_element_type=F32) + pwb_ref[...]
    ob_ref[0] = ob
    y_s[:, W_MIX:] = (ob * _silu(proj_ref[0, :, 4 * W_MIX:5 * W_MIX])).astype(BF16)

    h = x_ref[0] + jnp.dot(y_s[...], wout_ref[...], preferred_element_type=F32)
    r2 = lax.rsqrt(jnp.mean(h * h, axis=-1, keepdims=True) + RMS_EPS)
    hr = h * r2
    err = hr * fg_ref[...] - tgt_ref[0]
    loss_ref[...] += jnp.sum(err * err) * (0.5 / D_MODEL)
    dout = err * (1.0 / D_MODEL)
    gfg_ref[...] += _colsum8(dout * hr)
    gy = dout * fg_ref[...]
    dh_ref[0] = r2 * (gy - hr * jnp.mean(gy * hr, axis=-1, keepdims=True))

    uext[0:POOL_HALO, :] = uext[tm:tm + POOL_HALO, :]
    hext[0:CONV_HALO, :] = hext[tm:tm + CONV_HALO, :]


def _row(v):
    return v.reshape(1, -1)


def _fwd(x, tgt, proj, pool_w, pool_b, pool_scale, conv_dw_f, conv_b, ln_g, ln_b, pw_w_b, pw_b, w_out_b, final_g):
    bl, s, _ = x.shape
    tm = TILE_M
    tok = lambda n: pl.BlockSpec((1, tm, n), lambda b, i: (b, i, 0))
    acc = lambda n: pl.BlockSpec((SUBLANES, n), lambda b, i: (0, 0))
    act = lambda n: jax.ShapeDtypeStruct((bl, s, n), F32)
    return pl.pallas_call(
        functools.partial(_fwd_body),
        name="fwd",
        grid=(bl, s // tm),
        in_specs=[tok(D_MODEL), tok(D_MODEL), tok(D_IN)] + [VMEM] * 11,
        out_specs=[tok(W_MIX), tok(W_MIX), tok(W_MIX), tok(W_MIX), tok(D_MODEL), acc(LANES), acc(D_MODEL)],
        out_shape=[act(W_MIX), act(W_MIX), act(W_MIX), jax.ShapeDtypeStruct((bl, s, W_MIX), BF16), act(D_MODEL),
                   jax.ShapeDtypeStruct((SUBLANES, LANES), F32), jax.ShapeDtypeStruct((SUBLANES, D_MODEL), F32)],
        scratch_shapes=[pltpu.VMEM((POOL_HALO + tm, W_MIX), F32), pltpu.VMEM((CONV_HALO + tm + SUBLANES, W_MIX), F32),
                        pltpu.VMEM((tm, D_MODEL), BF16)],
        compiler_params=pltpu.CompilerParams(dimension_semantics=("arbitrary", "arbitrary"), vmem_limit_bytes=VMEM_LIMIT),
    )(x, tgt, proj, pool_w, pool_b.reshape(1, W_MIX), _row(pool_scale), conv_dw_f, _row(conv_b),
      _row(ln_g), _row(ln_b), pw_w_b, _row(pw_b), w_out_b, _row(final_g))


NT_DIMS = (((1,), (1,)), ((), ()))
TN_DIMS = (((0,), (0,)), ((), ()))
V_POOL_B, V_POOL_SCALE, V_CONV_B, V_LN_G, V_LN_B, V_PW_B = range(6)


def _bwd_taps():
    taps = {s: [] for s in range(SUBLANES)}
    for k in range(CONV_WIDTH):
        e = CONV_WIDTH - 1 - k
        taps[e % SUBLANES].append((e // SUBLANES, k))
    return taps


def _bwd1_body(n_tiles, dh_ref, proj_ref, h1_ref, z_ref, ob_ref, dpool_ref, wout_ref, pww_ref, poolw_ref, pools_ref, dw_ref,
               lg_ref, lb_ref,
               dproj_ref, gwout_ref, gpww_ref, gpoolw_ref, gvec_ref, gdw_ref,
               vext, gext, gsh, y_s):
    b = pl.program_id(0)
    i = pl.program_id(1)
    tm = TILE_M
    t0 = (n_tiles - 1 - i) * tm

    @pl.when(i == 0)
    def _():
        vext[tm:, :] = jnp.zeros((POOL_HALO, W_MIX), F32)
        gext[tm:, :] = jnp.zeros((CONV_HALO, W_MIX), F32)

    @pl.when((b == 0) & (i == 0))
    def _():
        gwout_ref[...] = jnp.zeros_like(gwout_ref)
        gpww_ref[...] = jnp.zeros_like(gpww_ref)
        gpoolw_ref[...] = jnp.zeros_like(gpoolw_ref)
        gvec_ref[...] = jnp.zeros_like(gvec_ref)
        gdw_ref[...] = jnp.zeros_like(gdw_ref)

    sub = tm // BWD_SUBTILES

    def dense_phase(r0):
        rs = slice(r0, r0 + sub)
        dhb = dh_ref[0, rs, :].astype(BF16)
        dy_a = lax.dot_general(dhb, wout_ref[0:W_MIX, :], NT_DIMS, preferred_element_type=F32)
        dy_b = lax.dot_general(dhb, wout_ref[W_MIX:, :], NT_DIMS, preferred_element_type=F32)

        a_gate = proj_ref[0, rs, W_MIX:2 * W_MIX]
        sig_a = jax.nn.sigmoid(a_gate)
        silu_a = a_gate * sig_a
        z = z_ref[0, rs, :]
        out_a = z * pools_ref[...]
        y_s[rs, 0:W_MIX] = (out_a * silu_a).astype(BF16)
        d_out_a = dy_a * silu_a
        d_ag = dy_a * out_a * (sig_a * (1.0 + a_gate * (1.0 - sig_a)))
        dproj_ref[0, rs, W_MIX:2 * W_MIX] = d_ag.astype(BF16)
        gvec_ref[V_POOL_SCALE] += _colsum8(d_out_a * z)
        dz = d_out_a * pools_ref[...]
        gvec_ref[V_POOL_B] += _colsum8(dz)
        t = _row_index((sub, POOL_GC), t0 + r0)
        n_ext = sub + POOL_HALO
        for g, w in enumerate(POOL_WINDOWS):
            lanes = slice(g * POOL_GC, (g + 1) * POOL_GC)
            dzg = dz[:, lanes].astype(BF16)
            gpoolw_ref[g] += lax.dot_general(dpool_ref[0, rs, lanes], dzg, TN_DIMS, preferred_element_type=F32)
            dd = lax.dot_general(dzg, poolw_ref[g].astype(BF16), NT_DIMS, preferred_element_type=F32)
            vext[rs, lanes] = dd * (1.0 / jnp.minimum(t + 1, w).astype(F32))
            s = vext[r0:r0 + n_ext, lanes]
            sh = 1
            while sh < w:
                s = s + pltpu.roll(s, n_ext - sh, 0)
                sh *= 2
            dproj_ref[0, rs, lanes] = (s[0:sub] - dd).astype(BF16)

        b_gate = proj_ref[0, rs, 4 * W_MIX:5 * W_MIX]
        sig_b = jax.nn.sigmoid(b_gate)
        silu_b = b_gate * sig_b
        ob = ob_ref[0, rs, :]
        y_s[rs, W_MIX:] = (ob * silu_b).astype(BF16)
        d_ob = dy_b * silu_b
        dproj_ref[0, rs, 4 * W_MIX:5 * W_MIX] = (dy_b * ob * (sig_b * (1.0 + b_gate * (1.0 - sig_b)))).astype(BF16)
        gvec_ref[V_PW_B] += _colsum8(d_ob)
        gwout_ref[...] += lax.dot_general(y_s[rs, :], dhb, TN_DIMS, preferred_element_type=F32)

        h1 = h1_ref[0, rs, :]
        mu = jnp.mean(h1, axis=-1, keepdims=True)
        xc = h1 - mu
        rstd = lax.rsqrt(jnp.mean(xc * xc, axis=-1, keepdims=True) + LN_EPS)
        xhat = xc * rstd
        h2 = xhat * lg_ref[...] + lb_ref[...]
        sig2 = jax.nn.sigmoid(h2)
        d_obb = d_ob.astype(BF16)
        gpww_ref[...] += lax.dot_general((h2 * sig2).astype(BF16), d_obb, TN_DIMS, preferred_element_type=F32)
        dh2 = lax.dot_general(d_obb, pww_ref[...], NT_DIMS, preferred_element_type=F32) * (sig2 * (1.0 + h2 * (1.0 - sig2)))
        gvec_ref[V_LN_G] += _colsum8(dh2 * xhat)
        gvec_ref[V_LN_B] += _colsum8(dh2)
        dxh = dh2 * lg_ref[...]
        dh1 = rstd * (dxh - jnp.mean(dxh, axis=-1, keepdims=True) - xhat * jnp.mean(dxh * xhat, axis=-1, keepdims=True))
        gvec_ref[V_CONV_B] += _colsum8(dh1)
        gext[rs, :] = dh1

    taps = _bwd_taps()
    cr = BWD_CONV_ROWS
    win = cr + CONV_HALO - SUBLANES

    def conv_phase(r0):
        for c0 in range(r0, r0 + sub, cr):
            for l0 in range(0, W_MIX, LANES):
                lanes = slice(l0, l0 + LANES)
                b_val = proj_ref[0, c0:c0 + cr, 2 * W_MIX + l0:2 * W_MIX + l0 + LANES]
                sg = jax.nn.sigmoid(proj_ref[0, c0:c0 + cr, 3 * W_MIX + l0:3 * W_MIX + l0 + LANES])
                h0c = b_val * sg
                acc = jnp.zeros((cr, LANES), F32)
                for s in range(SUBLANES):
                    gsh[...] = gext[c0 + s:c0 + s + win, lanes]
                    for q, k in taps[s]:
                        gq = gsh[SUBLANES * q:SUBLANES * q + cr, :]
                        acc = acc + gq * dw_ref[k:k + 1, lanes]
                        gdw_ref[k, :, lanes] += _colsum8(h0c * gq)
                d_bval = acc * sg
                dproj_ref[0, c0:c0 + cr, 2 * W_MIX + l0:2 * W_MIX + l0 + LANES] = d_bval.astype(BF16)
                dproj_ref[0, c0:c0 + cr, 3 * W_MIX + l0:3 * W_MIX + l0 + LANES] = (d_bval * b_val * (1.0 - sg)).astype(BF16)

    starts = [k * sub for k in reversed(range(BWD_SUBTILES))]
    for r0 in starts:
        dense_phase(r0)
    for r0 in starts:
        conv_phase(r0)
    vext[tm:, :] = vext[0:POOL_HALO, :]
    gext[tm:, :] = gext[0:CONV_HALO, :]


def _bwd1(dh, proj, h1, z, ob, dpool, w_out_b, pw_w_b, pool_w, pool_scale, conv_dw_f, ln_g, ln_b):
    bl, s, _ = dh.shape
    tm = TILE_M
    nt = s // tm
    tok = lambda n: pl.BlockSpec((1, tm, n), lambda b, i: (b, nt - 1 - i, 0))
    res = lambda shape: pl.BlockSpec(shape, lambda b, i: (0,) * len(shape))
    out_shapes = [(D_MODEL, D_MODEL), (W_MIX, W_MIX), (len(POOL_WINDOWS), POOL_GC, POOL_GC), (6, SUBLANES, W_MIX),
                  (CONV_WIDTH, SUBLANES, W_MIX)]
    return pl.pallas_call(
        functools.partial(_bwd1_body, nt),
        name="bwd1",
        grid=(bl, nt),
        in_specs=[tok(D_MODEL), tok(D_IN), tok(W_MIX), tok(W_MIX), tok(W_MIX), tok(W_MIX)] + [VMEM] * 7,
        out_specs=[tok(D_IN)] + [res(sh) for sh in out_shapes],
        out_shape=[jax.ShapeDtypeStruct((bl, s, D_IN), BF16)] + [jax.ShapeDtypeStruct(sh, F32) for sh in out_shapes],
        scratch_shapes=[pltpu.VMEM((tm + POOL_HALO, W_MIX), F32), pltpu.VMEM((tm + CONV_HALO, W_MIX), F32),
                        pltpu.VMEM((BWD_CONV_ROWS + CONV_HALO - SUBLANES, LANES), F32), pltpu.VMEM((tm, D_MODEL), BF16)],
        compiler_params=pltpu.CompilerParams(dimension_semantics=("arbitrary", "arbitrary"), vmem_limit_bytes=VMEM_LIMIT),
    )(dh, proj, h1, z, ob, dpool, w_out_b, pw_w_b, pool_w, _row(pool_scale), conv_dw_f, _row(ln_g), _row(ln_b))


MESH = pl.DeviceIdType.MESH
W_IN_BLK = D_IN // N_DEV
W_OUT_BLK = D_MODEL // N_DEV
PW_BLK = W_MIX // N_DEV
DW_BLK = W_MIX // N_DEV
N_PEER = N_DEV - 1


SLAB = 2 * W_IN_BLK
N_SLABS = 4
F_SIB, F_X, F_Y, F_ON, F_PASS_X, F_PASS_Y, F_PASS_D = range(7)
W_IN_ONLY, OTHERS = (0,), (1, 2, 3)


def _inproj_body(n_tiles, x_ref, ng_ref, wint_ref, wout_ref, pww_ref, dw_ref,
                 proj_hbm, wt_hbm, woutf_hbm, pwwf_hbm, dwf_ref,
                 hn_s, wt_s, wout_s, pww_s, dw_blk, stage, out_sems, w_sems, send_sems, recv_sems):
    s = pl.program_id(0)
    i = pl.program_id(1)
    tm = TILE_IN
    x, y, c = lax.axis_index("x"), lax.axis_index("y"), lax.axis_index("c")
    sibling = (x, y, 1 - c)
    xn, yn, dg = (1 - x, y), (x, 1 - y), (1 - x, 1 - y)
    pick = lambda a, b: tuple(jnp.where(c == 1, p, q) for p, q in zip(a, b))
    on_from, on_to = pick(xn, yn), pick(yn, xn)

    def blocks(px, py, pc):
        k = 4 * px + 2 * py + pc
        rows = lambda ref, n: ref.at[pl.ds(pl.multiple_of(k * n, n), n), :]
        return (rows(wt_s, W_IN_BLK), rows(wout_s, W_OUT_BLK), rows(pww_s, PW_BLK), dw_blk.at[k])

    def copies(arrays, flow, block, to):
        refs = blocks(*block)
        return [pltpu.make_async_remote_copy(src_ref=refs[a], dst_ref=refs[a], send_sem=send_sems.at[a, flow],
                                             recv_sem=recv_sems.at[a, flow], device_id=to, device_id_type=MESH)
                for a in arrays]

    def start(arrays):
        for flow, to in ((F_SIB, sibling), (F_X, (*xn, c)), (F_Y, (*yn, c))):
            for cp in copies(arrays, flow, (x, y, c), to):
                cp.start()

    def direct(arrays):
        for flow, chip in ((F_X, xn), (F_Y, yn)):
            for cp in copies(arrays, flow, (*chip, c), sibling):
                cp.wait_recv()
        for cp in copies(arrays, F_ON, (*on_from, c), (*on_to, c)):
            cp.start()
        for flow, chip in ((F_PASS_X, xn), (F_PASS_Y, yn)):
            for cp in copies(arrays, flow, (*chip, c), sibling):
                cp.start()

    def diagonal(arrays):
        for cp in copies(arrays, F_ON, (*dg, c), sibling):
            cp.wait_recv()
        for cp in copies(arrays, F_PASS_D, (*dg, c), sibling):
            cp.start()

    def from_sibling(arrays, flow, chip):
        for cp in copies(arrays, flow, (*chip, 1 - c), sibling):
            cp.wait_recv()

    def finish(arrays):
        mine = (x, y, c)
        for flow, block in ((F_SIB, mine), (F_X, mine), (F_Y, mine), (F_ON, (*on_from, c)), (F_PASS_X, (*xn, c)),
                            (F_PASS_Y, (*yn, c)), (F_PASS_D, (*dg, c))):
            for cp in copies(arrays, flow, block, sibling):
                cp.wait_send()

    step = s * n_tiles + i
    par = step % 2
    first = step == 0

    @pl.when(first)
    def _():
        mine = blocks(x, y, c)
        mine[0][...] = wint_ref[...].astype(BF16)
        mine[1][...] = wout_ref[...].astype(BF16)
        mine[2][...] = pww_ref[...].astype(BF16)
        mine[3][...] = dw_ref[...]
        start(W_IN_ONLY)
        start(OTHERS)

    rows_i = pl.ds(pl.multiple_of(i * tm, tm), tm)

    @pl.when(s == 0)
    def _():
        xt = x_ref[0]
        r1 = lax.rsqrt(jnp.mean(xt * xt, axis=-1, keepdims=True) + RMS_EPS)
        hn_s[rows_i, :] = (xt * r1 * ng_ref[...]).astype(BF16)

    @pl.when(first)
    def _():
        from_sibling(W_IN_ONLY, F_SIB, (x, y))

    @pl.when((s == 1) & (i == 0))
    def _():
        direct(W_IN_ONLY)
        from_sibling(W_IN_ONLY, F_PASS_Y, yn)

    @pl.when((s == 2) & (i == 0))
    def _():
        from_sibling(W_IN_ONLY, F_PASS_X, xn)

    weights_out = [pltpu.make_async_copy(src, dst, w_sems.at[k])
                   for k, (src, dst) in enumerate(((wt_s, wt_hbm), (wout_s, woutf_hbm), (pww_s, pwwf_hbm)))]

    @pl.when((s == 3) & (i == 0))
    def _():
        diagonal(W_IN_ONLY)
        direct(OTHERS)
        from_sibling(W_IN_ONLY, F_PASS_D, dg)
        weights_out[0].start()

    tiles_per_seq = proj_hbm.shape[1] // tm

    def out_copy(parity, col0):
        dst = proj_hbm.at[i // tiles_per_seq, pl.ds(pl.multiple_of((i % tiles_per_seq) * tm, tm), tm),
                          pl.ds(pl.multiple_of(col0, LANES), SLAB)]
        return pltpu.make_async_copy(stage.at[parity], dst, out_sems.at[parity])

    @pl.when(step >= 2)
    def _():
        out_copy(par, 0).wait()

    chip = jnp.where(s < 2, 2 * x, 2 * (1 - x)) + jnp.where((s == 0) | (s == 2), y, 1 - y)
    slab = wt_s[pl.ds(pl.multiple_of(chip * SLAB, SLAB), SLAB), :]
    stage[par] = lax.dot_general(hn_s[rows_i, :], slab, NT_DIMS, preferred_element_type=F32)
    out_copy(par, chip * SLAB).start()

    @pl.when(step == N_SLABS * n_tiles - 1)
    def _():
        out_copy(par, 0).wait()
        out_copy(1 - par, 0).wait()
        from_sibling(OTHERS, F_SIB, (x, y))
        from_sibling(OTHERS, F_PASS_X, xn)
        from_sibling(OTHERS, F_PASS_Y, yn)
        diagonal(OTHERS)
        from_sibling(OTHERS, F_PASS_D, dg)
        for cp in weights_out[1:]:
            cp.start()
        finish(W_IN_ONLY)
        finish(OTHERS)
        for cp in weights_out:
            cp.wait()
        for k in range(N_DEV):
            dwf_ref[:, k * DW_BLK:(k + 1) * DW_BLK] = dw_blk[k]


def _inproj(x, norm_g, w_in_t, w_out, pw_w, conv_dw):
    bl, s, _ = x.shape
    tm = TILE_IN
    nt = s // tm
    n_tiles = bl * nt

    def x_map(phase, i):
        i = jnp.where(phase == 0, i, n_tiles - 1)
        return (i // nt, i % nt, 0)

    dma = pltpu.SemaphoreType.DMA
    return pl.pallas_call(
        functools.partial(_inproj_body, n_tiles),
        name="inproj",
        grid=(N_SLABS, n_tiles),
        in_specs=[pl.BlockSpec((1, tm, D_MODEL), x_map)] + [VMEM] * 5,
        out_specs=[pl.BlockSpec(memory_space=pl.ANY)] * 4 + [VMEM],
        out_shape=[jax.ShapeDtypeStruct((bl, s, D_IN), F32), jax.ShapeDtypeStruct((D_IN, D_MODEL), BF16),
                   jax.ShapeDtypeStruct((D_MODEL, D_MODEL), BF16), jax.ShapeDtypeStruct((W_MIX, W_MIX), BF16),
                   jax.ShapeDtypeStruct((CONV_WIDTH, W_MIX), F32)],
        scratch_shapes=[pltpu.VMEM((n_tiles * tm, D_MODEL), BF16), pltpu.VMEM((D_IN, D_MODEL), BF16),
                        pltpu.VMEM((D_MODEL, D_MODEL), BF16), pltpu.VMEM((W_MIX, W_MIX), BF16),
                        pltpu.VMEM((N_DEV, CONV_WIDTH, DW_BLK), F32), pltpu.VMEM((2, tm, SLAB), F32),
                        dma((2,)), dma((3,)), dma((4, 7)), dma((4, 7))],
        compiler_params=pltpu.CompilerParams(dimension_semantics=("arbitrary", "arbitrary"), vmem_limit_bytes=VMEM_LIMIT),
    )(x, _row(norm_g), w_in_t, w_out, pw_w, conv_dw)


B_PLAIN, B_ON, B_MERGED = range(3)


def _routes():
    x, y, c = lax.axis_index("x"), lax.axis_index("y"), lax.axis_index("c")
    xn, yn = (1 - x, y), (x, 1 - y)
    pick = lambda a, b: tuple(jnp.where(c == 1, p, q) for p, q in zip(a, b))
    return pick(xn, yn), pick(yn, xn), (1 - x, 1 - y), (x, y), c


def _chip_index(chip):
    return 2 * chip[0] + chip[1]


def _copy(src, dst, send_sems, recv_sems, a, k, to):
    return pltpu.make_async_remote_copy(src_ref=src, dst_ref=dst, send_sem=send_sems.at[a, k], recv_sem=recv_sems.at[a, k],
                                        device_id=to, device_id_type=MESH)


def _rsb_send(plain, on, rb_plain, rb_on, send_b, recv_b):
    first, _, _, _, c = _routes()
    for k, srcs, dsts in ((B_ON, on, rb_on), (B_PLAIN, plain, rb_plain)):
        for a, (s, d) in enumerate(zip(srcs, dsts)):
            _copy(s, d, send_b, recv_b, a, k, (*first, c)).start()


def _rsb_merge(second_sums, merged, rb_on, rb_merged, send_b, recv_b):
    _, second, _, _, c = _routes()
    for a, (own, m, got, dst) in enumerate(zip(second_sums, merged, rb_on, rb_merged)):
        _copy(got, got, send_b, recv_b, a, B_ON, (*second, c)).wait_recv()
        m[...] = (own[...] + got[...].astype(F32)).astype(BF16)
        _copy(m, dst, send_b, recv_b, a, B_MERGED, (*second, c)).start()


def _rsb_total(own_sums, plain, on, merged, rb_plain, rb_merged, send_b, recv_b):
    _, second, _, _, c = _routes()
    out = []
    for a, (own, p, g) in enumerate(zip(own_sums, rb_plain, rb_merged)):
        _copy(p, p, send_b, recv_b, a, B_PLAIN, (*second, c)).wait_recv()
        _copy(g, g, send_b, recv_b, a, B_MERGED, (*second, c)).wait_recv()
        out.append(own[...] + p[...].astype(F32) + g[...].astype(F32))
    for k, srcs in ((B_PLAIN, plain), (B_ON, on), (B_MERGED, merged)):
        for a, s in enumerate(srcs):
            _copy(s, s, send_b, recv_b, a, k, (*second, c)).wait_send()
    return out


R_LOSS, R_FINAL_G, R_POOL, R_CONV, R_LN = range(5)
VEC_ROWS = 16
DW_ROWS = 32
A_PART, A_X, A_Y, A_ON, A_HALF = range(5)
HALF_DTYPES = (F32, BF16, BF16)
STEP_SUMS, STEP_MERGE, STEP_HALF = 1, 3, 5


def _bwd2a_body(n_steps, x_ref, dproj_ref, ng_ref, gwout_ref, gpww_ref, loss_ref, gfg_ref, gvec_ref, gdw_ref, gpoolw_ref,
                win_own_ref, win_staged_ref, owout_ref, opww_ref, ovec_ref, odw_ref, opoolw_ref,
                acc, st_win, a_win, a_wout, a_pww, st_wout, st_pww, rb_wout, rb_pww, part_vec, part_dw, part_pw,
                sib_vec, sib_dw, sib_pw, res_vec, res_dw, res_pw, half_vec, half_dw, half_pw,
                send_a, recv_a, send_b, recv_b, send_r, recv_r, send_w, recv_w):
    step = pl.program_id(0) * pl.num_programs(1) + pl.program_id(1)
    x, y, c = lax.axis_index("x"), lax.axis_index("y"), lax.axis_index("c")
    sibling = (x, y, 1 - c)
    xn, yn = (1 - x, y), (x, 1 - y)
    first, second, diag, own, _ = _routes()
    rows = lambda ref, k, n: ref.at[pl.ds(pl.multiple_of(k * n, n), n), :]

    blocks = lambda k: (rows(gwout_ref, k, W_OUT_BLK), rows(gpww_ref, k, PW_BLK))
    landed = lambda chip: (a_wout.at[chip], a_pww.at[chip])
    staged = lambda k: (st_wout.at[k], st_pww.at[k])
    arrived = lambda k: (rb_wout.at[k], rb_pww.at[k])

    parts, sibs, halves = (part_vec, part_dw, part_pw), (sib_vec, sib_dw, sib_pw), (half_vec, half_dw, half_pw)
    results = (res_vec, res_dw, res_pw)
    half_rows = lambda ref: pl.ds(pl.multiple_of(c * (ref.shape[0] // 2), SUBLANES), ref.shape[0] // 2)
    slot = lambda chip: tuple(h.at[_chip_index(chip)] for h in halves)

    @pl.when(step == 0)
    def _():
        acc[...] = jnp.zeros_like(acc)
        sum8 = lambda v: jnp.sum(v, axis=0, keepdims=True)
        part_vec[...] = jnp.zeros((VEC_ROWS, D_MODEL), F32)
        part_vec[R_LOSS:R_LOSS + 1, :] = jnp.broadcast_to(loss_ref[0:1, 0:1], (1, D_MODEL))
        part_vec[R_FINAL_G:R_FINAL_G + 1, :] = sum8(gfg_ref[...])
        for row, (lo, hi) in ((R_POOL, (V_POOL_B, V_POOL_SCALE)), (R_CONV, (V_CONV_B, V_LN_G)), (R_LN, (V_LN_B, V_PW_B))):
            part_vec[row:row + 1, 0:W_MIX] = sum8(gvec_ref[lo])
            part_vec[row:row + 1, W_MIX:] = sum8(gvec_ref[hi])
        part_dw[0:CONV_WIDTH, :] = jnp.sum(gdw_ref[...], axis=1)
        part_dw[CONV_WIDTH:, :] = jnp.zeros((DW_ROWS - CONV_WIDTH, W_MIX), F32)
        part_pw[...] = gpoolw_ref[...].reshape(len(POOL_WINDOWS) * POOL_GC, POOL_GC)
        for a, (p, s) in enumerate(zip(parts, sibs)):
            _copy(p, s, send_r, recv_r, a, A_PART, sibling).start()
        for chip in range(4):
            for a, (s, d) in enumerate(zip(blocks(2 * chip + (1 - c)), landed(chip))):
                _copy(s, d, send_a, recv_a, a, chip, sibling).start()

    x_t = x_ref[0]
    r1 = lax.rsqrt(jnp.mean(x_t * x_t, axis=-1, keepdims=True) + RMS_EPS)
    hn = (x_t * r1 * ng_ref[...]).astype(BF16)
    acc[...] += lax.dot_general(dproj_ref[0], hn, TN_DIMS, preferred_element_type=F32)

    @pl.when(step == STEP_SUMS)
    def _():
        for a, (p, s, h) in enumerate(zip(parts, sibs, slot(own))):
            _copy(p, s, send_r, recv_r, a, A_PART, sibling).wait_recv()
            h[...] = (p[half_rows(p), :] + s[half_rows(s), :]).astype(h.dtype)
            for k, to in ((A_X, xn), (A_Y, yn)):
                _copy(h, h, send_r, recv_r, a, k, (*to, c)).start()
        for chip in range(4):
            for a, (m, l) in enumerate(zip(blocks(2 * chip + c), landed(chip))):
                _copy(l, l, send_a, recv_a, a, chip, sibling).wait_recv()
                l[...] = m[...] + l[...]
        for k, chip in ((B_ON, diag), (B_PLAIN, first)):
            for s, l in zip(staged(k), landed(_chip_index(chip))):
                s[...] = l[...].astype(BF16)
        _rsb_send(staged(B_PLAIN), staged(B_ON), arrived(B_PLAIN), arrived(B_ON), send_b, recv_b)

    @pl.when(step == STEP_MERGE)
    def _():
        _rsb_merge(landed(_chip_index(second)), staged(B_MERGED), arrived(B_ON), arrived(B_MERGED), send_b, recv_b)
        for a in range(3):
            for k, chip in ((A_X, xn), (A_Y, yn)):
                h = slot(chip)[a]
                _copy(h, h, send_r, recv_r, a, k, sibling).wait_recv()
            h = slot(first)[a]
            _copy(h, h, send_r, recv_r, a, A_ON, (*second, c)).start()

    @pl.when(step == STEP_HALF)
    def _():
        for a, (r, hs) in enumerate(zip(results, halves)):
            h = slot(diag)[a]
            _copy(h, h, send_r, recv_r, a, A_ON, sibling).wait_recv()
            mine = r.at[half_rows(r), :]
            chip_sums = [hs[k].astype(F32) for k in range(4)]
            mine[...] = (chip_sums[0] + chip_sums[1]) + (chip_sums[2] + chip_sums[3])
            _copy(mine, mine, send_r, recv_r, a, A_HALF, sibling).start()

    @pl.when(step == n_steps - 1)
    def _():
        block = lambda k: rows(acc, k, W_IN_BLK)
        w_copy = lambda chip: _copy(st_win.at[chip], a_win.at[chip], send_w, recv_w, 0, chip, sibling)
        for chip in range(4):
            st_win[chip] = block(2 * chip + (1 - c))[...].astype(BF16)
            w_copy(chip).start()

        owout_ref[...], opww_ref[...] = _rsb_total(landed(_chip_index(own)), staged(B_PLAIN), staged(B_ON), staged(B_MERGED),
                                                   arrived(B_PLAIN), arrived(B_MERGED), send_b, recv_b)
        for a, (r, p, s) in enumerate(zip(results, parts, sibs)):
            theirs = r.at[pl.ds(pl.multiple_of((1 - c) * (r.shape[0] // 2), SUBLANES), r.shape[0] // 2), :]
            _copy(theirs, theirs, send_r, recv_r, a, A_HALF, sibling).wait_recv()
            _copy(p, s, send_r, recv_r, a, A_PART, sibling).wait_send()
            for k, h in ((A_X, slot(own)[a]), (A_Y, slot(own)[a]), (A_ON, slot(first)[a])):
                _copy(h, h, send_r, recv_r, a, k, sibling).wait_send()
            mine = r.at[half_rows(r), :]
            _copy(mine, mine, send_r, recv_r, a, A_HALF, sibling).wait_send()
        for chip in range(4):
            for a, (s, d) in enumerate(zip(blocks(2 * chip + (1 - c)), landed(chip))):
                _copy(s, d, send_a, recv_a, a, chip, sibling).wait_send()
        ovec_ref[...] = res_vec[...]
        opoolw_ref[...] = res_pw[...]
        for k in range(N_DEV):
            @pl.when(4 * x + 2 * y + c == k)
            def _():
                odw_ref[...] = res_dw[0:CONV_WIDTH, k * DW_BLK:(k + 1) * DW_BLK]

        for chip in range(4):
            w_copy(chip).wait()
        chip_sum = lambda chip: block(2 * _chip_index(chip) + c)[...] + a_win[_chip_index(chip)].astype(F32)
        win_own_ref[...] = chip_sum(own)
        for j, chip in enumerate((xn, yn, diag)):
            win_staged_ref[j] = chip_sum(chip).astype(BF16)


def _bwd2a(x, dproj, norm_g, gwout, gpww, loss8, gfg8, gvec, gdw8, gpoolw):
    bl, s, _ = x.shape
    tm = TILE_M
    nt = s // tm
    tok = lambda n: pl.BlockSpec((1, tm, n), lambda b, i: (b, i, 0))
    n_pw = len(POOL_WINDOWS) * POOL_GC
    win = (W_IN_BLK, D_MODEL)
    blk = ((W_OUT_BLK, D_MODEL), (PW_BLK, W_MIX))
    rep = ((VEC_ROWS, D_MODEL), (DW_ROWS, W_MIX), (n_pw, POOL_GC))
    half = lambda sh: (sh[0] // 2,) + sh[1:]
    dma = pltpu.SemaphoreType.DMA
    assert bl * nt > STEP_HALF + 1, "the exchanges' phases need their grid steps"
    return pl.pallas_call(
        functools.partial(_bwd2a_body, bl * nt),
        name="bwd2a",
        grid=(bl, nt),
        in_specs=[tok(D_MODEL), tok(D_IN)] + [VMEM] * 8,
        out_specs=[VMEM] * 7,
        out_shape=[jax.ShapeDtypeStruct(win, F32), jax.ShapeDtypeStruct((3,) + win, BF16)]
        + [jax.ShapeDtypeStruct(sh, F32) for sh in blk + (rep[0], (CONV_WIDTH, DW_BLK), rep[2])],
        scratch_shapes=[pltpu.VMEM((D_IN, D_MODEL), F32), pltpu.VMEM((4,) + win, BF16), pltpu.VMEM((4,) + win, BF16)]
        + [pltpu.VMEM((4,) + sh, F32) for sh in blk]
        + [pltpu.VMEM((3,) + sh, BF16) for sh in blk] * 2
        + [pltpu.VMEM(sh, F32) for sh in rep] * 3
        + [pltpu.VMEM((4,) + half(sh), dt) for sh, dt in zip(rep, HALF_DTYPES)]
        + [dma((2, 4)), dma((2, 4)), dma((2, 3)), dma((2, 3)), dma((3, 5)), dma((3, 5)), dma((1, 4)), dma((1, 4))],
        compiler_params=pltpu.CompilerParams(dimension_semantics=("arbitrary", "arbitrary"), vmem_limit_bytes=VMEM_LIMIT),
    )(x, dproj, _row(norm_g), gwout, gpww, loss8, gfg8, gvec, gdw8, gpoolw)


HBM = pl.BlockSpec(memory_space=pltpu.HBM)
SEM = pl.BlockSpec(memory_space=pltpu.SEMAPHORE)
EFFECT = pltpu.SideEffectType.DATAFLOW_SIDE_EFFECTING


def _owner_copies(src_ref, land_ref, send_sems, recv_sems):
    x, y, c = lax.axis_index("x"), lax.axis_index("y"), lax.axis_index("c")
    owners = ((1 - x, y, c), (x, 1 - y, c), (1 - x, 1 - y, c))
    return [pltpu.make_async_remote_copy(src_ref=src_ref.at[j], dst_ref=land_ref.at[j], send_sem=send_sems.at[j],
                                         recv_sem=recv_sems.at[j], device_id=owners[j], device_id_type=MESH)
            for j in range(3)]


def _rs_start(staged):
    def body(src_ref, land_ref, send_sems, recv_sems, src_thru, land_thru, token):
        for cp in _owner_copies(src_ref, land_ref, send_sems, recv_sems):
            cp.start()
        token[...] = jnp.zeros_like(token)

    dma = pltpu.SemaphoreType.DMA
    return pl.pallas_call(
        body, name="rs_start",
        out_shape=(dma((3,)), dma((3,)), pltpu.HBM(staged.shape, staged.dtype), pltpu.HBM(staged.shape, staged.dtype),
                   jax.ShapeDtypeStruct((SUBLANES, LANES), F32)),
        in_specs=(HBM, HBM), out_specs=(SEM, SEM, HBM, HBM, VMEM), input_output_aliases={0: 2, 1: 3},
        compiler_params=pltpu.CompilerParams(has_side_effects=EFFECT),
    )(pltpu.with_memory_space_constraint(staged, pltpu.HBM),
      pltpu.with_memory_space_constraint(lax.empty(staged.shape, staged.dtype), pltpu.HBM))


def _rs_wait(send_sems, recv_sems, src_thru, land_thru, after):
    def body(src_ref, land_ref, send_sems, recv_sems, after_ref, src_dead, got_ref):
        for cp in _owner_copies(src_ref, land_ref, send_sems, recv_sems):
            cp.wait_send()
            cp.wait_recv()

    return pl.pallas_call(
        body, name="rs_wait",
        out_shape=(pltpu.HBM(src_thru.shape, src_thru.dtype), pltpu.HBM(land_thru.shape, land_thru.dtype)),
        in_specs=(HBM, HBM, SEM, SEM, pl.BlockSpec(memory_space=pl.ANY)), out_specs=(HBM, HBM),
        input_output_aliases={0: 0, 1: 1},
        compiler_params=pltpu.CompilerParams(has_side_effects=EFFECT),
    )(src_thru, land_thru, send_sems, recv_sems, after)[1]


def _bwd2b_body(x_ref, dh_ref, dproj_ref, ng_ref, wint_ref, token_ref, gx_ref, gng_ref):
    @pl.when((pl.program_id(0) == 0) & (pl.program_id(1) == 0))
    def _():
        gng_ref[...] = jnp.zeros_like(gng_ref) + token_ref[0:1, 0:1]

    x_t = x_ref[0]
    r1 = lax.rsqrt(jnp.mean(x_t * x_t, axis=-1, keepdims=True) + RMS_EPS)
    xr = x_t * r1
    dhn = jnp.dot(dproj_ref[0], wint_ref[...], preferred_element_type=F32)
    gng_ref[...] += _colsum8(dhn * xr)
    gy = dhn * ng_ref[...]
    gx_ref[0] = dh_ref[0] + r1 * (gy - xr * jnp.mean(gy * xr, axis=-1, keepdims=True))


def _bwd2b(x, dh, dproj, norm_g, w_in_t, token):
    bl, s, _ = x.shape
    tm = TILE_IN
    tok = lambda n: pl.BlockSpec((1, tm, n), lambda b, i: (b, i, 0))
    return pl.pallas_call(
        functools.partial(_bwd2b_body),
        name="bwd2b",
        grid=(bl, s // tm),
        in_specs=[tok(D_MODEL), tok(D_MODEL), tok(D_IN), VMEM, VMEM, VMEM],
        out_specs=[tok(D_MODEL), pl.BlockSpec((SUBLANES, D_MODEL), lambda b, i: (0, 0))],
        out_shape=[jax.ShapeDtypeStruct((bl, s, D_MODEL), F32), jax.ShapeDtypeStruct((SUBLANES, D_MODEL), F32)],
        compiler_params=pltpu.CompilerParams(dimension_semantics=("arbitrary", "arbitrary"), vmem_limit_bytes=VMEM_LIMIT),
    )(x, dh, dproj, _row(norm_g), w_in_t, token)


def _partial_copies(part_ref, land_ref, send_sems, recv_sems):
    x, y, c = lax.axis_index("x"), lax.axis_index("y"), lax.axis_index("c")
    other_chips = [(1 - x, y), (x, 1 - y), (1 - x, 1 - y)]
    peers = [(x, y, 1 - c)] + [(*chip, c) for chip in other_chips] + [(*chip, 1 - c) for chip in other_chips]
    return [pltpu.make_async_remote_copy(src_ref=part_ref, dst_ref=land_ref.at[4 * x + 2 * y + c], send_sem=send_sems.at[r],
                                         recv_sem=recv_sems.at[r], device_id=peers[r], device_id_type=MESH)
            for r in range(N_PEER)]


def _partials_start(part):
    def body(part_ref, land_ref, send_sems, recv_sems, part_thru, land_thru):
        for cp in _partial_copies(part_ref, land_ref, send_sems, recv_sems):
            cp.start()

    dma = pltpu.SemaphoreType.DMA
    land = (N_DEV,) + part.shape
    return pl.pallas_call(
        body, name="partials_start",
        out_shape=(dma((N_PEER,)), dma((N_PEER,)), pltpu.HBM(part.shape, part.dtype), pltpu.HBM(land, part.dtype)),
        in_specs=(HBM, HBM), out_specs=(SEM, SEM, HBM, HBM), input_output_aliases={0: 2, 1: 3},
        compiler_params=pltpu.CompilerParams(has_side_effects=EFFECT),
    )(pltpu.with_memory_space_constraint(part, pltpu.HBM),
      pltpu.with_memory_space_constraint(lax.empty(land, part.dtype), pltpu.HBM))


def _partials_wait(send_sems, recv_sems, part_thru, land_thru, after):
    def body(part_ref, land_ref, send_sems, recv_sems, after_ref, part_out, land_out):
        for cp in _partial_copies(part_ref, land_ref, send_sems, recv_sems):
            cp.wait_send()
            cp.wait_recv()

    return pl.pallas_call(
        body, name="partials_wait",
        out_shape=(pltpu.HBM(part_thru.shape, part_thru.dtype), pltpu.HBM(land_thru.shape, land_thru.dtype)),
        in_specs=(HBM, HBM, SEM, SEM, pl.BlockSpec(memory_space=pl.ANY)), out_specs=(HBM, HBM),
        input_output_aliases={0: 0, 1: 1},
        compiler_params=pltpu.CompilerParams(has_side_effects=EFFECT),
    )(part_thru, land_thru, send_sems, recv_sems, after)


ADAM_LR = 0.001
ADAM_B1 = 0.9
ADAM_B2 = 0.999
ADAM_EPS = 1e-08
ADAM_WD = 0.01
ADAM_STEP = 10


PARAMS = ("norm_g", "w_in", "pool_w", "pool_b", "pool_scale", "conv_dw", "conv_b", "ln_g", "ln_b", "pw_w", "pw_b", "w_out",
          "final_g")


LATE_PARAMS = ("norm_g",)
MAIN_PARAMS = tuple(p for p in PARAMS if p not in LATE_PARAMS)


def _adamw_refs(params, n_sources, refs):
    n = len(params)
    ws, ms, vs = (dict(zip(params, refs[k * n:(k + 1) * n])) for k in range(3))
    outs = refs[3 * n + n_sources:]
    gs, ds, m2s, v2s = (dict(zip(params, outs[k * n:(k + 1) * n])) for k in range(4))

    def update(name, g, at=slice(None)):
        m = ADAM_B1 * ms[name][at] + (1.0 - ADAM_B1) * g
        v = ADAM_B2 * vs[name][at] + (1.0 - ADAM_B2) * (g * g)
        m_hat = m / (1.0 - ADAM_B1 ** ADAM_STEP)
        v_hat = v / (1.0 - ADAM_B2 ** ADAM_STEP)
        gs[name][at] = g
        ds[name][at] = -ADAM_LR * (m_hat / (jnp.sqrt(v_hat) + ADAM_EPS) + ADAM_WD * ws[name][at])
        m2s[name][at] = m
        v2s[name][at] = v

    return update, refs[3 * n:3 * n + n_sources], outs[4 * n:]


def _adamw_late_body(*refs):
    update, (gng_ref, gng_others_ref), _ = _adamw_refs(LATE_PARAMS, 2, refs)
    me = 4 * lax.axis_index("x") + 2 * lax.axis_index("y") + lax.axis_index("c")
    g_norm = jnp.zeros(gng_ref.shape, F32)
    for k in range(N_DEV):
        g_norm = g_norm + jnp.where(me == k, gng_ref[...], gng_others_ref[k])
    update("norm_g", jnp.sum(g_norm, axis=0, keepdims=True))


def _adamw_body(*refs):
    update, sources, (loss_ref,) = _adamw_refs(MAIN_PARAMS, 7, refs)
    gwin_ref, gwin_others_ref, gwout_ref, gpww_ref, vec_ref, gdw_ref, gpoolw_ref = sources
    ws = dict(zip(MAIN_PARAMS, refs))
    row = lambda r, lo, hi: vec_ref[r:r + 1, lo:hi]
    loss_ref[...] = row(R_LOSS, 0, 1)
    g_w_in = gwin_ref[...]
    for j in range(3):
        g_w_in = g_w_in + gwin_others_ref[j].astype(F32)
    update("w_in", g_w_in)
    update("pool_w", gpoolw_ref[...].reshape(ws["pool_w"].shape))
    for g in range(len(POOL_WINDOWS)):
        update("pool_b", row(R_POOL, g * POOL_GC, (g + 1) * POOL_GC), at=slice(g, g + 1))
    update("pool_scale", row(R_POOL, W_MIX, 2 * W_MIX))
    update("conv_dw", gdw_ref[...])
    update("conv_b", row(R_CONV, 0, W_MIX))
    update("ln_g", row(R_CONV, W_MIX, 2 * W_MIX))
    update("ln_b", row(R_LN, 0, W_MIX))
    update("pw_w", gpww_ref[...])
    update("pw_b", row(R_LN, W_MIX, 2 * W_MIX))
    update("w_out", gwout_ref[...])
    update("final_g", row(R_FINAL_G, 0, D_MODEL))


def _adamw(body, name, params, ws, ms, vs, grad_sources, extra_out_shapes=()):
    n = len(params)
    flat = lambda a: a.reshape(1, -1) if a.ndim == 1 else a
    args = [flat(d[p]) for d in (ws, ms, vs) for p in params]
    outs = pl.pallas_call(
        body,
        name=name,
        in_specs=[VMEM] * (3 * n + len(grad_sources)),
        out_specs=[VMEM] * (4 * n + len(extra_out_shapes)),
        out_shape=[jax.ShapeDtypeStruct(a.shape, F32) for a in args[:n]] * 4 + list(extra_out_shapes),
        compiler_params=pltpu.CompilerParams(vmem_limit_bytes=VMEM_LIMIT),
    )(*args, *grad_sources)
    groups = tuple({p: outs[k * n + j].reshape(ws[p].shape) for j, p in enumerate(params)} for k in range(4))
    return groups, outs[4 * n:]


def kernel(x, norm_g, w_in, pool_w, pool_b, pool_scale, conv_dw, conv_b, ln_g, ln_b, pw_w, pw_b, w_out, final_g, loss_target, m_norm_g, m_w_in, m_pool_w, m_pool_b, m_pool_scale, m_conv_dw, m_conv_b, m_ln_g, m_ln_b, m_pw_w, m_pw_b, m_w_out, m_final_g, v_norm_g, v_w_in, v_pool_w, v_pool_b, v_pool_scale, v_conv_dw, v_conv_b, v_ln_g, v_ln_b, v_pw_w, v_pw_b, v_w_out, v_final_g):
    proj, w_in_t, w_out_b, pw_w_b, conv_dw_f = _inproj(x, norm_g, w_in.T, w_out, pw_w, conv_dw)
    h1, z, ob, dpool, dh, loss8, gfg8 = _fwd(x, loss_target, proj, pool_w, pool_b, pool_scale, conv_dw_f,
                                             conv_b, ln_g, ln_b, pw_w_b, pw_b, w_out_b, final_g)
    dproj, gwout, gpww, gpoolw, gvec, gdw8 = _bwd1(dh, proj, h1, z, ob, dpool, w_out_b, pw_w_b, pool_w, pool_scale,
                                                   conv_dw_f, ln_g, ln_b)
    win_own, win_staged, g_w_out, g_pw_w, vec, g_conv_dw, poolw_sum = _bwd2a(x, dproj, norm_g, gwout, gpww, loss8, gfg8, gvec,
                                                                             gdw8, gpoolw)
    send_sems, recv_sems, staged_thru, land_thru, token = _rs_start(win_staged)
    grad_x, gng8 = _bwd2b(x, dh, dproj, norm_g, w_in_t, token)
    ng_send, ng_recv, gng8, gng_land = _partials_start(gng8)
    win_others = _rs_wait(send_sems, recv_sems, staged_thru, land_thru, gng8)

    weights = dict(norm_g=norm_g, w_in=w_in.T, pool_w=pool_w, pool_b=pool_b, pool_scale=pool_scale, conv_dw=conv_dw, conv_b=conv_b,
                   ln_g=ln_g, ln_b=ln_b, pw_w=pw_w, pw_b=pw_b, w_out=w_out, final_g=final_g)
    m_in = dict(norm_g=m_norm_g, w_in=m_w_in.T, pool_w=m_pool_w, pool_b=m_pool_b, pool_scale=m_pool_scale, conv_dw=m_conv_dw,
                conv_b=m_conv_b, ln_g=m_ln_g, ln_b=m_ln_b, pw_w=m_pw_w, pw_b=m_pw_b, w_out=m_w_out, final_g=m_final_g)
    v_in = dict(norm_g=v_norm_g, w_in=v_w_in.T, pool_w=v_pool_w, pool_b=v_pool_b, pool_scale=v_pool_scale, conv_dw=v_conv_dw,
                conv_b=v_conv_b, ln_g=v_ln_g, ln_b=v_ln_b, pw_w=v_pw_w, pw_b=v_pw_b, w_out=v_w_out, final_g=v_final_g)
    main, (loss,) = _adamw(functools.partial(_adamw_body), "adamw", MAIN_PARAMS, weights, m_in, v_in,
                           (win_own, win_others, g_w_out, g_pw_w, vec, g_conv_dw, poolw_sum), [jax.ShapeDtypeStruct((1, 1), F32)])
    gng8, gng_others = _partials_wait(ng_send, ng_recv, gng8, gng_land, main[0]["w_out"])
    late, _ = _adamw(functools.partial(_adamw_late_body), "adamw_norm_g", LATE_PARAMS, weights, m_in, v_in, (gng8, gng_others))
    groups = [[(late[k] if p in LATE_PARAMS else main[k])[p] for p in PARAMS] for k in range(4)]
    k_in = PARAMS.index("w_in")
    for group in groups:
        group[k_in] = group[k_in].T
    grads, delta, new_m, new_v = groups
    return (loss.reshape(()), grad_x, *grads, *delta, *new_m, *new_v)
```

```python
import functools

import jax
import jax.numpy as jnp
from jax import lax
from jax.experimental import pallas as pl
from jax.experimental.pallas import tpu as pltpu

F32 = jnp.float32
BF16 = jnp.bfloat16

D_MODEL = 1024
W_MIX = 512
D_IN = 5 * W_MIX
POOL_WINDOWS = (2, 4, 8, 16)
POOL_GC = 128
CONV_WIDTH = 31
RMS_EPS = 1e-6
LN_EPS = 1e-5
N_DEV = 8

LANES = 128
SUBLANES = 8
TILE_M = 512
TILE_IN = 1024
POOL_HALO = 16
CONV_HALO = 32
CONV_ROWS = 128
BWD_CONV_ROWS = 64
BWD_SUBTILES = 2
VMEM_LIMIT = 56 * 1024 * 1024

VMEM = pl.BlockSpec(memory_space=pltpu.VMEM)


def _silu(v):
    return v * jax.nn.sigmoid(v)


def _colsum8(v):
    m, n = v.shape
    return jnp.sum(v.reshape(m // SUBLANES, SUBLANES, n), axis=0)


def _row_index(shape, t0):
    return lax.broadcasted_iota(jnp.int32, shape, 0) + t0


def _conv_taps():
    taps = {s: [] for s in range(SUBLANES)}
    for k in range(CONV_WIDTH):
        e = k + CONV_HALO - (CONV_WIDTH - 1)
        taps[e % SUBLANES].append((e // SUBLANES, k))
    return taps


def _fwd_body(x_ref, tgt_ref, proj_ref, poolw_ref, poolb_ref, pools_ref, dw_ref, cb_ref, lg_ref, lb_ref,
              pww_ref, pwb_ref, wout_ref, fg_ref,
              h1_ref, z_ref, ob_ref, dpool_ref, dh_ref, loss_ref, gfg_ref,
              uext, hext, y_s):
    b = pl.program_id(0)
    i = pl.program_id(1)
    tm = TILE_M

    @pl.when(i == 0)
    def _():
        uext[0:POOL_HALO, :] = jnp.zeros((POOL_HALO, W_MIX), F32)
        hext[0:CONV_HALO, :] = jnp.zeros((CONV_HALO, W_MIX), F32)
        hext[CONV_HALO + tm:, :] = jnp.zeros((SUBLANES, W_MIX), F32)

    @pl.when((b == 0) & (i == 0))
    def _():
        loss_ref[...] = jnp.zeros_like(loss_ref)
        gfg_ref[...] = jnp.zeros_like(gfg_ref)

    uext[POOL_HALO:, :] = proj_ref[0, :, 0:W_MIX]
    t = _row_index((tm, POOL_GC), i * tm)
    for g, w in enumerate(POOL_WINDOWS):
        lanes = slice(g * POOL_GC, (g + 1) * POOL_GC)
        e = uext[:, lanes]
        s = e
        sh = 1
        while sh < w:
            s = s + pltpu.roll(s, sh, 0)
            sh *= 2
        inv_cnt = 1.0 / jnp.minimum(t + 1, w).astype(F32)
        d = (s[POOL_HALO:] * inv_cnt - e[POOL_HALO:]).astype(BF16)
        dpool_ref[0, :, lanes] = d
        z = jnp.dot(d, poolw_ref[g].astype(BF16), preferred_element_type=F32) + poolb_ref[:, lanes]
        z_ref[0, :, lanes] = z
        a_gate = proj_ref[0, :, W_MIX + g * POOL_GC:W_MIX + (g + 1) * POOL_GC]
        y_s[:, lanes] = (z * pools_ref[:, lanes] * _silu(a_gate)).astype(BF16)

    hext[CONV_HALO:CONV_HALO + tm, :] = proj_ref[0, :, 2 * W_MIX:3 * W_MIX] * jax.nn.sigmoid(proj_ref[0, :, 3 * W_MIX:4 * W_MIX])
    taps = _conv_taps()
    for c0 in range(0, tm, CONV_ROWS):
        for l0 in range(0, W_MIX, LANES):
            lanes = slice(l0, l0 + LANES)
            acc = jnp.zeros((CONV_ROWS, LANES), F32) + cb_ref[:, lanes]
            for s in range(SUBLANES):
                part = jnp.zeros((CONV_ROWS + SUBLANES, LANES), F32)
                for q, k in taps[s]:
                    r0 = c0 + SUBLANES * q
                    part = part + hext[r0:r0 + CONV_ROWS + SUBLANES, lanes] * dw_ref[k:k + 1, lanes]
                acc = acc + part[s:s + CONV_ROWS]
            h1_ref[0, c0:c0 + CONV_ROWS, lanes] = acc
    h1 = h1_ref[0]
    mu = jnp.mean(h1, axis=-1, keepdims=True)
    xc = h1 - mu
    rstd = lax.rsqrt(jnp.mean(xc * xc, axis=-1, keepdims=True) + LN_EPS)
    h3 = _silu(xc * rstd * lg_ref[...] + lb_ref[...])
    ob = jnp.dot(h3.astype(BF16), pww_ref[...], preferred_element_type=F32) + pwb_ref[...]
    ob_ref[0] = ob
    y_s[:, W_MIX:] = (ob * _silu(proj_ref[0, :, 4 * W_MIX:5 * W_MIX])).astype(BF16)

    h = x_ref[0] + jnp.dot(y_s[...], wout_ref[...], preferred_element_type=F32)
    r2 = lax.rsqrt(jnp.mean(h * h, axis=-1, keepdims=True) + RMS_EPS)
    hr = h * r2
    err = hr * fg_ref[...] - tgt_ref[0]
    loss_ref[...] += jnp.sum(err * err) * (0.5 / D_MODEL)
    dout = err * (1.0 / D_MODEL)
    gfg_ref[...] += _colsum8(dout * hr)
    gy = dout * fg_ref[...]
    dh_ref[0] = r2 * (gy - hr * jnp.mean(gy * hr, axis=-1, keepdims=True))

    uext[0:POOL_HALO, :] = uext[tm:tm + POOL_HALO, :]
    hext[0:CONV_HALO, :] = hext[tm:tm + CONV_HALO, :]


def _row(v):
    return v.reshape(1, -1)


def _fwd(x, tgt, proj, pool_w, pool_b, pool_scale, conv_dw_f, conv_b, ln_g, ln_b, pw_w_b, pw_b, w_out_b, final_g):
    bl, s, _ = x.shape
    tm = TILE_M
    tok = lambda n: pl.BlockSpec((1, tm, n), lambda b, i: (b, i, 0))
    acc = lambda n: pl.BlockSpec((SUBLANES, n), lambda b, i: (0, 0))
    act = lambda n: jax.ShapeDtypeStruct((bl, s, n), F32)
    return pl.pallas_call(
        functools.partial(_fwd_body),
        name="fwd",
        grid=(bl, s // tm),
        in_specs=[tok(D_MODEL), tok(D_MODEL), tok(D_IN)] + [VMEM] * 11,
        out_specs=[tok(W_MIX), tok(W_MIX), tok(W_MIX), tok(W_MIX), tok(D_MODEL), acc(LANES), acc(D_MODEL)],
        out_shape=[act(W_MIX), act(W_MIX), act(W_MIX), jax.ShapeDtypeStruct((bl, s, W_MIX), BF16), act(D_MODEL),
                   jax.ShapeDtypeStruct((SUBLANES, LANES), F32), jax.ShapeDtypeStruct((SUBLANES, D_MODEL), F32)],
        scratch_shapes=[pltpu.VMEM((POOL_HALO + tm, W_MIX), F32), pltpu.VMEM((CONV_HALO + tm + SUBLANES, W_MIX), F32),
                        pltpu.VMEM((tm, D_MODEL), BF16)],
        compiler_params=pltpu.CompilerParams(dimension_semantics=("arbitrary", "arbitrary"), vmem_limit_bytes=VMEM_LIMIT),
    )(x, tgt, proj, pool_w, pool_b.reshape(1, W_MIX), _row(pool_scale), conv_dw_f, _row(conv_b),
      _row(ln_g), _row(ln_b), pw_w_b, _row(pw_b), w_out_b, _row(final_g))


NT_DIMS = (((1,), (1,)), ((), ()))
TN_DIMS = (((0,), (0,)), ((), ()))
V_POOL_B, V_POOL_SCALE, V_CONV_B, V_LN_G, V_LN_B, V_PW_B = range(6)


def _bwd_taps():
    taps = {s: [] for s in range(SUBLANES)}
    for k in range(CONV_WIDTH):
        e = CONV_WIDTH - 1 - k
        taps[e % SUBLANES].append((e // SUBLANES, k))
    return taps


def _bwd1_body(n_tiles, dh_ref, proj_ref, h1_ref, z_ref, ob_ref, dpool_ref, wout_ref, pww_ref, poolw_ref, pools_ref, dw_ref,
               lg_ref, lb_ref,
               dproj_ref, gwout_ref, gpww_ref, gpoolw_ref, gvec_ref, gdw_ref,
               vext, gext, gsh, y_s):
    b = pl.program_id(0)
    i = pl.program_id(1)
    tm = TILE_M
    t0 = (n_tiles - 1 - i) * tm

    @pl.when(i == 0)
    def _():
        vext[tm:, :] = jnp.zeros((POOL_HALO, W_MIX), F32)
        gext[tm:, :] = jnp.zeros((CONV_HALO, W_MIX), F32)

    @pl.when((b == 0) & (i == 0))
    def _():
        gwout_ref[...] = jnp.zeros_like(gwout_ref)
        gpww_ref[...] = jnp.zeros_like(gpww_ref)
        gpoolw_ref[...] = jnp.zeros_like(gpoolw_ref)
        gvec_ref[...] = jnp.zeros_like(gvec_ref)
        gdw_ref[...] = jnp.zeros_like(gdw_ref)

    sub = tm // BWD_SUBTILES

    def dense_phase(r0):
        rs = slice(r0, r0 + sub)
        dhb = dh_ref[0, rs, :].astype(BF16)
        dy_a = lax.dot_general(dhb, wout_ref[0:W_MIX, :], NT_DIMS, preferred_element_type=F32)
        dy_b = lax.dot_general(dhb, wout_ref[W_MIX:, :], NT_DIMS, preferred_element_type=F32)

        a_gate = proj_ref[0, rs, W_MIX:2 * W_MIX]
        sig_a = jax.nn.sigmoid(a_gate)
        silu_a = a_gate * sig_a
        z = z_ref[0, rs, :]
        out_a = z * pools_ref[...]
        y_s[rs, 0:W_MIX] = (out_a * silu_a).astype(BF16)
        d_out_a = dy_a * silu_a
        d_ag = dy_a * out_a * (sig_a * (1.0 + a_gate * (1.0 - sig_a)))
        dproj_ref[0, rs, W_MIX:2 * W_MIX] = d_ag.astype(BF16)
        gvec_ref[V_POOL_SCALE] += _colsum8(d_out_a * z)
        dz = d_out_a * pools_ref[...]
        gvec_ref[V_POOL_B] += _colsum8(dz)
        t = _row_index((sub, POOL_GC), t0 + r0)
        n_ext = sub + POOL_HALO
        for g, w in enumerate(POOL_WINDOWS):
            lanes = slice(g * POOL_GC, (g + 1) * POOL_GC)
            dzg = dz[:, lanes].astype(BF16)
            gpoolw_ref[g] += lax.dot_general(dpool_ref[0, rs, lanes], dzg, TN_DIMS, preferred_element_type=F32)
            dd = lax.dot_general(dzg, poolw_ref[g].astype(BF16), NT_DIMS, preferred_element_type=F32)
            vext[rs, lanes] = dd * (1.0 / jnp.minimum(t + 1, w).astype(F32))
            s = vext[r0:r0 + n_ext, lanes]
            sh = 1
            while sh < w:
                s = s + pltpu.roll(s, n_ext - sh, 0)
                sh *= 2
            dproj_ref[0, rs, lanes] = (s[0:sub] - dd).astype(BF16)

        b_gate = proj_ref[0, rs, 4 * W_MIX:5 * W_MIX]
        sig_b = jax.nn.sigmoid(b_gate)
        silu_b = b_gate * sig_b
        ob = ob_ref[0, rs, :]
        y_s[rs, W_MIX:] = (ob * silu_b).astype(BF16)
        d_ob = dy_b * silu_b
        dproj_ref[0, rs, 4 * W_MIX:5 * W_MIX] = (dy_b * ob * (sig_b * (1.0 + b_gate * (1.0 - sig_b)))).astype(BF16)
        gvec_ref[V_PW_B] += _colsum8(d_ob)
        gwout_ref[...] += lax.dot_general(y_s[rs, :], dhb, TN_DIMS, preferred_element_type=F32)

        h1 = h1_ref[0, rs, :]
        mu = jnp.mean(h1, axis=-1, keepdims=True)
        xc = h1 - mu
        rstd = lax.rsqrt(jnp.mean(xc * xc, axis=-1, keepdims=True) + LN_EPS)
        xhat = xc * rstd
        h2 = xhat * lg_ref[...] + lb_ref[...]
        sig2 = jax.nn.sigmoid(h2)
        d_obb = d_ob.astype(BF16)
        gpww_ref[...] += lax.dot_general((h2 * sig2).astype(BF16), d_obb, TN_DIMS, preferred_element_type=F32)
        dh2 = lax.dot_general(d_obb, pww_ref[...], NT_DIMS, preferred_element_type=F32) * (sig2 * (1.0 + h2 * (1.0 - sig2)))
        gvec_ref[V_LN_G] += _colsum8(dh2 * xhat)
        gvec_ref[V_LN_B] += _colsum8(dh2)
        dxh = dh2 * lg_ref[...]
        dh1 = rstd * (dxh - jnp.mean(dxh, axis=-1, keepdims=True) - xhat * jnp.mean(dxh * xhat, axis=-1, keepdims=True))
        gvec_ref[V_CONV_B] += _colsum8(dh1)
        gext[rs, :] = dh1

    taps = _bwd_taps()
    cr = BWD_CONV_ROWS
    win = cr + CONV_HALO - SUBLANES

    def conv_phase(r0):
        for c0 in range(r0, r0 + sub, cr):
            for l0 in range(0, W_MIX, LANES):
                lanes = slice(l0, l0 + LANES)
                b_val = proj_ref[0, c0:c0 + cr, 2 * W_MIX + l0:2 * W_MIX + l0 + LANES]
                sg = jax.nn.sigmoid(proj_ref[0, c0:c0 + cr, 3 * W_MIX + l0:3 * W_MIX + l0 + LANES])
                h0c = b_val * sg
                acc = jnp.zeros((cr, LANES), F32)
                for s in range(SUBLANES):
                    gsh[...] = gext[c0 + s:c0 + s + win, lanes]
                    for q, k in taps[s]:
                        gq = gsh[SUBLANES * q:SUBLANES * q + cr, :]
                        acc = acc + gq * dw_ref[k:k + 1, lanes]
                        gdw_ref[k, :, lanes] += _colsum8(h0c * gq)
                d_bval = acc * sg
                dproj_ref[0, c0:c0 + cr, 2 * W_MIX + l0:2 * W_MIX + l0 + LANES] = d_bval.astype(BF16)
                dproj_ref[0, c0:c0 + cr, 3 * W_MIX + l0:3 * W_MIX + l0 + LANES] = (d_bval * b_val * (1.0 - sg)).astype(BF16)

    starts = [k * sub for k in reversed(range(BWD_SUBTILES))]
    for r0 in starts:
        dense_phase(r0)
    for r0 in starts:
        conv_phase(r0)
    vext[tm:, :] = vext[0:POOL_HALO, :]
    gext[tm:, :] = gext[0:CONV_HALO, :]


def _bwd1(dh, proj, h1, z, ob, dpool, w_out_b, pw_w_b, pool_w, pool_scale, conv_dw_f, ln_g, ln_b):
    bl, s, _ = dh.shape
    tm = TILE_M
    nt = s // tm
    tok = lambda n: pl.BlockSpec((1, tm, n), lambda b, i: (b, nt - 1 - i, 0))
    res = lambda shape: pl.BlockSpec(shape, lambda b, i: (0,) * len(shape))
    out_shapes = [(D_MODEL, D_MODEL), (W_MIX, W_MIX), (len(POOL_WINDOWS), POOL_GC, POOL_GC), (6, SUBLANES, W_MIX),
                  (CONV_WIDTH, SUBLANES, W_MIX)]
    return pl.pallas_call(
        functools.partial(_bwd1_body, nt),
        name="bwd1",
        grid=(bl, nt),
        in_specs=[tok(D_MODEL), tok(D_IN), tok(W_MIX), tok(W_MIX), tok(W_MIX), tok(W_MIX)] + [VMEM] * 7,
        out_specs=[tok(D_IN)] + [res(sh) for sh in out_shapes],
        out_shape=[jax.ShapeDtypeStruct((bl, s, D_IN), BF16)] + [jax.ShapeDtypeStruct(sh, F32) for sh in out_shapes],
        scratch_shapes=[pltpu.VMEM((tm + POOL_HALO, W_MIX), F32), pltpu.VMEM((tm + CONV_HALO, W_MIX), F32),
                        pltpu.VMEM((BWD_CONV_ROWS + CONV_HALO - SUBLANES, LANES), F32), pltpu.VMEM((tm, D_MODEL), BF16)],
        compiler_params=pltpu.CompilerParams(dimension_semantics=("arbitrary", "arbitrary"), vmem_limit_bytes=VMEM_LIMIT),
    )(dh, proj, h1, z, ob, dpool, w_out_b, pw_w_b, pool_w, _row(pool_scale), conv_dw_f, _row(ln_g), _row(ln_b))


MESH = pl.DeviceIdType.MESH
W_IN_BLK = D_IN // N_DEV
W_OUT_BLK = D_MODEL // N_DEV
PW_BLK = W_MIX // N_DEV
DW_BLK = W_MIX // N_DEV
N_PEER = N_DEV - 1


SLAB = 2 * W_IN_BLK
N_SLABS = 4
F_SIB, F_X, F_Y, F_ON, F_PASS_X, F_PASS_Y, F_PASS_D = range(7)
W_IN_ONLY, OTHERS = (0,), (1, 2, 3)


def _inproj_body(n_tiles, x_ref, ng_ref, wint_ref, wout_ref, pww_ref, dw_ref,
                 proj_hbm, wt_hbm, woutf_hbm, pwwf_hbm, dwf_ref,
                 hn_s, wt_s, wout_s, pww_s, dw_blk, stage, out_sems, w_sems, send_sems, recv_sems):
    s = pl.program_id(0)
    i = pl.program_id(1)
    tm = TILE_IN
    x, y, c = lax.axis_index("x"), lax.axis_index("y"), lax.axis_index("c")
    sibling = (x, y, 1 - c)
    xn, yn, dg = (1 - x, y), (x, 1 - y), (1 - x, 1 - y)
    pick = lambda a, b: tuple(jnp.where(c == 1, p, q) for p, q in zip(a, b))
    on_from, on_to = pick(xn, yn), pick(yn, xn)

    def blocks(px, py, pc):
        k = 4 * px + 2 * py + pc
        rows = lambda ref, n: ref.at[pl.ds(pl.multiple_of(k * n, n), n), :]
        return (rows(wt_s, W_IN_BLK), rows(wout_s, W_OUT_BLK), rows(pww_s, PW_BLK), dw_blk.at[k])

    def copies(arrays, flow, block, to):
        refs = blocks(*block)
        return [pltpu.make_async_remote_copy(src_ref=refs[a], dst_ref=refs[a], send_sem=send_sems.at[a, flow],
                                             recv_sem=recv_sems.at[a, flow], device_id=to, device_id_type=MESH)
                for a in arrays]

    def start(arrays):
        for flow, to in ((F_SIB, sibling), (F_X, (*xn, c)), (F_Y, (*yn, c))):
            for cp in copies(arrays, flow, (x, y, c), to):
                cp.start()

    def direct(arrays):
        for flow, chip in ((F_X, xn), (F_Y, yn)):
            for cp in copies(arrays, flow, (*chip, c), sibling):
                cp.wait_recv()
        for cp in copies(arrays, F_ON, (*on_from, c), (*on_to, c)):
            cp.start()
        for flow, chip in ((F_PASS_X, xn), (F_PASS_Y, yn)):
            for cp in copies(arrays, flow, (*chip, c), sibling):
                cp.start()

    def diagonal(arrays):
        for cp in copies(arrays, F_ON, (*dg, c), sibling):
            cp.wait_recv()
        for cp in copies(arrays, F_PASS_D, (*dg, c), sibling):
            cp.start()

    def from_sibling(arrays, flow, chip):
        for cp in copies(arrays, flow, (*chip, 1 - c), sibling):
            cp.wait_recv()

    def finish(arrays):
        mine = (x, y, c)
        for flow, block in ((F_SIB, mine), (F_X, mine), (F_Y, mine), (F_ON, (*on_from, c)), (F_PASS_X, (*xn, c)),
                            (F_PASS_Y, (*yn, c)), (F_PASS_D, (*dg, c))):
            for cp in copies(arrays, flow, block, sibling):
                cp.wait_send()

    step = s * n_tiles + i
    par = step % 2
    first = step == 0

    @pl.when(first)
    def _():
        mine = blocks(x, y, c)
        mine[0][...] = wint_ref[...].astype(BF16)
        mine[1][...] = wout_ref[...].astype(BF16)
        mine[2][...] = pww_ref[...].astype(BF16)
        mine[3][...] = dw_ref[...]
        start(W_IN_ONLY)
        start(OTHERS)

    rows_i = pl.ds(pl.multiple_of(i * tm, tm), tm)

    @pl.when(s == 0)
    def _():
        xt = x_ref[0]
        r1 = lax.rsqrt(jnp.mean(xt * xt, axis=-1, keepdims=True) + RMS_EPS)
        hn_s[rows_i, :] = (xt * r1 * ng_ref[...]).astype(BF16)

    @pl.when(first)
    def _():
        from_sibling(W_IN_ONLY, F_SIB, (x, y))

    @pl.when((s == 1) & (i == 0))
    def _():
        direct(W_IN_ONLY)
        from_sibling(W_IN_ONLY, F_PASS_Y, yn)

    @pl.when((s == 2) & (i == 0))
    def _():
        from_sibling(W_IN_ONLY, F_PASS_X, xn)

    weights_out = [pltpu.make_async_copy(src, dst, w_sems.at[k])
                   for k, (src, dst) in enumerate(((wt_s, wt_hbm), (wout_s, woutf_hbm), (pww_s, pwwf_hbm)))]

    @pl.when((s == 3) & (i == 0))
    def _():
        diagonal(W_IN_ONLY)
        direct(OTHERS)
        from_sibling(W_IN_ONLY, F_PASS_D, dg)
        weights_out[0].start()

    tiles_per_seq = proj_hbm.shape[1] // tm

    def out_copy(parity, col0):
        dst = proj_hbm.at[i // tiles_per_seq, pl.ds(pl.multiple_of((i % tiles_per_seq) * tm, tm), tm),
                          pl.ds(pl.multiple_of(col0, LANES), SLAB)]
        return pltpu.make_async_copy(stage.at[parity], dst, out_sems.at[parity])

    @pl.when(step >= 2)
    def _():
        out_copy(par, 0).wait()

    chip = jnp.where(s < 2, 2 * x, 2 * (1 - x)) + jnp.where((s == 0) | (s == 2), y, 1 - y)
    slab = wt_s[pl.ds(pl.multiple_of(chip * SLAB, SLAB), SLAB), :]
    stage[par] = lax.dot_general(hn_s[rows_i, :], slab, NT_DIMS, preferred_element_type=F32)
    out_copy(par, chip * SLAB).start()

    @pl.when(step == N_SLABS * n_tiles - 1)
    def _():
        out_copy(par, 0).wait()
        out_copy(1 - par, 0).wait()
        from_sibling(OTHERS, F_SIB, (x, y))
        from_sibling(OTHERS, F_PASS_X, xn)
        from_sibling(OTHERS, F_PASS_Y, yn)
        diagonal(OTHERS)
        from_sibling(OTHERS, F_PASS_D, dg)
        for cp in weights_out[1:]:
            cp.start()
        finish(W_IN_ONLY)
        finish(OTHERS)
        for cp in weights_out:
            cp.wait()
        for k in range(N_DEV):
            dwf_ref[:, k * DW_BLK:(k + 1) * DW_BLK] = dw_blk[k]


def _inproj(x, norm_g, w_in_t, w_out, pw_w, conv_dw):
    bl, s, _ = x.shape
    tm = TILE_IN
    nt = s // tm
    n_tiles = bl * nt

    def x_map(phase, i):
        i = jnp.where(phase == 0, i, n_tiles - 1)
        return (i // nt, i % nt, 0)

    dma = pltpu.SemaphoreType.DMA
    return pl.pallas_call(
        functools.partial(_inproj_body, n_tiles),
        name="inproj",
        grid=(N_SLABS, n_tiles),
        in_specs=[pl.BlockSpec((1, tm, D_MODEL), x_map)] + [VMEM] * 5,
        out_specs=[pl.BlockSpec(memory_space=pl.ANY)] * 4 + [VMEM],
        out_shape=[jax.ShapeDtypeStruct((bl, s, D_IN), F32), jax.ShapeDtypeStruct((D_IN, D_MODEL), BF16),
                   jax.ShapeDtypeStruct((D_MODEL, D_MODEL), BF16), jax.ShapeDtypeStruct((W_MIX, W_MIX), BF16),
                   jax.ShapeDtypeStruct((CONV_WIDTH, W_MIX), F32)],
        scratch_shapes=[pltpu.VMEM((n_tiles * tm, D_MODEL), BF16), pltpu.VMEM((D_IN, D_MODEL), BF16),
                        pltpu.VMEM((D_MODEL, D_MODEL), BF16), pltpu.VMEM((W_MIX, W_MIX), BF16),
                        pltpu.VMEM((N_DEV, CONV_WIDTH, DW_BLK), F32), pltpu.VMEM((2, tm, SLAB), F32),
                        dma((2,)), dma((3,)), dma((4, 7)), dma((4, 7))],
        compiler_params=pltpu.CompilerParams(dimension_semantics=("arbitrary", "arbitrary"), vmem_limit_bytes=VMEM_LIMIT),
    )(x, _row(norm_g), w_in_t, w_out, pw_w, conv_dw)


B_PLAIN, B_ON, B_MERGED = range(3)


def _routes():
    x, y, c = lax.axis_index("x"), lax.axis_index("y"), lax.axis_index("c")
    xn, yn = (1 - x, y), (x, 1 - y)
    pick = lambda a, b: tuple(jnp.where(c == 1, p, q) for p, q in zip(a, b))
    return pick(xn, yn), pick(yn, xn), (1 - x, 1 - y), (x, y), c


def _chip_index(chip):
    return 2 * chip[0] + chip[1]


def _copy(src, dst, send_sems, recv_sems, a, k, to):
    return pltpu.make_async_remote_copy(src_ref=src, dst_ref=dst, send_sem=send_sems.at[a, k], recv_sem=recv_sems.at[a, k],
                                        device_id=to, device_id_type=MESH)


def _rsb_send(plain, on, rb_plain, rb_on, send_b, recv_b):
    first, _, _, _, c = _routes()
    for k, srcs, dsts in ((B_ON, on, rb_on), (B_PLAIN, plain, rb_plain)):
        for a, (s, d) in enumerate(zip(srcs, dsts)):
            _copy(s, d, send_b, recv_b, a, k, (*first, c)).start()


def _rsb_merge(second_sums, merged, rb_on, rb_merged, send_b, recv_b):
    _, second, _, _, c = _routes()
    for a, (own, m, got, dst) in enumerate(zip(second_sums, merged, rb_on, rb_merged)):
        _copy(got, got, send_b, recv_b, a, B_ON, (*second, c)).wait_recv()
        m[...] = (own[...] + got[...].astype(F32)).astype(BF16)
        _copy(m, dst, send_b, recv_b, a, B_MERGED, (*second, c)).start()


def _rsb_total(own_sums, plain, on, merged, rb_plain, rb_merged, send_b, recv_b):
    _, second, _, _, c = _routes()
    out = []
    for a, (own, p, g) in enumerate(zip(own_sums, rb_plain, rb_merged)):
        _copy(p, p, send_b, recv_b, a, B_PLAIN, (*second, c)).wait_recv()
        _copy(g, g, send_b, recv_b, a, B_MERGED, (*second, c)).wait_recv()
        out.append(own[...] + p[...].astype(F32) + g[...].astype(F32))
    for k, srcs in ((B_PLAIN, plain), (B_ON, on), (B_MERGED, merged)):
        for a, s in enumerate(srcs):
            _copy(s, s, send_b, recv_b, a, k, (*second, c)).wait_send()
    return out


R_LOSS, R_FINAL_G, R_POOL, R_CONV, R_LN = range(5)
VEC_ROWS = 16
DW_ROWS = 32
A_PART, A_X, A_Y, A_ON, A_HALF = range(5)
HALF_DTYPES = (F32, BF16, BF16)
STEP_SUMS, STEP_MERGE, STEP_HALF = 1, 3, 5


def _bwd2a_body(n_steps, x_ref, dproj_ref, ng_ref, gwout_hbm, gpww_hbm, loss_ref, gfg_ref, gvec_ref, gdw_ref, gpoolw_ref,
                win_own_ref, win_staged_ref, owout_ref, opww_ref, ovec_ref, odw_ref, opoolw_ref,
                acc, st_win, a_win, own_wout, own_pww, a_wout, a_pww, st_wout, st_pww, rb_wout, rb_pww, part_vec, part_dw,
                part_pw, sib_vec, sib_dw, sib_pw, res_vec, res_dw, res_pw, half_vec, half_dw, half_pw,
                own_sems, send_a, recv_a, send_b, recv_b, send_r, recv_r, send_w, recv_w):
    step = pl.program_id(0) * pl.num_programs(1) + pl.program_id(1)
    x, y, c = lax.axis_index("x"), lax.axis_index("y"), lax.axis_index("c")
    sibling = (x, y, 1 - c)
    xn, yn = (1 - x, y), (x, 1 - y)
    first, second, diag, own, _ = _routes()
    rows = lambda ref, k, n: ref.at[pl.ds(pl.multiple_of(k * n, n), n), :]

    blocks = lambda k: (rows(gwout_hbm, k, W_OUT_BLK), rows(gpww_hbm, k, PW_BLK))
    mine = lambda chip: (own_wout.at[chip], own_pww.at[chip])
    mine_copies = lambda chip: [pltpu.make_async_copy(s, d, own_sems.at[a, chip])
                                for a, (s, d) in enumerate(zip(blocks(2 * chip + c), mine(chip)))]
    landed = lambda chip: (a_wout.at[chip], a_pww.at[chip])
    staged = lambda k: (st_wout.at[k], st_pww.at[k])
    arrived = lambda k: (rb_wout.at[k], rb_pww.at[k])

    parts, sibs, halves = (part_vec, part_dw, part_pw), (sib_vec, sib_dw, sib_pw), (half_vec, half_dw, half_pw)
    results = (res_vec, res_dw, res_pw)
    half_rows = lambda ref: pl.ds(pl.multiple_of(c * (ref.shape[0] // 2), SUBLANES), ref.shape[0] // 2)
    slot = lambda chip: tuple(h.at[_chip_index(chip)] for h in halves)

    @pl.when(step == 0)
    def _():
        acc[...] = jnp.zeros_like(acc)
        sum8 = lambda v: jnp.sum(v, axis=0, keepdims=True)
        part_vec[...] = jnp.zeros((VEC_ROWS, D_MODEL), F32)
        part_vec[R_LOSS:R_LOSS + 1, :] = jnp.broadcast_to(loss_ref[0:1, 0:1], (1, D_MODEL))
        part_vec[R_FINAL_G:R_FINAL_G + 1, :] = sum8(gfg_ref[...])
        for row, (lo, hi) in ((R_POOL, (V_POOL_B, V_POOL_SCALE)), (R_CONV, (V_CONV_B, V_LN_G)), (R_LN, (V_LN_B, V_PW_B))):
            part_vec[row:row + 1, 0:W_MIX] = sum8(gvec_ref[lo])
            part_vec[row:row + 1, W_MIX:] = sum8(gvec_ref[hi])
        part_dw[0:CONV_WIDTH, :] = jnp.sum(gdw_ref[...], axis=1)
        part_dw[CONV_WIDTH:, :] = jnp.zeros((DW_ROWS - CONV_WIDTH, W_MIX), F32)
        part_pw[...] = gpoolw_ref[...].reshape(len(POOL_WINDOWS) * POOL_GC, POOL_GC)
        for a, (p, s) in enumerate(zip(parts, sibs)):
            _copy(p, s, send_r, recv_r, a, A_PART, sibling).start()
        for chip in range(4):
            for a, (s, d) in enumerate(zip(blocks(2 * chip + (1 - c)), landed(chip))):
                _copy(s, d, send_a, recv_a, a, chip, sibling).start()
            for cp in mine_copies(chip):
                cp.start()

    x_t = x_ref[0]
    r1 = lax.rsqrt(jnp.mean(x_t * x_t, axis=-1, keepdims=True) + RMS_EPS)
    hn = (x_t * r1 * ng_ref[...]).astype(BF16)
    acc[...] += lax.dot_general(dproj_ref[0], hn, TN_DIMS, preferred_element_type=F32)

    @pl.when(step == STEP_SUMS)
    def _():
        for a, (p, s, h) in enumerate(zip(parts, sibs, slot(own))):
            _copy(p, s, send_r, recv_r, a, A_PART, sibling).wait_recv()
            h[...] = (p[half_rows(p), :] + s[half_rows(s), :]).astype(h.dtype)
            for k, to in ((A_X, xn), (A_Y, yn)):
                _copy(h, h, send_r, recv_r, a, k, (*to, c)).start()
        for chip in range(4):
            for cp in mine_copies(chip):
                cp.wait()
            for a, (m, l) in enumerate(zip(mine(chip), landed(chip))):
                _copy(l, l, send_a, recv_a, a, chip, sibling).wait_recv()
                l[...] = m[...] + l[...]
        for k, chip in ((B_ON, diag), (B_PLAIN, first)):
            for s, l in zip(staged(k), landed(_chip_index(chip))):
                s[...] = l[...].astype(BF16)
        _rsb_send(staged(B_PLAIN), staged(B_ON), arrived(B_PLAIN), arrived(B_ON), send_b, recv_b)

    @pl.when(step == STEP_MERGE)
    def _():
        _rsb_merge(landed(_chip_index(second)), staged(B_MERGED), arrived(B_ON), arrived(B_MERGED), send_b, recv_b)
        for a in range(3):
            for k, chip in ((A_X, xn), (A_Y, yn)):
                h = slot(chip)[a]
                _copy(h, h, send_r, recv_r, a, k, sibling).wait_recv()
            h = slot(first)[a]
            _copy(h, h, send_r, recv_r, a, A_ON, (*second, c)).start()

    @pl.when(step == STEP_HALF)
    def _():
        for a, (r, hs) in enumerate(zip(results, halves)):
            h = slot(diag)[a]
            _copy(h, h, send_r, recv_r, a, A_ON, sibling).wait_recv()
            mine = r.at[half_rows(r), :]
            chip_sums = [hs[k].astype(F32) for k in range(4)]
            mine[...] = (chip_sums[0] + chip_sums[1]) + (chip_sums[2] + chip_sums[3])
            _copy(mine, mine, send_r, recv_r, a, A_HALF, sibling).start()

    @pl.when(step == n_steps - 1)
    def _():
        block = lambda k: rows(acc, k, W_IN_BLK)
        w_copy = lambda chip: _copy(st_win.at[chip], a_win.at[chip], send_w, recv_w, 0, chip, sibling)
        for chip in range(4):
            st_win[chip] = block(2 * chip + (1 - c))[...].astype(BF16)
            w_copy(chip).start()

        owout_ref[...], opww_ref[...] = _rsb_total(landed(_chip_index(own)), staged(B_PLAIN), staged(B_ON), staged(B_MERGED),
                                                   arrived(B_PLAIN), arrived(B_MERGED), send_b, recv_b)
        for a, (r, p, s) in enumerate(zip(results, parts, sibs)):
            theirs = r.at[pl.ds(pl.multiple_of((1 - c) * (r.shape[0] // 2), SUBLANES), r.shape[0] // 2), :]
            _copy(theirs, theirs, send_r, recv_r, a, A_HALF, sibling).wait_recv()
            _copy(p, s, send_r, recv_r, a, A_PART, sibling).wait_send()
            for k, h in ((A_X, slot(own)[a]), (A_Y, slot(own)[a]), (A_ON, slot(first)[a])):
                _copy(h, h, send_r, recv_r, a, k, sibling).wait_send()
            mine = r.at[half_rows(r), :]
            _copy(mine, mine, send_r, recv_r, a, A_HALF, sibling).wait_send()
        for chip in range(4):
            for a, (s, d) in enumerate(zip(blocks(2 * chip + (1 - c)), landed(chip))):
                _copy(s, d, send_a, recv_a, a, chip, sibling).wait_send()
        ovec_ref[...] = res_vec[...]
        opoolw_ref[...] = res_pw[...]
        for k in range(N_DEV):
            @pl.when(4 * x + 2 * y + c == k)
            def _():
                odw_ref[...] = res_dw[0:CONV_WIDTH, k * DW_BLK:(k + 1) * DW_BLK]

        for chip in range(4):
            w_copy(chip).wait()
        chip_sum = lambda chip: block(2 * _chip_index(chip) + c)[...] + a_win[_chip_index(chip)].astype(F32)
        win_own_ref[...] = chip_sum(own)
        for j, chip in enumerate((xn, yn, diag)):
            win_staged_ref[j] = chip_sum(chip).astype(BF16)


def _bwd2a(x, dproj, norm_g, gwout, gpww, loss8, gfg8, gvec, gdw8, gpoolw):
    bl, s, _ = x.shape
    tm = TILE_M
    nt = s // tm
    tok = lambda n: pl.BlockSpec((1, tm, n), lambda b, i: (b, i, 0))
    n_pw = len(POOL_WINDOWS) * POOL_GC
    win = (W_IN_BLK, D_MODEL)
    blk = ((W_OUT_BLK, D_MODEL), (PW_BLK, W_MIX))
    rep = ((VEC_ROWS, D_MODEL), (DW_ROWS, W_MIX), (n_pw, POOL_GC))
    half = lambda sh: (sh[0] // 2,) + sh[1:]
    dma = pltpu.SemaphoreType.DMA
    assert bl * nt > STEP_HALF + 1, "the exchanges' phases need their grid steps"
    return pl.pallas_call(
        functools.partial(_bwd2a_body, bl * nt),
        name="bwd2a",
        grid=(bl, nt),
        in_specs=[tok(D_MODEL), tok(D_IN), VMEM] + [pl.BlockSpec(memory_space=pl.ANY)] * 2 + [VMEM] * 5,
        out_specs=[VMEM] * 7,
        out_shape=[jax.ShapeDtypeStruct(win, F32), jax.ShapeDtypeStruct((3,) + win, BF16)]
        + [jax.ShapeDtypeStruct(sh, F32) for sh in blk + (rep[0], (CONV_WIDTH, DW_BLK), rep[2])],
        scratch_shapes=[pltpu.VMEM((D_IN, D_MODEL), F32), pltpu.VMEM((4,) + win, BF16), pltpu.VMEM((4,) + win, BF16)]
        + [pltpu.VMEM((4,) + sh, F32) for sh in blk] * 2
        + [pltpu.VMEM((3,) + sh, BF16) for sh in blk] * 2
        + [pltpu.VMEM(sh, F32) for sh in rep] * 3
        + [pltpu.VMEM((4,) + half(sh), dt) for sh, dt in zip(rep, HALF_DTYPES)]
        + [dma((2, 4)), dma((2, 4)), dma((2, 4)), dma((2, 3)), dma((2, 3)), dma((3, 5)), dma((3, 5)), dma((1, 4)), dma((1, 4))],
        compiler_params=pltpu.CompilerParams(dimension_semantics=("arbitrary", "arbitrary"), vmem_limit_bytes=VMEM_LIMIT),
    )(x, dproj, _row(norm_g), gwout, gpww, loss8, gfg8, gvec, gdw8, gpoolw)


HBM = pl.BlockSpec(memory_space=pltpu.HBM)
SEM = pl.BlockSpec(memory_space=pltpu.SEMAPHORE)
EFFECT = pltpu.SideEffectType.DATAFLOW_SIDE_EFFECTING


def _owner_copies(src_ref, land_ref, send_sems, recv_sems):
    x, y, c = lax.axis_index("x"), lax.axis_index("y"), lax.axis_index("c")
    owners = ((1 - x, y, c), (x, 1 - y, c), (1 - x, 1 - y, c))
    return [pltpu.make_async_remote_copy(src_ref=src_ref.at[j], dst_ref=land_ref.at[j], send_sem=send_sems.at[j],
                                         recv_sem=recv_sems.at[j], device_id=owners[j], device_id_type=MESH)
            for j in range(3)]


def _rs_start(staged):
    def body(src_ref, land_ref, send_sems, recv_sems, src_thru, land_thru, token):
        for cp in _owner_copies(src_ref, land_ref, send_sems, recv_sems):
            cp.start()
        token[...] = jnp.zeros_like(token)

    dma = pltpu.SemaphoreType.DMA
    return pl.pallas_call(
        body, name="rs_start",
        out_shape=(dma((3,)), dma((3,)), pltpu.HBM(staged.shape, staged.dtype), pltpu.HBM(staged.shape, staged.dtype),
                   jax.ShapeDtypeStruct((SUBLANES, LANES), F32)),
        in_specs=(HBM, HBM), out_specs=(SEM, SEM, HBM, HBM, VMEM), input_output_aliases={0: 2, 1: 3},
        compiler_params=pltpu.CompilerParams(has_side_effects=EFFECT),
    )(pltpu.with_memory_space_constraint(staged, pltpu.HBM),
      pltpu.with_memory_space_constraint(lax.empty(staged.shape, staged.dtype), pltpu.HBM))


def _rs_wait(send_sems, recv_sems, src_thru, land_thru, after):
    def body(src_ref, land_ref, send_sems, recv_sems, after_ref, src_dead, got_ref):
        for cp in _owner_copies(src_ref, land_ref, send_sems, recv_sems):
            cp.wait_send()
            cp.wait_recv()

    return pl.pallas_call(
        body, name="rs_wait",
        out_shape=(pltpu.HBM(src_thru.shape, src_thru.dtype), pltpu.HBM(land_thru.shape, land_thru.dtype)),
        in_specs=(HBM, HBM, SEM, SEM, pl.BlockSpec(memory_space=pl.ANY)), out_specs=(HBM, HBM),
        input_output_aliases={0: 0, 1: 1},
        compiler_params=pltpu.CompilerParams(has_side_effects=EFFECT),
    )(src_thru, land_thru, send_sems, recv_sems, after)[1]


def _bwd2b_body(x_ref, dh_ref, dproj_ref, ng_ref, wint_ref, token_ref, gx_ref, gng_ref):
    @pl.when((pl.program_id(0) == 0) & (pl.program_id(1) == 0))
    def _():
        gng_ref[...] = jnp.zeros_like(gng_ref) + token_ref[0:1, 0:1]

    x_t = x_ref[0]
    r1 = lax.rsqrt(jnp.mean(x_t * x_t, axis=-1, keepdims=True) + RMS_EPS)
    xr = x_t * r1
    dhn = jnp.dot(dproj_ref[0], wint_ref[...], preferred_element_type=F32)
    gng_ref[...] += _colsum8(dhn * xr)
    gy = dhn * ng_ref[...]
    gx_ref[0] = dh_ref[0] + r1 * (gy - xr * jnp.mean(gy * xr, axis=-1, keepdims=True))


def _bwd2b(x, dh, dproj, norm_g, w_in_t, token):
    bl, s, _ = x.shape
    tm = TILE_M
    tok = lambda n: pl.BlockSpec((1, tm, n), lambda b, i: (b, i, 0))
    return pl.pallas_call(
        functools.partial(_bwd2b_body),
        name="bwd2b",
        grid=(bl, s // tm),
        in_specs=[tok(D_MODEL), tok(D_MODEL), tok(D_IN), VMEM, VMEM, VMEM],
        out_specs=[tok(D_MODEL), pl.BlockSpec((SUBLANES, D_MODEL), lambda b, i: (0, 0))],
        out_shape=[jax.ShapeDtypeStruct((bl, s, D_MODEL), F32), jax.ShapeDtypeStruct((SUBLANES, D_MODEL), F32)],
        compiler_params=pltpu.CompilerParams(dimension_semantics=("arbitrary", "arbitrary"), vmem_limit_bytes=VMEM_LIMIT),
    )(x, dh, dproj, _row(norm_g), w_in_t, token)


def _partial_copies(part_ref, land_ref, send_sems, recv_sems):
    x, y, c = lax.axis_index("x"), lax.axis_index("y"), lax.axis_index("c")
    other_chips = [(1 - x, y), (x, 1 - y), (1 - x, 1 - y)]
    peers = [(x, y, 1 - c)] + [(*chip, c) for chip in other_chips] + [(*chip, 1 - c) for chip in other_chips]
    return [pltpu.make_async_remote_copy(src_ref=part_ref, dst_ref=land_ref.at[4 * x + 2 * y + c], send_sem=send_sems.at[r],
                                         recv_sem=recv_sems.at[r], device_id=peers[r], device_id_type=MESH)
            for r in range(N_PEER)]


def _partials_start(part):
    def body(part_ref, land_ref, send_sems, recv_sems, part_thru, land_thru):
        for cp in _partial_copies(part_ref, land_ref, send_sems, recv_sems):
            cp.start()

    dma = pltpu.SemaphoreType.DMA
    land = (N_DEV,) + part.shape
    return pl.pallas_call(
        body, name="partials_start",
        out_shape=(dma((N_PEER,)), dma((N_PEER,)), pltpu.HBM(part.shape, part.dtype), pltpu.HBM(land, part.dtype)),
        in_specs=(HBM, HBM), out_specs=(SEM, SEM, HBM, HBM), input_output_aliases={0: 2, 1: 3},
        compiler_params=pltpu.CompilerParams(has_side_effects=EFFECT),
    )(pltpu.with_memory_space_constraint(part, pltpu.HBM),
      pltpu.with_memory_space_constraint(lax.empty(land, part.dtype), pltpu.HBM))


def _partials_wait(send_sems, recv_sems, part_thru, land_thru, after):
    def body(part_ref, land_ref, send_sems, recv_sems, after_ref, part_out, land_out):
        for cp in _partial_copies(part_ref, land_ref, send_sems, recv_sems):
            cp.wait_send()
            cp.wait_recv()

    return pl.pallas_call(
        body, name="partials_wait",
        out_shape=(pltpu.HBM(part_thru.shape, part_thru.dtype), pltpu.HBM(land_thru.shape, land_thru.dtype)),
        in_specs=(HBM, HBM, SEM, SEM, pl.BlockSpec(memory_space=pl.ANY)), out_specs=(HBM, HBM),
        input_output_aliases={0: 0, 1: 1},
        compiler_params=pltpu.CompilerParams(has_side_effects=EFFECT),
    )(part_thru, land_thru, send_sems, recv_sems, after)


ADAM_LR = 0.001
ADAM_B1 = 0.9
ADAM_B2 = 0.999
ADAM_EPS = 1e-08
ADAM_WD = 0.01
ADAM_STEP = 10


PARAMS = ("norm_g", "w_in", "pool_w", "pool_b", "pool_scale", "conv_dw", "conv_b", "ln_g", "ln_b", "pw_w", "pw_b", "w_out",
          "final_g")


LATE_PARAMS = ("norm_g",)
MAIN_PARAMS = tuple(p for p in PARAMS if p not in LATE_PARAMS)


def _adamw_refs(params, n_sources, refs):
    n = len(params)
    ws, ms, vs = (dict(zip(params, refs[k * n:(k + 1) * n])) for k in range(3))
    outs = refs[3 * n + n_sources:]
    gs, ds, m2s, v2s = (dict(zip(params, outs[k * n:(k + 1) * n])) for k in range(4))

    def update(name, g, at=slice(None)):
        m = ADAM_B1 * ms[name][at] + (1.0 - ADAM_B1) * g
        v = ADAM_B2 * vs[name][at] + (1.0 - ADAM_B2) * (g * g)
        m_hat = m / (1.0 - ADAM_B1 ** ADAM_STEP)
        v_hat = v / (1.0 - ADAM_B2 ** ADAM_STEP)
        gs[name][at] = g
        ds[name][at] = -ADAM_LR * (m_hat / (jnp.sqrt(v_hat) + ADAM_EPS) + ADAM_WD * ws[name][at])
        m2s[name][at] = m
        v2s[name][at] = v

    return update, refs[3 * n:3 * n + n_sources], outs[4 * n:]


def _adamw_late_body(*refs):
    update, (gng_ref, gng_others_ref), _ = _adamw_refs(LATE_PARAMS, 2, refs)
    me = 4 * lax.axis_index("x") + 2 * lax.axis_index("y") + lax.axis_index("c")
    g_norm = jnp.zeros(gng_ref.shape, F32)
    for k in range(N_DEV):
        g_norm = g_norm + jnp.where(me == k, gng_ref[...], gng_others_ref[k])
    update("norm_g", jnp.sum(g_norm, axis=0, keepdims=True))


def _adamw_body(*refs):
    update, sources, (loss_ref,) = _adamw_refs(MAIN_PARAMS, 7, refs)
    gwin_ref, gwin_others_ref, gwout_ref, gpww_ref, vec_ref, gdw_ref, gpoolw_ref = sources
    ws = dict(zip(MAIN_PARAMS, refs))
    row = lambda r, lo, hi: vec_ref[r:r + 1, lo:hi]
    loss_ref[...] = row(R_LOSS, 0, 1)
    g_w_in = gwin_ref[...]
    for j in range(3):
        g_w_in = g_w_in + gwin_others_ref[j].astype(F32)
    update("w_in", g_w_in)
    update("pool_w", gpoolw_ref[...].reshape(ws["pool_w"].shape))
    for g in range(len(POOL_WINDOWS)):
        update("pool_b", row(R_POOL, g * POOL_GC, (g + 1) * POOL_GC), at=slice(g, g + 1))
    update("pool_scale", row(R_POOL, W_MIX, 2 * W_MIX))
    update("conv_dw", gdw_ref[...])
    update("conv_b", row(R_CONV, 0, W_MIX))
    update("ln_g", row(R_CONV, W_MIX, 2 * W_MIX))
    update("ln_b", row(R_LN, 0, W_MIX))
    update("pw_w", gpww_ref[...])
    update("pw_b", row(R_LN, W_MIX, 2 * W_MIX))
    update("w_out", gwout_ref[...])
    update("final_g", row(R_FINAL_G, 0, D_MODEL))


def _adamw(body, name, params, ws, ms, vs, grad_sources, extra_out_shapes=()):
    n = len(params)
    flat = lambda a: a.reshape(1, -1) if a.ndim == 1 else a
    args = [flat(d[p]) for d in (ws, ms, vs) for p in params]
    outs = pl.pallas_call(
        body,
        name=name,
        in_specs=[VMEM] * (3 * n + len(grad_sources)),
        out_specs=[VMEM] * (4 * n + len(extra_out_shapes)),
        out_shape=[jax.ShapeDtypeStruct(a.shape, F32) for a in args[:n]] * 4 + list(extra_out_shapes),
        compiler_params=pltpu.CompilerParams(vmem_limit_bytes=VMEM_LIMIT),
    )(*args, *grad_sources)
    groups = tuple({p: outs[k * n + j].reshape(ws[p].shape) for j, p in enumerate(params)} for k in range(4))
    return groups, outs[4 * n:]


def kernel(x, norm_g, w_in, pool_w, pool_b, pool_scale, conv_dw, conv_b, ln_g, ln_b, pw_w, pw_b, w_out, final_g, loss_target, m_norm_g, m_w_in, m_pool_w, m_pool_b, m_pool_scale, m_conv_dw, m_conv_b, m_ln_g, m_ln_b, m_pw_w, m_pw_b, m_w_out, m_final_g, v_norm_g, v_w_in, v_pool_w, v_pool_b, v_pool_scale, v_conv_dw, v_conv_b, v_ln_g, v_ln_b, v_pw_w, v_pw_b, v_w_out, v_final_g):
    proj, w_in_t, w_out_b, pw_w_b, conv_dw_f = _inproj(x, norm_g, w_in.T, w_out, pw_w, conv_dw)
    h1, z, ob, dpool, dh, loss8, gfg8 = _fwd(x, loss_target, proj, pool_w, pool_b, pool_scale, conv_dw_f,
                                             conv_b, ln_g, ln_b, pw_w_b, pw_b, w_out_b, final_g)
    dproj, gwout, gpww, gpoolw, gvec, gdw8 = _bwd1(dh, proj, h1, z, ob, dpool, w_out_b, pw_w_b, pool_w, pool_scale,
                                                   conv_dw_f, ln_g, ln_b)
    win_own, win_staged, g_w_out, g_pw_w, vec, g_conv_dw, poolw_sum = _bwd2a(x, dproj, norm_g, gwout, gpww, loss8, gfg8, gvec,
                                                                             gdw8, gpoolw)
    send_sems, recv_sems, staged_thru, land_thru, token = _rs_start(win_staged)
    grad_x, gng8 = _bwd2b(x, dh, dproj, norm_g, w_in_t, token)
    ng_send, ng_recv, gng8, gng_land = _partials_start(gng8)
    win_others = _rs_wait(send_sems, recv_sems, staged_thru, land_thru, gng8)

    weights = dict(norm_g=norm_g, w_in=w_in.T, pool_w=pool_w, pool_b=pool_b, pool_scale=pool_scale, conv_dw=conv_dw, conv_b=conv_b,
                   ln_g=ln_g, ln_b=ln_b, pw_w=pw_w, pw_b=pw_b, w_out=w_out, final_g=final_g)
    m_in = dict(norm_g=m_norm_g, w_in=m_w_in.T, pool_w=m_pool_w, pool_b=m_pool_b, pool_scale=m_pool_scale, conv_dw=m_conv_dw,
                conv_b=m_conv_b, ln_g=m_ln_g, ln_b=m_ln_b, pw_w=m_pw_w, pw_b=m_pw_b, w_out=m_w_out, final_g=m_final_g)
    v_in = dict(norm_g=v_norm_g, w_in=v_w_in.T, pool_w=v_pool_w, pool_b=v_pool_b, pool_scale=v_pool_scale, conv_dw=v_conv_dw,
                conv_b=v_conv_b, ln_g=v_ln_g, ln_b=v_ln_b, pw_w=v_pw_w, pw_b=v_pw_b, w_out=v_w_out, final_g=v_final_g)
    main, (loss,) = _adamw(functools.partial(_adamw_body), "adamw", MAIN_PARAMS, weights, m_in, v_in,
                           (win_own, win_others, g_w_out, g_pw_w, vec, g_conv_dw, poolw_sum), [jax.ShapeDtypeStruct((1, 1), F32)])
    gng8, gng_others = _partials_wait(ng_send, ng_recv, gng8, gng_land, main[0]["w_out"])
    late, _ = _adamw(functools.partial(_adamw_late_body), "adamw_norm_g", LATE_PARAMS, weights, m_in, v_in, (gng8, gng_others))
    groups = [[(late[k] if p in LATE_PARAMS else main[k])[p] for p in PARAMS] for k in range(4)]
    k_in = PARAMS.index("w_in")
    for group in groups:
        group[k_in] = group[k_in].T
    grads, delta, new_m, new_v = groups
    return (loss.reshape(()), grad_x, *grads, *delta, *new_m, *new_v)
```

```python
import functools

import jax
import jax.numpy as jnp
from jax import lax
from jax.experimental import pallas as pl
from jax.experimental.pallas import tpu as pltpu

F32 = jnp.float32
BF16 = jnp.bfloat16

D_MODEL = 1024
W_MIX = 512
D_IN = 5 * W_MIX
POOL_WINDOWS = (2, 4, 8, 16)
POOL_GC = 128
CONV_WIDTH = 31
RMS_EPS = 1e-6
LN_EPS = 1e-5
N_DEV = 8

LANES = 128
SUBLANES = 8
TILE_M = 512
TILE_IN = 1024
POOL_HALO = 16
CONV_HALO = 32
CONV_ROWS = 128
BWD_CONV_ROWS = 64
BWD_SUBTILES = 2
VMEM_LIMIT = 56 * 1024 * 1024

VMEM = pl.BlockSpec(memory_space=pltpu.VMEM)


def _silu(v):
    return v * jax.nn.sigmoid(v)


def _colsum8(v):
    m, n = v.shape
    return jnp.sum(v.reshape(m // SUBLANES, SUBLANES, n), axis=0)


def _row_index(shape, t0):
    return lax.broadcasted_iota(jnp.int32, shape, 0) + t0


def _conv_taps():
    taps = {s: [] for s in range(SUBLANES)}
    for k in range(CONV_WIDTH):
        e = k + CONV_HALO - (CONV_WIDTH - 1)
        taps[e % SUBLANES].append((e // SUBLANES, k))
    return taps


def _fwd_body(x_ref, tgt_ref, proj_ref, poolw_ref, poolb_ref, pools_ref, dw_ref, cb_ref, lg_ref, lb_ref,
              pww_ref, pwb_ref, wout_ref, fg_ref,
              h1_ref, z_ref, ob_ref, dpool_ref, dh_ref, loss_ref, gfg_ref,
              uext, hext, y_s):
    b = pl.program_id(0)
    i = pl.program_id(1)
    tm = TILE_M

    @pl.when(i == 0)
    def _():
        uext[0:POOL_HALO, :] = jnp.zeros((POOL_HALO, W_MIX), F32)
        hext[0:CONV_HALO, :] = jnp.zeros((CONV_HALO, W_MIX), F32)
        hext[CONV_HALO + tm:, :] = jnp.zeros((SUBLANES, W_MIX), F32)

    @pl.when((b == 0) & (i == 0))
    def _():
        loss_ref[...] = jnp.zeros_like(loss_ref)
        gfg_ref[...] = jnp.zeros_like(gfg_ref)

    uext[POOL_HALO:, :] = proj_ref[0, :, 0:W_MIX]
    t = _row_index((tm, POOL_GC), i * tm)
    for g, w in enumerate(POOL_WINDOWS):
        lanes = slice(g * POOL_GC, (g + 1) * POOL_GC)
        e = uext[:, lanes]
        s = e
        sh = 1
        while sh < w:
            s = s + pltpu.roll(s, sh, 0)
            sh *= 2
        inv_cnt = 1.0 / jnp.minimum(t + 1, w).astype(F32)
        d = (s[POOL_HALO:] * inv_cnt - e[POOL_HALO:]).astype(BF16)
        dpool_ref[0, :, lanes] = d
        z = jnp.dot(d, poolw_ref[g].astype(BF16), preferred_element_type=F32) + poolb_ref[:, lanes]
        z_ref[0, :, lanes] = z
        a_gate = proj_ref[0, :, W_MIX + g * POOL_GC:W_MIX + (g + 1) * POOL_GC]
        y_s[:, lanes] = (z * pools_ref[:, lanes] * _silu(a_gate)).astype(BF16)

    hext[CONV_HALO:CONV_HALO + tm, :] = proj_ref[0, :, 2 * W_MIX:3 * W_MIX] * jax.nn.sigmoid(proj_ref[0, :, 3 * W_MIX:4 * W_MIX])
    taps = _conv_taps()
    for c0 in range(0, tm, CONV_ROWS):
        for l0 in range(0, W_MIX, LANES):
            lanes = slice(l0, l0 + LANES)
            acc = jnp.zeros((CONV_ROWS, LANES), F32) + cb_ref[:, lanes]
            for s in range(SUBLANES):
                part = jnp.zeros((CONV_ROWS + SUBLANES, LANES), F32)
                for q, k in taps[s]:
                    r0 = c0 + SUBLANES * q
                    part = part + hext[r0:r0 + CONV_ROWS + SUBLANES, lanes] * dw_ref[k:k + 1, lanes]
                acc = acc + part[s:s + CONV_ROWS]
            h1_ref[0, c0:c0 + CONV_ROWS, lanes] = acc
    h1 = h1_ref[0]
    mu = jnp.mean(h1, axis=-1, keepdims=True)
    xc = h1 - mu
    rstd = lax.rsqrt(jnp.mean(xc * xc, axis=-1, keepdims=True) + LN_EPS)
    h3 = _silu(xc * rstd * lg_ref[...] + lb_ref[...])
    ob = jnp.dot(h3.astype(BF16), pww_ref[...], preferred_element_type=F32) + pwb_ref[...]
    ob_ref[0] = ob
    y_s[:, W_MIX:] = (ob * _silu(proj_ref[0, :, 4 * W_MIX:5 * W_MIX])).astype(BF16)

    h = x_ref[0] + jnp.dot(y_s[...], wout_ref[...], preferred_element_type=F32)
    r2 = lax.rsqrt(jnp.mean(h * h, axis=-1, keepdims=True) + RMS_EPS)
    hr = h * r2
    err = hr * fg_ref[...] - tgt_ref[0]
    loss_ref[...] += jnp.sum(err * err) * (0.5 / D_MODEL)
    dout = err * (1.0 / D_MODEL)
    gfg_ref[...] += _colsum8(dout * hr)
    gy = dout * fg_ref[...]
    dh_ref[0] = r2 * (gy - hr * jnp.mean(gy * hr, axis=-1, keepdims=True))

    uext[0:POOL_HALO, :] = uext[tm:tm + POOL_HALO, :]
    hext[0:CONV_HALO, :] = hext[tm:tm + CONV_HALO, :]


def _row(v):
    return v.reshape(1, -1)


def _fwd(x, tgt, proj, pool_w, pool_b, pool_scale, conv_dw_f, conv_b, ln_g, ln_b, pw_w_b, pw_b, w_out_b, final_g):
    bl, s, _ = x.shape
    tm = TILE_M
    tok = lambda n: pl.BlockSpec((1, tm, n), lambda b, i: (b, i, 0))
    acc = lambda n: pl.BlockSpec((SUBLANES, n), lambda b, i: (0, 0))
    act = lambda n: jax.ShapeDtypeStruct((bl, s, n), F32)
    return pl.pallas_call(
        functools.partial(_fwd_body),
        name="fwd",
        grid=(bl, s // tm),
        in_specs=[tok(D_MODEL), tok(D_MODEL), tok(D_IN)] + [VMEM] * 11,
        out_specs=[tok(W_MIX), tok(W_MIX), tok(W_MIX), tok(W_MIX), tok(D_MODEL), acc(LANES), acc(D_MODEL)],
        out_shape=[act(W_MIX), act(W_MIX), act(W_MIX), jax.ShapeDtypeStruct((bl, s, W_MIX), BF16), act(D_MODEL),
                   jax.ShapeDtypeStruct((SUBLANES, LANES), F32), jax.ShapeDtypeStruct((SUBLANES, D_MODEL), F32)],
        scratch_shapes=[pltpu.VMEM((POOL_HALO + tm, W_MIX), F32), pltpu.VMEM((CONV_HALO + tm + SUBLANES, W_MIX), F32),
                        pltpu.VMEM((tm, D_MODEL), BF16)],
        compiler_params=pltpu.CompilerParams(dimension_semantics=("arbitrary", "arbitrary"), vmem_limit_bytes=VMEM_LIMIT),
    )(x, tgt, proj, pool_w, pool_b.reshape(1, W_MIX), _row(pool_scale), conv_dw_f, _row(conv_b),
      _row(ln_g), _row(ln_b), pw_w_b, _row(pw_b), w_out_b, _row(final_g))


NT_DIMS = (((1,), (1,)), ((), ()))
TN_DIMS = (((0,), (0,)), ((), ()))
V_POOL_B, V_POOL_SCALE, V_CONV_B, V_LN_G, V_LN_B, V_PW_B = range(6)


def _bwd_taps():
    taps = {s: [] for s in range(SUBLANES)}
    for k in range(CONV_WIDTH):
        e = CONV_WIDTH - 1 - k
        taps[e % SUBLANES].append((e // SUBLANES, k))
    return taps


def _bwd1_body(n_tiles, dh_ref, proj_ref, h1_ref, z_ref, ob_ref, dpool_ref, wout_ref, pww_ref, poolw_ref, pools_ref, dw_ref,
               lg_ref, lb_ref,
               dproj_ref, gwout_ref, gpww_ref, gpoolw_ref, gvec_ref, gdw_ref,
               vext, gext, gsh, y_s):
    b = pl.program_id(0)
    i = pl.program_id(1)
    tm = TILE_M
    t0 = (n_tiles - 1 - i) * tm

    @pl.when(i == 0)
    def _():
        vext[tm:, :] = jnp.zeros((POOL_HALO, W_MIX), F32)
        gext[tm:, :] = jnp.zeros((CONV_HALO, W_MIX), F32)

    @pl.when((b == 0) & (i == 0))
    def _():
        gwout_ref[...] = jnp.zeros_like(gwout_ref)
        gpww_ref[...] = jnp.zeros_like(gpww_ref)
        gpoolw_ref[...] = jnp.zeros_like(gpoolw_ref)
        gvec_ref[...] = jnp.zeros_like(gvec_ref)
        gdw_ref[...] = jnp.zeros_like(gdw_ref)

    sub = tm // BWD_SUBTILES

    def dense_phase(r0):
        rs = slice(r0, r0 + sub)
        dhb = dh_ref[0, rs, :].astype(BF16)
        dy_a = lax.dot_general(dhb, wout_ref[0:W_MIX, :], NT_DIMS, preferred_element_type=F32)
        dy_b = lax.dot_general(dhb, wout_ref[W_MIX:, :], NT_DIMS, preferred_element_type=F32)

        a_gate = proj_ref[0, rs, W_MIX:2 * W_MIX]
        sig_a = jax.nn.sigmoid(a_gate)
        silu_a = a_gate * sig_a
        z = z_ref[0, rs, :]
        out_a = z * pools_ref[...]
        y_s[rs, 0:W_MIX] = (out_a * silu_a).astype(BF16)
        d_out_a = dy_a * silu_a
        d_ag = dy_a * out_a * (sig_a * (1.0 + a_gate * (1.0 - sig_a)))
        dproj_ref[0, rs, W_MIX:2 * W_MIX] = d_ag.astype(BF16)
        gvec_ref[V_POOL_SCALE] += _colsum8(d_out_a * z)
        dz = d_out_a * pools_ref[...]
        gvec_ref[V_POOL_B] += _colsum8(dz)
        t = _row_index((sub, POOL_GC), t0 + r0)
        n_ext = sub + POOL_HALO
        for g, w in enumerate(POOL_WINDOWS):
            lanes = slice(g * POOL_GC, (g + 1) * POOL_GC)
            dzg = dz[:, lanes].astype(BF16)
            gpoolw_ref[g] += lax.dot_general(dpool_ref[0, rs, lanes], dzg, TN_DIMS, preferred_element_type=F32)
            dd = lax.dot_general(dzg, poolw_ref[g].astype(BF16), NT_DIMS, preferred_element_type=F32)
            vext[rs, lanes] = dd * (1.0 / jnp.minimum(t + 1, w).astype(F32))
            s = vext[r0:r0 + n_ext, lanes]
            sh = 1
            while sh < w:
                s = s + pltpu.roll(s, n_ext - sh, 0)
                sh *= 2
            dproj_ref[0, rs, lanes] = (s[0:sub] - dd).astype(BF16)

        b_gate = proj_ref[0, rs, 4 * W_MIX:5 * W_MIX]
        sig_b = jax.nn.sigmoid(b_gate)
        silu_b = b_gate * sig_b
        ob = ob_ref[0, rs, :]
        y_s[rs, W_MIX:] = (ob * silu_b).astype(BF16)
        d_ob = dy_b * silu_b
        dproj_ref[0, rs, 4 * W_MIX:5 * W_MIX] = (dy_b * ob * (sig_b * (1.0 + b_gate * (1.0 - sig_b)))).astype(BF16)
        gvec_ref[V_PW_B] += _colsum8(d_ob)
        gwout_ref[...] += lax.dot_general(y_s[rs, :], dhb, TN_DIMS, preferred_element_type=F32)

        h1 = h1_ref[0, rs, :]
        mu = jnp.mean(h1, axis=-1, keepdims=True)
        xc = h1 - mu
        rstd = lax.rsqrt(jnp.mean(xc * xc, axis=-1, keepdims=True) + LN_EPS)
        xhat = xc * rstd
        h2 = xhat * lg_ref[...] + lb_ref[...]
        sig2 = jax.nn.sigmoid(h2)
        d_obb = d_ob.astype(BF16)
        gpww_ref[...] += lax.dot_general((h2 * sig2).astype(BF16), d_obb, TN_DIMS, preferred_element_type=F32)
        dh2 = lax.dot_general(d_obb, pww_ref[...], NT_DIMS, preferred_element_type=F32) * (sig2 * (1.0 + h2 * (1.0 - sig2)))
        gvec_ref[V_LN_G] += _colsum8(dh2 * xhat)
        gvec_ref[V_LN_B] += _colsum8(dh2)
        dxh = dh2 * lg_ref[...]
        dh1 = rstd * (dxh - jnp.mean(dxh, axis=-1, keepdims=True) - xhat * jnp.mean(dxh * xhat, axis=-1, keepdims=True))
        gvec_ref[V_CONV_B] += _colsum8(dh1)
        gext[rs, :] = dh1

    taps = _bwd_taps()
    cr = BWD_CONV_ROWS
    win = cr + CONV_HALO - SUBLANES

    def conv_phase(r0):
        for c0 in range(r0, r0 + sub, cr):
            for l0 in range(0, W_MIX, LANES):
                lanes = slice(l0, l0 + LANES)
                b_val = proj_ref[0, c0:c0 + cr, 2 * W_MIX + l0:2 * W_MIX + l0 + LANES]
                sg = jax.nn.sigmoid(proj_ref[0, c0:c0 + cr, 3 * W_MIX + l0:3 * W_MIX + l0 + LANES])
                h0c = b_val * sg
                acc = jnp.zeros((cr, LANES), F32)
                for s in range(SUBLANES):
                    gsh[...] = gext[c0 + s:c0 + s + win, lanes]
                    for q, k in taps[s]:
                        gq = gsh[SUBLANES * q:SUBLANES * q + cr, :]
                        acc = acc + gq * dw_ref[k:k + 1, lanes]
                        gdw_ref[k, :, lanes] += _colsum8(h0c * gq)
                d_bval = acc * sg
                dproj_ref[0, c0:c0 + cr, 2 * W_MIX + l0:2 * W_MIX + l0 + LANES] = d_bval.astype(BF16)
                dproj_ref[0, c0:c0 + cr, 3 * W_MIX + l0:3 * W_MIX + l0 + LANES] = (d_bval * b_val * (1.0 - sg)).astype(BF16)

    starts = [k * sub for k in reversed(range(BWD_SUBTILES))]
    for r0 in starts:
        dense_phase(r0)
    for r0 in starts:
        conv_phase(r0)
    vext[tm:, :] = vext[0:POOL_HALO, :]
    gext[tm:, :] = gext[0:CONV_HALO, :]


def _bwd1(dh, proj, h1, z, ob, dpool, w_out_b, pw_w_b, pool_w, pool_scale, conv_dw_f, ln_g, ln_b):
    bl, s, _ = dh.shape
    tm = TILE_M
    nt = s // tm
    tok = lambda n: pl.BlockSpec((1, tm, n), lambda b, i: (b, nt - 1 - i, 0))
    res = lambda shape: pl.BlockSpec(shape, lambda b, i: (0,) * len(shape))
    out_shapes = [(D_MODEL, D_MODEL), (W_MIX, W_MIX), (len(POOL_WINDOWS), POOL_GC, POOL_GC), (6, SUBLANES, W_MIX),
                  (CONV_WIDTH, SUBLANES, W_MIX)]
    return pl.pallas_call(
        functools.partial(_bwd1_body, nt),
        name="bwd1",
        grid=(bl, nt),
        in_specs=[tok(D_MODEL), tok(D_IN), tok(W_MIX), tok(W_MIX), tok(W_MIX), tok(W_MIX)] + [VMEM] * 7,
        out_specs=[tok(D_IN)] + [res(sh) for sh in out_shapes],
        out_shape=[jax.ShapeDtypeStruct((bl, s, D_IN), BF16)] + [jax.ShapeDtypeStruct(sh, F32) for sh in out_shapes],
        scratch_shapes=[pltpu.VMEM((tm + POOL_HALO, W_MIX), F32), pltpu.VMEM((tm + CONV_HALO, W_MIX), F32),
                        pltpu.VMEM((BWD_CONV_ROWS + CONV_HALO - SUBLANES, LANES), F32), pltpu.VMEM((tm, D_MODEL), BF16)],
        compiler_params=pltpu.CompilerParams(dimension_semantics=("arbitrary", "arbitrary"), vmem_limit_bytes=VMEM_LIMIT),
    )(dh, proj, h1, z, ob, dpool, w_out_b, pw_w_b, pool_w, _row(pool_scale), conv_dw_f, _row(ln_g), _row(ln_b))


MESH = pl.DeviceIdType.MESH
W_IN_BLK = D_IN // N_DEV
W_OUT_BLK = D_MODEL // N_DEV
PW_BLK = W_MIX // N_DEV
DW_BLK = W_MIX // N_DEV
N_PEER = N_DEV - 1


SLAB = 2 * W_IN_BLK
N_SLABS = 4
F_SIB, F_X, F_Y, F_ON, F_PASS_X, F_PASS_Y, F_PASS_D = range(7)
W_IN_ONLY, OTHERS = (0,), (1, 2, 3)


def _inproj_body(n_tiles, x_ref, ng_ref, wint_ref, wout_ref, pww_ref, dw_ref,
                 proj_hbm, wt_hbm, woutf_hbm, pwwf_hbm, dwf_ref,
                 hn_s, wt_s, wout_s, pww_s, dw_blk, stage, out_sems, w_sems, send_sems, recv_sems):
    s = pl.program_id(0)
    i = pl.program_id(1)
    tm = TILE_IN
    x, y, c = lax.axis_index("x"), lax.axis_index("y"), lax.axis_index("c")
    sibling = (x, y, 1 - c)
    xn, yn, dg = (1 - x, y), (x, 1 - y), (1 - x, 1 - y)
    pick = lambda a, b: tuple(jnp.where(c == 1, p, q) for p, q in zip(a, b))
    on_from, on_to = pick(xn, yn), pick(yn, xn)

    def blocks(px, py, pc):
        k = 4 * px + 2 * py + pc
        rows = lambda ref, n: ref.at[pl.ds(pl.multiple_of(k * n, n), n), :]
        return (rows(wt_s, W_IN_BLK), rows(wout_s, W_OUT_BLK), rows(pww_s, PW_BLK), dw_blk.at[k])

    def copies(arrays, flow, block, to):
        refs = blocks(*block)
        return [pltpu.make_async_remote_copy(src_ref=refs[a], dst_ref=refs[a], send_sem=send_sems.at[a, flow],
                                             recv_sem=recv_sems.at[a, flow], device_id=to, device_id_type=MESH)
                for a in arrays]

    def start(arrays):
        for flow, to in ((F_SIB, sibling), (F_X, (*xn, c)), (F_Y, (*yn, c))):
            for cp in copies(arrays, flow, (x, y, c), to):
                cp.start()

    def direct(arrays):
        for flow, chip in ((F_X, xn), (F_Y, yn)):
            for cp in copies(arrays, flow, (*chip, c), sibling):
                cp.wait_recv()
        for cp in copies(arrays, F_ON, (*on_from, c), (*on_to, c)):
            cp.start()
        for flow, chip in ((F_PASS_X, xn), (F_PASS_Y, yn)):
            for cp in copies(arrays, flow, (*chip, c), sibling):
                cp.start()

    def diagonal(arrays):
        for cp in copies(arrays, F_ON, (*dg, c), sibling):
            cp.wait_recv()
        for cp in copies(arrays, F_PASS_D, (*dg, c), sibling):
            cp.start()

    def from_sibling(arrays, flow, chip):
        for cp in copies(arrays, flow, (*chip, 1 - c), sibling):
            cp.wait_recv()

    def finish(arrays):
        mine = (x, y, c)
        for flow, block in ((F_SIB, mine), (F_X, mine), (F_Y, mine), (F_ON, (*on_from, c)), (F_PASS_X, (*xn, c)),
                            (F_PASS_Y, (*yn, c)), (F_PASS_D, (*dg, c))):
            for cp in copies(arrays, flow, block, sibling):
                cp.wait_send()

    step = s * n_tiles + i
    par = step % 2
    first = step == 0

    @pl.when(first)
    def _():
        mine = blocks(x, y, c)
        mine[0][...] = wint_ref[...].astype(BF16)
        mine[1][...] = wout_ref[...].astype(BF16)
        mine[2][...] = pww_ref[...].astype(BF16)
        mine[3][...] = dw_ref[...]
        start(W_IN_ONLY)
        start(OTHERS)

    rows_i = pl.ds(pl.multiple_of(i * tm, tm), tm)

    @pl.when(s == 0)
    def _():
        xt = x_ref[0]
        r1 = lax.rsqrt(jnp.mean(xt * xt, axis=-1, keepdims=True) + RMS_EPS)
        hn_s[rows_i, :] = (xt * r1 * ng_ref[...]).astype(BF16)

    @pl.when(first)
    def _():
        from_sibling(W_IN_ONLY, F_SIB, (x, y))

    @pl.when((s == 1) & (i == 0))
    def _():
        direct(W_IN_ONLY)
        from_sibling(W_IN_ONLY, F_PASS_Y, yn)

    @pl.when((s == 2) & (i == 0))
    def _():
        from_sibling(W_IN_ONLY, F_PASS_X, xn)

    weights_out = [pltpu.make_async_copy(src, dst, w_sems.at[k])
                   for k, (src, dst) in enumerate(((wt_s, wt_hbm), (wout_s, woutf_hbm), (pww_s, pwwf_hbm)))]

    @pl.when((s == 3) & (i == 0))
    def _():
        diagonal(W_IN_ONLY)
        direct(OTHERS)
        from_sibling(W_IN_ONLY, F_PASS_D, dg)
        weights_out[0].start()

    tiles_per_seq = proj_hbm.shape[1] // tm

    def out_copy(parity, col0):
        dst = proj_hbm.at[i // tiles_per_seq, pl.ds(pl.multiple_of((i % tiles_per_seq) * tm, tm), tm),
                          pl.ds(pl.multiple_of(col0, LANES), SLAB)]
        return pltpu.make_async_copy(stage.at[parity], dst, out_sems.at[parity])

    @pl.when(step >= 2)
    def _():
        out_copy(par, 0).wait()

    chip = jnp.where(s < 2, 2 * x, 2 * (1 - x)) + jnp.where((s == 0) | (s == 2), y, 1 - y)
    slab = wt_s[pl.ds(pl.multiple_of(chip * SLAB, SLAB), SLAB), :]
    stage[par] = lax.dot_general(hn_s[rows_i, :], slab, NT_DIMS, preferred_element_type=F32)
    out_copy(par, chip * SLAB).start()

    @pl.when(step == N_SLABS * n_tiles - 1)
    def _():
        out_copy(par, 0).wait()
        out_copy(1 - par, 0).wait()
        from_sibling(OTHERS, F_SIB, (x, y))
        from_sibling(OTHERS, F_PASS_X, xn)
        from_sibling(OTHERS, F_PASS_Y, yn)
        diagonal(OTHERS)
        from_sibling(OTHERS, F_PASS_D, dg)
        for cp in weights_out[1:]:
            cp.start()
        finish(W_IN_ONLY)
        finish(OTHERS)
        for cp in weights_out:
            cp.wait()
        for k in range(N_DEV):
            dwf_ref[:, k * DW_BLK:(k + 1) * DW_BLK] = dw_blk[k]


def _inproj(x, norm_g, w_in_t, w_out, pw_w, conv_dw):
    bl, s, _ = x.shape
    tm = TILE_IN
    nt = s // tm
    n_tiles = bl * nt

    def x_map(phase, i):
        i = jnp.where(phase == 0, i, n_tiles - 1)
        return (i // nt, i % nt, 0)

    dma = pltpu.SemaphoreType.DMA
    return pl.pallas_call(
        functools.partial(_inproj_body, n_tiles),
        name="inproj",
        grid=(N_SLABS, n_tiles),
        in_specs=[pl.BlockSpec((1, tm, D_MODEL), x_map)] + [VMEM] * 5,
        out_specs=[pl.BlockSpec(memory_space=pl.ANY)] * 4 + [VMEM],
        out_shape=[jax.ShapeDtypeStruct((bl, s, D_IN), F32), jax.ShapeDtypeStruct((D_IN, D_MODEL), BF16),
                   jax.ShapeDtypeStruct((D_MODEL, D_MODEL), BF16), jax.ShapeDtypeStruct((W_MIX, W_MIX), BF16),
                   jax.ShapeDtypeStruct((CONV_WIDTH, W_MIX), F32)],
        scratch_shapes=[pltpu.VMEM((n_tiles * tm, D_MODEL), BF16), pltpu.VMEM((D_IN, D_MODEL), BF16),
                        pltpu.VMEM((D_MODEL, D_MODEL), BF16), pltpu.VMEM((W_MIX, W_MIX), BF16),
                        pltpu.VMEM((N_DEV, CONV_WIDTH, DW_BLK), F32), pltpu.VMEM((2, tm, SLAB), F32),
                        dma((2,)), dma((3,)), dma((4, 7)), dma((4, 7))],
        compiler_params=pltpu.CompilerParams(dimension_semantics=("arbitrary", "arbitrary"), vmem_limit_bytes=VMEM_LIMIT),
    )(x, _row(norm_g), w_in_t, w_out, pw_w, conv_dw)


B_PLAIN, B_ON, B_MERGED = range(3)


def _routes():
    x, y, c = lax.axis_index("x"), lax.axis_index("y"), lax.axis_index("c")
    xn, yn = (1 - x, y), (x, 1 - y)
    pick = lambda a, b: tuple(jnp.where(c == 1, p, q) for p, q in zip(a, b))
    return pick(xn, yn), pick(yn, xn), (1 - x, 1 - y), (x, y), c


def _chip_index(chip):
    return 2 * chip[0] + chip[1]


def _copy(src, dst, send_sems, recv_sems, a, k, to):
    return pltpu.make_async_remote_copy(src_ref=src, dst_ref=dst, send_sem=send_sems.at[a, k], recv_sem=recv_sems.at[a, k],
                                        device_id=to, device_id_type=MESH)


def _rsb_send(plain, on, rb_plain, rb_on, send_b, recv_b):
    first, _, _, _, c = _routes()
    for k, srcs, dsts in ((B_ON, on, rb_on), (B_PLAIN, plain, rb_plain)):
        for a, (s, d) in enumerate(zip(srcs, dsts)):
            _copy(s, d, send_b, recv_b, a, k, (*first, c)).start()


def _rsb_merge(second_sums, merged, rb_on, rb_merged, send_b, recv_b):
    _, second, _, _, c = _routes()
    for a, (own, m, got, dst) in enumerate(zip(second_sums, merged, rb_on, rb_merged)):
        _copy(got, got, send_b, recv_b, a, B_ON, (*second, c)).wait_recv()
        m[...] = (own[...] + got[...].astype(F32)).astype(BF16)
        _copy(m, dst, send_b, recv_b, a, B_MERGED, (*second, c)).start()


def _rsb_total(own_sums, plain, on, merged, rb_plain, rb_merged, send_b, recv_b):
    _, second, _, _, c = _routes()
    out = []
    for a, (own, p, g) in enumerate(zip(own_sums, rb_plain, rb_merged)):
        _copy(p, p, send_b, recv_b, a, B_PLAIN, (*second, c)).wait_recv()
        _copy(g, g, send_b, recv_b, a, B_MERGED, (*second, c)).wait_recv()
        out.append(own[...] + p[...].astype(F32) + g[...].astype(F32))
    for k, srcs in ((B_PLAIN, plain), (B_ON, on), (B_MERGED, merged)):
        for a, s in enumerate(srcs):
            _copy(s, s, send_b, recv_b, a, k, (*second, c)).wait_send()
    return out


R_LOSS, R_FINAL_G, R_POOL, R_CONV, R_LN = range(5)
VEC_ROWS = 16
DW_ROWS = 32
A_PART, A_X, A_Y, A_ON, A_HALF = range(5)
HALF_DTYPES = (F32, BF16, BF16)
STEP_SUMS, STEP_MERGE, STEP_HALF = 1, 3, 5


def _bwd2a_body(n_steps, x_ref, dproj_ref, ng_ref, gwout_hbm, gpww_hbm, loss_ref, gfg_ref, gvec_ref, gdw_ref, gpoolw_ref,
                win_own_ref, win_staged_ref, owout_ref, opww_ref, ovec_ref, odw_ref, opoolw_ref,
                acc, st_win, a_win, own_wout, own_pww, a_wout, a_pww, st_wout, st_pww, rb_wout, rb_pww, part_vec, part_dw,
                part_pw, sib_vec, sib_dw, sib_pw, res_vec, res_dw, res_pw, half_vec, half_dw, half_pw,
                own_sems, send_a, recv_a, send_b, recv_b, send_r, recv_r, send_w, recv_w):
    step = pl.program_id(0) * pl.num_programs(1) + pl.program_id(1)
    x, y, c = lax.axis_index("x"), lax.axis_index("y"), lax.axis_index("c")
    sibling = (x, y, 1 - c)
    xn, yn = (1 - x, y), (x, 1 - y)
    first, second, diag, own, _ = _routes()
    rows = lambda ref, k, n: ref.at[pl.ds(pl.multiple_of(k * n, n), n), :]

    blocks = lambda k: (rows(gwout_hbm, k, W_OUT_BLK), rows(gpww_hbm, k, PW_BLK))
    mine = lambda chip: (own_wout.at[chip], own_pww.at[chip])
    mine_copies = lambda chip: [pltpu.make_async_copy(s, d, own_sems.at[a, chip])
                                for a, (s, d) in enumerate(zip(blocks(2 * chip + c), mine(chip)))]
    landed = lambda chip: (a_wout.at[chip], a_pww.at[chip])
    staged = lambda k: (st_wout.at[k], st_pww.at[k])
    arrived = lambda k: (rb_wout.at[k], rb_pww.at[k])

    parts, sibs, halves = (part_vec, part_dw, part_pw), (sib_vec, sib_dw, sib_pw), (half_vec, half_dw, half_pw)
    results = (res_vec, res_dw, res_pw)
    half_rows = lambda ref: pl.ds(pl.multiple_of(c * (ref.shape[0] // 2), SUBLANES), ref.shape[0] // 2)
    slot = lambda chip: tuple(h.at[_chip_index(chip)] for h in halves)

    @pl.when(step == 0)
    def _():
        acc[...] = jnp.zeros_like(acc)
        sum8 = lambda v: jnp.sum(v, axis=0, keepdims=True)
        part_vec[...] = jnp.zeros((VEC_ROWS, D_MODEL), F32)
        part_vec[R_LOSS:R_LOSS + 1, :] = jnp.broadcast_to(loss_ref[0:1, 0:1], (1, D_MODEL))
        part_vec[R_FINAL_G:R_FINAL_G + 1, :] = sum8(gfg_ref[...])
        for row, (lo, hi) in ((R_POOL, (V_POOL_B, V_POOL_SCALE)), (R_CONV, (V_CONV_B, V_LN_G)), (R_LN, (V_LN_B, V_PW_B))):
            part_vec[row:row + 1, 0:W_MIX] = sum8(gvec_ref[lo])
            part_vec[row:row + 1, W_MIX:] = sum8(gvec_ref[hi])
        part_dw[0:CONV_WIDTH, :] = jnp.sum(gdw_ref[...], axis=1)
        part_dw[CONV_WIDTH:, :] = jnp.zeros((DW_ROWS - CONV_WIDTH, W_MIX), F32)
        part_pw[...] = gpoolw_ref[...].reshape(len(POOL_WINDOWS) * POOL_GC, POOL_GC)
        for a, (p, s) in enumerate(zip(parts, sibs)):
            _copy(p, s, send_r, recv_r, a, A_PART, sibling).start()
        for chip in range(4):
            for a, (s, d) in enumerate(zip(blocks(2 * chip + (1 - c)), landed(chip))):
                _copy(s, d, send_a, recv_a, a, chip, sibling).start()
            for cp in mine_copies(chip):
                cp.start()

    x_t = x_ref[0]
    r1 = lax.rsqrt(jnp.mean(x_t * x_t, axis=-1, keepdims=True) + RMS_EPS)
    hn = (x_t * r1 * ng_ref[...]).astype(BF16)
    acc[...] += lax.dot_general(dproj_ref[0], hn, TN_DIMS, preferred_element_type=F32)

    @pl.when(step == STEP_SUMS)
    def _():
        for a, (p, s, h) in enumerate(zip(parts, sibs, slot(own))):
            _copy(p, s, send_r, recv_r, a, A_PART, sibling).wait_recv()
            h[...] = (p[half_rows(p), :] + s[half_rows(s), :]).astype(h.dtype)
            for k, to in ((A_X, xn), (A_Y, yn)):
                _copy(h, h, send_r, recv_r, a, k, (*to, c)).start()
        for chip in range(4):
            for cp in mine_copies(chip):
                cp.wait()
            for a, (m, l) in enumerate(zip(mine(chip), landed(chip))):
                _copy(l, l, send_a, recv_a, a, chip, sibling).wait_recv()
                l[...] = m[...] + l[...]
        for k, chip in ((B_ON, diag), (B_PLAIN, first)):
            for s, l in zip(staged(k), landed(_chip_index(chip))):
                s[...] = l[...].astype(BF16)
        _rsb_send(staged(B_PLAIN), staged(B_ON), arrived(B_PLAIN), arrived(B_ON), send_b, recv_b)

    @pl.when(step == STEP_MERGE)
    def _():
        _rsb_merge(landed(_chip_index(second)), staged(B_MERGED), arrived(B_ON), arrived(B_MERGED), send_b, recv_b)
        for a in range(3):
            for k, chip in ((A_X, xn), (A_Y, yn)):
                h = slot(chip)[a]
                _copy(h, h, send_r, recv_r, a, k, sibling).wait_recv()
            h = slot(first)[a]
            _copy(h, h, send_r, recv_r, a, A_ON, (*second, c)).start()

    @pl.when(step == STEP_HALF)
    def _():
        for a, (r, hs) in enumerate(zip(results, halves)):
            h = slot(diag)[a]
            _copy(h, h, send_r, recv_r, a, A_ON, sibling).wait_recv()
            mine = r.at[half_rows(r), :]
            chip_sums = [hs[k].astype(F32) for k in range(4)]
            mine[...] = (chip_sums[0] + chip_sums[1]) + (chip_sums[2] + chip_sums[3])
            _copy(mine, mine, send_r, recv_r, a, A_HALF, sibling).start()

    @pl.when(step == n_steps - 1)
    def _():
        block = lambda k: rows(acc, k, W_IN_BLK)
        w_copy = lambda chip: _copy(st_win.at[chip], a_win.at[chip], send_w, recv_w, 0, chip, sibling)
        for chip in range(4):
            st_win[chip] = block(2 * chip + (1 - c))[...].astype(BF16)
            w_copy(chip).start()

        owout_ref[...], opww_ref[...] = _rsb_total(landed(_chip_index(own)), staged(B_PLAIN), staged(B_ON), staged(B_MERGED),
                                                   arrived(B_PLAIN), arrived(B_MERGED), send_b, recv_b)
        for a, (r, p, s) in enumerate(zip(results, parts, sibs)):
            theirs = r.at[pl.ds(pl.multiple_of((1 - c) * (r.shape[0] // 2), SUBLANES), r.shape[0] // 2), :]
            _copy(theirs, theirs, send_r, recv_r, a, A_HALF, sibling).wait_recv()
            _copy(p, s, send_r, recv_r, a, A_PART, sibling).wait_send()
            for k, h in ((A_X, slot(own)[a]), (A_Y, slot(own)[a]), (A_ON, slot(first)[a])):
                _copy(h, h, send_r, recv_r, a, k, sibling).wait_send()
            mine = r.at[half_rows(r), :]
            _copy(mine, mine, send_r, recv_r, a, A_HALF, sibling).wait_send()
        for chip in range(4):
            for a, (s, d) in enumerate(zip(blocks(2 * chip + (1 - c)), landed(chip))):
                _copy(s, d, send_a, recv_a, a, chip, sibling).wait_send()
        ovec_ref[...] = res_vec[...]
        opoolw_ref[...] = res_pw[...]
        for k in range(N_DEV):
            @pl.when(4 * x + 2 * y + c == k)
            def _():
                odw_ref[...] = res_dw[0:CONV_WIDTH, k * DW_BLK:(k + 1) * DW_BLK]

        for chip in range(4):
            w_copy(chip).wait()
        chip_sum = lambda chip: block(2 * _chip_index(chip) + c)[...] + a_win[_chip_index(chip)].astype(F32)
        win_own_ref[...] = chip_sum(own)
        for j, chip in enumerate((xn, yn, diag)):
            win_staged_ref[j] = chip_sum(chip).astype(BF16)


def _bwd2a(x, dproj, norm_g, gwout, gpww, loss8, gfg8, gvec, gdw8, gpoolw):
    bl, s, _ = x.shape
    tm = TILE_M
    nt = s // tm
    tok = lambda n: pl.BlockSpec((1, tm, n), lambda b, i: (b, i, 0))
    n_pw = len(POOL_WINDOWS) * POOL_GC
    win = (W_IN_BLK, D_MODEL)
    blk = ((W_OUT_BLK, D_MODEL), (PW_BLK, W_MIX))
    rep = ((VEC_ROWS, D_MODEL), (DW_ROWS, W_MIX), (n_pw, POOL_GC))
    half = lambda sh: (sh[0] // 2,) + sh[1:]
    dma = pltpu.SemaphoreType.DMA
    assert bl * nt > STEP_HALF + 1, "the exchanges' phases need their grid steps"
    return pl.pallas_call(
        functools.partial(_bwd2a_body, bl * nt),
        name="bwd2a",
        grid=(bl, nt),
        in_specs=[tok(D_MODEL), tok(D_IN), VMEM] + [pl.BlockSpec(memory_space=pl.ANY)] * 2 + [VMEM] * 5,
        out_specs=[VMEM] * 7,
        out_shape=[jax.ShapeDtypeStruct(win, F32), jax.ShapeDtypeStruct((3,) + win, BF16)]
        + [jax.ShapeDtypeStruct(sh, F32) for sh in blk + (rep[0], (CONV_WIDTH, DW_BLK), rep[2])],
        scratch_shapes=[pltpu.VMEM((D_IN, D_MODEL), F32), pltpu.VMEM((4,) + win, BF16), pltpu.VMEM((4,) + win, BF16)]
        + [pltpu.VMEM((4,) + sh, F32) for sh in blk] * 2
        + [pltpu.VMEM((3,) + sh, BF16) for sh in blk] * 2
        + [pltpu.VMEM(sh, F32) for sh in rep] * 3
        + [pltpu.VMEM((4,) + half(sh), dt) for sh, dt in zip(rep, HALF_DTYPES)]
        + [dma((2, 4)), dma((2, 4)), dma((2, 4)), dma((2, 3)), dma((2, 3)), dma((3, 5)), dma((3, 5)), dma((1, 4)), dma((1, 4))],
        compiler_params=pltpu.CompilerParams(dimension_semantics=("arbitrary", "arbitrary"), vmem_limit_bytes=VMEM_LIMIT),
    )(x, dproj, _row(norm_g), gwout, gpww, loss8, gfg8, gvec, gdw8, gpoolw)


HBM = pl.BlockSpec(memory_space=pltpu.HBM)
SEM = pl.BlockSpec(memory_space=pltpu.SEMAPHORE)
EFFECT = pltpu.SideEffectType.DATAFLOW_SIDE_EFFECTING


def _owner_copies(src_ref, land_ref, send_sems, recv_sems):
    x, y, c = lax.axis_index("x"), lax.axis_index("y"), lax.axis_index("c")
    owners = ((1 - x, y, c), (x, 1 - y, c), (1 - x, 1 - y, c))
    return [pltpu.make_async_remote_copy(src_ref=src_ref.at[j], dst_ref=land_ref.at[j], send_sem=send_sems.at[j],
                                         recv_sem=recv_sems.at[j], device_id=owners[j], device_id_type=MESH)
            for j in range(3)]


def _rs_start(staged):
    def body(src_ref, land_ref, send_sems, recv_sems, src_thru, land_thru, token):
        for cp in _owner_copies(src_ref, land_ref, send_sems, recv_sems):
            cp.start()
        token[...] = jnp.zeros_like(token)

    dma = pltpu.SemaphoreType.DMA
    return pl.pallas_call(
        body, name="rs_start",
        out_shape=(dma((3,)), dma((3,)), pltpu.HBM(staged.shape, staged.dtype), pltpu.HBM(staged.shape, staged.dtype),
                   jax.ShapeDtypeStruct((SUBLANES, LANES), F32)),
        in_specs=(HBM, HBM), out_specs=(SEM, SEM, HBM, HBM, VMEM), input_output_aliases={0: 2, 1: 3},
        compiler_params=pltpu.CompilerParams(has_side_effects=EFFECT),
    )(pltpu.with_memory_space_constraint(staged, pltpu.HBM),
      pltpu.with_memory_space_constraint(lax.empty(staged.shape, staged.dtype), pltpu.HBM))


def _rs_wait(send_sems, recv_sems, src_thru, land_thru, after):
    def body(src_ref, land_ref, send_sems, recv_sems, after_ref, src_dead, got_ref):
        for cp in _owner_copies(src_ref, land_ref, send_sems, recv_sems):
            cp.wait_send()
            cp.wait_recv()

    return pl.pallas_call(
        body, name="rs_wait",
        out_shape=(pltpu.HBM(src_thru.shape, src_thru.dtype), pltpu.HBM(land_thru.shape, land_thru.dtype)),
        in_specs=(HBM, HBM, SEM, SEM, pl.BlockSpec(memory_space=pl.ANY)), out_specs=(HBM, HBM),
        input_output_aliases={0: 0, 1: 1},
        compiler_params=pltpu.CompilerParams(has_side_effects=EFFECT),
    )(src_thru, land_thru, send_sems, recv_sems, after)[1]


def _bwd2b_body(x_ref, dh_ref, dproj_ref, ng_ref, wint_ref, token_ref, gx_ref, gng_ref):
    @pl.when((pl.program_id(0) == 0) & (pl.program_id(1) == 0))
    def _():
        gng_ref[...] = jnp.zeros_like(gng_ref) + token_ref[0:1, 0:1]

    x_t = x_ref[0]
    r1 = lax.rsqrt(jnp.mean(x_t * x_t, axis=-1, keepdims=True) + RMS_EPS)
    xr = x_t * r1
    dhn = jnp.dot(dproj_ref[0], wint_ref[...], preferred_element_type=F32)
    gng_ref[...] += _colsum8(dhn * xr)
    gy = dhn * ng_ref[...]
    gx_ref[0] = dh_ref[0] + r1 * (gy - xr * jnp.mean(gy * xr, axis=-1, keepdims=True))


def _bwd2b(x, dh, dproj, norm_g, w_in_t, token):
    bl, s, _ = x.shape
    tm = TILE_M
    tok = lambda n: pl.BlockSpec((1, tm, n), lambda b, i: (b, i, 0))
    return pl.pallas_call(
        functools.partial(_bwd2b_body),
        name="bwd2b",
        grid=(bl, s // tm),
        in_specs=[tok(D_MODEL), tok(D_MODEL), tok(D_IN), VMEM, VMEM, VMEM],
        out_specs=[tok(D_MODEL), pl.BlockSpec((SUBLANES, D_MODEL), lambda b, i: (0, 0))],
        out_shape=[jax.ShapeDtypeStruct((bl, s, D_MODEL), F32), jax.ShapeDtypeStruct((SUBLANES, D_MODEL), F32)],
        compiler_params=pltpu.CompilerParams(dimension_semantics=("arbitrary", "arbitrary"), vmem_limit_bytes=VMEM_LIMIT),
    )(x, dh, dproj, _row(norm_g), w_in_t, token)


def _partial_copies(part_ref, land_ref, send_sems, recv_sems):
    x, y, c = lax.axis_index("x"), lax.axis_index("y"), lax.axis_index("c")
    other_chips = [(1 - x, y), (x, 1 - y), (1 - x, 1 - y)]
    peers = [(x, y, 1 - c)] + [(*chip, c) for chip in other_chips] + [(*chip, 1 - c) for chip in other_chips]
    return [pltpu.make_async_remote_copy(src_ref=part_ref, dst_ref=land_ref.at[4 * x + 2 * y + c], send_sem=send_sems.at[r],
                                         recv_sem=recv_sems.at[r], device_id=peers[r], device_id_type=MESH)
            for r in range(N_PEER)]


def _partials_start(part):
    def body(part_ref, land_ref, send_sems, recv_sems, part_thru, land_thru):
        for cp in _partial_copies(part_ref, land_ref, send_sems, recv_sems):
            cp.start()

    dma = pltpu.SemaphoreType.DMA
    land = (N_DEV,) + part.shape
    return pl.pallas_call(
        body, name="partials_start",
        out_shape=(dma((N_PEER,)), dma((N_PEER,)), pltpu.HBM(part.shape, part.dtype), pltpu.HBM(land, part.dtype)),
        in_specs=(HBM, HBM), out_specs=(SEM, SEM, HBM, HBM), input_output_aliases={0: 2, 1: 3},
        compiler_params=pltpu.CompilerParams(has_side_effects=EFFECT),
    )(pltpu.with_memory_space_constraint(part, pltpu.HBM),
      pltpu.with_memory_space_constraint(lax.empty(land, part.dtype), pltpu.HBM))


def _partials_wait(send_sems, recv_sems, part_thru, land_thru, after):
    def body(part_ref, land_ref, send_sems, recv_sems, after_ref, part_out, land_out):
        for cp in _partial_copies(part_ref, land_ref, send_sems, recv_sems):
            cp.wait_send()
            cp.wait_recv()

    return pl.pallas_call(
        body, name="partials_wait",
        out_shape=(pltpu.HBM(part_thru.shape, part_thru.dtype), pltpu.HBM(land_thru.shape, land_thru.dtype)),
        in_specs=(HBM, HBM, SEM, SEM, pl.BlockSpec(memory_space=pl.ANY)), out_specs=(HBM, HBM),
        input_output_aliases={0: 0, 1: 1},
        compiler_params=pltpu.CompilerParams(has_side_effects=EFFECT),
    )(part_thru, land_thru, send_sems, recv_sems, after)


ADAM_LR = 0.001
ADAM_B1 = 0.9
ADAM_B2 = 0.999
ADAM_EPS = 1e-08
ADAM_WD = 0.01
ADAM_STEP = 10


PARAMS = ("norm_g", "w_in", "pool_w", "pool_b", "pool_scale", "conv_dw", "conv_b", "ln_g", "ln_b", "pw_w", "pw_b", "w_out",
          "final_g")


LATE_PARAMS = ("norm_g",)
MAIN_PARAMS = tuple(p for p in PARAMS if p not in LATE_PARAMS)


def _adamw_refs(params, n_sources, refs):
    n = len(params)
    ws, ms, vs = (dict(zip(params, refs[k * n:(k + 1) * n])) for k in range(3))
    outs = refs[3 * n + n_sources:]
    gs, ds, m2s, v2s = (dict(zip(params, outs[k * n:(k + 1) * n])) for k in range(4))

    def update(name, g, at=slice(None)):
        m = ADAM_B1 * ms[name][at] + (1.0 - ADAM_B1) * g
        v = ADAM_B2 * vs[name][at] + (1.0 - ADAM_B2) * (g * g)
        m_hat = m / (1.0 - ADAM_B1 ** ADAM_STEP)
        v_hat = v / (1.0 - ADAM_B2 ** ADAM_STEP)
        gs[name][at] = g
        ds[name][at] = -ADAM_LR * (m_hat / (jnp.sqrt(v_hat) + ADAM_EPS) + ADAM_WD * ws[name][at])
        m2s[name][at] = m
        v2s[name][at] = v

    return update, refs[3 * n:3 * n + n_sources], outs[4 * n:]


def _adamw_late_body(*refs):
    update, (gng_ref, gng_others_ref), _ = _adamw_refs(LATE_PARAMS, 2, refs)
    me = 4 * lax.axis_index("x") + 2 * lax.axis_index("y") + lax.axis_index("c")
    g_norm = jnp.zeros(gng_ref.shape, F32)
    for k in range(N_DEV):
        g_norm = g_norm + jnp.where(me == k, gng_ref[...], gng_others_ref[k])
    update("norm_g", jnp.sum(g_norm, axis=0, keepdims=True))


def _adamw_body(*refs):
    update, sources, (loss_ref,) = _adamw_refs(MAIN_PARAMS, 7, refs)
    gwin_ref, gwin_others_ref, gwout_ref, gpww_ref, vec_ref, gdw_ref, gpoolw_ref = sources
    ws = dict(zip(MAIN_PARAMS, refs))
    row = lambda r, lo, hi: vec_ref[r:r + 1, lo:hi]
    g_w_in = gwin_ref[...]
    for j in range(3):
        g_w_in = g_w_in + gwin_others_ref[j].astype(F32)
    update("w_in", g_w_in)

    @pl.when(pl.program_id(0) == 0)
    def _():
        loss_ref[...] = row(R_LOSS, 0, 1)
        update("pool_w", gpoolw_ref[...].reshape(ws["pool_w"].shape))
        for g in range(len(POOL_WINDOWS)):
            update("pool_b", row(R_POOL, g * POOL_GC, (g + 1) * POOL_GC), at=slice(g, g + 1))
        update("pool_scale", row(R_POOL, W_MIX, 2 * W_MIX))
        update("conv_dw", gdw_ref[...])
        update("conv_b", row(R_CONV, 0, W_MIX))
        update("ln_g", row(R_CONV, W_MIX, 2 * W_MIX))
        update("ln_b", row(R_LN, 0, W_MIX))
        update("pw_w", gpww_ref[...])
        update("pw_b", row(R_LN, W_MIX, 2 * W_MIX))
        update("w_out", gwout_ref[...])
        update("final_g", row(R_FINAL_G, 0, D_MODEL))


W_IN_CHUNKS = 5


def _adamw(body, name, params, ws, ms, vs, grad_sources, extra_out_shapes=(), chunked=None, n_chunks=1):
    n = len(params)
    flat = lambda a: a.reshape(1, -1) if a.ndim == 1 else a
    args = [flat(d[p]) for d in (ws, ms, vs) for p in params]

    def chunk(shape):
        rows = shape[-2] // n_chunks
        lead = len(shape) - 2
        return pl.BlockSpec(shape[:-2] + (rows, shape[-1]), lambda i: (0,) * lead + (i, 0))

    spec = lambda p, a: chunk(a.shape) if p == chunked else VMEM
    param_specs = [spec(p, a) for p, a in zip(params, args[:n])]
    source_specs = [chunk(g.shape) if chunked and k < 2 else VMEM for k, g in enumerate(grad_sources)]
    outs = pl.pallas_call(
        body,
        name=name,
        grid=(n_chunks,),
        in_specs=param_specs * 3 + source_specs,
        out_specs=param_specs * 4 + [VMEM] * len(extra_out_shapes),
        out_shape=[jax.ShapeDtypeStruct(a.shape, F32) for a in args[:n]] * 4 + list(extra_out_shapes),
        compiler_params=pltpu.CompilerParams(dimension_semantics=("arbitrary",), vmem_limit_bytes=VMEM_LIMIT),
    )(*args, *grad_sources)
    groups = tuple({p: outs[k * n + j].reshape(ws[p].shape) for j, p in enumerate(params)} for k in range(4))
    return groups, outs[4 * n:]


def kernel(x, norm_g, w_in, pool_w, pool_b, pool_scale, conv_dw, conv_b, ln_g, ln_b, pw_w, pw_b, w_out, final_g, loss_target, m_norm_g, m_w_in, m_pool_w, m_pool_b, m_pool_scale, m_conv_dw, m_conv_b, m_ln_g, m_ln_b, m_pw_w, m_pw_b, m_w_out, m_final_g, v_norm_g, v_w_in, v_pool_w, v_pool_b, v_pool_scale, v_conv_dw, v_conv_b, v_ln_g, v_ln_b, v_pw_w, v_pw_b, v_w_out, v_final_g):
    proj, w_in_t, w_out_b, pw_w_b, conv_dw_f = _inproj(x, norm_g, w_in.T, w_out, pw_w, conv_dw)
    h1, z, ob, dpool, dh, loss8, gfg8 = _fwd(x, loss_target, proj, pool_w, pool_b, pool_scale, conv_dw_f,
                                             conv_b, ln_g, ln_b, pw_w_b, pw_b, w_out_b, final_g)
    dproj, gwout, gpww, gpoolw, gvec, gdw8 = _bwd1(dh, proj, h1, z, ob, dpool, w_out_b, pw_w_b, pool_w, pool_scale,
                                                   conv_dw_f, ln_g, ln_b)
    win_own, win_staged, g_w_out, g_pw_w, vec, g_conv_dw, poolw_sum = _bwd2a(x, dproj, norm_g, gwout, gpww, loss8, gfg8, gvec,
                                                                             gdw8, gpoolw)
    send_sems, recv_sems, staged_thru, land_thru, token = _rs_start(win_staged)
    grad_x, gng8 = _bwd2b(x, dh, dproj, norm_g, w_in_t, token)
    ng_send, ng_recv, gng8, gng_land = _partials_start(gng8)
    win_others = _rs_wait(send_sems, recv_sems, staged_thru, land_thru, gng8)

    weights = dict(norm_g=norm_g, w_in=w_in.T, pool_w=pool_w, pool_b=pool_b, pool_scale=pool_scale, conv_dw=conv_dw, conv_b=conv_b,
                   ln_g=ln_g, ln_b=ln_b, pw_w=pw_w, pw_b=pw_b, w_out=w_out, final_g=final_g)
    m_in = dict(norm_g=m_norm_g, w_in=m_w_in.T, pool_w=m_pool_w, pool_b=m_pool_b, pool_scale=m_pool_scale, conv_dw=m_conv_dw,
                conv_b=m_conv_b, ln_g=m_ln_g, ln_b=m_ln_b, pw_w=m_pw_w, pw_b=m_pw_b, w_out=m_w_out, final_g=m_final_g)
    v_in = dict(norm_g=v_norm_g, w_in=v_w_in.T, pool_w=v_pool_w, pool_b=v_pool_b, pool_scale=v_pool_scale, conv_dw=v_conv_dw,
                conv_b=v_conv_b, ln_g=v_ln_g, ln_b=v_ln_b, pw_w=v_pw_w, pw_b=v_pw_b, w_out=v_w_out, final_g=v_final_g)
    main, (loss,) = _adamw(functools.partial(_adamw_body), "adamw", MAIN_PARAMS, weights, m_in, v_in,
                           (win_own, win_others, g_w_out, g_pw_w, vec, g_conv_dw, poolw_sum), [jax.ShapeDtypeStruct((1, 1), F32)],
                           chunked="w_in", n_chunks=W_IN_CHUNKS)
    gng8, gng_others = _partials_wait(ng_send, ng_recv, gng8, gng_land, main[0]["w_out"])
    late, _ = _adamw(functools.partial(_adamw_late_body), "adamw_norm_g", LATE_PARAMS, weights, m_in, v_in, (gng8, gng_others))
    groups = [[(late[k] if p in LATE_PARAMS else main[k])[p] for p in PARAMS] for k in range(4)]
    k_in = PARAMS.index("w_in")
    for group in groups:
        group[k_in] = group[k_in].T
    grads, delta, new_m, new_v = groups
    return (loss.reshape(()), grad_x, *grads, *delta, *new_m, *new_v)
```

```python
import functools

import jax
import jax.numpy as jnp
from jax import lax
from jax.experimental import pallas as pl
from jax.experimental.pallas import tpu as pltpu

F32 = jnp.float32
BF16 = jnp.bfloat16

D_MODEL = 1024
W_MIX = 512
D_IN = 5 * W_MIX
POOL_WINDOWS = (2, 4, 8, 16)
POOL_GC = 128
CONV_WIDTH = 31
RMS_EPS = 1e-6
LN_EPS = 1e-5
N_DEV = 8

LANES = 128
SUBLANES = 8
TILE_M = 512
TILE_IN = 1024
POOL_HALO = 16
CONV_HALO = 32
CONV_ROWS = 64
BWD_CONV_ROWS = 64
BWD_SUBTILES = 2
VMEM_LIMIT = 56 * 1024 * 1024

VMEM = pl.BlockSpec(memory_space=pltpu.VMEM)


def _silu(v):
    return v * jax.nn.sigmoid(v)


def _colsum8(v):
    m, n = v.shape
    return jnp.sum(v.reshape(m // SUBLANES, SUBLANES, n), axis=0)


def _row_index(shape, t0):
    return lax.broadcasted_iota(jnp.int32, shape, 0) + t0


def _conv_taps():
    taps = {s: [] for s in range(SUBLANES)}
    for k in range(CONV_WIDTH):
        e = k + CONV_HALO - (CONV_WIDTH - 1)
        taps[e % SUBLANES].append((e // SUBLANES, k))
    return taps


def _fwd_body(x_ref, tgt_ref, proj_ref, poolw_ref, poolb_ref, pools_ref, dw_ref, cb_ref, lg_ref, lb_ref,
              pww_ref, pwb_ref, wout_ref, fg_ref,
              h1_ref, z_ref, ob_ref, dpool_ref, dh_ref, loss_ref, gfg_ref,
              uext, hext, hsh, y_s):
    b = pl.program_id(0)
    i = pl.program_id(1)
    tm = TILE_M

    @pl.when(i == 0)
    def _():
        uext[0:POOL_HALO, :] = jnp.zeros((POOL_HALO, W_MIX), F32)
        hext[0:CONV_HALO, :] = jnp.zeros((CONV_HALO, W_MIX), F32)

    @pl.when((b == 0) & (i == 0))
    def _():
        loss_ref[...] = jnp.zeros_like(loss_ref)
        gfg_ref[...] = jnp.zeros_like(gfg_ref)

    uext[POOL_HALO:, :] = proj_ref[0, :, 0:W_MIX]
    t = _row_index((tm, POOL_GC), i * tm)
    for g, w in enumerate(POOL_WINDOWS):
        lanes = slice(g * POOL_GC, (g + 1) * POOL_GC)
        e = uext[:, lanes]
        s = e
        sh = 1
        while sh < w:
            s = s + pltpu.roll(s, sh, 0)
            sh *= 2
        inv_cnt = 1.0 / jnp.minimum(t + 1, w).astype(F32)
        d = (s[POOL_HALO:] * inv_cnt - e[POOL_HALO:]).astype(BF16)
        dpool_ref[0, :, lanes] = d
        z = jnp.dot(d, poolw_ref[g].astype(BF16), preferred_element_type=F32) + poolb_ref[:, lanes]
        z_ref[0, :, lanes] = z
        a_gate = proj_ref[0, :, W_MIX + g * POOL_GC:W_MIX + (g + 1) * POOL_GC]
        y_s[:, lanes] = (z * pools_ref[:, lanes] * _silu(a_gate)).astype(BF16)

    hext[CONV_HALO:CONV_HALO + tm, :] = proj_ref[0, :, 2 * W_MIX:3 * W_MIX] * jax.nn.sigmoid(proj_ref[0, :, 3 * W_MIX:4 * W_MIX])
    taps = _conv_taps()
    win = hsh.shape[1]

    def lane_tile(j, carry):
        lanes = pl.ds(pl.multiple_of(j * LANES, LANES), LANES)
        for c0 in range(0, tm, CONV_ROWS):
            acc = jnp.zeros((CONV_ROWS, LANES), F32) + cb_ref[:, lanes]
            for q, k in taps[0]:
                r0 = c0 + SUBLANES * q
                acc = acc + hext[r0:r0 + CONV_ROWS, lanes] * dw_ref[k:k + 1, lanes]
            for s in range(1, SUBLANES):
                shifted = hsh.at[s % 2]
                shifted[...] = hext[c0 + s:c0 + s + win, lanes]
                for q, k in taps[s]:
                    acc = acc + shifted[SUBLANES * q:SUBLANES * q + CONV_ROWS, :] * dw_ref[k:k + 1, lanes]
            h1_ref[0, c0:c0 + CONV_ROWS, lanes] = acc
        return carry

    lax.fori_loop(0, W_MIX // LANES, lane_tile, 0)
    h1 = h1_ref[0]
    mu = jnp.mean(h1, axis=-1, keepdims=True)
    xc = h1 - mu
    rstd = lax.rsqrt(jnp.mean(xc * xc, axis=-1, keepdims=True) + LN_EPS)
    h3 = _silu(xc * rstd * lg_ref[...] + lb_ref[...])
    ob = jnp.dot(h3.astype(BF16), pww_ref[...], preferred_element_type=F32) + pwb_ref[...]
    ob_ref[0] = ob
    y_s[:, W_MIX:] = (ob * _silu(proj_ref[0, :, 4 * W_MIX:5 * W_MIX])).astype(BF16)

    h = x_ref[0] + jnp.dot(y_s[...], wout_ref[...], preferred_element_type=F32)
    r2 = lax.rsqrt(jnp.mean(h * h, axis=-1, keepdims=True) + RMS_EPS)
    hr = h * r2
    err = hr * fg_ref[...] - tgt_ref[0]
    loss_ref[...] += jnp.sum(err * err) * (0.5 / D_MODEL)
    dout = err * (1.0 / D_MODEL)
    gfg_ref[...] += _colsum8(dout * hr)
    gy = dout * fg_ref[...]
    dh_ref[0] = r2 * (gy - hr * jnp.mean(gy * hr, axis=-1, keepdims=True))

    uext[0:POOL_HALO, :] = uext[tm:tm + POOL_HALO, :]
    hext[0:CONV_HALO, :] = hext[tm:tm + CONV_HALO, :]


def _row(v):
    return v.reshape(1, -1)


def _fwd(x, tgt, proj, pool_w, pool_b, pool_scale, conv_dw_f, conv_b, ln_g, ln_b, pw_w_b, pw_b, w_out_b, final_g):
    bl, s, _ = x.shape
    tm = TILE_M
    tok = lambda n: pl.BlockSpec((1, tm, n), lambda b, i: (b, i, 0))
    acc = lambda n: pl.BlockSpec((SUBLANES, n), lambda b, i: (0, 0))
    act = lambda n: jax.ShapeDtypeStruct((bl, s, n), F32)
    return pl.pallas_call(
        functools.partial(_fwd_body),
        name="fwd",
        grid=(bl, s // tm),
        in_specs=[tok(D_MODEL), tok(D_MODEL), tok(D_IN)] + [VMEM] * 11,
        out_specs=[tok(W_MIX), tok(W_MIX), tok(W_MIX), tok(W_MIX), tok(D_MODEL), acc(LANES), acc(D_MODEL)],
        out_shape=[act(W_MIX), act(W_MIX), act(W_MIX), jax.ShapeDtypeStruct((bl, s, W_MIX), BF16), act(D_MODEL),
                   jax.ShapeDtypeStruct((SUBLANES, LANES), F32), jax.ShapeDtypeStruct((SUBLANES, D_MODEL), F32)],
        scratch_shapes=[pltpu.VMEM((POOL_HALO + tm, W_MIX), F32), pltpu.VMEM((CONV_HALO + tm, W_MIX), F32),
                        pltpu.VMEM((2, CONV_ROWS + CONV_HALO - SUBLANES, LANES), F32), pltpu.VMEM((tm, D_MODEL), BF16)],
        compiler_params=pltpu.CompilerParams(dimension_semantics=("arbitrary", "arbitrary"), vmem_limit_bytes=VMEM_LIMIT),
    )(x, tgt, proj, pool_w, pool_b.reshape(1, W_MIX), _row(pool_scale), conv_dw_f, _row(conv_b),
      _row(ln_g), _row(ln_b), pw_w_b, _row(pw_b), w_out_b, _row(final_g))


NT_DIMS = (((1,), (1,)), ((), ()))
TN_DIMS = (((0,), (0,)), ((), ()))
V_POOL_B, V_POOL_SCALE, V_CONV_B, V_LN_G, V_LN_B, V_PW_B = range(6)


def _bwd_taps():
    taps = {s: [] for s in range(SUBLANES)}
    for k in range(CONV_WIDTH):
        e = CONV_WIDTH - 1 - k
        taps[e % SUBLANES].append((e // SUBLANES, k))
    return taps


def _bwd1_body(n_tiles, dh_ref, proj_ref, h1_ref, z_ref, ob_ref, dpool_ref, wout_ref, pww_ref, poolw_ref, pools_ref, dw_ref,
               lg_ref, lb_ref,
               dproj_ref, gwout_ref, gpww_ref, gpoolw_ref, gvec_ref, gdw_ref,
               vext, gext, gsh, y_s):
    b = pl.program_id(0)
    i = pl.program_id(1)
    tm = TILE_M
    t0 = (n_tiles - 1 - i) * tm

    @pl.when(i == 0)
    def _():
        vext[tm:, :] = jnp.zeros((POOL_HALO, W_MIX), F32)
        gext[tm:, :] = jnp.zeros((CONV_HALO, W_MIX), F32)

    @pl.when((b == 0) & (i == 0))
    def _():
        gwout_ref[...] = jnp.zeros_like(gwout_ref)
        gpww_ref[...] = jnp.zeros_like(gpww_ref)
        gpoolw_ref[...] = jnp.zeros_like(gpoolw_ref)
        gvec_ref[...] = jnp.zeros_like(gvec_ref)
        gdw_ref[...] = jnp.zeros_like(gdw_ref)

    sub = tm // BWD_SUBTILES

    def dense_phase(r0):
        rs = slice(r0, r0 + sub)
        dhb = dh_ref[0, rs, :].astype(BF16)
        dy_a = lax.dot_general(dhb, wout_ref[0:W_MIX, :], NT_DIMS, preferred_element_type=F32)
        dy_b = lax.dot_general(dhb, wout_ref[W_MIX:, :], NT_DIMS, preferred_element_type=F32)

        a_gate = proj_ref[0, rs, W_MIX:2 * W_MIX]
        sig_a = jax.nn.sigmoid(a_gate)
        silu_a = a_gate * sig_a
        z = z_ref[0, rs, :]
        out_a = z * pools_ref[...]
        y_s[rs, 0:W_MIX] = (out_a * silu_a).astype(BF16)
        d_out_a = dy_a * silu_a
        d_ag = dy_a * out_a * (sig_a * (1.0 + a_gate * (1.0 - sig_a)))
        dproj_ref[0, rs, W_MIX:2 * W_MIX] = d_ag.astype(BF16)
        gvec_ref[V_POOL_SCALE] += _colsum8(d_out_a * z)
        dz = d_out_a * pools_ref[...]
        gvec_ref[V_POOL_B] += _colsum8(dz)
        t = _row_index((sub, POOL_GC), t0 + r0)
        n_ext = sub + POOL_HALO
        for g, w in enumerate(POOL_WINDOWS):
            lanes = slice(g * POOL_GC, (g + 1) * POOL_GC)
            dzg = dz[:, lanes].astype(BF16)
            gpoolw_ref[g] += lax.dot_general(dpool_ref[0, rs, lanes], dzg, TN_DIMS, preferred_element_type=F32)
            dd = lax.dot_general(dzg, poolw_ref[g].astype(BF16), NT_DIMS, preferred_element_type=F32)
            vext[rs, lanes] = dd * (1.0 / jnp.minimum(t + 1, w).astype(F32))
            s = vext[r0:r0 + n_ext, lanes]
            sh = 1
            while sh < w:
                s = s + pltpu.roll(s, n_ext - sh, 0)
                sh *= 2
            dproj_ref[0, rs, lanes] = (s[0:sub] - dd).astype(BF16)

        b_gate = proj_ref[0, rs, 4 * W_MIX:5 * W_MIX]
        sig_b = jax.nn.sigmoid(b_gate)
        silu_b = b_gate * sig_b
        ob = ob_ref[0, rs, :]
        y_s[rs, W_MIX:] = (ob * silu_b).astype(BF16)
        d_ob = dy_b * silu_b
        dproj_ref[0, rs, 4 * W_MIX:5 * W_MIX] = (dy_b * ob * (sig_b * (1.0 + b_gate * (1.0 - sig_b)))).astype(BF16)
        gvec_ref[V_PW_B] += _colsum8(d_ob)
        gwout_ref[...] += lax.dot_general(y_s[rs, :], dhb, TN_DIMS, preferred_element_type=F32)

        h1 = h1_ref[0, rs, :]
        mu = jnp.mean(h1, axis=-1, keepdims=True)
        xc = h1 - mu
        rstd = lax.rsqrt(jnp.mean(xc * xc, axis=-1, keepdims=True) + LN_EPS)
        xhat = xc * rstd
        h2 = xhat * lg_ref[...] + lb_ref[...]
        sig2 = jax.nn.sigmoid(h2)
        d_obb = d_ob.astype(BF16)
        gpww_ref[...] += lax.dot_general((h2 * sig2).astype(BF16), d_obb, TN_DIMS, preferred_element_type=F32)
        dh2 = lax.dot_general(d_obb, pww_ref[...], NT_DIMS, preferred_element_type=F32) * (sig2 * (1.0 + h2 * (1.0 - sig2)))
        gvec_ref[V_LN_G] += _colsum8(dh2 * xhat)
        gvec_ref[V_LN_B] += _colsum8(dh2)
        dxh = dh2 * lg_ref[...]
        dh1 = rstd * (dxh - jnp.mean(dxh, axis=-1, keepdims=True) - xhat * jnp.mean(dxh * xhat, axis=-1, keepdims=True))
        gvec_ref[V_CONV_B] += _colsum8(dh1)
        gext[rs, :] = dh1

    taps = _bwd_taps()
    cr = BWD_CONV_ROWS
    win = cr + CONV_HALO - SUBLANES

    def conv_phase(r0):
        for c0 in range(r0, r0 + sub, cr):
            for l0 in range(0, W_MIX, LANES):
                lanes = slice(l0, l0 + LANES)
                b_val = proj_ref[0, c0:c0 + cr, 2 * W_MIX + l0:2 * W_MIX + l0 + LANES]
                sg = jax.nn.sigmoid(proj_ref[0, c0:c0 + cr, 3 * W_MIX + l0:3 * W_MIX + l0 + LANES])
                h0c = b_val * sg
                acc = jnp.zeros((cr, LANES), F32)
                for s in range(SUBLANES):
                    gsh[...] = gext[c0 + s:c0 + s + win, lanes]
                    for q, k in taps[s]:
                        gq = gsh[SUBLANES * q:SUBLANES * q + cr, :]
                        acc = acc + gq * dw_ref[k:k + 1, lanes]
                        gdw_ref[k, :, lanes] += _colsum8(h0c * gq)
                d_bval = acc * sg
                dproj_ref[0, c0:c0 + cr, 2 * W_MIX + l0:2 * W_MIX + l0 + LANES] = d_bval.astype(BF16)
                dproj_ref[0, c0:c0 + cr, 3 * W_MIX + l0:3 * W_MIX + l0 + LANES] = (d_bval * b_val * (1.0 - sg)).astype(BF16)

    starts = [k * sub for k in reversed(range(BWD_SUBTILES))]
    for r0 in starts:
        dense_phase(r0)
    for r0 in starts:
        conv_phase(r0)
    vext[tm:, :] = vext[0:POOL_HALO, :]
    gext[tm:, :] = gext[0:CONV_HALO, :]


def _bwd1(dh, proj, h1, z, ob, dpool, w_out_b, pw_w_b, pool_w, pool_scale, conv_dw_f, ln_g, ln_b):
    bl, s, _ = dh.shape
    tm = TILE_M
    nt = s // tm
    tok = lambda n: pl.BlockSpec((1, tm, n), lambda b, i: (b, nt - 1 - i, 0))
    res = lambda shape: pl.BlockSpec(shape, lambda b, i: (0,) * len(shape))
    out_shapes = [(D_MODEL, D_MODEL), (W_MIX, W_MIX), (len(POOL_WINDOWS), POOL_GC, POOL_GC), (6, SUBLANES, W_MIX),
                  (CONV_WIDTH, SUBLANES, W_MIX)]
    return pl.pallas_call(
        functools.partial(_bwd1_body, nt),
        name="bwd1",
        grid=(bl, nt),
        in_specs=[tok(D_MODEL), tok(D_IN), tok(W_MIX), tok(W_MIX), tok(W_MIX), tok(W_MIX)] + [VMEM] * 7,
        out_specs=[tok(D_IN)] + [res(sh) for sh in out_shapes],
        out_shape=[jax.ShapeDtypeStruct((bl, s, D_IN), BF16)] + [jax.ShapeDtypeStruct(sh, F32) for sh in out_shapes],
        scratch_shapes=[pltpu.VMEM((tm + POOL_HALO, W_MIX), F32), pltpu.VMEM((tm + CONV_HALO, W_MIX), F32),
                        pltpu.VMEM((BWD_CONV_ROWS + CONV_HALO - SUBLANES, LANES), F32), pltpu.VMEM((tm, D_MODEL), BF16)],
        compiler_params=pltpu.CompilerParams(dimension_semantics=("arbitrary", "arbitrary"), vmem_limit_bytes=VMEM_LIMIT),
    )(dh, proj, h1, z, ob, dpool, w_out_b, pw_w_b, pool_w, _row(pool_scale), conv_dw_f, _row(ln_g), _row(ln_b))


MESH = pl.DeviceIdType.MESH
W_IN_BLK = D_IN // N_DEV
W_OUT_BLK = D_MODEL // N_DEV
PW_BLK = W_MIX // N_DEV
DW_BLK = W_MIX // N_DEV
N_PEER = N_DEV - 1


SLAB = 2 * W_IN_BLK
N_SLABS = 4
F_SIB, F_X, F_Y, F_ON, F_PASS_X, F_PASS_Y, F_PASS_D = range(7)
W_IN_ONLY, OTHERS = (0,), (1, 2, 3)


def _inproj_body(n_tiles, x_ref, ng_ref, wint_ref, wout_ref, pww_ref, dw_ref,
                 proj_hbm, wt_hbm, woutf_hbm, pwwf_hbm, dwf_ref,
                 hn_s, wt_s, wout_s, pww_s, dw_blk, stage, out_sems, w_sems, send_sems, recv_sems):
    s = pl.program_id(0)
    i = pl.program_id(1)
    tm = TILE_IN
    x, y, c = lax.axis_index("x"), lax.axis_index("y"), lax.axis_index("c")
    sibling = (x, y, 1 - c)
    xn, yn, dg = (1 - x, y), (x, 1 - y), (1 - x, 1 - y)
    pick = lambda a, b: tuple(jnp.where(c == 1, p, q) for p, q in zip(a, b))
    on_from, on_to = pick(xn, yn), pick(yn, xn)

    def blocks(px, py, pc):
        k = 4 * px + 2 * py + pc
        rows = lambda ref, n: ref.at[pl.ds(pl.multiple_of(k * n, n), n), :]
        return (rows(wt_s, W_IN_BLK), rows(wout_s, W_OUT_BLK), rows(pww_s, PW_BLK), dw_blk.at[k])

    def copies(arrays, flow, block, to):
        refs = blocks(*block)
        return [pltpu.make_async_remote_copy(src_ref=refs[a], dst_ref=refs[a], send_sem=send_sems.at[a, flow],
                                             recv_sem=recv_sems.at[a, flow], device_id=to, device_id_type=MESH)
                for a in arrays]

    def start(arrays):
        for flow, to in ((F_SIB, sibling), (F_X, (*xn, c)), (F_Y, (*yn, c))):
            for cp in copies(arrays, flow, (x, y, c), to):
                cp.start()

    def direct(arrays):
        for flow, chip in ((F_X, xn), (F_Y, yn)):
            for cp in copies(arrays, flow, (*chip, c), sibling):
                cp.wait_recv()
        for cp in copies(arrays, F_ON, (*on_from, c), (*on_to, c)):
            cp.start()
        for flow, chip in ((F_PASS_X, xn), (F_PASS_Y, yn)):
            for cp in copies(arrays, flow, (*chip, c), sibling):
                cp.start()

    def diagonal(arrays):
        for cp in copies(arrays, F_ON, (*dg, c), sibling):
            cp.wait_recv()
        for cp in copies(arrays, F_PASS_D, (*dg, c), sibling):
            cp.start()

    def from_sibling(arrays, flow, chip):
        for cp in copies(arrays, flow, (*chip, 1 - c), sibling):
            cp.wait_recv()

    def finish(arrays):
        mine = (x, y, c)
        for flow, block in ((F_SIB, mine), (F_X, mine), (F_Y, mine), (F_ON, (*on_from, c)), (F_PASS_X, (*xn, c)),
                            (F_PASS_Y, (*yn, c)), (F_PASS_D, (*dg, c))):
            for cp in copies(arrays, flow, block, sibling):
                cp.wait_send()

    step = s * n_tiles + i
    par = step % 2
    first = step == 0

    @pl.when(first)
    def _():
        mine = blocks(x, y, c)
        mine[0][...] = wint_ref[...].astype(BF16)
        mine[1][...] = wout_ref[...].astype(BF16)
        mine[2][...] = pww_ref[...].astype(BF16)
        mine[3][...] = dw_ref[...]
        start(W_IN_ONLY)
        start(OTHERS)

    rows_i = pl.ds(pl.multiple_of(i * tm, tm), tm)

    @pl.when(s == 0)
    def _():
        xt = x_ref[0]
        r1 = lax.rsqrt(jnp.mean(xt * xt, axis=-1, keepdims=True) + RMS_EPS)
        hn_s[rows_i, :] = (xt * r1 * ng_ref[...]).astype(BF16)

    @pl.when(first)
    def _():
        from_sibling(W_IN_ONLY, F_SIB, (x, y))

    @pl.when((s == 1) & (i == 0))
    def _():
        direct(W_IN_ONLY)
        from_sibling(W_IN_ONLY, F_PASS_Y, yn)

    @pl.when((s == 2) & (i == 0))
    def _():
        from_sibling(W_IN_ONLY, F_PASS_X, xn)

    weights_out = [pltpu.make_async_copy(src, dst, w_sems.at[k])
                   for k, (src, dst) in enumerate(((wt_s, wt_hbm), (wout_s, woutf_hbm), (pww_s, pwwf_hbm)))]

    @pl.when((s == 3) & (i == 0))
    def _():
        diagonal(W_IN_ONLY)
        direct(OTHERS)
        from_sibling(W_IN_ONLY, F_PASS_D, dg)
        weights_out[0].start()

    tiles_per_seq = proj_hbm.shape[1] // tm

    def out_copy(parity, col0):
        dst = proj_hbm.at[i // tiles_per_seq, pl.ds(pl.multiple_of((i % tiles_per_seq) * tm, tm), tm),
                          pl.ds(pl.multiple_of(col0, LANES), SLAB)]
        return pltpu.make_async_copy(stage.at[parity], dst, out_sems.at[parity])

    @pl.when(step >= 2)
    def _():
        out_copy(par, 0).wait()

    chip = jnp.where(s < 2, 2 * x, 2 * (1 - x)) + jnp.where((s == 0) | (s == 2), y, 1 - y)
    slab = wt_s[pl.ds(pl.multiple_of(chip * SLAB, SLAB), SLAB), :]
    stage[par] = lax.dot_general(hn_s[rows_i, :], slab, NT_DIMS, preferred_element_type=F32)
    out_copy(par, chip * SLAB).start()

    @pl.when(step == N_SLABS * n_tiles - 1)
    def _():
        out_copy(par, 0).wait()
        out_copy(1 - par, 0).wait()
        from_sibling(OTHERS, F_SIB, (x, y))
        from_sibling(OTHERS, F_PASS_X, xn)
        from_sibling(OTHERS, F_PASS_Y, yn)
        diagonal(OTHERS)
        from_sibling(OTHERS, F_PASS_D, dg)
        for cp in weights_out[1:]:
            cp.start()
        finish(W_IN_ONLY)
        finish(OTHERS)
        for cp in weights_out:
            cp.wait()
        for k in range(N_DEV):
            dwf_ref[:, k * DW_BLK:(k + 1) * DW_BLK] = dw_blk[k]


def _inproj(x, norm_g, w_in_t, w_out, pw_w, conv_dw):
    bl, s, _ = x.shape
    tm = TILE_IN
    nt = s // tm
    n_tiles = bl * nt

    def x_map(phase, i):
        i = jnp.where(phase == 0, i, n_tiles - 1)
        return (i // nt, i % nt, 0)

    dma = pltpu.SemaphoreType.DMA
    return pl.pallas_call(
        functools.partial(_inproj_body, n_tiles),
        name="inproj",
        grid=(N_SLABS, n_tiles),
        in_specs=[pl.BlockSpec((1, tm, D_MODEL), x_map)] + [VMEM] * 5,
        out_specs=[pl.BlockSpec(memory_space=pl.ANY)] * 4 + [VMEM],
        out_shape=[jax.ShapeDtypeStruct((bl, s, D_IN), F32), jax.ShapeDtypeStruct((D_IN, D_MODEL), BF16),
                   jax.ShapeDtypeStruct((D_MODEL, D_MODEL), BF16), jax.ShapeDtypeStruct((W_MIX, W_MIX), BF16),
                   jax.ShapeDtypeStruct((CONV_WIDTH, W_MIX), F32)],
        scratch_shapes=[pltpu.VMEM((n_tiles * tm, D_MODEL), BF16), pltpu.VMEM((D_IN, D_MODEL), BF16),
                        pltpu.VMEM((D_MODEL, D_MODEL), BF16), pltpu.VMEM((W_MIX, W_MIX), BF16),
                        pltpu.VMEM((N_DEV, CONV_WIDTH, DW_BLK), F32), pltpu.VMEM((2, tm, SLAB), F32),
                        dma((2,)), dma((3,)), dma((4, 7)), dma((4, 7))],
        compiler_params=pltpu.CompilerParams(dimension_semantics=("arbitrary", "arbitrary"), vmem_limit_bytes=VMEM_LIMIT),
    )(x, _row(norm_g), w_in_t, w_out, pw_w, conv_dw)


B_PLAIN, B_ON, B_MERGED = range(3)


def _routes():
    x, y, c = lax.axis_index("x"), lax.axis_index("y"), lax.axis_index("c")
    xn, yn = (1 - x, y), (x, 1 - y)
    pick = lambda a, b: tuple(jnp.where(c == 1, p, q) for p, q in zip(a, b))
    return pick(xn, yn), pick(yn, xn), (1 - x, 1 - y), (x, y), c


def _chip_index(chip):
    return 2 * chip[0] + chip[1]


def _copy(src, dst, send_sems, recv_sems, a, k, to):
    return pltpu.make_async_remote_copy(src_ref=src, dst_ref=dst, send_sem=send_sems.at[a, k], recv_sem=recv_sems.at[a, k],
                                        device_id=to, device_id_type=MESH)


def _rsb_send(plain, on, rb_plain, rb_on, send_b, recv_b):
    first, _, _, _, c = _routes()
    for k, srcs, dsts in ((B_ON, on, rb_on), (B_PLAIN, plain, rb_plain)):
        for a, (s, d) in enumerate(zip(srcs, dsts)):
            _copy(s, d, send_b, recv_b, a, k, (*first, c)).start()


def _rsb_merge(second_sums, merged, rb_on, rb_merged, send_b, recv_b):
    _, second, _, _, c = _routes()
    for a, (own, m, got, dst) in enumerate(zip(second_sums, merged, rb_on, rb_merged)):
        _copy(got, got, send_b, recv_b, a, B_ON, (*second, c)).wait_recv()
        m[...] = (own[...] + got[...].astype(F32)).astype(BF16)
        _copy(m, dst, send_b, recv_b, a, B_MERGED, (*second, c)).start()


def _rsb_total(own_sums, plain, on, merged, rb_plain, rb_merged, send_b, recv_b):
    _, second, _, _, c = _routes()
    out = []
    for a, (own, p, g) in enumerate(zip(own_sums, rb_plain, rb_merged)):
        _copy(p, p, send_b, recv_b, a, B_PLAIN, (*second, c)).wait_recv()
        _copy(g, g, send_b, recv_b, a, B_MERGED, (*second, c)).wait_recv()
        out.append(own[...] + p[...].astype(F32) + g[...].astype(F32))
    for k, srcs in ((B_PLAIN, plain), (B_ON, on), (B_MERGED, merged)):
        for a, s in enumerate(srcs):
            _copy(s, s, send_b, recv_b, a, k, (*second, c)).wait_send()
    return out


R_LOSS, R_FINAL_G, R_POOL, R_CONV, R_LN = range(5)
VEC_ROWS = 16
DW_ROWS = 32
A_PART, A_X, A_Y, A_ON, A_HALF = range(5)
HALF_DTYPES = (F32, BF16, BF16)
STEP_SUMS, STEP_MERGE, STEP_HALF = 1, 3, 5


def _bwd2a_body(n_steps, x_ref, dproj_ref, ng_ref, gwout_hbm, gpww_hbm, loss_ref, gfg_ref, gvec_ref, gdw_ref, gpoolw_ref,
                win_own_ref, win_staged_ref, owout_ref, opww_ref, ovec_ref, odw_ref, opoolw_ref,
                acc, st_win, a_win, own_wout, own_pww, a_wout, a_pww, st_wout, st_pww, rb_wout, rb_pww, part_vec, part_dw,
                part_pw, sib_vec, sib_dw, sib_pw, res_vec, res_dw, res_pw, half_vec, half_dw, half_pw,
                own_sems, send_a, recv_a, send_b, recv_b, send_r, recv_r, send_w, recv_w):
    step = pl.program_id(0) * pl.num_programs(1) + pl.program_id(1)
    x, y, c = lax.axis_index("x"), lax.axis_index("y"), lax.axis_index("c")
    sibling = (x, y, 1 - c)
    xn, yn = (1 - x, y), (x, 1 - y)
    first, second, diag, own, _ = _routes()
    rows = lambda ref, k, n: ref.at[pl.ds(pl.multiple_of(k * n, n), n), :]

    blocks = lambda k: (rows(gwout_hbm, k, W_OUT_BLK), rows(gpww_hbm, k, PW_BLK))
    mine = lambda chip: (own_wout.at[chip], own_pww.at[chip])
    mine_copies = lambda chip: [pltpu.make_async_copy(s, d, own_sems.at[a, chip])
                                for a, (s, d) in enumerate(zip(blocks(2 * chip + c), mine(chip)))]
    landed = lambda chip: (a_wout.at[chip], a_pww.at[chip])
    staged = lambda k: (st_wout.at[k], st_pww.at[k])
    arrived = lambda k: (rb_wout.at[k], rb_pww.at[k])

    parts, sibs, halves = (part_vec, part_dw, part_pw), (sib_vec, sib_dw, sib_pw), (half_vec, half_dw, half_pw)
    results = (res_vec, res_dw, res_pw)
    half_rows = lambda ref: pl.ds(pl.multiple_of(c * (ref.shape[0] // 2), SUBLANES), ref.shape[0] // 2)
    slot = lambda chip: tuple(h.at[_chip_index(chip)] for h in halves)

    @pl.when(step == 0)
    def _():
        acc[...] = jnp.zeros_like(acc)
        sum8 = lambda v: jnp.sum(v, axis=0, keepdims=True)
        part_vec[...] = jnp.zeros((VEC_ROWS, D_MODEL), F32)
        part_vec[R_LOSS:R_LOSS + 1, :] = jnp.broadcast_to(loss_ref[0:1, 0:1], (1, D_MODEL))
        part_vec[R_FINAL_G:R_FINAL_G + 1, :] = sum8(gfg_ref[...])
        for row, (lo, hi) in ((R_POOL, (V_POOL_B, V_POOL_SCALE)), (R_CONV, (V_CONV_B, V_LN_G)), (R_LN, (V_LN_B, V_PW_B))):
            part_vec[row:row + 1, 0:W_MIX] = sum8(gvec_ref[lo])
            part_vec[row:row + 1, W_MIX:] = sum8(gvec_ref[hi])
        part_dw[0:CONV_WIDTH, :] = jnp.sum(gdw_ref[...], axis=1)
        part_dw[CONV_WIDTH:, :] = jnp.zeros((DW_ROWS - CONV_WIDTH, W_MIX), F32)
        part_pw[...] = gpoolw_ref[...].reshape(len(POOL_WINDOWS) * POOL_GC, POOL_GC)
        for a, (p, s) in enumerate(zip(parts, sibs)):
            _copy(p, s, send_r, recv_r, a, A_PART, sibling).start()
        for chip in range(4):
            for a, (s, d) in enumerate(zip(blocks(2 * chip + (1 - c)), landed(chip))):
                _copy(s, d, send_a, recv_a, a, chip, sibling).start()
            for cp in mine_copies(chip):
                cp.start()

    x_t = x_ref[0]
    r1 = lax.rsqrt(jnp.mean(x_t * x_t, axis=-1, keepdims=True) + RMS_EPS)
    hn = (x_t * r1 * ng_ref[...]).astype(BF16)
    acc[...] += lax.dot_general(dproj_ref[0], hn, TN_DIMS, preferred_element_type=F32)

    @pl.when(step == STEP_SUMS)
    def _():
        for a, (p, s, h) in enumerate(zip(parts, sibs, slot(own))):
            _copy(p, s, send_r, recv_r, a, A_PART, sibling).wait_recv()
            h[...] = (p[half_rows(p), :] + s[half_rows(s), :]).astype(h.dtype)
            for k, to in ((A_X, xn), (A_Y, yn)):
                _copy(h, h, send_r, recv_r, a, k, (*to, c)).start()
        for chip in range(4):
            for cp in mine_copies(chip):
                cp.wait()
            for a, (m, l) in enumerate(zip(mine(chip), landed(chip))):
                _copy(l, l, send_a, recv_a, a, chip, sibling).wait_recv()
                l[...] = m[...] + l[...]
        for k, chip in ((B_ON, diag), (B_PLAIN, first)):
            for s, l in zip(staged(k), landed(_chip_index(chip))):
                s[...] = l[...].astype(BF16)
        _rsb_send(staged(B_PLAIN), staged(B_ON), arrived(B_PLAIN), arrived(B_ON), send_b, recv_b)

    @pl.when(step == STEP_MERGE)
    def _():
        _rsb_merge(landed(_chip_index(second)), staged(B_MERGED), arrived(B_ON), arrived(B_MERGED), send_b, recv_b)
        for a in range(3):
            for k, chip in ((A_X, xn), (A_Y, yn)):
                h = slot(chip)[a]
                _copy(h, h, send_r, recv_r, a, k, sibling).wait_recv()
            h = slot(first)[a]
            _copy(h, h, send_r, recv_r, a, A_ON, (*second, c)).start()

    @pl.when(step == STEP_HALF)
    def _():
        for a, (r, hs) in enumerate(zip(results, halves)):
            h = slot(diag)[a]
            _copy(h, h, send_r, recv_r, a, A_ON, sibling).wait_recv()
            mine = r.at[half_rows(r), :]
            chip_sums = [hs[k].astype(F32) for k in range(4)]
            mine[...] = (chip_sums[0] + chip_sums[1]) + (chip_sums[2] + chip_sums[3])
            _copy(mine, mine, send_r, recv_r, a, A_HALF, sibling).start()

    @pl.when(step == n_steps - 1)
    def _():
        block = lambda k: rows(acc, k, W_IN_BLK)
        w_copy = lambda chip: _copy(st_win.at[chip], a_win.at[chip], send_w, recv_w, 0, chip, sibling)
        for chip in range(4):
            st_win[chip] = block(2 * chip + (1 - c))[...].astype(BF16)
            w_copy(chip).start()

        owout_ref[...], opww_ref[...] = _rsb_total(landed(_chip_index(own)), staged(B_PLAIN), staged(B_ON), staged(B_MERGED),
                                                   arrived(B_PLAIN), arrived(B_MERGED), send_b, recv_b)
        for a, (r, p, s) in enumerate(zip(results, parts, sibs)):
            theirs = r.at[pl.ds(pl.multiple_of((1 - c) * (r.shape[0] // 2), SUBLANES), r.shape[0] // 2), :]
            _copy(theirs, theirs, send_r, recv_r, a, A_HALF, sibling).wait_recv()
            _copy(p, s, send_r, recv_r, a, A_PART, sibling).wait_send()
            for k, h in ((A_X, slot(own)[a]), (A_Y, slot(own)[a]), (A_ON, slot(first)[a])):
                _copy(h, h, send_r, recv_r, a, k, sibling).wait_send()
            mine = r.at[half_rows(r), :]
            _copy(mine, mine, send_r, recv_r, a, A_HALF, sibling).wait_send()
        for chip in range(4):
            for a, (s, d) in enumerate(zip(blocks(2 * chip + (1 - c)), landed(chip))):
                _copy(s, d, send_a, recv_a, a, chip, sibling).wait_send()
        ovec_ref[...] = res_vec[...]
        opoolw_ref[...] = res_pw[...]
        for k in range(N_DEV):
            @pl.when(4 * x + 2 * y + c == k)
            def _():
                odw_ref[...] = res_dw[0:CONV_WIDTH, k * DW_BLK:(k + 1) * DW_BLK]

        for chip in range(4):
            w_copy(chip).wait()
        chip_sum = lambda chip: block(2 * _chip_index(chip) + c)[...] + a_win[_chip_index(chip)].astype(F32)
        win_own_ref[...] = chip_sum(own)
        for j, chip in enumerate((xn, yn, diag)):
            win_staged_ref[j] = chip_sum(chip).astype(BF16)


def _bwd2a(x, dproj, norm_g, gwout, gpww, loss8, gfg8, gvec, gdw8, gpoolw):
    bl, s, _ = x.shape
    tm = TILE_M
    nt = s // tm
    tok = lambda n: pl.BlockSpec((1, tm, n), lambda b, i: (b, i, 0))
    n_pw = len(POOL_WINDOWS) * POOL_GC
    win = (W_IN_BLK, D_MODEL)
    blk = ((W_OUT_BLK, D_MODEL), (PW_BLK, W_MIX))
    rep = ((VEC_ROWS, D_MODEL), (DW_ROWS, W_MIX), (n_pw, POOL_GC))
    half = lambda sh: (sh[0] // 2,) + sh[1:]
    dma = pltpu.SemaphoreType.DMA
    assert bl * nt > STEP_HALF + 1, "the exchanges' phases need their grid steps"
    return pl.pallas_call(
        functools.partial(_bwd2a_body, bl * nt),
        name="bwd2a",
        grid=(bl, nt),
        in_specs=[tok(D_MODEL), tok(D_IN), VMEM] + [pl.BlockSpec(memory_space=pl.ANY)] * 2 + [VMEM] * 5,
        out_specs=[VMEM] * 7,
        out_shape=[jax.ShapeDtypeStruct(win, F32), jax.ShapeDtypeStruct((3,) + win, BF16)]
        + [jax.ShapeDtypeStruct(sh, F32) for sh in blk + (rep[0], (CONV_WIDTH, DW_BLK), rep[2])],
        scratch_shapes=[pltpu.VMEM((D_IN, D_MODEL), F32), pltpu.VMEM((4,) + win, BF16), pltpu.VMEM((4,) + win, BF16)]
        + [pltpu.VMEM((4,) + sh, F32) for sh in blk] * 2
        + [pltpu.VMEM((3,) + sh, BF16) for sh in blk] * 2
        + [pltpu.VMEM(sh, F32) for sh in rep] * 3
        + [pltpu.VMEM((4,) + half(sh), dt) for sh, dt in zip(rep, HALF_DTYPES)]
        + [dma((2, 4)), dma((2, 4)), dma((2, 4)), dma((2, 3)), dma((2, 3)), dma((3, 5)), dma((3, 5)), dma((1, 4)), dma((1, 4))],
        compiler_params=pltpu.CompilerParams(dimension_semantics=("arbitrary", "arbitrary"), vmem_limit_bytes=VMEM_LIMIT),
    )(x, dproj, _row(norm_g), gwout, gpww, loss8, gfg8, gvec, gdw8, gpoolw)


HBM = pl.BlockSpec(memory_space=pltpu.HBM)
SEM = pl.BlockSpec(memory_space=pltpu.SEMAPHORE)
EFFECT = pltpu.SideEffectType.DATAFLOW_SIDE_EFFECTING


def _owner_copies(src_ref, land_ref, send_sems, recv_sems):
    x, y, c = lax.axis_index("x"), lax.axis_index("y"), lax.axis_index("c")
    owners = ((1 - x, y, c), (x, 1 - y, c), (1 - x, 1 - y, c))
    return [pltpu.make_async_remote_copy(src_ref=src_ref.at[j], dst_ref=land_ref.at[j], send_sem=send_sems.at[j],
                                         recv_sem=recv_sems.at[j], device_id=owners[j], device_id_type=MESH)
            for j in range(3)]


def _rs_start(staged):
    def body(src_ref, land_ref, send_sems, recv_sems, src_thru, land_thru, token):
        for cp in _owner_copies(src_ref, land_ref, send_sems, recv_sems):
            cp.start()
        token[...] = jnp.zeros_like(token)

    dma = pltpu.SemaphoreType.DMA
    return pl.pallas_call(
        body, name="rs_start",
        out_shape=(dma((3,)), dma((3,)), pltpu.HBM(staged.shape, staged.dtype), pltpu.HBM(staged.shape, staged.dtype),
                   jax.ShapeDtypeStruct((SUBLANES, LANES), F32)),
        in_specs=(HBM, HBM), out_specs=(SEM, SEM, HBM, HBM, VMEM), input_output_aliases={0: 2, 1: 3},
        compiler_params=pltpu.CompilerParams(has_side_effects=EFFECT),
    )(pltpu.with_memory_space_constraint(staged, pltpu.HBM),
      pltpu.with_memory_space_constraint(lax.empty(staged.shape, staged.dtype), pltpu.HBM))


def _rs_wait(send_sems, recv_sems, src_thru, land_thru, after):
    def body(src_ref, land_ref, send_sems, recv_sems, after_ref, src_dead, got_ref):
        for cp in _owner_copies(src_ref, land_ref, send_sems, recv_sems):
            cp.wait_send()
            cp.wait_recv()

    return pl.pallas_call(
        body, name="rs_wait",
        out_shape=(pltpu.HBM(src_thru.shape, src_thru.dtype), pltpu.HBM(land_thru.shape, land_thru.dtype)),
        in_specs=(HBM, HBM, SEM, SEM, pl.BlockSpec(memory_space=pl.ANY)), out_specs=(HBM, HBM),
        input_output_aliases={0: 0, 1: 1},
        compiler_params=pltpu.CompilerParams(has_side_effects=EFFECT),
    )(src_thru, land_thru, send_sems, recv_sems, after)[1]


def _bwd2b_body(x_ref, dh_ref, dproj_ref, ng_ref, wint_ref, token_ref, gx_ref, gng_ref):
    @pl.when((pl.program_id(0) == 0) & (pl.program_id(1) == 0))
    def _():
        gng_ref[...] = jnp.zeros_like(gng_ref) + token_ref[0:1, 0:1]

    x_t = x_ref[0]
    r1 = lax.rsqrt(jnp.mean(x_t * x_t, axis=-1, keepdims=True) + RMS_EPS)
    xr = x_t * r1
    dhn = jnp.dot(dproj_ref[0], wint_ref[...], preferred_element_type=F32)
    gng_ref[...] += _colsum8(dhn * xr)
    gy = dhn * ng_ref[...]
    gx_ref[0] = dh_ref[0] + r1 * (gy - xr * jnp.mean(gy * xr, axis=-1, keepdims=True))


def _bwd2b(x, dh, dproj, norm_g, w_in_t, token):
    bl, s, _ = x.shape
    tm = TILE_M
    tok = lambda n: pl.BlockSpec((1, tm, n), lambda b, i: (b, i, 0))
    return pl.pallas_call(
        functools.partial(_bwd2b_body),
        name="bwd2b",
        grid=(bl, s // tm),
        in_specs=[tok(D_MODEL), tok(D_MODEL), tok(D_IN), VMEM, VMEM, VMEM],
        out_specs=[tok(D_MODEL), pl.BlockSpec((SUBLANES, D_MODEL), lambda b, i: (0, 0))],
        out_shape=[jax.ShapeDtypeStruct((bl, s, D_MODEL), F32), jax.ShapeDtypeStruct((SUBLANES, D_MODEL), F32)],
        compiler_params=pltpu.CompilerParams(dimension_semantics=("arbitrary", "arbitrary"), vmem_limit_bytes=VMEM_LIMIT),
    )(x, dh, dproj, _row(norm_g), w_in_t, token)


def _partial_copies(part_ref, land_ref, send_sems, recv_sems):
    x, y, c = lax.axis_index("x"), lax.axis_index("y"), lax.axis_index("c")
    other_chips = [(1 - x, y), (x, 1 - y), (1 - x, 1 - y)]
    peers = [(x, y, 1 - c)] + [(*chip, c) for chip in other_chips] + [(*chip, 1 - c) for chip in other_chips]
    return [pltpu.make_async_remote_copy(src_ref=part_ref, dst_ref=land_ref.at[4 * x + 2 * y + c], send_sem=send_sems.at[r],
                                         recv_sem=recv_sems.at[r], device_id=peers[r], device_id_type=MESH)
            for r in range(N_PEER)]


def _partials_start(part):
    def body(part_ref, land_ref, send_sems, recv_sems, part_thru, land_thru):
        for cp in _partial_copies(part_ref, land_ref, send_sems, recv_sems):
            cp.start()

    dma = pltpu.SemaphoreType.DMA
    land = (N_DEV,) + part.shape
    return pl.pallas_call(
        body, name="partials_start",
        out_shape=(dma((N_PEER,)), dma((N_PEER,)), pltpu.HBM(part.shape, part.dtype), pltpu.HBM(land, part.dtype)),
        in_specs=(HBM, HBM), out_specs=(SEM, SEM, HBM, HBM), input_output_aliases={0: 2, 1: 3},
        compiler_params=pltpu.CompilerParams(has_side_effects=EFFECT),
    )(pltpu.with_memory_space_constraint(part, pltpu.HBM),
      pltpu.with_memory_space_constraint(lax.empty(land, part.dtype), pltpu.HBM))


def _partials_wait(send_sems, recv_sems, part_thru, land_thru, after):
    def body(part_ref, land_ref, send_sems, recv_sems, after_ref, part_out, land_out):
        for cp in _partial_copies(part_ref, land_ref, send_sems, recv_sems):
            cp.wait_send()
            cp.wait_recv()

    return pl.pallas_call(
        body, name="partials_wait",
        out_shape=(pltpu.HBM(part_thru.shape, part_thru.dtype), pltpu.HBM(land_thru.shape, land_thru.dtype)),
        in_specs=(HBM, HBM, SEM, SEM, pl.BlockSpec(memory_space=pl.ANY)), out_specs=(HBM, HBM),
        input_output_aliases={0: 0, 1: 1},
        compiler_params=pltpu.CompilerParams(has_side_effects=EFFECT),
    )(part_thru, land_thru, send_sems, recv_sems, after)


ADAM_LR = 0.001
ADAM_B1 = 0.9
ADAM_B2 = 0.999
ADAM_EPS = 1e-08
ADAM_WD = 0.01
ADAM_STEP = 10


PARAMS = ("norm_g", "w_in", "pool_w", "pool_b", "pool_scale", "conv_dw", "conv_b", "ln_g", "ln_b", "pw_w", "pw_b", "w_out",
          "final_g")


LATE_PARAMS = ("norm_g",)
MAIN_PARAMS = tuple(p for p in PARAMS if p not in LATE_PARAMS)


def _adamw_refs(params, n_sources, refs):
    n = len(params)
    ws, ms, vs = (dict(zip(params, refs[k * n:(k + 1) * n])) for k in range(3))
    outs = refs[3 * n + n_sources:]
    gs, ds, m2s, v2s = (dict(zip(params, outs[k * n:(k + 1) * n])) for k in range(4))

    def update(name, g, at=slice(None)):
        m = ADAM_B1 * ms[name][at] + (1.0 - ADAM_B1) * g
        v = ADAM_B2 * vs[name][at] + (1.0 - ADAM_B2) * (g * g)
        m_hat = m / (1.0 - ADAM_B1 ** ADAM_STEP)
        v_hat = v / (1.0 - ADAM_B2 ** ADAM_STEP)
        gs[name][at] = g
        ds[name][at] = -ADAM_LR * (m_hat / (jnp.sqrt(v_hat) + ADAM_EPS) + ADAM_WD * ws[name][at])
        m2s[name][at] = m
        v2s[name][at] = v

    return update, refs[3 * n:3 * n + n_sources], outs[4 * n:]


def _adamw_late_body(*refs):
    update, (gng_ref, gng_others_ref), _ = _adamw_refs(LATE_PARAMS, 2, refs)
    me = 4 * lax.axis_index("x") + 2 * lax.axis_index("y") + lax.axis_index("c")
    g_norm = jnp.zeros(gng_ref.shape, F32)
    for k in range(N_DEV):
        g_norm = g_norm + jnp.where(me == k, gng_ref[...], gng_others_ref[k])
    update("norm_g", jnp.sum(g_norm, axis=0, keepdims=True))


def _adamw_body(*refs):
    update, sources, (loss_ref,) = _adamw_refs(MAIN_PARAMS, 7, refs)
    gwin_ref, gwin_others_ref, gwout_ref, gpww_ref, vec_ref, gdw_ref, gpoolw_ref = sources
    ws = dict(zip(MAIN_PARAMS, refs))
    row = lambda r, lo, hi: vec_ref[r:r + 1, lo:hi]
    loss_ref[...] = row(R_LOSS, 0, 1)
    g_w_in = gwin_ref[...]
    for j in range(3):
        g_w_in = g_w_in + gwin_others_ref[j].astype(F32)
    update("w_in", g_w_in)
    update("pool_w", gpoolw_ref[...].reshape(ws["pool_w"].shape))
    for g in range(len(POOL_WINDOWS)):
        update("pool_b", row(R_POOL, g * POOL_GC, (g + 1) * POOL_GC), at=slice(g, g + 1))
    update("pool_scale", row(R_POOL, W_MIX, 2 * W_MIX))
    update("conv_dw", gdw_ref[...])
    update("conv_b", row(R_CONV, 0, W_MIX))
    update("ln_g", row(R_CONV, W_MIX, 2 * W_MIX))
    update("ln_b", row(R_LN, 0, W_MIX))
    update("pw_w", gpww_ref[...])
    update("pw_b", row(R_LN, W_MIX, 2 * W_MIX))
    update("w_out", gwout_ref[...])
    update("final_g", row(R_FINAL_G, 0, D_MODEL))


def _adamw(body, name, params, ws, ms, vs, grad_sources, extra_out_shapes=()):
    n = len(params)
    flat = lambda a: a.reshape(1, -1) if a.ndim == 1 else a
    args = [flat(d[p]) for d in (ws, ms, vs) for p in params]
    outs = pl.pallas_call(
        body,
        name=name,
        in_specs=[VMEM] * (3 * n + len(grad_sources)),
        out_specs=[VMEM] * (4 * n + len(extra_out_shapes)),
        out_shape=[jax.ShapeDtypeStruct(a.shape, F32) for a in args[:n]] * 4 + list(extra_out_shapes),
        compiler_params=pltpu.CompilerParams(vmem_limit_bytes=VMEM_LIMIT),
    )(*args, *grad_sources)
    groups = tuple({p: outs[k * n + j].reshape(ws[p].shape) for j, p in enumerate(params)} for k in range(4))
    return groups, outs[4 * n:]


def kernel(x, norm_g, w_in, pool_w, pool_b, pool_scale, conv_dw, conv_b, ln_g, ln_b, pw_w, pw_b, w_out, final_g, loss_target, m_norm_g, m_w_in, m_pool_w, m_pool_b, m_pool_scale, m_conv_dw, m_conv_b, m_ln_g, m_ln_b, m_pw_w, m_pw_b, m_w_out, m_final_g, v_norm_g, v_w_in, v_pool_w, v_pool_b, v_pool_scale, v_conv_dw, v_conv_b, v_ln_g, v_ln_b, v_pw_w, v_pw_b, v_w_out, v_final_g):
    proj, w_in_t, w_out_b, pw_w_b, conv_dw_f = _inproj(x, norm_g, w_in.T, w_out, pw_w, conv_dw)
    h1, z, ob, dpool, dh, loss8, gfg8 = _fwd(x, loss_target, proj, pool_w, pool_b, pool_scale, conv_dw_f,
                                             conv_b, ln_g, ln_b, pw_w_b, pw_b, w_out_b, final_g)
    dproj, gwout, gpww, gpoolw, gvec, gdw8 = _bwd1(dh, proj, h1, z, ob, dpool, w_out_b, pw_w_b, pool_w, pool_scale,
                                                   conv_dw_f, ln_g, ln_b)
    win_own, win_staged, g_w_out, g_pw_w, vec, g_conv_dw, poolw_sum = _bwd2a(x, dproj, norm_g, gwout, gpww, loss8, gfg8, gvec,
                                                                             gdw8, gpoolw)
    send_sems, recv_sems, staged_thru, land_thru, token = _rs_start(win_staged)
    grad_x, gng8 = _bwd2b(x, dh, dproj, norm_g, w_in_t, token)
    ng_send, ng_recv, gng8, gng_land = _partials_start(gng8)
    win_others = _rs_wait(send_sems, recv_sems, staged_thru, land_thru, gng8)

    weights = dict(norm_g=norm_g, w_in=w_in.T, pool_w=pool_w, pool_b=pool_b, pool_scale=pool_scale, conv_dw=conv_dw, conv_b=conv_b,
                   ln_g=ln_g, ln_b=ln_b, pw_w=pw_w, pw_b=pw_b, w_out=w_out, final_g=final_g)
    m_in = dict(norm_g=m_norm_g, w_in=m_w_in.T, pool_w=m_pool_w, pool_b=m_pool_b, pool_scale=m_pool_scale, conv_dw=m_conv_dw,
                conv_b=m_conv_b, ln_g=m_ln_g, ln_b=m_ln_b, pw_w=m_pw_w, pw_b=m_pw_b, w_out=m_w_out, final_g=m_final_g)
    v_in = dict(norm_g=v_norm_g, w_in=v_w_in.T, pool_w=v_pool_w, pool_b=v_pool_b, pool_scale=v_pool_scale, conv_dw=v_conv_dw,
                conv_b=v_conv_b, ln_g=v_ln_g, ln_b=v_ln_b, pw_w=v_pw_w, pw_b=v_pw_b, w_out=v_w_out, final_g=v_final_g)
    main, (loss,) = _adamw(functools.partial(_adamw_body), "adamw", MAIN_PARAMS, weights, m_in, v_in,
                           (win_own, win_others, g_w_out, g_pw_w, vec, g_conv_dw, poolw_sum), [jax.ShapeDtypeStruct((1, 1), F32)])
    gng8, gng_others = _partials_wait(ng_send, ng_recv, gng8, gng_land, main[0]["w_out"])
    late, _ = _adamw(functools.partial(_adamw_late_body), "adamw_norm_g", LATE_PARAMS, weights, m_in, v_in, (gng8, gng_others))
    groups = [[(late[k] if p in LATE_PARAMS else main[k])[p] for p in PARAMS] for k in range(4)]
    k_in = PARAMS.index("w_in")
    for group in groups:
        group[k_in] = group[k_in].T
    grads, delta, new_m, new_v = groups
    return (loss.reshape(()), grad_x, *grads, *delta, *new_m, *new_v)
```

```python
import functools

import jax
import jax.numpy as jnp
from jax import lax
from jax.experimental import pallas as pl
from jax.experimental.pallas import tpu as pltpu

F32 = jnp.float32
BF16 = jnp.bfloat16

D_MODEL = 1024
W_MIX = 512
D_IN = 5 * W_MIX
POOL_WINDOWS = (2, 4, 8, 16)
POOL_GC = 128
CONV_WIDTH = 31
RMS_EPS = 1e-6
LN_EPS = 1e-5
N_DEV = 8

LANES = 128
SUBLANES = 8
TILE_M = 512
TILE_IN = 1024
POOL_HALO = 16
CONV_HALO = 32
CONV_ROWS = 128
BWD_CONV_ROWS = 64
BWD_SUBTILES = 2
VMEM_LIMIT = 56 * 1024 * 1024

VMEM = pl.BlockSpec(memory_space=pltpu.VMEM)


def _silu(v):
    return v * jax.nn.sigmoid(v)


def _colsum8(v):
    m, n = v.shape
    return jnp.sum(v.reshape(m // SUBLANES, SUBLANES, n), axis=0)


def _row_index(shape, t0):
    return lax.broadcasted_iota(jnp.int32, shape, 0) + t0


def _conv_taps():
    taps = {s: [] for s in range(SUBLANES)}
    for k in range(CONV_WIDTH):
        e = k + CONV_HALO - (CONV_WIDTH - 1)
        taps[e % SUBLANES].append((e // SUBLANES, k))
    return taps


def _fwd_body(x_ref, tgt_ref, proj_ref, poolw_ref, poolb_ref, pools_ref, dw_ref, cb_ref, lg_ref, lb_ref,
              pww_ref, pwb_ref, wout_ref, fg_ref,
              h1_ref, z_ref, ob_ref, dpool_ref, dh_ref, loss_ref, gfg_ref,
              uext, hext, y_s):
    b = pl.program_id(0)
    i = pl.program_id(1)
    tm = TILE_M

    @pl.when(i == 0)
    def _():
        uext[0:POOL_HALO, :] = jnp.zeros((POOL_HALO, W_MIX), F32)
        hext[0:CONV_HALO, :] = jnp.zeros((CONV_HALO, W_MIX), F32)
        hext[CONV_HALO + tm:, :] = jnp.zeros((SUBLANES, W_MIX), F32)

    @pl.when((b == 0) & (i == 0))
    def _():
        loss_ref[...] = jnp.zeros_like(loss_ref)
        gfg_ref[...] = jnp.zeros_like(gfg_ref)

    uext[POOL_HALO:, :] = proj_ref[0, :, 0:W_MIX]
    t = _row_index((tm, POOL_GC), i * tm)
    for g, w in enumerate(POOL_WINDOWS):
        lanes = slice(g * POOL_GC, (g + 1) * POOL_GC)
        e = uext[:, lanes]
        s = e
        sh = 1
        while sh < w:
            s = s + pltpu.roll(s, sh, 0)
            sh *= 2
        inv_cnt = 1.0 / jnp.minimum(t + 1, w).astype(F32)
        d = (s[POOL_HALO:] * inv_cnt - e[POOL_HALO:]).astype(BF16)
        dpool_ref[0, :, lanes] = d
        z = jnp.dot(d, poolw_ref[g].astype(BF16), preferred_element_type=F32) + poolb_ref[:, lanes]
        z_ref[0, :, lanes] = z
        a_gate = proj_ref[0, :, W_MIX + g * POOL_GC:W_MIX + (g + 1) * POOL_GC]
        y_s[:, lanes] = (z * pools_ref[:, lanes] * _silu(a_gate)).astype(BF16)

    hext[CONV_HALO:CONV_HALO + tm, :] = proj_ref[0, :, 2 * W_MIX:3 * W_MIX] * jax.nn.sigmoid(proj_ref[0, :, 3 * W_MIX:4 * W_MIX])
    taps = _conv_taps()
    for c0 in range(0, tm, CONV_ROWS):
        for l0 in range(0, W_MIX, LANES):
            lanes = slice(l0, l0 + LANES)
            acc = jnp.zeros((CONV_ROWS, LANES), F32) + cb_ref[:, lanes]
            for s in range(SUBLANES):
                part = jnp.zeros((CONV_ROWS + SUBLANES, LANES), F32)
                for q, k in taps[s]:
                    r0 = c0 + SUBLANES * q
                    part = part + hext[r0:r0 + CONV_ROWS + SUBLANES, lanes] * dw_ref[k:k + 1, lanes]
                acc = acc + part[s:s + CONV_ROWS]
            h1_ref[0, c0:c0 + CONV_ROWS, lanes] = acc
    h1 = h1_ref[0]
    mu = jnp.mean(h1, axis=-1, keepdims=True)
    xc = h1 - mu
    rstd = lax.rsqrt(jnp.mean(xc * xc, axis=-1, keepdims=True) + LN_EPS)
    h3 = _silu(xc * rstd * lg_ref[...] + lb_ref[...])
    ob = jnp.dot(h3.astype(BF16), pww_ref[...], preferred_element_type=F32) + pwb_ref[...]
    ob_ref[0] = ob
    y_s[:, W_MIX:] = (ob * _silu(proj_ref[0, :, 4 * W_MIX:5 * W_MIX])).astype(BF16)

    h = x_ref[0] + jnp.dot(y_s[...], wout_ref[...], preferred_element_type=F32)
    r2 = lax.rsqrt(jnp.mean(h * h, axis=-1, keepdims=True) + RMS_EPS)
    hr = h * r2
    err = hr * fg_ref[...] - tgt_ref[0]
    loss_ref[...] += jnp.sum(err * err) * (0.5 / D_MODEL)
    dout = err * (1.0 / D_MODEL)
    gfg_ref[...] += _colsum8(dout * hr)
    gy = dout * fg_ref[...]
    dh_ref[0] = r2 * (gy - hr * jnp.mean(gy * hr, axis=-1, keepdims=True))

    uext[0:POOL_HALO, :] = uext[tm:tm + POOL_HALO, :]
    hext[0:CONV_HALO, :] = hext[tm:tm + CONV_HALO, :]


def _row(v):
    return v.reshape(1, -1)


def _fwd(x, tgt, proj, pool_w, pool_b, pool_scale, conv_dw_f, conv_b, ln_g, ln_b, pw_w_b, pw_b, w_out_b, final_g):
    bl, s, _ = x.shape
    tm = TILE_M
    tok = lambda n: pl.BlockSpec((1, tm, n), lambda b, i: (b, i, 0))
    acc = lambda n: pl.BlockSpec((SUBLANES, n), lambda b, i: (0, 0))
    act = lambda n: jax.ShapeDtypeStruct((bl, s, n), F32)
    return pl.pallas_call(
        functools.partial(_fwd_body),
        name="fwd",
        grid=(bl, s // tm),
        in_specs=[tok(D_MODEL), tok(D_MODEL), tok(D_IN)] + [VMEM] * 11,
        out_specs=[tok(W_MIX), tok(W_MIX), tok(W_MIX), tok(W_MIX), tok(D_MODEL), acc(LANES), acc(D_MODEL)],
        out_shape=[act(W_MIX), act(W_MIX), act(W_MIX), jax.ShapeDtypeStruct((bl, s, W_MIX), BF16), act(D_MODEL),
                   jax.ShapeDtypeStruct((SUBLANES, LANES), F32), jax.ShapeDtypeStruct((SUBLANES, D_MODEL), F32)],
        scratch_shapes=[pltpu.VMEM((POOL_HALO + tm, W_MIX), F32), pltpu.VMEM((CONV_HALO + tm + SUBLANES, W_MIX), F32),
                        pltpu.VMEM((tm, D_MODEL), BF16)],
        compiler_params=pltpu.CompilerParams(dimension_semantics=("arbitrary", "arbitrary"), vmem_limit_bytes=VMEM_LIMIT),
    )(x, tgt, proj, pool_w, pool_b.reshape(1, W_MIX), _row(pool_scale), conv_dw_f, _row(conv_b),
      _row(ln_g), _row(ln_b), pw_w_b, _row(pw_b), w_out_b, _row(final_g))


NT_DIMS = (((1,), (1,)), ((), ()))
TN_DIMS = (((0,), (0,)), ((), ()))
V_POOL_B, V_POOL_SCALE, V_CONV_B, V_LN_G, V_LN_B, V_PW_B = range(6)


def _bwd_taps():
    taps = {s: [] for s in range(SUBLANES)}
    for k in range(CONV_WIDTH):
        e = CONV_WIDTH - 1 - k
        taps[e % SUBLANES].append((e // SUBLANES, k))
    return taps


def _bwd1_body(n_tiles, dh_ref, proj_ref, h1_ref, z_ref, ob_ref, dpool_ref, wout_ref, pww_ref, poolw_ref, pools_ref, dw_ref,
               lg_ref, lb_ref,
               dproj_ref, gwout_ref, gpww_ref, gpoolw_ref, gvec_ref, gdw_ref,
               vext, gext, gsh, y_s):
    b = pl.program_id(0)
    i = pl.program_id(1)
    tm = TILE_M
    t0 = (n_tiles - 1 - i) * tm

    @pl.when(i == 0)
    def _():
        vext[tm:, :] = jnp.zeros((POOL_HALO, W_MIX), F32)
        gext[tm:, :] = jnp.zeros((CONV_HALO, W_MIX), F32)

    @pl.when((b == 0) & (i == 0))
    def _():
        gwout_ref[...] = jnp.zeros_like(gwout_ref)
        gpww_ref[...] = jnp.zeros_like(gpww_ref)
        gpoolw_ref[...] = jnp.zeros_like(gpoolw_ref)
        gvec_ref[...] = jnp.zeros_like(gvec_ref)
        gdw_ref[...] = jnp.zeros_like(gdw_ref)

    sub = tm // BWD_SUBTILES

    def dense_phase(r0):
        rs = slice(r0, r0 + sub)
        dhb = dh_ref[0, rs, :].astype(BF16)
        dy_a = lax.dot_general(dhb, wout_ref[0:W_MIX, :], NT_DIMS, preferred_element_type=F32)
        dy_b = lax.dot_general(dhb, wout_ref[W_MIX:, :], NT_DIMS, preferred_element_type=F32)

        a_gate = proj_ref[0, rs, W_MIX:2 * W_MIX]
        sig_a = jax.nn.sigmoid(a_gate)
        silu_a = a_gate * sig_a
        z = z_ref[0, rs, :]
        out_a = z * pools_ref[...]
        y_s[rs, 0:W_MIX] = (out_a * silu_a).astype(BF16)
        d_out_a = dy_a * silu_a
        d_ag = dy_a * out_a * (sig_a * (1.0 + a_gate * (1.0 - sig_a)))
        dproj_ref[0, rs, W_MIX:2 * W_MIX] = d_ag.astype(BF16)
        gvec_ref[V_POOL_SCALE] += _colsum8(d_out_a * z)
        dz = d_out_a * pools_ref[...]
        gvec_ref[V_POOL_B] += _colsum8(dz)
        t = _row_index((sub, POOL_GC), t0 + r0)
        n_ext = sub + POOL_HALO
        for g, w in enumerate(POOL_WINDOWS):
            lanes = slice(g * POOL_GC, (g + 1) * POOL_GC)
            dzg = dz[:, lanes].astype(BF16)
            gpoolw_ref[g] += lax.dot_general(dpool_ref[0, rs, lanes], dzg, TN_DIMS, preferred_element_type=F32)
            dd = lax.dot_general(dzg, poolw_ref[g].astype(BF16), NT_DIMS, preferred_element_type=F32)
            vext[rs, lanes] = dd * (1.0 / jnp.minimum(t + 1, w).astype(F32))
            s = vext[r0:r0 + n_ext, lanes]
            sh = 1
            while sh < w:
                s = s + pltpu.roll(s, n_ext - sh, 0)
                sh *= 2
            dproj_ref[0, rs, lanes] = (s[0:sub] - dd).astype(BF16)

        b_gate = proj_ref[0, rs, 4 * W_MIX:5 * W_MIX]
        sig_b = jax.nn.sigmoid(b_gate)
        silu_b = b_gate * sig_b
        ob = ob_ref[0, rs, :]
        y_s[rs, W_MIX:] = (ob * silu_b).astype(BF16)
        d_ob = dy_b * silu_b
        dproj_ref[0, rs, 4 * W_MIX:5 * W_MIX] = (dy_b * ob * (sig_b * (1.0 + b_gate * (1.0 - sig_b)))).astype(BF16)
        gvec_ref[V_PW_B] += _colsum8(d_ob)
        gwout_ref[...] += lax.dot_general(y_s[rs, :], dhb, TN_DIMS, preferred_element_type=F32)

        h1 = h1_ref[0, rs, :]
        mu = jnp.mean(h1, axis=-1, keepdims=True)
        xc = h1 - mu
        rstd = lax.rsqrt(jnp.mean(xc * xc, axis=-1, keepdims=True) + LN_EPS)
        xhat = xc * rstd
        h2 = xhat * lg_ref[...] + lb_ref[...]
        sig2 = jax.nn.sigmoid(h2)
        d_obb = d_ob.astype(BF16)
        gpww_ref[...] += lax.dot_general((h2 * sig2).astype(BF16), d_obb, TN_DIMS, preferred_element_type=F32)
        dh2 = lax.dot_general(d_obb, pww_ref[...], NT_DIMS, preferred_element_type=F32) * (sig2 * (1.0 + h2 * (1.0 - sig2)))
        gvec_ref[V_LN_G] += _colsum8(dh2 * xhat)
        gvec_ref[V_LN_B] += _colsum8(dh2)
        dxh = dh2 * lg_ref[...]
        dh1 = rstd * (dxh - jnp.mean(dxh, axis=-1, keepdims=True) - xhat * jnp.mean(dxh * xhat, axis=-1, keepdims=True))
        gvec_ref[V_CONV_B] += _colsum8(dh1)
        gext[rs, :] = dh1

    taps = _bwd_taps()
    cr = BWD_CONV_ROWS
    win = cr + CONV_HALO - SUBLANES

    def conv_phase(r0):
        for c0 in range(r0, r0 + sub, cr):
            for l0 in range(0, W_MIX, LANES):
                lanes = slice(l0, l0 + LANES)
                b_val = proj_ref[0, c0:c0 + cr, 2 * W_MIX + l0:2 * W_MIX + l0 + LANES]
                sg = jax.nn.sigmoid(proj_ref[0, c0:c0 + cr, 3 * W_MIX + l0:3 * W_MIX + l0 + LANES])
                h0c = b_val * sg
                acc = jnp.zeros((cr, LANES), F32)
                for s in range(SUBLANES):
                    gsh[...] = gext[c0 + s:c0 + s + win, lanes]
                    for q, k in taps[s]:
                        gq = gsh[SUBLANES * q:SUBLANES * q + cr, :]
                        acc = acc + gq * dw_ref[k:k + 1, lanes]
                        gdw_ref[k, :, lanes] += _colsum8(h0c * gq)
                d_bval = acc * sg
                dproj_ref[0, c0:c0 + cr, 2 * W_MIX + l0:2 * W_MIX + l0 + LANES] = d_bval.astype(BF16)
                dproj_ref[0, c0:c0 + cr, 3 * W_MIX + l0:3 * W_MIX + l0 + LANES] = (d_bval * b_val * (1.0 - sg)).astype(BF16)

    starts = [k * sub for k in reversed(range(BWD_SUBTILES))]
    for r0 in starts:
        dense_phase(r0)
    for r0 in starts:
        conv_phase(r0)
    vext[tm:, :] = vext[0:POOL_HALO, :]
    gext[tm:, :] = gext[0:CONV_HALO, :]


def _bwd1(dh, proj, h1, z, ob, dpool, w_out_b, pw_w_b, pool_w, pool_scale, conv_dw_f, ln_g, ln_b):
    bl, s, _ = dh.shape
    tm = TILE_M
    nt = s // tm
    tok = lambda n: pl.BlockSpec((1, tm, n), lambda b, i: (b, nt - 1 - i, 0))
    res = lambda shape: pl.BlockSpec(shape, lambda b, i: (0,) * len(shape))
    out_shapes = [(D_MODEL, D_MODEL), (W_MIX, W_MIX), (len(POOL_WINDOWS), POOL_GC, POOL_GC), (6, SUBLANES, W_MIX),
                  (CONV_WIDTH, SUBLANES, W_MIX)]
    return pl.pallas_call(
        functools.partial(_bwd1_body, nt),
        name="bwd1",
        grid=(bl, nt),
        in_specs=[tok(D_MODEL), tok(D_IN), tok(W_MIX), tok(W_MIX), tok(W_MIX), tok(W_MIX)] + [VMEM] * 7,
        out_specs=[tok(D_IN)] + [res(sh) for sh in out_shapes],
        out_shape=[jax.ShapeDtypeStruct((bl, s, D_IN), BF16)] + [jax.ShapeDtypeStruct(sh, F32) for sh in out_shapes],
        scratch_shapes=[pltpu.VMEM((tm + POOL_HALO, W_MIX), F32), pltpu.VMEM((tm + CONV_HALO, W_MIX), F32),
                        pltpu.VMEM((BWD_CONV_ROWS + CONV_HALO - SUBLANES, LANES), F32), pltpu.VMEM((tm, D_MODEL), BF16)],
        compiler_params=pltpu.CompilerParams(dimension_semantics=("arbitrary", "arbitrary"), vmem_limit_bytes=VMEM_LIMIT),
    )(dh, proj, h1, z, ob, dpool, w_out_b, pw_w_b, pool_w, _row(pool_scale), conv_dw_f, _row(ln_g), _row(ln_b))


MESH = pl.DeviceIdType.MESH
W_IN_BLK = D_IN // N_DEV
W_OUT_BLK = D_MODEL // N_DEV
PW_BLK = W_MIX // N_DEV
DW_BLK = W_MIX // N_DEV
N_PEER = N_DEV - 1


SLAB = 2 * W_IN_BLK
N_SLABS = 4
F_SIB, F_X, F_Y, F_ON, F_PASS_X, F_PASS_Y, F_PASS_D = range(7)
W_IN_ONLY, OTHERS = (0,), (1, 2, 3)


def _inproj_body(n_tiles, x_hbm, ng_ref, wint_ref, wout_ref, pww_ref, dw_ref,
                 proj_hbm, wt_hbm, woutf_hbm, pwwf_hbm, dwf_ref,
                 hn_s, wt_s, wout_s, pww_s, dw_blk, stage, xbuf, out_sems, x_sems, w_sems, send_sems, recv_sems):
    s = pl.program_id(0)
    i = pl.program_id(1)
    tm = TILE_IN
    x, y, c = lax.axis_index("x"), lax.axis_index("y"), lax.axis_index("c")
    sibling = (x, y, 1 - c)
    xn, yn, dg = (1 - x, y), (x, 1 - y), (1 - x, 1 - y)
    pick = lambda a, b: tuple(jnp.where(c == 1, p, q) for p, q in zip(a, b))
    on_from, on_to = pick(xn, yn), pick(yn, xn)

    def blocks(px, py, pc):
        k = 4 * px + 2 * py + pc
        rows = lambda ref, n: ref.at[pl.ds(pl.multiple_of(k * n, n), n), :]
        return (rows(wt_s, W_IN_BLK), rows(wout_s, W_OUT_BLK), rows(pww_s, PW_BLK), dw_blk.at[k])

    def copies(arrays, flow, block, to):
        refs = blocks(*block)
        return [pltpu.make_async_remote_copy(src_ref=refs[a], dst_ref=refs[a], send_sem=send_sems.at[a, flow],
                                             recv_sem=recv_sems.at[a, flow], device_id=to, device_id_type=MESH)
                for a in arrays]

    def start(arrays):
        for flow, to in ((F_SIB, sibling), (F_X, (*xn, c)), (F_Y, (*yn, c))):
            for cp in copies(arrays, flow, (x, y, c), to):
                cp.start()

    def direct(arrays):
        for flow, chip in ((F_X, xn), (F_Y, yn)):
            for cp in copies(arrays, flow, (*chip, c), sibling):
                cp.wait_recv()
        for cp in copies(arrays, F_ON, (*on_from, c), (*on_to, c)):
            cp.start()
        for flow, chip in ((F_PASS_X, xn), (F_PASS_Y, yn)):
            for cp in copies(arrays, flow, (*chip, c), sibling):
                cp.start()

    def diagonal(arrays):
        for cp in copies(arrays, F_ON, (*dg, c), sibling):
            cp.wait_recv()
        for cp in copies(arrays, F_PASS_D, (*dg, c), sibling):
            cp.start()

    def from_sibling(arrays, flow, chip):
        for cp in copies(arrays, flow, (*chip, 1 - c), sibling):
            cp.wait_recv()

    def finish(arrays):
        mine = (x, y, c)
        for flow, block in ((F_SIB, mine), (F_X, mine), (F_Y, mine), (F_ON, (*on_from, c)), (F_PASS_X, (*xn, c)),
                            (F_PASS_Y, (*yn, c)), (F_PASS_D, (*dg, c))):
            for cp in copies(arrays, flow, block, sibling):
                cp.wait_send()

    step = s * n_tiles + i
    par = step % 2
    first = step == 0

    @pl.when(first)
    def _():
        mine = blocks(x, y, c)
        mine[0][...] = wint_ref[...].astype(BF16)
        mine[1][...] = wout_ref[...].astype(BF16)
        mine[2][...] = pww_ref[...].astype(BF16)
        mine[3][...] = dw_ref[...]
        start(W_IN_ONLY)
        start(OTHERS)

    rows_i = pl.ds(pl.multiple_of(i * tm, tm), tm)
    tiles_per_seq = proj_hbm.shape[1] // tm

    def x_copy(k):
        src = x_hbm.at[k // tiles_per_seq, pl.ds(pl.multiple_of((k % tiles_per_seq) * tm, tm), tm), :]
        return pltpu.make_async_copy(src, xbuf.at[k % 2], x_sems.at[k % 2])

    @pl.when(first)
    def _():
        x_copy(0).start()

    @pl.when(s == 0)
    def _():
        x_copy(i).wait()

        @pl.when(i + 1 < n_tiles)
        def _():
            x_copy(i + 1).start()

        xt = xbuf[i % 2]
        r1 = lax.rsqrt(jnp.mean(xt * xt, axis=-1, keepdims=True) + RMS_EPS)
        hn_s[rows_i, :] = (xt * r1 * ng_ref[...]).astype(BF16)

    @pl.when(first)
    def _():
        from_sibling(W_IN_ONLY, F_SIB, (x, y))

    @pl.when((s == 1) & (i == 0))
    def _():
        direct(W_IN_ONLY)
        from_sibling(W_IN_ONLY, F_PASS_Y, yn)

    @pl.when((s == 2) & (i == 0))
    def _():
        from_sibling(W_IN_ONLY, F_PASS_X, xn)

    weights_out = [pltpu.make_async_copy(src, dst, w_sems.at[k])
                   for k, (src, dst) in enumerate(((wt_s, wt_hbm), (wout_s, woutf_hbm), (pww_s, pwwf_hbm)))]

    @pl.when((s == 3) & (i == 0))
    def _():
        diagonal(W_IN_ONLY)
        direct(OTHERS)
        from_sibling(W_IN_ONLY, F_PASS_D, dg)
        weights_out[0].start()

    def out_copy(parity, col0):
        dst = proj_hbm.at[i // tiles_per_seq, pl.ds(pl.multiple_of((i % tiles_per_seq) * tm, tm), tm),
                          pl.ds(pl.multiple_of(col0, LANES), SLAB)]
        return pltpu.make_async_copy(stage.at[parity], dst, out_sems.at[parity])

    @pl.when(step >= 2)
    def _():
        out_copy(par, 0).wait()

    chip = jnp.where(s < 2, 2 * x, 2 * (1 - x)) + jnp.where((s == 0) | (s == 2), y, 1 - y)
    slab = wt_s[pl.ds(pl.multiple_of(chip * SLAB, SLAB), SLAB), :]
    stage[par] = lax.dot_general(hn_s[rows_i, :], slab, NT_DIMS, preferred_element_type=F32)
    out_copy(par, chip * SLAB).start()

    @pl.when(step == N_SLABS * n_tiles - 1)
    def _():
        out_copy(par, 0).wait()
        out_copy(1 - par, 0).wait()
        from_sibling(OTHERS, F_SIB, (x, y))
        from_sibling(OTHERS, F_PASS_X, xn)
        from_sibling(OTHERS, F_PASS_Y, yn)
        diagonal(OTHERS)
        from_sibling(OTHERS, F_PASS_D, dg)
        for cp in weights_out[1:]:
            cp.start()
        finish(W_IN_ONLY)
        finish(OTHERS)
        for cp in weights_out:
            cp.wait()
        for k in range(N_DEV):
            dwf_ref[:, k * DW_BLK:(k + 1) * DW_BLK] = dw_blk[k]


def _inproj(x, norm_g, w_in_t, w_out, pw_w, conv_dw):
    bl, s, _ = x.shape
    tm = TILE_IN
    nt = s // tm
    n_tiles = bl * nt
    dma = pltpu.SemaphoreType.DMA
    return pl.pallas_call(
        functools.partial(_inproj_body, n_tiles),
        name="inproj",
        grid=(N_SLABS, n_tiles),
        in_specs=[pl.BlockSpec(memory_space=pl.ANY)] + [VMEM] * 5,
        out_specs=[pl.BlockSpec(memory_space=pl.ANY)] * 4 + [VMEM],
        out_shape=[jax.ShapeDtypeStruct((bl, s, D_IN), F32), jax.ShapeDtypeStruct((D_IN, D_MODEL), BF16),
                   jax.ShapeDtypeStruct((D_MODEL, D_MODEL), BF16), jax.ShapeDtypeStruct((W_MIX, W_MIX), BF16),
                   jax.ShapeDtypeStruct((CONV_WIDTH, W_MIX), F32)],
        scratch_shapes=[pltpu.VMEM((n_tiles * tm, D_MODEL), BF16), pltpu.VMEM((D_IN, D_MODEL), BF16),
                        pltpu.VMEM((D_MODEL, D_MODEL), BF16), pltpu.VMEM((W_MIX, W_MIX), BF16),
                        pltpu.VMEM((N_DEV, CONV_WIDTH, DW_BLK), F32), pltpu.VMEM((2, tm, SLAB), F32),
                        pltpu.VMEM((2, tm, D_MODEL), F32), dma((2,)), dma((2,)), dma((3,)), dma((4, 7)), dma((4, 7))],
        compiler_params=pltpu.CompilerParams(dimension_semantics=("arbitrary", "arbitrary"), vmem_limit_bytes=VMEM_LIMIT),
    )(x, _row(norm_g), w_in_t, w_out, pw_w, conv_dw)


B_PLAIN, B_ON, B_MERGED = range(3)


def _routes():
    x, y, c = lax.axis_index("x"), lax.axis_index("y"), lax.axis_index("c")
    xn, yn = (1 - x, y), (x, 1 - y)
    pick = lambda a, b: tuple(jnp.where(c == 1, p, q) for p, q in zip(a, b))
    return pick(xn, yn), pick(yn, xn), (1 - x, 1 - y), (x, y), c


def _chip_index(chip):
    return 2 * chip[0] + chip[1]


def _copy(src, dst, send_sems, recv_sems, a, k, to):
    return pltpu.make_async_remote_copy(src_ref=src, dst_ref=dst, send_sem=send_sems.at[a, k], recv_sem=recv_sems.at[a, k],
                                        device_id=to, device_id_type=MESH)


def _rsb_send(plain, on, rb_plain, rb_on, send_b, recv_b):
    first, _, _, _, c = _routes()
    for k, srcs, dsts in ((B_ON, on, rb_on), (B_PLAIN, plain, rb_plain)):
        for a, (s, d) in enumerate(zip(srcs, dsts)):
            _copy(s, d, send_b, recv_b, a, k, (*first, c)).start()


def _rsb_merge(second_sums, merged, rb_on, rb_merged, send_b, recv_b):
    _, second, _, _, c = _routes()
    for a, (own, m, got, dst) in enumerate(zip(second_sums, merged, rb_on, rb_merged)):
        _copy(got, got, send_b, recv_b, a, B_ON, (*second, c)).wait_recv()
        m[...] = (own[...] + got[...].astype(F32)).astype(BF16)
        _copy(m, dst, send_b, recv_b, a, B_MERGED, (*second, c)).start()


def _rsb_total(own_sums, plain, on, merged, rb_plain, rb_merged, send_b, recv_b):
    _, second, _, _, c = _routes()
    out = []
    for a, (own, p, g) in enumerate(zip(own_sums, rb_plain, rb_merged)):
        _copy(p, p, send_b, recv_b, a, B_PLAIN, (*second, c)).wait_recv()
        _copy(g, g, send_b, recv_b, a, B_MERGED, (*second, c)).wait_recv()
        out.append(own[...] + p[...].astype(F32) + g[...].astype(F32))
    for k, srcs in ((B_PLAIN, plain), (B_ON, on), (B_MERGED, merged)):
        for a, s in enumerate(srcs):
            _copy(s, s, send_b, recv_b, a, k, (*second, c)).wait_send()
    return out


R_LOSS, R_FINAL_G, R_POOL, R_CONV, R_LN = range(5)
VEC_ROWS = 16
DW_ROWS = 32
A_PART, A_X, A_Y, A_ON, A_HALF = range(5)
HALF_DTYPES = (F32, BF16, BF16)
STEP_SUMS, STEP_MERGE, STEP_HALF = 1, 3, 5


def _bwd2a_body(n_steps, x_ref, dproj_ref, ng_ref, gwout_hbm, gpww_hbm, loss_ref, gfg_ref, gvec_ref, gdw_ref, gpoolw_ref,
                win_own_ref, win_staged_ref, owout_ref, opww_ref, ovec_ref, odw_ref, opoolw_ref,
                acc, st_win, a_win, own_wout, own_pww, a_wout, a_pww, st_wout, st_pww, rb_wout, rb_pww, part_vec, part_dw,
                part_pw, sib_vec, sib_dw, sib_pw, res_vec, res_dw, res_pw, half_vec, half_dw, half_pw,
                own_sems, send_a, recv_a, send_b, recv_b, send_r, recv_r, send_w, recv_w):
    step = pl.program_id(0) * pl.num_programs(1) + pl.program_id(1)
    x, y, c = lax.axis_index("x"), lax.axis_index("y"), lax.axis_index("c")
    sibling = (x, y, 1 - c)
    xn, yn = (1 - x, y), (x, 1 - y)
    first, second, diag, own, _ = _routes()
    rows = lambda ref, k, n: ref.at[pl.ds(pl.multiple_of(k * n, n), n), :]

    blocks = lambda k: (rows(gwout_hbm, k, W_OUT_BLK), rows(gpww_hbm, k, PW_BLK))
    mine = lambda chip: (own_wout.at[chip], own_pww.at[chip])
    mine_copies = lambda chip: [pltpu.make_async_copy(s, d, own_sems.at[a, chip])
                                for a, (s, d) in enumerate(zip(blocks(2 * chip + c), mine(chip)))]
    landed = lambda chip: (a_wout.at[chip], a_pww.at[chip])
    staged = lambda k: (st_wout.at[k], st_pww.at[k])
    arrived = lambda k: (rb_wout.at[k], rb_pww.at[k])

    parts, sibs, halves = (part_vec, part_dw, part_pw), (sib_vec, sib_dw, sib_pw), (half_vec, half_dw, half_pw)
    results = (res_vec, res_dw, res_pw)
    half_rows = lambda ref: pl.ds(pl.multiple_of(c * (ref.shape[0] // 2), SUBLANES), ref.shape[0] // 2)
    slot = lambda chip: tuple(h.at[_chip_index(chip)] for h in halves)

    @pl.when(step == 0)
    def _():
        acc[...] = jnp.zeros_like(acc)
        sum8 = lambda v: jnp.sum(v, axis=0, keepdims=True)
        part_vec[...] = jnp.zeros((VEC_ROWS, D_MODEL), F32)
        part_vec[R_LOSS:R_LOSS + 1, :] = jnp.broadcast_to(loss_ref[0:1, 0:1], (1, D_MODEL))
        part_vec[R_FINAL_G:R_FINAL_G + 1, :] = sum8(gfg_ref[...])
        for row, (lo, hi) in ((R_POOL, (V_POOL_B, V_POOL_SCALE)), (R_CONV, (V_CONV_B, V_LN_G)), (R_LN, (V_LN_B, V_PW_B))):
            part_vec[row:row + 1, 0:W_MIX] = sum8(gvec_ref[lo])
            part_vec[row:row + 1, W_MIX:] = sum8(gvec_ref[hi])
        part_dw[0:CONV_WIDTH, :] = jnp.sum(gdw_ref[...], axis=1)
        part_dw[CONV_WIDTH:, :] = jnp.zeros((DW_ROWS - CONV_WIDTH, W_MIX), F32)
        part_pw[...] = gpoolw_ref[...].reshape(len(POOL_WINDOWS) * POOL_GC, POOL_GC)
        for a, (p, s) in enumerate(zip(parts, sibs)):
            _copy(p, s, send_r, recv_r, a, A_PART, sibling).start()
        for chip in range(4):
            for a, (s, d) in enumerate(zip(blocks(2 * chip + (1 - c)), landed(chip))):
                _copy(s, d, send_a, recv_a, a, chip, sibling).start()
            for cp in mine_copies(chip):
                cp.start()

    x_t = x_ref[0]
    r1 = lax.rsqrt(jnp.mean(x_t * x_t, axis=-1, keepdims=True) + RMS_EPS)
    hn = (x_t * r1 * ng_ref[...]).astype(BF16)
    acc[...] += lax.dot_general(dproj_ref[0], hn, TN_DIMS, preferred_element_type=F32)

    @pl.when(step == STEP_SUMS)
    def _():
        for a, (p, s, h) in enumerate(zip(parts, sibs, slot(own))):
            _copy(p, s, send_r, recv_r, a, A_PART, sibling).wait_recv()
            h[...] = (p[half_rows(p), :] + s[half_rows(s), :]).astype(h.dtype)
            for k, to in ((A_X, xn), (A_Y, yn)):
                _copy(h, h, send_r, recv_r, a, k, (*to, c)).start()
        for chip in range(4):
            for cp in mine_copies(chip):
                cp.wait()
            for a, (m, l) in enumerate(zip(mine(chip), landed(chip))):
                _copy(l, l, send_a, recv_a, a, chip, sibling).wait_recv()
                l[...] = m[...] + l[...]
        for k, chip in ((B_ON, diag), (B_PLAIN, first)):
            for s, l in zip(staged(k), landed(_chip_index(chip))):
                s[...] = l[...].astype(BF16)
        _rsb_send(staged(B_PLAIN), staged(B_ON), arrived(B_PLAIN), arrived(B_ON), send_b, recv_b)

    @pl.when(step == STEP_MERGE)
    def _():
        _rsb_merge(landed(_chip_index(second)), staged(B_MERGED), arrived(B_ON), arrived(B_MERGED), send_b, recv_b)
        for a in range(3):
            for k, chip in ((A_X, xn), (A_Y, yn)):
                h = slot(chip)[a]
                _copy(h, h, send_r, recv_r, a, k, sibling).wait_recv()
            h = slot(first)[a]
            _copy(h, h, send_r, recv_r, a, A_ON, (*second, c)).start()

    @pl.when(step == STEP_HALF)
    def _():
        for a, (r, hs) in enumerate(zip(results, halves)):
            h = slot(diag)[a]
            _copy(h, h, send_r, recv_r, a, A_ON, sibling).wait_recv()
            mine = r.at[half_rows(r), :]
            chip_sums = [hs[k].astype(F32) for k in range(4)]
            mine[...] = (chip_sums[0] + chip_sums[1]) + (chip_sums[2] + chip_sums[3])
            _copy(mine, mine, send_r, recv_r, a, A_HALF, sibling).start()

    @pl.when(step == n_steps - 1)
    def _():
        block = lambda k: rows(acc, k, W_IN_BLK)
        w_copy = lambda chip: _copy(st_win.at[chip], a_win.at[chip], send_w, recv_w, 0, chip, sibling)
        for chip in range(4):
            st_win[chip] = block(2 * chip + (1 - c))[...].astype(BF16)
            w_copy(chip).start()

        owout_ref[...], opww_ref[...] = _rsb_total(landed(_chip_index(own)), staged(B_PLAIN), staged(B_ON), staged(B_MERGED),
                                                   arrived(B_PLAIN), arrived(B_MERGED), send_b, recv_b)
        for a, (r, p, s) in enumerate(zip(results, parts, sibs)):
            theirs = r.at[pl.ds(pl.multiple_of((1 - c) * (r.shape[0] // 2), SUBLANES), r.shape[0] // 2), :]
            _copy(theirs, theirs, send_r, recv_r, a, A_HALF, sibling).wait_recv()
            _copy(p, s, send_r, recv_r, a, A_PART, sibling).wait_send()
            for k, h in ((A_X, slot(own)[a]), (A_Y, slot(own)[a]), (A_ON, slot(first)[a])):
                _copy(h, h, send_r, recv_r, a, k, sibling).wait_send()
            mine = r.at[half_rows(r), :]
            _copy(mine, mine, send_r, recv_r, a, A_HALF, sibling).wait_send()
        for chip in range(4):
            for a, (s, d) in enumerate(zip(blocks(2 * chip + (1 - c)), landed(chip))):
                _copy(s, d, send_a, recv_a, a, chip, sibling).wait_send()
        ovec_ref[...] = res_vec[...]
        opoolw_ref[...] = res_pw[...]
        for k in range(N_DEV):
            @pl.when(4 * x + 2 * y + c == k)
            def _():
                odw_ref[...] = res_dw[0:CONV_WIDTH, k * DW_BLK:(k + 1) * DW_BLK]

        for chip in range(4):
            w_copy(chip).wait()
        chip_sum = lambda chip: block(2 * _chip_index(chip) + c)[...] + a_win[_chip_index(chip)].astype(F32)
        win_own_ref[...] = chip_sum(own)
        for j, chip in enumerate((xn, yn, diag)):
            win_staged_ref[j] = chip_sum(chip).astype(BF16)


def _bwd2a(x, dproj, norm_g, gwout, gpww, loss8, gfg8, gvec, gdw8, gpoolw):
    bl, s, _ = x.shape
    tm = TILE_M
    nt = s // tm
    tok = lambda n: pl.BlockSpec((1, tm, n), lambda b, i: (b, i, 0))
    n_pw = len(POOL_WINDOWS) * POOL_GC
    win = (W_IN_BLK, D_MODEL)
    blk = ((W_OUT_BLK, D_MODEL), (PW_BLK, W_MIX))
    rep = ((VEC_ROWS, D_MODEL), (DW_ROWS, W_MIX), (n_pw, POOL_GC))
    half = lambda sh: (sh[0] // 2,) + sh[1:]
    dma = pltpu.SemaphoreType.DMA
    assert bl * nt > STEP_HALF + 1, "the exchanges' phases need their grid steps"
    return pl.pallas_call(
        functools.partial(_bwd2a_body, bl * nt),
        name="bwd2a",
        grid=(bl, nt),
        in_specs=[tok(D_MODEL), tok(D_IN), VMEM] + [pl.BlockSpec(memory_space=pl.ANY)] * 2 + [VMEM] * 5,
        out_specs=[VMEM] * 7,
        out_shape=[jax.ShapeDtypeStruct(win, F32), jax.ShapeDtypeStruct((3,) + win, BF16)]
        + [jax.ShapeDtypeStruct(sh, F32) for sh in blk + (rep[0], (CONV_WIDTH, DW_BLK), rep[2])],
        scratch_shapes=[pltpu.VMEM((D_IN, D_MODEL), F32), pltpu.VMEM((4,) + win, BF16), pltpu.VMEM((4,) + win, BF16)]
        + [pltpu.VMEM((4,) + sh, F32) for sh in blk] * 2
        + [pltpu.VMEM((3,) + sh, BF16) for sh in blk] * 2
        + [pltpu.VMEM(sh, F32) for sh in rep] * 3
        + [pltpu.VMEM((4,) + half(sh), dt) for sh, dt in zip(rep, HALF_DTYPES)]
        + [dma((2, 4)), dma((2, 4)), dma((2, 4)), dma((2, 3)), dma((2, 3)), dma((3, 5)), dma((3, 5)), dma((1, 4)), dma((1, 4))],
        compiler_params=pltpu.CompilerParams(dimension_semantics=("arbitrary", "arbitrary"), vmem_limit_bytes=VMEM_LIMIT),
    )(x, dproj, _row(norm_g), gwout, gpww, loss8, gfg8, gvec, gdw8, gpoolw)


HBM = pl.BlockSpec(memory_space=pltpu.HBM)
SEM = pl.BlockSpec(memory_space=pltpu.SEMAPHORE)
EFFECT = pltpu.SideEffectType.DATAFLOW_SIDE_EFFECTING


def _owner_copies(src_ref, land_ref, send_sems, recv_sems):
    x, y, c = lax.axis_index("x"), lax.axis_index("y"), lax.axis_index("c")
    owners = ((1 - x, y, c), (x, 1 - y, c), (1 - x, 1 - y, c))
    return [pltpu.make_async_remote_copy(src_ref=src_ref.at[j], dst_ref=land_ref.at[j], send_sem=send_sems.at[j],
                                         recv_sem=recv_sems.at[j], device_id=owners[j], device_id_type=MESH)
            for j in range(3)]


def _rs_start(staged):
    def body(src_ref, land_ref, send_sems, recv_sems, src_thru, land_thru, token):
        for cp in _owner_copies(src_ref, land_ref, send_sems, recv_sems):
            cp.start()
        token[...] = jnp.zeros_like(token)

    dma = pltpu.SemaphoreType.DMA
    return pl.pallas_call(
        body, name="rs_start",
        out_shape=(dma((3,)), dma((3,)), pltpu.HBM(staged.shape, staged.dtype), pltpu.HBM(staged.shape, staged.dtype),
                   jax.ShapeDtypeStruct((SUBLANES, LANES), F32)),
        in_specs=(HBM, HBM), out_specs=(SEM, SEM, HBM, HBM, VMEM), input_output_aliases={0: 2, 1: 3},
        compiler_params=pltpu.CompilerParams(has_side_effects=EFFECT),
    )(pltpu.with_memory_space_constraint(staged, pltpu.HBM),
      pltpu.with_memory_space_constraint(lax.empty(staged.shape, staged.dtype), pltpu.HBM))


def _rs_wait(send_sems, recv_sems, src_thru, land_thru, after):
    def body(src_ref, land_ref, send_sems, recv_sems, after_ref, src_dead, got_ref):
        for cp in _owner_copies(src_ref, land_ref, send_sems, recv_sems):
            cp.wait_send()
            cp.wait_recv()

    return pl.pallas_call(
        body, name="rs_wait",
        out_shape=(pltpu.HBM(src_thru.shape, src_thru.dtype), pltpu.HBM(land_thru.shape, land_thru.dtype)),
        in_specs=(HBM, HBM, SEM, SEM, pl.BlockSpec(memory_space=pl.ANY)), out_specs=(HBM, HBM),
        input_output_aliases={0: 0, 1: 1},
        compiler_params=pltpu.CompilerParams(has_side_effects=EFFECT),
    )(src_thru, land_thru, send_sems, recv_sems, after)[1]


def _bwd2b_body(x_ref, dh_ref, dproj_ref, ng_ref, wint_ref, token_ref, gx_ref, gng_ref):
    @pl.when((pl.program_id(0) == 0) & (pl.program_id(1) == 0))
    def _():
        gng_ref[...] = jnp.zeros_like(gng_ref) + token_ref[0:1, 0:1]

    x_t = x_ref[0]
    r1 = lax.rsqrt(jnp.mean(x_t * x_t, axis=-1, keepdims=True) + RMS_EPS)
    xr = x_t * r1
    dhn = jnp.dot(dproj_ref[0], wint_ref[...], preferred_element_type=F32)
    gng_ref[...] += _colsum8(dhn * xr)
    gy = dhn * ng_ref[...]
    gx_ref[0] = dh_ref[0] + r1 * (gy - xr * jnp.mean(gy * xr, axis=-1, keepdims=True))


def _bwd2b(x, dh, dproj, norm_g, w_in_t, token):
    bl, s, _ = x.shape
    tm = TILE_M
    tok = lambda n: pl.BlockSpec((1, tm, n), lambda b, i: (b, i, 0))
    return pl.pallas_call(
        functools.partial(_bwd2b_body),
        name="bwd2b",
        grid=(bl, s // tm),
        in_specs=[tok(D_MODEL), tok(D_MODEL), tok(D_IN), VMEM, VMEM, VMEM],
        out_specs=[tok(D_MODEL), pl.BlockSpec((SUBLANES, D_MODEL), lambda b, i: (0, 0))],
        out_shape=[jax.ShapeDtypeStruct((bl, s, D_MODEL), F32), jax.ShapeDtypeStruct((SUBLANES, D_MODEL), F32)],
        compiler_params=pltpu.CompilerParams(dimension_semantics=("arbitrary", "arbitrary"), vmem_limit_bytes=VMEM_LIMIT),
    )(x, dh, dproj, _row(norm_g), w_in_t, token)


def _partial_copies(part_ref, land_ref, send_sems, recv_sems):
    x, y, c = lax.axis_index("x"), lax.axis_index("y"), lax.axis_index("c")
    other_chips = [(1 - x, y), (x, 1 - y), (1 - x, 1 - y)]
    peers = [(x, y, 1 - c)] + [(*chip, c) for chip in other_chips] + [(*chip, 1 - c) for chip in other_chips]
    return [pltpu.make_async_remote_copy(src_ref=part_ref, dst_ref=land_ref.at[4 * x + 2 * y + c], send_sem=send_sems.at[r],
                                         recv_sem=recv_sems.at[r], device_id=peers[r], device_id_type=MESH)
            for r in range(N_PEER)]


def _partials_start(part):
    def body(part_ref, land_ref, send_sems, recv_sems, part_thru, land_thru):
        for cp in _partial_copies(part_ref, land_ref, send_sems, recv_sems):
            cp.start()

    dma = pltpu.SemaphoreType.DMA
    land = (N_DEV,) + part.shape
    return pl.pallas_call(
        body, name="partials_start",
        out_shape=(dma((N_PEER,)), dma((N_PEER,)), pltpu.HBM(part.shape, part.dtype), pltpu.HBM(land, part.dtype)),
        in_specs=(HBM, HBM), out_specs=(SEM, SEM, HBM, HBM), input_output_aliases={0: 2, 1: 3},
        compiler_params=pltpu.CompilerParams(has_side_effects=EFFECT),
    )(pltpu.with_memory_space_constraint(part, pltpu.HBM),
      pltpu.with_memory_space_constraint(lax.empty(land, part.dtype), pltpu.HBM))


def _partials_wait(send_sems, recv_sems, part_thru, land_thru, after):
    def body(part_ref, land_ref, send_sems, recv_sems, after_ref, part_out, land_out):
        for cp in _partial_copies(part_ref, land_ref, send_sems, recv_sems):
            cp.wait_send()
            cp.wait_recv()

    return pl.pallas_call(
        body, name="partials_wait",
        out_shape=(pltpu.HBM(part_thru.shape, part_thru.dtype), pltpu.HBM(land_thru.shape, land_thru.dtype)),
        in_specs=(HBM, HBM, SEM, SEM, pl.BlockSpec(memory_space=pl.ANY)), out_specs=(HBM, HBM),
        input_output_aliases={0: 0, 1: 1},
        compiler_params=pltpu.CompilerParams(has_side_effects=EFFECT),
    )(part_thru, land_thru, send_sems, recv_sems, after)


ADAM_LR = 0.001
ADAM_B1 = 0.9
ADAM_B2 = 0.999
ADAM_EPS = 1e-08
ADAM_WD = 0.01
ADAM_STEP = 10


PARAMS = ("norm_g", "w_in", "pool_w", "pool_b", "pool_scale", "conv_dw", "conv_b", "ln_g", "ln_b", "pw_w", "pw_b", "w_out",
          "final_g")


LATE_PARAMS = ("norm_g",)
MAIN_PARAMS = tuple(p for p in PARAMS if p not in LATE_PARAMS)


def _adamw_refs(params, n_sources, refs):
    n = len(params)
    ws, ms, vs = (dict(zip(params, refs[k * n:(k + 1) * n])) for k in range(3))
    outs = refs[3 * n + n_sources:]
    gs, ds, m2s, v2s = (dict(zip(params, outs[k * n:(k + 1) * n])) for k in range(4))

    def update(name, g, at=slice(None)):
        m = ADAM_B1 * ms[name][at] + (1.0 - ADAM_B1) * g
        v = ADAM_B2 * vs[name][at] + (1.0 - ADAM_B2) * (g * g)
        m_hat = m / (1.0 - ADAM_B1 ** ADAM_STEP)
        v_hat = v / (1.0 - ADAM_B2 ** ADAM_STEP)
        gs[name][at] = g
        ds[name][at] = -ADAM_LR * (m_hat / (jnp.sqrt(v_hat) + ADAM_EPS) + ADAM_WD * ws[name][at])
        m2s[name][at] = m
        v2s[name][at] = v

    return update, refs[3 * n:3 * n + n_sources], outs[4 * n:]


def _adamw_late_body(*refs):
    update, (gng_ref, gng_others_ref), _ = _adamw_refs(LATE_PARAMS, 2, refs)
    me = 4 * lax.axis_index("x") + 2 * lax.axis_index("y") + lax.axis_index("c")
    g_norm = jnp.zeros(gng_ref.shape, F32)
    for k in range(N_DEV):
        g_norm = g_norm + jnp.where(me == k, gng_ref[...], gng_others_ref[k])
    update("norm_g", jnp.sum(g_norm, axis=0, keepdims=True))


def _adamw_body(*refs):
    update, sources, (loss_ref,) = _adamw_refs(MAIN_PARAMS, 7, refs)
    gwin_ref, gwin_others_ref, gwout_ref, gpww_ref, vec_ref, gdw_ref, gpoolw_ref = sources
    ws = dict(zip(MAIN_PARAMS, refs))
    row = lambda r, lo, hi: vec_ref[r:r + 1, lo:hi]
    loss_ref[...] = row(R_LOSS, 0, 1)
    g_w_in = gwin_ref[...]
    for j in range(3):
        g_w_in = g_w_in + gwin_others_ref[j].astype(F32)
    update("w_in", g_w_in)
    update("pool_w", gpoolw_ref[...].reshape(ws["pool_w"].shape))
    for g in range(len(POOL_WINDOWS)):
        update("pool_b", row(R_POOL, g * POOL_GC, (g + 1) * POOL_GC), at=slice(g, g + 1))
    update("pool_scale", row(R_POOL, W_MIX, 2 * W_MIX))
    update("conv_dw", gdw_ref[...])
    update("conv_b", row(R_CONV, 0, W_MIX))
    update("ln_g", row(R_CONV, W_MIX, 2 * W_MIX))
    update("ln_b", row(R_LN, 0, W_MIX))
    update("pw_w", gpww_ref[...])
    update("pw_b", row(R_LN, W_MIX, 2 * W_MIX))
    update("w_out", gwout_ref[...])
    update("final_g", row(R_FINAL_G, 0, D_MODEL))


def _adamw(body, name, params, ws, ms, vs, grad_sources, extra_out_shapes=()):
    n = len(params)
    flat = lambda a: a.reshape(1, -1) if a.ndim == 1 else a
    args = [flat(d[p]) for d in (ws, ms, vs) for p in params]
    outs = pl.pallas_call(
        body,
        name=name,
        in_specs=[VMEM] * (3 * n + len(grad_sources)),
        out_specs=[VMEM] * (4 * n + len(extra_out_shapes)),
        out_shape=[jax.ShapeDtypeStruct(a.shape, F32) for a in args[:n]] * 4 + list(extra_out_shapes),
        compiler_params=pltpu.CompilerParams(vmem_limit_bytes=VMEM_LIMIT),
    )(*args, *grad_sources)
    groups = tuple({p: outs[k * n + j].reshape(ws[p].shape) for j, p in enumerate(params)} for k in range(4))
    return groups, outs[4 * n:]


def kernel(x, norm_g, w_in, pool_w, pool_b, pool_scale, conv_dw, conv_b, ln_g, ln_b, pw_w, pw_b, w_out, final_g, loss_target, m_norm_g, m_w_in, m_pool_w, m_pool_b, m_pool_scale, m_conv_dw, m_conv_b, m_ln_g, m_ln_b, m_pw_w, m_pw_b, m_w_out, m_final_g, v_norm_g, v_w_in, v_pool_w, v_pool_b, v_pool_scale, v_conv_dw, v_conv_b, v_ln_g, v_ln_b, v_pw_w, v_pw_b, v_w_out, v_final_g):
    proj, w_in_t, w_out_b, pw_w_b, conv_dw_f = _inproj(x, norm_g, w_in.T, w_out, pw_w, conv_dw)
    h1, z, ob, dpool, dh, loss8, gfg8 = _fwd(x, loss_target, proj, pool_w, pool_b, pool_scale, conv_dw_f,
                                             conv_b, ln_g, ln_b, pw_w_b, pw_b, w_out_b, final_g)
    dproj, gwout, gpww, gpoolw, gvec, gdw8 = _bwd1(dh, proj, h1, z, ob, dpool, w_out_b, pw_w_b, pool_w, pool_scale,
                                                   conv_dw_f, ln_g, ln_b)
    win_own, win_staged, g_w_out, g_pw_w, vec, g_conv_dw, poolw_sum = _bwd2a(x, dproj, norm_g, gwout, gpww, loss8, gfg8, gvec,
                                                                             gdw8, gpoolw)
    send_sems, recv_sems, staged_thru, land_thru, token = _rs_start(win_staged)
    grad_x, gng8 = _bwd2b(x, dh, dproj, norm_g, w_in_t, token)
    ng_send, ng_recv, gng8, gng_land = _partials_start(gng8)
    win_others = _rs_wait(send_sems, recv_sems, staged_thru, land_thru, gng8)

    weights = dict(norm_g=norm_g, w_in=w_in.T, pool_w=pool_w, pool_b=pool_b, pool_scale=pool_scale, conv_dw=conv_dw, conv_b=conv_b,
                   ln_g=ln_g, ln_b=ln_b, pw_w=pw_w, pw_b=pw_b, w_out=w_out, final_g=final_g)
    m_in = dict(norm_g=m_norm_g, w_in=m_w_in.T, pool_w=m_pool_w, pool_b=m_pool_b, pool_scale=m_pool_scale, conv_dw=m_conv_dw,
                conv_b=m_conv_b, ln_g=m_ln_g, ln_b=m_ln_b, pw_w=m_pw_w, pw_b=m_pw_b, w_out=m_w_out, final_g=m_final_g)
    v_in = dict(norm_g=v_norm_g, w_in=v_w_in.T, pool_w=v_pool_w, pool_b=v_pool_b, pool_scale=v_pool_scale, conv_dw=v_conv_dw,
                conv_b=v_conv_b, ln_g=v_ln_g, ln_b=v_ln_b, pw_w=v_pw_w, pw_b=v_pw_b, w_out=v_w_out, final_g=v_final_g)
    main, (loss,) = _adamw(functools.partial(_adamw_body), "adamw", MAIN_PARAMS, weights, m_in, v_in,
                           (win_own, win_others, g_w_out, g_pw_w, vec, g_conv_dw, poolw_sum), [jax.ShapeDtypeStruct((1, 1), F32)])
    gng8, gng_others = _partials_wait(ng_send, ng_recv, gng8, gng_land, main[0]["w_out"])
    late, _ = _adamw(functools.partial(_adamw_late_body), "adamw_norm_g", LATE_PARAMS, weights, m_in, v_in, (gng8, gng_others))
    groups = [[(late[k] if p in LATE_PARAMS else main[k])[p] for p in PARAMS] for k in range(4)]
    k_in = PARAMS.index("w_in")
    for group in groups:
        group[k_in] = group[k_in].T
    grads, delta, new_m, new_v = groups
    return (loss.reshape(()), grad_x, *grads, *delta, *new_m, *new_v)
```

```python
import functools

import jax
import jax.numpy as jnp
from jax import lax
from jax.experimental import pallas as pl
from jax.experimental.pallas import tpu as pltpu

F32 = jnp.float32
BF16 = jnp.bfloat16

D_MODEL = 1024
W_MIX = 512
D_IN = 5 * W_MIX
POOL_WINDOWS = (2, 4, 8, 16)
POOL_GC = 128
CONV_WIDTH = 31
RMS_EPS = 1e-6
LN_EPS = 1e-5
N_DEV = 8

LANES = 128
SUBLANES = 8
TILE_M = 512
TILE_IN = 1024
POOL_HALO = 16
CONV_HALO = 32
CONV_ROWS = 128
BWD_CONV_ROWS = 64
BWD_SUBTILES = 2
VMEM_LIMIT = 56 * 1024 * 1024

VMEM = pl.BlockSpec(memory_space=pltpu.VMEM)


def _silu(v):
    return v * jax.nn.sigmoid(v)


def _colsum8(v):
    m, n = v.shape
    return jnp.sum(v.reshape(m // SUBLANES, SUBLANES, n), axis=0)


def _row_index(shape, t0):
    return lax.broadcasted_iota(jnp.int32, shape, 0) + t0


def _conv_taps():
    taps = {s: [] for s in range(SUBLANES)}
    for k in range(CONV_WIDTH):
        e = k + CONV_HALO - (CONV_WIDTH - 1)
        taps[e % SUBLANES].append((e // SUBLANES, k))
    return taps


def _fwd_body(x_ref, tgt_ref, proj_ref, poolw_ref, poolb_ref, pools_ref, dw_ref, cb_ref, lg_ref, lb_ref,
              pww_hbm, pwb_ref, wout_hbm, fg_ref,
              h1_ref, z_ref, ob_ref, dpool_ref, dh_ref, loss_ref, gfg_ref,
              uext, hext, y_s, pww_ref, wout_ref, w_sems):
    b = pl.program_id(0)
    i = pl.program_id(1)
    tm = TILE_M
    first = (b == 0) & (i == 0)
    weights_in = [pltpu.make_async_copy(src, dst, w_sems.at[k])
                  for k, (src, dst) in enumerate(((pww_hbm, pww_ref), (wout_hbm, wout_ref)))]

    @pl.when(first)
    def _():
        for cp in weights_in:
            cp.start()

    @pl.when(i == 0)
    def _():
        uext[0:POOL_HALO, :] = jnp.zeros((POOL_HALO, W_MIX), F32)
        hext[0:CONV_HALO, :] = jnp.zeros((CONV_HALO, W_MIX), F32)
        hext[CONV_HALO + tm:, :] = jnp.zeros((SUBLANES, W_MIX), F32)

    @pl.when((b == 0) & (i == 0))
    def _():
        loss_ref[...] = jnp.zeros_like(loss_ref)
        gfg_ref[...] = jnp.zeros_like(gfg_ref)

    uext[POOL_HALO:, :] = proj_ref[0, :, 0:W_MIX]
    t = _row_index((tm, POOL_GC), i * tm)
    for g, w in enumerate(POOL_WINDOWS):
        lanes = slice(g * POOL_GC, (g + 1) * POOL_GC)
        e = uext[:, lanes]
        s = e
        sh = 1
        while sh < w:
            s = s + pltpu.roll(s, sh, 0)
            sh *= 2
        inv_cnt = 1.0 / jnp.minimum(t + 1, w).astype(F32)
        d = (s[POOL_HALO:] * inv_cnt - e[POOL_HALO:]).astype(BF16)
        dpool_ref[0, :, lanes] = d
        z = jnp.dot(d, poolw_ref[g].astype(BF16), preferred_element_type=F32) + poolb_ref[:, lanes]
        z_ref[0, :, lanes] = z
        a_gate = proj_ref[0, :, W_MIX + g * POOL_GC:W_MIX + (g + 1) * POOL_GC]
        y_s[:, lanes] = (z * pools_ref[:, lanes] * _silu(a_gate)).astype(BF16)

    hext[CONV_HALO:CONV_HALO + tm, :] = proj_ref[0, :, 2 * W_MIX:3 * W_MIX] * jax.nn.sigmoid(proj_ref[0, :, 3 * W_MIX:4 * W_MIX])
    taps = _conv_taps()
    for c0 in range(0, tm, CONV_ROWS):
        for l0 in range(0, W_MIX, LANES):
            lanes = slice(l0, l0 + LANES)
            acc = jnp.zeros((CONV_ROWS, LANES), F32) + cb_ref[:, lanes]
            for s in range(SUBLANES):
                part = jnp.zeros((CONV_ROWS + SUBLANES, LANES), F32)
                for q, k in taps[s]:
                    r0 = c0 + SUBLANES * q
                    part = part + hext[r0:r0 + CONV_ROWS + SUBLANES, lanes] * dw_ref[k:k + 1, lanes]
                acc = acc + part[s:s + CONV_ROWS]
            h1_ref[0, c0:c0 + CONV_ROWS, lanes] = acc
    h1 = h1_ref[0]
    mu = jnp.mean(h1, axis=-1, keepdims=True)
    xc = h1 - mu
    rstd = lax.rsqrt(jnp.mean(xc * xc, axis=-1, keepdims=True) + LN_EPS)
    h3 = _silu(xc * rstd * lg_ref[...] + lb_ref[...])

    @pl.when(first)
    def _():
        weights_in[0].wait()

    ob = jnp.dot(h3.astype(BF16), pww_ref[...], preferred_element_type=F32) + pwb_ref[...]
    ob_ref[0] = ob
    y_s[:, W_MIX:] = (ob * _silu(proj_ref[0, :, 4 * W_MIX:5 * W_MIX])).astype(BF16)

    @pl.when(first)
    def _():
        weights_in[1].wait()

    h = x_ref[0] + jnp.dot(y_s[...], wout_ref[...], preferred_element_type=F32)
    r2 = lax.rsqrt(jnp.mean(h * h, axis=-1, keepdims=True) + RMS_EPS)
    hr = h * r2
    err = hr * fg_ref[...] - tgt_ref[0]
    loss_ref[...] += jnp.sum(err * err) * (0.5 / D_MODEL)
    dout = err * (1.0 / D_MODEL)
    gfg_ref[...] += _colsum8(dout * hr)
    gy = dout * fg_ref[...]
    dh_ref[0] = r2 * (gy - hr * jnp.mean(gy * hr, axis=-1, keepdims=True))

    uext[0:POOL_HALO, :] = uext[tm:tm + POOL_HALO, :]
    hext[0:CONV_HALO, :] = hext[tm:tm + CONV_HALO, :]


def _row(v):
    return v.reshape(1, -1)


def _fwd(x, tgt, proj, pool_w, pool_b, pool_scale, conv_dw_f, conv_b, ln_g, ln_b, pw_w_b, pw_b, w_out_b, final_g):
    bl, s, _ = x.shape
    tm = TILE_M
    tok = lambda n: pl.BlockSpec((1, tm, n), lambda b, i: (b, i, 0))
    acc = lambda n: pl.BlockSpec((SUBLANES, n), lambda b, i: (0, 0))
    act = lambda n: jax.ShapeDtypeStruct((bl, s, n), F32)
    hbm = pl.BlockSpec(memory_space=pl.ANY)
    return pl.pallas_call(
        functools.partial(_fwd_body),
        name="fwd",
        grid=(bl, s // tm),
        in_specs=[tok(D_MODEL), tok(D_MODEL), tok(D_IN)] + [VMEM] * 7 + [hbm, VMEM, hbm, VMEM],
        out_specs=[tok(W_MIX), tok(W_MIX), tok(W_MIX), tok(W_MIX), tok(D_MODEL), acc(LANES), acc(D_MODEL)],
        out_shape=[act(W_MIX), act(W_MIX), act(W_MIX), jax.ShapeDtypeStruct((bl, s, W_MIX), BF16), act(D_MODEL),
                   jax.ShapeDtypeStruct((SUBLANES, LANES), F32), jax.ShapeDtypeStruct((SUBLANES, D_MODEL), F32)],
        scratch_shapes=[pltpu.VMEM((POOL_HALO + tm, W_MIX), F32), pltpu.VMEM((CONV_HALO + tm + SUBLANES, W_MIX), F32),
                        pltpu.VMEM((tm, D_MODEL), BF16), pltpu.VMEM(pw_w_b.shape, BF16), pltpu.VMEM(w_out_b.shape, BF16),
                        pltpu.SemaphoreType.DMA((2,))],
        compiler_params=pltpu.CompilerParams(dimension_semantics=("arbitrary", "arbitrary"), vmem_limit_bytes=VMEM_LIMIT),
    )(x, tgt, proj, pool_w, pool_b.reshape(1, W_MIX), _row(pool_scale), conv_dw_f, _row(conv_b),
      _row(ln_g), _row(ln_b), pw_w_b, _row(pw_b), w_out_b, _row(final_g))


NT_DIMS = (((1,), (1,)), ((), ()))
TN_DIMS = (((0,), (0,)), ((), ()))
V_POOL_B, V_POOL_SCALE, V_CONV_B, V_LN_G, V_LN_B, V_PW_B = range(6)


def _bwd_taps():
    taps = {s: [] for s in range(SUBLANES)}
    for k in range(CONV_WIDTH):
        e = CONV_WIDTH - 1 - k
        taps[e % SUBLANES].append((e // SUBLANES, k))
    return taps


def _bwd1_body(n_tiles, dh_ref, proj_ref, h1_ref, z_ref, ob_ref, dpool_ref, wout_ref, pww_ref, poolw_ref, pools_ref, dw_ref,
               lg_ref, lb_ref,
               dproj_ref, gwout_ref, gpww_ref, gpoolw_ref, gvec_ref, gdw_ref,
               vext, gext, gsh, y_s):
    b = pl.program_id(0)
    i = pl.program_id(1)
    tm = TILE_M
    t0 = (n_tiles - 1 - i) * tm

    @pl.when(i == 0)
    def _():
        vext[tm:, :] = jnp.zeros((POOL_HALO, W_MIX), F32)
        gext[tm:, :] = jnp.zeros((CONV_HALO, W_MIX), F32)

    @pl.when((b == 0) & (i == 0))
    def _():
        gwout_ref[...] = jnp.zeros_like(gwout_ref)
        gpww_ref[...] = jnp.zeros_like(gpww_ref)
        gpoolw_ref[...] = jnp.zeros_like(gpoolw_ref)
        gvec_ref[...] = jnp.zeros_like(gvec_ref)
        gdw_ref[...] = jnp.zeros_like(gdw_ref)

    sub = tm // BWD_SUBTILES

    def dense_phase(r0):
        rs = slice(r0, r0 + sub)
        dhb = dh_ref[0, rs, :].astype(BF16)
        dy_a = lax.dot_general(dhb, wout_ref[0:W_MIX, :], NT_DIMS, preferred_element_type=F32)
        dy_b = lax.dot_general(dhb, wout_ref[W_MIX:, :], NT_DIMS, preferred_element_type=F32)

        a_gate = proj_ref[0, rs, W_MIX:2 * W_MIX]
        sig_a = jax.nn.sigmoid(a_gate)
        silu_a = a_gate * sig_a
        z = z_ref[0, rs, :]
        out_a = z * pools_ref[...]
        y_s[rs, 0:W_MIX] = (out_a * silu_a).astype(BF16)
        d_out_a = dy_a * silu_a
        d_ag = dy_a * out_a * (sig_a * (1.0 + a_gate * (1.0 - sig_a)))
        dproj_ref[0, rs, W_MIX:2 * W_MIX] = d_ag.astype(BF16)
        gvec_ref[V_POOL_SCALE] += _colsum8(d_out_a * z)
        dz = d_out_a * pools_ref[...]
        gvec_ref[V_POOL_B] += _colsum8(dz)
        t = _row_index((sub, POOL_GC), t0 + r0)
        n_ext = sub + POOL_HALO
        for g, w in enumerate(POOL_WINDOWS):
            lanes = slice(g * POOL_GC, (g + 1) * POOL_GC)
            dzg = dz[:, lanes].astype(BF16)
            gpoolw_ref[g] += lax.dot_general(dpool_ref[0, rs, lanes], dzg, TN_DIMS, preferred_element_type=F32)
            dd = lax.dot_general(dzg, poolw_ref[g].astype(BF16), NT_DIMS, preferred_element_type=F32)
            vext[rs, lanes] = dd * (1.0 / jnp.minimum(t + 1, w).astype(F32))
            s = vext[r0:r0 + n_ext, lanes]
            sh = 1
            while sh < w:
                s = s + pltpu.roll(s, n_ext - sh, 0)
                sh *= 2
            dproj_ref[0, rs, lanes] = (s[0:sub] - dd).astype(BF16)

        b_gate = proj_ref[0, rs, 4 * W_MIX:5 * W_MIX]
        sig_b = jax.nn.sigmoid(b_gate)
        silu_b = b_gate * sig_b
        ob = ob_ref[0, rs, :]
        y_s[rs, W_MIX:] = (ob * silu_b).astype(BF16)
        d_ob = dy_b * silu_b
        dproj_ref[0, rs, 4 * W_MIX:5 * W_MIX] = (dy_b * ob * (sig_b * (1.0 + b_gate * (1.0 - sig_b)))).astype(BF16)
        gvec_ref[V_PW_B] += _colsum8(d_ob)
        gwout_ref[...] += lax.dot_general(y_s[rs, :], dhb, TN_DIMS, preferred_element_type=F32)

        h1 = h1_ref[0, rs, :]
        mu = jnp.mean(h1, axis=-1, keepdims=True)
        xc = h1 - mu
        rstd = lax.rsqrt(jnp.mean(xc * xc, axis=-1, keepdims=True) + LN_EPS)
        xhat = xc * rstd
        h2 = xhat * lg_ref[...] + lb_ref[...]
        sig2 = jax.nn.sigmoid(h2)
        d_obb = d_ob.astype(BF16)
        gpww_ref[...] += lax.dot_general((h2 * sig2).astype(BF16), d_obb, TN_DIMS, preferred_element_type=F32)
        dh2 = lax.dot_general(d_obb, pww_ref[...], NT_DIMS, preferred_element_type=F32) * (sig2 * (1.0 + h2 * (1.0 - sig2)))
        gvec_ref[V_LN_G] += _colsum8(dh2 * xhat)
        gvec_ref[V_LN_B] += _colsum8(dh2)
        dxh = dh2 * lg_ref[...]
        dh1 = rstd * (dxh - jnp.mean(dxh, axis=-1, keepdims=True) - xhat * jnp.mean(dxh * xhat, axis=-1, keepdims=True))
        gvec_ref[V_CONV_B] += _colsum8(dh1)
        gext[rs, :] = dh1

    taps = _bwd_taps()
    cr = BWD_CONV_ROWS
    win = cr + CONV_HALO - SUBLANES

    def conv_phase(r0):
        for c0 in range(r0, r0 + sub, cr):
            for l0 in range(0, W_MIX, LANES):
                lanes = slice(l0, l0 + LANES)
                b_val = proj_ref[0, c0:c0 + cr, 2 * W_MIX + l0:2 * W_MIX + l0 + LANES]
                sg = jax.nn.sigmoid(proj_ref[0, c0:c0 + cr, 3 * W_MIX + l0:3 * W_MIX + l0 + LANES])
                h0c = b_val * sg
                acc = jnp.zeros((cr, LANES), F32)
                for s in range(SUBLANES):
                    gsh[...] = gext[c0 + s:c0 + s + win, lanes]
                    for q, k in taps[s]:
                        gq = gsh[SUBLANES * q:SUBLANES * q + cr, :]
                        acc = acc + gq * dw_ref[k:k + 1, lanes]
                        gdw_ref[k, :, lanes] += _colsum8(h0c * gq)
                d_bval = acc * sg
                dproj_ref[0, c0:c0 + cr, 2 * W_MIX + l0:2 * W_MIX + l0 + LANES] = d_bval.astype(BF16)
                dproj_ref[0, c0:c0 + cr, 3 * W_MIX + l0:3 * W_MIX + l0 + LANES] = (d_bval * b_val * (1.0 - sg)).astype(BF16)

    starts = [k * sub for k in reversed(range(BWD_SUBTILES))]
    for r0 in starts:
        dense_phase(r0)
    for r0 in starts:
        conv_phase(r0)
    vext[tm:, :] = vext[0:POOL_HALO, :]
    gext[tm:, :] = gext[0:CONV_HALO, :]


def _bwd1(dh, proj, h1, z, ob, dpool, w_out_b, pw_w_b, pool_w, pool_scale, conv_dw_f, ln_g, ln_b):
    bl, s, _ = dh.shape
    tm = TILE_M
    nt = s // tm
    tok = lambda n: pl.BlockSpec((1, tm, n), lambda b, i: (b, nt - 1 - i, 0))
    res = lambda shape: pl.BlockSpec(shape, lambda b, i: (0,) * len(shape))
    out_shapes = [(D_MODEL, D_MODEL), (W_MIX, W_MIX), (len(POOL_WINDOWS), POOL_GC, POOL_GC), (6, SUBLANES, W_MIX),
                  (CONV_WIDTH, SUBLANES, W_MIX)]
    return pl.pallas_call(
        functools.partial(_bwd1_body, nt),
        name="bwd1",
        grid=(bl, nt),
        in_specs=[tok(D_MODEL), tok(D_IN), tok(W_MIX), tok(W_MIX), tok(W_MIX), tok(W_MIX)] + [VMEM] * 7,
        out_specs=[tok(D_IN)] + [res(sh) for sh in out_shapes],
        out_shape=[jax.ShapeDtypeStruct((bl, s, D_IN), BF16)] + [jax.ShapeDtypeStruct(sh, F32) for sh in out_shapes],
        scratch_shapes=[pltpu.VMEM((tm + POOL_HALO, W_MIX), F32), pltpu.VMEM((tm + CONV_HALO, W_MIX), F32),
                        pltpu.VMEM((BWD_CONV_ROWS + CONV_HALO - SUBLANES, LANES), F32), pltpu.VMEM((tm, D_MODEL), BF16)],
        compiler_params=pltpu.CompilerParams(dimension_semantics=("arbitrary", "arbitrary"), vmem_limit_bytes=VMEM_LIMIT),
    )(dh, proj, h1, z, ob, dpool, w_out_b, pw_w_b, pool_w, _row(pool_scale), conv_dw_f, _row(ln_g), _row(ln_b))


MESH = pl.DeviceIdType.MESH
W_IN_BLK = D_IN // N_DEV
W_OUT_BLK = D_MODEL // N_DEV
PW_BLK = W_MIX // N_DEV
DW_BLK = W_MIX // N_DEV
N_PEER = N_DEV - 1


SLAB = 2 * W_IN_BLK
N_SLABS = 4
F_SIB, F_X, F_Y, F_ON, F_PASS_X, F_PASS_Y, F_PASS_D = range(7)
W_IN_ONLY, OTHERS = (0,), (1, 2, 3)


def _inproj_body(n_tiles, x_hbm, ng_ref, wint_ref, wout_ref, pww_ref, dw_ref,
                 proj_hbm, wt_hbm, woutf_hbm, pwwf_hbm, dwf_ref,
                 hn_s, wt_s, wout_s, pww_s, dw_blk, stage, xbuf, out_sems, x_sems, w_sems, send_sems, recv_sems):
    s = pl.program_id(0)
    i = pl.program_id(1)
    tm = TILE_IN
    x, y, c = lax.axis_index("x"), lax.axis_index("y"), lax.axis_index("c")
    sibling = (x, y, 1 - c)
    xn, yn, dg = (1 - x, y), (x, 1 - y), (1 - x, 1 - y)
    pick = lambda a, b: tuple(jnp.where(c == 1, p, q) for p, q in zip(a, b))
    on_from, on_to = pick(xn, yn), pick(yn, xn)

    def blocks(px, py, pc):
        k = 4 * px + 2 * py + pc
        rows = lambda ref, n: ref.at[pl.ds(pl.multiple_of(k * n, n), n), :]
        return (rows(wt_s, W_IN_BLK), rows(wout_s, W_OUT_BLK), rows(pww_s, PW_BLK), dw_blk.at[k])

    def copies(arrays, flow, block, to):
        refs = blocks(*block)
        return [pltpu.make_async_remote_copy(src_ref=refs[a], dst_ref=refs[a], send_sem=send_sems.at[a, flow],
                                             recv_sem=recv_sems.at[a, flow], device_id=to, device_id_type=MESH)
                for a in arrays]

    def start(arrays):
        for flow, to in ((F_SIB, sibling), (F_X, (*xn, c)), (F_Y, (*yn, c))):
            for cp in copies(arrays, flow, (x, y, c), to):
                cp.start()

    def direct(arrays):
        for flow, chip in ((F_X, xn), (F_Y, yn)):
            for cp in copies(arrays, flow, (*chip, c), sibling):
                cp.wait_recv()
        for cp in copies(arrays, F_ON, (*on_from, c), (*on_to, c)):
            cp.start()
        for flow, chip in ((F_PASS_X, xn), (F_PASS_Y, yn)):
            for cp in copies(arrays, flow, (*chip, c), sibling):
                cp.start()

    def diagonal(arrays):
        for cp in copies(arrays, F_ON, (*dg, c), sibling):
            cp.wait_recv()
        for cp in copies(arrays, F_PASS_D, (*dg, c), sibling):
            cp.start()

    def from_sibling(arrays, flow, chip):
        for cp in copies(arrays, flow, (*chip, 1 - c), sibling):
            cp.wait_recv()

    def finish(arrays):
        mine = (x, y, c)
        for flow, block in ((F_SIB, mine), (F_X, mine), (F_Y, mine), (F_ON, (*on_from, c)), (F_PASS_X, (*xn, c)),
                            (F_PASS_Y, (*yn, c)), (F_PASS_D, (*dg, c))):
            for cp in copies(arrays, flow, block, sibling):
                cp.wait_send()

    step = s * n_tiles + i
    par = step % 2
    first = step == 0

    @pl.when(first)
    def _():
        mine = blocks(x, y, c)
        mine[0][...] = wint_ref[...].astype(BF16)
        mine[1][...] = wout_ref[...].astype(BF16)
        mine[2][...] = pww_ref[...].astype(BF16)
        mine[3][...] = dw_ref[...]
        start(W_IN_ONLY)
        start(OTHERS)

    rows_i = pl.ds(pl.multiple_of(i * tm, tm), tm)
    tiles_per_seq = proj_hbm.shape[1] // tm

    def x_copy(k):
        src = x_hbm.at[k // tiles_per_seq, pl.ds(pl.multiple_of((k % tiles_per_seq) * tm, tm), tm), :]
        return pltpu.make_async_copy(src, xbuf.at[k % 2], x_sems.at[k % 2])

    @pl.when(first)
    def _():
        x_copy(0).start()

    @pl.when(s == 0)
    def _():
        x_copy(i).wait()

        @pl.when(i + 1 < n_tiles)
        def _():
            x_copy(i + 1).start()

        xt = xbuf[i % 2]
        r1 = lax.rsqrt(jnp.mean(xt * xt, axis=-1, keepdims=True) + RMS_EPS)
        hn_s[rows_i, :] = (xt * r1 * ng_ref[...]).astype(BF16)

    @pl.when(first)
    def _():
        from_sibling(W_IN_ONLY, F_SIB, (x, y))

    @pl.when((s == 1) & (i == 0))
    def _():
        direct(W_IN_ONLY)
        from_sibling(W_IN_ONLY, F_PASS_Y, yn)

    @pl.when((s == 2) & (i == 0))
    def _():
        from_sibling(W_IN_ONLY, F_PASS_X, xn)

    weights_out = [pltpu.make_async_copy(src, dst, w_sems.at[k])
                   for k, (src, dst) in enumerate(((wt_s, wt_hbm), (wout_s, woutf_hbm), (pww_s, pwwf_hbm)))]

    @pl.when((s == 3) & (i == 0))
    def _():
        diagonal(W_IN_ONLY)
        direct(OTHERS)
        from_sibling(W_IN_ONLY, F_PASS_D, dg)
        weights_out[0].start()

    def out_copy(parity, col0):
        dst = proj_hbm.at[i // tiles_per_seq, pl.ds(pl.multiple_of((i % tiles_per_seq) * tm, tm), tm),
                          pl.ds(pl.multiple_of(col0, LANES), SLAB)]
        return pltpu.make_async_copy(stage.at[parity], dst, out_sems.at[parity])

    @pl.when(step >= 2)
    def _():
        out_copy(par, 0).wait()

    chip = jnp.where(s < 2, 2 * x, 2 * (1 - x)) + jnp.where((s == 0) | (s == 2), y, 1 - y)
    slab = wt_s[pl.ds(pl.multiple_of(chip * SLAB, SLAB), SLAB), :]
    stage[par] = lax.dot_general(hn_s[rows_i, :], slab, NT_DIMS, preferred_element_type=F32)
    out_copy(par, chip * SLAB).start()

    @pl.when(step == N_SLABS * n_tiles - 1)
    def _():
        out_copy(par, 0).wait()
        out_copy(1 - par, 0).wait()
        from_sibling(OTHERS, F_SIB, (x, y))
        from_sibling(OTHERS, F_PASS_X, xn)
        from_sibling(OTHERS, F_PASS_Y, yn)
        diagonal(OTHERS)
        from_sibling(OTHERS, F_PASS_D, dg)
        for cp in weights_out[1:]:
            cp.start()
        finish(W_IN_ONLY)
        finish(OTHERS)
        for cp in weights_out:
            cp.wait()
        for k in range(N_DEV):
            dwf_ref[:, k * DW_BLK:(k + 1) * DW_BLK] = dw_blk[k]


def _inproj(x, norm_g, w_in_t, w_out, pw_w, conv_dw):
    bl, s, _ = x.shape
    tm = TILE_IN
    nt = s // tm
    n_tiles = bl * nt
    dma = pltpu.SemaphoreType.DMA
    return pl.pallas_call(
        functools.partial(_inproj_body, n_tiles),
        name="inproj",
        grid=(N_SLABS, n_tiles),
        in_specs=[pl.BlockSpec(memory_space=pl.ANY)] + [VMEM] * 5,
        out_specs=[pl.BlockSpec(memory_space=pl.ANY)] * 4 + [VMEM],
        out_shape=[jax.ShapeDtypeStruct((bl, s, D_IN), F32), jax.ShapeDtypeStruct((D_IN, D_MODEL), BF16),
                   jax.ShapeDtypeStruct((D_MODEL, D_MODEL), BF16), jax.ShapeDtypeStruct((W_MIX, W_MIX), BF16),
                   jax.ShapeDtypeStruct((CONV_WIDTH, W_MIX), F32)],
        scratch_shapes=[pltpu.VMEM((n_tiles * tm, D_MODEL), BF16), pltpu.VMEM((D_IN, D_MODEL), BF16),
                        pltpu.VMEM((D_MODEL, D_MODEL), BF16), pltpu.VMEM((W_MIX, W_MIX), BF16),
                        pltpu.VMEM((N_DEV, CONV_WIDTH, DW_BLK), F32), pltpu.VMEM((2, tm, SLAB), F32),
                        pltpu.VMEM((2, tm, D_MODEL), F32), dma((2,)), dma((2,)), dma((3,)), dma((4, 7)), dma((4, 7))],
        compiler_params=pltpu.CompilerParams(dimension_semantics=("arbitrary", "arbitrary"), vmem_limit_bytes=VMEM_LIMIT),
    )(x, _row(norm_g), w_in_t, w_out, pw_w, conv_dw)


B_PLAIN, B_ON, B_MERGED = range(3)


def _routes():
    x, y, c = lax.axis_index("x"), lax.axis_index("y"), lax.axis_index("c")
    xn, yn = (1 - x, y), (x, 1 - y)
    pick = lambda a, b: tuple(jnp.where(c == 1, p, q) for p, q in zip(a, b))
    return pick(xn, yn), pick(yn, xn), (1 - x, 1 - y), (x, y), c


def _chip_index(chip):
    return 2 * chip[0] + chip[1]


def _copy(src, dst, send_sems, recv_sems, a, k, to):
    return pltpu.make_async_remote_copy(src_ref=src, dst_ref=dst, send_sem=send_sems.at[a, k], recv_sem=recv_sems.at[a, k],
                                        device_id=to, device_id_type=MESH)


def _rsb_send(plain, on, rb_plain, rb_on, send_b, recv_b):
    first, _, _, _, c = _routes()
    for k, srcs, dsts in ((B_ON, on, rb_on), (B_PLAIN, plain, rb_plain)):
        for a, (s, d) in enumerate(zip(srcs, dsts)):
            _copy(s, d, send_b, recv_b, a, k, (*first, c)).start()


def _rsb_merge(second_sums, merged, rb_on, rb_merged, send_b, recv_b):
    _, second, _, _, c = _routes()
    for a, (own, m, got, dst) in enumerate(zip(second_sums, merged, rb_on, rb_merged)):
        _copy(got, got, send_b, recv_b, a, B_ON, (*second, c)).wait_recv()
        m[...] = (own[...] + got[...].astype(F32)).astype(BF16)
        _copy(m, dst, send_b, recv_b, a, B_MERGED, (*second, c)).start()


def _rsb_total(own_sums, plain, on, merged, rb_plain, rb_merged, send_b, recv_b):
    _, second, _, _, c = _routes()
    out = []
    for a, (own, p, g) in enumerate(zip(own_sums, rb_plain, rb_merged)):
        _copy(p, p, send_b, recv_b, a, B_PLAIN, (*second, c)).wait_recv()
        _copy(g, g, send_b, recv_b, a, B_MERGED, (*second, c)).wait_recv()
        out.append(own[...] + p[...].astype(F32) + g[...].astype(F32))
    for k, srcs in ((B_PLAIN, plain), (B_ON, on), (B_MERGED, merged)):
        for a, s in enumerate(srcs):
            _copy(s, s, send_b, recv_b, a, k, (*second, c)).wait_send()
    return out


R_LOSS, R_FINAL_G, R_POOL, R_CONV, R_LN = range(5)
VEC_ROWS = 16
DW_ROWS = 32
A_PART, A_X, A_Y, A_ON, A_HALF = range(5)
HALF_DTYPES = (F32, BF16, BF16)
STEP_SUMS, STEP_MERGE, STEP_HALF = 1, 3, 5


def _bwd2a_body(n_steps, x_ref, dproj_ref, ng_ref, gwout_hbm, gpww_hbm, loss_ref, gfg_ref, gvec_ref, gdw_ref, gpoolw_ref,
                win_own_ref, win_staged_ref, owout_ref, opww_ref, ovec_ref, odw_ref, opoolw_ref,
                acc, st_win, a_win, own_wout, own_pww, a_wout, a_pww, st_wout, st_pww, rb_wout, rb_pww, part_vec, part_dw,
                part_pw, sib_vec, sib_dw, sib_pw, res_vec, res_dw, res_pw, half_vec, half_dw, half_pw,
                own_sems, send_a, recv_a, send_b, recv_b, send_r, recv_r, send_w, recv_w):
    step = pl.program_id(0) * pl.num_programs(1) + pl.program_id(1)
    x, y, c = lax.axis_index("x"), lax.axis_index("y"), lax.axis_index("c")
    sibling = (x, y, 1 - c)
    xn, yn = (1 - x, y), (x, 1 - y)
    first, second, diag, own, _ = _routes()
    rows = lambda ref, k, n: ref.at[pl.ds(pl.multiple_of(k * n, n), n), :]

    blocks = lambda k: (rows(gwout_hbm, k, W_OUT_BLK), rows(gpww_hbm, k, PW_BLK))
    mine = lambda chip: (own_wout.at[chip], own_pww.at[chip])
    mine_copies = lambda chip: [pltpu.make_async_copy(s, d, own_sems.at[a, chip])
                                for a, (s, d) in enumerate(zip(blocks(2 * chip + c), mine(chip)))]
    landed = lambda chip: (a_wout.at[chip], a_pww.at[chip])
    staged = lambda k: (st_wout.at[k], st_pww.at[k])
    arrived = lambda k: (rb_wout.at[k], rb_pww.at[k])

    parts, sibs, halves = (part_vec, part_dw, part_pw), (sib_vec, sib_dw, sib_pw), (half_vec, half_dw, half_pw)
    results = (res_vec, res_dw, res_pw)
    half_rows = lambda ref: pl.ds(pl.multiple_of(c * (ref.shape[0] // 2), SUBLANES), ref.shape[0] // 2)
    slot = lambda chip: tuple(h.at[_chip_index(chip)] for h in halves)

    @pl.when(step == 0)
    def _():
        acc[...] = jnp.zeros_like(acc)
        sum8 = lambda v: jnp.sum(v, axis=0, keepdims=True)
        part_vec[...] = jnp.zeros((VEC_ROWS, D_MODEL), F32)
        part_vec[R_LOSS:R_LOSS + 1, :] = jnp.broadcast_to(loss_ref[0:1, 0:1], (1, D_MODEL))
        part_vec[R_FINAL_G:R_FINAL_G + 1, :] = sum8(gfg_ref[...])
        for row, (lo, hi) in ((R_POOL, (V_POOL_B, V_POOL_SCALE)), (R_CONV, (V_CONV_B, V_LN_G)), (R_LN, (V_LN_B, V_PW_B))):
            part_vec[row:row + 1, 0:W_MIX] = sum8(gvec_ref[lo])
            part_vec[row:row + 1, W_MIX:] = sum8(gvec_ref[hi])
        part_dw[0:CONV_WIDTH, :] = jnp.sum(gdw_ref[...], axis=1)
        part_dw[CONV_WIDTH:, :] = jnp.zeros((DW_ROWS - CONV_WIDTH, W_MIX), F32)
        part_pw[...] = gpoolw_ref[...].reshape(len(POOL_WINDOWS) * POOL_GC, POOL_GC)
        for a, (p, s) in enumerate(zip(parts, sibs)):
            _copy(p, s, send_r, recv_r, a, A_PART, sibling).start()
        for chip in range(4):
            for a, (s, d) in enumerate(zip(blocks(2 * chip + (1 - c)), landed(chip))):
                _copy(s, d, send_a, recv_a, a, chip, sibling).start()
            for cp in mine_copies(chip):
                cp.start()

    x_t = x_ref[0]
    r1 = lax.rsqrt(jnp.mean(x_t * x_t, axis=-1, keepdims=True) + RMS_EPS)
    hn = (x_t * r1 * ng_ref[...]).astype(BF16)
    acc[...] += lax.dot_general(dproj_ref[0], hn, TN_DIMS, preferred_element_type=F32)

    @pl.when(step == STEP_SUMS)
    def _():
        for a, (p, s, h) in enumerate(zip(parts, sibs, slot(own))):
            _copy(p, s, send_r, recv_r, a, A_PART, sibling).wait_recv()
            h[...] = (p[half_rows(p), :] + s[half_rows(s), :]).astype(h.dtype)
            for k, to in ((A_X, xn), (A_Y, yn)):
                _copy(h, h, send_r, recv_r, a, k, (*to, c)).start()
        for chip in range(4):
            for cp in mine_copies(chip):
                cp.wait()
            for a, (m, l) in enumerate(zip(mine(chip), landed(chip))):
                _copy(l, l, send_a, recv_a, a, chip, sibling).wait_recv()
                l[...] = m[...] + l[...]
        for k, chip in ((B_ON, diag), (B_PLAIN, first)):
            for s, l in zip(staged(k), landed(_chip_index(chip))):
                s[...] = l[...].astype(BF16)
        _rsb_send(staged(B_PLAIN), staged(B_ON), arrived(B_PLAIN), arrived(B_ON), send_b, recv_b)

    @pl.when(step == STEP_MERGE)
    def _():
        _rsb_merge(landed(_chip_index(second)), staged(B_MERGED), arrived(B_ON), arrived(B_MERGED), send_b, recv_b)
        for a in range(3):
            for k, chip in ((A_X, xn), (A_Y, yn)):
                h = slot(chip)[a]
                _copy(h, h, send_r, recv_r, a, k, sibling).wait_recv()
            h = slot(first)[a]
            _copy(h, h, send_r, recv_r, a, A_ON, (*second, c)).start()

    @pl.when(step == STEP_HALF)
    def _():
        for a, (r, hs) in enumerate(zip(results, halves)):
            h = slot(diag)[a]
            _copy(h, h, send_r, recv_r, a, A_ON, sibling).wait_recv()
            mine = r.at[half_rows(r), :]
            chip_sums = [hs[k].astype(F32) for k in range(4)]
            mine[...] = (chip_sums[0] + chip_sums[1]) + (chip_sums[2] + chip_sums[3])
            _copy(mine, mine, send_r, recv_r, a, A_HALF, sibling).start()

    @pl.when(step == n_steps - 1)
    def _():
        block = lambda k: rows(acc, k, W_IN_BLK)
        w_copy = lambda chip: _copy(st_win.at[chip], a_win.at[chip], send_w, recv_w, 0, chip, sibling)
        for chip in range(4):
            st_win[chip] = block(2 * chip + (1 - c))[...].astype(BF16)
            w_copy(chip).start()

        owout_ref[...], opww_ref[...] = _rsb_total(landed(_chip_index(own)), staged(B_PLAIN), staged(B_ON), staged(B_MERGED),
                                                   arrived(B_PLAIN), arrived(B_MERGED), send_b, recv_b)
        for a, (r, p, s) in enumerate(zip(results, parts, sibs)):
            theirs = r.at[pl.ds(pl.multiple_of((1 - c) * (r.shape[0] // 2), SUBLANES), r.shape[0] // 2), :]
            _copy(theirs, theirs, send_r, recv_r, a, A_HALF, sibling).wait_recv()
            _copy(p, s, send_r, recv_r, a, A_PART, sibling).wait_send()
            for k, h in ((A_X, slot(own)[a]), (A_Y, slot(own)[a]), (A_ON, slot(first)[a])):
                _copy(h, h, send_r, recv_r, a, k, sibling).wait_send()
            mine = r.at[half_rows(r), :]
            _copy(mine, mine, send_r, recv_r, a, A_HALF, sibling).wait_send()
        for chip in range(4):
            for a, (s, d) in enumerate(zip(blocks(2 * chip + (1 - c)), landed(chip))):
                _copy(s, d, send_a, recv_a, a, chip, sibling).wait_send()
        ovec_ref[...] = res_vec[...]
        opoolw_ref[...] = res_pw[...]
        for k in range(N_DEV):
            @pl.when(4 * x + 2 * y + c == k)
            def _():
                odw_ref[...] = res_dw[0:CONV_WIDTH, k * DW_BLK:(k + 1) * DW_BLK]

        for chip in range(4):
            w_copy(chip).wait()
        chip_sum = lambda chip: block(2 * _chip_index(chip) + c)[...] + a_win[_chip_index(chip)].astype(F32)
        win_own_ref[...] = chip_sum(own)
        for j, chip in enumerate((xn, yn, diag)):
            win_staged_ref[j] = chip_sum(chip).astype(BF16)


def _bwd2a(x, dproj, norm_g, gwout, gpww, loss8, gfg8, gvec, gdw8, gpoolw):
    bl, s, _ = x.shape
    tm = TILE_M
    nt = s // tm
    tok = lambda n: pl.BlockSpec((1, tm, n), lambda b, i: (b, i, 0))
    n_pw = len(POOL_WINDOWS) * POOL_GC
    win = (W_IN_BLK, D_MODEL)
    blk = ((W_OUT_BLK, D_MODEL), (PW_BLK, W_MIX))
    rep = ((VEC_ROWS, D_MODEL), (DW_ROWS, W_MIX), (n_pw, POOL_GC))
    half = lambda sh: (sh[0] // 2,) + sh[1:]
    dma = pltpu.SemaphoreType.DMA
    assert bl * nt > STEP_HALF + 1, "the exchanges' phases need their grid steps"
    return pl.pallas_call(
        functools.partial(_bwd2a_body, bl * nt),
        name="bwd2a",
        grid=(bl, nt),
        in_specs=[tok(D_MODEL), tok(D_IN), VMEM] + [pl.BlockSpec(memory_space=pl.ANY)] * 2 + [VMEM] * 5,
        out_specs=[VMEM] * 7,
        out_shape=[jax.ShapeDtypeStruct(win, F32), jax.ShapeDtypeStruct((3,) + win, BF16)]
        + [jax.ShapeDtypeStruct(sh, F32) for sh in blk + (rep[0], (CONV_WIDTH, DW_BLK), rep[2])],
        scratch_shapes=[pltpu.VMEM((D_IN, D_MODEL), F32), pltpu.VMEM((4,) + win, BF16), pltpu.VMEM((4,) + win, BF16)]
        + [pltpu.VMEM((4,) + sh, F32) for sh in blk] * 2
        + [pltpu.VMEM((3,) + sh, BF16) for sh in blk] * 2
        + [pltpu.VMEM(sh, F32) for sh in rep] * 3
        + [pltpu.VMEM((4,) + half(sh), dt) for sh, dt in zip(rep, HALF_DTYPES)]
        + [dma((2, 4)), dma((2, 4)), dma((2, 4)), dma((2, 3)), dma((2, 3)), dma((3, 5)), dma((3, 5)), dma((1, 4)), dma((1, 4))],
        compiler_params=pltpu.CompilerParams(dimension_semantics=("arbitrary", "arbitrary"), vmem_limit_bytes=VMEM_LIMIT),
    )(x, dproj, _row(norm_g), gwout, gpww, loss8, gfg8, gvec, gdw8, gpoolw)


HBM = pl.BlockSpec(memory_space=pltpu.HBM)
SEM = pl.BlockSpec(memory_space=pltpu.SEMAPHORE)
EFFECT = pltpu.SideEffectType.DATAFLOW_SIDE_EFFECTING


def _owner_copies(src_ref, land_ref, send_sems, recv_sems):
    x, y, c = lax.axis_index("x"), lax.axis_index("y"), lax.axis_index("c")
    owners = ((1 - x, y, c), (x, 1 - y, c), (1 - x, 1 - y, c))
    return [pltpu.make_async_remote_copy(src_ref=src_ref.at[j], dst_ref=land_ref.at[j], send_sem=send_sems.at[j],
                                         recv_sem=recv_sems.at[j], device_id=owners[j], device_id_type=MESH)
            for j in range(3)]


def _rs_start(staged):
    def body(src_ref, land_ref, send_sems, recv_sems, src_thru, land_thru, token):
        for cp in _owner_copies(src_ref, land_ref, send_sems, recv_sems):
            cp.start()
        token[...] = jnp.zeros_like(token)

    dma = pltpu.SemaphoreType.DMA
    return pl.pallas_call(
        body, name="rs_start",
        out_shape=(dma((3,)), dma((3,)), pltpu.HBM(staged.shape, staged.dtype), pltpu.HBM(staged.shape, staged.dtype),
                   jax.ShapeDtypeStruct((SUBLANES, LANES), F32)),
        in_specs=(HBM, HBM), out_specs=(SEM, SEM, HBM, HBM, VMEM), input_output_aliases={0: 2, 1: 3},
        compiler_params=pltpu.CompilerParams(has_side_effects=EFFECT),
    )(pltpu.with_memory_space_constraint(staged, pltpu.HBM),
      pltpu.with_memory_space_constraint(lax.empty(staged.shape, staged.dtype), pltpu.HBM))


def _rs_wait(send_sems, recv_sems, src_thru, land_thru, after):
    def body(src_ref, land_ref, send_sems, recv_sems, after_ref, src_dead, got_ref):
        for cp in _owner_copies(src_ref, land_ref, send_sems, recv_sems):
            cp.wait_send()
            cp.wait_recv()

    return pl.pallas_call(
        body, name="rs_wait",
        out_shape=(pltpu.HBM(src_thru.shape, src_thru.dtype), pltpu.HBM(land_thru.shape, land_thru.dtype)),
        in_specs=(HBM, HBM, SEM, SEM, pl.BlockSpec(memory_space=pl.ANY)), out_specs=(HBM, HBM),
        input_output_aliases={0: 0, 1: 1},
        compiler_params=pltpu.CompilerParams(has_side_effects=EFFECT),
    )(src_thru, land_thru, send_sems, recv_sems, after)[1]


def _bwd2b_body(x_ref, dh_ref, dproj_ref, ng_ref, wint_hbm, token_ref, gx_ref, gng_ref, wint_ref, dhn_s, w_sems):
    first = (pl.program_id(0) == 0) & (pl.program_id(1) == 0)
    slab = lambda k: pl.ds(k * SLAB, SLAB)
    slabs_in = [pltpu.make_async_copy(wint_hbm.at[slab(k), :], wint_ref.at[slab(k), :], w_sems.at[k]) for k in range(N_SLABS)]

    @pl.when(first)
    def _():
        for cp in slabs_in:
            cp.start()
        gng_ref[...] = jnp.zeros_like(gng_ref) + token_ref[0:1, 0:1]

    x_t = x_ref[0]
    r1 = lax.rsqrt(jnp.mean(x_t * x_t, axis=-1, keepdims=True) + RMS_EPS)
    xr = x_t * r1

    @pl.when(first)
    def _():
        for k, cp in enumerate(slabs_in):
            cp.wait()
            part = jnp.dot(dproj_ref[0, :, k * SLAB:(k + 1) * SLAB], wint_ref[slab(k), :], preferred_element_type=F32)
            dhn_s[...] = part if k == 0 else dhn_s[...] + part

    @pl.when(jnp.logical_not(first))
    def _():
        dhn_s[...] = jnp.dot(dproj_ref[0], wint_ref[...], preferred_element_type=F32)

    dhn = dhn_s[...]
    gng_ref[...] += _colsum8(dhn * xr)
    gy = dhn * ng_ref[...]
    gx_ref[0] = dh_ref[0] + r1 * (gy - xr * jnp.mean(gy * xr, axis=-1, keepdims=True))


def _bwd2b(x, dh, dproj, norm_g, w_in_t, token):
    bl, s, _ = x.shape
    tm = TILE_M
    tok = lambda n: pl.BlockSpec((1, tm, n), lambda b, i: (b, i, 0))
    return pl.pallas_call(
        functools.partial(_bwd2b_body),
        name="bwd2b",
        grid=(bl, s // tm),
        in_specs=[tok(D_MODEL), tok(D_MODEL), tok(D_IN), VMEM, pl.BlockSpec(memory_space=pl.ANY), VMEM],
        out_specs=[tok(D_MODEL), pl.BlockSpec((SUBLANES, D_MODEL), lambda b, i: (0, 0))],
        out_shape=[jax.ShapeDtypeStruct((bl, s, D_MODEL), F32), jax.ShapeDtypeStruct((SUBLANES, D_MODEL), F32)],
        scratch_shapes=[pltpu.VMEM(w_in_t.shape, BF16), pltpu.VMEM((tm, D_MODEL), F32), pltpu.SemaphoreType.DMA((N_SLABS,))],
        compiler_params=pltpu.CompilerParams(dimension_semantics=("arbitrary", "arbitrary"), vmem_limit_bytes=VMEM_LIMIT),
    )(x, dh, dproj, _row(norm_g), w_in_t, token)


def _partial_copies(part_ref, land_ref, send_sems, recv_sems):
    x, y, c = lax.axis_index("x"), lax.axis_index("y"), lax.axis_index("c")
    other_chips = [(1 - x, y), (x, 1 - y), (1 - x, 1 - y)]
    peers = [(x, y, 1 - c)] + [(*chip, c) for chip in other_chips] + [(*chip, 1 - c) for chip in other_chips]
    return [pltpu.make_async_remote_copy(src_ref=part_ref, dst_ref=land_ref.at[4 * x + 2 * y + c], send_sem=send_sems.at[r],
                                         recv_sem=recv_sems.at[r], device_id=peers[r], device_id_type=MESH)
            for r in range(N_PEER)]


def _partials_start(part):
    def body(part_ref, land_ref, send_sems, recv_sems, part_thru, land_thru):
        for cp in _partial_copies(part_ref, land_ref, send_sems, recv_sems):
            cp.start()

    dma = pltpu.SemaphoreType.DMA
    land = (N_DEV,) + part.shape
    return pl.pallas_call(
        body, name="partials_start",
        out_shape=(dma((N_PEER,)), dma((N_PEER,)), pltpu.HBM(part.shape, part.dtype), pltpu.HBM(land, part.dtype)),
        in_specs=(HBM, HBM), out_specs=(SEM, SEM, HBM, HBM), input_output_aliases={0: 2, 1: 3},
        compiler_params=pltpu.CompilerParams(has_side_effects=EFFECT),
    )(pltpu.with_memory_space_constraint(part, pltpu.HBM),
      pltpu.with_memory_space_constraint(lax.empty(land, part.dtype), pltpu.HBM))


def _partials_wait(send_sems, recv_sems, part_thru, land_thru, after):
    def body(part_ref, land_ref, send_sems, recv_sems, after_ref, part_out, land_out):
        for cp in _partial_copies(part_ref, land_ref, send_sems, recv_sems):
            cp.wait_send()
            cp.wait_recv()

    return pl.pallas_call(
        body, name="partials_wait",
        out_shape=(pltpu.HBM(part_thru.shape, part_thru.dtype), pltpu.HBM(land_thru.shape, land_thru.dtype)),
        in_specs=(HBM, HBM, SEM, SEM, pl.BlockSpec(memory_space=pl.ANY)), out_specs=(HBM, HBM),
        input_output_aliases={0: 0, 1: 1},
        compiler_params=pltpu.CompilerParams(has_side_effects=EFFECT),
    )(part_thru, land_thru, send_sems, recv_sems, after)


ADAM_LR = 0.001
ADAM_B1 = 0.9
ADAM_B2 = 0.999
ADAM_EPS = 1e-08
ADAM_WD = 0.01
ADAM_STEP = 10


PARAMS = ("norm_g", "w_in", "pool_w", "pool_b", "pool_scale", "conv_dw", "conv_b", "ln_g", "ln_b", "pw_w", "pw_b", "w_out",
          "final_g")


LATE_PARAMS = ("norm_g",)
MAIN_PARAMS = tuple(p for p in PARAMS if p not in LATE_PARAMS)


def _adamw_refs(params, n_sources, refs):
    n = len(params)
    ws, ms, vs = (dict(zip(params, refs[k * n:(k + 1) * n])) for k in range(3))
    outs = refs[3 * n + n_sources:]
    gs, ds, m2s, v2s = (dict(zip(params, outs[k * n:(k + 1) * n])) for k in range(4))

    def update(name, g, at=slice(None)):
        m = ADAM_B1 * ms[name][at] + (1.0 - ADAM_B1) * g
        v = ADAM_B2 * vs[name][at] + (1.0 - ADAM_B2) * (g * g)
        m_hat = m / (1.0 - ADAM_B1 ** ADAM_STEP)
        v_hat = v / (1.0 - ADAM_B2 ** ADAM_STEP)
        gs[name][at] = g
        ds[name][at] = -ADAM_LR * (m_hat / (jnp.sqrt(v_hat) + ADAM_EPS) + ADAM_WD * ws[name][at])
        m2s[name][at] = m
        v2s[name][at] = v

    return update, refs[3 * n:3 * n + n_sources], outs[4 * n:]


def _adamw_late_body(*refs):
    update, (gng_ref, gng_others_ref), _ = _adamw_refs(LATE_PARAMS, 2, refs)
    me = 4 * lax.axis_index("x") + 2 * lax.axis_index("y") + lax.axis_index("c")
    g_norm = jnp.zeros(gng_ref.shape, F32)
    for k in range(N_DEV):
        g_norm = g_norm + jnp.where(me == k, gng_ref[...], gng_others_ref[k])
    update("norm_g", jnp.sum(g_norm, axis=0, keepdims=True))


def _adamw_body(*refs):
    update, sources, (loss_ref,) = _adamw_refs(MAIN_PARAMS, 7, refs)
    gwin_ref, gwin_others_ref, gwout_ref, gpww_ref, vec_ref, gdw_ref, gpoolw_ref = sources
    ws = dict(zip(MAIN_PARAMS, refs))
    row = lambda r, lo, hi: vec_ref[r:r + 1, lo:hi]
    loss_ref[...] = row(R_LOSS, 0, 1)
    g_w_in = gwin_ref[...]
    for j in range(3):
        g_w_in = g_w_in + gwin_others_ref[j].astype(F32)
    update("w_in", g_w_in)
    update("pool_w", gpoolw_ref[...].reshape(ws["pool_w"].shape))
    for g in range(len(POOL_WINDOWS)):
        update("pool_b", row(R_POOL, g * POOL_GC, (g + 1) * POOL_GC), at=slice(g, g + 1))
    update("pool_scale", row(R_POOL, W_MIX, 2 * W_MIX))
    update("conv_dw", gdw_ref[...])
    update("conv_b", row(R_CONV, 0, W_MIX))
    update("ln_g", row(R_CONV, W_MIX, 2 * W_MIX))
    update("ln_b", row(R_LN, 0, W_MIX))
    update("pw_w", gpww_ref[...])
    update("pw_b", row(R_LN, W_MIX, 2 * W_MIX))
    update("w_out", gwout_ref[...])
    update("final_g", row(R_FINAL_G, 0, D_MODEL))


def _adamw(body, name, params, ws, ms, vs, grad_sources, extra_out_shapes=()):
    n = len(params)
    flat = lambda a: a.reshape(1, -1) if a.ndim == 1 else a
    args = [flat(d[p]) for d in (ws, ms, vs) for p in params]
    outs = pl.pallas_call(
        body,
        name=name,
        in_specs=[VMEM] * (3 * n + len(grad_sources)),
        out_specs=[VMEM] * (4 * n + len(extra_out_shapes)),
        out_shape=[jax.ShapeDtypeStruct(a.shape, F32) for a in args[:n]] * 4 + list(extra_out_shapes),
        compiler_params=pltpu.CompilerParams(vmem_limit_bytes=VMEM_LIMIT),
    )(*args, *grad_sources)
    groups = tuple({p: outs[k * n + j].reshape(ws[p].shape) for j, p in enumerate(params)} for k in range(4))
    return groups, outs[4 * n:]


def kernel(x, norm_g, w_in, pool_w, pool_b, pool_scale, conv_dw, conv_b, ln_g, ln_b, pw_w, pw_b, w_out, final_g, loss_target, m_norm_g, m_w_in, m_pool_w, m_pool_b, m_pool_scale, m_conv_dw, m_conv_b, m_ln_g, m_ln_b, m_pw_w, m_pw_b, m_w_out, m_final_g, v_norm_g, v_w_in, v_pool_w, v_pool_b, v_pool_scale, v_conv_dw, v_conv_b, v_ln_g, v_ln_b, v_pw_w, v_pw_b, v_w_out, v_final_g):
    proj, w_in_t, w_out_b, pw_w_b, conv_dw_f = _inproj(x, norm_g, w_in.T, w_out, pw_w, conv_dw)
    h1, z, ob, dpool, dh, loss8, gfg8 = _fwd(x, loss_target, proj, pool_w, pool_b, pool_scale, conv_dw_f,
                                             conv_b, ln_g, ln_b, pw_w_b, pw_b, w_out_b, final_g)
    dproj, gwout, gpww, gpoolw, gvec, gdw8 = _bwd1(dh, proj, h1, z, ob, dpool, w_out_b, pw_w_b, pool_w, pool_scale,
                                                   conv_dw_f, ln_g, ln_b)
    win_own, win_staged, g_w_out, g_pw_w, vec, g_conv_dw, poolw_sum = _bwd2a(x, dproj, norm_g, gwout, gpww, loss8, gfg8, gvec,
                                                                             gdw8, gpoolw)
    send_sems, recv_sems, staged_thru, land_thru, token = _rs_start(win_staged)
    grad_x, gng8 = _bwd2b(x, dh, dproj, norm_g, w_in_t, token)
    ng_send, ng_recv, gng8, gng_land = _partials_start(gng8)
    win_others = _rs_wait(send_sems, recv_sems, staged_thru, land_thru, gng8)

    weights = dict(norm_g=norm_g, w_in=w_in.T, pool_w=pool_w, pool_b=pool_b, pool_scale=pool_scale, conv_dw=conv_dw, conv_b=conv_b,
                   ln_g=ln_g, ln_b=ln_b, pw_w=pw_w, pw_b=pw_b, w_out=w_out, final_g=final_g)
    m_in = dict(norm_g=m_norm_g, w_in=m_w_in.T, pool_w=m_pool_w, pool_b=m_pool_b, pool_scale=m_pool_scale, conv_dw=m_conv_dw,
                conv_b=m_conv_b, ln_g=m_ln_g, ln_b=m_ln_b, pw_w=m_pw_w, pw_b=m_pw_b, w_out=m_w_out, final_g=m_final_g)
    v_in = dict(norm_g=v_norm_g, w_in=v_w_in.T, pool_w=v_pool_w, pool_b=v_pool_b, pool_scale=v_pool_scale, conv_dw=v_conv_dw,
                conv_b=v_conv_b, ln_g=v_ln_g, ln_b=v_ln_b, pw_w=v_pw_w, pw_b=v_pw_b, w_out=v_w_out, final_g=v_final_g)
    main, (loss,) = _adamw(functools.partial(_adamw_body), "adamw", MAIN_PARAMS, weights, m_in, v_in,
                           (win_own, win_others, g_w_out, g_pw_w, vec, g_conv_dw, poolw_sum), [jax.ShapeDtypeStruct((1, 1), F32)])
    gng8, gng_others = _partials_wait(ng_send, ng_recv, gng8, gng_land, main[0]["w_out"])
    late, _ = _adamw(functools.partial(_adamw_late_body), "adamw_norm_g", LATE_PARAMS, weights, m_in, v_in, (gng8, gng_others))
    groups = [[(late[k] if p in LATE_PARAMS else main[k])[p] for p in PARAMS] for k in range(4)]
    k_in = PARAMS.index("w_in")
    for group in groups:
        group[k_in] = group[k_in].T
    grads, delta, new_m, new_v = groups
    return (loss.reshape(()), grad_x, *grads, *delta, *new_m, *new_v)
```

```python
import functools

import jax
import jax.numpy as jnp
from jax import lax
from jax.experimental import pallas as pl
from jax.experimental.pallas import tpu as pltpu

F32 = jnp.float32
BF16 = jnp.bfloat16

D_MODEL = 1024
W_MIX = 512
D_IN = 5 * W_MIX
POOL_WINDOWS = (2, 4, 8, 16)
POOL_GC = 128
CONV_WIDTH = 31
RMS_EPS = 1e-6
LN_EPS = 1e-5
N_DEV = 8

LANES = 128
SUBLANES = 8
TILE_M = 512
TILE_IN = 1024
POOL_HALO = 16
CONV_HALO = 32
CONV_ROWS = 128
BWD_CONV_ROWS = 64
BWD_SUBTILES = 2
VMEM_LIMIT = 56 * 1024 * 1024

VMEM = pl.BlockSpec(memory_space=pltpu.VMEM)


def _silu(v):
    return v * jax.nn.sigmoid(v)


def _colsum8(v):
    m, n = v.shape
    return jnp.sum(v.reshape(m // SUBLANES, SUBLANES, n), axis=0)


def _row_index(shape, t0):
    return lax.broadcasted_iota(jnp.int32, shape, 0) + t0


def _conv_taps():
    taps = {s: [] for s in range(SUBLANES)}
    for k in range(CONV_WIDTH):
        e = k + CONV_HALO - (CONV_WIDTH - 1)
        taps[e % SUBLANES].append((e // SUBLANES, k))
    return taps


def _fwd_body(x_ref, tgt_ref, proj_ref, poolw_ref, poolb_ref, pools_ref, dw_ref, cb_ref, lg_ref, lb_ref,
              pww_ref, pwb_ref, wout_ref, fg_ref,
              h1_ref, z_ref, ob_ref, dpool_ref, dh_ref, loss_ref, gfg_ref,
              uext, hext, y_s):
    b = pl.program_id(0)
    i = pl.program_id(1)
    tm = TILE_M

    @pl.when(i == 0)
    def _():
        uext[0:POOL_HALO, :] = jnp.zeros((POOL_HALO, W_MIX), F32)
        hext[0:CONV_HALO, :] = jnp.zeros((CONV_HALO, W_MIX), F32)
        hext[CONV_HALO + tm:, :] = jnp.zeros((SUBLANES, W_MIX), F32)

    @pl.when((b == 0) & (i == 0))
    def _():
        loss_ref[...] = jnp.zeros_like(loss_ref)
        gfg_ref[...] = jnp.zeros_like(gfg_ref)

    uext[POOL_HALO:, :] = proj_ref[0, :, 0:W_MIX]
    t = _row_index((tm, POOL_GC), i * tm)
    for g, w in enumerate(POOL_WINDOWS):
        lanes = slice(g * POOL_GC, (g + 1) * POOL_GC)
        e = uext[:, lanes]
        s = e
        sh = 1
        while sh < w:
            s = s + pltpu.roll(s, sh, 0)
            sh *= 2
        inv_cnt = 1.0 / jnp.minimum(t + 1, w).astype(F32)
        d = (s[POOL_HALO:] * inv_cnt - e[POOL_HALO:]).astype(BF16)
        dpool_ref[0, :, lanes] = d
        z = jnp.dot(d, poolw_ref[g].astype(BF16), preferred_element_type=F32) + poolb_ref[:, lanes]
        z_ref[0, :, lanes] = z
        a_gate = proj_ref[0, :, W_MIX + g * POOL_GC:W_MIX + (g + 1) * POOL_GC]
        y_s[:, lanes] = (z * pools_ref[:, lanes] * _silu(a_gate)).astype(BF16)

    hext[CONV_HALO:CONV_HALO + tm, :] = proj_ref[0, :, 2 * W_MIX:3 * W_MIX] * jax.nn.sigmoid(proj_ref[0, :, 3 * W_MIX:4 * W_MIX])
    taps = _conv_taps()
    for c0 in range(0, tm, CONV_ROWS):
        for l0 in range(0, W_MIX, LANES):
            lanes = slice(l0, l0 + LANES)
            acc = jnp.zeros((CONV_ROWS, LANES), F32) + cb_ref[:, lanes]
            for s in range(SUBLANES):
                part = jnp.zeros((CONV_ROWS + SUBLANES, LANES), F32)
                for q, k in taps[s]:
                    r0 = c0 + SUBLANES * q
                    part = part + hext[r0:r0 + CONV_ROWS + SUBLANES, lanes] * dw_ref[k:k + 1, lanes]
                acc = acc + part[s:s + CONV_ROWS]
            h1_ref[0, c0:c0 + CONV_ROWS, lanes] = acc
    h1 = h1_ref[0]
    mu = jnp.mean(h1, axis=-1, keepdims=True)
    xc = h1 - mu
    rstd = lax.rsqrt(jnp.mean(xc * xc, axis=-1, keepdims=True) + LN_EPS)
    h3 = _silu(xc * rstd * lg_ref[...] + lb_ref[...])
    ob = jnp.dot(h3.astype(BF16), pww_ref[...], preferred_element_type=F32) + pwb_ref[...]
    ob_ref[0] = ob
    y_s[:, W_MIX:] = (ob * _silu(proj_ref[0, :, 4 * W_MIX:5 * W_MIX])).astype(BF16)

    h = x_ref[0] + jnp.dot(y_s[...], wout_ref[...], preferred_element_type=F32)
    r2 = lax.rsqrt(jnp.mean(h * h, axis=-1, keepdims=True) + RMS_EPS)
    hr = h * r2
    err = hr * fg_ref[...] - tgt_ref[0]
    loss_ref[...] += jnp.sum(err * err) * (0.5 / D_MODEL)
    dout = err * (1.0 / D_MODEL)
    gfg_ref[...] += _colsum8(dout * hr)
    gy = dout * fg_ref[...]
    dh_ref[0] = r2 * (gy - hr * jnp.mean(gy * hr, axis=-1, keepdims=True))

    uext[0:POOL_HALO, :] = uext[tm:tm + POOL_HALO, :]
    hext[0:CONV_HALO, :] = hext[tm:tm + CONV_HALO, :]


def _row(v):
    return v.reshape(1, -1)


def _fwd(x, tgt, proj, pool_w, pool_b, pool_scale, conv_dw_f, conv_b, ln_g, ln_b, pw_w_b, pw_b, w_out_b, final_g):
    bl, s, _ = x.shape
    tm = TILE_M
    tok = lambda n: pl.BlockSpec((1, tm, n), lambda b, i: (b, i, 0))
    acc = lambda n: pl.BlockSpec((SUBLANES, n), lambda b, i: (0, 0))
    act = lambda n: jax.ShapeDtypeStruct((bl, s, n), F32)
    return pl.pallas_call(
        functools.partial(_fwd_body),
        name="fwd",
        grid=(bl, s // tm),
        in_specs=[tok(D_MODEL), tok(D_MODEL), tok(D_IN)] + [VMEM] * 11,
        out_specs=[tok(W_MIX), tok(W_MIX), tok(W_MIX), tok(W_MIX), tok(D_MODEL), acc(LANES), acc(D_MODEL)],
        out_shape=[act(W_MIX), act(W_MIX), act(W_MIX), jax.ShapeDtypeStruct((bl, s, W_MIX), BF16), act(D_MODEL),
                   jax.ShapeDtypeStruct((SUBLANES, LANES), F32), jax.ShapeDtypeStruct((SUBLANES, D_MODEL), F32)],
        scratch_shapes=[pltpu.VMEM((POOL_HALO + tm, W_MIX), F32), pltpu.VMEM((CONV_HALO + tm + SUBLANES, W_MIX), F32),
                        pltpu.VMEM((tm, D_MODEL), BF16)],
        compiler_params=pltpu.CompilerParams(dimension_semantics=("arbitrary", "arbitrary"), vmem_limit_bytes=VMEM_LIMIT),
    )(x, tgt, proj, pool_w, pool_b.reshape(1, W_MIX), _row(pool_scale), conv_dw_f, _row(conv_b),
      _row(ln_g), _row(ln_b), pw_w_b, _row(pw_b), w_out_b, _row(final_g))


NT_DIMS = (((1,), (1,)), ((), ()))
TN_DIMS = (((0,), (0,)), ((), ()))
V_POOL_B, V_POOL_SCALE, V_CONV_B, V_LN_G, V_LN_B, V_PW_B = range(6)


def _bwd_taps():
    taps = {s: [] for s in range(SUBLANES)}
    for k in range(CONV_WIDTH):
        e = CONV_WIDTH - 1 - k
        taps[e % SUBLANES].append((e // SUBLANES, k))
    return taps


def _bwd1_body(n_tiles, dh_ref, proj_ref, h1_ref, z_ref, ob_ref, dpool_ref, wout_ref, pww_ref, poolw_ref, pools_ref, dw_ref,
               lg_ref, lb_ref,
               dproj_ref, gwout_ref, gpww_ref, gpoolw_ref, gvec_ref, gdw_ref,
               vext, gext, gsh, y_s):
    b = pl.program_id(0)
    i = pl.program_id(1)
    tm = TILE_M
    t0 = (n_tiles - 1 - i) * tm

    @pl.when(i == 0)
    def _():
        vext[tm:, :] = jnp.zeros((POOL_HALO, W_MIX), F32)
        gext[tm:, :] = jnp.zeros((CONV_HALO, W_MIX), F32)

    @pl.when((b == 0) & (i == 0))
    def _():
        gwout_ref[...] = jnp.zeros_like(gwout_ref)
        gpww_ref[...] = jnp.zeros_like(gpww_ref)
        gpoolw_ref[...] = jnp.zeros_like(gpoolw_ref)
        gvec_ref[...] = jnp.zeros_like(gvec_ref)
        gdw_ref[...] = jnp.zeros_like(gdw_ref)

    sub = tm // BWD_SUBTILES

    def dense_phase(r0):
        rs = slice(r0, r0 + sub)
        dhb = dh_ref[0, rs, :].astype(BF16)
        dy_a = lax.dot_general(dhb, wout_ref[0:W_MIX, :], NT_DIMS, preferred_element_type=F32)
        dy_b = lax.dot_general(dhb, wout_ref[W_MIX:, :], NT_DIMS, preferred_element_type=F32)

        a_gate = proj_ref[0, rs, W_MIX:2 * W_MIX]
        sig_a = jax.nn.sigmoid(a_gate)
        silu_a = a_gate * sig_a
        z = z_ref[0, rs, :]
        out_a = z * pools_ref[...]
        y_s[rs, 0:W_MIX] = (out_a * silu_a).astype(BF16)
        d_out_a = dy_a * silu_a
        d_ag = dy_a * out_a * (sig_a * (1.0 + a_gate * (1.0 - sig_a)))
        dproj_ref[0, rs, W_MIX:2 * W_MIX] = d_ag.astype(BF16)
        gvec_ref[V_POOL_SCALE] += _colsum8(d_out_a * z)
        dz = d_out_a * pools_ref[...]
        gvec_ref[V_POOL_B] += _colsum8(dz)
        t = _row_index((sub, POOL_GC), t0 + r0)
        n_ext = sub + POOL_HALO
        for g, w in enumerate(POOL_WINDOWS):
            lanes = slice(g * POOL_GC, (g + 1) * POOL_GC)
            dzg = dz[:, lanes].astype(BF16)
            gpoolw_ref[g] += lax.dot_general(dpool_ref[0, rs, lanes], dzg, TN_DIMS, preferred_element_type=F32)
            dd = lax.dot_general(dzg, poolw_ref[g].astype(BF16), NT_DIMS, preferred_element_type=F32)
            vext[rs, lanes] = dd * (1.0 / jnp.minimum(t + 1, w).astype(F32))
            s = vext[r0:r0 + n_ext, lanes]
            sh = 1
            while sh < w:
                s = s + pltpu.roll(s, n_ext - sh, 0)
                sh *= 2
            dproj_ref[0, rs, lanes] = (s[0:sub] - dd).astype(BF16)

        b_gate = proj_ref[0, rs, 4 * W_MIX:5 * W_MIX]
        sig_b = jax.nn.sigmoid(b_gate)
        silu_b = b_gate * sig_b
        ob = ob_ref[0, rs, :]
        y_s[rs, W_MIX:] = (ob * silu_b).astype(BF16)
        d_ob = dy_b * silu_b
        dproj_ref[0, rs, 4 * W_MIX:5 * W_MIX] = (dy_b * ob * (sig_b * (1.0 + b_gate * (1.0 - sig_b)))).astype(BF16)
        gvec_ref[V_PW_B] += _colsum8(d_ob)
        gwout_ref[...] += lax.dot_general(y_s[rs, :], dhb, TN_DIMS, preferred_element_type=F32)

        h1 = h1_ref[0, rs, :]
        mu = jnp.mean(h1, axis=-1, keepdims=True)
        xc = h1 - mu
        rstd = lax.rsqrt(jnp.mean(xc * xc, axis=-1, keepdims=True) + LN_EPS)
        xhat = xc * rstd
        h2 = xhat * lg_ref[...] + lb_ref[...]
        sig2 = jax.nn.sigmoid(h2)
        d_obb = d_ob.astype(BF16)
        gpww_ref[...] += lax.dot_general((h2 * sig2).astype(BF16), d_obb, TN_DIMS, preferred_element_type=F32)
        dh2 = lax.dot_general(d_obb, pww_ref[...], NT_DIMS, preferred_element_type=F32) * (sig2 * (1.0 + h2 * (1.0 - sig2)))
        gvec_ref[V_LN_G] += _colsum8(dh2 * xhat)
        gvec_ref[V_LN_B] += _colsum8(dh2)
        dxh = dh2 * lg_ref[...]
        dh1 = rstd * (dxh - jnp.mean(dxh, axis=-1, keepdims=True) - xhat * jnp.mean(dxh * xhat, axis=-1, keepdims=True))
        gvec_ref[V_CONV_B] += _colsum8(dh1)
        gext[rs, :] = dh1

    taps = _bwd_taps()
    cr = BWD_CONV_ROWS
    win = cr + CONV_HALO - SUBLANES

    def conv_phase(r0):
        for c0 in range(r0, r0 + sub, cr):
            for l0 in range(0, W_MIX, LANES):
                lanes = slice(l0, l0 + LANES)
                b_val = proj_ref[0, c0:c0 + cr, 2 * W_MIX + l0:2 * W_MIX + l0 + LANES]
                sg = jax.nn.sigmoid(proj_ref[0, c0:c0 + cr, 3 * W_MIX + l0:3 * W_MIX + l0 + LANES])
                h0c = b_val * sg
                acc = jnp.zeros((cr, LANES), F32)
                for s in range(SUBLANES):
                    gsh[...] = gext[c0 + s:c0 + s + win, lanes]
                    for q, k in taps[s]:
                        gq = gsh[SUBLANES * q:SUBLANES * q + cr, :]
                        acc = acc + gq * dw_ref[k:k + 1, lanes]
                        gdw_ref[k, :, lanes] += _colsum8(h0c * gq)
                d_bval = acc * sg
                dproj_ref[0, c0:c0 + cr, 2 * W_MIX + l0:2 * W_MIX + l0 + LANES] = d_bval.astype(BF16)
                dproj_ref[0, c0:c0 + cr, 3 * W_MIX + l0:3 * W_MIX + l0 + LANES] = (d_bval * b_val * (1.0 - sg)).astype(BF16)

    starts = [k * sub for k in reversed(range(BWD_SUBTILES))]
    for r0 in starts:
        dense_phase(r0)
    for r0 in starts:
        conv_phase(r0)
    vext[tm:, :] = vext[0:POOL_HALO, :]
    gext[tm:, :] = gext[0:CONV_HALO, :]


def _bwd1(dh, proj, h1, z, ob, dpool, w_out_b, pw_w_b, pool_w, pool_scale, conv_dw_f, ln_g, ln_b):
    bl, s, _ = dh.shape
    tm = TILE_M
    nt = s // tm
    tok = lambda n: pl.BlockSpec((1, tm, n), lambda b, i: (b, nt - 1 - i, 0))
    res = lambda shape: pl.BlockSpec(shape, lambda b, i: (0,) * len(shape))
    out_shapes = [(D_MODEL, D_MODEL), (W_MIX, W_MIX), (len(POOL_WINDOWS), POOL_GC, POOL_GC), (6, SUBLANES, W_MIX),
                  (CONV_WIDTH, SUBLANES, W_MIX)]
    return pl.pallas_call(
        functools.partial(_bwd1_body, nt),
        name="bwd1",
        grid=(bl, nt),
        in_specs=[tok(D_MODEL), tok(D_IN), tok(W_MIX), tok(W_MIX), tok(W_MIX), tok(W_MIX)] + [VMEM] * 7,
        out_specs=[tok(D_IN)] + [res(sh) for sh in out_shapes],
        out_shape=[jax.ShapeDtypeStruct((bl, s, D_IN), BF16)] + [jax.ShapeDtypeStruct(sh, F32) for sh in out_shapes],
        scratch_shapes=[pltpu.VMEM((tm + POOL_HALO, W_MIX), F32), pltpu.VMEM((tm + CONV_HALO, W_MIX), F32),
                        pltpu.VMEM((BWD_CONV_ROWS + CONV_HALO - SUBLANES, LANES), F32), pltpu.VMEM((tm, D_MODEL), BF16)],
        compiler_params=pltpu.CompilerParams(dimension_semantics=("arbitrary", "arbitrary"), vmem_limit_bytes=VMEM_LIMIT),
    )(dh, proj, h1, z, ob, dpool, w_out_b, pw_w_b, pool_w, _row(pool_scale), conv_dw_f, _row(ln_g), _row(ln_b))


MESH = pl.DeviceIdType.MESH
W_IN_BLK = D_IN // N_DEV
W_OUT_BLK = D_MODEL // N_DEV
PW_BLK = W_MIX // N_DEV
DW_BLK = W_MIX // N_DEV
N_PEER = N_DEV - 1


SLAB = 2 * W_IN_BLK
N_SLABS = 4
F_SIB, F_X, F_Y, F_ON, F_PASS_X, F_PASS_Y, F_PASS_D = range(7)
W_IN_ONLY, OTHERS = (0,), (1, 2, 3)


def _inproj_body(n_tiles, x_hbm, ng_ref, wint_ref, wout_ref, pww_ref, dw_ref,
                 proj_hbm, wt_hbm, woutf_hbm, pwwf_hbm, dwf_ref,
                 hn_s, wt_s, wout_s, pww_s, dw_blk, stage, xbuf, out_sems, x_sems, w_sems, send_sems, recv_sems):
    s = pl.program_id(0)
    i = pl.program_id(1)
    tm = TILE_IN
    x, y, c = lax.axis_index("x"), lax.axis_index("y"), lax.axis_index("c")
    sibling = (x, y, 1 - c)
    xn, yn, dg = (1 - x, y), (x, 1 - y), (1 - x, 1 - y)
    pick = lambda a, b: tuple(jnp.where(c == 1, p, q) for p, q in zip(a, b))
    on_from, on_to = pick(xn, yn), pick(yn, xn)

    def blocks(px, py, pc):
        k = 4 * px + 2 * py + pc
        rows = lambda ref, n: ref.at[pl.ds(pl.multiple_of(k * n, n), n), :]
        return (rows(wt_s, W_IN_BLK), rows(wout_s, W_OUT_BLK), rows(pww_s, PW_BLK), dw_blk.at[k])

    def copies(arrays, flow, block, to):
        refs = blocks(*block)
        return [pltpu.make_async_remote_copy(src_ref=refs[a], dst_ref=refs[a], send_sem=send_sems.at[a, flow],
                                             recv_sem=recv_sems.at[a, flow], device_id=to, device_id_type=MESH)
                for a in arrays]

    def start(arrays):
        for flow, to in ((F_SIB, sibling), (F_X, (*xn, c)), (F_Y, (*yn, c))):
            for cp in copies(arrays, flow, (x, y, c), to):
                cp.start()

    def direct(arrays):
        for flow, chip in ((F_X, xn), (F_Y, yn)):
            for cp in copies(arrays, flow, (*chip, c), sibling):
                cp.wait_recv()
        for cp in copies(arrays, F_ON, (*on_from, c), (*on_to, c)):
            cp.start()
        for flow, chip in ((F_PASS_X, xn), (F_PASS_Y, yn)):
            for cp in copies(arrays, flow, (*chip, c), sibling):
                cp.start()

    def diagonal(arrays):
        for cp in copies(arrays, F_ON, (*dg, c), sibling):
            cp.wait_recv()
        for cp in copies(arrays, F_PASS_D, (*dg, c), sibling):
            cp.start()

    def from_sibling(arrays, flow, chip):
        for cp in copies(arrays, flow, (*chip, 1 - c), sibling):
            cp.wait_recv()

    def finish(arrays):
        mine = (x, y, c)
        for flow, block in ((F_SIB, mine), (F_X, mine), (F_Y, mine), (F_ON, (*on_from, c)), (F_PASS_X, (*xn, c)),
                            (F_PASS_Y, (*yn, c)), (F_PASS_D, (*dg, c))):
            for cp in copies(arrays, flow, block, sibling):
                cp.wait_send()

    step = s * n_tiles + i
    par = step % 2
    first = step == 0

    @pl.when(first)
    def _():
        mine = blocks(x, y, c)
        mine[0][...] = wint_ref[...].astype(BF16)
        mine[1][...] = wout_ref[...].astype(BF16)
        mine[2][...] = pww_ref[...].astype(BF16)
        mine[3][...] = dw_ref[...]
        start(W_IN_ONLY)
        start(OTHERS)

    rows_i = pl.ds(pl.multiple_of(i * tm, tm), tm)
    tiles_per_seq = proj_hbm.shape[1] // tm

    def x_copy(k):
        src = x_hbm.at[k // tiles_per_seq, pl.ds(pl.multiple_of((k % tiles_per_seq) * tm, tm), tm), :]
        return pltpu.make_async_copy(src, xbuf.at[k % 2], x_sems.at[k % 2])

    @pl.when(first)
    def _():
        x_copy(0).start()

    @pl.when(s == 0)
    def _():
        x_copy(i).wait()

        @pl.when(i + 1 < n_tiles)
        def _():
            x_copy(i + 1).start()

        xt = xbuf[i % 2]
        r1 = lax.rsqrt(jnp.mean(xt * xt, axis=-1, keepdims=True) + RMS_EPS)
        hn_s[rows_i, :] = (xt * r1 * ng_ref[...]).astype(BF16)

    @pl.when(first)
    def _():
        from_sibling(W_IN_ONLY, F_SIB, (x, y))

    @pl.when((s == 1) & (i == 0))
    def _():
        direct(W_IN_ONLY)
        from_sibling(W_IN_ONLY, F_PASS_Y, yn)

    @pl.when((s == 2) & (i == 0))
    def _():
        from_sibling(W_IN_ONLY, F_PASS_X, xn)
        direct(OTHERS)

    weights_out = [pltpu.make_async_copy(src, dst, w_sems.at[k])
                   for k, (src, dst) in enumerate(((wt_s, wt_hbm), (wout_s, woutf_hbm), (pww_s, pwwf_hbm)))]

    @pl.when((s == 3) & (i == 0))
    def _():
        diagonal(W_IN_ONLY)
        from_sibling(W_IN_ONLY, F_PASS_D, dg)
        weights_out[0].start()

    def out_copy(parity, col0):
        dst = proj_hbm.at[i // tiles_per_seq, pl.ds(pl.multiple_of((i % tiles_per_seq) * tm, tm), tm),
                          pl.ds(pl.multiple_of(col0, LANES), SLAB)]
        return pltpu.make_async_copy(stage.at[parity], dst, out_sems.at[parity])

    @pl.when(step >= 2)
    def _():
        out_copy(par, 0).wait()

    chip = jnp.where(s < 2, 2 * x, 2 * (1 - x)) + jnp.where((s == 0) | (s == 2), y, 1 - y)
    slab = wt_s[pl.ds(pl.multiple_of(chip * SLAB, SLAB), SLAB), :]
    stage[par] = lax.dot_general(hn_s[rows_i, :], slab, NT_DIMS, preferred_element_type=F32)
    out_copy(par, chip * SLAB).start()

    @pl.when(step == N_SLABS * n_tiles - 1)
    def _():
        out_copy(par, 0).wait()
        out_copy(1 - par, 0).wait()
        from_sibling(OTHERS, F_SIB, (x, y))
        from_sibling(OTHERS, F_PASS_X, xn)
        from_sibling(OTHERS, F_PASS_Y, yn)
        diagonal(OTHERS)
        from_sibling(OTHERS, F_PASS_D, dg)
        for cp in weights_out[1:]:
            cp.start()
        finish(W_IN_ONLY)
        finish(OTHERS)
        for cp in weights_out:
            cp.wait()
        for k in range(N_DEV):
            dwf_ref[:, k * DW_BLK:(k + 1) * DW_BLK] = dw_blk[k]


def _inproj(x, norm_g, w_in_t, w_out, pw_w, conv_dw):
    bl, s, _ = x.shape
    tm = TILE_IN
    nt = s // tm
    n_tiles = bl * nt
    dma = pltpu.SemaphoreType.DMA
    return pl.pallas_call(
        functools.partial(_inproj_body, n_tiles),
        name="inproj",
        grid=(N_SLABS, n_tiles),
        in_specs=[pl.BlockSpec(memory_space=pl.ANY)] + [VMEM] * 5,
        out_specs=[pl.BlockSpec(memory_space=pl.ANY)] * 4 + [VMEM],
        out_shape=[jax.ShapeDtypeStruct((bl, s, D_IN), F32), jax.ShapeDtypeStruct((D_IN, D_MODEL), BF16),
                   jax.ShapeDtypeStruct((D_MODEL, D_MODEL), BF16), jax.ShapeDtypeStruct((W_MIX, W_MIX), BF16),
                   jax.ShapeDtypeStruct((CONV_WIDTH, W_MIX), F32)],
        scratch_shapes=[pltpu.VMEM((n_tiles * tm, D_MODEL), BF16), pltpu.VMEM((D_IN, D_MODEL), BF16),
                        pltpu.VMEM((D_MODEL, D_MODEL), BF16), pltpu.VMEM((W_MIX, W_MIX), BF16),
                        pltpu.VMEM((N_DEV, CONV_WIDTH, DW_BLK), F32), pltpu.VMEM((2, tm, SLAB), F32),
                        pltpu.VMEM((2, tm, D_MODEL), F32), dma((2,)), dma((2,)), dma((3,)), dma((4, 7)), dma((4, 7))],
        compiler_params=pltpu.CompilerParams(dimension_semantics=("arbitrary", "arbitrary"), vmem_limit_bytes=VMEM_LIMIT),
    )(x, _row(norm_g), w_in_t, w_out, pw_w, conv_dw)


B_PLAIN, B_ON, B_MERGED = range(3)


def _routes():
    x, y, c = lax.axis_index("x"), lax.axis_index("y"), lax.axis_index("c")
    xn, yn = (1 - x, y), (x, 1 - y)
    pick = lambda a, b: tuple(jnp.where(c == 1, p, q) for p, q in zip(a, b))
    return pick(xn, yn), pick(yn, xn), (1 - x, 1 - y), (x, y), c


def _chip_index(chip):
    return 2 * chip[0] + chip[1]


def _copy(src, dst, send_sems, recv_sems, a, k, to):
    return pltpu.make_async_remote_copy(src_ref=src, dst_ref=dst, send_sem=send_sems.at[a, k], recv_sem=recv_sems.at[a, k],
                                        device_id=to, device_id_type=MESH)


def _rsb_send(plain, on, rb_plain, rb_on, send_b, recv_b):
    first, _, _, _, c = _routes()
    for k, srcs, dsts in ((B_ON, on, rb_on), (B_PLAIN, plain, rb_plain)):
        for a, (s, d) in enumerate(zip(srcs, dsts)):
            _copy(s, d, send_b, recv_b, a, k, (*first, c)).start()


def _rsb_merge(second_sums, merged, rb_on, rb_merged, send_b, recv_b):
    _, second, _, _, c = _routes()
    for a, (own, m, got, dst) in enumerate(zip(second_sums, merged, rb_on, rb_merged)):
        _copy(got, got, send_b, recv_b, a, B_ON, (*second, c)).wait_recv()
        m[...] = (own[...] + got[...].astype(F32)).astype(BF16)
        _copy(m, dst, send_b, recv_b, a, B_MERGED, (*second, c)).start()


def _rsb_total(own_sums, plain, on, merged, rb_plain, rb_merged, send_b, recv_b):
    _, second, _, _, c = _routes()
    out = []
    for a, (own, p, g) in enumerate(zip(own_sums, rb_plain, rb_merged)):
        _copy(p, p, send_b, recv_b, a, B_PLAIN, (*second, c)).wait_recv()
        _copy(g, g, send_b, recv_b, a, B_MERGED, (*second, c)).wait_recv()
        out.append(own[...] + p[...].astype(F32) + g[...].astype(F32))
    for k, srcs in ((B_PLAIN, plain), (B_ON, on), (B_MERGED, merged)):
        for a, s in enumerate(srcs):
            _copy(s, s, send_b, recv_b, a, k, (*second, c)).wait_send()
    return out


R_LOSS, R_FINAL_G, R_POOL, R_CONV, R_LN = range(5)
VEC_ROWS = 16
DW_ROWS = 32
A_PART, A_X, A_Y, A_ON, A_HALF = range(5)
HALF_DTYPES = (F32, BF16, BF16)
STEP_SUMS, STEP_MERGE, STEP_HALF = 1, 3, 5


def _bwd2a_body(n_steps, x_ref, dproj_ref, ng_ref, gwout_hbm, gpww_hbm, loss_ref, gfg_ref, gvec_ref, gdw_ref, gpoolw_ref,
                win_own_ref, win_staged_ref, owout_ref, opww_ref, ovec_ref, odw_ref, opoolw_ref,
                acc, st_win, a_win, own_wout, own_pww, a_wout, a_pww, st_wout, st_pww, rb_wout, rb_pww, part_vec, part_dw,
                part_pw, sib_vec, sib_dw, sib_pw, res_vec, res_dw, res_pw, half_vec, half_dw, half_pw,
                own_sems, send_a, recv_a, send_b, recv_b, send_r, recv_r, send_w, recv_w):
    step = pl.program_id(0) * pl.num_programs(1) + pl.program_id(1)
    x, y, c = lax.axis_index("x"), lax.axis_index("y"), lax.axis_index("c")
    sibling = (x, y, 1 - c)
    xn, yn = (1 - x, y), (x, 1 - y)
    first, second, diag, own, _ = _routes()
    rows = lambda ref, k, n: ref.at[pl.ds(pl.multiple_of(k * n, n), n), :]

    blocks = lambda k: (rows(gwout_hbm, k, W_OUT_BLK), rows(gpww_hbm, k, PW_BLK))
    mine = lambda chip: (own_wout.at[chip], own_pww.at[chip])
    mine_copies = lambda chip: [pltpu.make_async_copy(s, d, own_sems.at[a, chip])
                                for a, (s, d) in enumerate(zip(blocks(2 * chip + c), mine(chip)))]
    landed = lambda chip: (a_wout.at[chip], a_pww.at[chip])
    staged = lambda k: (st_wout.at[k], st_pww.at[k])
    arrived = lambda k: (rb_wout.at[k], rb_pww.at[k])

    parts, sibs, halves = (part_vec, part_dw, part_pw), (sib_vec, sib_dw, sib_pw), (half_vec, half_dw, half_pw)
    results = (res_vec, res_dw, res_pw)
    half_rows = lambda ref: pl.ds(pl.multiple_of(c * (ref.shape[0] // 2), SUBLANES), ref.shape[0] // 2)
    slot = lambda chip: tuple(h.at[_chip_index(chip)] for h in halves)

    @pl.when(step == 0)
    def _():
        acc[...] = jnp.zeros_like(acc)
        sum8 = lambda v: jnp.sum(v, axis=0, keepdims=True)
        part_vec[...] = jnp.zeros((VEC_ROWS, D_MODEL), F32)
        part_vec[R_LOSS:R_LOSS + 1, :] = jnp.broadcast_to(loss_ref[0:1, 0:1], (1, D_MODEL))
        part_vec[R_FINAL_G:R_FINAL_G + 1, :] = sum8(gfg_ref[...])
        for row, (lo, hi) in ((R_POOL, (V_POOL_B, V_POOL_SCALE)), (R_CONV, (V_CONV_B, V_LN_G)), (R_LN, (V_LN_B, V_PW_B))):
            part_vec[row:row + 1, 0:W_MIX] = sum8(gvec_ref[lo])
            part_vec[row:row + 1, W_MIX:] = sum8(gvec_ref[hi])
        part_dw[0:CONV_WIDTH, :] = jnp.sum(gdw_ref[...], axis=1)
        part_dw[CONV_WIDTH:, :] = jnp.zeros((DW_ROWS - CONV_WIDTH, W_MIX), F32)
        part_pw[...] = gpoolw_ref[...].reshape(len(POOL_WINDOWS) * POOL_GC, POOL_GC)
        for a, (p, s) in enumerate(zip(parts, sibs)):
            _copy(p, s, send_r, recv_r, a, A_PART, sibling).start()
        for chip in range(4):
            for a, (s, d) in enumerate(zip(blocks(2 * chip + (1 - c)), landed(chip))):
                _copy(s, d, send_a, recv_a, a, chip, sibling).start()
            for cp in mine_copies(chip):
                cp.start()

    x_t = x_ref[0]
    r1 = lax.rsqrt(jnp.mean(x_t * x_t, axis=-1, keepdims=True) + RMS_EPS)
    hn = (x_t * r1 * ng_ref[...]).astype(BF16)
    acc[...] += lax.dot_general(dproj_ref[0], hn, TN_DIMS, preferred_element_type=F32)

    @pl.when(step == STEP_SUMS)
    def _():
        for a, (p, s, h) in enumerate(zip(parts, sibs, slot(own))):
            _copy(p, s, send_r, recv_r, a, A_PART, sibling).wait_recv()
            h[...] = (p[half_rows(p), :] + s[half_rows(s), :]).astype(h.dtype)
            for k, to in ((A_X, xn), (A_Y, yn)):
                _copy(h, h, send_r, recv_r, a, k, (*to, c)).start()
        for chip in range(4):
            for cp in mine_copies(chip):
                cp.wait()
            for a, (m, l) in enumerate(zip(mine(chip), landed(chip))):
                _copy(l, l, send_a, recv_a, a, chip, sibling).wait_recv()
                l[...] = m[...] + l[...]
        for k, chip in ((B_ON, diag), (B_PLAIN, first)):
            for s, l in zip(staged(k), landed(_chip_index(chip))):
                s[...] = l[...].astype(BF16)
        _rsb_send(staged(B_PLAIN), staged(B_ON), arrived(B_PLAIN), arrived(B_ON), send_b, recv_b)

    @pl.when(step == STEP_MERGE)
    def _():
        _rsb_merge(landed(_chip_index(second)), staged(B_MERGED), arrived(B_ON), arrived(B_MERGED), send_b, recv_b)
        for a in range(3):
            for k, chip in ((A_X, xn), (A_Y, yn)):
                h = slot(chip)[a]
                _copy(h, h, send_r, recv_r, a, k, sibling).wait_recv()
            h = slot(first)[a]
            _copy(h, h, send_r, recv_r, a, A_ON, (*second, c)).start()

    @pl.when(step == STEP_HALF)
    def _():
        for a, (r, hs) in enumerate(zip(results, halves)):
            h = slot(diag)[a]
            _copy(h, h, send_r, recv_r, a, A_ON, sibling).wait_recv()
            mine = r.at[half_rows(r), :]
            chip_sums = [hs[k].astype(F32) for k in range(4)]
            mine[...] = (chip_sums[0] + chip_sums[1]) + (chip_sums[2] + chip_sums[3])
            _copy(mine, mine, send_r, recv_r, a, A_HALF, sibling).start()

    @pl.when(step == n_steps - 1)
    def _():
        block = lambda k: rows(acc, k, W_IN_BLK)
        w_copy = lambda chip: _copy(st_win.at[chip], a_win.at[chip], send_w, recv_w, 0, chip, sibling)
        for chip in range(4):
            st_win[chip] = block(2 * chip + (1 - c))[...].astype(BF16)
            w_copy(chip).start()

        owout_ref[...], opww_ref[...] = _rsb_total(landed(_chip_index(own)), staged(B_PLAIN), staged(B_ON), staged(B_MERGED),
                                                   arrived(B_PLAIN), arrived(B_MERGED), send_b, recv_b)
        for a, (r, p, s) in enumerate(zip(results, parts, sibs)):
            theirs = r.at[pl.ds(pl.multiple_of((1 - c) * (r.shape[0] // 2), SUBLANES), r.shape[0] // 2), :]
            _copy(theirs, theirs, send_r, recv_r, a, A_HALF, sibling).wait_recv()
            _copy(p, s, send_r, recv_r, a, A_PART, sibling).wait_send()
            for k, h in ((A_X, slot(own)[a]), (A_Y, slot(own)[a]), (A_ON, slot(first)[a])):
                _copy(h, h, send_r, recv_r, a, k, sibling).wait_send()
            mine = r.at[half_rows(r), :]
            _copy(mine, mine, send_r, recv_r, a, A_HALF, sibling).wait_send()
        for chip in range(4):
            for a, (s, d) in enumerate(zip(blocks(2 * chip + (1 - c)), landed(chip))):
                _copy(s, d, send_a, recv_a, a, chip, sibling).wait_send()
        ovec_ref[...] = res_vec[...]
        opoolw_ref[...] = res_pw[...]
        for k in range(N_DEV):
            @pl.when(4 * x + 2 * y + c == k)
            def _():
                odw_ref[...] = res_dw[0:CONV_WIDTH, k * DW_BLK:(k + 1) * DW_BLK]

        for chip in range(4):
            w_copy(chip).wait()
        chip_sum = lambda chip: block(2 * _chip_index(chip) + c)[...] + a_win[_chip_index(chip)].astype(F32)
        win_own_ref[...] = chip_sum(own)
        for j, chip in enumerate((xn, yn, diag)):
            win_staged_ref[j] = chip_sum(chip).astype(BF16)


def _bwd2a(x, dproj, norm_g, gwout, gpww, loss8, gfg8, gvec, gdw8, gpoolw):
    bl, s, _ = x.shape
    tm = TILE_M
    nt = s // tm
    tok = lambda n: pl.BlockSpec((1, tm, n), lambda b, i: (b, i, 0))
    n_pw = len(POOL_WINDOWS) * POOL_GC
    win = (W_IN_BLK, D_MODEL)
    blk = ((W_OUT_BLK, D_MODEL), (PW_BLK, W_MIX))
    rep = ((VEC_ROWS, D_MODEL), (DW_ROWS, W_MIX), (n_pw, POOL_GC))
    half = lambda sh: (sh[0] // 2,) + sh[1:]
    dma = pltpu.SemaphoreType.DMA
    assert bl * nt > STEP_HALF + 1, "the exchanges' phases need their grid steps"
    return pl.pallas_call(
        functools.partial(_bwd2a_body, bl * nt),
        name="bwd2a",
        grid=(bl, nt),
        in_specs=[tok(D_MODEL), tok(D_IN), VMEM] + [pl.BlockSpec(memory_space=pl.ANY)] * 2 + [VMEM] * 5,
        out_specs=[VMEM] * 7,
        out_shape=[jax.ShapeDtypeStruct(win, F32), jax.ShapeDtypeStruct((3,) + win, BF16)]
        + [jax.ShapeDtypeStruct(sh, F32) for sh in blk + (rep[0], (CONV_WIDTH, DW_BLK), rep[2])],
        scratch_shapes=[pltpu.VMEM((D_IN, D_MODEL), F32), pltpu.VMEM((4,) + win, BF16), pltpu.VMEM((4,) + win, BF16)]
        + [pltpu.VMEM((4,) + sh, F32) for sh in blk] * 2
        + [pltpu.VMEM((3,) + sh, BF16) for sh in blk] * 2
        + [pltpu.VMEM(sh, F32) for sh in rep] * 3
        + [pltpu.VMEM((4,) + half(sh), dt) for sh, dt in zip(rep, HALF_DTYPES)]
        + [dma((2, 4)), dma((2, 4)), dma((2, 4)), dma((2, 3)), dma((2, 3)), dma((3, 5)), dma((3, 5)), dma((1, 4)), dma((1, 4))],
        compiler_params=pltpu.CompilerParams(dimension_semantics=("arbitrary", "arbitrary"), vmem_limit_bytes=VMEM_LIMIT),
    )(x, dproj, _row(norm_g), gwout, gpww, loss8, gfg8, gvec, gdw8, gpoolw)


HBM = pl.BlockSpec(memory_space=pltpu.HBM)
SEM = pl.BlockSpec(memory_space=pltpu.SEMAPHORE)
EFFECT = pltpu.SideEffectType.DATAFLOW_SIDE_EFFECTING


def _owner_copies(src_ref, land_ref, send_sems, recv_sems):
    x, y, c = lax.axis_index("x"), lax.axis_index("y"), lax.axis_index("c")
    owners = ((1 - x, y, c), (x, 1 - y, c), (1 - x, 1 - y, c))
    return [pltpu.make_async_remote_copy(src_ref=src_ref.at[j], dst_ref=land_ref.at[j], send_sem=send_sems.at[j],
                                         recv_sem=recv_sems.at[j], device_id=owners[j], device_id_type=MESH)
            for j in range(3)]


def _rs_start(staged):
    def body(src_ref, land_ref, send_sems, recv_sems, src_thru, land_thru, token):
        for cp in _owner_copies(src_ref, land_ref, send_sems, recv_sems):
            cp.start()
        token[...] = jnp.zeros_like(token)

    dma = pltpu.SemaphoreType.DMA
    return pl.pallas_call(
        body, name="rs_start",
        out_shape=(dma((3,)), dma((3,)), pltpu.HBM(staged.shape, staged.dtype), pltpu.HBM(staged.shape, staged.dtype),
                   jax.ShapeDtypeStruct((SUBLANES, LANES), F32)),
        in_specs=(HBM, HBM), out_specs=(SEM, SEM, HBM, HBM, VMEM), input_output_aliases={0: 2, 1: 3},
        compiler_params=pltpu.CompilerParams(has_side_effects=EFFECT),
    )(pltpu.with_memory_space_constraint(staged, pltpu.HBM),
      pltpu.with_memory_space_constraint(lax.empty(staged.shape, staged.dtype), pltpu.HBM))


def _rs_wait(send_sems, recv_sems, src_thru, land_thru, after):
    def body(src_ref, land_ref, send_sems, recv_sems, after_ref, src_dead, got_ref):
        for cp in _owner_copies(src_ref, land_ref, send_sems, recv_sems):
            cp.wait_send()
            cp.wait_recv()

    return pl.pallas_call(
        body, name="rs_wait",
        out_shape=(pltpu.HBM(src_thru.shape, src_thru.dtype), pltpu.HBM(land_thru.shape, land_thru.dtype)),
        in_specs=(HBM, HBM, SEM, SEM, pl.BlockSpec(memory_space=pl.ANY)), out_specs=(HBM, HBM),
        input_output_aliases={0: 0, 1: 1},
        compiler_params=pltpu.CompilerParams(has_side_effects=EFFECT),
    )(src_thru, land_thru, send_sems, recv_sems, after)[1]


def _bwd2b_body(x_ref, dh_ref, dproj_ref, ng_ref, wint_ref, token_ref, gx_ref, gng_ref):
    @pl.when((pl.program_id(0) == 0) & (pl.program_id(1) == 0))
    def _():
        gng_ref[...] = jnp.zeros_like(gng_ref) + token_ref[0:1, 0:1]

    x_t = x_ref[0]
    r1 = lax.rsqrt(jnp.mean(x_t * x_t, axis=-1, keepdims=True) + RMS_EPS)
    xr = x_t * r1
    dhn = jnp.dot(dproj_ref[0], wint_ref[...], preferred_element_type=F32)
    gng_ref[...] += _colsum8(dhn * xr)
    gy = dhn * ng_ref[...]
    gx_ref[0] = dh_ref[0] + r1 * (gy - xr * jnp.mean(gy * xr, axis=-1, keepdims=True))


def _bwd2b(x, dh, dproj, norm_g, w_in_t, token):
    bl, s, _ = x.shape
    tm = TILE_M
    tok = lambda n: pl.BlockSpec((1, tm, n), lambda b, i: (b, i, 0))
    return pl.pallas_call(
        functools.partial(_bwd2b_body),
        name="bwd2b",
        grid=(bl, s // tm),
        in_specs=[tok(D_MODEL), tok(D_MODEL), tok(D_IN), VMEM, VMEM, VMEM],
        out_specs=[tok(D_MODEL), pl.BlockSpec((SUBLANES, D_MODEL), lambda b, i: (0, 0))],
        out_shape=[jax.ShapeDtypeStruct((bl, s, D_MODEL), F32), jax.ShapeDtypeStruct((SUBLANES, D_MODEL), F32)],
        compiler_params=pltpu.CompilerParams(dimension_semantics=("arbitrary", "arbitrary"), vmem_limit_bytes=VMEM_LIMIT),
    )(x, dh, dproj, _row(norm_g), w_in_t, token)


def _partial_copies(part_ref, land_ref, send_sems, recv_sems):
    x, y, c = lax.axis_index("x"), lax.axis_index("y"), lax.axis_index("c")
    other_chips = [(1 - x, y), (x, 1 - y), (1 - x, 1 - y)]
    peers = [(x, y, 1 - c)] + [(*chip, c) for chip in other_chips] + [(*chip, 1 - c) for chip in other_chips]
    return [pltpu.make_async_remote_copy(src_ref=part_ref, dst_ref=land_ref.at[4 * x + 2 * y + c], send_sem=send_sems.at[r],
                                         recv_sem=recv_sems.at[r], device_id=peers[r], device_id_type=MESH)
            for r in range(N_PEER)]


def _partials_start(part):
    def body(part_ref, land_ref, send_sems, recv_sems, part_thru, land_thru):
        for cp in _partial_copies(part_ref, land_ref, send_sems, recv_sems):
            cp.start()

    dma = pltpu.SemaphoreType.DMA
    land = (N_DEV,) + part.shape
    return pl.pallas_call(
        body, name="partials_start",
        out_shape=(dma((N_PEER,)), dma((N_PEER,)), pltpu.HBM(part.shape, part.dtype), pltpu.HBM(land, part.dtype)),
        in_specs=(HBM, HBM), out_specs=(SEM, SEM, HBM, HBM), input_output_aliases={0: 2, 1: 3},
        compiler_params=pltpu.CompilerParams(has_side_effects=EFFECT),
    )(pltpu.with_memory_space_constraint(part, pltpu.HBM),
      pltpu.with_memory_space_constraint(lax.empty(land, part.dtype), pltpu.HBM))


def _partials_wait(send_sems, recv_sems, part_thru, land_thru, after):
    def body(part_ref, land_ref, send_sems, recv_sems, after_ref, part_out, land_out):
        for cp in _partial_copies(part_ref, land_ref, send_sems, recv_sems):
            cp.wait_send()
            cp.wait_recv()

    return pl.pallas_call(
        body, name="partials_wait",
        out_shape=(pltpu.HBM(part_thru.shape, part_thru.dtype), pltpu.HBM(land_thru.shape, land_thru.dtype)),
        in_specs=(HBM, HBM, SEM, SEM, pl.BlockSpec(memory_space=pl.ANY)), out_specs=(HBM, HBM),
        input_output_aliases={0: 0, 1: 1},
        compiler_params=pltpu.CompilerParams(has_side_effects=EFFECT),
    )(part_thru, land_thru, send_sems, recv_sems, after)


ADAM_LR = 0.001
ADAM_B1 = 0.9
ADAM_B2 = 0.999
ADAM_EPS = 1e-08
ADAM_WD = 0.01
ADAM_STEP = 10


PARAMS = ("norm_g", "w_in", "pool_w", "pool_b", "pool_scale", "conv_dw", "conv_b", "ln_g", "ln_b", "pw_w", "pw_b", "w_out",
          "final_g")


LATE_PARAMS = ("norm_g",)
MAIN_PARAMS = tuple(p for p in PARAMS if p not in LATE_PARAMS)


def _adamw_refs(params, n_sources, refs):
    n = len(params)
    ws, ms, vs = (dict(zip(params, refs[k * n:(k + 1) * n])) for k in range(3))
    outs = refs[3 * n + n_sources:]
    gs, ds, m2s, v2s = (dict(zip(params, outs[k * n:(k + 1) * n])) for k in range(4))

    def update(name, g, at=slice(None)):
        m = ADAM_B1 * ms[name][at] + (1.0 - ADAM_B1) * g
        v = ADAM_B2 * vs[name][at] + (1.0 - ADAM_B2) * (g * g)
        m_hat = m / (1.0 - ADAM_B1 ** ADAM_STEP)
        v_hat = v / (1.0 - ADAM_B2 ** ADAM_STEP)
        gs[name][at] = g
        ds[name][at] = -ADAM_LR * (m_hat / (jnp.sqrt(v_hat) + ADAM_EPS) + ADAM_WD * ws[name][at])
        m2s[name][at] = m
        v2s[name][at] = v

    return update, refs[3 * n:3 * n + n_sources], outs[4 * n:]


def _adamw_late_body(*refs):
    update, (gng_ref, gng_others_ref), _ = _adamw_refs(LATE_PARAMS, 2, refs)
    me = 4 * lax.axis_index("x") + 2 * lax.axis_index("y") + lax.axis_index("c")
    g_norm = jnp.zeros(gng_ref.shape, F32)
    for k in range(N_DEV):
        g_norm = g_norm + jnp.where(me == k, gng_ref[...], gng_others_ref[k])
    update("norm_g", jnp.sum(g_norm, axis=0, keepdims=True))


def _adamw_body(*refs):
    update, sources, (loss_ref,) = _adamw_refs(MAIN_PARAMS, 7, refs)
    gwin_ref, gwin_others_ref, gwout_ref, gpww_ref, vec_ref, gdw_ref, gpoolw_ref = sources
    ws = dict(zip(MAIN_PARAMS, refs))
    row = lambda r, lo, hi: vec_ref[r:r + 1, lo:hi]
    loss_ref[...] = row(R_LOSS, 0, 1)
    g_w_in = gwin_ref[...]
    for j in range(3):
        g_w_in = g_w_in + gwin_others_ref[j].astype(F32)
    update("w_in", g_w_in)
    update("pool_w", gpoolw_ref[...].reshape(ws["pool_w"].shape))
    for g in range(len(POOL_WINDOWS)):
        update("pool_b", row(R_POOL, g * POOL_GC, (g + 1) * POOL_GC), at=slice(g, g + 1))
    update("pool_scale", row(R_POOL, W_MIX, 2 * W_MIX))
    update("conv_dw", gdw_ref[...])
    update("conv_b", row(R_CONV, 0, W_MIX))
    update("ln_g", row(R_CONV, W_MIX, 2 * W_MIX))
    update("ln_b", row(R_LN, 0, W_MIX))
    update("pw_w", gpww_ref[...])
    update("pw_b", row(R_LN, W_MIX, 2 * W_MIX))
    update("w_out", gwout_ref[...])
    update("final_g", row(R_FINAL_G, 0, D_MODEL))


def _adamw(body, name, params, ws, ms, vs, grad_sources, extra_out_shapes=()):
    n = len(params)
    flat = lambda a: a.reshape(1, -1) if a.ndim == 1 else a
    args = [flat(d[p]) for d in (ws, ms, vs) for p in params]
    outs = pl.pallas_call(
        body,
        name=name,
        in_specs=[VMEM] * (3 * n + len(grad_sources)),
        out_specs=[VMEM] * (4 * n + len(extra_out_shapes)),
        out_shape=[jax.ShapeDtypeStruct(a.shape, F32) for a in args[:n]] * 4 + list(extra_out_shapes),
        compiler_params=pltpu.CompilerParams(vmem_limit_bytes=VMEM_LIMIT),
    )(*args, *grad_sources)
    groups = tuple({p: outs[k * n + j].reshape(ws[p].shape) for j, p in enumerate(params)} for k in range(4))
    return groups, outs[4 * n:]


def kernel(x, norm_g, w_in, pool_w, pool_b, pool_scale, conv_dw, conv_b, ln_g, ln_b, pw_w, pw_b, w_out, final_g, loss_target, m_norm_g, m_w_in, m_pool_w, m_pool_b, m_pool_scale, m_conv_dw, m_conv_b, m_ln_g, m_ln_b, m_pw_w, m_pw_b, m_w_out, m_final_g, v_norm_g, v_w_in, v_pool_w, v_pool_b, v_pool_scale, v_conv_dw, v_conv_b, v_ln_g, v_ln_b, v_pw_w, v_pw_b, v_w_out, v_final_g):
    proj, w_in_t, w_out_b, pw_w_b, conv_dw_f = _inproj(x, norm_g, w_in.T, w_out, pw_w, conv_dw)
    h1, z, ob, dpool, dh, loss8, gfg8 = _fwd(x, loss_target, proj, pool_w, pool_b, pool_scale, conv_dw_f,
                                             conv_b, ln_g, ln_b, pw_w_b, pw_b, w_out_b, final_g)
    dproj, gwout, gpww, gpoolw, gvec, gdw8 = _bwd1(dh, proj, h1, z, ob, dpool, w_out_b, pw_w_b, pool_w, pool_scale,
                                                   conv_dw_f, ln_g, ln_b)
    win_own, win_staged, g_w_out, g_pw_w, vec, g_conv_dw, poolw_sum = _bwd2a(x, dproj, norm_g, gwout, gpww, loss8, gfg8, gvec,
                                                                             gdw8, gpoolw)
    send_sems, recv_sems, staged_thru, land_thru, token = _rs_start(win_staged)
    grad_x, gng8 = _bwd2b(x, dh, dproj, norm_g, w_in_t, token)
    ng_send, ng_recv, gng8, gng_land = _partials_start(gng8)
    win_others = _rs_wait(send_sems, recv_sems, staged_thru, land_thru, gng8)

    weights = dict(norm_g=norm_g, w_in=w_in.T, pool_w=pool_w, pool_b=pool_b, pool_scale=pool_scale, conv_dw=conv_dw, conv_b=conv_b,
                   ln_g=ln_g, ln_b=ln_b, pw_w=pw_w, pw_b=pw_b, w_out=w_out, final_g=final_g)
    m_in = dict(norm_g=m_norm_g, w_in=m_w_in.T, pool_w=m_pool_w, pool_b=m_pool_b, pool_scale=m_pool_scale, conv_dw=m_conv_dw,
                conv_b=m_conv_b, ln_g=m_ln_g, ln_b=m_ln_b, pw_w=m_pw_w, pw_b=m_pw_b, w_out=m_w_out, final_g=m_final_g)
    v_in = dict(norm_g=v_norm_g, w_in=v_w_in.T, pool_w=v_pool_w, pool_b=v_pool_b, pool_scale=v_pool_scale, conv_dw=v_conv_dw,
                conv_b=v_conv_b, ln_g=v_ln_g, ln_b=v_ln_b, pw_w=v_pw_w, pw_b=v_pw_b, w_out=v_w_out, final_g=v_final_g)
    main, (loss,) = _adamw(functools.partial(_adamw_body), "adamw", MAIN_PARAMS, weights, m_in, v_in,
                           (win_own, win_others, g_w_out, g_pw_w, vec, g_conv_dw, poolw_sum), [jax.ShapeDtypeStruct((1, 1), F32)])
    gng8, gng_others = _partials_wait(ng_send, ng_recv, gng8, gng_land, main[0]["w_out"])
    late, _ = _adamw(functools.partial(_adamw_late_body), "adamw_norm_g", LATE_PARAMS, weights, m_in, v_in, (gng8, gng_others))
    groups = [[(late[k] if p in LATE_PARAMS else main[k])[p] for p in PARAMS] for k in range(4)]
    k_in = PARAMS.index("w_in")
    for group in groups:
        group[k_in] = group[k_in].T
    grads, delta, new_m, new_v = groups
    return (loss.reshape(()), grad_x, *grads, *delta, *new_m, *new_v)
```

```python
import functools

import jax
import jax.numpy as jnp
from jax import lax
from jax.experimental import pallas as pl
from jax.experimental.pallas import tpu as pltpu

F32 = jnp.float32
BF16 = jnp.bfloat16

D_MODEL = 1024
W_MIX = 512
D_IN = 5 * W_MIX
POOL_WINDOWS = (2, 4, 8, 16)
POOL_GC = 128
CONV_WIDTH = 31
RMS_EPS = 1e-6
LN_EPS = 1e-5
N_DEV = 8

LANES = 128
SUBLANES = 8
TILE_M = 512
TILE_IN = 1024
POOL_HALO = 16
CONV_HALO = 32
CONV_ROWS = 128
BWD_CONV_ROWS = 64
BWD_SUBTILES = 2
VMEM_LIMIT = 56 * 1024 * 1024

VMEM = pl.BlockSpec(memory_space=pltpu.VMEM)


def _silu(v):
    return v * jax.nn.sigmoid(v)


def _colsum8(v):
    m, n = v.shape
    return jnp.sum(v.reshape(m // SUBLANES, SUBLANES, n), axis=0)


def _row_index(shape, t0):
    return lax.broadcasted_iota(jnp.int32, shape, 0) + t0


def _conv_taps():
    taps = {s: [] for s in range(SUBLANES)}
    for k in range(CONV_WIDTH):
        e = k + CONV_HALO - (CONV_WIDTH - 1)
        taps[e % SUBLANES].append((e // SUBLANES, k))
    return taps


def _fwd_body(x_ref, tgt_ref, proj_ref, poolw_ref, poolb_ref, pools_ref, dw_ref, cb_ref, lg_ref, lb_ref,
              pww_ref, pwb_ref, wout_ref, fg_ref,
              h1_ref, z_ref, ob_ref, dpool_ref, dh_ref, loss_ref, gfg_ref,
              uext, hext, y_s):
    b = pl.program_id(0)
    i = pl.program_id(1)
    tm = TILE_M

    @pl.when(i == 0)
    def _():
        uext[0:POOL_HALO, :] = jnp.zeros((POOL_HALO, W_MIX), F32)
        hext[0:CONV_HALO, :] = jnp.zeros((CONV_HALO, W_MIX), F32)
        hext[CONV_HALO + tm:, :] = jnp.zeros((SUBLANES, W_MIX), F32)

    @pl.when((b == 0) & (i == 0))
    def _():
        loss_ref[...] = jnp.zeros_like(loss_ref)
        gfg_ref[...] = jnp.zeros_like(gfg_ref)

    uext[POOL_HALO:, :] = proj_ref[0, :, 0:W_MIX]
    t = _row_index((tm, POOL_GC), i * tm)
    for g, w in enumerate(POOL_WINDOWS):
        lanes = slice(g * POOL_GC, (g + 1) * POOL_GC)
        e = uext[:, lanes]
        s = e
        sh = 1
        while sh < w:
            s = s + pltpu.roll(s, sh, 0)
            sh *= 2
        inv_cnt = 1.0 / jnp.minimum(t + 1, w).astype(F32)
        d = (s[POOL_HALO:] * inv_cnt - e[POOL_HALO:]).astype(BF16)
        dpool_ref[0, :, lanes] = d
        z = jnp.dot(d, poolw_ref[g].astype(BF16), preferred_element_type=F32) + poolb_ref[:, lanes]
        z_ref[0, :, lanes] = z
        a_gate = proj_ref[0, :, W_MIX + g * POOL_GC:W_MIX + (g + 1) * POOL_GC]
        y_s[:, lanes] = (z * pools_ref[:, lanes] * _silu(a_gate)).astype(BF16)

    hext[CONV_HALO:CONV_HALO + tm, :] = proj_ref[0, :, 2 * W_MIX:3 * W_MIX] * jax.nn.sigmoid(proj_ref[0, :, 3 * W_MIX:4 * W_MIX])
    taps = _conv_taps()
    for c0 in range(0, tm, CONV_ROWS):
        for l0 in range(0, W_MIX, LANES):
            lanes = slice(l0, l0 + LANES)
            acc = jnp.zeros((CONV_ROWS, LANES), F32) + cb_ref[:, lanes]
            for s in range(SUBLANES):
                part = jnp.zeros((CONV_ROWS + SUBLANES, LANES), F32)
                for q, k in taps[s]:
                    r0 = c0 + SUBLANES * q
                    part = part + hext[r0:r0 + CONV_ROWS + SUBLANES, lanes] * dw_ref[k:k + 1, lanes]
                acc = acc + part[s:s + CONV_ROWS]
            h1_ref[0, c0:c0 + CONV_ROWS, lanes] = acc
    h1 = h1_ref[0]
    mu = jnp.mean(h1, axis=-1, keepdims=True)
    xc = h1 - mu
    rstd = lax.rsqrt(jnp.mean(xc * xc, axis=-1, keepdims=True) + LN_EPS)
    h3 = _silu(xc * rstd * lg_ref[...] + lb_ref[...])
    ob = jnp.dot(h3.astype(BF16), pww_ref[...], preferred_element_type=F32) + pwb_ref[...]
    ob_ref[0] = ob
    y_s[:, W_MIX:] = (ob * _silu(proj_ref[0, :, 4 * W_MIX:5 * W_MIX])).astype(BF16)

    h = x_ref[0] + jnp.dot(y_s[...], wout_ref[...], preferred_element_type=F32)
    r2 = lax.rsqrt(jnp.mean(h * h, axis=-1, keepdims=True) + RMS_EPS)
    hr = h * r2
    err = hr * fg_ref[...] - tgt_ref[0]
    loss_ref[...] += jnp.sum(err * err) * (0.5 / D_MODEL)
    dout = err * (1.0 / D_MODEL)
    gfg_ref[...] += _colsum8(dout * hr)
    gy = dout * fg_ref[...]
    dh_ref[0] = r2 * (gy - hr * jnp.mean(gy * hr, axis=-1, keepdims=True))

    uext[0:POOL_HALO, :] = uext[tm:tm + POOL_HALO, :]
    hext[0:CONV_HALO, :] = hext[tm:tm + CONV_HALO, :]


def _row(v):
    return v.reshape(1, -1)


def _fwd(x, tgt, proj, pool_w, pool_b, pool_scale, conv_dw_f, conv_b, ln_g, ln_b, pw_w_b, pw_b, w_out_b, final_g):
    bl, s, _ = x.shape
    tm = TILE_M
    tok = lambda n: pl.BlockSpec((1, tm, n), lambda b, i: (b, i, 0))
    acc = lambda n: pl.BlockSpec((SUBLANES, n), lambda b, i: (0, 0))
    act = lambda n: jax.ShapeDtypeStruct((bl, s, n), F32)
    return pl.pallas_call(
        functools.partial(_fwd_body),
        name="fwd",
        grid=(bl, s // tm),
        in_specs=[tok(D_MODEL), tok(D_MODEL), tok(D_IN)] + [VMEM] * 11,
        out_specs=[tok(W_MIX), tok(W_MIX), tok(W_MIX), tok(W_MIX), tok(D_MODEL), acc(LANES), acc(D_MODEL)],
        out_shape=[act(W_MIX), act(W_MIX), act(W_MIX), jax.ShapeDtypeStruct((bl, s, W_MIX), BF16), act(D_MODEL),
                   jax.ShapeDtypeStruct((SUBLANES, LANES), F32), jax.ShapeDtypeStruct((SUBLANES, D_MODEL), F32)],
        scratch_shapes=[pltpu.VMEM((POOL_HALO + tm, W_MIX), F32), pltpu.VMEM((CONV_HALO + tm + SUBLANES, W_MIX), F32),
                        pltpu.VMEM((tm, D_MODEL), BF16)],
        compiler_params=pltpu.CompilerParams(dimension_semantics=("arbitrary", "arbitrary"), vmem_limit_bytes=VMEM_LIMIT),
    )(x, tgt, proj, pool_w, pool_b.reshape(1, W_MIX), _row(pool_scale), conv_dw_f, _row(conv_b),
      _row(ln_g), _row(ln_b), pw_w_b, _row(pw_b), w_out_b, _row(final_g))


NT_DIMS = (((1,), (1,)), ((), ()))
TN_DIMS = (((0,), (0,)), ((), ()))
V_POOL_B, V_POOL_SCALE, V_CONV_B, V_LN_G, V_LN_B, V_PW_B = range(6)


def _bwd_taps():
    taps = {s: [] for s in range(SUBLANES)}
    for k in range(CONV_WIDTH):
        e = CONV_WIDTH - 1 - k
        taps[e % SUBLANES].append((e // SUBLANES, k))
    return taps


def _bwd1_body(n_tiles, dh_ref, proj_ref, h1_ref, z_ref, ob_ref, dpool_ref, wout_ref, pww_ref, poolw_ref, pools_ref, dw_ref,
               lg_ref, lb_ref,
               dproj_ref, gwout_ref, gpww_ref, gpoolw_ref, gvec_ref, gdw_ref,
               vext, gext, gsh, y_s):
    b = pl.program_id(0)
    i = pl.program_id(1)
    tm = TILE_M
    t0 = (n_tiles - 1 - i) * tm

    @pl.when(i == 0)
    def _():
        vext[tm:, :] = jnp.zeros((POOL_HALO, W_MIX), F32)
        gext[tm:, :] = jnp.zeros((CONV_HALO, W_MIX), F32)

    @pl.when((b == 0) & (i == 0))
    def _():
        gwout_ref[...] = jnp.zeros_like(gwout_ref)
        gpww_ref[...] = jnp.zeros_like(gpww_ref)
        gpoolw_ref[...] = jnp.zeros_like(gpoolw_ref)
        gvec_ref[...] = jnp.zeros_like(gvec_ref)
        gdw_ref[...] = jnp.zeros_like(gdw_ref)

    sub = tm // BWD_SUBTILES

    def dense_phase(r0):
        rs = slice(r0, r0 + sub)
        dhb = dh_ref[0, rs, :].astype(BF16)
        dy_a = lax.dot_general(dhb, wout_ref[0:W_MIX, :], NT_DIMS, preferred_element_type=F32)
        dy_b = lax.dot_general(dhb, wout_ref[W_MIX:, :], NT_DIMS, preferred_element_type=F32)

        a_gate = proj_ref[0, rs, W_MIX:2 * W_MIX]
        sig_a = jax.nn.sigmoid(a_gate)
        silu_a = a_gate * sig_a
        z = z_ref[0, rs, :]
        out_a = z * pools_ref[...]
        y_s[rs, 0:W_MIX] = (out_a * silu_a).astype(BF16)
        d_out_a = dy_a * silu_a
        d_ag = dy_a * out_a * (sig_a * (1.0 + a_gate * (1.0 - sig_a)))
        dproj_ref[0, rs, W_MIX:2 * W_MIX] = d_ag.astype(BF16)
        gvec_ref[V_POOL_SCALE] += _colsum8(d_out_a * z)
        dz = d_out_a * pools_ref[...]
        gvec_ref[V_POOL_B] += _colsum8(dz)
        t = _row_index((sub, POOL_GC), t0 + r0)
        n_ext = sub + POOL_HALO
        for g, w in enumerate(POOL_WINDOWS):
            lanes = slice(g * POOL_GC, (g + 1) * POOL_GC)
            dzg = dz[:, lanes].astype(BF16)
            gpoolw_ref[g] += lax.dot_general(dpool_ref[0, rs, lanes], dzg, TN_DIMS, preferred_element_type=F32)
            dd = lax.dot_general(dzg, poolw_ref[g].astype(BF16), NT_DIMS, preferred_element_type=F32)
            vext[rs, lanes] = dd * (1.0 / jnp.minimum(t + 1, w).astype(F32))
            s = vext[r0:r0 + n_ext, lanes]
            sh = 1
            while sh < w:
                s = s + pltpu.roll(s, n_ext - sh, 0)
                sh *= 2
            dproj_ref[0, rs, lanes] = (s[0:sub] - dd).astype(BF16)

        b_gate = proj_ref[0, rs, 4 * W_MIX:5 * W_MIX]
        sig_b = jax.nn.sigmoid(b_gate)
        silu_b = b_gate * sig_b
        ob = ob_ref[0, rs, :]
        y_s[rs, W_MIX:] = (ob * silu_b).astype(BF16)
        d_ob = dy_b * silu_b
        dproj_ref[0, rs, 4 * W_MIX:5 * W_MIX] = (dy_b * ob * (sig_b * (1.0 + b_gate * (1.0 - sig_b)))).astype(BF16)
        gvec_ref[V_PW_B] += _colsum8(d_ob)
        gwout_ref[...] += lax.dot_general(y_s[rs, :], dhb, TN_DIMS, preferred_element_type=F32)

        h1 = h1_ref[0, rs, :]
        mu = jnp.mean(h1, axis=-1, keepdims=True)
        xc = h1 - mu
        rstd = lax.rsqrt(jnp.mean(xc * xc, axis=-1, keepdims=True) + LN_EPS)
        xhat = xc * rstd
        h2 = xhat * lg_ref[...] + lb_ref[...]
        sig2 = jax.nn.sigmoid(h2)
        d_obb = d_ob.astype(BF16)
        gpww_ref[...] += lax.dot_general((h2 * sig2).astype(BF16), d_obb, TN_DIMS, preferred_element_type=F32)
        dh2 = lax.dot_general(d_obb, pww_ref[...], NT_DIMS, preferred_element_type=F32) * (sig2 * (1.0 + h2 * (1.0 - sig2)))
        gvec_ref[V_LN_G] += _colsum8(dh2 * xhat)
        gvec_ref[V_LN_B] += _colsum8(dh2)
        dxh = dh2 * lg_ref[...]
        dh1 = rstd * (dxh - jnp.mean(dxh, axis=-1, keepdims=True) - xhat * jnp.mean(dxh * xhat, axis=-1, keepdims=True))
        gvec_ref[V_CONV_B] += _colsum8(dh1)
        gext[rs, :] = dh1

    taps = _bwd_taps()
    cr = BWD_CONV_ROWS
    win = cr + CONV_HALO - SUBLANES

    def conv_phase(r0):
        for c0 in range(r0, r0 + sub, cr):
            for l0 in range(0, W_MIX, LANES):
                lanes = slice(l0, l0 + LANES)
                b_val = proj_ref[0, c0:c0 + cr, 2 * W_MIX + l0:2 * W_MIX + l0 + LANES]
                sg = jax.nn.sigmoid(proj_ref[0, c0:c0 + cr, 3 * W_MIX + l0:3 * W_MIX + l0 + LANES])
                h0c = b_val * sg
                acc = jnp.zeros((cr, LANES), F32)
                for s in range(SUBLANES):
                    gsh[...] = gext[c0 + s:c0 + s + win, lanes]
                    for q, k in taps[s]:
                        gq = gsh[SUBLANES * q:SUBLANES * q + cr, :]
                        acc = acc + gq * dw_ref[k:k + 1, lanes]
                        gdw_ref[k, :, lanes] += _colsum8(h0c * gq)
                d_bval = acc * sg
                dproj_ref[0, c0:c0 + cr, 2 * W_MIX + l0:2 * W_MIX + l0 + LANES] = d_bval.astype(BF16)
                dproj_ref[0, c0:c0 + cr, 3 * W_MIX + l0:3 * W_MIX + l0 + LANES] = (d_bval * b_val * (1.0 - sg)).astype(BF16)

    starts = [k * sub for k in reversed(range(BWD_SUBTILES))]
    for r0 in starts:
        dense_phase(r0)
    for r0 in starts:
        conv_phase(r0)
    vext[tm:, :] = vext[0:POOL_HALO, :]
    gext[tm:, :] = gext[0:CONV_HALO, :]


def _bwd1(dh, proj, h1, z, ob, dpool, w_out_b, pw_w_b, pool_w, pool_scale, conv_dw_f, ln_g, ln_b):
    bl, s, _ = dh.shape
    tm = TILE_M
    nt = s // tm
    tok = lambda n: pl.BlockSpec((1, tm, n), lambda b, i: (b, nt - 1 - i, 0))
    res = lambda shape: pl.BlockSpec(shape, lambda b, i: (0,) * len(shape))
    out_shapes = [(D_MODEL, D_MODEL), (W_MIX, W_MIX), (len(POOL_WINDOWS), POOL_GC, POOL_GC), (6, SUBLANES, W_MIX),
                  (CONV_WIDTH, SUBLANES, W_MIX)]
    return pl.pallas_call(
        functools.partial(_bwd1_body, nt),
        name="bwd1",
        grid=(bl, nt),
        in_specs=[tok(D_MODEL), tok(D_IN), tok(W_MIX), tok(W_MIX), tok(W_MIX), tok(W_MIX)] + [VMEM] * 7,
        out_specs=[tok(D_IN)] + [res(sh) for sh in out_shapes],
        out_shape=[jax.ShapeDtypeStruct((bl, s, D_IN), BF16)] + [jax.ShapeDtypeStruct(sh, F32) for sh in out_shapes],
        scratch_shapes=[pltpu.VMEM((tm + POOL_HALO, W_MIX), F32), pltpu.VMEM((tm + CONV_HALO, W_MIX), F32),
                        pltpu.VMEM((BWD_CONV_ROWS + CONV_HALO - SUBLANES, LANES), F32), pltpu.VMEM((tm, D_MODEL), BF16)],
        compiler_params=pltpu.CompilerParams(dimension_semantics=("arbitrary", "arbitrary"), vmem_limit_bytes=VMEM_LIMIT),
    )(dh, proj, h1, z, ob, dpool, w_out_b, pw_w_b, pool_w, _row(pool_scale), conv_dw_f, _row(ln_g), _row(ln_b))


MESH = pl.DeviceIdType.MESH
W_IN_BLK = D_IN // N_DEV
W_OUT_BLK = D_MODEL // N_DEV
PW_BLK = W_MIX // N_DEV
DW_BLK = W_MIX // N_DEV
N_PEER = N_DEV - 1


SLAB = 2 * W_IN_BLK
N_SLABS = 4
F_SIB, F_X, F_Y, F_ON, F_PASS_X, F_PASS_Y, F_PASS_D = range(7)
W_IN_ONLY, OTHERS = (0,), (1, 2, 3)


def _inproj_body(n_tiles, x_hbm, ng_ref, wint_ref, wout_ref, pww_ref, dw_ref,
                 proj_hbm, wt_hbm, woutf_hbm, pwwf_hbm, dwf_ref,
                 hn_s, wt_s, wout_s, pww_s, dw_blk, stage, xbuf, out_sems, x_sems, w_sems, send_sems, recv_sems):
    s = pl.program_id(0)
    i = pl.program_id(1)
    tm = TILE_IN
    x, y, c = lax.axis_index("x"), lax.axis_index("y"), lax.axis_index("c")
    sibling = (x, y, 1 - c)
    xn, yn, dg = (1 - x, y), (x, 1 - y), (1 - x, 1 - y)
    pick = lambda a, b: tuple(jnp.where(c == 1, p, q) for p, q in zip(a, b))
    on_from, on_to = pick(xn, yn), pick(yn, xn)

    def blocks(px, py, pc):
        k = 4 * px + 2 * py + pc
        rows = lambda ref, n: ref.at[pl.ds(pl.multiple_of(k * n, n), n), :]
        return (rows(wt_s, W_IN_BLK), rows(wout_s, W_OUT_BLK), rows(pww_s, PW_BLK), dw_blk.at[k])

    def copies(arrays, flow, block, to):
        refs = blocks(*block)
        return [pltpu.make_async_remote_copy(src_ref=refs[a], dst_ref=refs[a], send_sem=send_sems.at[a, flow],
                                             recv_sem=recv_sems.at[a, flow], device_id=to, device_id_type=MESH)
                for a in arrays]

    def start(arrays):
        for flow, to in ((F_SIB, sibling), (F_X, (*xn, c)), (F_Y, (*yn, c))):
            for cp in copies(arrays, flow, (x, y, c), to):
                cp.start()

    def direct(arrays):
        for flow, chip in ((F_X, xn), (F_Y, yn)):
            for cp in copies(arrays, flow, (*chip, c), sibling):
                cp.wait_recv()
        for cp in copies(arrays, F_ON, (*on_from, c), (*on_to, c)):
            cp.start()
        for flow, chip in ((F_PASS_X, xn), (F_PASS_Y, yn)):
            for cp in copies(arrays, flow, (*chip, c), sibling):
                cp.start()

    def diagonal(arrays):
        for cp in copies(arrays, F_ON, (*dg, c), sibling):
            cp.wait_recv()
        for cp in copies(arrays, F_PASS_D, (*dg, c), sibling):
            cp.start()

    def from_sibling(arrays, flow, chip):
        for cp in copies(arrays, flow, (*chip, 1 - c), sibling):
            cp.wait_recv()

    def finish(arrays):
        mine = (x, y, c)
        for flow, block in ((F_SIB, mine), (F_X, mine), (F_Y, mine), (F_ON, (*on_from, c)), (F_PASS_X, (*xn, c)),
                            (F_PASS_Y, (*yn, c)), (F_PASS_D, (*dg, c))):
            for cp in copies(arrays, flow, block, sibling):
                cp.wait_send()

    step = s * n_tiles + i
    par = step % 2
    first = step == 0

    @pl.when(first)
    def _():
        mine = blocks(x, y, c)
        mine[0][...] = wint_ref[...].astype(BF16)
        mine[1][...] = wout_ref[...].astype(BF16)
        mine[2][...] = pww_ref[...].astype(BF16)
        mine[3][...] = dw_ref[...]
        start(W_IN_ONLY)
        start(OTHERS)

    rows_i = pl.ds(pl.multiple_of(i * tm, tm), tm)
    tiles_per_seq = proj_hbm.shape[1] // tm

    def x_copy(k):
        src = x_hbm.at[k // tiles_per_seq, pl.ds(pl.multiple_of((k % tiles_per_seq) * tm, tm), tm), :]
        return pltpu.make_async_copy(src, xbuf.at[k % 2], x_sems.at[k % 2])

    @pl.when(first)
    def _():
        x_copy(0).start()

    @pl.when(s == 0)
    def _():
        x_copy(i).wait()

        @pl.when(i + 1 < n_tiles)
        def _():
            x_copy(i + 1).start()

        xt = xbuf[i % 2]
        r1 = lax.rsqrt(jnp.mean(xt * xt, axis=-1, keepdims=True) + RMS_EPS)
        hn_s[rows_i, :] = (xt * r1 * ng_ref[...]).astype(BF16)

    @pl.when(first)
    def _():
        from_sibling(W_IN_ONLY, F_SIB, (x, y))

    @pl.when((s == 1) & (i == 0))
    def _():
        direct(W_IN_ONLY)
        from_sibling(W_IN_ONLY, F_PASS_Y, yn)

    @pl.when((s == 2) & (i == 0))
    def _():
        from_sibling(W_IN_ONLY, F_PASS_X, xn)

    weights_out = [pltpu.make_async_copy(src, dst, w_sems.at[k])
                   for k, (src, dst) in enumerate(((wt_s, wt_hbm), (wout_s, woutf_hbm), (pww_s, pwwf_hbm)))]

    @pl.when((s == 3) & (i == 0))
    def _():
        diagonal(W_IN_ONLY)
        direct(OTHERS)
        from_sibling(W_IN_ONLY, F_PASS_D, dg)
        weights_out[0].start()

    def out_copy(parity, col0):
        dst = proj_hbm.at[i // tiles_per_seq, pl.ds(pl.multiple_of((i % tiles_per_seq) * tm, tm), tm),
                          pl.ds(pl.multiple_of(col0, LANES), SLAB)]
        return pltpu.make_async_copy(stage.at[parity], dst, out_sems.at[parity])

    @pl.when(step >= 2)
    def _():
        out_copy(par, 0).wait()

    chip = jnp.where(s < 2, 2 * x, 2 * (1 - x)) + jnp.where((s == 0) | (s == 2), y, 1 - y)
    slab = wt_s[pl.ds(pl.multiple_of(chip * SLAB, SLAB), SLAB), :]
    stage[par] = lax.dot_general(hn_s[rows_i, :], slab, NT_DIMS, preferred_element_type=F32)
    out_copy(par, chip * SLAB).start()

    @pl.when(step == N_SLABS * n_tiles - 1)
    def _():
        out_copy(par, 0).wait()
        out_copy(1 - par, 0).wait()
        from_sibling(OTHERS, F_SIB, (x, y))
        from_sibling(OTHERS, F_PASS_X, xn)
        from_sibling(OTHERS, F_PASS_Y, yn)
        diagonal(OTHERS)
        from_sibling(OTHERS, F_PASS_D, dg)
        for cp in weights_out[1:]:
            cp.start()
        finish(W_IN_ONLY)
        finish(OTHERS)
        for cp in weights_out:
            cp.wait()
        for k in range(N_DEV):
            dwf_ref[:, k * DW_BLK:(k + 1) * DW_BLK] = dw_blk[k]


def _inproj(x, norm_g, w_in_t, w_out, pw_w, conv_dw):
    bl, s, _ = x.shape
    tm = TILE_IN
    nt = s // tm
    n_tiles = bl * nt
    dma = pltpu.SemaphoreType.DMA
    return pl.pallas_call(
        functools.partial(_inproj_body, n_tiles),
        name="inproj",
        grid=(N_SLABS, n_tiles),
        in_specs=[pl.BlockSpec(memory_space=pl.ANY)] + [VMEM] * 5,
        out_specs=[pl.BlockSpec(memory_space=pl.ANY)] * 4 + [VMEM],
        out_shape=[jax.ShapeDtypeStruct((bl, s, D_IN), F32), jax.ShapeDtypeStruct((D_IN, D_MODEL), BF16),
                   jax.ShapeDtypeStruct((D_MODEL, D_MODEL), BF16), jax.ShapeDtypeStruct((W_MIX, W_MIX), BF16),
                   jax.ShapeDtypeStruct((CONV_WIDTH, W_MIX), F32)],
        scratch_shapes=[pltpu.VMEM((n_tiles * tm, D_MODEL), BF16), pltpu.VMEM((D_IN, D_MODEL), BF16),
                        pltpu.VMEM((D_MODEL, D_MODEL), BF16), pltpu.VMEM((W_MIX, W_MIX), BF16),
                        pltpu.VMEM((N_DEV, CONV_WIDTH, DW_BLK), F32), pltpu.VMEM((2, tm, SLAB), F32),
                        pltpu.VMEM((2, tm, D_MODEL), F32), dma((2,)), dma((2,)), dma((3,)), dma((4, 7)), dma((4, 7))],
        compiler_params=pltpu.CompilerParams(dimension_semantics=("arbitrary", "arbitrary"), vmem_limit_bytes=VMEM_LIMIT),
    )(x, _row(norm_g), w_in_t, w_out, pw_w, conv_dw)


B_PLAIN, B_ON, B_MERGED = range(3)


def _routes():
    x, y, c = lax.axis_index("x"), lax.axis_index("y"), lax.axis_index("c")
    xn, yn = (1 - x, y), (x, 1 - y)
    pick = lambda a, b: tuple(jnp.where(c == 1, p, q) for p, q in zip(a, b))
    return pick(xn, yn), pick(yn, xn), (1 - x, 1 - y), (x, y), c


def _chip_index(chip):
    return 2 * chip[0] + chip[1]


def _copy(src, dst, send_sems, recv_sems, a, k, to):
    return pltpu.make_async_remote_copy(src_ref=src, dst_ref=dst, send_sem=send_sems.at[a, k], recv_sem=recv_sems.at[a, k],
                                        device_id=to, device_id_type=MESH)


def _rsb_send(k, srcs, dsts, send_b, recv_b):
    first, _, _, _, c = _routes()
    for a, (s, d) in enumerate(zip(srcs, dsts)):
        _copy(s, d, send_b, recv_b, a, k, (*first, c)).start()


def _rsb_merge(second_sums, merged, rb_on, rb_merged, send_b, recv_b):
    _, second, _, _, c = _routes()
    for a, (own, m, got, dst) in enumerate(zip(second_sums, merged, rb_on, rb_merged)):
        _copy(got, got, send_b, recv_b, a, B_ON, (*second, c)).wait_recv()
        m[...] = (own[...] + got[...].astype(F32)).astype(BF16)
        _copy(m, dst, send_b, recv_b, a, B_MERGED, (*second, c)).start()


def _rsb_total(own_sums, plain, on, merged, rb_plain, rb_merged, send_b, recv_b):
    _, second, _, _, c = _routes()
    out = []
    for a, (own, p, g) in enumerate(zip(own_sums, rb_plain, rb_merged)):
        _copy(p, p, send_b, recv_b, a, B_PLAIN, (*second, c)).wait_recv()
        _copy(g, g, send_b, recv_b, a, B_MERGED, (*second, c)).wait_recv()
        out.append(own[...] + p[...].astype(F32) + g[...].astype(F32))
    for k, srcs in ((B_PLAIN, plain), (B_ON, on), (B_MERGED, merged)):
        for a, s in enumerate(srcs):
            _copy(s, s, send_b, recv_b, a, k, (*second, c)).wait_send()
    return out


R_LOSS, R_FINAL_G, R_POOL, R_CONV, R_LN = range(5)
VEC_ROWS = 16
DW_ROWS = 32
A_PART, A_X, A_Y, A_ON, A_HALF = range(5)
HALF_DTYPES = (F32, BF16, BF16)
STEP_SUMS, STEP_MERGE, STEP_HALF = 1, 3, 5


def _bwd2a_body(n_steps, x_ref, dproj_ref, ng_ref, gwout_hbm, gpww_hbm, loss_ref, gfg_ref, gvec_ref, gdw_ref, gpoolw_ref,
                win_own_ref, win_staged_ref, owout_ref, opww_ref, ovec_ref, odw_ref, opoolw_ref,
                acc, st_win, a_win, own_wout, own_pww, a_wout, a_pww, st_wout, st_pww, rb_wout, rb_pww, part_vec, part_dw,
                part_pw, sib_vec, sib_dw, sib_pw, res_vec, res_dw, res_pw, half_vec, half_dw, half_pw,
                own_sems, send_a, recv_a, send_b, recv_b, send_r, recv_r, send_w, recv_w):
    step = pl.program_id(0) * pl.num_programs(1) + pl.program_id(1)
    x, y, c = lax.axis_index("x"), lax.axis_index("y"), lax.axis_index("c")
    sibling = (x, y, 1 - c)
    xn, yn = (1 - x, y), (x, 1 - y)
    first, second, diag, own, _ = _routes()
    rows = lambda ref, k, n: ref.at[pl.ds(pl.multiple_of(k * n, n), n), :]

    blocks = lambda k: (rows(gwout_hbm, k, W_OUT_BLK), rows(gpww_hbm, k, PW_BLK))
    mine = lambda chip: (own_wout.at[chip], own_pww.at[chip])
    mine_copies = lambda chip: [pltpu.make_async_copy(s, d, own_sems.at[a, chip])
                                for a, (s, d) in enumerate(zip(blocks(2 * chip + c), mine(chip)))]
    landed = lambda chip: (a_wout.at[chip], a_pww.at[chip])
    staged = lambda k: (st_wout.at[k], st_pww.at[k])
    arrived = lambda k: (rb_wout.at[k], rb_pww.at[k])

    parts, sibs, halves = (part_vec, part_dw, part_pw), (sib_vec, sib_dw, sib_pw), (half_vec, half_dw, half_pw)
    results = (res_vec, res_dw, res_pw)
    half_rows = lambda ref: pl.ds(pl.multiple_of(c * (ref.shape[0] // 2), SUBLANES), ref.shape[0] // 2)
    slot = lambda chip: tuple(h.at[_chip_index(chip)] for h in halves)

    @pl.when(step == 0)
    def _():
        acc[...] = jnp.zeros_like(acc)
        sum8 = lambda v: jnp.sum(v, axis=0, keepdims=True)
        part_vec[...] = jnp.zeros((VEC_ROWS, D_MODEL), F32)
        part_vec[R_LOSS:R_LOSS + 1, :] = jnp.broadcast_to(loss_ref[0:1, 0:1], (1, D_MODEL))
        part_vec[R_FINAL_G:R_FINAL_G + 1, :] = sum8(gfg_ref[...])
        for row, (lo, hi) in ((R_POOL, (V_POOL_B, V_POOL_SCALE)), (R_CONV, (V_CONV_B, V_LN_G)), (R_LN, (V_LN_B, V_PW_B))):
            part_vec[row:row + 1, 0:W_MIX] = sum8(gvec_ref[lo])
            part_vec[row:row + 1, W_MIX:] = sum8(gvec_ref[hi])
        part_dw[0:CONV_WIDTH, :] = jnp.sum(gdw_ref[...], axis=1)
        part_dw[CONV_WIDTH:, :] = jnp.zeros((DW_ROWS - CONV_WIDTH, W_MIX), F32)
        part_pw[...] = gpoolw_ref[...].reshape(len(POOL_WINDOWS) * POOL_GC, POOL_GC)
        for a, (p, s) in enumerate(zip(parts, sibs)):
            _copy(p, s, send_r, recv_r, a, A_PART, sibling).start()
        for chip in range(4):
            for a, (s, d) in enumerate(zip(blocks(2 * chip + (1 - c)), landed(chip))):
                _copy(s, d, send_a, recv_a, a, chip, sibling).start()
            for cp in mine_copies(chip):
                cp.start()

    x_t = x_ref[0]
    r1 = lax.rsqrt(jnp.mean(x_t * x_t, axis=-1, keepdims=True) + RMS_EPS)
    hn = (x_t * r1 * ng_ref[...]).astype(BF16)
    acc[...] += lax.dot_general(dproj_ref[0], hn, TN_DIMS, preferred_element_type=F32)

    @pl.when(step == STEP_SUMS)
    def _():
        for a, (p, s, h) in enumerate(zip(parts, sibs, slot(own))):
            _copy(p, s, send_r, recv_r, a, A_PART, sibling).wait_recv()
            h[...] = (p[half_rows(p), :] + s[half_rows(s), :]).astype(h.dtype)
            for k, to in ((A_X, xn), (A_Y, yn)):
                _copy(h, h, send_r, recv_r, a, k, (*to, c)).start()
        for chip in range(4):
            for cp in mine_copies(chip):
                cp.wait()
            for a, (m, l) in enumerate(zip(mine(chip), landed(chip))):
                _copy(l, l, send_a, recv_a, a, chip, sibling).wait_recv()
                l[...] = m[...] + l[...]
        for k, chip in ((B_ON, diag), (B_PLAIN, first)):
            for s, l in zip(staged(k), landed(_chip_index(chip))):
                s[...] = l[...].astype(BF16)
        _rsb_send(B_ON, staged(B_ON), arrived(B_ON), send_b, recv_b)

    @pl.when(step == STEP_SUMS + 1)
    def _():
        _rsb_send(B_PLAIN, staged(B_PLAIN), arrived(B_PLAIN), send_b, recv_b)

    @pl.when(step == STEP_MERGE)
    def _():
        _rsb_merge(landed(_chip_index(second)), staged(B_MERGED), arrived(B_ON), arrived(B_MERGED), send_b, recv_b)
        for a in range(3):
            for k, chip in ((A_X, xn), (A_Y, yn)):
                h = slot(chip)[a]
                _copy(h, h, send_r, recv_r, a, k, sibling).wait_recv()
            h = slot(first)[a]
            _copy(h, h, send_r, recv_r, a, A_ON, (*second, c)).start()

    @pl.when(step == STEP_HALF)
    def _():
        for a, (r, hs) in enumerate(zip(results, halves)):
            h = slot(diag)[a]
            _copy(h, h, send_r, recv_r, a, A_ON, sibling).wait_recv()
            mine = r.at[half_rows(r), :]
            chip_sums = [hs[k].astype(F32) for k in range(4)]
            mine[...] = (chip_sums[0] + chip_sums[1]) + (chip_sums[2] + chip_sums[3])
            _copy(mine, mine, send_r, recv_r, a, A_HALF, sibling).start()

    @pl.when(step == n_steps - 1)
    def _():
        block = lambda k: rows(acc, k, W_IN_BLK)
        w_copy = lambda chip: _copy(st_win.at[chip], a_win.at[chip], send_w, recv_w, 0, chip, sibling)
        for chip in range(4):
            st_win[chip] = block(2 * chip + (1 - c))[...].astype(BF16)
            w_copy(chip).start()

        owout_ref[...], opww_ref[...] = _rsb_total(landed(_chip_index(own)), staged(B_PLAIN), staged(B_ON), staged(B_MERGED),
                                                   arrived(B_PLAIN), arrived(B_MERGED), send_b, recv_b)
        for a, (r, p, s) in enumerate(zip(results, parts, sibs)):
            theirs = r.at[pl.ds(pl.multiple_of((1 - c) * (r.shape[0] // 2), SUBLANES), r.shape[0] // 2), :]
            _copy(theirs, theirs, send_r, recv_r, a, A_HALF, sibling).wait_recv()
            _copy(p, s, send_r, recv_r, a, A_PART, sibling).wait_send()
            for k, h in ((A_X, slot(own)[a]), (A_Y, slot(own)[a]), (A_ON, slot(first)[a])):
                _copy(h, h, send_r, recv_r, a, k, sibling).wait_send()
            mine = r.at[half_rows(r), :]
            _copy(mine, mine, send_r, recv_r, a, A_HALF, sibling).wait_send()
        for chip in range(4):
            for a, (s, d) in enumerate(zip(blocks(2 * chip + (1 - c)), landed(chip))):
                _copy(s, d, send_a, recv_a, a, chip, sibling).wait_send()
        ovec_ref[...] = res_vec[...]
        opoolw_ref[...] = res_pw[...]
        for k in range(N_DEV):
            @pl.when(4 * x + 2 * y + c == k)
            def _():
                odw_ref[...] = res_dw[0:CONV_WIDTH, k * DW_BLK:(k + 1) * DW_BLK]

        for chip in range(4):
            w_copy(chip).wait()
        chip_sum = lambda chip: block(2 * _chip_index(chip) + c)[...] + a_win[_chip_index(chip)].astype(F32)
        win_own_ref[...] = chip_sum(own)
        for j, chip in enumerate((xn, yn, diag)):
            win_staged_ref[j] = chip_sum(chip).astype(BF16)


def _bwd2a(x, dproj, norm_g, gwout, gpww, loss8, gfg8, gvec, gdw8, gpoolw):
    bl, s, _ = x.shape
    tm = TILE_M
    nt = s // tm
    tok = lambda n: pl.BlockSpec((1, tm, n), lambda b, i: (b, i, 0))
    n_pw = len(POOL_WINDOWS) * POOL_GC
    win = (W_IN_BLK, D_MODEL)
    blk = ((W_OUT_BLK, D_MODEL), (PW_BLK, W_MIX))
    rep = ((VEC_ROWS, D_MODEL), (DW_ROWS, W_MIX), (n_pw, POOL_GC))
    half = lambda sh: (sh[0] // 2,) + sh[1:]
    dma = pltpu.SemaphoreType.DMA
    assert bl * nt > STEP_HALF + 1, "the exchanges' phases need their grid steps"
    return pl.pallas_call(
        functools.partial(_bwd2a_body, bl * nt),
        name="bwd2a",
        grid=(bl, nt),
        in_specs=[tok(D_MODEL), tok(D_IN), VMEM] + [pl.BlockSpec(memory_space=pl.ANY)] * 2 + [VMEM] * 5,
        out_specs=[VMEM] * 7,
        out_shape=[jax.ShapeDtypeStruct(win, F32), jax.ShapeDtypeStruct((3,) + win, BF16)]
        + [jax.ShapeDtypeStruct(sh, F32) for sh in blk + (rep[0], (CONV_WIDTH, DW_BLK), rep[2])],
        scratch_shapes=[pltpu.VMEM((D_IN, D_MODEL), F32), pltpu.VMEM((4,) + win, BF16), pltpu.VMEM((4,) + win, BF16)]
        + [pltpu.VMEM((4,) + sh, F32) for sh in blk] * 2
        + [pltpu.VMEM((3,) + sh, BF16) for sh in blk] * 2
        + [pltpu.VMEM(sh, F32) for sh in rep] * 3
        + [pltpu.VMEM((4,) + half(sh), dt) for sh, dt in zip(rep, HALF_DTYPES)]
        + [dma((2, 4)), dma((2, 4)), dma((2, 4)), dma((2, 3)), dma((2, 3)), dma((3, 5)), dma((3, 5)), dma((1, 4)), dma((1, 4))],
        compiler_params=pltpu.CompilerParams(dimension_semantics=("arbitrary", "arbitrary"), vmem_limit_bytes=VMEM_LIMIT),
    )(x, dproj, _row(norm_g), gwout, gpww, loss8, gfg8, gvec, gdw8, gpoolw)


HBM = pl.BlockSpec(memory_space=pltpu.HBM)
SEM = pl.BlockSpec(memory_space=pltpu.SEMAPHORE)
EFFECT = pltpu.SideEffectType.DATAFLOW_SIDE_EFFECTING


def _owner_copies(src_ref, land_ref, send_sems, recv_sems):
    x, y, c = lax.axis_index("x"), lax.axis_index("y"), lax.axis_index("c")
    owners = ((1 - x, y, c), (x, 1 - y, c), (1 - x, 1 - y, c))
    return [pltpu.make_async_remote_copy(src_ref=src_ref.at[j], dst_ref=land_ref.at[j], send_sem=send_sems.at[j],
                                         recv_sem=recv_sems.at[j], device_id=owners[j], device_id_type=MESH)
            for j in range(3)]


def _rs_start(staged):
    def body(src_ref, land_ref, send_sems, recv_sems, src_thru, land_thru, token):
        for cp in _owner_copies(src_ref, land_ref, send_sems, recv_sems):
            cp.start()
        token[...] = jnp.zeros_like(token)

    dma = pltpu.SemaphoreType.DMA
    return pl.pallas_call(
        body, name="rs_start",
        out_shape=(dma((3,)), dma((3,)), pltpu.HBM(staged.shape, staged.dtype), pltpu.HBM(staged.shape, staged.dtype),
                   jax.ShapeDtypeStruct((SUBLANES, LANES), F32)),
        in_specs=(HBM, HBM), out_specs=(SEM, SEM, HBM, HBM, VMEM), input_output_aliases={0: 2, 1: 3},
        compiler_params=pltpu.CompilerParams(has_side_effects=EFFECT),
    )(pltpu.with_memory_space_constraint(staged, pltpu.HBM),
      pltpu.with_memory_space_constraint(lax.empty(staged.shape, staged.dtype), pltpu.HBM))


def _rs_wait(send_sems, recv_sems, src_thru, land_thru, after):
    def body(src_ref, land_ref, send_sems, recv_sems, after_ref, src_dead, got_ref):
        for cp in _owner_copies(src_ref, land_ref, send_sems, recv_sems):
            cp.wait_send()
            cp.wait_recv()

    return pl.pallas_call(
        body, name="rs_wait",
        out_shape=(pltpu.HBM(src_thru.shape, src_thru.dtype), pltpu.HBM(land_thru.shape, land_thru.dtype)),
        in_specs=(HBM, HBM, SEM, SEM, pl.BlockSpec(memory_space=pl.ANY)), out_specs=(HBM, HBM),
        input_output_aliases={0: 0, 1: 1},
        compiler_params=pltpu.CompilerParams(has_side_effects=EFFECT),
    )(src_thru, land_thru, send_sems, recv_sems, after)[1]


def _bwd2b_body(x_ref, dh_ref, dproj_ref, ng_ref, wint_ref, token_ref, gx_ref, gng_ref):
    @pl.when((pl.program_id(0) == 0) & (pl.program_id(1) == 0))
    def _():
        gng_ref[...] = jnp.zeros_like(gng_ref) + token_ref[0:1, 0:1]

    x_t = x_ref[0]
    r1 = lax.rsqrt(jnp.mean(x_t * x_t, axis=-1, keepdims=True) + RMS_EPS)
    xr = x_t * r1
    dhn = jnp.dot(dproj_ref[0], wint_ref[...], preferred_element_type=F32)
    gng_ref[...] += _colsum8(dhn * xr)
    gy = dhn * ng_ref[...]
    gx_ref[0] = dh_ref[0] + r1 * (gy - xr * jnp.mean(gy * xr, axis=-1, keepdims=True))


def _bwd2b(x, dh, dproj, norm_g, w_in_t, token):
    bl, s, _ = x.shape
    tm = TILE_M
    tok = lambda n: pl.BlockSpec((1, tm, n), lambda b, i: (b, i, 0))
    return pl.pallas_call(
        functools.partial(_bwd2b_body),
        name="bwd2b",
        grid=(bl, s // tm),
        in_specs=[tok(D_MODEL), tok(D_MODEL), tok(D_IN), VMEM, VMEM, VMEM],
        out_specs=[tok(D_MODEL), pl.BlockSpec((SUBLANES, D_MODEL), lambda b, i: (0, 0))],
        out_shape=[jax.ShapeDtypeStruct((bl, s, D_MODEL), F32), jax.ShapeDtypeStruct((SUBLANES, D_MODEL), F32)],
        compiler_params=pltpu.CompilerParams(dimension_semantics=("arbitrary", "arbitrary"), vmem_limit_bytes=VMEM_LIMIT),
    )(x, dh, dproj, _row(norm_g), w_in_t, token)


def _partial_copies(part_ref, land_ref, send_sems, recv_sems):
    x, y, c = lax.axis_index("x"), lax.axis_index("y"), lax.axis_index("c")
    other_chips = [(1 - x, y), (x, 1 - y), (1 - x, 1 - y)]
    peers = [(x, y, 1 - c)] + [(*chip, c) for chip in other_chips] + [(*chip, 1 - c) for chip in other_chips]
    return [pltpu.make_async_remote_copy(src_ref=part_ref, dst_ref=land_ref.at[4 * x + 2 * y + c], send_sem=send_sems.at[r],
                                         recv_sem=recv_sems.at[r], device_id=peers[r], device_id_type=MESH)
            for r in range(N_PEER)]


def _partials_start(part):
    def body(part_ref, land_ref, send_sems, recv_sems, part_thru, land_thru):
        for cp in _partial_copies(part_ref, land_ref, send_sems, recv_sems):
            cp.start()

    dma = pltpu.SemaphoreType.DMA
    land = (N_DEV,) + part.shape
    return pl.pallas_call(
        body, name="partials_start",
        out_shape=(dma((N_PEER,)), dma((N_PEER,)), pltpu.HBM(part.shape, part.dtype), pltpu.HBM(land, part.dtype)),
        in_specs=(HBM, HBM), out_specs=(SEM, SEM, HBM, HBM), input_output_aliases={0: 2, 1: 3},
        compiler_params=pltpu.CompilerParams(has_side_effects=EFFECT),
    )(pltpu.with_memory_space_constraint(part, pltpu.HBM),
      pltpu.with_memory_space_constraint(lax.empty(land, part.dtype), pltpu.HBM))


def _partials_wait(send_sems, recv_sems, part_thru, land_thru, after):
    def body(part_ref, land_ref, send_sems, recv_sems, after_ref, part_out, land_out):
        for cp in _partial_copies(part_ref, land_ref, send_sems, recv_sems):
            cp.wait_send()
            cp.wait_recv()

    return pl.pallas_call(
        body, name="partials_wait",
        out_shape=(pltpu.HBM(part_thru.shape, part_thru.dtype), pltpu.HBM(land_thru.shape, land_thru.dtype)),
        in_specs=(HBM, HBM, SEM, SEM, pl.BlockSpec(memory_space=pl.ANY)), out_specs=(HBM, HBM),
        input_output_aliases={0: 0, 1: 1},
        compiler_params=pltpu.CompilerParams(has_side_effects=EFFECT),
    )(part_thru, land_thru, send_sems, recv_sems, after)


ADAM_LR = 0.001
ADAM_B1 = 0.9
ADAM_B2 = 0.999
ADAM_EPS = 1e-08
ADAM_WD = 0.01
ADAM_STEP = 10


PARAMS = ("norm_g", "w_in", "pool_w", "pool_b", "pool_scale", "conv_dw", "conv_b", "ln_g", "ln_b", "pw_w", "pw_b", "w_out",
          "final_g")


LATE_PARAMS = ("norm_g",)
MAIN_PARAMS = tuple(p for p in PARAMS if p not in LATE_PARAMS)


def _adamw_refs(params, n_sources, refs):
    n = len(params)
    ws, ms, vs = (dict(zip(params, refs[k * n:(k + 1) * n])) for k in range(3))
    outs = refs[3 * n + n_sources:]
    gs, ds, m2s, v2s = (dict(zip(params, outs[k * n:(k + 1) * n])) for k in range(4))

    def update(name, g, at=slice(None)):
        m = ADAM_B1 * ms[name][at] + (1.0 - ADAM_B1) * g
        v = ADAM_B2 * vs[name][at] + (1.0 - ADAM_B2) * (g * g)
        m_hat = m / (1.0 - ADAM_B1 ** ADAM_STEP)
        v_hat = v / (1.0 - ADAM_B2 ** ADAM_STEP)
        gs[name][at] = g
        ds[name][at] = -ADAM_LR * (m_hat / (jnp.sqrt(v_hat) + ADAM_EPS) + ADAM_WD * ws[name][at])
        m2s[name][at] = m
        v2s[name][at] = v

    return update, refs[3 * n:3 * n + n_sources], outs[4 * n:]


def _adamw_late_body(*refs):
    update, (gng_ref, gng_others_ref), _ = _adamw_refs(LATE_PARAMS, 2, refs)
    me = 4 * lax.axis_index("x") + 2 * lax.axis_index("y") + lax.axis_index("c")
    g_norm = jnp.zeros(gng_ref.shape, F32)
    for k in range(N_DEV):
        g_norm = g_norm + jnp.where(me == k, gng_ref[...], gng_others_ref[k])
    update("norm_g", jnp.sum(g_norm, axis=0, keepdims=True))


def _adamw_body(*refs):
    update, sources, (loss_ref,) = _adamw_refs(MAIN_PARAMS, 7, refs)
    gwin_ref, gwin_others_ref, gwout_ref, gpww_ref, vec_ref, gdw_ref, gpoolw_ref = sources
    ws = dict(zip(MAIN_PARAMS, refs))
    row = lambda r, lo, hi: vec_ref[r:r + 1, lo:hi]
    loss_ref[...] = row(R_LOSS, 0, 1)
    g_w_in = gwin_ref[...]
    for j in range(3):
        g_w_in = g_w_in + gwin_others_ref[j].astype(F32)
    update("w_in", g_w_in)
    update("pool_w", gpoolw_ref[...].reshape(ws["pool_w"].shape))
    for g in range(len(POOL_WINDOWS)):
        update("pool_b", row(R_POOL, g * POOL_GC, (g + 1) * POOL_GC), at=slice(g, g + 1))
    update("pool_scale", row(R_POOL, W_MIX, 2 * W_MIX))
    update("conv_dw", gdw_ref[...])
    update("conv_b", row(R_CONV, 0, W_MIX))
    update("ln_g", row(R_CONV, W_MIX, 2 * W_MIX))
    update("ln_b", row(R_LN, 0, W_MIX))
    update("pw_w", gpww_ref[...])
    update("pw_b", row(R_LN, W_MIX, 2 * W_MIX))
    update("w_out", gwout_ref[...])
    update("final_g", row(R_FINAL_G, 0, D_MODEL))


def _adamw(body, name, params, ws, ms, vs, grad_sources, extra_out_shapes=()):
    n = len(params)
    flat = lambda a: a.reshape(1, -1) if a.ndim == 1 else a
    args = [flat(d[p]) for d in (ws, ms, vs) for p in params]
    outs = pl.pallas_call(
        body,
        name=name,
        in_specs=[VMEM] * (3 * n + len(grad_sources)),
        out_specs=[VMEM] * (4 * n + len(extra_out_shapes)),
        out_shape=[jax.ShapeDtypeStruct(a.shape, F32) for a in args[:n]] * 4 + list(extra_out_shapes),
        compiler_params=pltpu.CompilerParams(vmem_limit_bytes=VMEM_LIMIT),
    )(*args, *grad_sources)
    groups = tuple({p: outs[k * n + j].reshape(ws[p].shape) for j, p in enumerate(params)} for k in range(4))
    return groups, outs[4 * n:]


def kernel(x, norm_g, w_in, pool_w, pool_b, pool_scale, conv_dw, conv_b, ln_g, ln_b, pw_w, pw_b, w_out, final_g, loss_target, m_norm_g, m_w_in, m_pool_w, m_pool_b, m_pool_scale, m_conv_dw, m_conv_b, m_ln_g, m_ln_b, m_pw_w, m_pw_b, m_w_out, m_final_g, v_norm_g, v_w_in, v_pool_w, v_pool_b, v_pool_scale, v_conv_dw, v_conv_b, v_ln_g, v_ln_b, v_pw_w, v_pw_b, v_w_out, v_final_g):
    proj, w_in_t, w_out_b, pw_w_b, conv_dw_f = _inproj(x, norm_g, w_in.T, w_out, pw_w, conv_dw)
    h1, z, ob, dpool, dh, loss8, gfg8 = _fwd(x, loss_target, proj, pool_w, pool_b, pool_scale, conv_dw_f,
                                             conv_b, ln_g, ln_b, pw_w_b, pw_b, w_out_b, final_g)
    dproj, gwout, gpww, gpoolw, gvec, gdw8 = _bwd1(dh, proj, h1, z, ob, dpool, w_out_b, pw_w_b, pool_w, pool_scale,
                                                   conv_dw_f, ln_g, ln_b)
    win_own, win_staged, g_w_out, g_pw_w, vec, g_conv_dw, poolw_sum = _bwd2a(x, dproj, norm_g, gwout, gpww, loss8, gfg8, gvec,
                                                                             gdw8, gpoolw)
    send_sems, recv_sems, staged_thru, land_thru, token = _rs_start(win_staged)
    grad_x, gng8 = _bwd2b(x, dh, dproj, norm_g, w_in_t, token)
    ng_send, ng_recv, gng8, gng_land = _partials_start(gng8)
    win_others = _rs_wait(send_sems, recv_sems, staged_thru, land_thru, gng8)

    weights = dict(norm_g=norm_g, w_in=w_in.T, pool_w=pool_w, pool_b=pool_b, pool_scale=pool_scale, conv_dw=conv_dw, conv_b=conv_b,
                   ln_g=ln_g, ln_b=ln_b, pw_w=pw_w, pw_b=pw_b, w_out=w_out, final_g=final_g)
    m_in = dict(norm_g=m_norm_g, w_in=m_w_in.T, pool_w=m_pool_w, pool_b=m_pool_b, pool_scale=m_pool_scale, conv_dw=m_conv_dw,
                conv_b=m_conv_b, ln_g=m_ln_g, ln_b=m_ln_b, pw_w=m_pw_w, pw_b=m_pw_b, w_out=m_w_out, final_g=m_final_g)
    v_in = dict(norm_g=v_norm_g, w_in=v_w_in.T, pool_w=v_pool_w, pool_b=v_pool_b, pool_scale=v_pool_scale, conv_dw=v_conv_dw,
                conv_b=v_conv_b, ln_g=v_ln_g, ln_b=v_ln_b, pw_w=v_pw_w, pw_b=v_pw_b, w_out=v_w_out, final_g=v_final_g)
    main, (loss,) = _adamw(functools.partial(_adamw_body), "adamw", MAIN_PARAMS, weights, m_in, v_in,
                           (win_own, win_others, g_w_out, g_pw_w, vec, g_conv_dw, poolw_sum), [jax.ShapeDtypeStruct((1, 1), F32)])
    gng8, gng_others = _partials_wait(ng_send, ng_recv, gng8, gng_land, main[0]["w_out"])
    late, _ = _adamw(functools.partial(_adamw_late_body), "adamw_norm_g", LATE_PARAMS, weights, m_in, v_in, (gng8, gng_others))
    groups = [[(late[k] if p in LATE_PARAMS else main[k])[p] for p in PARAMS] for k in range(4)]
    k_in = PARAMS.index("w_in")
    for group in groups:
        group[k_in] = group[k_in].T
    grads, delta, new_m, new_v = groups
    return (loss.reshape(()), grad_x, *grads, *delta, *new_m, *new_v)
```

```python
import functools

import jax
import jax.numpy as jnp
from jax import lax
from jax.experimental import pallas as pl
from jax.experimental.pallas import tpu as pltpu

F32 = jnp.float32
BF16 = jnp.bfloat16

D_MODEL = 1024
W_MIX = 512
D_IN = 5 * W_MIX
POOL_WINDOWS = (2, 4, 8, 16)
POOL_GC = 128
CONV_WIDTH = 31
RMS_EPS = 1e-6
LN_EPS = 1e-5
N_DEV = 8

LANES = 128
SUBLANES = 8
TILE_M = 512
TILE_IN = 1024
POOL_HALO = 16
CONV_HALO = 32
CONV_ROWS = 128
BWD_CONV_ROWS = 64
BWD_SUBTILES = 2
VMEM_LIMIT = 56 * 1024 * 1024

VMEM = pl.BlockSpec(memory_space=pltpu.VMEM)


def _silu(v):
    return v * jax.nn.sigmoid(v)


def _colsum8(v):
    m, n = v.shape
    return jnp.sum(v.reshape(m // SUBLANES, SUBLANES, n), axis=0)


def _row_index(shape, t0):
    return lax.broadcasted_iota(jnp.int32, shape, 0) + t0


def _conv_taps():
    taps = {s: [] for s in range(SUBLANES)}
    for k in range(CONV_WIDTH):
        e = k + CONV_HALO - (CONV_WIDTH - 1)
        taps[e % SUBLANES].append((e // SUBLANES, k))
    return taps


def _fwd_body(x_ref, tgt_ref, proj_ref, poolw_ref, poolb_ref, pools_ref, dw_ref, cb_ref, lg_ref, lb_ref,
              pww_ref, pwb_ref, wout_ref, fg_ref,
              h1_ref, z_ref, ob_ref, dpool_ref, dh_ref, loss_ref, gfg_ref,
              uext, hext, y_s):
    b = pl.program_id(0)
    i = pl.program_id(1)
    tm = TILE_M

    @pl.when(i == 0)
    def _():
        uext[0:POOL_HALO, :] = jnp.zeros((POOL_HALO, W_MIX), F32)
        hext[0:CONV_HALO, :] = jnp.zeros((CONV_HALO, W_MIX), F32)
        hext[CONV_HALO + tm:, :] = jnp.zeros((SUBLANES, W_MIX), F32)

    @pl.when((b == 0) & (i == 0))
    def _():
        loss_ref[...] = jnp.zeros_like(loss_ref)
        gfg_ref[...] = jnp.zeros_like(gfg_ref)

    uext[POOL_HALO:, :] = proj_ref[0, :, 0:W_MIX]
    t = _row_index((tm, POOL_GC), i * tm)
    for g, w in enumerate(POOL_WINDOWS):
        lanes = slice(g * POOL_GC, (g + 1) * POOL_GC)
        e = uext[:, lanes]
        s = e
        sh = 1
        while sh < w:
            s = s + pltpu.roll(s, sh, 0)
            sh *= 2
        inv_cnt = 1.0 / jnp.minimum(t + 1, w).astype(F32)
        d = (s[POOL_HALO:] * inv_cnt - e[POOL_HALO:]).astype(BF16)
        dpool_ref[0, :, lanes] = d
        z = jnp.dot(d, poolw_ref[g].astype(BF16), preferred_element_type=F32) + poolb_ref[:, lanes]
        z_ref[0, :, lanes] = z
        a_gate = proj_ref[0, :, W_MIX + g * POOL_GC:W_MIX + (g + 1) * POOL_GC]
        y_s[:, lanes] = (z * pools_ref[:, lanes] * _silu(a_gate)).astype(BF16)

    hext[CONV_HALO:CONV_HALO + tm, :] = proj_ref[0, :, 2 * W_MIX:3 * W_MIX] * jax.nn.sigmoid(proj_ref[0, :, 3 * W_MIX:4 * W_MIX])
    taps = _conv_taps()
    for c0 in range(0, tm, CONV_ROWS):
        for l0 in range(0, W_MIX, LANES):
            lanes = slice(l0, l0 + LANES)
            acc = jnp.zeros((CONV_ROWS, LANES), F32) + cb_ref[:, lanes]
            for s in range(SUBLANES):
                part = jnp.zeros((CONV_ROWS + SUBLANES, LANES), F32)
                for q, k in taps[s]:
                    r0 = c0 + SUBLANES * q
                    part = part + hext[r0:r0 + CONV_ROWS + SUBLANES, lanes] * dw_ref[k:k + 1, lanes]
                acc = acc + part[s:s + CONV_ROWS]
            h1_ref[0, c0:c0 + CONV_ROWS, lanes] = acc
    h1 = h1_ref[0]
    mu = jnp.mean(h1, axis=-1, keepdims=True)
    xc = h1 - mu
    rstd = lax.rsqrt(jnp.mean(xc * xc, axis=-1, keepdims=True) + LN_EPS)
    h3 = _silu(xc * rstd * lg_ref[...] + lb_ref[...])
    ob = jnp.dot(h3.astype(BF16), pww_ref[...], preferred_element_type=F32) + pwb_ref[...]
    ob_ref[0] = ob
    y_s[:, W_MIX:] = (ob * _silu(proj_ref[0, :, 4 * W_MIX:5 * W_MIX])).astype(BF16)

    h = x_ref[0] + jnp.dot(y_s[...], wout_ref[...], preferred_element_type=F32)
    r2 = lax.rsqrt(jnp.mean(h * h, axis=-1, keepdims=True) + RMS_EPS)
    hr = h * r2
    err = hr * fg_ref[...] - tgt_ref[0]
    loss_ref[...] += jnp.sum(err * err) * (0.5 / D_MODEL)
    dout = err * (1.0 / D_MODEL)
    gfg_ref[...] += _colsum8(dout * hr)
    gy = dout * fg_ref[...]
    dh_ref[0] = r2 * (gy - hr * jnp.mean(gy * hr, axis=-1, keepdims=True))

    uext[0:POOL_HALO, :] = uext[tm:tm + POOL_HALO, :]
    hext[0:CONV_HALO, :] = hext[tm:tm + CONV_HALO, :]


def _row(v):
    return v.reshape(1, -1)


def _fwd(x, tgt, proj, pool_w, pool_b, pool_scale, conv_dw_f, conv_b, ln_g, ln_b, pw_w_b, pw_b, w_out_b, final_g):
    bl, s, _ = x.shape
    tm = TILE_M
    tok = lambda n: pl.BlockSpec((1, tm, n), lambda b, i: (b, i, 0))
    acc = lambda n: pl.BlockSpec((SUBLANES, n), lambda b, i: (0, 0))
    act = lambda n: jax.ShapeDtypeStruct((bl, s, n), F32)
    return pl.pallas_call(
        functools.partial(_fwd_body),
        name="fwd",
        grid=(bl, s // tm),
        in_specs=[tok(D_MODEL), tok(D_MODEL), tok(D_IN)] + [VMEM] * 11,
        out_specs=[tok(W_MIX), tok(W_MIX), tok(W_MIX), tok(W_MIX), tok(D_MODEL), acc(LANES), acc(D_MODEL)],
        out_shape=[act(W_MIX), act(W_MIX), act(W_MIX), jax.ShapeDtypeStruct((bl, s, W_MIX), BF16), act(D_MODEL),
                   jax.ShapeDtypeStruct((SUBLANES, LANES), F32), jax.ShapeDtypeStruct((SUBLANES, D_MODEL), F32)],
        scratch_shapes=[pltpu.VMEM((POOL_HALO + tm, W_MIX), F32), pltpu.VMEM((CONV_HALO + tm + SUBLANES, W_MIX), F32),
                        pltpu.VMEM((tm, D_MODEL), BF16)],
        compiler_params=pltpu.CompilerParams(dimension_semantics=("arbitrary", "arbitrary"), vmem_limit_bytes=VMEM_LIMIT),
    )(x, tgt, proj, pool_w, pool_b.reshape(1, W_MIX), _row(pool_scale), conv_dw_f, _row(conv_b),
      _row(ln_g), _row(ln_b), pw_w_b, _row(pw_b), w_out_b, _row(final_g))


NT_DIMS = (((1,), (1,)), ((), ()))
TN_DIMS = (((0,), (0,)), ((), ()))
V_POOL_B, V_POOL_SCALE, V_CONV_B, V_LN_G, V_LN_B, V_PW_B = range(6)


def _bwd_taps():
    taps = {s: [] for s in range(SUBLANES)}
    for k in range(CONV_WIDTH):
        e = CONV_WIDTH - 1 - k
        taps[e % SUBLANES].append((e // SUBLANES, k))
    return taps


def _bwd1_body(n_tiles, dh_ref, proj_ref, h1_ref, z_ref, ob_ref, dpool_ref, wout_ref, pww_ref, poolw_ref, pools_ref, dw_ref,
               lg_ref, lb_ref,
               dproj_ref, gwout_ref, gpww_ref, gpoolw_ref, gvec_ref, gdw_ref,
               vext, gext, gsh, y_s):
    b = pl.program_id(0)
    i = pl.program_id(1)
    tm = TILE_M
    t0 = (n_tiles - 1 - i) * tm

    @pl.when(i == 0)
    def _():
        vext[tm:, :] = jnp.zeros((POOL_HALO, W_MIX), F32)
        gext[tm:, :] = jnp.zeros((CONV_HALO, W_MIX), F32)

    @pl.when((b == 0) & (i == 0))
    def _():
        gwout_ref[...] = jnp.zeros_like(gwout_ref)
        gpww_ref[...] = jnp.zeros_like(gpww_ref)
        gpoolw_ref[...] = jnp.zeros_like(gpoolw_ref)
        gvec_ref[...] = jnp.zeros_like(gvec_ref)
        gdw_ref[...] = jnp.zeros_like(gdw_ref)

    sub = tm // BWD_SUBTILES

    def dense_phase(r0):
        rs = slice(r0, r0 + sub)
        dhb = dh_ref[0, rs, :].astype(BF16)
        dy_a = lax.dot_general(dhb, wout_ref[0:W_MIX, :], NT_DIMS, preferred_element_type=F32)
        dy_b = lax.dot_general(dhb, wout_ref[W_MIX:, :], NT_DIMS, preferred_element_type=F32)

        a_gate = proj_ref[0, rs, W_MIX:2 * W_MIX]
        sig_a = jax.nn.sigmoid(a_gate)
        silu_a = a_gate * sig_a
        z = z_ref[0, rs, :]
        out_a = z * pools_ref[...]
        y_s[rs, 0:W_MIX] = (out_a * silu_a).astype(BF16)
        d_out_a = dy_a * silu_a
        d_ag = dy_a * out_a * (sig_a * (1.0 + a_gate * (1.0 - sig_a)))
        dproj_ref[0, rs, W_MIX:2 * W_MIX] = d_ag.astype(BF16)
        gvec_ref[V_POOL_SCALE] += _colsum8(d_out_a * z)
        dz = d_out_a * pools_ref[...]
        gvec_ref[V_POOL_B] += _colsum8(dz)
        t = _row_index((sub, POOL_GC), t0 + r0)
        n_ext = sub + POOL_HALO
        for g, w in enumerate(POOL_WINDOWS):
            lanes = slice(g * POOL_GC, (g + 1) * POOL_GC)
            dzg = dz[:, lanes].astype(BF16)
            gpoolw_ref[g] += lax.dot_general(dpool_ref[0, rs, lanes], dzg, TN_DIMS, preferred_element_type=F32)
            dd = lax.dot_general(dzg, poolw_ref[g].astype(BF16), NT_DIMS, preferred_element_type=F32)
            vext[rs, lanes] = dd * (1.0 / jnp.minimum(t + 1, w).astype(F32))
            s = vext[r0:r0 + n_ext, lanes]
            sh = 1
            while sh < w:
                s = s + pltpu.roll(s, n_ext - sh, 0)
                sh *= 2
            dproj_ref[0, rs, lanes] = (s[0:sub] - dd).astype(BF16)

        b_gate = proj_ref[0, rs, 4 * W_MIX:5 * W_MIX]
        sig_b = jax.nn.sigmoid(b_gate)
        silu_b = b_gate * sig_b
        ob = ob_ref[0, rs, :]
        y_s[rs, W_MIX:] = (ob * silu_b).astype(BF16)
        d_ob = dy_b * silu_b
        dproj_ref[0, rs, 4 * W_MIX:5 * W_MIX] = (dy_b * ob * (sig_b * (1.0 + b_gate * (1.0 - sig_b)))).astype(BF16)
        gvec_ref[V_PW_B] += _colsum8(d_ob)
        gwout_ref[...] += lax.dot_general(y_s[rs, :], dhb, TN_DIMS, preferred_element_type=F32)

        h1 = h1_ref[0, rs, :]
        mu = jnp.mean(h1, axis=-1, keepdims=True)
        xc = h1 - mu
        rstd = lax.rsqrt(jnp.mean(xc * xc, axis=-1, keepdims=True) + LN_EPS)
        xhat = xc * rstd
        h2 = xhat * lg_ref[...] + lb_ref[...]
        sig2 = jax.nn.sigmoid(h2)
        d_obb = d_ob.astype(BF16)
        gpww_ref[...] += lax.dot_general((h2 * sig2).astype(BF16), d_obb, TN_DIMS, preferred_element_type=F32)
        dh2 = lax.dot_general(d_obb, pww_ref[...], NT_DIMS, preferred_element_type=F32) * (sig2 * (1.0 + h2 * (1.0 - sig2)))
        gvec_ref[V_LN_G] += _colsum8(dh2 * xhat)
        gvec_ref[V_LN_B] += _colsum8(dh2)
        dxh = dh2 * lg_ref[...]
        dh1 = rstd * (dxh - jnp.mean(dxh, axis=-1, keepdims=True) - xhat * jnp.mean(dxh * xhat, axis=-1, keepdims=True))
        gvec_ref[V_CONV_B] += _colsum8(dh1)
        gext[rs, :] = dh1

    taps = _bwd_taps()
    cr = BWD_CONV_ROWS
    win = cr + CONV_HALO - SUBLANES

    def conv_phase(r0):
        for c0 in range(r0, r0 + sub, cr):
            for l0 in range(0, W_MIX, LANES):
                lanes = slice(l0, l0 + LANES)
                b_val = proj_ref[0, c0:c0 + cr, 2 * W_MIX + l0:2 * W_MIX + l0 + LANES]
                sg = jax.nn.sigmoid(proj_ref[0, c0:c0 + cr, 3 * W_MIX + l0:3 * W_MIX + l0 + LANES])
                h0c = b_val * sg
                acc = jnp.zeros((cr, LANES), F32)
                for s in range(SUBLANES):
                    gsh[...] = gext[c0 + s:c0 + s + win, lanes]
                    for q, k in taps[s]:
                        gq = gsh[SUBLANES * q:SUBLANES * q + cr, :]
                        acc = acc + gq * dw_ref[k:k + 1, lanes]
                        gdw_ref[k, :, lanes] += _colsum8(h0c * gq)
                d_bval = acc * sg
                dproj_ref[0, c0:c0 + cr, 2 * W_MIX + l0:2 * W_MIX + l0 + LANES] = d_bval.astype(BF16)
                dproj_ref[0, c0:c0 + cr, 3 * W_MIX + l0:3 * W_MIX + l0 + LANES] = (d_bval * b_val * (1.0 - sg)).astype(BF16)

    starts = [k * sub for k in reversed(range(BWD_SUBTILES))]
    for r0 in starts:
        dense_phase(r0)
    for r0 in starts:
        conv_phase(r0)
    vext[tm:, :] = vext[0:POOL_HALO, :]
    gext[tm:, :] = gext[0:CONV_HALO, :]


def _bwd1(dh, proj, h1, z, ob, dpool, w_out_b, pw_w_b, pool_w, pool_scale, conv_dw_f, ln_g, ln_b):
    bl, s, _ = dh.shape
    tm = TILE_M
    nt = s // tm
    tok = lambda n: pl.BlockSpec((1, tm, n), lambda b, i: (b, nt - 1 - i, 0))
    res = lambda shape: pl.BlockSpec(shape, lambda b, i: (0,) * len(shape))
    out_shapes = [(D_MODEL, D_MODEL), (W_MIX, W_MIX), (len(POOL_WINDOWS), POOL_GC, POOL_GC), (6, SUBLANES, W_MIX),
                  (CONV_WIDTH, SUBLANES, W_MIX)]
    return pl.pallas_call(
        functools.partial(_bwd1_body, nt),
        name="bwd1",
        grid=(bl, nt),
        in_specs=[tok(D_MODEL), tok(D_IN), tok(W_MIX), tok(W_MIX), tok(W_MIX), tok(W_MIX)] + [VMEM] * 7,
        out_specs=[tok(D_IN)] + [res(sh) for sh in out_shapes],
        out_shape=[jax.ShapeDtypeStruct((bl, s, D_IN), BF16)] + [jax.ShapeDtypeStruct(sh, F32) for sh in out_shapes],
        scratch_shapes=[pltpu.VMEM((tm + POOL_HALO, W_MIX), F32), pltpu.VMEM((tm + CONV_HALO, W_MIX), F32),
                        pltpu.VMEM((BWD_CONV_ROWS + CONV_HALO - SUBLANES, LANES), F32), pltpu.VMEM((tm, D_MODEL), BF16)],
        compiler_params=pltpu.CompilerParams(dimension_semantics=("arbitrary", "arbitrary"), vmem_limit_bytes=VMEM_LIMIT),
    )(dh, proj, h1, z, ob, dpool, w_out_b, pw_w_b, pool_w, _row(pool_scale), conv_dw_f, _row(ln_g), _row(ln_b))


MESH = pl.DeviceIdType.MESH
W_IN_BLK = D_IN // N_DEV
W_OUT_BLK = D_MODEL // N_DEV
PW_BLK = W_MIX // N_DEV
DW_BLK = W_MIX // N_DEV
N_PEER = N_DEV - 1


SLAB = 2 * W_IN_BLK
N_SLABS = 4
F_SIB, F_X, F_Y, F_ON, F_PASS_X, F_PASS_Y, F_PASS_D = range(7)
W_IN_ONLY, OTHERS = (0,), (1, 2, 3)


def _inproj_body(n_tiles, x_hbm, ng_ref, wint_ref, wout_ref, pww_ref, dw_ref,
                 proj_hbm, wt_hbm, woutf_hbm, pwwf_hbm, dwf_ref,
                 hn_s, wt_s, wout_s, pww_s, dw_blk, stage, xbuf, out_sems, x_sems, w_sems, send_sems, recv_sems):
    s = pl.program_id(0)
    i = pl.program_id(1)
    tm = TILE_IN
    x, y, c = lax.axis_index("x"), lax.axis_index("y"), lax.axis_index("c")
    sibling = (x, y, 1 - c)
    xn, yn, dg = (1 - x, y), (x, 1 - y), (1 - x, 1 - y)
    pick = lambda a, b: tuple(jnp.where(c == 1, p, q) for p, q in zip(a, b))
    on_from, on_to = pick(xn, yn), pick(yn, xn)

    def blocks(px, py, pc):
        k = 4 * px + 2 * py + pc
        rows = lambda ref, n: ref.at[pl.ds(pl.multiple_of(k * n, n), n), :]
        return (rows(wt_s, W_IN_BLK), rows(wout_s, W_OUT_BLK), rows(pww_s, PW_BLK), dw_blk.at[k])

    def copies(arrays, flow, block, to):
        refs = blocks(*block)
        return [pltpu.make_async_remote_copy(src_ref=refs[a], dst_ref=refs[a], send_sem=send_sems.at[a, flow],
                                             recv_sem=recv_sems.at[a, flow], device_id=to, device_id_type=MESH)
                for a in arrays]

    def start(arrays):
        for flow, to in ((F_SIB, sibling), (F_X, (*xn, c)), (F_Y, (*yn, c))):
            for cp in copies(arrays, flow, (x, y, c), to):
                cp.start()

    def direct(arrays):
        for flow, chip in ((F_X, xn), (F_Y, yn)):
            for cp in copies(arrays, flow, (*chip, c), sibling):
                cp.wait_recv()
        for cp in copies(arrays, F_ON, (*on_from, c), (*on_to, c)):
            cp.start()
        for flow, chip in ((F_PASS_X, xn), (F_PASS_Y, yn)):
            for cp in copies(arrays, flow, (*chip, c), sibling):
                cp.start()

    def diagonal(arrays):
        for cp in copies(arrays, F_ON, (*dg, c), sibling):
            cp.wait_recv()
        for cp in copies(arrays, F_PASS_D, (*dg, c), sibling):
            cp.start()

    def from_sibling(arrays, flow, chip):
        for cp in copies(arrays, flow, (*chip, 1 - c), sibling):
            cp.wait_recv()

    def finish(arrays):
        mine = (x, y, c)
        for flow, block in ((F_SIB, mine), (F_X, mine), (F_Y, mine), (F_ON, (*on_from, c)), (F_PASS_X, (*xn, c)),
                            (F_PASS_Y, (*yn, c)), (F_PASS_D, (*dg, c))):
            for cp in copies(arrays, flow, block, sibling):
                cp.wait_send()

    step = s * n_tiles + i
    par = step % 2
    first = step == 0

    @pl.when(first)
    def _():
        mine = blocks(x, y, c)
        mine[0][...] = wint_ref[...].astype(BF16)
        mine[1][...] = wout_ref[...].astype(BF16)
        mine[2][...] = pww_ref[...].astype(BF16)
        mine[3][...] = dw_ref[...]
        start(W_IN_ONLY)
        start(OTHERS)

    rows_i = pl.ds(pl.multiple_of(i * tm, tm), tm)
    tiles_per_seq = proj_hbm.shape[1] // tm

    def x_copy(k):
        src = x_hbm.at[k // tiles_per_seq, pl.ds(pl.multiple_of((k % tiles_per_seq) * tm, tm), tm), :]
        return pltpu.make_async_copy(src, xbuf.at[k % 2], x_sems.at[k % 2])

    @pl.when(first)
    def _():
        x_copy(0).start()

    @pl.when(s == 0)
    def _():
        x_copy(i).wait()

        @pl.when(i + 1 < n_tiles)
        def _():
            x_copy(i + 1).start()

        xt = xbuf[i % 2]
        r1 = lax.rsqrt(jnp.mean(xt * xt, axis=-1, keepdims=True) + RMS_EPS)
        hn_s[rows_i, :] = (xt * r1 * ng_ref[...]).astype(BF16)

    @pl.when(first)
    def _():
        from_sibling(W_IN_ONLY, F_SIB, (x, y))

    @pl.when((s == 1) & (i == 0))
    def _():
        direct(W_IN_ONLY)
        from_sibling(W_IN_ONLY, F_PASS_Y, yn)

    @pl.when((s == 2) & (i == 0))
    def _():
        from_sibling(W_IN_ONLY, F_PASS_X, xn)

    weights_out = [pltpu.make_async_copy(src, dst, w_sems.at[k])
                   for k, (src, dst) in enumerate(((wt_s, wt_hbm), (wout_s, woutf_hbm), (pww_s, pwwf_hbm)))]

    @pl.when((s == 3) & (i == 0))
    def _():
        diagonal(W_IN_ONLY)
        direct(OTHERS)
        from_sibling(W_IN_ONLY, F_PASS_D, dg)
        weights_out[0].start()

    def out_copy(parity, col0):
        dst = proj_hbm.at[i // tiles_per_seq, pl.ds(pl.multiple_of((i % tiles_per_seq) * tm, tm), tm),
                          pl.ds(pl.multiple_of(col0, LANES), SLAB)]
        return pltpu.make_async_copy(stage.at[parity], dst, out_sems.at[parity])

    @pl.when(step >= 2)
    def _():
        out_copy(par, 0).wait()

    chip = jnp.where(s < 2, 2 * x, 2 * (1 - x)) + jnp.where((s == 0) | (s == 2), y, 1 - y)
    slab = wt_s[pl.ds(pl.multiple_of(chip * SLAB, SLAB), SLAB), :]
    stage[par] = lax.dot_general(hn_s[rows_i, :], slab, NT_DIMS, preferred_element_type=F32)
    out_copy(par, chip * SLAB).start()

    @pl.when(step == N_SLABS * n_tiles - 1)
    def _():
        out_copy(par, 0).wait()
        out_copy(1 - par, 0).wait()
        from_sibling(OTHERS, F_SIB, (x, y))
        from_sibling(OTHERS, F_PASS_X, xn)
        from_sibling(OTHERS, F_PASS_Y, yn)
        diagonal(OTHERS)
        from_sibling(OTHERS, F_PASS_D, dg)
        for cp in weights_out[1:]:
            cp.start()
        finish(W_IN_ONLY)
        finish(OTHERS)
        for cp in weights_out:
            cp.wait()
        for k in range(N_DEV):
            dwf_ref[:, k * DW_BLK:(k + 1) * DW_BLK] = dw_blk[k]


def _inproj(x, norm_g, w_in_t, w_out, pw_w, conv_dw):
    bl, s, _ = x.shape
    tm = TILE_IN
    nt = s // tm
    n_tiles = bl * nt
    dma = pltpu.SemaphoreType.DMA
    return pl.pallas_call(
        functools.partial(_inproj_body, n_tiles),
        name="inproj",
        grid=(N_SLABS, n_tiles),
        in_specs=[pl.BlockSpec(memory_space=pl.ANY)] + [VMEM] * 5,
        out_specs=[pl.BlockSpec(memory_space=pl.ANY)] * 4 + [VMEM],
        out_shape=[jax.ShapeDtypeStruct((bl, s, D_IN), F32), jax.ShapeDtypeStruct((D_IN, D_MODEL), BF16),
                   jax.ShapeDtypeStruct((D_MODEL, D_MODEL), BF16), jax.ShapeDtypeStruct((W_MIX, W_MIX), BF16),
                   jax.ShapeDtypeStruct((CONV_WIDTH, W_MIX), F32)],
        scratch_shapes=[pltpu.VMEM((n_tiles * tm, D_MODEL), BF16), pltpu.VMEM((D_IN, D_MODEL), BF16),
                        pltpu.VMEM((D_MODEL, D_MODEL), BF16), pltpu.VMEM((W_MIX, W_MIX), BF16),
                        pltpu.VMEM((N_DEV, CONV_WIDTH, DW_BLK), F32), pltpu.VMEM((2, tm, SLAB), F32),
                        pltpu.VMEM((2, tm, D_MODEL), F32), dma((2,)), dma((2,)), dma((3,)), dma((4, 7)), dma((4, 7))],
        compiler_params=pltpu.CompilerParams(dimension_semantics=("arbitrary", "arbitrary"), vmem_limit_bytes=VMEM_LIMIT),
    )(x, _row(norm_g), w_in_t, w_out, pw_w, conv_dw)


B_PLAIN, B_ON, B_MERGED = range(3)


def _routes():
    x, y, c = lax.axis_index("x"), lax.axis_index("y"), lax.axis_index("c")
    xn, yn = (1 - x, y), (x, 1 - y)
    pick = lambda a, b: tuple(jnp.where(c == 1, p, q) for p, q in zip(a, b))
    return pick(xn, yn), pick(yn, xn), (1 - x, 1 - y), (x, y), c


def _chip_index(chip):
    return 2 * chip[0] + chip[1]


def _copy(src, dst, send_sems, recv_sems, a, k, to):
    return pltpu.make_async_remote_copy(src_ref=src, dst_ref=dst, send_sem=send_sems.at[a, k], recv_sem=recv_sems.at[a, k],
                                        device_id=to, device_id_type=MESH)


def _rsb_send(plain, on, rb_plain, rb_on, send_b, recv_b):
    first, _, _, _, c = _routes()
    for k, srcs, dsts in ((B_ON, on, rb_on), (B_PLAIN, plain, rb_plain)):
        for a, (s, d) in enumerate(zip(srcs, dsts)):
            _copy(s, d, send_b, recv_b, a, k, (*first, c)).start()


def _rsb_merge(second_sums, merged, rb_on, rb_merged, send_b, recv_b):
    _, second, _, _, c = _routes()
    for a, (own, m, got, dst) in enumerate(zip(second_sums, merged, rb_on, rb_merged)):
        _copy(got, got, send_b, recv_b, a, B_ON, (*second, c)).wait_recv()
        m[...] = (own[...] + got[...].astype(F32)).astype(BF16)
        _copy(m, dst, send_b, recv_b, a, B_MERGED, (*second, c)).start()


def _rsb_total(own_sums, plain, on, merged, rb_plain, rb_merged, send_b, recv_b):
    _, second, _, _, c = _routes()
    out = []
    for a, (own, p, g) in enumerate(zip(own_sums, rb_plain, rb_merged)):
        _copy(p, p, send_b, recv_b, a, B_PLAIN, (*second, c)).wait_recv()
        _copy(g, g, send_b, recv_b, a, B_MERGED, (*second, c)).wait_recv()
        out.append(own[...] + p[...].astype(F32) + g[...].astype(F32))
    for k, srcs in ((B_PLAIN, plain), (B_ON, on), (B_MERGED, merged)):
        for a, s in enumerate(srcs):
            _copy(s, s, send_b, recv_b, a, k, (*second, c)).wait_send()
    return out


R_LOSS, R_FINAL_G, R_POOL, R_CONV, R_LN = range(5)
VEC_ROWS = 16
DW_ROWS = 32
A_PART, A_X, A_Y, A_ON, A_HALF = range(5)
HALF_DTYPES = (F32, BF16, BF16)
STEP_SUMS, STEP_MERGE, STEP_HALF = 1, 3, 5


def _bwd2a_body(n_steps, x_ref, dproj_ref, ng_ref, gwout_hbm, gpww_hbm, loss_ref, gfg_ref, gvec_ref, gdw_ref, gpoolw_ref,
                win_own_ref, win_staged_ref, owout_ref, opww_ref, ovec_ref, odw_ref, opoolw_ref,
                acc, st_win, a_win, own_wout, own_pww, a_wout, a_pww, st_wout, st_pww, rb_wout, rb_pww, part_vec, part_dw,
                part_pw, sib_vec, sib_dw, sib_pw, res_vec, res_dw, res_pw, half_vec, half_dw, half_pw,
                own_sems, send_a, recv_a, send_b, recv_b, send_r, recv_r, send_w, recv_w):
    step = pl.program_id(0) * pl.num_programs(1) + pl.program_id(1)
    x, y, c = lax.axis_index("x"), lax.axis_index("y"), lax.axis_index("c")
    sibling = (x, y, 1 - c)
    xn, yn = (1 - x, y), (x, 1 - y)
    first, second, diag, own, _ = _routes()
    rows = lambda ref, k, n: ref.at[pl.ds(pl.multiple_of(k * n, n), n), :]

    blocks = lambda k: (rows(gwout_hbm, k, W_OUT_BLK), rows(gpww_hbm, k, PW_BLK))
    mine = lambda chip: (own_wout.at[chip], own_pww.at[chip])
    mine_copies = lambda chip: [pltpu.make_async_copy(s, d, own_sems.at[a, chip])
                                for a, (s, d) in enumerate(zip(blocks(2 * chip + c), mine(chip)))]
    landed = lambda chip: (a_wout.at[chip], a_pww.at[chip])
    staged = lambda k: (st_wout.at[k], st_pww.at[k])
    arrived = lambda k: (rb_wout.at[k], rb_pww.at[k])

    parts, sibs, halves = (part_vec, part_dw, part_pw), (sib_vec, sib_dw, sib_pw), (half_vec, half_dw, half_pw)
    results = (res_vec, res_dw, res_pw)
    half_rows = lambda ref: pl.ds(pl.multiple_of(c * (ref.shape[0] // 2), SUBLANES), ref.shape[0] // 2)
    slot = lambda chip: tuple(h.at[_chip_index(chip)] for h in halves)

    @pl.when(step == 0)
    def _():
        acc[...] = jnp.zeros_like(acc)
        sum8 = lambda v: jnp.sum(v, axis=0, keepdims=True)
        part_vec[...] = jnp.zeros((VEC_ROWS, D_MODEL), F32)
        part_vec[R_LOSS:R_LOSS + 1, :] = jnp.broadcast_to(loss_ref[0:1, 0:1], (1, D_MODEL))
        part_vec[R_FINAL_G:R_FINAL_G + 1, :] = sum8(gfg_ref[...])
        for row, (lo, hi) in ((R_POOL, (V_POOL_B, V_POOL_SCALE)), (R_CONV, (V_CONV_B, V_LN_G)), (R_LN, (V_LN_B, V_PW_B))):
            part_vec[row:row + 1, 0:W_MIX] = sum8(gvec_ref[lo])
            part_vec[row:row + 1, W_MIX:] = sum8(gvec_ref[hi])
        part_dw[0:CONV_WIDTH, :] = jnp.sum(gdw_ref[...], axis=1)
        part_dw[CONV_WIDTH:, :] = jnp.zeros((DW_ROWS - CONV_WIDTH, W_MIX), F32)
        part_pw[...] = gpoolw_ref[...].reshape(len(POOL_WINDOWS) * POOL_GC, POOL_GC)
        for a, (p, s) in enumerate(zip(parts, sibs)):
            _copy(p, s, send_r, recv_r, a, A_PART, sibling).start()
        for chip in range(4):
            for a, (s, d) in enumerate(zip(blocks(2 * chip + (1 - c)), landed(chip))):
                _copy(s, d, send_a, recv_a, a, chip, sibling).start()
            for cp in mine_copies(chip):
                cp.start()

    x_t = x_ref[0]
    r1 = lax.rsqrt(jnp.mean(x_t * x_t, axis=-1, keepdims=True) + RMS_EPS)
    hn = (x_t * r1 * ng_ref[...]).astype(BF16)
    acc[...] += lax.dot_general(dproj_ref[0], hn, TN_DIMS, preferred_element_type=F32)

    @pl.when(step == STEP_SUMS)
    def _():
        for a, (p, s, h) in enumerate(zip(parts, sibs, slot(own))):
            _copy(p, s, send_r, recv_r, a, A_PART, sibling).wait_recv()
            h[...] = (p[half_rows(p), :] + s[half_rows(s), :]).astype(h.dtype)
            for k, to in ((A_X, xn), (A_Y, yn)):
                _copy(h, h, send_r, recv_r, a, k, (*to, c)).start()
        for chip in range(4):
            for cp in mine_copies(chip):
                cp.wait()
            for a, (m, l) in enumerate(zip(mine(chip), landed(chip))):
                _copy(l, l, send_a, recv_a, a, chip, sibling).wait_recv()
                l[...] = m[...] + l[...]
        for k, chip in ((B_ON, diag), (B_PLAIN, first)):
            for s, l in zip(staged(k), landed(_chip_index(chip))):
                s[...] = l[...].astype(BF16)
        _rsb_send(staged(B_PLAIN), staged(B_ON), arrived(B_PLAIN), arrived(B_ON), send_b, recv_b)

    @pl.when(step == STEP_MERGE)
    def _():
        _rsb_merge(landed(_chip_index(second)), staged(B_MERGED), arrived(B_ON), arrived(B_MERGED), send_b, recv_b)
        for a in range(3):
            for k, chip in ((A_X, xn), (A_Y, yn)):
                h = slot(chip)[a]
                _copy(h, h, send_r, recv_r, a, k, sibling).wait_recv()
            h = slot(first)[a]
            _copy(h, h, send_r, recv_r, a, A_ON, (*second, c)).start()

    @pl.when(step == STEP_HALF)
    def _():
        for a, (r, hs) in enumerate(zip(results, halves)):
            h = slot(diag)[a]
            _copy(h, h, send_r, recv_r, a, A_ON, sibling).wait_recv()
            mine = r.at[half_rows(r), :]
            chip_sums = [hs[k].astype(F32) for k in range(4)]
            mine[...] = (chip_sums[0] + chip_sums[1]) + (chip_sums[2] + chip_sums[3])
            _copy(mine, mine, send_r, recv_r, a, A_HALF, sibling).start()

    @pl.when(step == n_steps - 1)
    def _():
        block = lambda k: rows(acc, k, W_IN_BLK)
        w_copy = lambda chip: _copy(st_win.at[chip], a_win.at[chip], send_w, recv_w, 0, chip, sibling)
        for chip in range(4):
            st_win[chip] = block(2 * chip + (1 - c))[...].astype(BF16)
            w_copy(chip).start()

        owout_ref[...], opww_ref[...] = _rsb_total(landed(_chip_index(own)), staged(B_PLAIN), staged(B_ON), staged(B_MERGED),
                                                   arrived(B_PLAIN), arrived(B_MERGED), send_b, recv_b)
        for a, (r, p, s) in enumerate(zip(results, parts, sibs)):
            theirs = r.at[pl.ds(pl.multiple_of((1 - c) * (r.shape[0] // 2), SUBLANES), r.shape[0] // 2), :]
            _copy(theirs, theirs, send_r, recv_r, a, A_HALF, sibling).wait_recv()
            _copy(p, s, send_r, recv_r, a, A_PART, sibling).wait_send()
            for k, h in ((A_X, slot(own)[a]), (A_Y, slot(own)[a]), (A_ON, slot(first)[a])):
                _copy(h, h, send_r, recv_r, a, k, sibling).wait_send()
            mine = r.at[half_rows(r), :]
            _copy(mine, mine, send_r, recv_r, a, A_HALF, sibling).wait_send()
        for chip in range(4):
            for a, (s, d) in enumerate(zip(blocks(2 * chip + (1 - c)), landed(chip))):
                _copy(s, d, send_a, recv_a, a, chip, sibling).wait_send()
        ovec_ref[...] = res_vec[...]
        opoolw_ref[...] = res_pw[...]
        for k in range(N_DEV):
            @pl.when(4 * x + 2 * y + c == k)
            def _():
                odw_ref[...] = res_dw[0:CONV_WIDTH, k * DW_BLK:(k + 1) * DW_BLK]

        for chip in range(4):
            w_copy(chip).wait()
        chip_sum = lambda chip: block(2 * _chip_index(chip) + c)[...] + a_win[_chip_index(chip)].astype(F32)
        win_own_ref[...] = chip_sum(own)
        for j, chip in enumerate((xn, yn, diag)):
            win_staged_ref[j] = chip_sum(chip).astype(BF16)


def _bwd2a(x, dproj, norm_g, gwout, gpww, loss8, gfg8, gvec, gdw8, gpoolw):
    bl, s, _ = x.shape
    tm = TILE_M
    nt = s // tm
    tok = lambda n: pl.BlockSpec((1, tm, n), lambda b, i: (b, i, 0))
    n_pw = len(POOL_WINDOWS) * POOL_GC
    win = (W_IN_BLK, D_MODEL)
    blk = ((W_OUT_BLK, D_MODEL), (PW_BLK, W_MIX))
    rep = ((VEC_ROWS, D_MODEL), (DW_ROWS, W_MIX), (n_pw, POOL_GC))
    half = lambda sh: (sh[0] // 2,) + sh[1:]
    dma = pltpu.SemaphoreType.DMA
    assert bl * nt > STEP_HALF + 1, "the exchanges' phases need their grid steps"
    return pl.pallas_call(
        functools.partial(_bwd2a_body, bl * nt),
        name="bwd2a",
        grid=(bl, nt),
        in_specs=[tok(D_MODEL), tok(D_IN), VMEM] + [pl.BlockSpec(memory_space=pl.ANY)] * 2 + [VMEM] * 5,
        out_specs=[VMEM] * 7,
        out_shape=[jax.ShapeDtypeStruct(win, F32), jax.ShapeDtypeStruct((3,) + win, BF16)]
        + [jax.ShapeDtypeStruct(sh, F32) for sh in blk + (rep[0], (CONV_WIDTH, DW_BLK), rep[2])],
        scratch_shapes=[pltpu.VMEM((D_IN, D_MODEL), F32), pltpu.VMEM((4,) + win, BF16), pltpu.VMEM((4,) + win, BF16)]
        + [pltpu.VMEM((4,) + sh, F32) for sh in blk] * 2
        + [pltpu.VMEM((3,) + sh, BF16) for sh in blk] * 2
        + [pltpu.VMEM(sh, F32) for sh in rep] * 3
        + [pltpu.VMEM((4,) + half(sh), dt) for sh, dt in zip(rep, HALF_DTYPES)]
        + [dma((2, 4)), dma((2, 4)), dma((2, 4)), dma((2, 3)), dma((2, 3)), dma((3, 5)), dma((3, 5)), dma((1, 4)), dma((1, 4))],
        compiler_params=pltpu.CompilerParams(dimension_semantics=("arbitrary", "arbitrary"), vmem_limit_bytes=VMEM_LIMIT),
    )(x, dproj, _row(norm_g), gwout, gpww, loss8, gfg8, gvec, gdw8, gpoolw)


HBM = pl.BlockSpec(memory_space=pltpu.HBM)
SEM = pl.BlockSpec(memory_space=pltpu.SEMAPHORE)
EFFECT = pltpu.SideEffectType.DATAFLOW_SIDE_EFFECTING


def _owner_copies(src_ref, land_ref, send_sems, recv_sems):
    x, y, c = lax.axis_index("x"), lax.axis_index("y"), lax.axis_index("c")
    owners = ((1 - x, y, c), (x, 1 - y, c), (1 - x, 1 - y, c))
    return [pltpu.make_async_remote_copy(src_ref=src_ref.at[j], dst_ref=land_ref.at[j], send_sem=send_sems.at[j],
                                         recv_sem=recv_sems.at[j], device_id=owners[j], device_id_type=MESH)
            for j in range(3)]


def _rs_start(staged):
    def body(src_ref, land_ref, send_sems, recv_sems, src_thru, land_thru, token):
        for cp in _owner_copies(src_ref, land_ref, send_sems, recv_sems):
            cp.start()
        token[...] = jnp.zeros_like(token)

    dma = pltpu.SemaphoreType.DMA
    return pl.pallas_call(
        body, name="rs_start",
        out_shape=(dma((3,)), dma((3,)), pltpu.HBM(staged.shape, staged.dtype), pltpu.HBM(staged.shape, staged.dtype),
                   jax.ShapeDtypeStruct((SUBLANES, LANES), F32)),
        in_specs=(HBM, HBM), out_specs=(SEM, SEM, HBM, HBM, VMEM), input_output_aliases={0: 2, 1: 3},
        compiler_params=pltpu.CompilerParams(has_side_effects=EFFECT),
    )(pltpu.with_memory_space_constraint(staged, pltpu.HBM),
      pltpu.with_memory_space_constraint(lax.empty(staged.shape, staged.dtype), pltpu.HBM))


def _rs_wait(send_sems, recv_sems, src_thru, land_thru, after):
    def body(src_ref, land_ref, send_sems, recv_sems, after_ref, src_dead, got_ref):
        for cp in _owner_copies(src_ref, land_ref, send_sems, recv_sems):
            cp.wait_send()
            cp.wait_recv()

    return pl.pallas_call(
        body, name="rs_wait",
        out_shape=(pltpu.HBM(src_thru.shape, src_thru.dtype), pltpu.HBM(land_thru.shape, land_thru.dtype)),
        in_specs=(HBM, HBM, SEM, SEM, pl.BlockSpec(memory_space=pl.ANY)), out_specs=(HBM, HBM),
        input_output_aliases={0: 0, 1: 1},
        compiler_params=pltpu.CompilerParams(has_side_effects=EFFECT),
    )(src_thru, land_thru, send_sems, recv_sems, after)[1]


RING = 3


def _bwd2b_body(n_steps, tiles_per_seq, x_hbm, dh_hbm, dproj_hbm, ng_ref, wint_ref, token_ref, gx_ref, gng_ref,
                x_ring, dh_ring, dproj_ring, sems):
    step = pl.program_id(0) * pl.num_programs(1) + pl.program_id(1)
    tm = TILE_M

    def tile_copies(k):
        k = jnp.asarray(k, jnp.int32)
        slot = k % RING
        rows = pl.ds(pl.multiple_of((k % tiles_per_seq) * tm, tm), tm)
        return [pltpu.make_async_copy(src.at[k // tiles_per_seq, rows, :], ring.at[slot], sems.at[j, slot])
                for j, (src, ring) in enumerate(((x_hbm, x_ring), (dh_hbm, dh_ring), (dproj_hbm, dproj_ring)))]

    @pl.when(step == 0)
    def _():
        gng_ref[...] = jnp.zeros_like(gng_ref) + token_ref[0:1, 0:1]
        for k in range(RING - 1):
            for cp in tile_copies(k):
                cp.start()

    @pl.when(step + RING - 1 < n_steps)
    def _():
        for cp in tile_copies(step + RING - 1):
            cp.start()

    for cp in tile_copies(step):
        cp.wait()
    slot = step % RING
    x_t = x_ring[slot]
    r1 = lax.rsqrt(jnp.mean(x_t * x_t, axis=-1, keepdims=True) + RMS_EPS)
    xr = x_t * r1
    dhn = jnp.dot(dproj_ring[slot], wint_ref[...], preferred_element_type=F32)
    gng_ref[...] += _colsum8(dhn * xr)
    gy = dhn * ng_ref[...]
    gx_ref[0] = dh_ring[slot] + r1 * (gy - xr * jnp.mean(gy * xr, axis=-1, keepdims=True))


def _bwd2b(x, dh, dproj, norm_g, w_in_t, token):
    bl, s, _ = x.shape
    tm = TILE_M
    nt = s // tm
    assert bl * nt >= RING
    hbm = pl.BlockSpec(memory_space=pl.ANY)
    return pl.pallas_call(
        functools.partial(_bwd2b_body, bl * nt, nt),
        name="bwd2b",
        grid=(bl, nt),
        in_specs=[hbm, hbm, hbm, VMEM, VMEM, VMEM],
        out_specs=[pl.BlockSpec((1, tm, D_MODEL), lambda b, i: (b, i, 0)),
                   pl.BlockSpec((SUBLANES, D_MODEL), lambda b, i: (0, 0))],
        out_shape=[jax.ShapeDtypeStruct((bl, s, D_MODEL), F32), jax.ShapeDtypeStruct((SUBLANES, D_MODEL), F32)],
        scratch_shapes=[pltpu.VMEM((RING, tm, D_MODEL), F32), pltpu.VMEM((RING, tm, D_MODEL), F32),
                        pltpu.VMEM((RING, tm, D_IN), BF16), pltpu.SemaphoreType.DMA((3, RING))],
        compiler_params=pltpu.CompilerParams(dimension_semantics=("arbitrary", "arbitrary"), vmem_limit_bytes=VMEM_LIMIT),
    )(x, dh, dproj, _row(norm_g), w_in_t, token)


def _partial_copies(part_ref, land_ref, send_sems, recv_sems):
    x, y, c = lax.axis_index("x"), lax.axis_index("y"), lax.axis_index("c")
    other_chips = [(1 - x, y), (x, 1 - y), (1 - x, 1 - y)]
    peers = [(x, y, 1 - c)] + [(*chip, c) for chip in other_chips] + [(*chip, 1 - c) for chip in other_chips]
    return [pltpu.make_async_remote_copy(src_ref=part_ref, dst_ref=land_ref.at[4 * x + 2 * y + c], send_sem=send_sems.at[r],
                                         recv_sem=recv_sems.at[r], device_id=peers[r], device_id_type=MESH)
            for r in range(N_PEER)]


def _partials_start(part):
    def body(part_ref, land_ref, send_sems, recv_sems, part_thru, land_thru):
        for cp in _partial_copies(part_ref, land_ref, send_sems, recv_sems):
            cp.start()

    dma = pltpu.SemaphoreType.DMA
    land = (N_DEV,) + part.shape
    return pl.pallas_call(
        body, name="partials_start",
        out_shape=(dma((N_PEER,)), dma((N_PEER,)), pltpu.HBM(part.shape, part.dtype), pltpu.HBM(land, part.dtype)),
        in_specs=(HBM, HBM), out_specs=(SEM, SEM, HBM, HBM), input_output_aliases={0: 2, 1: 3},
        compiler_params=pltpu.CompilerParams(has_side_effects=EFFECT),
    )(pltpu.with_memory_space_constraint(part, pltpu.HBM),
      pltpu.with_memory_space_constraint(lax.empty(land, part.dtype), pltpu.HBM))


def _partials_wait(send_sems, recv_sems, part_thru, land_thru, after):
    def body(part_ref, land_ref, send_sems, recv_sems, after_ref, part_out, land_out):
        for cp in _partial_copies(part_ref, land_ref, send_sems, recv_sems):
            cp.wait_send()
            cp.wait_recv()

    return pl.pallas_call(
        body, name="partials_wait",
        out_shape=(pltpu.HBM(part_thru.shape, part_thru.dtype), pltpu.HBM(land_thru.shape, land_thru.dtype)),
        in_specs=(HBM, HBM, SEM, SEM, pl.BlockSpec(memory_space=pl.ANY)), out_specs=(HBM, HBM),
        input_output_aliases={0: 0, 1: 1},
        compiler_params=pltpu.CompilerParams(has_side_effects=EFFECT),
    )(part_thru, land_thru, send_sems, recv_sems, after)


ADAM_LR = 0.001
ADAM_B1 = 0.9
ADAM_B2 = 0.999
ADAM_EPS = 1e-08
ADAM_WD = 0.01
ADAM_STEP = 10


PARAMS = ("norm_g", "w_in", "pool_w", "pool_b", "pool_scale", "conv_dw", "conv_b", "ln_g", "ln_b", "pw_w", "pw_b", "w_out",
          "final_g")


LATE_PARAMS = ("norm_g",)
MAIN_PARAMS = tuple(p for p in PARAMS if p not in LATE_PARAMS)


def _adamw_refs(params, n_sources, refs):
    n = len(params)
    ws, ms, vs = (dict(zip(params, refs[k * n:(k + 1) * n])) for k in range(3))
    outs = refs[3 * n + n_sources:]
    gs, ds, m2s, v2s = (dict(zip(params, outs[k * n:(k + 1) * n])) for k in range(4))

    def update(name, g, at=slice(None)):
        m = ADAM_B1 * ms[name][at] + (1.0 - ADAM_B1) * g
        v = ADAM_B2 * vs[name][at] + (1.0 - ADAM_B2) * (g * g)
        m_hat = m / (1.0 - ADAM_B1 ** ADAM_STEP)
        v_hat = v / (1.0 - ADAM_B2 ** ADAM_STEP)
        gs[name][at] = g
        ds[name][at] = -ADAM_LR * (m_hat / (jnp.sqrt(v_hat) + ADAM_EPS) + ADAM_WD * ws[name][at])
        m2s[name][at] = m
        v2s[name][at] = v

    return update, refs[3 * n:3 * n + n_sources], outs[4 * n:]


def _adamw_late_body(*refs):
    update, (gng_ref, gng_others_ref), _ = _adamw_refs(LATE_PARAMS, 2, refs)
    me = 4 * lax.axis_index("x") + 2 * lax.axis_index("y") + lax.axis_index("c")
    g_norm = jnp.zeros(gng_ref.shape, F32)
    for k in range(N_DEV):
        g_norm = g_norm + jnp.where(me == k, gng_ref[...], gng_others_ref[k])
    update("norm_g", jnp.sum(g_norm, axis=0, keepdims=True))


def _adamw_body(*refs):
    update, sources, (loss_ref,) = _adamw_refs(MAIN_PARAMS, 7, refs)
    gwin_ref, gwin_others_ref, gwout_ref, gpww_ref, vec_ref, gdw_ref, gpoolw_ref = sources
    ws = dict(zip(MAIN_PARAMS, refs))
    row = lambda r, lo, hi: vec_ref[r:r + 1, lo:hi]
    loss_ref[...] = row(R_LOSS, 0, 1)
    g_w_in = gwin_ref[...]
    for j in range(3):
        g_w_in = g_w_in + gwin_others_ref[j].astype(F32)
    update("w_in", g_w_in)
    update("pool_w", gpoolw_ref[...].reshape(ws["pool_w"].shape))
    for g in range(len(POOL_WINDOWS)):
        update("pool_b", row(R_POOL, g * POOL_GC, (g + 1) * POOL_GC), at=slice(g, g + 1))
    update("pool_scale", row(R_POOL, W_MIX, 2 * W_MIX))
    update("conv_dw", gdw_ref[...])
    update("conv_b", row(R_CONV, 0, W_MIX))
    update("ln_g", row(R_CONV, W_MIX, 2 * W_MIX))
    update("ln_b", row(R_LN, 0, W_MIX))
    update("pw_w", gpww_ref[...])
    update("pw_b", row(R_LN, W_MIX, 2 * W_MIX))
    update("w_out", gwout_ref[...])
    update("final_g", row(R_FINAL_G, 0, D_MODEL))


def _adamw(body, name, params, ws, ms, vs, grad_sources, extra_out_shapes=()):
    n = len(params)
    flat = lambda a: a.reshape(1, -1) if a.ndim == 1 else a
    args = [flat(d[p]) for d in (ws, ms, vs) for p in params]
    outs = pl.pallas_call(
        body,
        name=name,
        in_specs=[VMEM] * (3 * n + len(grad_sources)),
        out_specs=[VMEM] * (4 * n + len(extra_out_shapes)),
        out_shape=[jax.ShapeDtypeStruct(a.shape, F32) for a in args[:n]] * 4 + list(extra_out_shapes),
        compiler_params=pltpu.CompilerParams(vmem_limit_bytes=VMEM_LIMIT),
    )(*args, *grad_sources)
    groups = tuple({p: outs[k * n + j].reshape(ws[p].shape) for j, p in enumerate(params)} for k in range(4))
    return groups, outs[4 * n:]


def kernel(x, norm_g, w_in, pool_w, pool_b, pool_scale, conv_dw, conv_b, ln_g, ln_b, pw_w, pw_b, w_out, final_g, loss_target, m_norm_g, m_w_in, m_pool_w, m_pool_b, m_pool_scale, m_conv_dw, m_conv_b, m_ln_g, m_ln_b, m_pw_w, m_pw_b, m_w_out, m_final_g, v_norm_g, v_w_in, v_pool_w, v_pool_b, v_pool_scale, v_conv_dw, v_conv_b, v_ln_g, v_ln_b, v_pw_w, v_pw_b, v_w_out, v_final_g):
    proj, w_in_t, w_out_b, pw_w_b, conv_dw_f = _inproj(x, norm_g, w_in.T, w_out, pw_w, conv_dw)
    h1, z, ob, dpool, dh, loss8, gfg8 = _fwd(x, loss_target, proj, pool_w, pool_b, pool_scale, conv_dw_f,
                                             conv_b, ln_g, ln_b, pw_w_b, pw_b, w_out_b, final_g)
    dproj, gwout, gpww, gpoolw, gvec, gdw8 = _bwd1(dh, proj, h1, z, ob, dpool, w_out_b, pw_w_b, pool_w, pool_scale,
                                                   conv_dw_f, ln_g, ln_b)
    win_own, win_staged, g_w_out, g_pw_w, vec, g_conv_dw, poolw_sum = _bwd2a(x, dproj, norm_g, gwout, gpww, loss8, gfg8, gvec,
                                                                             gdw8, gpoolw)
    send_sems, recv_sems, staged_thru, land_thru, token = _rs_start(win_staged)
    grad_x, gng8 = _bwd2b(x, dh, dproj, norm_g, w_in_t, token)
    ng_send, ng_recv, gng8, gng_land = _partials_start(gng8)
    win_others = _rs_wait(send_sems, recv_sems, staged_thru, land_thru, gng8)

    weights = dict(norm_g=norm_g, w_in=w_in.T, pool_w=pool_w, pool_b=pool_b, pool_scale=pool_scale, conv_dw=conv_dw, conv_b=conv_b,
                   ln_g=ln_g, ln_b=ln_b, pw_w=pw_w, pw_b=pw_b, w_out=w_out, final_g=final_g)
    m_in = dict(norm_g=m_norm_g, w_in=m_w_in.T, pool_w=m_pool_w, pool_b=m_pool_b, pool_scale=m_pool_scale, conv_dw=m_conv_dw,
                conv_b=m_conv_b, ln_g=m_ln_g, ln_b=m_ln_b, pw_w=m_pw_w, pw_b=m_pw_b, w_out=m_w_out, final_g=m_final_g)
    v_in = dict(norm_g=v_norm_g, w_in=v_w_in.T, pool_w=v_pool_w, pool_b=v_pool_b, pool_scale=v_pool_scale, conv_dw=v_conv_dw,
                conv_b=v_conv_b, ln_g=v_ln_g, ln_b=v_ln_b, pw_w=v_pw_w, pw_b=v_pw_b, w_out=v_w_out, final_g=v_final_g)
    main, (loss,) = _adamw(functools.partial(_adamw_body), "adamw", MAIN_PARAMS, weights, m_in, v_in,
                           (win_own, win_others, g_w_out, g_pw_w, vec, g_conv_dw, poolw_sum), [jax.ShapeDtypeStruct((1, 1), F32)])
    gng8, gng_others = _partials_wait(ng_send, ng_recv, gng8, gng_land, main[0]["w_out"])
    late, _ = _adamw(functools.partial(_adamw_late_body), "adamw_norm_g", LATE_PARAMS, weights, m_in, v_in, (gng8, gng_others))
    groups = [[(late[k] if p in LATE_PARAMS else main[k])[p] for p in PARAMS] for k in range(4)]
    k_in = PARAMS.index("w_in")
    for group in groups:
        group[k_in] = group[k_in].T
    grads, delta, new_m, new_v = groups
    return (loss.reshape(()), grad_x, *grads, *delta, *new_m, *new_v)
```

```python
import functools

import jax
import jax.numpy as jnp
from jax import lax
from jax.experimental import pallas as pl
from jax.experimental.pallas import tpu as pltpu

F32 = jnp.float32
BF16 = jnp.bfloat16

D_MODEL = 1024
W_MIX = 512
D_IN = 5 * W_MIX
POOL_WINDOWS = (2, 4, 8, 16)
POOL_GC = 128
CONV_WIDTH = 31
RMS_EPS = 1e-6
LN_EPS = 1e-5
N_DEV = 8

LANES = 128
SUBLANES = 8
TILE_M = 512
TILE_IN = 1024
POOL_HALO = 16
CONV_HALO = 32
CONV_ROWS = 128
BWD_CONV_ROWS = 64
BWD_SUBTILES = 2
VMEM_LIMIT = 56 * 1024 * 1024

VMEM = pl.BlockSpec(memory_space=pltpu.VMEM)


def _silu(v):
    return v * jax.nn.sigmoid(v)


def _colsum8(v):
    m, n = v.shape
    return jnp.sum(v.reshape(m // SUBLANES, SUBLANES, n), axis=0)


def _row_index(shape, t0):
    return lax.broadcasted_iota(jnp.int32, shape, 0) + t0


def _conv_taps():
    taps = {s: [] for s in range(SUBLANES)}
    for k in range(CONV_WIDTH):
        e = k + CONV_HALO - (CONV_WIDTH - 1)
        taps[e % SUBLANES].append((e // SUBLANES, k))
    return taps


def _fwd_body(x_ref, tgt_ref, proj_ref, poolw_ref, poolb_ref, pools_ref, dw_ref, cb_ref, lg_ref, lb_ref,
              pww_ref, pwb_ref, wout_ref, fg_ref,
              h1_ref, z_ref, ob_ref, dpool_ref, dh_ref, loss_ref, gfg_ref,
              uext, hext, y_s):
    b = pl.program_id(0)
    i = pl.program_id(1)
    tm = TILE_M

    @pl.when(i == 0)
    def _():
        uext[0:POOL_HALO, :] = jnp.zeros((POOL_HALO, W_MIX), F32)
        hext[0:CONV_HALO, :] = jnp.zeros((CONV_HALO, W_MIX), F32)
        hext[CONV_HALO + tm:, :] = jnp.zeros((SUBLANES, W_MIX), F32)

    @pl.when((b == 0) & (i == 0))
    def _():
        loss_ref[...] = jnp.zeros_like(loss_ref)
        gfg_ref[...] = jnp.zeros_like(gfg_ref)

    uext[POOL_HALO:, :] = proj_ref[0, :, 0:W_MIX]
    t = _row_index((tm, POOL_GC), i * tm)
    for g, w in enumerate(POOL_WINDOWS):
        lanes = slice(g * POOL_GC, (g + 1) * POOL_GC)
        e = uext[:, lanes]
        s = e
        sh = 1
        while sh < w:
            s = s + pltpu.roll(s, sh, 0)
            sh *= 2
        inv_cnt = 1.0 / jnp.minimum(t + 1, w).astype(F32)
        d = (s[POOL_HALO:] * inv_cnt - e[POOL_HALO:]).astype(BF16)
        dpool_ref[0, :, lanes] = d
        z = jnp.dot(d, poolw_ref[g].astype(BF16), preferred_element_type=F32) + poolb_ref[:, lanes]
        z_ref[0, :, lanes] = z
        a_gate = proj_ref[0, :, W_MIX + g * POOL_GC:W_MIX + (g + 1) * POOL_GC]
        y_s[:, lanes] = (z * pools_ref[:, lanes] * _silu(a_gate)).astype(BF16)

    hext[CONV_HALO:CONV_HALO + tm, :] = proj_ref[0, :, 2 * W_MIX:3 * W_MIX] * jax.nn.sigmoid(proj_ref[0, :, 3 * W_MIX:4 * W_MIX])
    taps = _conv_taps()
    for c0 in range(0, tm, CONV_ROWS):
        for l0 in range(0, W_MIX, LANES):
            lanes = slice(l0, l0 + LANES)
            acc = jnp.zeros((CONV_ROWS, LANES), F32) + cb_ref[:, lanes]
            for s in range(SUBLANES):
                part = jnp.zeros((CONV_ROWS + SUBLANES, LANES), F32)
                for q, k in taps[s]:
                    r0 = c0 + SUBLANES * q
                    part = part + hext[r0:r0 + CONV_ROWS + SUBLANES, lanes] * dw_ref[k:k + 1, lanes]
                acc = acc + part[s:s + CONV_ROWS]
            h1_ref[0, c0:c0 + CONV_ROWS, lanes] = acc
    h1 = h1_ref[0]
    mu = jnp.mean(h1, axis=-1, keepdims=True)
    xc = h1 - mu
    rstd = lax.rsqrt(jnp.mean(xc * xc, axis=-1, keepdims=True) + LN_EPS)
    h3 = _silu(xc * rstd * lg_ref[...] + lb_ref[...])
    ob = jnp.dot(h3.astype(BF16), pww_ref[...], preferred_element_type=F32) + pwb_ref[...]
    ob_ref[0] = ob
    y_s[:, W_MIX:] = (ob * _silu(proj_ref[0, :, 4 * W_MIX:5 * W_MIX])).astype(BF16)

    h = x_ref[0] + jnp.dot(y_s[...], wout_ref[...], preferred_element_type=F32)
    r2 = lax.rsqrt(jnp.mean(h * h, axis=-1, keepdims=True) + RMS_EPS)
    hr = h * r2
    err = hr * fg_ref[...] - tgt_ref[0]
    loss_ref[...] += jnp.sum(err * err) * (0.5 / D_MODEL)
    dout = err * (1.0 / D_MODEL)
    gfg_ref[...] += _colsum8(dout * hr)
    gy = dout * fg_ref[...]
    dh_ref[0] = r2 * (gy - hr * jnp.mean(gy * hr, axis=-1, keepdims=True))

    uext[0:POOL_HALO, :] = uext[tm:tm + POOL_HALO, :]
    hext[0:CONV_HALO, :] = hext[tm:tm + CONV_HALO, :]


def _row(v):
    return v.reshape(1, -1)


def _fwd(x, tgt, proj, pool_w, pool_b, pool_scale, conv_dw_f, conv_b, ln_g, ln_b, pw_w_b, pw_b, w_out_b, final_g):
    bl, s, _ = x.shape
    tm = TILE_M
    tok = lambda n: pl.BlockSpec((1, tm, n), lambda b, i: (b, i, 0))
    acc = lambda n: pl.BlockSpec((SUBLANES, n), lambda b, i: (0, 0))
    act = lambda n: jax.ShapeDtypeStruct((bl, s, n), F32)
    return pl.pallas_call(
        functools.partial(_fwd_body),
        name="fwd",
        grid=(bl, s // tm),
        in_specs=[tok(D_MODEL), tok(D_MODEL), tok(D_IN)] + [VMEM] * 11,
        out_specs=[tok(W_MIX), tok(W_MIX), tok(W_MIX), tok(W_MIX), tok(D_MODEL), acc(LANES), acc(D_MODEL)],
        out_shape=[act(W_MIX), act(W_MIX), act(W_MIX), jax.ShapeDtypeStruct((bl, s, W_MIX), BF16), act(D_MODEL),
                   jax.ShapeDtypeStruct((SUBLANES, LANES), F32), jax.ShapeDtypeStruct((SUBLANES, D_MODEL), F32)],
        scratch_shapes=[pltpu.VMEM((POOL_HALO + tm, W_MIX), F32), pltpu.VMEM((CONV_HALO + tm + SUBLANES, W_MIX), F32),
                        pltpu.VMEM((tm, D_MODEL), BF16)],
        compiler_params=pltpu.CompilerParams(dimension_semantics=("arbitrary", "arbitrary"), vmem_limit_bytes=VMEM_LIMIT),
    )(x, tgt, proj, pool_w, pool_b.reshape(1, W_MIX), _row(pool_scale), conv_dw_f, _row(conv_b),
      _row(ln_g), _row(ln_b), pw_w_b, _row(pw_b), w_out_b, _row(final_g))


NT_DIMS = (((1,), (1,)), ((), ()))
TN_DIMS = (((0,), (0,)), ((), ()))
V_POOL_B, V_POOL_SCALE, V_CONV_B, V_LN_G, V_LN_B, V_PW_B = range(6)


def _bwd_taps():
    taps = {s: [] for s in range(SUBLANES)}
    for k in range(CONV_WIDTH):
        e = CONV_WIDTH - 1 - k
        taps[e % SUBLANES].append((e // SUBLANES, k))
    return taps


def _bwd1_body(n_tiles, dh_ref, proj_ref, h1_ref, z_ref, ob_ref, dpool_ref, wout_ref, pww_ref, poolw_ref, pools_ref, dw_ref,
               lg_ref, lb_ref,
               dproj_ref, gwout_ref, gpww_ref, gpoolw_ref, gvec_ref, gdw_ref,
               vext, gext, gsh, y_s):
    b = pl.program_id(0)
    i = pl.program_id(1)
    tm = TILE_M
    t0 = (n_tiles - 1 - i) * tm

    @pl.when(i == 0)
    def _():
        vext[tm:, :] = jnp.zeros((POOL_HALO, W_MIX), F32)
        gext[tm:, :] = jnp.zeros((CONV_HALO, W_MIX), F32)

    @pl.when((b == 0) & (i == 0))
    def _():
        gwout_ref[...] = jnp.zeros_like(gwout_ref)
        gpww_ref[...] = jnp.zeros_like(gpww_ref)
        gpoolw_ref[...] = jnp.zeros_like(gpoolw_ref)
        gvec_ref[...] = jnp.zeros_like(gvec_ref)
        gdw_ref[...] = jnp.zeros_like(gdw_ref)

    sub = tm // BWD_SUBTILES

    def dense_phase(r0):
        rs = slice(r0, r0 + sub)
        dhb = dh_ref[0, rs, :].astype(BF16)
        dy_a = lax.dot_general(dhb, wout_ref[0:W_MIX, :], NT_DIMS, preferred_element_type=F32)
        dy_b = lax.dot_general(dhb, wout_ref[W_MIX:, :], NT_DIMS, preferred_element_type=F32)

        a_gate = proj_ref[0, rs, W_MIX:2 * W_MIX]
        sig_a = jax.nn.sigmoid(a_gate)
        silu_a = a_gate * sig_a
        z = z_ref[0, rs, :]
        out_a = z * pools_ref[...]
        y_s[rs, 0:W_MIX] = (out_a * silu_a).astype(BF16)
        d_out_a = dy_a * silu_a
        d_ag = dy_a * out_a * (sig_a * (1.0 + a_gate * (1.0 - sig_a)))
        dproj_ref[0, rs, W_MIX:2 * W_MIX] = d_ag.astype(BF16)
        gvec_ref[V_POOL_SCALE] += _colsum8(d_out_a * z)
        dz = d_out_a * pools_ref[...]
        gvec_ref[V_POOL_B] += _colsum8(dz)
        t = _row_index((sub, POOL_GC), t0 + r0)
        n_ext = sub + POOL_HALO
        for g, w in enumerate(POOL_WINDOWS):
            lanes = slice(g * POOL_GC, (g + 1) * POOL_GC)
            dzg = dz[:, lanes].astype(BF16)
            gpoolw_ref[g] += lax.dot_general(dpool_ref[0, rs, lanes], dzg, TN_DIMS, preferred_element_type=F32)
            dd = lax.dot_general(dzg, poolw_ref[g].astype(BF16), NT_DIMS, preferred_element_type=F32)
            vext[rs, lanes] = dd * (1.0 / jnp.minimum(t + 1, w).astype(F32))
            s = vext[r0:r0 + n_ext, lanes]
            sh = 1
            while sh < w:
                s = s + pltpu.roll(s, n_ext - sh, 0)
                sh *= 2
            dproj_ref[0, rs, lanes] = (s[0:sub] - dd).astype(BF16)

        b_gate = proj_ref[0, rs, 4 * W_MIX:5 * W_MIX]
        sig_b = jax.nn.sigmoid(b_gate)
        silu_b = b_gate * sig_b
        ob = ob_ref[0, rs, :]
        y_s[rs, W_MIX:] = (ob * silu_b).astype(BF16)
        d_ob = dy_b * silu_b
        dproj_ref[0, rs, 4 * W_MIX:5 * W_MIX] = (dy_b * ob * (sig_b * (1.0 + b_gate * (1.0 - sig_b)))).astype(BF16)
        gvec_ref[V_PW_B] += _colsum8(d_ob)
        gwout_ref[...] += lax.dot_general(y_s[rs, :], dhb, TN_DIMS, preferred_element_type=F32)

        h1 = h1_ref[0, rs, :]
        mu = jnp.mean(h1, axis=-1, keepdims=True)
        xc = h1 - mu
        rstd = lax.rsqrt(jnp.mean(xc * xc, axis=-1, keepdims=True) + LN_EPS)
        xhat = xc * rstd
        h2 = xhat * lg_ref[...] + lb_ref[...]
        sig2 = jax.nn.sigmoid(h2)
        d_obb = d_ob.astype(BF16)
        gpww_ref[...] += lax.dot_general((h2 * sig2).astype(BF16), d_obb, TN_DIMS, preferred_element_type=F32)
        dh2 = lax.dot_general(d_obb, pww_ref[...], NT_DIMS, preferred_element_type=F32) * (sig2 * (1.0 + h2 * (1.0 - sig2)))
        gvec_ref[V_LN_G] += _colsum8(dh2 * xhat)
        gvec_ref[V_LN_B] += _colsum8(dh2)
        dxh = dh2 * lg_ref[...]
        dh1 = rstd * (dxh - jnp.mean(dxh, axis=-1, keepdims=True) - xhat * jnp.mean(dxh * xhat, axis=-1, keepdims=True))
        gvec_ref[V_CONV_B] += _colsum8(dh1)
        gext[rs, :] = dh1

    taps = _bwd_taps()
    cr = BWD_CONV_ROWS
    win = cr + CONV_HALO - SUBLANES

    def conv_phase(r0):
        for c0 in range(r0, r0 + sub, cr):
            for l0 in range(0, W_MIX, LANES):
                lanes = slice(l0, l0 + LANES)
                b_val = proj_ref[0, c0:c0 + cr, 2 * W_MIX + l0:2 * W_MIX + l0 + LANES]
                sg = jax.nn.sigmoid(proj_ref[0, c0:c0 + cr, 3 * W_MIX + l0:3 * W_MIX + l0 + LANES])
                h0c = b_val * sg
                acc = jnp.zeros((cr, LANES), F32)
                for s in range(SUBLANES):
                    gsh[...] = gext[c0 + s:c0 + s + win, lanes]
                    for q, k in taps[s]:
                        gq = gsh[SUBLANES * q:SUBLANES * q + cr, :]
                        acc = acc + gq * dw_ref[k:k + 1, lanes]
                        gdw_ref[k, :, lanes] += _colsum8(h0c * gq)
                d_bval = acc * sg
                dproj_ref[0, c0:c0 + cr, 2 * W_MIX + l0:2 * W_MIX + l0 + LANES] = d_bval.astype(BF16)
                dproj_ref[0, c0:c0 + cr, 3 * W_MIX + l0:3 * W_MIX + l0 + LANES] = (d_bval * b_val * (1.0 - sg)).astype(BF16)

    starts = [k * sub for k in reversed(range(BWD_SUBTILES))]
    for r0 in starts:
        dense_phase(r0)
    for r0 in starts:
        conv_phase(r0)
    vext[tm:, :] = vext[0:POOL_HALO, :]
    gext[tm:, :] = gext[0:CONV_HALO, :]


def _bwd1(dh, proj, h1, z, ob, dpool, w_out_b, pw_w_b, pool_w, pool_scale, conv_dw_f, ln_g, ln_b):
    bl, s, _ = dh.shape
    tm = TILE_M
    nt = s // tm
    tok = lambda n: pl.BlockSpec((1, tm, n), lambda b, i: (b, nt - 1 - i, 0))
    res = lambda shape: pl.BlockSpec(shape, lambda b, i: (0,) * len(shape))
    out_shapes = [(D_MODEL, D_MODEL), (W_MIX, W_MIX), (len(POOL_WINDOWS), POOL_GC, POOL_GC), (6, SUBLANES, W_MIX),
                  (CONV_WIDTH, SUBLANES, W_MIX)]
    return pl.pallas_call(
        functools.partial(_bwd1_body, nt),
        name="bwd1",
        grid=(bl, nt),
        in_specs=[tok(D_MODEL), tok(D_IN), tok(W_MIX), tok(W_MIX), tok(W_MIX), tok(W_MIX)] + [VMEM] * 7,
        out_specs=[tok(D_IN)] + [res(sh) for sh in out_shapes],
        out_shape=[jax.ShapeDtypeStruct((bl, s, D_IN), BF16)] + [jax.ShapeDtypeStruct(sh, F32) for sh in out_shapes],
        scratch_shapes=[pltpu.VMEM((tm + POOL_HALO, W_MIX), F32), pltpu.VMEM((tm + CONV_HALO, W_MIX), F32),
                        pltpu.VMEM((BWD_CONV_ROWS + CONV_HALO - SUBLANES, LANES), F32), pltpu.VMEM((tm, D_MODEL), BF16)],
        compiler_params=pltpu.CompilerParams(dimension_semantics=("arbitrary", "arbitrary"), vmem_limit_bytes=VMEM_LIMIT),
    )(dh, proj, h1, z, ob, dpool, w_out_b, pw_w_b, pool_w, _row(pool_scale), conv_dw_f, _row(ln_g), _row(ln_b))


MESH = pl.DeviceIdType.MESH
W_IN_BLK = D_IN // N_DEV
W_OUT_BLK = D_MODEL // N_DEV
PW_BLK = W_MIX // N_DEV
DW_BLK = W_MIX // N_DEV
N_PEER = N_DEV - 1


SLAB = 2 * W_IN_BLK
N_SLABS = 4
F_SIB, F_X, F_Y, F_ON, F_PASS_X, F_PASS_Y, F_PASS_D = range(7)
W_IN_ONLY, OTHERS = (0,), (1, 2, 3)


def _inproj_body(n_tiles, x_hbm, ng_ref, wint_ref, wout_ref, pww_ref, dw_ref,
                 proj_hbm, wt_hbm, woutf_hbm, pwwf_hbm, dwf_ref,
                 hn_s, wt_s, wout_s, pww_s, dw_blk, stage, xbuf, out_sems, x_sems, w_sems, send_sems, recv_sems):
    s = pl.program_id(0)
    i = pl.program_id(1)
    tm = TILE_IN
    x, y, c = lax.axis_index("x"), lax.axis_index("y"), lax.axis_index("c")
    sibling = (x, y, 1 - c)
    xn, yn, dg = (1 - x, y), (x, 1 - y), (1 - x, 1 - y)
    pick = lambda a, b: tuple(jnp.where(c == 1, p, q) for p, q in zip(a, b))
    on_from, on_to = pick(xn, yn), pick(yn, xn)

    def blocks(px, py, pc):
        k = 4 * px + 2 * py + pc
        rows = lambda ref, n: ref.at[pl.ds(pl.multiple_of(k * n, n), n), :]
        return (rows(wt_s, W_IN_BLK), rows(wout_s, W_OUT_BLK), rows(pww_s, PW_BLK), dw_blk.at[k])

    def copies(arrays, flow, block, to):
        refs = blocks(*block)
        return [pltpu.make_async_remote_copy(src_ref=refs[a], dst_ref=refs[a], send_sem=send_sems.at[a, flow],
                                             recv_sem=recv_sems.at[a, flow], device_id=to, device_id_type=MESH)
                for a in arrays]

    def start(arrays):
        for flow, to in ((F_SIB, sibling), (F_X, (*xn, c)), (F_Y, (*yn, c))):
            for cp in copies(arrays, flow, (x, y, c), to):
                cp.start()

    def direct(arrays):
        for flow, chip in ((F_X, xn), (F_Y, yn)):
            for cp in copies(arrays, flow, (*chip, c), sibling):
                cp.wait_recv()
        for cp in copies(arrays, F_ON, (*on_from, c), (*on_to, c)):
            cp.start()
        for flow, chip in ((F_PASS_X, xn), (F_PASS_Y, yn)):
            for cp in copies(arrays, flow, (*chip, c), sibling):
                cp.start()

    def diagonal(arrays):
        for cp in copies(arrays, F_ON, (*dg, c), sibling):
            cp.wait_recv()
        for cp in copies(arrays, F_PASS_D, (*dg, c), sibling):
            cp.start()

    def from_sibling(arrays, flow, chip):
        for cp in copies(arrays, flow, (*chip, 1 - c), sibling):
            cp.wait_recv()

    def finish(arrays):
        mine = (x, y, c)
        for flow, block in ((F_SIB, mine), (F_X, mine), (F_Y, mine), (F_ON, (*on_from, c)), (F_PASS_X, (*xn, c)),
                            (F_PASS_Y, (*yn, c)), (F_PASS_D, (*dg, c))):
            for cp in copies(arrays, flow, block, sibling):
                cp.wait_send()

    step = s * n_tiles + i
    par = step % 2
    first = step == 0

    @pl.when(first)
    def _():
        mine = blocks(x, y, c)
        mine[0][...] = wint_ref[...].astype(BF16)
        mine[1][...] = wout_ref[...].astype(BF16)
        mine[2][...] = pww_ref[...].astype(BF16)
        mine[3][...] = dw_ref[...]
        start(W_IN_ONLY)
        start(OTHERS)

    rows_i = pl.ds(pl.multiple_of(i * tm, tm), tm)
    tiles_per_seq = proj_hbm.shape[1] // tm

    def x_copy(k):
        src = x_hbm.at[k // tiles_per_seq, pl.ds(pl.multiple_of((k % tiles_per_seq) * tm, tm), tm), :]
        return pltpu.make_async_copy(src, xbuf.at[k % 2], x_sems.at[k % 2])

    @pl.when(first)
    def _():
        x_copy(0).start()

    @pl.when(s == 0)
    def _():
        x_copy(i).wait()

        @pl.when(i + 1 < n_tiles)
        def _():
            x_copy(i + 1).start()

        xt = xbuf[i % 2]
        r1 = lax.rsqrt(jnp.mean(xt * xt, axis=-1, keepdims=True) + RMS_EPS)
        hn_s[rows_i, :] = (xt * r1 * ng_ref[...]).astype(BF16)

    @pl.when(first)
    def _():
        from_sibling(W_IN_ONLY, F_SIB, (x, y))

    @pl.when((s == 1) & (i == 0))
    def _():
        direct(W_IN_ONLY)
        from_sibling(W_IN_ONLY, F_PASS_Y, yn)

    @pl.when((s == 2) & (i == 0))
    def _():
        from_sibling(W_IN_ONLY, F_PASS_X, xn)

    weights_out = [pltpu.make_async_copy(src, dst, w_sems.at[k])
                   for k, (src, dst) in enumerate(((wt_s, wt_hbm), (wout_s, woutf_hbm), (pww_s, pwwf_hbm)))]

    @pl.when((s == 3) & (i == 0))
    def _():
        diagonal(W_IN_ONLY)
        direct(OTHERS)
        from_sibling(W_IN_ONLY, F_PASS_D, dg)
        weights_out[0].start()

    def out_copy(parity, col0):
        dst = proj_hbm.at[i // tiles_per_seq, pl.ds(pl.multiple_of((i % tiles_per_seq) * tm, tm), tm),
                          pl.ds(pl.multiple_of(col0, LANES), SLAB)]
        return pltpu.make_async_copy(stage.at[parity], dst, out_sems.at[parity])

    @pl.when(step >= 2)
    def _():
        out_copy(par, 0).wait()

    chip = jnp.where(s < 2, 2 * x, 2 * (1 - x)) + jnp.where((s == 0) | (s == 2), y, 1 - y)
    slab = wt_s[pl.ds(pl.multiple_of(chip * SLAB, SLAB), SLAB), :]
    stage[par] = lax.dot_general(hn_s[rows_i, :], slab, NT_DIMS, preferred_element_type=F32)
    out_copy(par, chip * SLAB).start()

    @pl.when(step == N_SLABS * n_tiles - 1)
    def _():
        out_copy(par, 0).wait()
        out_copy(1 - par, 0).wait()
        from_sibling(OTHERS, F_SIB, (x, y))
        from_sibling(OTHERS, F_PASS_X, xn)
        from_sibling(OTHERS, F_PASS_Y, yn)
        diagonal(OTHERS)
        from_sibling(OTHERS, F_PASS_D, dg)
        for cp in weights_out[1:]:
            cp.start()
        finish(W_IN_ONLY)
        finish(OTHERS)
        for cp in weights_out:
            cp.wait()
        for k in range(N_DEV):
            dwf_ref[:, k * DW_BLK:(k + 1) * DW_BLK] = dw_blk[k]


def _inproj(x, norm_g, w_in_t, w_out, pw_w, conv_dw):
    bl, s, _ = x.shape
    tm = TILE_IN
    nt = s // tm
    n_tiles = bl * nt
    dma = pltpu.SemaphoreType.DMA
    return pl.pallas_call(
        functools.partial(_inproj_body, n_tiles),
        name="inproj",
        grid=(N_SLABS, n_tiles),
        in_specs=[pl.BlockSpec(memory_space=pl.ANY)] + [VMEM] * 5,
        out_specs=[pl.BlockSpec(memory_space=pl.ANY)] * 4 + [VMEM],
        out_shape=[jax.ShapeDtypeStruct((bl, s, D_IN), F32), jax.ShapeDtypeStruct((D_IN, D_MODEL), BF16),
                   jax.ShapeDtypeStruct((D_MODEL, D_MODEL), BF16), jax.ShapeDtypeStruct((W_MIX, W_MIX), BF16),
                   jax.ShapeDtypeStruct((CONV_WIDTH, W_MIX), F32)],
        scratch_shapes=[pltpu.VMEM((n_tiles * tm, D_MODEL), BF16), pltpu.VMEM((D_IN, D_MODEL), BF16),
                        pltpu.VMEM((D_MODEL, D_MODEL), BF16), pltpu.VMEM((W_MIX, W_MIX), BF16),
                        pltpu.VMEM((N_DEV, CONV_WIDTH, DW_BLK), F32), pltpu.VMEM((2, tm, SLAB), F32),
                        pltpu.VMEM((2, tm, D_MODEL), F32), dma((2,)), dma((2,)), dma((3,)), dma((4, 7)), dma((4, 7))],
        compiler_params=pltpu.CompilerParams(dimension_semantics=("arbitrary", "arbitrary"), vmem_limit_bytes=VMEM_LIMIT),
    )(x, _row(norm_g), w_in_t, w_out, pw_w, conv_dw)


B_PLAIN, B_ON, B_MERGED = range(3)


def _routes():
    x, y, c = lax.axis_index("x"), lax.axis_index("y"), lax.axis_index("c")
    xn, yn = (1 - x, y), (x, 1 - y)
    pick = lambda a, b: tuple(jnp.where(c == 1, p, q) for p, q in zip(a, b))
    return pick(xn, yn), pick(yn, xn), (1 - x, 1 - y), (x, y), c


def _chip_index(chip):
    return 2 * chip[0] + chip[1]


def _copy(src, dst, send_sems, recv_sems, a, k, to):
    return pltpu.make_async_remote_copy(src_ref=src, dst_ref=dst, send_sem=send_sems.at[a, k], recv_sem=recv_sems.at[a, k],
                                        device_id=to, device_id_type=MESH)


def _rsb_send(plain, on, rb_plain, rb_on, send_b, recv_b):
    first, _, _, _, c = _routes()
    for k, srcs, dsts in ((B_ON, on, rb_on), (B_PLAIN, plain, rb_plain)):
        for a, (s, d) in enumerate(zip(srcs, dsts)):
            _copy(s, d, send_b, recv_b, a, k, (*first, c)).start()


def _rsb_merge(second_sums, merged, rb_on, rb_merged, send_b, recv_b):
    _, second, _, _, c = _routes()
    for a, (own, m, got, dst) in enumerate(zip(second_sums, merged, rb_on, rb_merged)):
        _copy(got, got, send_b, recv_b, a, B_ON, (*second, c)).wait_recv()
        m[...] = (own[...] + got[...].astype(F32)).astype(BF16)
        _copy(m, dst, send_b, recv_b, a, B_MERGED, (*second, c)).start()


def _rsb_total(own_sums, plain, on, merged, rb_plain, rb_merged, send_b, recv_b):
    _, second, _, _, c = _routes()
    out = []
    for a, (own, p, g) in enumerate(zip(own_sums, rb_plain, rb_merged)):
        _copy(p, p, send_b, recv_b, a, B_PLAIN, (*second, c)).wait_recv()
        _copy(g, g, send_b, recv_b, a, B_MERGED, (*second, c)).wait_recv()
        out.append(own[...] + p[...].astype(F32) + g[...].astype(F32))
    for k, srcs in ((B_PLAIN, plain), (B_ON, on), (B_MERGED, merged)):
        for a, s in enumerate(srcs):
            _copy(s, s, send_b, recv_b, a, k, (*second, c)).wait_send()
    return out


R_LOSS, R_FINAL_G, R_POOL, R_CONV, R_LN = range(5)
VEC_ROWS = 16
DW_ROWS = 32
A_PART, A_X, A_Y, A_ON, A_HALF = range(5)
HALF_DTYPES = (F32, BF16, BF16)
STEP_SUMS, STEP_MERGE, STEP_HALF = 1, 3, 5


def _bwd2a_body(n_steps, x_ref, dproj_ref, ng_ref, gwout_hbm, gpww_hbm, loss_ref, gfg_ref, gvec_ref, gdw_ref, gpoolw_ref,
                win_own_ref, win_staged_ref, owout_ref, opww_ref, ovec_ref, odw_ref, opoolw_ref,
                acc, st_win, a_win, own_wout, own_pww, a_wout, a_pww, st_wout, st_pww, rb_wout, rb_pww, part_vec, part_dw,
                part_pw, sib_vec, sib_dw, sib_pw, res_vec, res_dw, res_pw, half_vec, half_dw, half_pw,
                own_sems, send_a, recv_a, send_b, recv_b, send_r, recv_r, send_w, recv_w):
    step = pl.program_id(0) * pl.num_programs(1) + pl.program_id(1)
    x, y, c = lax.axis_index("x"), lax.axis_index("y"), lax.axis_index("c")
    sibling = (x, y, 1 - c)
    xn, yn = (1 - x, y), (x, 1 - y)
    first, second, diag, own, _ = _routes()
    rows = lambda ref, k, n: ref.at[pl.ds(pl.multiple_of(k * n, n), n), :]

    blocks = lambda k: (rows(gwout_hbm, k, W_OUT_BLK), rows(gpww_hbm, k, PW_BLK))
    mine = lambda chip: (own_wout.at[chip], own_pww.at[chip])
    mine_copies = lambda chip: [pltpu.make_async_copy(s, d, own_sems.at[a, chip])
                                for a, (s, d) in enumerate(zip(blocks(2 * chip + c), mine(chip)))]
    landed = lambda chip: (a_wout.at[chip], a_pww.at[chip])
    staged = lambda k: (st_wout.at[k], st_pww.at[k])
    arrived = lambda k: (rb_wout.at[k], rb_pww.at[k])

    parts, sibs, halves = (part_vec, part_dw, part_pw), (sib_vec, sib_dw, sib_pw), (half_vec, half_dw, half_pw)
    results = (res_vec, res_dw, res_pw)
    half_rows = lambda ref: pl.ds(pl.multiple_of(c * (ref.shape[0] // 2), SUBLANES), ref.shape[0] // 2)
    slot = lambda chip: tuple(h.at[_chip_index(chip)] for h in halves)

    @pl.when(step == 0)
    def _():
        acc[...] = jnp.zeros_like(acc)
        sum8 = lambda v: jnp.sum(v, axis=0, keepdims=True)
        part_vec[...] = jnp.zeros((VEC_ROWS, D_MODEL), F32)
        part_vec[R_LOSS:R_LOSS + 1, :] = jnp.broadcast_to(loss_ref[0:1, 0:1], (1, D_MODEL))
        part_vec[R_FINAL_G:R_FINAL_G + 1, :] = sum8(gfg_ref[...])
        for row, (lo, hi) in ((R_POOL, (V_POOL_B, V_POOL_SCALE)), (R_CONV, (V_CONV_B, V_LN_G)), (R_LN, (V_LN_B, V_PW_B))):
            part_vec[row:row + 1, 0:W_MIX] = sum8(gvec_ref[lo])
            part_vec[row:row + 1, W_MIX:] = sum8(gvec_ref[hi])
        part_dw[0:CONV_WIDTH, :] = jnp.sum(gdw_ref[...], axis=1)
        part_dw[CONV_WIDTH:, :] = jnp.zeros((DW_ROWS - CONV_WIDTH, W_MIX), F32)
        part_pw[...] = gpoolw_ref[...].reshape(len(POOL_WINDOWS) * POOL_GC, POOL_GC)
        for a, (p, s) in enumerate(zip(parts, sibs)):
            _copy(p, s, send_r, recv_r, a, A_PART, sibling).start()
        for chip in range(4):
            for a, (s, d) in enumerate(zip(blocks(2 * chip + (1 - c)), landed(chip))):
                _copy(s, d, send_a, recv_a, a, chip, sibling).start()
            for cp in mine_copies(chip):
                cp.start()

    x_t = x_ref[0]
    r1 = lax.rsqrt(jnp.mean(x_t * x_t, axis=-1, keepdims=True) + RMS_EPS)
    hn = (x_t * r1 * ng_ref[...]).astype(BF16)
    acc[...] += lax.dot_general(dproj_ref[0], hn, TN_DIMS, preferred_element_type=F32)

    @pl.when(step == STEP_SUMS)
    def _():
        for a, (p, s, h) in enumerate(zip(parts, sibs, slot(own))):
            _copy(p, s, send_r, recv_r, a, A_PART, sibling).wait_recv()
            h[...] = (p[half_rows(p), :] + s[half_rows(s), :]).astype(h.dtype)
            for k, to in ((A_X, xn), (A_Y, yn)):
                _copy(h, h, send_r, recv_r, a, k, (*to, c)).start()
        for chip in range(4):
            for cp in mine_copies(chip):
                cp.wait()
            for a, (m, l) in enumerate(zip(mine(chip), landed(chip))):
                _copy(l, l, send_a, recv_a, a, chip, sibling).wait_recv()
                l[...] = m[...] + l[...]
        for k, chip in ((B_ON, diag), (B_PLAIN, first)):
            for s, l in zip(staged(k), landed(_chip_index(chip))):
                s[...] = l[...].astype(BF16)
        _rsb_send(staged(B_PLAIN), staged(B_ON), arrived(B_PLAIN), arrived(B_ON), send_b, recv_b)

    @pl.when(step == STEP_MERGE)
    def _():
        _rsb_merge(landed(_chip_index(second)), staged(B_MERGED), arrived(B_ON), arrived(B_MERGED), send_b, recv_b)
        for a in range(3):
            for k, chip in ((A_X, xn), (A_Y, yn)):
                h = slot(chip)[a]
                _copy(h, h, send_r, recv_r, a, k, sibling).wait_recv()
            h = slot(first)[a]
            _copy(h, h, send_r, recv_r, a, A_ON, (*second, c)).start()

    @pl.when(step == STEP_HALF)
    def _():
        for a, (r, hs) in enumerate(zip(results, halves)):
            h = slot(diag)[a]
            _copy(h, h, send_r, recv_r, a, A_ON, sibling).wait_recv()
            mine = r.at[half_rows(r), :]
            chip_sums = [hs[k].astype(F32) for k in range(4)]
            mine[...] = (chip_sums[0] + chip_sums[1]) + (chip_sums[2] + chip_sums[3])
            _copy(mine, mine, send_r, recv_r, a, A_HALF, sibling).start()

    @pl.when(step == n_steps - 1)
    def _():
        block = lambda k: rows(acc, k, W_IN_BLK)
        w_copy = lambda chip: _copy(st_win.at[chip], a_win.at[chip], send_w, recv_w, 0, chip, sibling)
        for chip in range(4):
            st_win[chip] = block(2 * chip + (1 - c))[...].astype(BF16)
            w_copy(chip).start()

        owout_ref[...], opww_ref[...] = _rsb_total(landed(_chip_index(own)), staged(B_PLAIN), staged(B_ON), staged(B_MERGED),
                                                   arrived(B_PLAIN), arrived(B_MERGED), send_b, recv_b)
        for a, (r, p, s) in enumerate(zip(results, parts, sibs)):
            theirs = r.at[pl.ds(pl.multiple_of((1 - c) * (r.shape[0] // 2), SUBLANES), r.shape[0] // 2), :]
            _copy(theirs, theirs, send_r, recv_r, a, A_HALF, sibling).wait_recv()
            _copy(p, s, send_r, recv_r, a, A_PART, sibling).wait_send()
            for k, h in ((A_X, slot(own)[a]), (A_Y, slot(own)[a]), (A_ON, slot(first)[a])):
                _copy(h, h, send_r, recv_r, a, k, sibling).wait_send()
            mine = r.at[half_rows(r), :]
            _copy(mine, mine, send_r, recv_r, a, A_HALF, sibling).wait_send()
        for chip in range(4):
            for a, (s, d) in enumerate(zip(blocks(2 * chip + (1 - c)), landed(chip))):
                _copy(s, d, send_a, recv_a, a, chip, sibling).wait_send()
        ovec_ref[...] = res_vec[...]
        opoolw_ref[...] = res_pw[...]
        for k in range(N_DEV):
            @pl.when(4 * x + 2 * y + c == k)
            def _():
                odw_ref[...] = res_dw[0:CONV_WIDTH, k * DW_BLK:(k + 1) * DW_BLK]

        for chip in range(4):
            w_copy(chip).wait()
        chip_sum = lambda chip: block(2 * _chip_index(chip) + c)[...] + a_win[_chip_index(chip)].astype(F32)
        win_own_ref[...] = chip_sum(own)
        for j, chip in enumerate((xn, yn, diag)):
            win_staged_ref[j] = chip_sum(chip).astype(BF16)


def _bwd2a(x, dproj, norm_g, gwout, gpww, loss8, gfg8, gvec, gdw8, gpoolw):
    bl, s, _ = x.shape
    tm = TILE_M
    nt = s // tm
    tok = lambda n: pl.BlockSpec((1, tm, n), lambda b, i: (b, i, 0))
    n_pw = len(POOL_WINDOWS) * POOL_GC
    win = (W_IN_BLK, D_MODEL)
    blk = ((W_OUT_BLK, D_MODEL), (PW_BLK, W_MIX))
    rep = ((VEC_ROWS, D_MODEL), (DW_ROWS, W_MIX), (n_pw, POOL_GC))
    half = lambda sh: (sh[0] // 2,) + sh[1:]
    dma = pltpu.SemaphoreType.DMA
    assert bl * nt > STEP_HALF + 1, "the exchanges' phases need their grid steps"
    return pl.pallas_call(
        functools.partial(_bwd2a_body, bl * nt),
        name="bwd2a",
        grid=(bl, nt),
        in_specs=[tok(D_MODEL), tok(D_IN), VMEM] + [pl.BlockSpec(memory_space=pl.ANY)] * 2 + [VMEM] * 5,
        out_specs=[VMEM] * 7,
        out_shape=[jax.ShapeDtypeStruct(win, F32), jax.ShapeDtypeStruct((3,) + win, BF16)]
        + [jax.ShapeDtypeStruct(sh, F32) for sh in blk + (rep[0], (CONV_WIDTH, DW_BLK), rep[2])],
        scratch_shapes=[pltpu.VMEM((D_IN, D_MODEL), F32), pltpu.VMEM((4,) + win, BF16), pltpu.VMEM((4,) + win, BF16)]
        + [pltpu.VMEM((4,) + sh, F32) for sh in blk] * 2
        + [pltpu.VMEM((3,) + sh, BF16) for sh in blk] * 2
        + [pltpu.VMEM(sh, F32) for sh in rep] * 3
        + [pltpu.VMEM((4,) + half(sh), dt) for sh, dt in zip(rep, HALF_DTYPES)]
        + [dma((2, 4)), dma((2, 4)), dma((2, 4)), dma((2, 3)), dma((2, 3)), dma((3, 5)), dma((3, 5)), dma((1, 4)), dma((1, 4))],
        compiler_params=pltpu.CompilerParams(dimension_semantics=("arbitrary", "arbitrary"), vmem_limit_bytes=VMEM_LIMIT),
    )(x, dproj, _row(norm_g), gwout, gpww, loss8, gfg8, gvec, gdw8, gpoolw)


HBM = pl.BlockSpec(memory_space=pltpu.HBM)
SEM = pl.BlockSpec(memory_space=pltpu.SEMAPHORE)
EFFECT = pltpu.SideEffectType.DATAFLOW_SIDE_EFFECTING


def _owner_copies(src_ref, land_ref, send_sems, recv_sems):
    x, y, c = lax.axis_index("x"), lax.axis_index("y"), lax.axis_index("c")
    owners = ((1 - x, y, c), (x, 1 - y, c), (1 - x, 1 - y, c))
    return [pltpu.make_async_remote_copy(src_ref=src_ref.at[j], dst_ref=land_ref.at[j], send_sem=send_sems.at[j],
                                         recv_sem=recv_sems.at[j], device_id=owners[j], device_id_type=MESH)
            for j in range(3)]


def _rs_start(staged):
    def body(src_ref, land_ref, send_sems, recv_sems, src_thru, land_thru, token):
        for cp in _owner_copies(src_ref, land_ref, send_sems, recv_sems):
            cp.start()
        token[...] = jnp.zeros_like(token)

    dma = pltpu.SemaphoreType.DMA
    return pl.pallas_call(
        body, name="rs_start",
        out_shape=(dma((3,)), dma((3,)), pltpu.HBM(staged.shape, staged.dtype), pltpu.HBM(staged.shape, staged.dtype),
                   jax.ShapeDtypeStruct((SUBLANES, LANES), F32)),
        in_specs=(HBM, HBM), out_specs=(SEM, SEM, HBM, HBM, VMEM), input_output_aliases={0: 2, 1: 3},
        compiler_params=pltpu.CompilerParams(has_side_effects=EFFECT),
    )(pltpu.with_memory_space_constraint(staged, pltpu.HBM),
      pltpu.with_memory_space_constraint(lax.empty(staged.shape, staged.dtype), pltpu.HBM))


def _rs_wait(send_sems, recv_sems, src_thru, land_thru, after):
    def body(src_ref, land_ref, send_sems, recv_sems, after_ref, src_dead, got_ref):
        for cp in _owner_copies(src_ref, land_ref, send_sems, recv_sems):
            cp.wait_send()
            cp.wait_recv()

    return pl.pallas_call(
        body, name="rs_wait",
        out_shape=(pltpu.HBM(src_thru.shape, src_thru.dtype), pltpu.HBM(land_thru.shape, land_thru.dtype)),
        in_specs=(HBM, HBM, SEM, SEM, pl.BlockSpec(memory_space=pl.ANY)), out_specs=(HBM, HBM),
        input_output_aliases={0: 0, 1: 1},
        compiler_params=pltpu.CompilerParams(has_side_effects=EFFECT),
    )(src_thru, land_thru, send_sems, recv_sems, after)[1]


RING = 3


def _bwd2b_body(n_steps, tiles_per_seq, x_hbm, dh_hbm, dproj_hbm, ng_ref, wint_hbm, token_ref, gx_ref, gng_ref,
                x_ring, dh_ring, dproj_ring, wint_ref, sems, w_sem):
    step = pl.program_id(0) * pl.num_programs(1) + pl.program_id(1)
    tm = TILE_M
    weight_copy = pltpu.make_async_copy(wint_hbm, wint_ref, w_sem.at[0])

    def tile_copies(k):
        k = jnp.asarray(k, jnp.int32)
        slot = k % RING
        rows = pl.ds(pl.multiple_of((k % tiles_per_seq) * tm, tm), tm)
        return [pltpu.make_async_copy(src.at[k // tiles_per_seq, rows, :], ring.at[slot], sems.at[j, slot])
                for j, (src, ring) in enumerate(((x_hbm, x_ring), (dh_hbm, dh_ring), (dproj_hbm, dproj_ring)))]

    @pl.when(step == 0)
    def _():
        gng_ref[...] = jnp.zeros_like(gng_ref) + token_ref[0:1, 0:1]
        for cp in tile_copies(0):
            cp.start()
        weight_copy.start()
        for k in range(1, RING - 1):
            for cp in tile_copies(k):
                cp.start()

    @pl.when(step + RING - 1 < n_steps)
    def _():
        for cp in tile_copies(step + RING - 1):
            cp.start()

    for cp in tile_copies(step):
        cp.wait()

    @pl.when(step == 0)
    def _():
        weight_copy.wait()

    slot = step % RING
    x_t = x_ring[slot]
    r1 = lax.rsqrt(jnp.mean(x_t * x_t, axis=-1, keepdims=True) + RMS_EPS)
    xr = x_t * r1
    dhn = jnp.dot(dproj_ring[slot], wint_ref[...], preferred_element_type=F32)
    gng_ref[...] += _colsum8(dhn * xr)
    gy = dhn * ng_ref[...]
    gx_ref[0] = dh_ring[slot] + r1 * (gy - xr * jnp.mean(gy * xr, axis=-1, keepdims=True))


def _bwd2b(x, dh, dproj, norm_g, w_in_t, token):
    bl, s, _ = x.shape
    tm = TILE_M
    nt = s // tm
    assert bl * nt >= RING
    hbm = pl.BlockSpec(memory_space=pl.ANY)
    return pl.pallas_call(
        functools.partial(_bwd2b_body, bl * nt, nt),
        name="bwd2b",
        grid=(bl, nt),
        in_specs=[hbm, hbm, hbm, VMEM, hbm, VMEM],
        out_specs=[pl.BlockSpec((1, tm, D_MODEL), lambda b, i: (b, i, 0)),
                   pl.BlockSpec((SUBLANES, D_MODEL), lambda b, i: (0, 0))],
        out_shape=[jax.ShapeDtypeStruct((bl, s, D_MODEL), F32), jax.ShapeDtypeStruct((SUBLANES, D_MODEL), F32)],
        scratch_shapes=[pltpu.VMEM((RING, tm, D_MODEL), F32), pltpu.VMEM((RING, tm, D_MODEL), F32),
                        pltpu.VMEM((RING, tm, D_IN), BF16), pltpu.VMEM(w_in_t.shape, w_in_t.dtype),
                        pltpu.SemaphoreType.DMA((3, RING)), pltpu.SemaphoreType.DMA((1,))],
        compiler_params=pltpu.CompilerParams(dimension_semantics=("arbitrary", "arbitrary"), vmem_limit_bytes=VMEM_LIMIT),
    )(x, dh, dproj, _row(norm_g), w_in_t, token)


def _partial_copies(part_ref, land_ref, send_sems, recv_sems):
    x, y, c = lax.axis_index("x"), lax.axis_index("y"), lax.axis_index("c")
    other_chips = [(1 - x, y), (x, 1 - y), (1 - x, 1 - y)]
    peers = [(x, y, 1 - c)] + [(*chip, c) for chip in other_chips] + [(*chip, 1 - c) for chip in other_chips]
    return [pltpu.make_async_remote_copy(src_ref=part_ref, dst_ref=land_ref.at[4 * x + 2 * y + c], send_sem=send_sems.at[r],
                                         recv_sem=recv_sems.at[r], device_id=peers[r], device_id_type=MESH)
            for r in range(N_PEER)]


def _partials_start(part):
    def body(part_ref, land_ref, send_sems, recv_sems, part_thru, land_thru):
        for cp in _partial_copies(part_ref, land_ref, send_sems, recv_sems):
            cp.start()

    dma = pltpu.SemaphoreType.DMA
    land = (N_DEV,) + part.shape
    return pl.pallas_call(
        body, name="partials_start",
        out_shape=(dma((N_PEER,)), dma((N_PEER,)), pltpu.HBM(part.shape, part.dtype), pltpu.HBM(land, part.dtype)),
        in_specs=(HBM, HBM), out_specs=(SEM, SEM, HBM, HBM), input_output_aliases={0: 2, 1: 3},
        compiler_params=pltpu.CompilerParams(has_side_effects=EFFECT),
    )(pltpu.with_memory_space_constraint(part, pltpu.HBM),
      pltpu.with_memory_space_constraint(lax.empty(land, part.dtype), pltpu.HBM))


def _partials_wait(send_sems, recv_sems, part_thru, land_thru, after):
    def body(part_ref, land_ref, send_sems, recv_sems, after_ref, part_out, land_out):
        for cp in _partial_copies(part_ref, land_ref, send_sems, recv_sems):
            cp.wait_send()
            cp.wait_recv()

    return pl.pallas_call(
        body, name="partials_wait",
        out_shape=(pltpu.HBM(part_thru.shape, part_thru.dtype), pltpu.HBM(land_thru.shape, land_thru.dtype)),
        in_specs=(HBM, HBM, SEM, SEM, pl.BlockSpec(memory_space=pl.ANY)), out_specs=(HBM, HBM),
        input_output_aliases={0: 0, 1: 1},
        compiler_params=pltpu.CompilerParams(has_side_effects=EFFECT),
    )(part_thru, land_thru, send_sems, recv_sems, after)


ADAM_LR = 0.001
ADAM_B1 = 0.9
ADAM_B2 = 0.999
ADAM_EPS = 1e-08
ADAM_WD = 0.01
ADAM_STEP = 10


PARAMS = ("norm_g", "w_in", "pool_w", "pool_b", "pool_scale", "conv_dw", "conv_b", "ln_g", "ln_b", "pw_w", "pw_b", "w_out",
          "final_g")


LATE_PARAMS = ("norm_g",)
MAIN_PARAMS = tuple(p for p in PARAMS if p not in LATE_PARAMS)


def _adamw_refs(params, n_sources, refs):
    n = len(params)
    ws, ms, vs = (dict(zip(params, refs[k * n:(k + 1) * n])) for k in range(3))
    outs = refs[3 * n + n_sources:]
    gs, ds, m2s, v2s = (dict(zip(params, outs[k * n:(k + 1) * n])) for k in range(4))

    def update(name, g, at=slice(None)):
        m = ADAM_B1 * ms[name][at] + (1.0 - ADAM_B1) * g
        v = ADAM_B2 * vs[name][at] + (1.0 - ADAM_B2) * (g * g)
        m_hat = m / (1.0 - ADAM_B1 ** ADAM_STEP)
        v_hat = v / (1.0 - ADAM_B2 ** ADAM_STEP)
        gs[name][at] = g
        ds[name][at] = -ADAM_LR * (m_hat / (jnp.sqrt(v_hat) + ADAM_EPS) + ADAM_WD * ws[name][at])
        m2s[name][at] = m
        v2s[name][at] = v

    return update, refs[3 * n:3 * n + n_sources], outs[4 * n:]


def _adamw_late_body(*refs):
    update, (gng_ref, gng_others_ref), _ = _adamw_refs(LATE_PARAMS, 2, refs)
    me = 4 * lax.axis_index("x") + 2 * lax.axis_index("y") + lax.axis_index("c")
    g_norm = jnp.zeros(gng_ref.shape, F32)
    for k in range(N_DEV):
        g_norm = g_norm + jnp.where(me == k, gng_ref[...], gng_others_ref[k])
    update("norm_g", jnp.sum(g_norm, axis=0, keepdims=True))


def _adamw_body(*refs):
    update, sources, (loss_ref,) = _adamw_refs(MAIN_PARAMS, 7, refs)
    gwin_ref, gwin_others_ref, gwout_ref, gpww_ref, vec_ref, gdw_ref, gpoolw_ref = sources
    ws = dict(zip(MAIN_PARAMS, refs))
    row = lambda r, lo, hi: vec_ref[r:r + 1, lo:hi]
    loss_ref[...] = row(R_LOSS, 0, 1)
    g_w_in = gwin_ref[...]
    for j in range(3):
        g_w_in = g_w_in + gwin_others_ref[j].astype(F32)
    update("w_in", g_w_in)
    update("pool_w", gpoolw_ref[...].reshape(ws["pool_w"].shape))
    for g in range(len(POOL_WINDOWS)):
        update("pool_b", row(R_POOL, g * POOL_GC, (g + 1) * POOL_GC), at=slice(g, g + 1))
    update("pool_scale", row(R_POOL, W_MIX, 2 * W_MIX))
    update("conv_dw", gdw_ref[...])
    update("conv_b", row(R_CONV, 0, W_MIX))
    update("ln_g", row(R_CONV, W_MIX, 2 * W_MIX))
    update("ln_b", row(R_LN, 0, W_MIX))
    update("pw_w", gpww_ref[...])
    update("pw_b", row(R_LN, W_MIX, 2 * W_MIX))
    update("w_out", gwout_ref[...])
    update("final_g", row(R_FINAL_G, 0, D_MODEL))


def _adamw(body, name, params, ws, ms, vs, grad_sources, extra_out_shapes=()):
    n = len(params)
    flat = lambda a: a.reshape(1, -1) if a.ndim == 1 else a
    args = [flat(d[p]) for d in (ws, ms, vs) for p in params]
    outs = pl.pallas_call(
        body,
        name=name,
        in_specs=[VMEM] * (3 * n + len(grad_sources)),
        out_specs=[VMEM] * (4 * n + len(extra_out_shapes)),
        out_shape=[jax.ShapeDtypeStruct(a.shape, F32) for a in args[:n]] * 4 + list(extra_out_shapes),
        compiler_params=pltpu.CompilerParams(vmem_limit_bytes=VMEM_LIMIT),
    )(*args, *grad_sources)
    groups = tuple({p: outs[k * n + j].reshape(ws[p].shape) for j, p in enumerate(params)} for k in range(4))
    return groups, outs[4 * n:]


def kernel(x, norm_g, w_in, pool_w, pool_b, pool_scale, conv_dw, conv_b, ln_g, ln_b, pw_w, pw_b, w_out, final_g, loss_target, m_norm_g, m_w_in, m_pool_w, m_pool_b, m_pool_scale, m_conv_dw, m_conv_b, m_ln_g, m_ln_b, m_pw_w, m_pw_b, m_w_out, m_final_g, v_norm_g, v_w_in, v_pool_w, v_pool_b, v_pool_scale, v_conv_dw, v_conv_b, v_ln_g, v_ln_b, v_pw_w, v_pw_b, v_w_out, v_final_g):
    proj, w_in_t, w_out_b, pw_w_b, conv_dw_f = _inproj(x, norm_g, w_in.T, w_out, pw_w, conv_dw)
    h1, z, ob, dpool, dh, loss8, gfg8 = _fwd(x, loss_target, proj, pool_w, pool_b, pool_scale, conv_dw_f,
                                             conv_b, ln_g, ln_b, pw_w_b, pw_b, w_out_b, final_g)
    dproj, gwout, gpww, gpoolw, gvec, gdw8 = _bwd1(dh, proj, h1, z, ob, dpool, w_out_b, pw_w_b, pool_w, pool_scale,
                                                   conv_dw_f, ln_g, ln_b)
    win_own, win_staged, g_w_out, g_pw_w, vec, g_conv_dw, poolw_sum = _bwd2a(x, dproj, norm_g, gwout, gpww, loss8, gfg8, gvec,
                                                                             gdw8, gpoolw)
    send_sems, recv_sems, staged_thru, land_thru, token = _rs_start(win_staged)
    grad_x, gng8 = _bwd2b(x, dh, dproj, norm_g, w_in_t, token)
    ng_send, ng_recv, gng8, gng_land = _partials_start(gng8)
    win_others = _rs_wait(send_sems, recv_sems, staged_thru, land_thru, gng8)

    weights = dict(norm_g=norm_g, w_in=w_in.T, pool_w=pool_w, pool_b=pool_b, pool_scale=pool_scale, conv_dw=conv_dw, conv_b=conv_b,
                   ln_g=ln_g, ln_b=ln_b, pw_w=pw_w, pw_b=pw_b, w_out=w_out, final_g=final_g)
    m_in = dict(norm_g=m_norm_g, w_in=m_w_in.T, pool_w=m_pool_w, pool_b=m_pool_b, pool_scale=m_pool_scale, conv_dw=m_conv_dw,
                conv_b=m_conv_b, ln_g=m_ln_g, ln_b=m_ln_b, pw_w=m_pw_w, pw_b=m_pw_b, w_out=m_w_out, final_g=m_final_g)
    v_in = dict(norm_g=v_norm_g, w_in=v_w_in.T, pool_w=v_pool_w, pool_b=v_pool_b, pool_scale=v_pool_scale, conv_dw=v_conv_dw,
                conv_b=v_conv_b, ln_g=v_ln_g, ln_b=v_ln_b, pw_w=v_pw_w, pw_b=v_pw_b, w_out=v_w_out, final_g=v_final_g)
    main, (loss,) = _adamw(functools.partial(_adamw_body), "adamw", MAIN_PARAMS, weights, m_in, v_in,
                           (win_own, win_others, g_w_out, g_pw_w, vec, g_conv_dw, poolw_sum), [jax.ShapeDtypeStruct((1, 1), F32)])
    gng8, gng_others = _partials_wait(ng_send, ng_recv, gng8, gng_land, main[0]["w_out"])
    late, _ = _adamw(functools.partial(_adamw_late_body), "adamw_norm_g", LATE_PARAMS, weights, m_in, v_in, (gng8, gng_others))
    groups = [[(late[k] if p in LATE_PARAMS else main[k])[p] for p in PARAMS] for k in range(4)]
    k_in = PARAMS.index("w_in")
    for group in groups:
        group[k_in] = group[k_in].T
    grads, delta, new_m, new_v = groups
    return (loss.reshape(()), grad_x, *grads, *delta, *new_m, *new_v)
```

```python
import functools

import jax
import jax.numpy as jnp
from jax import lax
from jax.experimental import pallas as pl
from jax.experimental.pallas import tpu as pltpu

F32 = jnp.float32
BF16 = jnp.bfloat16

D_MODEL = 1024
W_MIX = 512
D_IN = 5 * W_MIX
POOL_WINDOWS = (2, 4, 8, 16)
POOL_GC = 128
CONV_WIDTH = 31
RMS_EPS = 1e-6
LN_EPS = 1e-5
N_DEV = 8

LANES = 128
SUBLANES = 8
TILE_M = 512
TILE_IN = 1024
POOL_HALO = 16
CONV_HALO = 32
CONV_ROWS = 128
BWD_CONV_ROWS = 64
BWD_SUBTILES = 2
VMEM_LIMIT = 56 * 1024 * 1024

VMEM = pl.BlockSpec(memory_space=pltpu.VMEM)


def _silu(v):
    return v * jax.nn.sigmoid(v)


def _colsum8(v):
    m, n = v.shape
    return jnp.sum(v.reshape(m // SUBLANES, SUBLANES, n), axis=0)


def _row_index(shape, t0):
    return lax.broadcasted_iota(jnp.int32, shape, 0) + t0


def _conv_taps():
    taps = {s: [] for s in range(SUBLANES)}
    for k in range(CONV_WIDTH):
        e = k + CONV_HALO - (CONV_WIDTH - 1)
        taps[e % SUBLANES].append((e // SUBLANES, k))
    return taps


RING = 3


def _ring_step(step, n_steps, tiles_per_seq, streams, sems):
    def tile_copies(k):
        k = jnp.asarray(k, jnp.int32)
        rows = pl.ds(pl.multiple_of((k % tiles_per_seq) * TILE_M, TILE_M), TILE_M)
        return [pltpu.make_async_copy(src.at[k // tiles_per_seq, rows, :], ring.at[k % RING], sems.at[j, k % RING])
                for j, (src, ring) in enumerate(streams)]

    @pl.when(step == 0)
    def _():
        for k in range(RING - 1):
            for cp in tile_copies(k):
                cp.start()

    @pl.when(step + RING - 1 < n_steps)
    def _():
        for cp in tile_copies(step + RING - 1):
            cp.start()

    for cp in tile_copies(step):
        cp.wait()
    return step % RING


def _fwd_body(n_steps, x_hbm, tgt_hbm, proj_hbm, poolw_ref, poolb_ref, pools_ref, dw_ref, cb_ref, lg_ref, lb_ref,
              pww_ref, pwb_ref, wout_ref, fg_ref,
              h1_ref, z_ref, ob_ref, dpool_ref, dh_ref, loss_ref, gfg_ref,
              uext, hext, y_s, x_ring, tgt_ring, proj_ring, sems):
    b = pl.program_id(0)
    i = pl.program_id(1)
    tm = TILE_M

    @pl.when(i == 0)
    def _():
        uext[0:POOL_HALO, :] = jnp.zeros((POOL_HALO, W_MIX), F32)
        hext[0:CONV_HALO, :] = jnp.zeros((CONV_HALO, W_MIX), F32)
        hext[CONV_HALO + tm:, :] = jnp.zeros((SUBLANES, W_MIX), F32)

    @pl.when((b == 0) & (i == 0))
    def _():
        loss_ref[...] = jnp.zeros_like(loss_ref)
        gfg_ref[...] = jnp.zeros_like(gfg_ref)

    slot = _ring_step(b * pl.num_programs(1) + i, n_steps, pl.num_programs(1),
                      ((x_hbm, x_ring), (tgt_hbm, tgt_ring), (proj_hbm, proj_ring)), sems)
    x_ref, tgt_ref, proj_ref = x_ring.at[slot], tgt_ring.at[slot], proj_ring.at[slot]

    uext[POOL_HALO:, :] = proj_ref[:, 0:W_MIX]
    t = _row_index((tm, POOL_GC), i * tm)
    for g, w in enumerate(POOL_WINDOWS):
        lanes = slice(g * POOL_GC, (g + 1) * POOL_GC)
        e = uext[:, lanes]
        s = e
        sh = 1
        while sh < w:
            s = s + pltpu.roll(s, sh, 0)
            sh *= 2
        inv_cnt = 1.0 / jnp.minimum(t + 1, w).astype(F32)
        d = (s[POOL_HALO:] * inv_cnt - e[POOL_HALO:]).astype(BF16)
        dpool_ref[0, :, lanes] = d
        z = jnp.dot(d, poolw_ref[g].astype(BF16), preferred_element_type=F32) + poolb_ref[:, lanes]
        z_ref[0, :, lanes] = z
        a_gate = proj_ref[:, W_MIX + g * POOL_GC:W_MIX + (g + 1) * POOL_GC]
        y_s[:, lanes] = (z * pools_ref[:, lanes] * _silu(a_gate)).astype(BF16)

    hext[CONV_HALO:CONV_HALO + tm, :] = proj_ref[:, 2 * W_MIX:3 * W_MIX] * jax.nn.sigmoid(proj_ref[:, 3 * W_MIX:4 * W_MIX])
    taps = _conv_taps()
    for c0 in range(0, tm, CONV_ROWS):
        for l0 in range(0, W_MIX, LANES):
            lanes = slice(l0, l0 + LANES)
            acc = jnp.zeros((CONV_ROWS, LANES), F32) + cb_ref[:, lanes]
            for s in range(SUBLANES):
                part = jnp.zeros((CONV_ROWS + SUBLANES, LANES), F32)
                for q, k in taps[s]:
                    r0 = c0 + SUBLANES * q
                    part = part + hext[r0:r0 + CONV_ROWS + SUBLANES, lanes] * dw_ref[k:k + 1, lanes]
                acc = acc + part[s:s + CONV_ROWS]
            h1_ref[0, c0:c0 + CONV_ROWS, lanes] = acc
    h1 = h1_ref[0]
    mu = jnp.mean(h1, axis=-1, keepdims=True)
    xc = h1 - mu
    rstd = lax.rsqrt(jnp.mean(xc * xc, axis=-1, keepdims=True) + LN_EPS)
    h3 = _silu(xc * rstd * lg_ref[...] + lb_ref[...])
    ob = jnp.dot(h3.astype(BF16), pww_ref[...], preferred_element_type=F32) + pwb_ref[...]
    ob_ref[0] = ob
    y_s[:, W_MIX:] = (ob * _silu(proj_ref[:, 4 * W_MIX:5 * W_MIX])).astype(BF16)

    h = x_ref[...] + jnp.dot(y_s[...], wout_ref[...], preferred_element_type=F32)
    r2 = lax.rsqrt(jnp.mean(h * h, axis=-1, keepdims=True) + RMS_EPS)
    hr = h * r2
    err = hr * fg_ref[...] - tgt_ref[...]
    loss_ref[...] += jnp.sum(err * err) * (0.5 / D_MODEL)
    dout = err * (1.0 / D_MODEL)
    gfg_ref[...] += _colsum8(dout * hr)
    gy = dout * fg_ref[...]
    dh_ref[0] = r2 * (gy - hr * jnp.mean(gy * hr, axis=-1, keepdims=True))

    uext[0:POOL_HALO, :] = uext[tm:tm + POOL_HALO, :]
    hext[0:CONV_HALO, :] = hext[tm:tm + CONV_HALO, :]


def _row(v):
    return v.reshape(1, -1)


def _fwd(x, tgt, proj, pool_w, pool_b, pool_scale, conv_dw_f, conv_b, ln_g, ln_b, pw_w_b, pw_b, w_out_b, final_g):
    bl, s, _ = x.shape
    tm = TILE_M
    tok = lambda n: pl.BlockSpec((1, tm, n), lambda b, i: (b, i, 0))
    acc = lambda n: pl.BlockSpec((SUBLANES, n), lambda b, i: (0, 0))
    act = lambda n: jax.ShapeDtypeStruct((bl, s, n), F32)
    ring = lambda n: pltpu.VMEM((RING, tm, n), F32)
    assert bl * (s // tm) >= RING
    return pl.pallas_call(
        functools.partial(_fwd_body, bl * (s // tm)),
        name="fwd",
        grid=(bl, s // tm),
        in_specs=[pl.BlockSpec(memory_space=pl.ANY)] * 3 + [VMEM] * 11,
        out_specs=[tok(W_MIX), tok(W_MIX), tok(W_MIX), tok(W_MIX), tok(D_MODEL), acc(LANES), acc(D_MODEL)],
        out_shape=[act(W_MIX), act(W_MIX), act(W_MIX), jax.ShapeDtypeStruct((bl, s, W_MIX), BF16), act(D_MODEL),
                   jax.ShapeDtypeStruct((SUBLANES, LANES), F32), jax.ShapeDtypeStruct((SUBLANES, D_MODEL), F32)],
        scratch_shapes=[pltpu.VMEM((POOL_HALO + tm, W_MIX), F32), pltpu.VMEM((CONV_HALO + tm + SUBLANES, W_MIX), F32),
                        pltpu.VMEM((tm, D_MODEL), BF16), ring(D_MODEL), ring(D_MODEL), ring(D_IN),
                        pltpu.SemaphoreType.DMA((3, RING))],
        compiler_params=pltpu.CompilerParams(dimension_semantics=("arbitrary", "arbitrary"), vmem_limit_bytes=VMEM_LIMIT),
    )(x, tgt, proj, pool_w, pool_b.reshape(1, W_MIX), _row(pool_scale), conv_dw_f, _row(conv_b),
      _row(ln_g), _row(ln_b), pw_w_b, _row(pw_b), w_out_b, _row(final_g))


NT_DIMS = (((1,), (1,)), ((), ()))
TN_DIMS = (((0,), (0,)), ((), ()))
V_POOL_B, V_POOL_SCALE, V_CONV_B, V_LN_G, V_LN_B, V_PW_B = range(6)


def _bwd_taps():
    taps = {s: [] for s in range(SUBLANES)}
    for k in range(CONV_WIDTH):
        e = CONV_WIDTH - 1 - k
        taps[e % SUBLANES].append((e // SUBLANES, k))
    return taps


def _bwd1_body(n_tiles, dh_ref, proj_ref, h1_ref, z_ref, ob_ref, dpool_ref, wout_ref, pww_ref, poolw_ref, pools_ref, dw_ref,
               lg_ref, lb_ref,
               dproj_ref, gwout_ref, gpww_ref, gpoolw_ref, gvec_ref, gdw_ref,
               vext, gext, gsh, y_s):
    b = pl.program_id(0)
    i = pl.program_id(1)
    tm = TILE_M
    t0 = (n_tiles - 1 - i) * tm

    @pl.when(i == 0)
    def _():
        vext[tm:, :] = jnp.zeros((POOL_HALO, W_MIX), F32)
        gext[tm:, :] = jnp.zeros((CONV_HALO, W_MIX), F32)

    @pl.when((b == 0) & (i == 0))
    def _():
        gwout_ref[...] = jnp.zeros_like(gwout_ref)
        gpww_ref[...] = jnp.zeros_like(gpww_ref)
        gpoolw_ref[...] = jnp.zeros_like(gpoolw_ref)
        gvec_ref[...] = jnp.zeros_like(gvec_ref)
        gdw_ref[...] = jnp.zeros_like(gdw_ref)

    sub = tm // BWD_SUBTILES

    def dense_phase(r0):
        rs = slice(r0, r0 + sub)
        dhb = dh_ref[0, rs, :].astype(BF16)
        dy_a = lax.dot_general(dhb, wout_ref[0:W_MIX, :], NT_DIMS, preferred_element_type=F32)
        dy_b = lax.dot_general(dhb, wout_ref[W_MIX:, :], NT_DIMS, preferred_element_type=F32)

        a_gate = proj_ref[0, rs, W_MIX:2 * W_MIX]
        sig_a = jax.nn.sigmoid(a_gate)
        silu_a = a_gate * sig_a
        z = z_ref[0, rs, :]
        out_a = z * pools_ref[...]
        y_s[rs, 0:W_MIX] = (out_a * silu_a).astype(BF16)
        d_out_a = dy_a * silu_a
        d_ag = dy_a * out_a * (sig_a * (1.0 + a_gate * (1.0 - sig_a)))
        dproj_ref[0, rs, W_MIX:2 * W_MIX] = d_ag.astype(BF16)
        gvec_ref[V_POOL_SCALE] += _colsum8(d_out_a * z)
        dz = d_out_a * pools_ref[...]
        gvec_ref[V_POOL_B] += _colsum8(dz)
        t = _row_index((sub, POOL_GC), t0 + r0)
        n_ext = sub + POOL_HALO
        for g, w in enumerate(POOL_WINDOWS):
            lanes = slice(g * POOL_GC, (g + 1) * POOL_GC)
            dzg = dz[:, lanes].astype(BF16)
            gpoolw_ref[g] += lax.dot_general(dpool_ref[0, rs, lanes], dzg, TN_DIMS, preferred_element_type=F32)
            dd = lax.dot_general(dzg, poolw_ref[g].astype(BF16), NT_DIMS, preferred_element_type=F32)
            vext[rs, lanes] = dd * (1.0 / jnp.minimum(t + 1, w).astype(F32))
            s = vext[r0:r0 + n_ext, lanes]
            sh = 1
            while sh < w:
                s = s + pltpu.roll(s, n_ext - sh, 0)
                sh *= 2
            dproj_ref[0, rs, lanes] = (s[0:sub] - dd).astype(BF16)

        b_gate = proj_ref[0, rs, 4 * W_MIX:5 * W_MIX]
        sig_b = jax.nn.sigmoid(b_gate)
        silu_b = b_gate * sig_b
        ob = ob_ref[0, rs, :]
        y_s[rs, W_MIX:] = (ob * silu_b).astype(BF16)
        d_ob = dy_b * silu_b
        dproj_ref[0, rs, 4 * W_MIX:5 * W_MIX] = (dy_b * ob * (sig_b * (1.0 + b_gate * (1.0 - sig_b)))).astype(BF16)
        gvec_ref[V_PW_B] += _colsum8(d_ob)
        gwout_ref[...] += lax.dot_general(y_s[rs, :], dhb, TN_DIMS, preferred_element_type=F32)

        h1 = h1_ref[0, rs, :]
        mu = jnp.mean(h1, axis=-1, keepdims=True)
        xc = h1 - mu
        rstd = lax.rsqrt(jnp.mean(xc * xc, axis=-1, keepdims=True) + LN_EPS)
        xhat = xc * rstd
        h2 = xhat * lg_ref[...] + lb_ref[...]
        sig2 = jax.nn.sigmoid(h2)
        d_obb = d_ob.astype(BF16)
        gpww_ref[...] += lax.dot_general((h2 * sig2).astype(BF16), d_obb, TN_DIMS, preferred_element_type=F32)
        dh2 = lax.dot_general(d_obb, pww_ref[...], NT_DIMS, preferred_element_type=F32) * (sig2 * (1.0 + h2 * (1.0 - sig2)))
        gvec_ref[V_LN_G] += _colsum8(dh2 * xhat)
        gvec_ref[V_LN_B] += _colsum8(dh2)
        dxh = dh2 * lg_ref[...]
        dh1 = rstd * (dxh - jnp.mean(dxh, axis=-1, keepdims=True) - xhat * jnp.mean(dxh * xhat, axis=-1, keepdims=True))
        gvec_ref[V_CONV_B] += _colsum8(dh1)
        gext[rs, :] = dh1

    taps = _bwd_taps()
    cr = BWD_CONV_ROWS
    win = cr + CONV_HALO - SUBLANES

    def conv_phase(r0):
        for c0 in range(r0, r0 + sub, cr):
            for l0 in range(0, W_MIX, LANES):
                lanes = slice(l0, l0 + LANES)
                b_val = proj_ref[0, c0:c0 + cr, 2 * W_MIX + l0:2 * W_MIX + l0 + LANES]
                sg = jax.nn.sigmoid(proj_ref[0, c0:c0 + cr, 3 * W_MIX + l0:3 * W_MIX + l0 + LANES])
                h0c = b_val * sg
                acc = jnp.zeros((cr, LANES), F32)
                for s in range(SUBLANES):
                    gsh[...] = gext[c0 + s:c0 + s + win, lanes]
                    for q, k in taps[s]:
                        gq = gsh[SUBLANES * q:SUBLANES * q + cr, :]
                        acc = acc + gq * dw_ref[k:k + 1, lanes]
                        gdw_ref[k, :, lanes] += _colsum8(h0c * gq)
                d_bval = acc * sg
                dproj_ref[0, c0:c0 + cr, 2 * W_MIX + l0:2 * W_MIX + l0 + LANES] = d_bval.astype(BF16)
                dproj_ref[0, c0:c0 + cr, 3 * W_MIX + l0:3 * W_MIX + l0 + LANES] = (d_bval * b_val * (1.0 - sg)).astype(BF16)

    starts = [k * sub for k in reversed(range(BWD_SUBTILES))]
    for r0 in starts:
        dense_phase(r0)
    for r0 in starts:
        conv_phase(r0)
    vext[tm:, :] = vext[0:POOL_HALO, :]
    gext[tm:, :] = gext[0:CONV_HALO, :]


def _bwd1(dh, proj, h1, z, ob, dpool, w_out_b, pw_w_b, pool_w, pool_scale, conv_dw_f, ln_g, ln_b):
    bl, s, _ = dh.shape
    tm = TILE_M
    nt = s // tm
    tok = lambda n: pl.BlockSpec((1, tm, n), lambda b, i: (b, nt - 1 - i, 0))
    res = lambda shape: pl.BlockSpec(shape, lambda b, i: (0,) * len(shape))
    out_shapes = [(D_MODEL, D_MODEL), (W_MIX, W_MIX), (len(POOL_WINDOWS), POOL_GC, POOL_GC), (6, SUBLANES, W_MIX),
                  (CONV_WIDTH, SUBLANES, W_MIX)]
    return pl.pallas_call(
        functools.partial(_bwd1_body, nt),
        name="bwd1",
        grid=(bl, nt),
        in_specs=[tok(D_MODEL), tok(D_IN), tok(W_MIX), tok(W_MIX), tok(W_MIX), tok(W_MIX)] + [VMEM] * 7,
        out_specs=[tok(D_IN)] + [res(sh) for sh in out_shapes],
        out_shape=[jax.ShapeDtypeStruct((bl, s, D_IN), BF16)] + [jax.ShapeDtypeStruct(sh, F32) for sh in out_shapes],
        scratch_shapes=[pltpu.VMEM((tm + POOL_HALO, W_MIX), F32), pltpu.VMEM((tm + CONV_HALO, W_MIX), F32),
                        pltpu.VMEM((BWD_CONV_ROWS + CONV_HALO - SUBLANES, LANES), F32), pltpu.VMEM((tm, D_MODEL), BF16)],
        compiler_params=pltpu.CompilerParams(dimension_semantics=("arbitrary", "arbitrary"), vmem_limit_bytes=VMEM_LIMIT),
    )(dh, proj, h1, z, ob, dpool, w_out_b, pw_w_b, pool_w, _row(pool_scale), conv_dw_f, _row(ln_g), _row(ln_b))


MESH = pl.DeviceIdType.MESH
W_IN_BLK = D_IN // N_DEV
W_OUT_BLK = D_MODEL // N_DEV
PW_BLK = W_MIX // N_DEV
DW_BLK = W_MIX // N_DEV
N_PEER = N_DEV - 1


SLAB = 2 * W_IN_BLK
N_SLABS = 4
F_SIB, F_X, F_Y, F_ON, F_PASS_X, F_PASS_Y, F_PASS_D = range(7)
W_IN_ONLY, OTHERS = (0,), (1, 2, 3)


def _inproj_body(n_tiles, x_hbm, ng_ref, wint_ref, wout_ref, pww_ref, dw_ref,
                 proj_hbm, wt_hbm, woutf_hbm, pwwf_hbm, dwf_ref,
                 hn_s, wt_s, wout_s, pww_s, dw_blk, stage, xbuf, out_sems, x_sems, w_sems, send_sems, recv_sems):
    s = pl.program_id(0)
    i = pl.program_id(1)
    tm = TILE_IN
    x, y, c = lax.axis_index("x"), lax.axis_index("y"), lax.axis_index("c")
    sibling = (x, y, 1 - c)
    xn, yn, dg = (1 - x, y), (x, 1 - y), (1 - x, 1 - y)
    pick = lambda a, b: tuple(jnp.where(c == 1, p, q) for p, q in zip(a, b))
    on_from, on_to = pick(xn, yn), pick(yn, xn)

    def blocks(px, py, pc):
        k = 4 * px + 2 * py + pc
        rows = lambda ref, n: ref.at[pl.ds(pl.multiple_of(k * n, n), n), :]
        return (rows(wt_s, W_IN_BLK), rows(wout_s, W_OUT_BLK), rows(pww_s, PW_BLK), dw_blk.at[k])

    def copies(arrays, flow, block, to):
        refs = blocks(*block)
        return [pltpu.make_async_remote_copy(src_ref=refs[a], dst_ref=refs[a], send_sem=send_sems.at[a, flow],
                                             recv_sem=recv_sems.at[a, flow], device_id=to, device_id_type=MESH)
                for a in arrays]

    def start(arrays):
        for flow, to in ((F_SIB, sibling), (F_X, (*xn, c)), (F_Y, (*yn, c))):
            for cp in copies(arrays, flow, (x, y, c), to):
                cp.start()

    def direct(arrays):
        for flow, chip in ((F_X, xn), (F_Y, yn)):
            for cp in copies(arrays, flow, (*chip, c), sibling):
                cp.wait_recv()
        for cp in copies(arrays, F_ON, (*on_from, c), (*on_to, c)):
            cp.start()
        for flow, chip in ((F_PASS_X, xn), (F_PASS_Y, yn)):
            for cp in copies(arrays, flow, (*chip, c), sibling):
                cp.start()

    def diagonal(arrays):
        for cp in copies(arrays, F_ON, (*dg, c), sibling):
            cp.wait_recv()
        for cp in copies(arrays, F_PASS_D, (*dg, c), sibling):
            cp.start()

    def from_sibling(arrays, flow, chip):
        for cp in copies(arrays, flow, (*chip, 1 - c), sibling):
            cp.wait_recv()

    def finish(arrays):
        mine = (x, y, c)
        for flow, block in ((F_SIB, mine), (F_X, mine), (F_Y, mine), (F_ON, (*on_from, c)), (F_PASS_X, (*xn, c)),
                            (F_PASS_Y, (*yn, c)), (F_PASS_D, (*dg, c))):
            for cp in copies(arrays, flow, block, sibling):
                cp.wait_send()

    step = s * n_tiles + i
    par = step % 2
    first = step == 0

    @pl.when(first)
    def _():
        mine = blocks(x, y, c)
        mine[0][...] = wint_ref[...].astype(BF16)
        mine[1][...] = wout_ref[...].astype(BF16)
        mine[2][...] = pww_ref[...].astype(BF16)
        mine[3][...] = dw_ref[...]
        start(W_IN_ONLY)
        start(OTHERS)

    rows_i = pl.ds(pl.multiple_of(i * tm, tm), tm)
    tiles_per_seq = proj_hbm.shape[1] // tm

    def x_copy(k):
        src = x_hbm.at[k // tiles_per_seq, pl.ds(pl.multiple_of((k % tiles_per_seq) * tm, tm), tm), :]
        return pltpu.make_async_copy(src, xbuf.at[k % 2], x_sems.at[k % 2])

    @pl.when(first)
    def _():
        x_copy(0).start()

    @pl.when(s == 0)
    def _():
        x_copy(i).wait()

        @pl.when(i + 1 < n_tiles)
        def _():
            x_copy(i + 1).start()

        xt = xbuf[i % 2]
        r1 = lax.rsqrt(jnp.mean(xt * xt, axis=-1, keepdims=True) + RMS_EPS)
        hn_s[rows_i, :] = (xt * r1 * ng_ref[...]).astype(BF16)

    @pl.when(first)
    def _():
        from_sibling(W_IN_ONLY, F_SIB, (x, y))

    @pl.when((s == 1) & (i == 0))
    def _():
        direct(W_IN_ONLY)
        from_sibling(W_IN_ONLY, F_PASS_Y, yn)

    @pl.when((s == 2) & (i == 0))
    def _():
        from_sibling(W_IN_ONLY, F_PASS_X, xn)

    weights_out = [pltpu.make_async_copy(src, dst, w_sems.at[k])
                   for k, (src, dst) in enumerate(((wt_s, wt_hbm), (wout_s, woutf_hbm), (pww_s, pwwf_hbm)))]

    @pl.when((s == 3) & (i == 0))
    def _():
        diagonal(W_IN_ONLY)
        direct(OTHERS)
        from_sibling(W_IN_ONLY, F_PASS_D, dg)
        weights_out[0].start()

    def out_copy(parity, col0):
        dst = proj_hbm.at[i // tiles_per_seq, pl.ds(pl.multiple_of((i % tiles_per_seq) * tm, tm), tm),
                          pl.ds(pl.multiple_of(col0, LANES), SLAB)]
        return pltpu.make_async_copy(stage.at[parity], dst, out_sems.at[parity])

    @pl.when(step >= 2)
    def _():
        out_copy(par, 0).wait()

    chip = jnp.where(s < 2, 2 * x, 2 * (1 - x)) + jnp.where((s == 0) | (s == 2), y, 1 - y)
    slab = wt_s[pl.ds(pl.multiple_of(chip * SLAB, SLAB), SLAB), :]
    stage[par] = lax.dot_general(hn_s[rows_i, :], slab, NT_DIMS, preferred_element_type=F32)
    out_copy(par, chip * SLAB).start()

    @pl.when(step == N_SLABS * n_tiles - 1)
    def _():
        out_copy(par, 0).wait()
        out_copy(1 - par, 0).wait()
        from_sibling(OTHERS, F_SIB, (x, y))
        from_sibling(OTHERS, F_PASS_X, xn)
        from_sibling(OTHERS, F_PASS_Y, yn)
        diagonal(OTHERS)
        from_sibling(OTHERS, F_PASS_D, dg)
        for cp in weights_out[1:]:
            cp.start()
        finish(W_IN_ONLY)
        finish(OTHERS)
        for cp in weights_out:
            cp.wait()
        for k in range(N_DEV):
            dwf_ref[:, k * DW_BLK:(k + 1) * DW_BLK] = dw_blk[k]


def _inproj(x, norm_g, w_in_t, w_out, pw_w, conv_dw):
    bl, s, _ = x.shape
    tm = TILE_IN
    nt = s // tm
    n_tiles = bl * nt
    dma = pltpu.SemaphoreType.DMA
    return pl.pallas_call(
        functools.partial(_inproj_body, n_tiles),
        name="inproj",
        grid=(N_SLABS, n_tiles),
        in_specs=[pl.BlockSpec(memory_space=pl.ANY)] + [VMEM] * 5,
        out_specs=[pl.BlockSpec(memory_space=pl.ANY)] * 4 + [VMEM],
        out_shape=[jax.ShapeDtypeStruct((bl, s, D_IN), F32), jax.ShapeDtypeStruct((D_IN, D_MODEL), BF16),
                   jax.ShapeDtypeStruct((D_MODEL, D_MODEL), BF16), jax.ShapeDtypeStruct((W_MIX, W_MIX), BF16),
                   jax.ShapeDtypeStruct((CONV_WIDTH, W_MIX), F32)],
        scratch_shapes=[pltpu.VMEM((n_tiles * tm, D_MODEL), BF16), pltpu.VMEM((D_IN, D_MODEL), BF16),
                        pltpu.VMEM((D_MODEL, D_MODEL), BF16), pltpu.VMEM((W_MIX, W_MIX), BF16),
                        pltpu.VMEM((N_DEV, CONV_WIDTH, DW_BLK), F32), pltpu.VMEM((2, tm, SLAB), F32),
                        pltpu.VMEM((2, tm, D_MODEL), F32), dma((2,)), dma((2,)), dma((3,)), dma((4, 7)), dma((4, 7))],
        compiler_params=pltpu.CompilerParams(dimension_semantics=("arbitrary", "arbitrary"), vmem_limit_bytes=VMEM_LIMIT),
    )(x, _row(norm_g), w_in_t, w_out, pw_w, conv_dw)


B_PLAIN, B_ON, B_MERGED = range(3)


def _routes():
    x, y, c = lax.axis_index("x"), lax.axis_index("y"), lax.axis_index("c")
    xn, yn = (1 - x, y), (x, 1 - y)
    pick = lambda a, b: tuple(jnp.where(c == 1, p, q) for p, q in zip(a, b))
    return pick(xn, yn), pick(yn, xn), (1 - x, 1 - y), (x, y), c


def _chip_index(chip):
    return 2 * chip[0] + chip[1]


def _copy(src, dst, send_sems, recv_sems, a, k, to):
    return pltpu.make_async_remote_copy(src_ref=src, dst_ref=dst, send_sem=send_sems.at[a, k], recv_sem=recv_sems.at[a, k],
                                        device_id=to, device_id_type=MESH)


def _rsb_send(plain, on, rb_plain, rb_on, send_b, recv_b):
    first, _, _, _, c = _routes()
    for k, srcs, dsts in ((B_ON, on, rb_on), (B_PLAIN, plain, rb_plain)):
        for a, (s, d) in enumerate(zip(srcs, dsts)):
            _copy(s, d, send_b, recv_b, a, k, (*first, c)).start()


def _rsb_merge(second_sums, merged, rb_on, rb_merged, send_b, recv_b):
    _, second, _, _, c = _routes()
    for a, (own, m, got, dst) in enumerate(zip(second_sums, merged, rb_on, rb_merged)):
        _copy(got, got, send_b, recv_b, a, B_ON, (*second, c)).wait_recv()
        m[...] = (own[...] + got[...].astype(F32)).astype(BF16)
        _copy(m, dst, send_b, recv_b, a, B_MERGED, (*second, c)).start()


def _rsb_total(own_sums, plain, on, merged, rb_plain, rb_merged, send_b, recv_b):
    _, second, _, _, c = _routes()
    out = []
    for a, (own, p, g) in enumerate(zip(own_sums, rb_plain, rb_merged)):
        _copy(p, p, send_b, recv_b, a, B_PLAIN, (*second, c)).wait_recv()
        _copy(g, g, send_b, recv_b, a, B_MERGED, (*second, c)).wait_recv()
        out.append(own[...] + p[...].astype(F32) + g[...].astype(F32))
    for k, srcs in ((B_PLAIN, plain), (B_ON, on), (B_MERGED, merged)):
        for a, s in enumerate(srcs):
            _copy(s, s, send_b, recv_b, a, k, (*second, c)).wait_send()
    return out


R_LOSS, R_FINAL_G, R_POOL, R_CONV, R_LN = range(5)
VEC_ROWS = 16
DW_ROWS = 32
A_PART, A_X, A_Y, A_ON, A_HALF = range(5)
HALF_DTYPES = (F32, BF16, BF16)
STEP_SUMS, STEP_MERGE, STEP_HALF = 1, 3, 5


def _bwd2a_body(n_steps, x_ref, dproj_ref, ng_ref, gwout_hbm, gpww_hbm, loss_ref, gfg_ref, gvec_ref, gdw_ref, gpoolw_ref,
                win_own_ref, win_staged_ref, owout_ref, opww_ref, ovec_ref, odw_ref, opoolw_ref,
                acc, st_win, a_win, own_wout, own_pww, a_wout, a_pww, st_wout, st_pww, rb_wout, rb_pww, part_vec, part_dw,
                part_pw, sib_vec, sib_dw, sib_pw, res_vec, res_dw, res_pw, half_vec, half_dw, half_pw,
                own_sems, send_a, recv_a, send_b, recv_b, send_r, recv_r, send_w, recv_w):
    step = pl.program_id(0) * pl.num_programs(1) + pl.program_id(1)
    x, y, c = lax.axis_index("x"), lax.axis_index("y"), lax.axis_index("c")
    sibling = (x, y, 1 - c)
    xn, yn = (1 - x, y), (x, 1 - y)
    first, second, diag, own, _ = _routes()
    rows = lambda ref, k, n: ref.at[pl.ds(pl.multiple_of(k * n, n), n), :]

    blocks = lambda k: (rows(gwout_hbm, k, W_OUT_BLK), rows(gpww_hbm, k, PW_BLK))
    mine = lambda chip: (own_wout.at[chip], own_pww.at[chip])
    mine_copies = lambda chip: [pltpu.make_async_copy(s, d, own_sems.at[a, chip])
                                for a, (s, d) in enumerate(zip(blocks(2 * chip + c), mine(chip)))]
    landed = lambda chip: (a_wout.at[chip], a_pww.at[chip])
    staged = lambda k: (st_wout.at[k], st_pww.at[k])
    arrived = lambda k: (rb_wout.at[k], rb_pww.at[k])

    parts, sibs, halves = (part_vec, part_dw, part_pw), (sib_vec, sib_dw, sib_pw), (half_vec, half_dw, half_pw)
    results = (res_vec, res_dw, res_pw)
    half_rows = lambda ref: pl.ds(pl.multiple_of(c * (ref.shape[0] // 2), SUBLANES), ref.shape[0] // 2)
    slot = lambda chip: tuple(h.at[_chip_index(chip)] for h in halves)

    @pl.when(step == 0)
    def _():
        acc[...] = jnp.zeros_like(acc)
        sum8 = lambda v: jnp.sum(v, axis=0, keepdims=True)
        part_vec[...] = jnp.zeros((VEC_ROWS, D_MODEL), F32)
        part_vec[R_LOSS:R_LOSS + 1, :] = jnp.broadcast_to(loss_ref[0:1, 0:1], (1, D_MODEL))
        part_vec[R_FINAL_G:R_FINAL_G + 1, :] = sum8(gfg_ref[...])
        for row, (lo, hi) in ((R_POOL, (V_POOL_B, V_POOL_SCALE)), (R_CONV, (V_CONV_B, V_LN_G)), (R_LN, (V_LN_B, V_PW_B))):
            part_vec[row:row + 1, 0:W_MIX] = sum8(gvec_ref[lo])
            part_vec[row:row + 1, W_MIX:] = sum8(gvec_ref[hi])
        part_dw[0:CONV_WIDTH, :] = jnp.sum(gdw_ref[...], axis=1)
        part_dw[CONV_WIDTH:, :] = jnp.zeros((DW_ROWS - CONV_WIDTH, W_MIX), F32)
        part_pw[...] = gpoolw_ref[...].reshape(len(POOL_WINDOWS) * POOL_GC, POOL_GC)
        for a, (p, s) in enumerate(zip(parts, sibs)):
            _copy(p, s, send_r, recv_r, a, A_PART, sibling).start()
        for chip in range(4):
            for a, (s, d) in enumerate(zip(blocks(2 * chip + (1 - c)), landed(chip))):
                _copy(s, d, send_a, recv_a, a, chip, sibling).start()
            for cp in mine_copies(chip):
                cp.start()

    x_t = x_ref[0]
    r1 = lax.rsqrt(jnp.mean(x_t * x_t, axis=-1, keepdims=True) + RMS_EPS)
    hn = (x_t * r1 * ng_ref[...]).astype(BF16)
    acc[...] += lax.dot_general(dproj_ref[0], hn, TN_DIMS, preferred_element_type=F32)

    @pl.when(step == STEP_SUMS)
    def _():
        for a, (p, s, h) in enumerate(zip(parts, sibs, slot(own))):
            _copy(p, s, send_r, recv_r, a, A_PART, sibling).wait_recv()
            h[...] = (p[half_rows(p), :] + s[half_rows(s), :]).astype(h.dtype)
            for k, to in ((A_X, xn), (A_Y, yn)):
                _copy(h, h, send_r, recv_r, a, k, (*to, c)).start()
        for chip in range(4):
            for cp in mine_copies(chip):
                cp.wait()
            for a, (m, l) in enumerate(zip(mine(chip), landed(chip))):
                _copy(l, l, send_a, recv_a, a, chip, sibling).wait_recv()
                l[...] = m[...] + l[...]
        for k, chip in ((B_ON, diag), (B_PLAIN, first)):
            for s, l in zip(staged(k), landed(_chip_index(chip))):
                s[...] = l[...].astype(BF16)
        _rsb_send(staged(B_PLAIN), staged(B_ON), arrived(B_PLAIN), arrived(B_ON), send_b, recv_b)

    @pl.when(step == STEP_MERGE)
    def _():
        _rsb_merge(landed(_chip_index(second)), staged(B_MERGED), arrived(B_ON), arrived(B_MERGED), send_b, recv_b)
        for a in range(3):
            for k, chip in ((A_X, xn), (A_Y, yn)):
                h = slot(chip)[a]
                _copy(h, h, send_r, recv_r, a, k, sibling).wait_recv()
            h = slot(first)[a]
            _copy(h, h, send_r, recv_r, a, A_ON, (*second, c)).start()

    @pl.when(step == STEP_HALF)
    def _():
        for a, (r, hs) in enumerate(zip(results, halves)):
            h = slot(diag)[a]
            _copy(h, h, send_r, recv_r, a, A_ON, sibling).wait_recv()
            mine = r.at[half_rows(r), :]
            chip_sums = [hs[k].astype(F32) for k in range(4)]
            mine[...] = (chip_sums[0] + chip_sums[1]) + (chip_sums[2] + chip_sums[3])
            _copy(mine, mine, send_r, recv_r, a, A_HALF, sibling).start()

    @pl.when(step == n_steps - 1)
    def _():
        block = lambda k: rows(acc, k, W_IN_BLK)
        w_copy = lambda chip: _copy(st_win.at[chip], a_win.at[chip], send_w, recv_w, 0, chip, sibling)
        for chip in range(4):
            st_win[chip] = block(2 * chip + (1 - c))[...].astype(BF16)
            w_copy(chip).start()

        owout_ref[...], opww_ref[...] = _rsb_total(landed(_chip_index(own)), staged(B_PLAIN), staged(B_ON), staged(B_MERGED),
                                                   arrived(B_PLAIN), arrived(B_MERGED), send_b, recv_b)
        for a, (r, p, s) in enumerate(zip(results, parts, sibs)):
            theirs = r.at[pl.ds(pl.multiple_of((1 - c) * (r.shape[0] // 2), SUBLANES), r.shape[0] // 2), :]
            _copy(theirs, theirs, send_r, recv_r, a, A_HALF, sibling).wait_recv()
            _copy(p, s, send_r, recv_r, a, A_PART, sibling).wait_send()
            for k, h in ((A_X, slot(own)[a]), (A_Y, slot(own)[a]), (A_ON, slot(first)[a])):
                _copy(h, h, send_r, recv_r, a, k, sibling).wait_send()
            mine = r.at[half_rows(r), :]
            _copy(mine, mine, send_r, recv_r, a, A_HALF, sibling).wait_send()
        for chip in range(4):
            for a, (s, d) in enumerate(zip(blocks(2 * chip + (1 - c)), landed(chip))):
                _copy(s, d, send_a, recv_a, a, chip, sibling).wait_send()
        ovec_ref[...] = res_vec[...]
        opoolw_ref[...] = res_pw[...]
        for k in range(N_DEV):
            @pl.when(4 * x + 2 * y + c == k)
            def _():
                odw_ref[...] = res_dw[0:CONV_WIDTH, k * DW_BLK:(k + 1) * DW_BLK]

        for chip in range(4):
            w_copy(chip).wait()
        chip_sum = lambda chip: block(2 * _chip_index(chip) + c)[...] + a_win[_chip_index(chip)].astype(F32)
        win_own_ref[...] = chip_sum(own)
        for j, chip in enumerate((xn, yn, diag)):
            win_staged_ref[j] = chip_sum(chip).astype(BF16)


def _bwd2a(x, dproj, norm_g, gwout, gpww, loss8, gfg8, gvec, gdw8, gpoolw):
    bl, s, _ = x.shape
    tm = TILE_M
    nt = s // tm
    tok = lambda n: pl.BlockSpec((1, tm, n), lambda b, i: (b, i, 0))
    n_pw = len(POOL_WINDOWS) * POOL_GC
    win = (W_IN_BLK, D_MODEL)
    blk = ((W_OUT_BLK, D_MODEL), (PW_BLK, W_MIX))
    rep = ((VEC_ROWS, D_MODEL), (DW_ROWS, W_MIX), (n_pw, POOL_GC))
    half = lambda sh: (sh[0] // 2,) + sh[1:]
    dma = pltpu.SemaphoreType.DMA
    assert bl * nt > STEP_HALF + 1, "the exchanges' phases need their grid steps"
    return pl.pallas_call(
        functools.partial(_bwd2a_body, bl * nt),
        name="bwd2a",
        grid=(bl, nt),
        in_specs=[tok(D_MODEL), tok(D_IN), VMEM] + [pl.BlockSpec(memory_space=pl.ANY)] * 2 + [VMEM] * 5,
        out_specs=[VMEM] * 7,
        out_shape=[jax.ShapeDtypeStruct(win, F32), jax.ShapeDtypeStruct((3,) + win, BF16)]
        + [jax.ShapeDtypeStruct(sh, F32) for sh in blk + (rep[0], (CONV_WIDTH, DW_BLK), rep[2])],
        scratch_shapes=[pltpu.VMEM((D_IN, D_MODEL), F32), pltpu.VMEM((4,) + win, BF16), pltpu.VMEM((4,) + win, BF16)]
        + [pltpu.VMEM((4,) + sh, F32) for sh in blk] * 2
        + [pltpu.VMEM((3,) + sh, BF16) for sh in blk] * 2
        + [pltpu.VMEM(sh, F32) for sh in rep] * 3
        + [pltpu.VMEM((4,) + half(sh), dt) for sh, dt in zip(rep, HALF_DTYPES)]
        + [dma((2, 4)), dma((2, 4)), dma((2, 4)), dma((2, 3)), dma((2, 3)), dma((3, 5)), dma((3, 5)), dma((1, 4)), dma((1, 4))],
        compiler_params=pltpu.CompilerParams(dimension_semantics=("arbitrary", "arbitrary"), vmem_limit_bytes=VMEM_LIMIT),
    )(x, dproj, _row(norm_g), gwout, gpww, loss8, gfg8, gvec, gdw8, gpoolw)


HBM = pl.BlockSpec(memory_space=pltpu.HBM)
SEM = pl.BlockSpec(memory_space=pltpu.SEMAPHORE)
EFFECT = pltpu.SideEffectType.DATAFLOW_SIDE_EFFECTING


def _owner_copies(src_ref, land_ref, send_sems, recv_sems):
    x, y, c = lax.axis_index("x"), lax.axis_index("y"), lax.axis_index("c")
    owners = ((1 - x, y, c), (x, 1 - y, c), (1 - x, 1 - y, c))
    return [pltpu.make_async_remote_copy(src_ref=src_ref.at[j], dst_ref=land_ref.at[j], send_sem=send_sems.at[j],
                                         recv_sem=recv_sems.at[j], device_id=owners[j], device_id_type=MESH)
            for j in range(3)]


def _rs_start(staged):
    def body(src_ref, land_ref, send_sems, recv_sems, src_thru, land_thru, token):
        for cp in _owner_copies(src_ref, land_ref, send_sems, recv_sems):
            cp.start()
        token[...] = jnp.zeros_like(token)

    dma = pltpu.SemaphoreType.DMA
    return pl.pallas_call(
        body, name="rs_start",
        out_shape=(dma((3,)), dma((3,)), pltpu.HBM(staged.shape, staged.dtype), pltpu.HBM(staged.shape, staged.dtype),
                   jax.ShapeDtypeStruct((SUBLANES, LANES), F32)),
        in_specs=(HBM, HBM), out_specs=(SEM, SEM, HBM, HBM, VMEM), input_output_aliases={0: 2, 1: 3},
        compiler_params=pltpu.CompilerParams(has_side_effects=EFFECT),
    )(pltpu.with_memory_space_constraint(staged, pltpu.HBM),
      pltpu.with_memory_space_constraint(lax.empty(staged.shape, staged.dtype), pltpu.HBM))


def _rs_wait(send_sems, recv_sems, src_thru, land_thru, after):
    def body(src_ref, land_ref, send_sems, recv_sems, after_ref, src_dead, got_ref):
        for cp in _owner_copies(src_ref, land_ref, send_sems, recv_sems):
            cp.wait_send()
            cp.wait_recv()

    return pl.pallas_call(
        body, name="rs_wait",
        out_shape=(pltpu.HBM(src_thru.shape, src_thru.dtype), pltpu.HBM(land_thru.shape, land_thru.dtype)),
        in_specs=(HBM, HBM, SEM, SEM, pl.BlockSpec(memory_space=pl.ANY)), out_specs=(HBM, HBM),
        input_output_aliases={0: 0, 1: 1},
        compiler_params=pltpu.CompilerParams(has_side_effects=EFFECT),
    )(src_thru, land_thru, send_sems, recv_sems, after)[1]


def _bwd2b_body(n_steps, tiles_per_seq, x_hbm, dh_hbm, dproj_hbm, ng_ref, wint_ref, token_ref, gx_ref, gng_ref,
                x_ring, dh_ring, dproj_ring, sems):
    step = pl.program_id(0) * pl.num_programs(1) + pl.program_id(1)

    @pl.when(step == 0)
    def _():
        gng_ref[...] = jnp.zeros_like(gng_ref) + token_ref[0:1, 0:1]

    slot = _ring_step(step, n_steps, tiles_per_seq, ((x_hbm, x_ring), (dh_hbm, dh_ring), (dproj_hbm, dproj_ring)), sems)
    x_t = x_ring[slot]
    r1 = lax.rsqrt(jnp.mean(x_t * x_t, axis=-1, keepdims=True) + RMS_EPS)
    xr = x_t * r1
    dhn = jnp.dot(dproj_ring[slot], wint_ref[...], preferred_element_type=F32)
    gng_ref[...] += _colsum8(dhn * xr)
    gy = dhn * ng_ref[...]
    gx_ref[0] = dh_ring[slot] + r1 * (gy - xr * jnp.mean(gy * xr, axis=-1, keepdims=True))


def _bwd2b(x, dh, dproj, norm_g, w_in_t, token):
    bl, s, _ = x.shape
    tm = TILE_M
    nt = s // tm
    assert bl * nt >= RING
    hbm = pl.BlockSpec(memory_space=pl.ANY)
    return pl.pallas_call(
        functools.partial(_bwd2b_body, bl * nt, nt),
        name="bwd2b",
        grid=(bl, nt),
        in_specs=[hbm, hbm, hbm, VMEM, VMEM, VMEM],
        out_specs=[pl.BlockSpec((1, tm, D_MODEL), lambda b, i: (b, i, 0)),
                   pl.BlockSpec((SUBLANES, D_MODEL), lambda b, i: (0, 0))],
        out_shape=[jax.ShapeDtypeStruct((bl, s, D_MODEL), F32), jax.ShapeDtypeStruct((SUBLANES, D_MODEL), F32)],
        scratch_shapes=[pltpu.VMEM((RING, tm, D_MODEL), F32), pltpu.VMEM((RING, tm, D_MODEL), F32),
                        pltpu.VMEM((RING, tm, D_IN), BF16), pltpu.SemaphoreType.DMA((3, RING))],
        compiler_params=pltpu.CompilerParams(dimension_semantics=("arbitrary", "arbitrary"), vmem_limit_bytes=VMEM_LIMIT),
    )(x, dh, dproj, _row(norm_g), w_in_t, token)


def _partial_copies(part_ref, land_ref, send_sems, recv_sems):
    x, y, c = lax.axis_index("x"), lax.axis_index("y"), lax.axis_index("c")
    other_chips = [(1 - x, y), (x, 1 - y), (1 - x, 1 - y)]
    peers = [(x, y, 1 - c)] + [(*chip, c) for chip in other_chips] + [(*chip, 1 - c) for chip in other_chips]
    return [pltpu.make_async_remote_copy(src_ref=part_ref, dst_ref=land_ref.at[4 * x + 2 * y + c], send_sem=send_sems.at[r],
                                         recv_sem=recv_sems.at[r], device_id=peers[r], device_id_type=MESH)
            for r in range(N_PEER)]


def _partials_start(part):
    def body(part_ref, land_ref, send_sems, recv_sems, part_thru, land_thru):
        for cp in _partial_copies(part_ref, land_ref, send_sems, recv_sems):
            cp.start()

    dma = pltpu.SemaphoreType.DMA
    land = (N_DEV,) + part.shape
    return pl.pallas_call(
        body, name="partials_start",
        out_shape=(dma((N_PEER,)), dma((N_PEER,)), pltpu.HBM(part.shape, part.dtype), pltpu.HBM(land, part.dtype)),
        in_specs=(HBM, HBM), out_specs=(SEM, SEM, HBM, HBM), input_output_aliases={0: 2, 1: 3},
        compiler_params=pltpu.CompilerParams(has_side_effects=EFFECT),
    )(pltpu.with_memory_space_constraint(part, pltpu.HBM),
      pltpu.with_memory_space_constraint(lax.empty(land, part.dtype), pltpu.HBM))


def _partials_wait(send_sems, recv_sems, part_thru, land_thru, after):
    def body(part_ref, land_ref, send_sems, recv_sems, after_ref, part_out, land_out):
        for cp in _partial_copies(part_ref, land_ref, send_sems, recv_sems):
            cp.wait_send()
            cp.wait_recv()

    return pl.pallas_call(
        body, name="partials_wait",
        out_shape=(pltpu.HBM(part_thru.shape, part_thru.dtype), pltpu.HBM(land_thru.shape, land_thru.dtype)),
        in_specs=(HBM, HBM, SEM, SEM, pl.BlockSpec(memory_space=pl.ANY)), out_specs=(HBM, HBM),
        input_output_aliases={0: 0, 1: 1},
        compiler_params=pltpu.CompilerParams(has_side_effects=EFFECT),
    )(part_thru, land_thru, send_sems, recv_sems, after)


ADAM_LR = 0.001
ADAM_B1 = 0.9
ADAM_B2 = 0.999
ADAM_EPS = 1e-08
ADAM_WD = 0.01
ADAM_STEP = 10


PARAMS = ("norm_g", "w_in", "pool_w", "pool_b", "pool_scale", "conv_dw", "conv_b", "ln_g", "ln_b", "pw_w", "pw_b", "w_out",
          "final_g")


LATE_PARAMS = ("norm_g",)
MAIN_PARAMS = tuple(p for p in PARAMS if p not in LATE_PARAMS)


def _adamw_refs(params, n_sources, refs):
    n = len(params)
    ws, ms, vs = (dict(zip(params, refs[k * n:(k + 1) * n])) for k in range(3))
    outs = refs[3 * n + n_sources:]
    gs, ds, m2s, v2s = (dict(zip(params, outs[k * n:(k + 1) * n])) for k in range(4))

    def update(name, g, at=slice(None)):
        m = ADAM_B1 * ms[name][at] + (1.0 - ADAM_B1) * g
        v = ADAM_B2 * vs[name][at] + (1.0 - ADAM_B2) * (g * g)
        m_hat = m / (1.0 - ADAM_B1 ** ADAM_STEP)
        v_hat = v / (1.0 - ADAM_B2 ** ADAM_STEP)
        gs[name][at] = g
        ds[name][at] = -ADAM_LR * (m_hat / (jnp.sqrt(v_hat) + ADAM_EPS) + ADAM_WD * ws[name][at])
        m2s[name][at] = m
        v2s[name][at] = v

    return update, refs[3 * n:3 * n + n_sources], outs[4 * n:]


def _adamw_late_body(*refs):
    update, (gng_ref, gng_others_ref), _ = _adamw_refs(LATE_PARAMS, 2, refs)
    me = 4 * lax.axis_index("x") + 2 * lax.axis_index("y") + lax.axis_index("c")
    g_norm = jnp.zeros(gng_ref.shape, F32)
    for k in range(N_DEV):
        g_norm = g_norm + jnp.where(me == k, gng_ref[...], gng_others_ref[k])
    update("norm_g", jnp.sum(g_norm, axis=0, keepdims=True))


def _adamw_body(*refs):
    update, sources, (loss_ref,) = _adamw_refs(MAIN_PARAMS, 7, refs)
    gwin_ref, gwin_others_ref, gwout_ref, gpww_ref, vec_ref, gdw_ref, gpoolw_ref = sources
    ws = dict(zip(MAIN_PARAMS, refs))
    row = lambda r, lo, hi: vec_ref[r:r + 1, lo:hi]
    loss_ref[...] = row(R_LOSS, 0, 1)
    g_w_in = gwin_ref[...]
    for j in range(3):
        g_w_in = g_w_in + gwin_others_ref[j].astype(F32)
    update("w_in", g_w_in)
    update("pool_w", gpoolw_ref[...].reshape(ws["pool_w"].shape))
    for g in range(len(POOL_WINDOWS)):
        update("pool_b", row(R_POOL, g * POOL_GC, (g + 1) * POOL_GC), at=slice(g, g + 1))
    update("pool_scale", row(R_POOL, W_MIX, 2 * W_MIX))
    update("conv_dw", gdw_ref[...])
    update("conv_b", row(R_CONV, 0, W_MIX))
    update("ln_g", row(R_CONV, W_MIX, 2 * W_MIX))
    update("ln_b", row(R_LN, 0, W_MIX))
    update("pw_w", gpww_ref[...])
    update("pw_b", row(R_LN, W_MIX, 2 * W_MIX))
    update("w_out", gwout_ref[...])
    update("final_g", row(R_FINAL_G, 0, D_MODEL))


def _adamw(body, name, params, ws, ms, vs, grad_sources, extra_out_shapes=()):
    n = len(params)
    flat = lambda a: a.reshape(1, -1) if a.ndim == 1 else a
    args = [flat(d[p]) for d in (ws, ms, vs) for p in params]
    outs = pl.pallas_call(
        body,
        name=name,
        in_specs=[VMEM] * (3 * n + len(grad_sources)),
        out_specs=[VMEM] * (4 * n + len(extra_out_shapes)),
        out_shape=[jax.ShapeDtypeStruct(a.shape, F32) for a in args[:n]] * 4 + list(extra_out_shapes),
        compiler_params=pltpu.CompilerParams(vmem_limit_bytes=VMEM_LIMIT),
    )(*args, *grad_sources)
    groups = tuple({p: outs[k * n + j].reshape(ws[p].shape) for j, p in enumerate(params)} for k in range(4))
    return groups, outs[4 * n:]


def kernel(x, norm_g, w_in, pool_w, pool_b, pool_scale, conv_dw, conv_b, ln_g, ln_b, pw_w, pw_b, w_out, final_g, loss_target, m_norm_g, m_w_in, m_pool_w, m_pool_b, m_pool_scale, m_conv_dw, m_conv_b, m_ln_g, m_ln_b, m_pw_w, m_pw_b, m_w_out, m_final_g, v_norm_g, v_w_in, v_pool_w, v_pool_b, v_pool_scale, v_conv_dw, v_conv_b, v_ln_g, v_ln_b, v_pw_w, v_pw_b, v_w_out, v_final_g):
    proj, w_in_t, w_out_b, pw_w_b, conv_dw_f = _inproj(x, norm_g, w_in.T, w_out, pw_w, conv_dw)
    h1, z, ob, dpool, dh, loss8, gfg8 = _fwd(x, loss_target, proj, pool_w, pool_b, pool_scale, conv_dw_f,
                                             conv_b, ln_g, ln_b, pw_w_b, pw_b, w_out_b, final_g)
    dproj, gwout, gpww, gpoolw, gvec, gdw8 = _bwd1(dh, proj, h1, z, ob, dpool, w_out_b, pw_w_b, pool_w, pool_scale,
                                                   conv_dw_f, ln_g, ln_b)
    win_own, win_staged, g_w_out, g_pw_w, vec, g_conv_dw, poolw_sum = _bwd2a(x, dproj, norm_g, gwout, gpww, loss8, gfg8, gvec,
                                                                             gdw8, gpoolw)
    send_sems, recv_sems, staged_thru, land_thru, token = _rs_start(win_staged)
    grad_x, gng8 = _bwd2b(x, dh, dproj, norm_g, w_in_t, token)
    ng_send, ng_recv, gng8, gng_land = _partials_start(gng8)
    win_others = _rs_wait(send_sems, recv_sems, staged_thru, land_thru, gng8)

    weights = dict(norm_g=norm_g, w_in=w_in.T, pool_w=pool_w, pool_b=pool_b, pool_scale=pool_scale, conv_dw=conv_dw, conv_b=conv_b,
                   ln_g=ln_g, ln_b=ln_b, pw_w=pw_w, pw_b=pw_b, w_out=w_out, final_g=final_g)
    m_in = dict(norm_g=m_norm_g, w_in=m_w_in.T, pool_w=m_pool_w, pool_b=m_pool_b, pool_scale=m_pool_scale, conv_dw=m_conv_dw,
                conv_b=m_conv_b, ln_g=m_ln_g, ln_b=m_ln_b, pw_w=m_pw_w, pw_b=m_pw_b, w_out=m_w_out, final_g=m_final_g)
    v_in = dict(norm_g=v_norm_g, w_in=v_w_in.T, pool_w=v_pool_w, pool_b=v_pool_b, pool_scale=v_pool_scale, conv_dw=v_conv_dw,
                conv_b=v_conv_b, ln_g=v_ln_g, ln_b=v_ln_b, pw_w=v_pw_w, pw_b=v_pw_b, w_out=v_w_out, final_g=v_final_g)
    main, (loss,) = _adamw(functools.partial(_adamw_body), "adamw", MAIN_PARAMS, weights, m_in, v_in,
                           (win_own, win_others, g_w_out, g_pw_w, vec, g_conv_dw, poolw_sum), [jax.ShapeDtypeStruct((1, 1), F32)])
    gng8, gng_others = _partials_wait(ng_send, ng_recv, gng8, gng_land, main[0]["w_out"])
    late, _ = _adamw(functools.partial(_adamw_late_body), "adamw_norm_g", LATE_PARAMS, weights, m_in, v_in, (gng8, gng_others))
    groups = [[(late[k] if p in LATE_PARAMS else main[k])[p] for p in PARAMS] for k in range(4)]
    k_in = PARAMS.index("w_in")
    for group in groups:
        group[k_in] = group[k_in].T
    grads, delta, new_m, new_v = groups
    return (loss.reshape(()), grad_x, *grads, *delta, *new_m, *new_v)
```

```python
import functools

import jax
import jax.numpy as jnp
from jax import lax
from jax.experimental import pallas as pl
from jax.experimental.pallas import tpu as pltpu

F32 = jnp.float32
BF16 = jnp.bfloat16

D_MODEL = 1024
W_MIX = 512
D_IN = 5 * W_MIX
POOL_WINDOWS = (2, 4, 8, 16)
POOL_GC = 128
CONV_WIDTH = 31
RMS_EPS = 1e-6
LN_EPS = 1e-5
N_DEV = 8

LANES = 128
SUBLANES = 8
TILE_M = 512
TILE_IN = 1024
POOL_HALO = 16
CONV_HALO = 32
CONV_ROWS = 256
BWD_CONV_ROWS = 64
BWD_SUBTILES = 2
VMEM_LIMIT = 56 * 1024 * 1024

VMEM = pl.BlockSpec(memory_space=pltpu.VMEM)


def _silu(v):
    return v * jax.nn.sigmoid(v)


def _colsum8(v):
    m, n = v.shape
    return jnp.sum(v.reshape(m // SUBLANES, SUBLANES, n), axis=0)


def _row_index(shape, t0):
    return lax.broadcasted_iota(jnp.int32, shape, 0) + t0


def _conv_taps():
    taps = {s: [] for s in range(SUBLANES)}
    for k in range(CONV_WIDTH):
        e = k + CONV_HALO - (CONV_WIDTH - 1)
        taps[e % SUBLANES].append((e // SUBLANES, k))
    return taps


def _fwd_body(x_ref, tgt_ref, proj_ref, poolw_ref, poolb_ref, pools_ref, dw_ref, cb_ref, lg_ref, lb_ref,
              pww_ref, pwb_ref, wout_ref, fg_ref,
              h1_ref, z_ref, ob_ref, dpool_ref, dh_ref, loss_ref, gfg_ref,
              uext, hext, y_s):
    b = pl.program_id(0)
    i = pl.program_id(1)
    tm = TILE_M

    @pl.when(i == 0)
    def _():
        uext[0:POOL_HALO, :] = jnp.zeros((POOL_HALO, W_MIX), F32)
        hext[0:CONV_HALO, :] = jnp.zeros((CONV_HALO, W_MIX), F32)
        hext[CONV_HALO + tm:, :] = jnp.zeros((SUBLANES, W_MIX), F32)

    @pl.when((b == 0) & (i == 0))
    def _():
        loss_ref[...] = jnp.zeros_like(loss_ref)
        gfg_ref[...] = jnp.zeros_like(gfg_ref)

    uext[POOL_HALO:, :] = proj_ref[0, :, 0:W_MIX]
    t = _row_index((tm, POOL_GC), i * tm)
    for g, w in enumerate(POOL_WINDOWS):
        lanes = slice(g * POOL_GC, (g + 1) * POOL_GC)
        e = uext[:, lanes]
        s = e
        sh = 1
        while sh < w:
            s = s + pltpu.roll(s, sh, 0)
            sh *= 2
        inv_cnt = 1.0 / jnp.minimum(t + 1, w).astype(F32)
        d = (s[POOL_HALO:] * inv_cnt - e[POOL_HALO:]).astype(BF16)
        dpool_ref[0, :, lanes] = d
        z = jnp.dot(d, poolw_ref[g].astype(BF16), preferred_element_type=F32) + poolb_ref[:, lanes]
        z_ref[0, :, lanes] = z
        a_gate = proj_ref[0, :, W_MIX + g * POOL_GC:W_MIX + (g + 1) * POOL_GC]
        y_s[:, lanes] = (z * pools_ref[:, lanes] * _silu(a_gate)).astype(BF16)

    hext[CONV_HALO:CONV_HALO + tm, :] = proj_ref[0, :, 2 * W_MIX:3 * W_MIX] * jax.nn.sigmoid(proj_ref[0, :, 3 * W_MIX:4 * W_MIX])
    taps = _conv_taps()
    for c0 in range(0, tm, CONV_ROWS):
        for l0 in range(0, W_MIX, LANES):
            lanes = slice(l0, l0 + LANES)
            acc = jnp.zeros((CONV_ROWS, LANES), F32) + cb_ref[:, lanes]
            for s in range(SUBLANES):
                part = jnp.zeros((CONV_ROWS + SUBLANES, LANES), F32)
                for q, k in taps[s]:
                    r0 = c0 + SUBLANES * q
                    part = part + hext[r0:r0 + CONV_ROWS + SUBLANES, lanes] * dw_ref[k:k + 1, lanes]
                acc = acc + part[s:s + CONV_ROWS]
            h1_ref[0, c0:c0 + CONV_ROWS, lanes] = acc
    h1 = h1_ref[0]
    mu = jnp.mean(h1, axis=-1, keepdims=True)
    xc = h1 - mu
    rstd = lax.rsqrt(jnp.mean(xc * xc, axis=-1, keepdims=True) + LN_EPS)
    h3 = _silu(xc * rstd * lg_ref[...] + lb_ref[...])
    ob = jnp.dot(h3.astype(BF16), pww_ref[...], preferred_element_type=F32) + pwb_ref[...]
    ob_ref[0] = ob
    y_s[:, W_MIX:] = (ob * _silu(proj_ref[0, :, 4 * W_MIX:5 * W_MIX])).astype(BF16)

    h = x_ref[0] + jnp.dot(y_s[...], wout_ref[...], preferred_element_type=F32)
    r2 = lax.rsqrt(jnp.mean(h * h, axis=-1, keepdims=True) + RMS_EPS)
    hr = h * r2
    err = hr * fg_ref[...] - tgt_ref[0]
    loss_ref[...] += jnp.sum(err * err) * (0.5 / D_MODEL)
    dout = err * (1.0 / D_MODEL)
    gfg_ref[...] += _colsum8(dout * hr)
    gy = dout * fg_ref[...]
    dh_ref[0] = r2 * (gy - hr * jnp.mean(gy * hr, axis=-1, keepdims=True))

    uext[0:POOL_HALO, :] = uext[tm:tm + POOL_HALO, :]
    hext[0:CONV_HALO, :] = hext[tm:tm + CONV_HALO, :]


def _row(v):
    return v.reshape(1, -1)


def _fwd(x, tgt, proj, pool_w, pool_b, pool_scale, conv_dw_f, conv_b, ln_g, ln_b, pw_w_b, pw_b, w_out_b, final_g):
    bl, s, _ = x.shape
    tm = TILE_M
    tok = lambda n: pl.BlockSpec((1, tm, n), lambda b, i: (b, i, 0))
    acc = lambda n: pl.BlockSpec((SUBLANES, n), lambda b, i: (0, 0))
    act = lambda n: jax.ShapeDtypeStruct((bl, s, n), F32)
    return pl.pallas_call(
        functools.partial(_fwd_body),
        name="fwd",
        grid=(bl, s // tm),
        in_specs=[tok(D_MODEL), tok(D_MODEL), tok(D_IN)] + [VMEM] * 11,
        out_specs=[tok(W_MIX), tok(W_MIX), tok(W_MIX), tok(W_MIX), tok(D_MODEL), acc(LANES), acc(D_MODEL)],
        out_shape=[act(W_MIX), act(W_MIX), act(W_MIX), jax.ShapeDtypeStruct((bl, s, W_MIX), BF16), act(D_MODEL),
                   jax.ShapeDtypeStruct((SUBLANES, LANES), F32), jax.ShapeDtypeStruct((SUBLANES, D_MODEL), F32)],
        scratch_shapes=[pltpu.VMEM((POOL_HALO + tm, W_MIX), F32), pltpu.VMEM((CONV_HALO + tm + SUBLANES, W_MIX), F32),
                        pltpu.VMEM((tm, D_MODEL), BF16)],
        compiler_params=pltpu.CompilerParams(dimension_semantics=("arbitrary", "arbitrary"), vmem_limit_bytes=VMEM_LIMIT),
    )(x, tgt, proj, pool_w, pool_b.reshape(1, W_MIX), _row(pool_scale), conv_dw_f, _row(conv_b),
      _row(ln_g), _row(ln_b), pw_w_b, _row(pw_b), w_out_b, _row(final_g))


NT_DIMS = (((1,), (1,)), ((), ()))
TN_DIMS = (((0,), (0,)), ((), ()))
V_POOL_B, V_POOL_SCALE, V_CONV_B, V_LN_G, V_LN_B, V_PW_B = range(6)


def _bwd_taps():
    taps = {s: [] for s in range(SUBLANES)}
    for k in range(CONV_WIDTH):
        e = CONV_WIDTH - 1 - k
        taps[e % SUBLANES].append((e // SUBLANES, k))
    return taps


def _bwd1_body(n_tiles, dh_ref, proj_ref, h1_ref, z_ref, ob_ref, dpool_ref, wout_ref, pww_ref, poolw_ref, pools_ref, dw_ref,
               lg_ref, lb_ref,
               dproj_ref, gwout_ref, gpww_ref, gpoolw_ref, gvec_ref, gdw_ref,
               vext, gext, gsh, y_s):
    b = pl.program_id(0)
    i = pl.program_id(1)
    tm = TILE_M
    t0 = (n_tiles - 1 - i) * tm

    @pl.when(i == 0)
    def _():
        vext[tm:, :] = jnp.zeros((POOL_HALO, W_MIX), F32)
        gext[tm:, :] = jnp.zeros((CONV_HALO, W_MIX), F32)

    @pl.when((b == 0) & (i == 0))
    def _():
        gwout_ref[...] = jnp.zeros_like(gwout_ref)
        gpww_ref[...] = jnp.zeros_like(gpww_ref)
        gpoolw_ref[...] = jnp.zeros_like(gpoolw_ref)
        gvec_ref[...] = jnp.zeros_like(gvec_ref)
        gdw_ref[...] = jnp.zeros_like(gdw_ref)

    sub = tm // BWD_SUBTILES

    def dense_phase(r0):
        rs = slice(r0, r0 + sub)
        dhb = dh_ref[0, rs, :].astype(BF16)
        dy_a = lax.dot_general(dhb, wout_ref[0:W_MIX, :], NT_DIMS, preferred_element_type=F32)
        dy_b = lax.dot_general(dhb, wout_ref[W_MIX:, :], NT_DIMS, preferred_element_type=F32)

        a_gate = proj_ref[0, rs, W_MIX:2 * W_MIX]
        sig_a = jax.nn.sigmoid(a_gate)
        silu_a = a_gate * sig_a
        z = z_ref[0, rs, :]
        out_a = z * pools_ref[...]
        y_s[rs, 0:W_MIX] = (out_a * silu_a).astype(BF16)
        d_out_a = dy_a * silu_a
        d_ag = dy_a * out_a * (sig_a * (1.0 + a_gate * (1.0 - sig_a)))
        dproj_ref[0, rs, W_MIX:2 * W_MIX] = d_ag.astype(BF16)
        gvec_ref[V_POOL_SCALE] += _colsum8(d_out_a * z)
        dz = d_out_a * pools_ref[...]
        gvec_ref[V_POOL_B] += _colsum8(dz)
        t = _row_index((sub, POOL_GC), t0 + r0)
        n_ext = sub + POOL_HALO
        for g, w in enumerate(POOL_WINDOWS):
            lanes = slice(g * POOL_GC, (g + 1) * POOL_GC)
            dzg = dz[:, lanes].astype(BF16)
            gpoolw_ref[g] += lax.dot_general(dpool_ref[0, rs, lanes], dzg, TN_DIMS, preferred_element_type=F32)
            dd = lax.dot_general(dzg, poolw_ref[g].astype(BF16), NT_DIMS, preferred_element_type=F32)
            vext[rs, lanes] = dd * (1.0 / jnp.minimum(t + 1, w).astype(F32))
            s = vext[r0:r0 + n_ext, lanes]
            sh = 1
            while sh < w:
                s = s + pltpu.roll(s, n_ext - sh, 0)
                sh *= 2
            dproj_ref[0, rs, lanes] = (s[0:sub] - dd).astype(BF16)

        b_gate = proj_ref[0, rs, 4 * W_MIX:5 * W_MIX]
        sig_b = jax.nn.sigmoid(b_gate)
        silu_b = b_gate * sig_b
        ob = ob_ref[0, rs, :]
        y_s[rs, W_MIX:] = (ob * silu_b).astype(BF16)
        d_ob = dy_b * silu_b
        dproj_ref[0, rs, 4 * W_MIX:5 * W_MIX] = (dy_b * ob * (sig_b * (1.0 + b_gate * (1.0 - sig_b)))).astype(BF16)
        gvec_ref[V_PW_B] += _colsum8(d_ob)
        gwout_ref[...] += lax.dot_general(y_s[rs, :], dhb, TN_DIMS, preferred_element_type=F32)

        h1 = h1_ref[0, rs, :]
        mu = jnp.mean(h1, axis=-1, keepdims=True)
        xc = h1 - mu
        rstd = lax.rsqrt(jnp.mean(xc * xc, axis=-1, keepdims=True) + LN_EPS)
        xhat = xc * rstd
        h2 = xhat * lg_ref[...] + lb_ref[...]
        sig2 = jax.nn.sigmoid(h2)
        d_obb = d_ob.astype(BF16)
        gpww_ref[...] += lax.dot_general((h2 * sig2).astype(BF16), d_obb, TN_DIMS, preferred_element_type=F32)
        dh2 = lax.dot_general(d_obb, pww_ref[...], NT_DIMS, preferred_element_type=F32) * (sig2 * (1.0 + h2 * (1.0 - sig2)))
        gvec_ref[V_LN_G] += _colsum8(dh2 * xhat)
        gvec_ref[V_LN_B] += _colsum8(dh2)
        dxh = dh2 * lg_ref[...]
        dh1 = rstd * (dxh - jnp.mean(dxh, axis=-1, keepdims=True) - xhat * jnp.mean(dxh * xhat, axis=-1, keepdims=True))
        gvec_ref[V_CONV_B] += _colsum8(dh1)
        gext[rs, :] = dh1

    taps = _bwd_taps()
    cr = BWD_CONV_ROWS
    win = cr + CONV_HALO - SUBLANES

    def conv_phase(r0):
        for c0 in range(r0, r0 + sub, cr):
            for l0 in range(0, W_MIX, LANES):
                lanes = slice(l0, l0 + LANES)
                b_val = proj_ref[0, c0:c0 + cr, 2 * W_MIX + l0:2 * W_MIX + l0 + LANES]
                sg = jax.nn.sigmoid(proj_ref[0, c0:c0 + cr, 3 * W_MIX + l0:3 * W_MIX + l0 + LANES])
                h0c = b_val * sg
                acc = jnp.zeros((cr, LANES), F32)
                for s in range(SUBLANES):
                    gsh[...] = gext[c0 + s:c0 + s + win, lanes]
                    for q, k in taps[s]:
                        gq = gsh[SUBLANES * q:SUBLANES * q + cr, :]
                        acc = acc + gq * dw_ref[k:k + 1, lanes]
                        gdw_ref[k, :, lanes] += _colsum8(h0c * gq)
                d_bval = acc * sg
                dproj_ref[0, c0:c0 + cr, 2 * W_MIX + l0:2 * W_MIX + l0 + LANES] = d_bval.astype(BF16)
                dproj_ref[0, c0:c0 + cr, 3 * W_MIX + l0:3 * W_MIX + l0 + LANES] = (d_bval * b_val * (1.0 - sg)).astype(BF16)

    starts = [k * sub for k in reversed(range(BWD_SUBTILES))]
    for r0 in starts:
        dense_phase(r0)
    for r0 in starts:
        conv_phase(r0)
    vext[tm:, :] = vext[0:POOL_HALO, :]
    gext[tm:, :] = gext[0:CONV_HALO, :]


def _bwd1(dh, proj, h1, z, ob, dpool, w_out_b, pw_w_b, pool_w, pool_scale, conv_dw_f, ln_g, ln_b):
    bl, s, _ = dh.shape
    tm = TILE_M
    nt = s // tm
    tok = lambda n: pl.BlockSpec((1, tm, n), lambda b, i: (b, nt - 1 - i, 0))
    res = lambda shape: pl.BlockSpec(shape, lambda b, i: (0,) * len(shape))
    out_shapes = [(D_MODEL, D_MODEL), (W_MIX, W_MIX), (len(POOL_WINDOWS), POOL_GC, POOL_GC), (6, SUBLANES, W_MIX),
                  (CONV_WIDTH, SUBLANES, W_MIX)]
    return pl.pallas_call(
        functools.partial(_bwd1_body, nt),
        name="bwd1",
        grid=(bl, nt),
        in_specs=[tok(D_MODEL), tok(D_IN), tok(W_MIX), tok(W_MIX), tok(W_MIX), tok(W_MIX)] + [VMEM] * 7,
        out_specs=[tok(D_IN)] + [res(sh) for sh in out_shapes],
        out_shape=[jax.ShapeDtypeStruct((bl, s, D_IN), BF16)] + [jax.ShapeDtypeStruct(sh, F32) for sh in out_shapes],
        scratch_shapes=[pltpu.VMEM((tm + POOL_HALO, W_MIX), F32), pltpu.VMEM((tm + CONV_HALO, W_MIX), F32),
                        pltpu.VMEM((BWD_CONV_ROWS + CONV_HALO - SUBLANES, LANES), F32), pltpu.VMEM((tm, D_MODEL), BF16)],
        compiler_params=pltpu.CompilerParams(dimension_semantics=("arbitrary", "arbitrary"), vmem_limit_bytes=VMEM_LIMIT),
    )(dh, proj, h1, z, ob, dpool, w_out_b, pw_w_b, pool_w, _row(pool_scale), conv_dw_f, _row(ln_g), _row(ln_b))


MESH = pl.DeviceIdType.MESH
W_IN_BLK = D_IN // N_DEV
W_OUT_BLK = D_MODEL // N_DEV
PW_BLK = W_MIX // N_DEV
DW_BLK = W_MIX // N_DEV
N_PEER = N_DEV - 1


SLAB = 2 * W_IN_BLK
N_SLABS = 4
F_SIB, F_X, F_Y, F_ON, F_PASS_X, F_PASS_Y, F_PASS_D = range(7)
W_IN_ONLY, OTHERS = (0,), (1, 2, 3)


def _inproj_body(n_tiles, x_hbm, ng_ref, wint_ref, wout_ref, pww_ref, dw_ref,
                 proj_hbm, wt_hbm, woutf_hbm, pwwf_hbm, dwf_ref,
                 hn_s, wt_s, wout_s, pww_s, dw_blk, stage, xbuf, out_sems, x_sems, w_sems, send_sems, recv_sems):
    s = pl.program_id(0)
    i = pl.program_id(1)
    tm = TILE_IN
    x, y, c = lax.axis_index("x"), lax.axis_index("y"), lax.axis_index("c")
    sibling = (x, y, 1 - c)
    xn, yn, dg = (1 - x, y), (x, 1 - y), (1 - x, 1 - y)
    pick = lambda a, b: tuple(jnp.where(c == 1, p, q) for p, q in zip(a, b))
    on_from, on_to = pick(xn, yn), pick(yn, xn)

    def blocks(px, py, pc):
        k = 4 * px + 2 * py + pc
        rows = lambda ref, n: ref.at[pl.ds(pl.multiple_of(k * n, n), n), :]
        return (rows(wt_s, W_IN_BLK), rows(wout_s, W_OUT_BLK), rows(pww_s, PW_BLK), dw_blk.at[k])

    def copies(arrays, flow, block, to):
        refs = blocks(*block)
        return [pltpu.make_async_remote_copy(src_ref=refs[a], dst_ref=refs[a], send_sem=send_sems.at[a, flow],
                                             recv_sem=recv_sems.at[a, flow], device_id=to, device_id_type=MESH)
                for a in arrays]

    def start(arrays):
        for flow, to in ((F_SIB, sibling), (F_X, (*xn, c)), (F_Y, (*yn, c))):
            for cp in copies(arrays, flow, (x, y, c), to):
                cp.start()

    def direct(arrays):
        for flow, chip in ((F_X, xn), (F_Y, yn)):
            for cp in copies(arrays, flow, (*chip, c), sibling):
                cp.wait_recv()
        for cp in copies(arrays, F_ON, (*on_from, c), (*on_to, c)):
            cp.start()
        for flow, chip in ((F_PASS_X, xn), (F_PASS_Y, yn)):
            for cp in copies(arrays, flow, (*chip, c), sibling):
                cp.start()

    def diagonal(arrays):
        for cp in copies(arrays, F_ON, (*dg, c), sibling):
            cp.wait_recv()
        for cp in copies(arrays, F_PASS_D, (*dg, c), sibling):
            cp.start()

    def from_sibling(arrays, flow, chip):
        for cp in copies(arrays, flow, (*chip, 1 - c), sibling):
            cp.wait_recv()

    def finish(arrays):
        mine = (x, y, c)
        for flow, block in ((F_SIB, mine), (F_X, mine), (F_Y, mine), (F_ON, (*on_from, c)), (F_PASS_X, (*xn, c)),
                            (F_PASS_Y, (*yn, c)), (F_PASS_D, (*dg, c))):
            for cp in copies(arrays, flow, block, sibling):
                cp.wait_send()

    step = s * n_tiles + i
    par = step % 2
    first = step == 0

    @pl.when(first)
    def _():
        mine = blocks(x, y, c)
        mine[0][...] = wint_ref[...].astype(BF16)
        mine[1][...] = wout_ref[...].astype(BF16)
        mine[2][...] = pww_ref[...].astype(BF16)
        mine[3][...] = dw_ref[...]
        start(W_IN_ONLY)
        start(OTHERS)

    rows_i = pl.ds(pl.multiple_of(i * tm, tm), tm)
    tiles_per_seq = proj_hbm.shape[1] // tm

    def x_copy(k):
        src = x_hbm.at[k // tiles_per_seq, pl.ds(pl.multiple_of((k % tiles_per_seq) * tm, tm), tm), :]
        return pltpu.make_async_copy(src, xbuf.at[k % 2], x_sems.at[k % 2])

    @pl.when(first)
    def _():
        x_copy(0).start()

    @pl.when(s == 0)
    def _():
        x_copy(i).wait()

        @pl.when(i + 1 < n_tiles)
        def _():
            x_copy(i + 1).start()

        xt = xbuf[i % 2]
        r1 = lax.rsqrt(jnp.mean(xt * xt, axis=-1, keepdims=True) + RMS_EPS)
        hn_s[rows_i, :] = (xt * r1 * ng_ref[...]).astype(BF16)

    @pl.when(first)
    def _():
        from_sibling(W_IN_ONLY, F_SIB, (x, y))

    @pl.when((s == 1) & (i == 0))
    def _():
        direct(W_IN_ONLY)
        from_sibling(W_IN_ONLY, F_PASS_Y, yn)

    @pl.when((s == 2) & (i == 0))
    def _():
        from_sibling(W_IN_ONLY, F_PASS_X, xn)

    weights_out = [pltpu.make_async_copy(src, dst, w_sems.at[k])
                   for k, (src, dst) in enumerate(((wt_s, wt_hbm), (wout_s, woutf_hbm), (pww_s, pwwf_hbm)))]

    @pl.when((s == 3) & (i == 0))
    def _():
        diagonal(W_IN_ONLY)
        direct(OTHERS)
        from_sibling(W_IN_ONLY, F_PASS_D, dg)
        weights_out[0].start()

    def out_copy(parity, col0):
        dst = proj_hbm.at[i // tiles_per_seq, pl.ds(pl.multiple_of((i % tiles_per_seq) * tm, tm), tm),
                          pl.ds(pl.multiple_of(col0, LANES), SLAB)]
        return pltpu.make_async_copy(stage.at[parity], dst, out_sems.at[parity])

    @pl.when(step >= 2)
    def _():
        out_copy(par, 0).wait()

    chip = jnp.where(s < 2, 2 * x, 2 * (1 - x)) + jnp.where((s == 0) | (s == 2), y, 1 - y)
    slab = wt_s[pl.ds(pl.multiple_of(chip * SLAB, SLAB), SLAB), :]
    stage[par] = lax.dot_general(hn_s[rows_i, :], slab, NT_DIMS, preferred_element_type=F32)
    out_copy(par, chip * SLAB).start()

    @pl.when(step == N_SLABS * n_tiles - 1)
    def _():
        out_copy(par, 0).wait()
        out_copy(1 - par, 0).wait()
        from_sibling(OTHERS, F_SIB, (x, y))
        from_sibling(OTHERS, F_PASS_X, xn)
        from_sibling(OTHERS, F_PASS_Y, yn)
        diagonal(OTHERS)
        from_sibling(OTHERS, F_PASS_D, dg)
        for cp in weights_out[1:]:
            cp.start()
        finish(W_IN_ONLY)
        finish(OTHERS)
        for cp in weights_out:
            cp.wait()
        for k in range(N_DEV):
            dwf_ref[:, k * DW_BLK:(k + 1) * DW_BLK] = dw_blk[k]


def _inproj(x, norm_g, w_in_t, w_out, pw_w, conv_dw):
    bl, s, _ = x.shape
    tm = TILE_IN
    nt = s // tm
    n_tiles = bl * nt
    dma = pltpu.SemaphoreType.DMA
    return pl.pallas_call(
        functools.partial(_inproj_body, n_tiles),
        name="inproj",
        grid=(N_SLABS, n_tiles),
        in_specs=[pl.BlockSpec(memory_space=pl.ANY)] + [VMEM] * 5,
        out_specs=[pl.BlockSpec(memory_space=pl.ANY)] * 4 + [VMEM],
        out_shape=[jax.ShapeDtypeStruct((bl, s, D_IN), F32), jax.ShapeDtypeStruct((D_IN, D_MODEL), BF16),
                   jax.ShapeDtypeStruct((D_MODEL, D_MODEL), BF16), jax.ShapeDtypeStruct((W_MIX, W_MIX), BF16),
                   jax.ShapeDtypeStruct((CONV_WIDTH, W_MIX), F32)],
        scratch_shapes=[pltpu.VMEM((n_tiles * tm, D_MODEL), BF16), pltpu.VMEM((D_IN, D_MODEL), BF16),
                        pltpu.VMEM((D_MODEL, D_MODEL), BF16), pltpu.VMEM((W_MIX, W_MIX), BF16),
                        pltpu.VMEM((N_DEV, CONV_WIDTH, DW_BLK), F32), pltpu.VMEM((2, tm, SLAB), F32),
                        pltpu.VMEM((2, tm, D_MODEL), F32), dma((2,)), dma((2,)), dma((3,)), dma((4, 7)), dma((4, 7))],
        compiler_params=pltpu.CompilerParams(dimension_semantics=("arbitrary", "arbitrary"), vmem_limit_bytes=VMEM_LIMIT),
    )(x, _row(norm_g), w_in_t, w_out, pw_w, conv_dw)


B_PLAIN, B_ON, B_MERGED = range(3)


def _routes():
    x, y, c = lax.axis_index("x"), lax.axis_index("y"), lax.axis_index("c")
    xn, yn = (1 - x, y), (x, 1 - y)
    pick = lambda a, b: tuple(jnp.where(c == 1, p, q) for p, q in zip(a, b))
    return pick(xn, yn), pick(yn, xn), (1 - x, 1 - y), (x, y), c


def _chip_index(chip):
    return 2 * chip[0] + chip[1]


def _copy(src, dst, send_sems, recv_sems, a, k, to):
    return pltpu.make_async_remote_copy(src_ref=src, dst_ref=dst, send_sem=send_sems.at[a, k], recv_sem=recv_sems.at[a, k],
                                        device_id=to, device_id_type=MESH)


def _rsb_send(plain, on, rb_plain, rb_on, send_b, recv_b):
    first, _, _, _, c = _routes()
    for k, srcs, dsts in ((B_ON, on, rb_on), (B_PLAIN, plain, rb_plain)):
        for a, (s, d) in enumerate(zip(srcs, dsts)):
            _copy(s, d, send_b, recv_b, a, k, (*first, c)).start()


def _rsb_merge(second_sums, merged, rb_on, rb_merged, send_b, recv_b):
    _, second, _, _, c = _routes()
    for a, (own, m, got, dst) in enumerate(zip(second_sums, merged, rb_on, rb_merged)):
        _copy(got, got, send_b, recv_b, a, B_ON, (*second, c)).wait_recv()
        m[...] = (own[...] + got[...].astype(F32)).astype(BF16)
        _copy(m, dst, send_b, recv_b, a, B_MERGED, (*second, c)).start()


def _rsb_total(own_sums, plain, on, merged, rb_plain, rb_merged, send_b, recv_b):
    _, second, _, _, c = _routes()
    out = []
    for a, (own, p, g) in enumerate(zip(own_sums, rb_plain, rb_merged)):
        _copy(p, p, send_b, recv_b, a, B_PLAIN, (*second, c)).wait_recv()
        _copy(g, g, send_b, recv_b, a, B_MERGED, (*second, c)).wait_recv()
        out.append(own[...] + p[...].astype(F32) + g[...].astype(F32))
    for k, srcs in ((B_PLAIN, plain), (B_ON, on), (B_MERGED, merged)):
        for a, s in enumerate(srcs):
            _copy(s, s, send_b, recv_b, a, k, (*second, c)).wait_send()
    return out


R_LOSS, R_FINAL_G, R_POOL, R_CONV, R_LN = range(5)
VEC_ROWS = 16
DW_ROWS = 32
A_PART, A_X, A_Y, A_ON, A_HALF = range(5)
HALF_DTYPES = (F32, BF16, BF16)
STEP_SUMS, STEP_MERGE, STEP_HALF = 1, 3, 5


def _bwd2a_body(n_steps, x_ref, dproj_ref, ng_ref, gwout_hbm, gpww_hbm, loss_ref, gfg_ref, gvec_ref, gdw_ref, gpoolw_ref,
                win_own_ref, win_staged_ref, owout_ref, opww_ref, ovec_ref, odw_ref, opoolw_ref,
                acc, st_win, a_win, own_wout, own_pww, a_wout, a_pww, st_wout, st_pww, rb_wout, rb_pww, part_vec, part_dw,
                part_pw, sib_vec, sib_dw, sib_pw, res_vec, res_dw, res_pw, half_vec, half_dw, half_pw,
                own_sems, send_a, recv_a, send_b, recv_b, send_r, recv_r, send_w, recv_w):
    step = pl.program_id(0) * pl.num_programs(1) + pl.program_id(1)
    x, y, c = lax.axis_index("x"), lax.axis_index("y"), lax.axis_index("c")
    sibling = (x, y, 1 - c)
    xn, yn = (1 - x, y), (x, 1 - y)
    first, second, diag, own, _ = _routes()
    rows = lambda ref, k, n: ref.at[pl.ds(pl.multiple_of(k * n, n), n), :]

    blocks = lambda k: (rows(gwout_hbm, k, W_OUT_BLK), rows(gpww_hbm, k, PW_BLK))
    mine = lambda chip: (own_wout.at[chip], own_pww.at[chip])
    mine_copies = lambda chip: [pltpu.make_async_copy(s, d, own_sems.at[a, chip])
                                for a, (s, d) in enumerate(zip(blocks(2 * chip + c), mine(chip)))]
    landed = lambda chip: (a_wout.at[chip], a_pww.at[chip])
    staged = lambda k: (st_wout.at[k], st_pww.at[k])
    arrived = lambda k: (rb_wout.at[k], rb_pww.at[k])

    parts, sibs, halves = (part_vec, part_dw, part_pw), (sib_vec, sib_dw, sib_pw), (half_vec, half_dw, half_pw)
    results = (res_vec, res_dw, res_pw)
    half_rows = lambda ref: pl.ds(pl.multiple_of(c * (ref.shape[0] // 2), SUBLANES), ref.shape[0] // 2)
    slot = lambda chip: tuple(h.at[_chip_index(chip)] for h in halves)

    @pl.when(step == 0)
    def _():
        acc[...] = jnp.zeros_like(acc)
        sum8 = lambda v: jnp.sum(v, axis=0, keepdims=True)
        part_vec[...] = jnp.zeros((VEC_ROWS, D_MODEL), F32)
        part_vec[R_LOSS:R_LOSS + 1, :] = jnp.broadcast_to(loss_ref[0:1, 0:1], (1, D_MODEL))
        part_vec[R_FINAL_G:R_FINAL_G + 1, :] = sum8(gfg_ref[...])
        for row, (lo, hi) in ((R_POOL, (V_POOL_B, V_POOL_SCALE)), (R_CONV, (V_CONV_B, V_LN_G)), (R_LN, (V_LN_B, V_PW_B))):
            part_vec[row:row + 1, 0:W_MIX] = sum8(gvec_ref[lo])
            part_vec[row:row + 1, W_MIX:] = sum8(gvec_ref[hi])
        part_dw[0:CONV_WIDTH, :] = jnp.sum(gdw_ref[...], axis=1)
        part_dw[CONV_WIDTH:, :] = jnp.zeros((DW_ROWS - CONV_WIDTH, W_MIX), F32)
        part_pw[...] = gpoolw_ref[...].reshape(len(POOL_WINDOWS) * POOL_GC, POOL_GC)
        for a, (p, s) in enumerate(zip(parts, sibs)):
            _copy(p, s, send_r, recv_r, a, A_PART, sibling).start()
        for chip in range(4):
            for a, (s, d) in enumerate(zip(blocks(2 * chip + (1 - c)), landed(chip))):
                _copy(s, d, send_a, recv_a, a, chip, sibling).start()
            for cp in mine_copies(chip):
                cp.start()

    x_t = x_ref[0]
    r1 = lax.rsqrt(jnp.mean(x_t * x_t, axis=-1, keepdims=True) + RMS_EPS)
    hn = (x_t * r1 * ng_ref[...]).astype(BF16)
    acc[...] += lax.dot_general(dproj_ref[0], hn, TN_DIMS, preferred_element_type=F32)

    @pl.when(step == STEP_SUMS)
    def _():
        for a, (p, s, h) in enumerate(zip(parts, sibs, slot(own))):
            _copy(p, s, send_r, recv_r, a, A_PART, sibling).wait_recv()
            h[...] = (p[half_rows(p), :] + s[half_rows(s), :]).astype(h.dtype)
            for k, to in ((A_X, xn), (A_Y, yn)):
                _copy(h, h, send_r, recv_r, a, k, (*to, c)).start()
        for chip in range(4):
            for cp in mine_copies(chip):
                cp.wait()
            for a, (m, l) in enumerate(zip(mine(chip), landed(chip))):
                _copy(l, l, send_a, recv_a, a, chip, sibling).wait_recv()
                l[...] = m[...] + l[...]
        for k, chip in ((B_ON, diag), (B_PLAIN, first)):
            for s, l in zip(staged(k), landed(_chip_index(chip))):
                s[...] = l[...].astype(BF16)
        _rsb_send(staged(B_PLAIN), staged(B_ON), arrived(B_PLAIN), arrived(B_ON), send_b, recv_b)

    @pl.when(step == STEP_MERGE)
    def _():
        _rsb_merge(landed(_chip_index(second)), staged(B_MERGED), arrived(B_ON), arrived(B_MERGED), send_b, recv_b)
        for a in range(3):
            for k, chip in ((A_X, xn), (A_Y, yn)):
                h = slot(chip)[a]
                _copy(h, h, send_r, recv_r, a, k, sibling).wait_recv()
            h = slot(first)[a]
            _copy(h, h, send_r, recv_r, a, A_ON, (*second, c)).start()

    @pl.when(step == STEP_HALF)
    def _():
        for a, (r, hs) in enumerate(zip(results, halves)):
            h = slot(diag)[a]
            _copy(h, h, send_r, recv_r, a, A_ON, sibling).wait_recv()
            mine = r.at[half_rows(r), :]
            chip_sums = [hs[k].astype(F32) for k in range(4)]
            mine[...] = (chip_sums[0] + chip_sums[1]) + (chip_sums[2] + chip_sums[3])
            _copy(mine, mine, send_r, recv_r, a, A_HALF, sibling).start()

    @pl.when(step == n_steps - 1)
    def _():
        block = lambda k: rows(acc, k, W_IN_BLK)
        w_copy = lambda chip: _copy(st_win.at[chip], a_win.at[chip], send_w, recv_w, 0, chip, sibling)
        for chip in range(4):
            st_win[chip] = block(2 * chip + (1 - c))[...].astype(BF16)
            w_copy(chip).start()

        owout_ref[...], opww_ref[...] = _rsb_total(landed(_chip_index(own)), staged(B_PLAIN), staged(B_ON), staged(B_MERGED),
                                                   arrived(B_PLAIN), arrived(B_MERGED), send_b, recv_b)
        for a, (r, p, s) in enumerate(zip(results, parts, sibs)):
            theirs = r.at[pl.ds(pl.multiple_of((1 - c) * (r.shape[0] // 2), SUBLANES), r.shape[0] // 2), :]
            _copy(theirs, theirs, send_r, recv_r, a, A_HALF, sibling).wait_recv()
            _copy(p, s, send_r, recv_r, a, A_PART, sibling).wait_send()
            for k, h in ((A_X, slot(own)[a]), (A_Y, slot(own)[a]), (A_ON, slot(first)[a])):
                _copy(h, h, send_r, recv_r, a, k, sibling).wait_send()
            mine = r.at[half_rows(r), :]
            _copy(mine, mine, send_r, recv_r, a, A_HALF, sibling).wait_send()
        for chip in range(4):
            for a, (s, d) in enumerate(zip(blocks(2 * chip + (1 - c)), landed(chip))):
                _copy(s, d, send_a, recv_a, a, chip, sibling).wait_send()
        ovec_ref[...] = res_vec[...]
        opoolw_ref[...] = res_pw[...]
        for k in range(N_DEV):
            @pl.when(4 * x + 2 * y + c == k)
            def _():
                odw_ref[...] = res_dw[0:CONV_WIDTH, k * DW_BLK:(k + 1) * DW_BLK]

        for chip in range(4):
            w_copy(chip).wait()
        chip_sum = lambda chip: block(2 * _chip_index(chip) + c)[...] + a_win[_chip_index(chip)].astype(F32)
        win_own_ref[...] = chip_sum(own)
        for j, chip in enumerate((xn, yn, diag)):
            win_staged_ref[j] = chip_sum(chip).astype(BF16)


def _bwd2a(x, dproj, norm_g, gwout, gpww, loss8, gfg8, gvec, gdw8, gpoolw):
    bl, s, _ = x.shape
    tm = TILE_M
    nt = s // tm
    tok = lambda n: pl.BlockSpec((1, tm, n), lambda b, i: (b, i, 0))
    n_pw = len(POOL_WINDOWS) * POOL_GC
    win = (W_IN_BLK, D_MODEL)
    blk = ((W_OUT_BLK, D_MODEL), (PW_BLK, W_MIX))
    rep = ((VEC_ROWS, D_MODEL), (DW_ROWS, W_MIX), (n_pw, POOL_GC))
    half = lambda sh: (sh[0] // 2,) + sh[1:]
    dma = pltpu.SemaphoreType.DMA
    assert bl * nt > STEP_HALF + 1, "the exchanges' phases need their grid steps"
    return pl.pallas_call(
        functools.partial(_bwd2a_body, bl * nt),
        name="bwd2a",
        grid=(bl, nt),
        in_specs=[tok(D_MODEL), tok(D_IN), VMEM] + [pl.BlockSpec(memory_space=pl.ANY)] * 2 + [VMEM] * 5,
        out_specs=[VMEM] * 7,
        out_shape=[jax.ShapeDtypeStruct(win, F32), jax.ShapeDtypeStruct((3,) + win, BF16)]
        + [jax.ShapeDtypeStruct(sh, F32) for sh in blk + (rep[0], (CONV_WIDTH, DW_BLK), rep[2])],
        scratch_shapes=[pltpu.VMEM((D_IN, D_MODEL), F32), pltpu.VMEM((4,) + win, BF16), pltpu.VMEM((4,) + win, BF16)]
        + [pltpu.VMEM((4,) + sh, F32) for sh in blk] * 2
        + [pltpu.VMEM((3,) + sh, BF16) for sh in blk] * 2
        + [pltpu.VMEM(sh, F32) for sh in rep] * 3
        + [pltpu.VMEM((4,) + half(sh), dt) for sh, dt in zip(rep, HALF_DTYPES)]
        + [dma((2, 4)), dma((2, 4)), dma((2, 4)), dma((2, 3)), dma((2, 3)), dma((3, 5)), dma((3, 5)), dma((1, 4)), dma((1, 4))],
        compiler_params=pltpu.CompilerParams(dimension_semantics=("arbitrary", "arbitrary"), vmem_limit_bytes=VMEM_LIMIT),
    )(x, dproj, _row(norm_g), gwout, gpww, loss8, gfg8, gvec, gdw8, gpoolw)


HBM = pl.BlockSpec(memory_space=pltpu.HBM)
SEM = pl.BlockSpec(memory_space=pltpu.SEMAPHORE)
EFFECT = pltpu.SideEffectType.DATAFLOW_SIDE_EFFECTING


def _owner_copies(src_ref, land_ref, send_sems, recv_sems):
    x, y, c = lax.axis_index("x"), lax.axis_index("y"), lax.axis_index("c")
    owners = ((1 - x, y, c), (x, 1 - y, c), (1 - x, 1 - y, c))
    return [pltpu.make_async_remote_copy(src_ref=src_ref.at[j], dst_ref=land_ref.at[j], send_sem=send_sems.at[j],
                                         recv_sem=recv_sems.at[j], device_id=owners[j], device_id_type=MESH)
            for j in range(3)]


def _rs_start(staged):
    def body(src_ref, land_ref, send_sems, recv_sems, src_thru, land_thru, token):
        for cp in _owner_copies(src_ref, land_ref, send_sems, recv_sems):
            cp.start()
        token[...] = jnp.zeros_like(token)

    dma = pltpu.SemaphoreType.DMA
    return pl.pallas_call(
        body, name="rs_start",
        out_shape=(dma((3,)), dma((3,)), pltpu.HBM(staged.shape, staged.dtype), pltpu.HBM(staged.shape, staged.dtype),
                   jax.ShapeDtypeStruct((SUBLANES, LANES), F32)),
        in_specs=(HBM, HBM), out_specs=(SEM, SEM, HBM, HBM, VMEM), input_output_aliases={0: 2, 1: 3},
        compiler_params=pltpu.CompilerParams(has_side_effects=EFFECT),
    )(pltpu.with_memory_space_constraint(staged, pltpu.HBM),
      pltpu.with_memory_space_constraint(lax.empty(staged.shape, staged.dtype), pltpu.HBM))


def _rs_wait(send_sems, recv_sems, src_thru, land_thru, after):
    def body(src_ref, land_ref, send_sems, recv_sems, after_ref, src_dead, got_ref):
        for cp in _owner_copies(src_ref, land_ref, send_sems, recv_sems):
            cp.wait_send()
            cp.wait_recv()

    return pl.pallas_call(
        body, name="rs_wait",
        out_shape=(pltpu.HBM(src_thru.shape, src_thru.dtype), pltpu.HBM(land_thru.shape, land_thru.dtype)),
        in_specs=(HBM, HBM, SEM, SEM, pl.BlockSpec(memory_space=pl.ANY)), out_specs=(HBM, HBM),
        input_output_aliases={0: 0, 1: 1},
        compiler_params=pltpu.CompilerParams(has_side_effects=EFFECT),
    )(src_thru, land_thru, send_sems, recv_sems, after)[1]


RING = 3


def _bwd2b_body(n_steps, tiles_per_seq, x_hbm, dh_hbm, dproj_hbm, ng_ref, wint_ref, token_ref, gx_ref, gng_ref,
                x_ring, dh_ring, dproj_ring, sems):
    step = pl.program_id(0) * pl.num_programs(1) + pl.program_id(1)
    tm = TILE_M

    def tile_copies(k):
        k = jnp.asarray(k, jnp.int32)
        slot = k % RING
        rows = pl.ds(pl.multiple_of((k % tiles_per_seq) * tm, tm), tm)
        return [pltpu.make_async_copy(src.at[k // tiles_per_seq, rows, :], ring.at[slot], sems.at[j, slot])
                for j, (src, ring) in enumerate(((x_hbm, x_ring), (dh_hbm, dh_ring), (dproj_hbm, dproj_ring)))]

    @pl.when(step == 0)
    def _():
        gng_ref[...] = jnp.zeros_like(gng_ref) + token_ref[0:1, 0:1]
        for k in range(RING - 1):
            for cp in tile_copies(k):
                cp.start()

    @pl.when(step + RING - 1 < n_steps)
    def _():
        for cp in tile_copies(step + RING - 1):
            cp.start()

    for cp in tile_copies(step):
        cp.wait()
    slot = step % RING
    x_t = x_ring[slot]
    r1 = lax.rsqrt(jnp.mean(x_t * x_t, axis=-1, keepdims=True) + RMS_EPS)
    xr = x_t * r1
    dhn = jnp.dot(dproj_ring[slot], wint_ref[...], preferred_element_type=F32)
    gng_ref[...] += _colsum8(dhn * xr)
    gy = dhn * ng_ref[...]
    gx_ref[0] = dh_ring[slot] + r1 * (gy - xr * jnp.mean(gy * xr, axis=-1, keepdims=True))


def _bwd2b(x, dh, dproj, norm_g, w_in_t, token):
    bl, s, _ = x.shape
    tm = TILE_M
    nt = s // tm
    assert bl * nt >= RING
    hbm = pl.BlockSpec(memory_space=pl.ANY)
    return pl.pallas_call(
        functools.partial(_bwd2b_body, bl * nt, nt),
        name="bwd2b",
        grid=(bl, nt),
        in_specs=[hbm, hbm, hbm, VMEM, VMEM, VMEM],
        out_specs=[pl.BlockSpec((1, tm, D_MODEL), lambda b, i: (b, i, 0)),
                   pl.BlockSpec((SUBLANES, D_MODEL), lambda b, i: (0, 0))],
        out_shape=[jax.ShapeDtypeStruct((bl, s, D_MODEL), F32), jax.ShapeDtypeStruct((SUBLANES, D_MODEL), F32)],
        scratch_shapes=[pltpu.VMEM((RING, tm, D_MODEL), F32), pltpu.VMEM((RING, tm, D_MODEL), F32),
                        pltpu.VMEM((RING, tm, D_IN), BF16), pltpu.SemaphoreType.DMA((3, RING))],
        compiler_params=pltpu.CompilerParams(dimension_semantics=("arbitrary", "arbitrary"), vmem_limit_bytes=VMEM_LIMIT),
    )(x, dh, dproj, _row(norm_g), w_in_t, token)


def _partial_copies(part_ref, land_ref, send_sems, recv_sems):
    x, y, c = lax.axis_index("x"), lax.axis_index("y"), lax.axis_index("c")
    other_chips = [(1 - x, y), (x, 1 - y), (1 - x, 1 - y)]
    peers = [(x, y, 1 - c)] + [(*chip, c) for chip in other_chips] + [(*chip, 1 - c) for chip in other_chips]
    return [pltpu.make_async_remote_copy(src_ref=part_ref, dst_ref=land_ref.at[4 * x + 2 * y + c], send_sem=send_sems.at[r],
                                         recv_sem=recv_sems.at[r], device_id=peers[r], device_id_type=MESH)
            for r in range(N_PEER)]


def _partials_start(part):
    def body(part_ref, land_ref, send_sems, recv_sems, part_thru, land_thru):
        for cp in _partial_copies(part_ref, land_ref, send_sems, recv_sems):
            cp.start()

    dma = pltpu.SemaphoreType.DMA
    land = (N_DEV,) + part.shape
    return pl.pallas_call(
        body, name="partials_start",
        out_shape=(dma((N_PEER,)), dma((N_PEER,)), pltpu.HBM(part.shape, part.dtype), pltpu.HBM(land, part.dtype)),
        in_specs=(HBM, HBM), out_specs=(SEM, SEM, HBM, HBM), input_output_aliases={0: 2, 1: 3},
        compiler_params=pltpu.CompilerParams(has_side_effects=EFFECT),
    )(pltpu.with_memory_space_constraint(part, pltpu.HBM),
      pltpu.with_memory_space_constraint(lax.empty(land, part.dtype), pltpu.HBM))


def _partials_wait(send_sems, recv_sems, part_thru, land_thru, after):
    def body(part_ref, land_ref, send_sems, recv_sems, after_ref, part_out, land_out):
        for cp in _partial_copies(part_ref, land_ref, send_sems, recv_sems):
            cp.wait_send()
            cp.wait_recv()

    return pl.pallas_call(
        body, name="partials_wait",
        out_shape=(pltpu.HBM(part_thru.shape, part_thru.dtype), pltpu.HBM(land_thru.shape, land_thru.dtype)),
        in_specs=(HBM, HBM, SEM, SEM, pl.BlockSpec(memory_space=pl.ANY)), out_specs=(HBM, HBM),
        input_output_aliases={0: 0, 1: 1},
        compiler_params=pltpu.CompilerParams(has_side_effects=EFFECT),
    )(part_thru, land_thru, send_sems, recv_sems, after)


ADAM_LR = 0.001
ADAM_B1 = 0.9
ADAM_B2 = 0.999
ADAM_EPS = 1e-08
ADAM_WD = 0.01
ADAM_STEP = 10


PARAMS = ("norm_g", "w_in", "pool_w", "pool_b", "pool_scale", "conv_dw", "conv_b", "ln_g", "ln_b", "pw_w", "pw_b", "w_out",
          "final_g")


LATE_PARAMS = ("norm_g",)
MAIN_PARAMS = tuple(p for p in PARAMS if p not in LATE_PARAMS)


def _adamw_refs(params, n_sources, refs):
    n = len(params)
    ws, ms, vs = (dict(zip(params, refs[k * n:(k + 1) * n])) for k in range(3))
    outs = refs[3 * n + n_sources:]
    gs, ds, m2s, v2s = (dict(zip(params, outs[k * n:(k + 1) * n])) for k in range(4))

    def update(name, g, at=slice(None)):
        m = ADAM_B1 * ms[name][at] + (1.0 - ADAM_B1) * g
        v = ADAM_B2 * vs[name][at] + (1.0 - ADAM_B2) * (g * g)
        m_hat = m / (1.0 - ADAM_B1 ** ADAM_STEP)
        v_hat = v / (1.0 - ADAM_B2 ** ADAM_STEP)
        gs[name][at] = g
        ds[name][at] = -ADAM_LR * (m_hat / (jnp.sqrt(v_hat) + ADAM_EPS) + ADAM_WD * ws[name][at])
        m2s[name][at] = m
        v2s[name][at] = v

    return update, refs[3 * n:3 * n + n_sources], outs[4 * n:]


def _adamw_late_body(*refs):
    update, (gng_ref, gng_others_ref), _ = _adamw_refs(LATE_PARAMS, 2, refs)
    me = 4 * lax.axis_index("x") + 2 * lax.axis_index("y") + lax.axis_index("c")
    g_norm = jnp.zeros(gng_ref.shape, F32)
    for k in range(N_DEV):
        g_norm = g_norm + jnp.where(me == k, gng_ref[...], gng_others_ref[k])
    update("norm_g", jnp.sum(g_norm, axis=0, keepdims=True))


def _adamw_body(*refs):
    update, sources, (loss_ref,) = _adamw_refs(MAIN_PARAMS, 7, refs)
    gwin_ref, gwin_others_ref, gwout_ref, gpww_ref, vec_ref, gdw_ref, gpoolw_ref = sources
    ws = dict(zip(MAIN_PARAMS, refs))
    row = lambda r, lo, hi: vec_ref[r:r + 1, lo:hi]
    loss_ref[...] = row(R_LOSS, 0, 1)
    g_w_in = gwin_ref[...]
    for j in range(3):
        g_w_in = g_w_in + gwin_others_ref[j].astype(F32)
    update("w_in", g_w_in)
    update("pool_w", gpoolw_ref[...].reshape(ws["pool_w"].shape))
    for g in range(len(POOL_WINDOWS)):
        update("pool_b", row(R_POOL, g * POOL_GC, (g + 1) * POOL_GC), at=slice(g, g + 1))
    update("pool_scale", row(R_POOL, W_MIX, 2 * W_MIX))
    update("conv_dw", gdw_ref[...])
    update("conv_b", row(R_CONV, 0, W_MIX))
    update("ln_g", row(R_CONV, W_MIX, 2 * W_MIX))
    update("ln_b", row(R_LN, 0, W_MIX))
    update("pw_w", gpww_ref[...])
    update("pw_b", row(R_LN, W_MIX, 2 * W_MIX))
    update("w_out", gwout_ref[...])
    update("final_g", row(R_FINAL_G, 0, D_MODEL))


def _adamw(body, name, params, ws, ms, vs, grad_sources, extra_out_shapes=()):
    n = len(params)
    flat = lambda a: a.reshape(1, -1) if a.ndim == 1 else a
    args = [flat(d[p]) for d in (ws, ms, vs) for p in params]
    outs = pl.pallas_call(
        body,
        name=name,
        in_specs=[VMEM] * (3 * n + len(grad_sources)),
        out_specs=[VMEM] * (4 * n + len(extra_out_shapes)),
        out_shape=[jax.ShapeDtypeStruct(a.shape, F32) for a in args[:n]] * 4 + list(extra_out_shapes),
        compiler_params=pltpu.CompilerParams(vmem_limit_bytes=VMEM_LIMIT),
    )(*args, *grad_sources)
    groups = tuple({p: outs[k * n + j].reshape(ws[p].shape) for j, p in enumerate(params)} for k in range(4))
    return groups, outs[4 * n:]


def kernel(x, norm_g, w_in, pool_w, pool_b, pool_scale, conv_dw, conv_b, ln_g, ln_b, pw_w, pw_b, w_out, final_g, loss_target, m_norm_g, m_w_in, m_pool_w, m_pool_b, m_pool_scale, m_conv_dw, m_conv_b, m_ln_g, m_ln_b, m_pw_w, m_pw_b, m_w_out, m_final_g, v_norm_g, v_w_in, v_pool_w, v_pool_b, v_pool_scale, v_conv_dw, v_conv_b, v_ln_g, v_ln_b, v_pw_w, v_pw_b, v_w_out, v_final_g):
    proj, w_in_t, w_out_b, pw_w_b, conv_dw_f = _inproj(x, norm_g, w_in.T, w_out, pw_w, conv_dw)
    h1, z, ob, dpool, dh, loss8, gfg8 = _fwd(x, loss_target, proj, pool_w, pool_b, pool_scale, conv_dw_f,
                                             conv_b, ln_g, ln_b, pw_w_b, pw_b, w_out_b, final_g)
    dproj, gwout, gpww, gpoolw, gvec, gdw8 = _bwd1(dh, proj, h1, z, ob, dpool, w_out_b, pw_w_b, pool_w, pool_scale,
                                                   conv_dw_f, ln_g, ln_b)
    win_own, win_staged, g_w_out, g_pw_w, vec, g_conv_dw, poolw_sum = _bwd2a(x, dproj, norm_g, gwout, gpww, loss8, gfg8, gvec,
                                                                             gdw8, gpoolw)
    send_sems, recv_sems, staged_thru, land_thru, token = _rs_start(win_staged)
    grad_x, gng8 = _bwd2b(x, dh, dproj, norm_g, w_in_t, token)
    ng_send, ng_recv, gng8, gng_land = _partials_start(gng8)
    win_others = _rs_wait(send_sems, recv_sems, staged_thru, land_thru, gng8)

    weights = dict(norm_g=norm_g, w_in=w_in.T, pool_w=pool_w, pool_b=pool_b, pool_scale=pool_scale, conv_dw=conv_dw, conv_b=conv_b,
                   ln_g=ln_g, ln_b=ln_b, pw_w=pw_w, pw_b=pw_b, w_out=w_out, final_g=final_g)
    m_in = dict(norm_g=m_norm_g, w_in=m_w_in.T, pool_w=m_pool_w, pool_b=m_pool_b, pool_scale=m_pool_scale, conv_dw=m_conv_dw,
                conv_b=m_conv_b, ln_g=m_ln_g, ln_b=m_ln_b, pw_w=m_pw_w, pw_b=m_pw_b, w_out=m_w_out, final_g=m_final_g)
    v_in = dict(norm_g=v_norm_g, w_in=v_w_in.T, pool_w=v_pool_w, pool_b=v_pool_b, pool_scale=v_pool_scale, conv_dw=v_conv_dw,
                conv_b=v_conv_b, ln_g=v_ln_g, ln_b=v_ln_b, pw_w=v_pw_w, pw_b=v_pw_b, w_out=v_w_out, final_g=v_final_g)
    main, (loss,) = _adamw(functools.partial(_adamw_body), "adamw", MAIN_PARAMS, weights, m_in, v_in,
                           (win_own, win_others, g_w_out, g_pw_w, vec, g_conv_dw, poolw_sum), [jax.ShapeDtypeStruct((1, 1), F32)])
    gng8, gng_others = _partials_wait(ng_send, ng_recv, gng8, gng_land, main[0]["w_out"])
    late, _ = _adamw(functools.partial(_adamw_late_body), "adamw_norm_g", LATE_PARAMS, weights, m_in, v_in, (gng8, gng_others))
    groups = [[(late[k] if p in LATE_PARAMS else main[k])[p] for p in PARAMS] for k in range(4)]
    k_in = PARAMS.index("w_in")
    for group in groups:
        group[k_in] = group[k_in].T
    grads, delta, new_m, new_v = groups
    return (loss.reshape(()), grad_x, *grads, *delta, *new_m, *new_v)
```

```python
import functools

import jax
import jax.numpy as jnp
from jax import lax
from jax.experimental import pallas as pl
from jax.experimental.pallas import tpu as pltpu

F32 = jnp.float32
BF16 = jnp.bfloat16

D_MODEL = 1024
W_MIX = 512
D_IN = 5 * W_MIX
POOL_WINDOWS = (2, 4, 8, 16)
POOL_GC = 128
CONV_WIDTH = 31
RMS_EPS = 1e-6
LN_EPS = 1e-5
N_DEV = 8

LANES = 128
SUBLANES = 8
TILE_M = 512
TILE_IN = 1024
POOL_HALO = 16
CONV_HALO = 32
CONV_ROWS = 256
BWD_CONV_ROWS = 64
BWD_SUBTILES = 2
VMEM_LIMIT = 56 * 1024 * 1024

VMEM = pl.BlockSpec(memory_space=pltpu.VMEM)


def _silu(v):
    return v * jax.nn.sigmoid(v)


def _colsum8(v):
    m, n = v.shape
    return jnp.sum(v.reshape(m // SUBLANES, SUBLANES, n), axis=0)


def _row_index(shape, t0):
    return lax.broadcasted_iota(jnp.int32, shape, 0) + t0


def _conv_taps():
    taps = {s: [] for s in range(SUBLANES)}
    for k in range(CONV_WIDTH):
        e = k + CONV_HALO - (CONV_WIDTH - 1)
        taps[e % SUBLANES].append((e // SUBLANES, k))
    return taps


def _fwd_body(x_ref, tgt_ref, proj_ref, poolw_ref, poolb_ref, pools_ref, dw_ref, cb_ref, lg_ref, lb_ref,
              pww_ref, pwb_ref, wout_ref, fg_ref,
              h1_ref, z_ref, ob_ref, dpool_ref, dh_ref, loss_ref, gfg_ref,
              uext, hext, y_s):
    b = pl.program_id(0)
    i = pl.program_id(1)
    tm = TILE_M

    @pl.when(i == 0)
    def _():
        uext[0:POOL_HALO, :] = jnp.zeros((POOL_HALO, W_MIX), F32)
        hext[0:CONV_HALO, :] = jnp.zeros((CONV_HALO, W_MIX), F32)
        hext[CONV_HALO + tm:, :] = jnp.zeros((SUBLANES, W_MIX), F32)

    @pl.when((b == 0) & (i == 0))
    def _():
        loss_ref[...] = jnp.zeros_like(loss_ref)
        gfg_ref[...] = jnp.zeros_like(gfg_ref)

    uext[POOL_HALO:, :] = proj_ref[0, :, 0:W_MIX]
    t = _row_index((tm, POOL_GC), i * tm)
    for g, w in enumerate(POOL_WINDOWS):
        lanes = slice(g * POOL_GC, (g + 1) * POOL_GC)
        e = uext[:, lanes]
        s = e
        sh = 1
        while sh < w:
            s = s + pltpu.roll(s, sh, 0)
            sh *= 2
        inv_cnt = 1.0 / jnp.minimum(t + 1, w).astype(F32)
        d = (s[POOL_HALO:] * inv_cnt - e[POOL_HALO:]).astype(BF16)
        dpool_ref[0, :, lanes] = d
        z = jnp.dot(d, poolw_ref[g].astype(BF16), preferred_element_type=F32) + poolb_ref[:, lanes]
        z_ref[0, :, lanes] = z
        a_gate = proj_ref[0, :, W_MIX + g * POOL_GC:W_MIX + (g + 1) * POOL_GC]
        y_s[:, lanes] = (z * pools_ref[:, lanes] * _silu(a_gate)).astype(BF16)

    hext[CONV_HALO:CONV_HALO + tm, :] = proj_ref[0, :, 2 * W_MIX:3 * W_MIX] * jax.nn.sigmoid(proj_ref[0, :, 3 * W_MIX:4 * W_MIX])
    taps = _conv_taps()
    for c0 in range(0, tm, CONV_ROWS):
        for l0 in range(0, W_MIX, LANES):
            lanes = slice(l0, l0 + LANES)
            acc = jnp.zeros((CONV_ROWS, LANES), F32) + cb_ref[:, lanes]
            for s in range(SUBLANES):
                part = jnp.zeros((CONV_ROWS + SUBLANES, LANES), F32)
                for q, k in taps[s]:
                    r0 = c0 + SUBLANES * q
                    part = part + hext[r0:r0 + CONV_ROWS + SUBLANES, lanes] * dw_ref[k:k + 1, lanes]
                acc = acc + part[s:s + CONV_ROWS]
            h1_ref[0, c0:c0 + CONV_ROWS, lanes] = acc
    h1 = h1_ref[0]
    mu = jnp.mean(h1, axis=-1, keepdims=True)
    xc = h1 - mu
    rstd = lax.rsqrt(jnp.mean(xc * xc, axis=-1, keepdims=True) + LN_EPS)
    h3 = _silu(xc * rstd * lg_ref[...] + lb_ref[...])
    ob = jnp.dot(h3.astype(BF16), pww_ref[...], preferred_element_type=F32) + pwb_ref[...]
    ob_ref[0] = ob
    y_s[:, W_MIX:] = (ob * _silu(proj_ref[0, :, 4 * W_MIX:5 * W_MIX])).astype(BF16)

    h = x_ref[0] + jnp.dot(y_s[...], wout_ref[...], preferred_element_type=F32)
    r2 = lax.rsqrt(jnp.mean(h * h, axis=-1, keepdims=True) + RMS_EPS)
    hr = h * r2
    err = hr * fg_ref[...] - tgt_ref[0]
    loss_ref[...] += jnp.sum(err * err) * (0.5 / D_MODEL)
    dout = err * (1.0 / D_MODEL)
    gfg_ref[...] += _colsum8(dout * hr)
    gy = dout * fg_ref[...]
    dh_ref[0] = r2 * (gy - hr * jnp.mean(gy * hr, axis=-1, keepdims=True))

    uext[0:POOL_HALO, :] = uext[tm:tm + POOL_HALO, :]
    hext[0:CONV_HALO, :] = hext[tm:tm + CONV_HALO, :]


def _row(v):
    return v.reshape(1, -1)


def _fwd(x, tgt, proj, pool_w, pool_b, pool_scale, conv_dw_f, conv_b, ln_g, ln_b, pw_w_b, pw_b, w_out_b, final_g):
    bl, s, _ = x.shape
    tm = TILE_M
    tok = lambda n: pl.BlockSpec((1, tm, n), lambda b, i: (b, i, 0))
    acc = lambda n: pl.BlockSpec((SUBLANES, n), lambda b, i: (0, 0))
    act = lambda n: jax.ShapeDtypeStruct((bl, s, n), F32)
    return pl.pallas_call(
        functools.partial(_fwd_body),
        name="fwd",
        grid=(bl, s // tm),
        in_specs=[tok(D_MODEL), tok(D_MODEL), tok(D_IN)] + [VMEM] * 11,
        out_specs=[tok(W_MIX), tok(W_MIX), tok(W_MIX), tok(W_MIX), tok(D_MODEL), acc(LANES), acc(D_MODEL)],
        out_shape=[act(W_MIX), act(W_MIX), act(W_MIX), jax.ShapeDtypeStruct((bl, s, W_MIX), BF16), act(D_MODEL),
                   jax.ShapeDtypeStruct((SUBLANES, LANES), F32), jax.ShapeDtypeStruct((SUBLANES, D_MODEL), F32)],
        scratch_shapes=[pltpu.VMEM((POOL_HALO + tm, W_MIX), F32), pltpu.VMEM((CONV_HALO + tm + SUBLANES, W_MIX), F32),
                        pltpu.VMEM((tm, D_MODEL), BF16)],
        compiler_params=pltpu.CompilerParams(dimension_semantics=("arbitrary", "arbitrary"), vmem_limit_bytes=VMEM_LIMIT),
    )(x, tgt, proj, pool_w, pool_b.reshape(1, W_MIX), _row(pool_scale), conv_dw_f, _row(conv_b),
      _row(ln_g), _row(ln_b), pw_w_b, _row(pw_b), w_out_b, _row(final_g))


NT_DIMS = (((1,), (1,)), ((), ()))
TN_DIMS = (((0,), (0,)), ((), ()))
V_POOL_B, V_POOL_SCALE, V_CONV_B, V_LN_G, V_LN_B, V_PW_B = range(6)


def _bwd_taps():
    taps = {s: [] for s in range(SUBLANES)}
    for k in range(CONV_WIDTH):
        e = CONV_WIDTH - 1 - k
        taps[e % SUBLANES].append((e // SUBLANES, k))
    return taps


def _bwd1_body(n_tiles, dh_ref, proj_ref, h1_ref, z_ref, ob_ref, dpool_ref, wout_ref, pww_ref, poolw_ref, pools_ref, dw_ref,
               lg_ref, lb_ref,
               dproj_ref, gwout_ref, gpww_ref, gpoolw_ref, gvec_ref, gdw_ref,
               vext, gext, gsh, y_s):
    b = pl.program_id(0)
    i = pl.program_id(1)
    tm = TILE_M
    t0 = (n_tiles - 1 - i) * tm

    @pl.when(i == 0)
    def _():
        vext[tm:, :] = jnp.zeros((POOL_HALO, W_MIX), F32)
        gext[tm:, :] = jnp.zeros((CONV_HALO, W_MIX), F32)

    @pl.when((b == 0) & (i == 0))
    def _():
        gwout_ref[...] = jnp.zeros_like(gwout_ref)
        gpww_ref[...] = jnp.zeros_like(gpww_ref)
        gpoolw_ref[...] = jnp.zeros_like(gpoolw_ref)
        gvec_ref[...] = jnp.zeros_like(gvec_ref)
        gdw_ref[...] = jnp.zeros_like(gdw_ref)

    sub = tm // BWD_SUBTILES

    def dense_phase(r0):
        rs = slice(r0, r0 + sub)
        dhb = dh_ref[0, rs, :].astype(BF16)
        dy_a = lax.dot_general(dhb, wout_ref[0:W_MIX, :], NT_DIMS, preferred_element_type=F32)
        dy_b = lax.dot_general(dhb, wout_ref[W_MIX:, :], NT_DIMS, preferred_element_type=F32)

        a_gate = proj_ref[0, rs, W_MIX:2 * W_MIX]
        sig_a = jax.nn.sigmoid(a_gate)
        silu_a = a_gate * sig_a
        z = z_ref[0, rs, :]
        out_a = z * pools_ref[...]
        y_s[rs, 0:W_MIX] = (out_a * silu_a).astype(BF16)
        d_out_a = dy_a * silu_a
        d_ag = dy_a * out_a * (sig_a * (1.0 + a_gate * (1.0 - sig_a)))
        dproj_ref[0, rs, W_MIX:2 * W_MIX] = d_ag.astype(BF16)
        gvec_ref[V_POOL_SCALE] += _colsum8(d_out_a * z)
        dz = d_out_a * pools_ref[...]
        gvec_ref[V_POOL_B] += _colsum8(dz)
        t = _row_index((sub, POOL_GC), t0 + r0)
        n_ext = sub + POOL_HALO
        for g, w in enumerate(POOL_WINDOWS):
            lanes = slice(g * POOL_GC, (g + 1) * POOL_GC)
            dzg = dz[:, lanes].astype(BF16)
            gpoolw_ref[g] += lax.dot_general(dpool_ref[0, rs, lanes], dzg, TN_DIMS, preferred_element_type=F32)
            dd = lax.dot_general(dzg, poolw_ref[g].astype(BF16), NT_DIMS, preferred_element_type=F32)
            vext[rs, lanes] = dd * (1.0 / jnp.minimum(t + 1, w).astype(F32))
            s = vext[r0:r0 + n_ext, lanes]
            sh = 1
            while sh < w:
                s = s + pltpu.roll(s, n_ext - sh, 0)
                sh *= 2
            dproj_ref[0, rs, lanes] = (s[0:sub] - dd).astype(BF16)

        b_gate = proj_ref[0, rs, 4 * W_MIX:5 * W_MIX]
        sig_b = jax.nn.sigmoid(b_gate)
        silu_b = b_gate * sig_b
        ob = ob_ref[0, rs, :]
        y_s[rs, W_MIX:] = (ob * silu_b).astype(BF16)
        d_ob = dy_b * silu_b
        dproj_ref[0, rs, 4 * W_MIX:5 * W_MIX] = (dy_b * ob * (sig_b * (1.0 + b_gate * (1.0 - sig_b)))).astype(BF16)
        gvec_ref[V_PW_B] += _colsum8(d_ob)
        gwout_ref[...] += lax.dot_general(y_s[rs, :], dhb, TN_DIMS, preferred_element_type=F32)

        h1 = h1_ref[0, rs, :]
        mu = jnp.mean(h1, axis=-1, keepdims=True)
        xc = h1 - mu
        rstd = lax.rsqrt(jnp.mean(xc * xc, axis=-1, keepdims=True) + LN_EPS)
        xhat = xc * rstd
        h2 = xhat * lg_ref[...] + lb_ref[...]
        sig2 = jax.nn.sigmoid(h2)
        d_obb = d_ob.astype(BF16)
        gpww_ref[...] += lax.dot_general((h2 * sig2).astype(BF16), d_obb, TN_DIMS, preferred_element_type=F32)
        dh2 = lax.dot_general(d_obb, pww_ref[...], NT_DIMS, preferred_element_type=F32) * (sig2 * (1.0 + h2 * (1.0 - sig2)))
        gvec_ref[V_LN_G] += _colsum8(dh2 * xhat)
        gvec_ref[V_LN_B] += _colsum8(dh2)
        dxh = dh2 * lg_ref[...]
        dh1 = rstd * (dxh - jnp.mean(dxh, axis=-1, keepdims=True) - xhat * jnp.mean(dxh * xhat, axis=-1, keepdims=True))
        gvec_ref[V_CONV_B] += _colsum8(dh1)
        gext[rs, :] = dh1

    taps = _bwd_taps()
    cr = BWD_CONV_ROWS
    win = cr + CONV_HALO - SUBLANES

    def conv_phase(r0):
        for c0 in range(r0, r0 + sub, cr):
            for l0 in range(0, W_MIX, LANES):
                lanes = slice(l0, l0 + LANES)
                b_val = proj_ref[0, c0:c0 + cr, 2 * W_MIX + l0:2 * W_MIX + l0 + LANES]
                sg = jax.nn.sigmoid(proj_ref[0, c0:c0 + cr, 3 * W_MIX + l0:3 * W_MIX + l0 + LANES])
                h0c = b_val * sg
                acc = jnp.zeros((cr, LANES), F32)
                for s in range(SUBLANES):
                    gsh[...] = gext[c0 + s:c0 + s + win, lanes]
                    for q, k in taps[s]:
                        gq = gsh[SUBLANES * q:SUBLANES * q + cr, :]
                        acc = acc + gq * dw_ref[k:k + 1, lanes]
                        gdw_ref[k, :, lanes] += _colsum8(h0c * gq)
                d_bval = acc * sg
                dproj_ref[0, c0:c0 + cr, 2 * W_MIX + l0:2 * W_MIX + l0 + LANES] = d_bval.astype(BF16)
                dproj_ref[0, c0:c0 + cr, 3 * W_MIX + l0:3 * W_MIX + l0 + LANES] = (d_bval * b_val * (1.0 - sg)).astype(BF16)

    starts = [k * sub for k in reversed(range(BWD_SUBTILES))]
    for r0 in starts:
        dense_phase(r0)
    for r0 in starts:
        conv_phase(r0)
    vext[tm:, :] = vext[0:POOL_HALO, :]
    gext[tm:, :] = gext[0:CONV_HALO, :]


def _bwd1(dh, proj, h1, z, ob, dpool, w_out_b, pw_w_b, pool_w, pool_scale, conv_dw_f, ln_g, ln_b):
    bl, s, _ = dh.shape
    tm = TILE_M
    nt = s // tm
    tok = lambda n: pl.BlockSpec((1, tm, n), lambda b, i: (b, nt - 1 - i, 0))
    res = lambda shape: pl.BlockSpec(shape, lambda b, i: (0,) * len(shape))
    out_shapes = [(D_MODEL, D_MODEL), (W_MIX, W_MIX), (len(POOL_WINDOWS), POOL_GC, POOL_GC), (6, SUBLANES, W_MIX),
                  (CONV_WIDTH, SUBLANES, W_MIX)]
    return pl.pallas_call(
        functools.partial(_bwd1_body, nt),
        name="bwd1",
        grid=(bl, nt),
        in_specs=[tok(D_MODEL), tok(D_IN), tok(W_MIX), tok(W_MIX), tok(W_MIX), tok(W_MIX)] + [VMEM] * 7,
        out_specs=[tok(D_IN)] + [res(sh) for sh in out_shapes],
        out_shape=[jax.ShapeDtypeStruct((bl, s, D_IN), BF16)] + [jax.ShapeDtypeStruct(sh, F32) for sh in out_shapes],
        scratch_shapes=[pltpu.VMEM((tm + POOL_HALO, W_MIX), F32), pltpu.VMEM((tm + CONV_HALO, W_MIX), F32),
                        pltpu.VMEM((BWD_CONV_ROWS + CONV_HALO - SUBLANES, LANES), F32), pltpu.VMEM((tm, D_MODEL), BF16)],
        compiler_params=pltpu.CompilerParams(dimension_semantics=("arbitrary", "arbitrary"), vmem_limit_bytes=VMEM_LIMIT),
    )(dh, proj, h1, z, ob, dpool, w_out_b, pw_w_b, pool_w, _row(pool_scale), conv_dw_f, _row(ln_g), _row(ln_b))


MESH = pl.DeviceIdType.MESH
W_IN_BLK = D_IN // N_DEV
W_OUT_BLK = D_MODEL // N_DEV
PW_BLK = W_MIX // N_DEV
DW_BLK = W_MIX // N_DEV
N_PEER = N_DEV - 1


SLAB = 2 * W_IN_BLK
N_SLABS = 4
F_SIB, F_X, F_Y, F_ON, F_PASS_X, F_PASS_Y, F_PASS_D = range(7)
W_IN_ONLY, OTHERS = (0,), (1, 2, 3)


def _inproj_body(n_tiles, x_hbm, ng_ref, wint_ref, wout_ref, pww_ref, dw_ref,
                 proj_hbm, wt_hbm, woutf_hbm, pwwf_hbm, dwf_ref,
                 hn_s, wt_s, wout_s, pww_s, dw_blk, stage, xbuf, out_sems, x_sems, w_sems, send_sems, recv_sems):
    s = pl.program_id(0)
    i = pl.program_id(1)
    tm = TILE_IN
    x, y, c = lax.axis_index("x"), lax.axis_index("y"), lax.axis_index("c")
    sibling = (x, y, 1 - c)
    xn, yn, dg = (1 - x, y), (x, 1 - y), (1 - x, 1 - y)
    pick = lambda a, b: tuple(jnp.where(c == 1, p, q) for p, q in zip(a, b))
    on_from, on_to = pick(xn, yn), pick(yn, xn)

    def blocks(px, py, pc):
        k = 4 * px + 2 * py + pc
        rows = lambda ref, n: ref.at[pl.ds(pl.multiple_of(k * n, n), n), :]
        return (rows(wt_s, W_IN_BLK), rows(wout_s, W_OUT_BLK), rows(pww_s, PW_BLK), dw_blk.at[k])

    def copies(arrays, flow, block, to):
        refs = blocks(*block)
        return [pltpu.make_async_remote_copy(src_ref=refs[a], dst_ref=refs[a], send_sem=send_sems.at[a, flow],
                                             recv_sem=recv_sems.at[a, flow], device_id=to, device_id_type=MESH)
                for a in arrays]

    def start(arrays):
        for flow, to in ((F_SIB, sibling), (F_X, (*xn, c)), (F_Y, (*yn, c))):
            for cp in copies(arrays, flow, (x, y, c), to):
                cp.start()

    def direct(arrays):
        for flow, chip in ((F_X, xn), (F_Y, yn)):
            for cp in copies(arrays, flow, (*chip, c), sibling):
                cp.wait_recv()
        for cp in copies(arrays, F_ON, (*on_from, c), (*on_to, c)):
            cp.start()
        for flow, chip in ((F_PASS_X, xn), (F_PASS_Y, yn)):
            for cp in copies(arrays, flow, (*chip, c), sibling):
                cp.start()

    def diagonal(arrays):
        for cp in copies(arrays, F_ON, (*dg, c), sibling):
            cp.wait_recv()
        for cp in copies(arrays, F_PASS_D, (*dg, c), sibling):
            cp.start()

    def from_sibling(arrays, flow, chip):
        for cp in copies(arrays, flow, (*chip, 1 - c), sibling):
            cp.wait_recv()

    def finish(arrays):
        mine = (x, y, c)
        for flow, block in ((F_SIB, mine), (F_X, mine), (F_Y, mine), (F_ON, (*on_from, c)), (F_PASS_X, (*xn, c)),
                            (F_PASS_Y, (*yn, c)), (F_PASS_D, (*dg, c))):
            for cp in copies(arrays, flow, block, sibling):
                cp.wait_send()

    step = s * n_tiles + i
    par = step % 2
    first = step == 0

    @pl.when(first)
    def _():
        mine = blocks(x, y, c)
        mine[0][...] = wint_ref[...].astype(BF16)
        mine[1][...] = wout_ref[...].astype(BF16)
        mine[2][...] = pww_ref[...].astype(BF16)
        mine[3][...] = dw_ref[...]
        start(W_IN_ONLY)
        start(OTHERS)

    rows_i = pl.ds(pl.multiple_of(i * tm, tm), tm)
    tiles_per_seq = proj_hbm.shape[1] // tm

    def x_copy(k):
        src = x_hbm.at[k // tiles_per_seq, pl.ds(pl.multiple_of((k % tiles_per_seq) * tm, tm), tm), :]
        return pltpu.make_async_copy(src, xbuf.at[k % 2], x_sems.at[k % 2])

    @pl.when(first)
    def _():
        x_copy(0).start()

    @pl.when(s == 0)
    def _():
        x_copy(i).wait()

        @pl.when(i + 1 < n_tiles)
        def _():
            x_copy(i + 1).start()

        xt = xbuf[i % 2]
        r1 = lax.rsqrt(jnp.mean(xt * xt, axis=-1, keepdims=True) + RMS_EPS)
        hn_s[rows_i, :] = (xt * r1 * ng_ref[...]).astype(BF16)

    @pl.when(first)
    def _():
        from_sibling(W_IN_ONLY, F_SIB, (x, y))

    @pl.when((s == 1) & (i == 0))
    def _():
        direct(W_IN_ONLY)
        from_sibling(W_IN_ONLY, F_PASS_Y, yn)

    @pl.when((s == 2) & (i == 0))
    def _():
        from_sibling(W_IN_ONLY, F_PASS_X, xn)

    weights_out = [pltpu.make_async_copy(src, dst, w_sems.at[k])
                   for k, (src, dst) in enumerate(((wt_s, wt_hbm), (wout_s, woutf_hbm), (pww_s, pwwf_hbm)))]

    @pl.when((s == 3) & (i == 0))
    def _():
        diagonal(W_IN_ONLY)
        direct(OTHERS)
        from_sibling(W_IN_ONLY, F_PASS_D, dg)
        weights_out[0].start()

    def out_copy(parity, col0):
        dst = proj_hbm.at[i // tiles_per_seq, pl.ds(pl.multiple_of((i % tiles_per_seq) * tm, tm), tm),
                          pl.ds(pl.multiple_of(col0, LANES), SLAB)]
        return pltpu.make_async_copy(stage.at[parity], dst, out_sems.at[parity])

    @pl.when(step >= 2)
    def _():
        out_copy(par, 0).wait()

    chip = jnp.where(s < 2, 2 * x, 2 * (1 - x)) + jnp.where((s == 0) | (s == 2), y, 1 - y)
    slab = wt_s[pl.ds(pl.multiple_of(chip * SLAB, SLAB), SLAB), :]
    stage[par] = lax.dot_general(hn_s[rows_i, :], slab, NT_DIMS, preferred_element_type=F32)
    out_copy(par, chip * SLAB).start()

    @pl.when(step == N_SLABS * n_tiles - 1)
    def _():
        out_copy(par, 0).wait()
        out_copy(1 - par, 0).wait()
        from_sibling(OTHERS, F_SIB, (x, y))
        from_sibling(OTHERS, F_PASS_X, xn)
        from_sibling(OTHERS, F_PASS_Y, yn)
        diagonal(OTHERS)
        from_sibling(OTHERS, F_PASS_D, dg)
        for cp in weights_out[1:]:
            cp.start()
        finish(W_IN_ONLY)
        finish(OTHERS)
        for cp in weights_out:
            cp.wait()
        for k in range(N_DEV):
            dwf_ref[:, k * DW_BLK:(k + 1) * DW_BLK] = dw_blk[k]


def _inproj(x, norm_g, w_in_t, w_out, pw_w, conv_dw):
    bl, s, _ = x.shape
    tm = TILE_IN
    nt = s // tm
    n_tiles = bl * nt
    dma = pltpu.SemaphoreType.DMA
    return pl.pallas_call(
        functools.partial(_inproj_body, n_tiles),
        name="inproj",
        grid=(N_SLABS, n_tiles),
        in_specs=[pl.BlockSpec(memory_space=pl.ANY)] + [VMEM] * 5,
        out_specs=[pl.BlockSpec(memory_space=pl.ANY)] * 4 + [VMEM],
        out_shape=[jax.ShapeDtypeStruct((bl, s, D_IN), F32), jax.ShapeDtypeStruct((D_IN, D_MODEL), BF16),
                   jax.ShapeDtypeStruct((D_MODEL, D_MODEL), BF16), jax.ShapeDtypeStruct((W_MIX, W_MIX), BF16),
                   jax.ShapeDtypeStruct((CONV_WIDTH, W_MIX), F32)],
        scratch_shapes=[pltpu.VMEM((n_tiles * tm, D_MODEL), BF16), pltpu.VMEM((D_IN, D_MODEL), BF16),
                        pltpu.VMEM((D_MODEL, D_MODEL), BF16), pltpu.VMEM((W_MIX, W_MIX), BF16),
                        pltpu.VMEM((N_DEV, CONV_WIDTH, DW_BLK), F32), pltpu.VMEM((2, tm, SLAB), F32),
                        pltpu.VMEM((2, tm, D_MODEL), F32), dma((2,)), dma((2,)), dma((3,)), dma((4, 7)), dma((4, 7))],
        compiler_params=pltpu.CompilerParams(dimension_semantics=("arbitrary", "arbitrary"), vmem_limit_bytes=VMEM_LIMIT),
    )(x, _row(norm_g), w_in_t, w_out, pw_w, conv_dw)


B_PLAIN, B_ON, B_MERGED = range(3)


def _routes():
    x, y, c = lax.axis_index("x"), lax.axis_index("y"), lax.axis_index("c")
    xn, yn = (1 - x, y), (x, 1 - y)
    pick = lambda a, b: tuple(jnp.where(c == 1, p, q) for p, q in zip(a, b))
    return pick(xn, yn), pick(yn, xn), (1 - x, 1 - y), (x, y), c


def _chip_index(chip):
    return 2 * chip[0] + chip[1]


def _copy(src, dst, send_sems, recv_sems, a, k, to):
    return pltpu.make_async_remote_copy(src_ref=src, dst_ref=dst, send_sem=send_sems.at[a, k], recv_sem=recv_sems.at[a, k],
                                        device_id=to, device_id_type=MESH)


def _rsb_send(plain, on, rb_plain, rb_on, send_b, recv_b):
    first, _, _, _, c = _routes()
    for k, srcs, dsts in ((B_ON, on, rb_on), (B_PLAIN, plain, rb_plain)):
        for a, (s, d) in enumerate(zip(srcs, dsts)):
            _copy(s, d, send_b, recv_b, a, k, (*first, c)).start()


def _rsb_merge(second_sums, merged, rb_on, rb_merged, send_b, recv_b):
    _, second, _, _, c = _routes()
    for a, (own, m, got, dst) in enumerate(zip(second_sums, merged, rb_on, rb_merged)):
        _copy(got, got, send_b, recv_b, a, B_ON, (*second, c)).wait_recv()
        m[...] = (own[...] + got[...].astype(F32)).astype(BF16)
        _copy(m, dst, send_b, recv_b, a, B_MERGED, (*second, c)).start()


def _rsb_total(own_sums, plain, on, merged, rb_plain, rb_merged, send_b, recv_b):
    _, second, _, _, c = _routes()
    out = []
    for a, (own, p, g) in enumerate(zip(own_sums, rb_plain, rb_merged)):
        _copy(p, p, send_b, recv_b, a, B_PLAIN, (*second, c)).wait_recv()
        _copy(g, g, send_b, recv_b, a, B_MERGED, (*second, c)).wait_recv()
        out.append(own[...] + p[...].astype(F32) + g[...].astype(F32))
    for k, srcs in ((B_PLAIN, plain), (B_ON, on), (B_MERGED, merged)):
        for a, s in enumerate(srcs):
            _copy(s, s, send_b, recv_b, a, k, (*second, c)).wait_send()
    return out


R_LOSS, R_FINAL_G, R_POOL, R_CONV, R_LN = range(5)
VEC_ROWS = 16
DW_ROWS = 32
A_PART, A_X, A_Y, A_ON, A_HALF = range(5)
HALF_DTYPES = (F32, BF16, BF16)
STEP_SUMS, STEP_MERGE, STEP_HALF = 1, 3, 5


def _bwd2a_body(n_steps, x_hbm, dproj_hbm, ng_ref, gwout_hbm, gpww_hbm, loss_ref, gfg_ref, gvec_ref, gdw_ref, gpoolw_ref,
                win_own_ref, win_staged_ref, owout_ref, opww_ref, ovec_ref, odw_ref, opoolw_ref,
                acc, st_win, a_win, own_wout, own_pww, a_wout, a_pww, st_wout, st_pww, rb_wout, rb_pww, part_vec, part_dw,
                part_pw, sib_vec, sib_dw, sib_pw, res_vec, res_dw, res_pw, half_vec, half_dw, half_pw,
                own_sems, send_a, recv_a, send_b, recv_b, send_r, recv_r, send_w, recv_w, x_ring, dproj_ring, ring_sems):
    step = pl.program_id(0) * pl.num_programs(1) + pl.program_id(1)
    x, y, c = lax.axis_index("x"), lax.axis_index("y"), lax.axis_index("c")
    sibling = (x, y, 1 - c)
    xn, yn = (1 - x, y), (x, 1 - y)
    first, second, diag, own, _ = _routes()
    rows = lambda ref, k, n: ref.at[pl.ds(pl.multiple_of(k * n, n), n), :]

    blocks = lambda k: (rows(gwout_hbm, k, W_OUT_BLK), rows(gpww_hbm, k, PW_BLK))
    mine = lambda chip: (own_wout.at[chip], own_pww.at[chip])
    mine_copies = lambda chip: [pltpu.make_async_copy(s, d, own_sems.at[a, chip])
                                for a, (s, d) in enumerate(zip(blocks(2 * chip + c), mine(chip)))]
    landed = lambda chip: (a_wout.at[chip], a_pww.at[chip])
    staged = lambda k: (st_wout.at[k], st_pww.at[k])
    arrived = lambda k: (rb_wout.at[k], rb_pww.at[k])

    parts, sibs, halves = (part_vec, part_dw, part_pw), (sib_vec, sib_dw, sib_pw), (half_vec, half_dw, half_pw)
    results = (res_vec, res_dw, res_pw)
    half_rows = lambda ref: pl.ds(pl.multiple_of(c * (ref.shape[0] // 2), SUBLANES), ref.shape[0] // 2)
    slot = lambda chip: tuple(h.at[_chip_index(chip)] for h in halves)

    @pl.when(step == 0)
    def _():
        acc[...] = jnp.zeros_like(acc)
        sum8 = lambda v: jnp.sum(v, axis=0, keepdims=True)
        part_vec[...] = jnp.zeros((VEC_ROWS, D_MODEL), F32)
        part_vec[R_LOSS:R_LOSS + 1, :] = jnp.broadcast_to(loss_ref[0:1, 0:1], (1, D_MODEL))
        part_vec[R_FINAL_G:R_FINAL_G + 1, :] = sum8(gfg_ref[...])
        for row, (lo, hi) in ((R_POOL, (V_POOL_B, V_POOL_SCALE)), (R_CONV, (V_CONV_B, V_LN_G)), (R_LN, (V_LN_B, V_PW_B))):
            part_vec[row:row + 1, 0:W_MIX] = sum8(gvec_ref[lo])
            part_vec[row:row + 1, W_MIX:] = sum8(gvec_ref[hi])
        part_dw[0:CONV_WIDTH, :] = jnp.sum(gdw_ref[...], axis=1)
        part_dw[CONV_WIDTH:, :] = jnp.zeros((DW_ROWS - CONV_WIDTH, W_MIX), F32)
        part_pw[...] = gpoolw_ref[...].reshape(len(POOL_WINDOWS) * POOL_GC, POOL_GC)
        for a, (p, s) in enumerate(zip(parts, sibs)):
            _copy(p, s, send_r, recv_r, a, A_PART, sibling).start()
        for chip in range(4):
            for a, (s, d) in enumerate(zip(blocks(2 * chip + (1 - c)), landed(chip))):
                _copy(s, d, send_a, recv_a, a, chip, sibling).start()
            for cp in mine_copies(chip):
                cp.start()

    def tile_copies(k):
        k = jnp.asarray(k, jnp.int32)
        seq, tile = k // pl.num_programs(1), k % pl.num_programs(1)
        tile_rows = pl.ds(pl.multiple_of(tile * TILE_M, TILE_M), TILE_M)
        return [pltpu.make_async_copy(src.at[seq, tile_rows, :], ring.at[k % RING], ring_sems.at[j, k % RING])
                for j, (src, ring) in enumerate(((x_hbm, x_ring), (dproj_hbm, dproj_ring)))]

    @pl.when(step == 0)
    def _():
        for k in range(RING - 1):
            for cp in tile_copies(k):
                cp.start()

    @pl.when(step + RING - 1 < n_steps)
    def _():
        for cp in tile_copies(step + RING - 1):
            cp.start()

    for cp in tile_copies(step):
        cp.wait()
    x_t = x_ring[step % RING]
    r1 = lax.rsqrt(jnp.mean(x_t * x_t, axis=-1, keepdims=True) + RMS_EPS)
    hn = (x_t * r1 * ng_ref[...]).astype(BF16)
    acc[...] += lax.dot_general(dproj_ring[step % RING], hn, TN_DIMS, preferred_element_type=F32)

    @pl.when(step == STEP_SUMS)
    def _():
        for a, (p, s, h) in enumerate(zip(parts, sibs, slot(own))):
            _copy(p, s, send_r, recv_r, a, A_PART, sibling).wait_recv()
            h[...] = (p[half_rows(p), :] + s[half_rows(s), :]).astype(h.dtype)
            for k, to in ((A_X, xn), (A_Y, yn)):
                _copy(h, h, send_r, recv_r, a, k, (*to, c)).start()
        for chip in range(4):
            for cp in mine_copies(chip):
                cp.wait()
            for a, (m, l) in enumerate(zip(mine(chip), landed(chip))):
                _copy(l, l, send_a, recv_a, a, chip, sibling).wait_recv()
                l[...] = m[...] + l[...]
        for k, chip in ((B_ON, diag), (B_PLAIN, first)):
            for s, l in zip(staged(k), landed(_chip_index(chip))):
                s[...] = l[...].astype(BF16)
        _rsb_send(staged(B_PLAIN), staged(B_ON), arrived(B_PLAIN), arrived(B_ON), send_b, recv_b)

    @pl.when(step == STEP_MERGE)
    def _():
        _rsb_merge(landed(_chip_index(second)), staged(B_MERGED), arrived(B_ON), arrived(B_MERGED), send_b, recv_b)
        for a in range(3):
            for k, chip in ((A_X, xn), (A_Y, yn)):
                h = slot(chip)[a]
                _copy(h, h, send_r, recv_r, a, k, sibling).wait_recv()
            h = slot(first)[a]
            _copy(h, h, send_r, recv_r, a, A_ON, (*second, c)).start()

    @pl.when(step == STEP_HALF)
    def _():
        for a, (r, hs) in enumerate(zip(results, halves)):
            h = slot(diag)[a]
            _copy(h, h, send_r, recv_r, a, A_ON, sibling).wait_recv()
            mine = r.at[half_rows(r), :]
            chip_sums = [hs[k].astype(F32) for k in range(4)]
            mine[...] = (chip_sums[0] + chip_sums[1]) + (chip_sums[2] + chip_sums[3])
            _copy(mine, mine, send_r, recv_r, a, A_HALF, sibling).start()

    @pl.when(step == n_steps - 1)
    def _():
        block = lambda k: rows(acc, k, W_IN_BLK)
        w_copy = lambda chip: _copy(st_win.at[chip], a_win.at[chip], send_w, recv_w, 0, chip, sibling)
        for chip in range(4):
            st_win[chip] = block(2 * chip + (1 - c))[...].astype(BF16)
            w_copy(chip).start()

        owout_ref[...], opww_ref[...] = _rsb_total(landed(_chip_index(own)), staged(B_PLAIN), staged(B_ON), staged(B_MERGED),
                                                   arrived(B_PLAIN), arrived(B_MERGED), send_b, recv_b)
        for a, (r, p, s) in enumerate(zip(results, parts, sibs)):
            theirs = r.at[pl.ds(pl.multiple_of((1 - c) * (r.shape[0] // 2), SUBLANES), r.shape[0] // 2), :]
            _copy(theirs, theirs, send_r, recv_r, a, A_HALF, sibling).wait_recv()
            _copy(p, s, send_r, recv_r, a, A_PART, sibling).wait_send()
            for k, h in ((A_X, slot(own)[a]), (A_Y, slot(own)[a]), (A_ON, slot(first)[a])):
                _copy(h, h, send_r, recv_r, a, k, sibling).wait_send()
            mine = r.at[half_rows(r), :]
            _copy(mine, mine, send_r, recv_r, a, A_HALF, sibling).wait_send()
        for chip in range(4):
            for a, (s, d) in enumerate(zip(blocks(2 * chip + (1 - c)), landed(chip))):
                _copy(s, d, send_a, recv_a, a, chip, sibling).wait_send()
        ovec_ref[...] = res_vec[...]
        opoolw_ref[...] = res_pw[...]
        for k in range(N_DEV):
            @pl.when(4 * x + 2 * y + c == k)
            def _():
                odw_ref[...] = res_dw[0:CONV_WIDTH, k * DW_BLK:(k + 1) * DW_BLK]

        for chip in range(4):
            w_copy(chip).wait()
        chip_sum = lambda chip: block(2 * _chip_index(chip) + c)[...] + a_win[_chip_index(chip)].astype(F32)
        win_own_ref[...] = chip_sum(own)
        for j, chip in enumerate((xn, yn, diag)):
            win_staged_ref[j] = chip_sum(chip).astype(BF16)


def _bwd2a(x, dproj, norm_g, gwout, gpww, loss8, gfg8, gvec, gdw8, gpoolw):
    bl, s, _ = x.shape
    tm = TILE_M
    nt = s // tm
    tok = lambda n: pl.BlockSpec((1, tm, n), lambda b, i: (b, i, 0))
    n_pw = len(POOL_WINDOWS) * POOL_GC
    win = (W_IN_BLK, D_MODEL)
    blk = ((W_OUT_BLK, D_MODEL), (PW_BLK, W_MIX))
    rep = ((VEC_ROWS, D_MODEL), (DW_ROWS, W_MIX), (n_pw, POOL_GC))
    half = lambda sh: (sh[0] // 2,) + sh[1:]
    dma = pltpu.SemaphoreType.DMA
    assert bl * nt > STEP_HALF + 1, "the exchanges' phases need their grid steps"
    return pl.pallas_call(
        functools.partial(_bwd2a_body, bl * nt),
        name="bwd2a",
        grid=(bl, nt),
        in_specs=[pl.BlockSpec(memory_space=pl.ANY)] * 2 + [VMEM] + [pl.BlockSpec(memory_space=pl.ANY)] * 2 + [VMEM] * 5,
        out_specs=[VMEM] * 7,
        out_shape=[jax.ShapeDtypeStruct(win, F32), jax.ShapeDtypeStruct((3,) + win, BF16)]
        + [jax.ShapeDtypeStruct(sh, F32) for sh in blk + (rep[0], (CONV_WIDTH, DW_BLK), rep[2])],
        scratch_shapes=[pltpu.VMEM((D_IN, D_MODEL), F32), pltpu.VMEM((4,) + win, BF16), pltpu.VMEM((4,) + win, BF16)]
        + [pltpu.VMEM((4,) + sh, F32) for sh in blk] * 2
        + [pltpu.VMEM((3,) + sh, BF16) for sh in blk] * 2
        + [pltpu.VMEM(sh, F32) for sh in rep] * 3
        + [pltpu.VMEM((4,) + half(sh), dt) for sh, dt in zip(rep, HALF_DTYPES)]
        + [dma((2, 4)), dma((2, 4)), dma((2, 4)), dma((2, 3)), dma((2, 3)), dma((3, 5)), dma((3, 5)), dma((1, 4)), dma((1, 4))]
        + [pltpu.VMEM((RING, tm, D_MODEL), F32), pltpu.VMEM((RING, tm, D_IN), BF16), dma((2, RING))],
        compiler_params=pltpu.CompilerParams(dimension_semantics=("arbitrary", "arbitrary"), vmem_limit_bytes=VMEM_LIMIT),
    )(x, dproj, _row(norm_g), gwout, gpww, loss8, gfg8, gvec, gdw8, gpoolw)


HBM = pl.BlockSpec(memory_space=pltpu.HBM)
SEM = pl.BlockSpec(memory_space=pltpu.SEMAPHORE)
EFFECT = pltpu.SideEffectType.DATAFLOW_SIDE_EFFECTING


def _owner_copies(src_ref, land_ref, send_sems, recv_sems):
    x, y, c = lax.axis_index("x"), lax.axis_index("y"), lax.axis_index("c")
    owners = ((1 - x, y, c), (x, 1 - y, c), (1 - x, 1 - y, c))
    return [pltpu.make_async_remote_copy(src_ref=src_ref.at[j], dst_ref=land_ref.at[j], send_sem=send_sems.at[j],
                                         recv_sem=recv_sems.at[j], device_id=owners[j], device_id_type=MESH)
            for j in range(3)]


def _rs_start(staged):
    def body(src_ref, land_ref, send_sems, recv_sems, src_thru, land_thru, token):
        for cp in _owner_copies(src_ref, land_ref, send_sems, recv_sems):
            cp.start()
        token[...] = jnp.zeros_like(token)

    dma = pltpu.SemaphoreType.DMA
    return pl.pallas_call(
        body, name="rs_start",
        out_shape=(dma((3,)), dma((3,)), pltpu.HBM(staged.shape, staged.dtype), pltpu.HBM(staged.shape, staged.dtype),
                   jax.ShapeDtypeStruct((SUBLANES, LANES), F32)),
        in_specs=(HBM, HBM), out_specs=(SEM, SEM, HBM, HBM, VMEM), input_output_aliases={0: 2, 1: 3},
        compiler_params=pltpu.CompilerParams(has_side_effects=EFFECT),
    )(pltpu.with_memory_space_constraint(staged, pltpu.HBM),
      pltpu.with_memory_space_constraint(lax.empty(staged.shape, staged.dtype), pltpu.HBM))


def _rs_wait(send_sems, recv_sems, src_thru, land_thru, after):
    def body(src_ref, land_ref, send_sems, recv_sems, after_ref, src_dead, got_ref):
        for cp in _owner_copies(src_ref, land_ref, send_sems, recv_sems):
            cp.wait_send()
            cp.wait_recv()

    return pl.pallas_call(
        body, name="rs_wait",
        out_shape=(pltpu.HBM(src_thru.shape, src_thru.dtype), pltpu.HBM(land_thru.shape, land_thru.dtype)),
        in_specs=(HBM, HBM, SEM, SEM, pl.BlockSpec(memory_space=pl.ANY)), out_specs=(HBM, HBM),
        input_output_aliases={0: 0, 1: 1},
        compiler_params=pltpu.CompilerParams(has_side_effects=EFFECT),
    )(src_thru, land_thru, send_sems, recv_sems, after)[1]


RING = 3


def _bwd2b_body(n_steps, tiles_per_seq, x_hbm, dh_hbm, dproj_hbm, ng_ref, wint_ref, token_ref, gx_ref, gng_ref,
                x_ring, dh_ring, dproj_ring, sems):
    step = pl.program_id(0) * pl.num_programs(1) + pl.program_id(1)
    tm = TILE_M

    def tile_copies(k):
        k = jnp.asarray(k, jnp.int32)
        slot = k % RING
        rows = pl.ds(pl.multiple_of((k % tiles_per_seq) * tm, tm), tm)
        return [pltpu.make_async_copy(src.at[k // tiles_per_seq, rows, :], ring.at[slot], sems.at[j, slot])
                for j, (src, ring) in enumerate(((x_hbm, x_ring), (dh_hbm, dh_ring), (dproj_hbm, dproj_ring)))]

    @pl.when(step == 0)
    def _():
        gng_ref[...] = jnp.zeros_like(gng_ref) + token_ref[0:1, 0:1]
        for k in range(RING - 1):
            for cp in tile_copies(k):
                cp.start()

    @pl.when(step + RING - 1 < n_steps)
    def _():
        for cp in tile_copies(step + RING - 1):
            cp.start()

    for cp in tile_copies(step):
        cp.wait()
    slot = step % RING
    x_t = x_ring[slot]
    r1 = lax.rsqrt(jnp.mean(x_t * x_t, axis=-1, keepdims=True) + RMS_EPS)
    xr = x_t * r1
    dhn = jnp.dot(dproj_ring[slot], wint_ref[...], preferred_element_type=F32)
    gng_ref[...] += _colsum8(dhn * xr)
    gy = dhn * ng_ref[...]
    gx_ref[0] = dh_ring[slot] + r1 * (gy - xr * jnp.mean(gy * xr, axis=-1, keepdims=True))


def _bwd2b(x, dh, dproj, norm_g, w_in_t, token):
    bl, s, _ = x.shape
    tm = TILE_M
    nt = s // tm
    assert bl * nt >= RING
    hbm = pl.BlockSpec(memory_space=pl.ANY)
    return pl.pallas_call(
        functools.partial(_bwd2b_body, bl * nt, nt),
        name="bwd2b",
        grid=(bl, nt),
        in_specs=[hbm, hbm, hbm, VMEM, VMEM, VMEM],
        out_specs=[pl.BlockSpec((1, tm, D_MODEL), lambda b, i: (b, i, 0)),
                   pl.BlockSpec((SUBLANES, D_MODEL), lambda b, i: (0, 0))],
        out_shape=[jax.ShapeDtypeStruct((bl, s, D_MODEL), F32), jax.ShapeDtypeStruct((SUBLANES, D_MODEL), F32)],
        scratch_shapes=[pltpu.VMEM((RING, tm, D_MODEL), F32), pltpu.VMEM((RING, tm, D_MODEL), F32),
                        pltpu.VMEM((RING, tm, D_IN), BF16), pltpu.SemaphoreType.DMA((3, RING))],
        compiler_params=pltpu.CompilerParams(dimension_semantics=("arbitrary", "arbitrary"), vmem_limit_bytes=VMEM_LIMIT),
    )(x, dh, dproj, _row(norm_g), w_in_t, token)


def _partial_copies(part_ref, land_ref, send_sems, recv_sems):
    x, y, c = lax.axis_index("x"), lax.axis_index("y"), lax.axis_index("c")
    other_chips = [(1 - x, y), (x, 1 - y), (1 - x, 1 - y)]
    peers = [(x, y, 1 - c)] + [(*chip, c) for chip in other_chips] + [(*chip, 1 - c) for chip in other_chips]
    return [pltpu.make_async_remote_copy(src_ref=part_ref, dst_ref=land_ref.at[4 * x + 2 * y + c], send_sem=send_sems.at[r],
                                         recv_sem=recv_sems.at[r], device_id=peers[r], device_id_type=MESH)
            for r in range(N_PEER)]


def _partials_start(part):
    def body(part_ref, land_ref, send_sems, recv_sems, part_thru, land_thru):
        for cp in _partial_copies(part_ref, land_ref, send_sems, recv_sems):
            cp.start()

    dma = pltpu.SemaphoreType.DMA
    land = (N_DEV,) + part.shape
    return pl.pallas_call(
        body, name="partials_start",
        out_shape=(dma((N_PEER,)), dma((N_PEER,)), pltpu.HBM(part.shape, part.dtype), pltpu.HBM(land, part.dtype)),
        in_specs=(HBM, HBM), out_specs=(SEM, SEM, HBM, HBM), input_output_aliases={0: 2, 1: 3},
        compiler_params=pltpu.CompilerParams(has_side_effects=EFFECT),
    )(pltpu.with_memory_space_constraint(part, pltpu.HBM),
      pltpu.with_memory_space_constraint(lax.empty(land, part.dtype), pltpu.HBM))


def _partials_wait(send_sems, recv_sems, part_thru, land_thru, after):
    def body(part_ref, land_ref, send_sems, recv_sems, after_ref, part_out, land_out):
        for cp in _partial_copies(part_ref, land_ref, send_sems, recv_sems):
            cp.wait_send()
            cp.wait_recv()

    return pl.pallas_call(
        body, name="partials_wait",
        out_shape=(pltpu.HBM(part_thru.shape, part_thru.dtype), pltpu.HBM(land_thru.shape, land_thru.dtype)),
        in_specs=(HBM, HBM, SEM, SEM, pl.BlockSpec(memory_space=pl.ANY)), out_specs=(HBM, HBM),
        input_output_aliases={0: 0, 1: 1},
        compiler_params=pltpu.CompilerParams(has_side_effects=EFFECT),
    )(part_thru, land_thru, send_sems, recv_sems, after)


ADAM_LR = 0.001
ADAM_B1 = 0.9
ADAM_B2 = 0.999
ADAM_EPS = 1e-08
ADAM_WD = 0.01
ADAM_STEP = 10


PARAMS = ("norm_g", "w_in", "pool_w", "pool_b", "pool_scale", "conv_dw", "conv_b", "ln_g", "ln_b", "pw_w", "pw_b", "w_out",
          "final_g")


LATE_PARAMS = ("norm_g",)
MAIN_PARAMS = tuple(p for p in PARAMS if p not in LATE_PARAMS)


def _adamw_refs(params, n_sources, refs):
    n = len(params)
    ws, ms, vs = (dict(zip(params, refs[k * n:(k + 1) * n])) for k in range(3))
    outs = refs[3 * n + n_sources:]
    gs, ds, m2s, v2s = (dict(zip(params, outs[k * n:(k + 1) * n])) for k in range(4))

    def update(name, g, at=slice(None)):
        m = ADAM_B1 * ms[name][at] + (1.0 - ADAM_B1) * g
        v = ADAM_B2 * vs[name][at] + (1.0 - ADAM_B2) * (g * g)
        m_hat = m / (1.0 - ADAM_B1 ** ADAM_STEP)
        v_hat = v / (1.0 - ADAM_B2 ** ADAM_STEP)
        gs[name][at] = g
        ds[name][at] = -ADAM_LR * (m_hat / (jnp.sqrt(v_hat) + ADAM_EPS) + ADAM_WD * ws[name][at])
        m2s[name][at] = m
        v2s[name][at] = v

    return update, refs[3 * n:3 * n + n_sources], outs[4 * n:]


def _adamw_late_body(*refs):
    update, (gng_ref, gng_others_ref), _ = _adamw_refs(LATE_PARAMS, 2, refs)
    me = 4 * lax.axis_index("x") + 2 * lax.axis_index("y") + lax.axis_index("c")
    g_norm = jnp.zeros(gng_ref.shape, F32)
    for k in range(N_DEV):
        g_norm = g_norm + jnp.where(me == k, gng_ref[...], gng_others_ref[k])
    update("norm_g", jnp.sum(g_norm, axis=0, keepdims=True))


def _adamw_body(*refs):
    update, sources, (loss_ref,) = _adamw_refs(MAIN_PARAMS, 7, refs)
    gwin_ref, gwin_others_ref, gwout_ref, gpww_ref, vec_ref, gdw_ref, gpoolw_ref = sources
    ws = dict(zip(MAIN_PARAMS, refs))
    row = lambda r, lo, hi: vec_ref[r:r + 1, lo:hi]
    loss_ref[...] = row(R_LOSS, 0, 1)
    g_w_in = gwin_ref[...]
    for j in range(3):
        g_w_in = g_w_in + gwin_others_ref[j].astype(F32)
    update("w_in", g_w_in)
    update("pool_w", gpoolw_ref[...].reshape(ws["pool_w"].shape))
    for g in range(len(POOL_WINDOWS)):
        update("pool_b", row(R_POOL, g * POOL_GC, (g + 1) * POOL_GC), at=slice(g, g + 1))
    update("pool_scale", row(R_POOL, W_MIX, 2 * W_MIX))
    update("conv_dw", gdw_ref[...])
    update("conv_b", row(R_CONV, 0, W_MIX))
    update("ln_g", row(R_CONV, W_MIX, 2 * W_MIX))
    update("ln_b", row(R_LN, 0, W_MIX))
    update("pw_w", gpww_ref[...])
    update("pw_b", row(R_LN, W_MIX, 2 * W_MIX))
    update("w_out", gwout_ref[...])
    update("final_g", row(R_FINAL_G, 0, D_MODEL))


def _adamw(body, name, params, ws, ms, vs, grad_sources, extra_out_shapes=()):
    n = len(params)
    flat = lambda a: a.reshape(1, -1) if a.ndim == 1 else a
    args = [flat(d[p]) for d in (ws, ms, vs) for p in params]
    outs = pl.pallas_call(
        body,
        name=name,
        in_specs=[VMEM] * (3 * n + len(grad_sources)),
        out_specs=[VMEM] * (4 * n + len(extra_out_shapes)),
        out_shape=[jax.ShapeDtypeStruct(a.shape, F32) for a in args[:n]] * 4 + list(extra_out_shapes),
        compiler_params=pltpu.CompilerParams(vmem_limit_bytes=VMEM_LIMIT),
    )(*args, *grad_sources)
    groups = tuple({p: outs[k * n + j].reshape(ws[p].shape) for j, p in enumerate(params)} for k in range(4))
    return groups, outs[4 * n:]


def kernel(x, norm_g, w_in, pool_w, pool_b, pool_scale, conv_dw, conv_b, ln_g, ln_b, pw_w, pw_b, w_out, final_g, loss_target, m_norm_g, m_w_in, m_pool_w, m_pool_b, m_pool_scale, m_conv_dw, m_conv_b, m_ln_g, m_ln_b, m_pw_w, m_pw_b, m_w_out, m_final_g, v_norm_g, v_w_in, v_pool_w, v_pool_b, v_pool_scale, v_conv_dw, v_conv_b, v_ln_g, v_ln_b, v_pw_w, v_pw_b, v_w_out, v_final_g):
    proj, w_in_t, w_out_b, pw_w_b, conv_dw_f = _inproj(x, norm_g, w_in.T, w_out, pw_w, conv_dw)
    h1, z, ob, dpool, dh, loss8, gfg8 = _fwd(x, loss_target, proj, pool_w, pool_b, pool_scale, conv_dw_f,
                                             conv_b, ln_g, ln_b, pw_w_b, pw_b, w_out_b, final_g)
    dproj, gwout, gpww, gpoolw, gvec, gdw8 = _bwd1(dh, proj, h1, z, ob, dpool, w_out_b, pw_w_b, pool_w, pool_scale,
                                                   conv_dw_f, ln_g, ln_b)
    win_own, win_staged, g_w_out, g_pw_w, vec, g_conv_dw, poolw_sum = _bwd2a(x, dproj, norm_g, gwout, gpww, loss8, gfg8, gvec,
                                                                             gdw8, gpoolw)
    send_sems, recv_sems, staged_thru, land_thru, token = _rs_start(win_staged)
    grad_x, gng8 = _bwd2b(x, dh, dproj, norm_g, w_in_t, token)
    ng_send, ng_recv, gng8, gng_land = _partials_start(gng8)
    win_others = _rs_wait(send_sems, recv_sems, staged_thru, land_thru, gng8)

    weights = dict(norm_g=norm_g, w_in=w_in.T, pool_w=pool_w, pool_b=pool_b, pool_scale=pool_scale, conv_dw=conv_dw, conv_b=conv_b,
                   ln_g=ln_g, ln_b=ln_b, pw_w=pw_w, pw_b=pw_b, w_out=w_out, final_g=final_g)
    m_in = dict(norm_g=m_norm_g, w_in=m_w_in.T, pool_w=m_pool_w, pool_b=m_pool_b, pool_scale=m_pool_scale, conv_dw=m_conv_dw,
                conv_b=m_conv_b, ln_g=m_ln_g, ln_b=m_ln_b, pw_w=m_pw_w, pw_b=m_pw_b, w_out=m_w_out, final_g=m_final_g)
    v_in = dict(norm_g=v_norm_g, w_in=v_w_in.T, pool_w=v_pool_w, pool_b=v_pool_b, pool_scale=v_pool_scale, conv_dw=v_conv_dw,
                conv_b=v_conv_b, ln_g=v_ln_g, ln_b=v_ln_b, pw_w=v_pw_w, pw_b=v_pw_b, w_out=v_w_out, final_g=v_final_g)
    main, (loss,) = _adamw(functools.partial(_adamw_body), "adamw", MAIN_PARAMS, weights, m_in, v_in,
                           (win_own, win_others, g_w_out, g_pw_w, vec, g_conv_dw, poolw_sum), [jax.ShapeDtypeStruct((1, 1), F32)])
    gng8, gng_others = _partials_wait(ng_send, ng_recv, gng8, gng_land, main[0]["w_out"])
    late, _ = _adamw(functools.partial(_adamw_late_body), "adamw_norm_g", LATE_PARAMS, weights, m_in, v_in, (gng8, gng_others))
    groups = [[(late[k] if p in LATE_PARAMS else main[k])[p] for p in PARAMS] for k in range(4)]
    k_in = PARAMS.index("w_in")
    for group in groups:
        group[k_in] = group[k_in].T
    grads, delta, new_m, new_v = groups
    return (loss.reshape(()), grad_x, *grads, *delta, *new_m, *new_v)
```

```python
import functools

import jax
import jax.numpy as jnp
from jax import lax
from jax.experimental import pallas as pl
from jax.experimental.pallas import tpu as pltpu

F32 = jnp.float32
BF16 = jnp.bfloat16

D_MODEL = 1024
W_MIX = 512
D_IN = 5 * W_MIX
POOL_WINDOWS = (2, 4, 8, 16)
POOL_GC = 128
CONV_WIDTH = 31
RMS_EPS = 1e-6
LN_EPS = 1e-5
N_DEV = 8

LANES = 128
SUBLANES = 8
TILE_M = 512
TILE_IN = 1024
POOL_HALO = 16
CONV_HALO = 32
CONV_ROWS = 512
BWD_CONV_ROWS = 64
BWD_SUBTILES = 2
VMEM_LIMIT = 56 * 1024 * 1024

VMEM = pl.BlockSpec(memory_space=pltpu.VMEM)


def _silu(v):
    return v * jax.nn.sigmoid(v)


def _colsum8(v):
    m, n = v.shape
    return jnp.sum(v.reshape(m // SUBLANES, SUBLANES, n), axis=0)


def _row_index(shape, t0):
    return lax.broadcasted_iota(jnp.int32, shape, 0) + t0


def _conv_taps():
    taps = {s: [] for s in range(SUBLANES)}
    for k in range(CONV_WIDTH):
        e = k + CONV_HALO - (CONV_WIDTH - 1)
        taps[e % SUBLANES].append((e // SUBLANES, k))
    return taps


def _fwd_body(x_ref, tgt_ref, proj_ref, poolw_ref, poolb_ref, pools_ref, dw_ref, cb_ref, lg_ref, lb_ref,
              pww_ref, pwb_ref, wout_ref, fg_ref,
              h1_ref, z_ref, ob_ref, dpool_ref, dh_ref, loss_ref, gfg_ref,
              uext, hext, y_s):
    b = pl.program_id(0)
    i = pl.program_id(1)
    tm = TILE_M

    @pl.when(i == 0)
    def _():
        uext[0:POOL_HALO, :] = jnp.zeros((POOL_HALO, W_MIX), F32)
        hext[0:CONV_HALO, :] = jnp.zeros((CONV_HALO, W_MIX), F32)
        hext[CONV_HALO + tm:, :] = jnp.zeros((SUBLANES, W_MIX), F32)

    @pl.when((b == 0) & (i == 0))
    def _():
        loss_ref[...] = jnp.zeros_like(loss_ref)
        gfg_ref[...] = jnp.zeros_like(gfg_ref)

    uext[POOL_HALO:, :] = proj_ref[0, :, 0:W_MIX]
    t = _row_index((tm, POOL_GC), i * tm)
    for g, w in enumerate(POOL_WINDOWS):
        lanes = slice(g * POOL_GC, (g + 1) * POOL_GC)
        e = uext[:, lanes]
        s = e
        sh = 1
        while sh < w:
            s = s + pltpu.roll(s, sh, 0)
            sh *= 2
        inv_cnt = 1.0 / jnp.minimum(t + 1, w).astype(F32)
        d = (s[POOL_HALO:] * inv_cnt - e[POOL_HALO:]).astype(BF16)
        dpool_ref[0, :, lanes] = d
        z = jnp.dot(d, poolw_ref[g].astype(BF16), preferred_element_type=F32) + poolb_ref[:, lanes]
        z_ref[0, :, lanes] = z
        a_gate = proj_ref[0, :, W_MIX + g * POOL_GC:W_MIX + (g + 1) * POOL_GC]
        y_s[:, lanes] = (z * pools_ref[:, lanes] * _silu(a_gate)).astype(BF16)

    hext[CONV_HALO:CONV_HALO + tm, :] = proj_ref[0, :, 2 * W_MIX:3 * W_MIX] * jax.nn.sigmoid(proj_ref[0, :, 3 * W_MIX:4 * W_MIX])
    taps = _conv_taps()
    for c0 in range(0, tm, CONV_ROWS):
        for l0 in range(0, W_MIX, LANES):
            lanes = slice(l0, l0 + LANES)
            acc = jnp.zeros((CONV_ROWS, LANES), F32) + cb_ref[:, lanes]
            for s in range(SUBLANES):
                part = jnp.zeros((CONV_ROWS + SUBLANES, LANES), F32)
                for q, k in taps[s]:
                    r0 = c0 + SUBLANES * q
                    part = part + hext[r0:r0 + CONV_ROWS + SUBLANES, lanes] * dw_ref[k:k + 1, lanes]
                acc = acc + part[s:s + CONV_ROWS]
            h1_ref[0, c0:c0 + CONV_ROWS, lanes] = acc
    h1 = h1_ref[0]
    mu = jnp.mean(h1, axis=-1, keepdims=True)
    xc = h1 - mu
    rstd = lax.rsqrt(jnp.mean(xc * xc, axis=-1, keepdims=True) + LN_EPS)
    h3 = _silu(xc * rstd * lg_ref[...] + lb_ref[...])
    ob = jnp.dot(h3.astype(BF16), pww_ref[...], preferred_element_type=F32) + pwb_ref[...]
    ob_ref[0] = ob
    y_s[:, W_MIX:] = (ob * _silu(proj_ref[0, :, 4 * W_MIX:5 * W_MIX])).astype(BF16)

    h = x_ref[0] + jnp.dot(y_s[...], wout_ref[...], preferred_element_type=F32)
    r2 = lax.rsqrt(jnp.mean(h * h, axis=-1, keepdims=True) + RMS_EPS)
    hr = h * r2
    err = hr * fg_ref[...] - tgt_ref[0]
    loss_ref[...] += jnp.sum(err * err) * (0.5 / D_MODEL)
    dout = err * (1.0 / D_MODEL)
    gfg_ref[...] += _colsum8(dout * hr)
    gy = dout * fg_ref[...]
    dh_ref[0] = r2 * (gy - hr * jnp.mean(gy * hr, axis=-1, keepdims=True))

    uext[0:POOL_HALO, :] = uext[tm:tm + POOL_HALO, :]
    hext[0:CONV_HALO, :] = hext[tm:tm + CONV_HALO, :]


def _row(v):
    return v.reshape(1, -1)


def _fwd(x, tgt, proj, pool_w, pool_b, pool_scale, conv_dw_f, conv_b, ln_g, ln_b, pw_w_b, pw_b, w_out_b, final_g):
    bl, s, _ = x.shape
    tm = TILE_M
    tok = lambda n: pl.BlockSpec((1, tm, n), lambda b, i: (b, i, 0))
    acc = lambda n: pl.BlockSpec((SUBLANES, n), lambda b, i: (0, 0))
    act = lambda n: jax.ShapeDtypeStruct((bl, s, n), F32)
    return pl.pallas_call(
        functools.partial(_fwd_body),
        name="fwd",
        grid=(bl, s // tm),
        in_specs=[tok(D_MODEL), tok(D_MODEL), tok(D_IN)] + [VMEM] * 11,
        out_specs=[tok(W_MIX), tok(W_MIX), tok(W_MIX), tok(W_MIX), tok(D_MODEL), acc(LANES), acc(D_MODEL)],
        out_shape=[act(W_MIX), act(W_MIX), act(W_MIX), jax.ShapeDtypeStruct((bl, s, W_MIX), BF16), act(D_MODEL),
                   jax.ShapeDtypeStruct((SUBLANES, LANES), F32), jax.ShapeDtypeStruct((SUBLANES, D_MODEL), F32)],
        scratch_shapes=[pltpu.VMEM((POOL_HALO + tm, W_MIX), F32), pltpu.VMEM((CONV_HALO + tm + SUBLANES, W_MIX), F32),
                        pltpu.VMEM((tm, D_MODEL), BF16)],
        compiler_params=pltpu.CompilerParams(dimension_semantics=("arbitrary", "arbitrary"), vmem_limit_bytes=VMEM_LIMIT),
    )(x, tgt, proj, pool_w, pool_b.reshape(1, W_MIX), _row(pool_scale), conv_dw_f, _row(conv_b),
      _row(ln_g), _row(ln_b), pw_w_b, _row(pw_b), w_out_b, _row(final_g))


NT_DIMS = (((1,), (1,)), ((), ()))
TN_DIMS = (((0,), (0,)), ((), ()))
V_POOL_B, V_POOL_SCALE, V_CONV_B, V_LN_G, V_LN_B, V_PW_B = range(6)


def _bwd_taps():
    taps = {s: [] for s in range(SUBLANES)}
    for k in range(CONV_WIDTH):
        e = CONV_WIDTH - 1 - k
        taps[e % SUBLANES].append((e // SUBLANES, k))
    return taps


def _bwd1_body(n_tiles, dh_ref, proj_ref, h1_ref, z_ref, ob_ref, dpool_ref, wout_ref, pww_ref, poolw_ref, pools_ref, dw_ref,
               lg_ref, lb_ref,
               dproj_ref, gwout_ref, gpww_ref, gpoolw_ref, gvec_ref, gdw_ref,
               vext, gext, gsh, y_s):
    b = pl.program_id(0)
    i = pl.program_id(1)
    tm = TILE_M
    t0 = (n_tiles - 1 - i) * tm

    @pl.when(i == 0)
    def _():
        vext[tm:, :] = jnp.zeros((POOL_HALO, W_MIX), F32)
        gext[tm:, :] = jnp.zeros((CONV_HALO, W_MIX), F32)

    @pl.when((b == 0) & (i == 0))
    def _():
        gwout_ref[...] = jnp.zeros_like(gwout_ref)
        gpww_ref[...] = jnp.zeros_like(gpww_ref)
        gpoolw_ref[...] = jnp.zeros_like(gpoolw_ref)
        gvec_ref[...] = jnp.zeros_like(gvec_ref)
        gdw_ref[...] = jnp.zeros_like(gdw_ref)

    sub = tm // BWD_SUBTILES

    def dense_phase(r0):
        rs = slice(r0, r0 + sub)
        dhb = dh_ref[0, rs, :].astype(BF16)
        dy_a = lax.dot_general(dhb, wout_ref[0:W_MIX, :], NT_DIMS, preferred_element_type=F32)
        dy_b = lax.dot_general(dhb, wout_ref[W_MIX:, :], NT_DIMS, preferred_element_type=F32)

        a_gate = proj_ref[0, rs, W_MIX:2 * W_MIX]
        sig_a = jax.nn.sigmoid(a_gate)
        silu_a = a_gate * sig_a
        z = z_ref[0, rs, :]
        out_a = z * pools_ref[...]
        y_s[rs, 0:W_MIX] = (out_a * silu_a).astype(BF16)
        d_out_a = dy_a * silu_a
        d_ag = dy_a * out_a * (sig_a * (1.0 + a_gate * (1.0 - sig_a)))
        dproj_ref[0, rs, W_MIX:2 * W_MIX] = d_ag.astype(BF16)
        gvec_ref[V_POOL_SCALE] += _colsum8(d_out_a * z)
        dz = d_out_a * pools_ref[...]
        gvec_ref[V_POOL_B] += _colsum8(dz)
        t = _row_index((sub, POOL_GC), t0 + r0)
        n_ext = sub + POOL_HALO
        for g, w in enumerate(POOL_WINDOWS):
            lanes = slice(g * POOL_GC, (g + 1) * POOL_GC)
            dzg = dz[:, lanes].astype(BF16)
            gpoolw_ref[g] += lax.dot_general(dpool_ref[0, rs, lanes], dzg, TN_DIMS, preferred_element_type=F32)
            dd = lax.dot_general(dzg, poolw_ref[g].astype(BF16), NT_DIMS, preferred_element_type=F32)
            vext[rs, lanes] = dd * (1.0 / jnp.minimum(t + 1, w).astype(F32))
            s = vext[r0:r0 + n_ext, lanes]
            sh = 1
            while sh < w:
                s = s + pltpu.roll(s, n_ext - sh, 0)
                sh *= 2
            dproj_ref[0, rs, lanes] = (s[0:sub] - dd).astype(BF16)

        b_gate = proj_ref[0, rs, 4 * W_MIX:5 * W_MIX]
        sig_b = jax.nn.sigmoid(b_gate)
        silu_b = b_gate * sig_b
        ob = ob_ref[0, rs, :]
        y_s[rs, W_MIX:] = (ob * silu_b).astype(BF16)
        d_ob = dy_b * silu_b
        dproj_ref[0, rs, 4 * W_MIX:5 * W_MIX] = (dy_b * ob * (sig_b * (1.0 + b_gate * (1.0 - sig_b)))).astype(BF16)
        gvec_ref[V_PW_B] += _colsum8(d_ob)
        gwout_ref[...] += lax.dot_general(y_s[rs, :], dhb, TN_DIMS, preferred_element_type=F32)

        h1 = h1_ref[0, rs, :]
        mu = jnp.mean(h1, axis=-1, keepdims=True)
        xc = h1 - mu
        rstd = lax.rsqrt(jnp.mean(xc * xc, axis=-1, keepdims=True) + LN_EPS)
        xhat = xc * rstd
        h2 = xhat * lg_ref[...] + lb_ref[...]
        sig2 = jax.nn.sigmoid(h2)
        d_obb = d_ob.astype(BF16)
        gpww_ref[...] += lax.dot_general((h2 * sig2).astype(BF16), d_obb, TN_DIMS, preferred_element_type=F32)
        dh2 = lax.dot_general(d_obb, pww_ref[...], NT_DIMS, preferred_element_type=F32) * (sig2 * (1.0 + h2 * (1.0 - sig2)))
        gvec_ref[V_LN_G] += _colsum8(dh2 * xhat)
        gvec_ref[V_LN_B] += _colsum8(dh2)
        dxh = dh2 * lg_ref[...]
        dh1 = rstd * (dxh - jnp.mean(dxh, axis=-1, keepdims=True) - xhat * jnp.mean(dxh * xhat, axis=-1, keepdims=True))
        gvec_ref[V_CONV_B] += _colsum8(dh1)
        gext[rs, :] = dh1

    taps = _bwd_taps()
    cr = BWD_CONV_ROWS
    win = cr + CONV_HALO - SUBLANES

    def conv_phase(r0):
        for c0 in range(r0, r0 + sub, cr):
            for l0 in range(0, W_MIX, LANES):
                lanes = slice(l0, l0 + LANES)
                b_val = proj_ref[0, c0:c0 + cr, 2 * W_MIX + l0:2 * W_MIX + l0 + LANES]
                sg = jax.nn.sigmoid(proj_ref[0, c0:c0 + cr, 3 * W_MIX + l0:3 * W_MIX + l0 + LANES])
                h0c = b_val * sg
                acc = jnp.zeros((cr, LANES), F32)
                for s in range(SUBLANES):
                    gsh[...] = gext[c0 + s:c0 + s + win, lanes]
                    for q, k in taps[s]:
                        gq = gsh[SUBLANES * q:SUBLANES * q + cr, :]
                        acc = acc + gq * dw_ref[k:k + 1, lanes]
                        gdw_ref[k, :, lanes] += _colsum8(h0c * gq)
                d_bval = acc * sg
                dproj_ref[0, c0:c0 + cr, 2 * W_MIX + l0:2 * W_MIX + l0 + LANES] = d_bval.astype(BF16)
                dproj_ref[0, c0:c0 + cr, 3 * W_MIX + l0:3 * W_MIX + l0 + LANES] = (d_bval * b_val * (1.0 - sg)).astype(BF16)

    starts = [k * sub for k in reversed(range(BWD_SUBTILES))]
    for r0 in starts:
        dense_phase(r0)
    for r0 in starts:
        conv_phase(r0)
    vext[tm:, :] = vext[0:POOL_HALO, :]
    gext[tm:, :] = gext[0:CONV_HALO, :]


def _bwd1(dh, proj, h1, z, ob, dpool, w_out_b, pw_w_b, pool_w, pool_scale, conv_dw_f, ln_g, ln_b):
    bl, s, _ = dh.shape
    tm = TILE_M
    nt = s // tm
    tok = lambda n: pl.BlockSpec((1, tm, n), lambda b, i: (b, nt - 1 - i, 0))
    res = lambda shape: pl.BlockSpec(shape, lambda b, i: (0,) * len(shape))
    out_shapes = [(D_MODEL, D_MODEL), (W_MIX, W_MIX), (len(POOL_WINDOWS), POOL_GC, POOL_GC), (6, SUBLANES, W_MIX),
                  (CONV_WIDTH, SUBLANES, W_MIX)]
    return pl.pallas_call(
        functools.partial(_bwd1_body, nt),
        name="bwd1",
        grid=(bl, nt),
        in_specs=[tok(D_MODEL), tok(D_IN), tok(W_MIX), tok(W_MIX), tok(W_MIX), tok(W_MIX)] + [VMEM] * 7,
        out_specs=[tok(D_IN)] + [res(sh) for sh in out_shapes],
        out_shape=[jax.ShapeDtypeStruct((bl, s, D_IN), BF16)] + [jax.ShapeDtypeStruct(sh, F32) for sh in out_shapes],
        scratch_shapes=[pltpu.VMEM((tm + POOL_HALO, W_MIX), F32), pltpu.VMEM((tm + CONV_HALO, W_MIX), F32),
                        pltpu.VMEM((BWD_CONV_ROWS + CONV_HALO - SUBLANES, LANES), F32), pltpu.VMEM((tm, D_MODEL), BF16)],
        compiler_params=pltpu.CompilerParams(dimension_semantics=("arbitrary", "arbitrary"), vmem_limit_bytes=VMEM_LIMIT),
    )(dh, proj, h1, z, ob, dpool, w_out_b, pw_w_b, pool_w, _row(pool_scale), conv_dw_f, _row(ln_g), _row(ln_b))


MESH = pl.DeviceIdType.MESH
W_IN_BLK = D_IN // N_DEV
W_OUT_BLK = D_MODEL // N_DEV
PW_BLK = W_MIX // N_DEV
DW_BLK = W_MIX // N_DEV
N_PEER = N_DEV - 1


SLAB = 2 * W_IN_BLK
N_SLABS = 4
F_SIB, F_X, F_Y, F_ON, F_PASS_X, F_PASS_Y, F_PASS_D = range(7)
W_IN_ONLY, OTHERS = (0,), (1, 2, 3)


def _inproj_body(n_tiles, x_hbm, ng_ref, wint_ref, wout_ref, pww_ref, dw_ref,
                 proj_hbm, wt_hbm, woutf_hbm, pwwf_hbm, dwf_ref,
                 hn_s, wt_s, wout_s, pww_s, dw_blk, stage, xbuf, out_sems, x_sems, w_sems, send_sems, recv_sems):
    s = pl.program_id(0)
    i = pl.program_id(1)
    tm = TILE_IN
    x, y, c = lax.axis_index("x"), lax.axis_index("y"), lax.axis_index("c")
    sibling = (x, y, 1 - c)
    xn, yn, dg = (1 - x, y), (x, 1 - y), (1 - x, 1 - y)
    pick = lambda a, b: tuple(jnp.where(c == 1, p, q) for p, q in zip(a, b))
    on_from, on_to = pick(xn, yn), pick(yn, xn)

    def blocks(px, py, pc):
        k = 4 * px + 2 * py + pc
        rows = lambda ref, n: ref.at[pl.ds(pl.multiple_of(k * n, n), n), :]
        return (rows(wt_s, W_IN_BLK), rows(wout_s, W_OUT_BLK), rows(pww_s, PW_BLK), dw_blk.at[k])

    def copies(arrays, flow, block, to):
        refs = blocks(*block)
        return [pltpu.make_async_remote_copy(src_ref=refs[a], dst_ref=refs[a], send_sem=send_sems.at[a, flow],
                                             recv_sem=recv_sems.at[a, flow], device_id=to, device_id_type=MESH)
                for a in arrays]

    def start(arrays):
        for flow, to in ((F_SIB, sibling), (F_X, (*xn, c)), (F_Y, (*yn, c))):
            for cp in copies(arrays, flow, (x, y, c), to):
                cp.start()

    def direct(arrays):
        for flow, chip in ((F_X, xn), (F_Y, yn)):
            for cp in copies(arrays, flow, (*chip, c), sibling):
                cp.wait_recv()
        for cp in copies(arrays, F_ON, (*on_from, c), (*on_to, c)):
            cp.start()
        for flow, chip in ((F_PASS_X, xn), (F_PASS_Y, yn)):
            for cp in copies(arrays, flow, (*chip, c), sibling):
                cp.start()

    def diagonal(arrays):
        for cp in copies(arrays, F_ON, (*dg, c), sibling):
            cp.wait_recv()
        for cp in copies(arrays, F_PASS_D, (*dg, c), sibling):
            cp.start()

    def from_sibling(arrays, flow, chip):
        for cp in copies(arrays, flow, (*chip, 1 - c), sibling):
            cp.wait_recv()

    def finish(arrays):
        mine = (x, y, c)
        for flow, block in ((F_SIB, mine), (F_X, mine), (F_Y, mine), (F_ON, (*on_from, c)), (F_PASS_X, (*xn, c)),
                            (F_PASS_Y, (*yn, c)), (F_PASS_D, (*dg, c))):
            for cp in copies(arrays, flow, block, sibling):
                cp.wait_send()

    step = s * n_tiles + i
    par = step % 2
    first = step == 0

    @pl.when(first)
    def _():
        mine = blocks(x, y, c)
        mine[0][...] = wint_ref[...].astype(BF16)
        mine[1][...] = wout_ref[...].astype(BF16)
        mine[2][...] = pww_ref[...].astype(BF16)
        mine[3][...] = dw_ref[...]
        start(W_IN_ONLY)
        start(OTHERS)

    rows_i = pl.ds(pl.multiple_of(i * tm, tm), tm)
    tiles_per_seq = proj_hbm.shape[1] // tm

    def x_copy(k):
        src = x_hbm.at[k // tiles_per_seq, pl.ds(pl.multiple_of((k % tiles_per_seq) * tm, tm), tm), :]
        return pltpu.make_async_copy(src, xbuf.at[k % 2], x_sems.at[k % 2])

    @pl.when(first)
    def _():
        x_copy(0).start()

    @pl.when(s == 0)
    def _():
        x_copy(i).wait()

        @pl.when(i + 1 < n_tiles)
        def _():
            x_copy(i + 1).start()

        xt = xbuf[i % 2]
        r1 = lax.rsqrt(jnp.mean(xt * xt, axis=-1, keepdims=True) + RMS_EPS)
        hn_s[rows_i, :] = (xt * r1 * ng_ref[...]).astype(BF16)

    @pl.when(first)
    def _():
        from_sibling(W_IN_ONLY, F_SIB, (x, y))

    @pl.when((s == 1) & (i == 0))
    def _():
        direct(W_IN_ONLY)
        from_sibling(W_IN_ONLY, F_PASS_Y, yn)

    @pl.when((s == 2) & (i == 0))
    def _():
        from_sibling(W_IN_ONLY, F_PASS_X, xn)

    weights_out = [pltpu.make_async_copy(src, dst, w_sems.at[k])
                   for k, (src, dst) in enumerate(((wt_s, wt_hbm), (wout_s, woutf_hbm), (pww_s, pwwf_hbm)))]

    @pl.when((s == 3) & (i == 0))
    def _():
        diagonal(W_IN_ONLY)
        direct(OTHERS)
        from_sibling(W_IN_ONLY, F_PASS_D, dg)
        weights_out[0].start()

    def out_copy(parity, col0):
        dst = proj_hbm.at[i // tiles_per_seq, pl.ds(pl.multiple_of((i % tiles_per_seq) * tm, tm), tm),
                          pl.ds(pl.multiple_of(col0, LANES), SLAB)]
        return pltpu.make_async_copy(stage.at[parity], dst, out_sems.at[parity])

    @pl.when(step >= 2)
    def _():
        out_copy(par, 0).wait()

    chip = jnp.where(s < 2, 2 * x, 2 * (1 - x)) + jnp.where((s == 0) | (s == 2), y, 1 - y)
    slab = wt_s[pl.ds(pl.multiple_of(chip * SLAB, SLAB), SLAB), :]
    stage[par] = lax.dot_general(hn_s[rows_i, :], slab, NT_DIMS, preferred_element_type=F32)
    out_copy(par, chip * SLAB).start()

    @pl.when(step == N_SLABS * n_tiles - 1)
    def _():
        out_copy(par, 0).wait()
        out_copy(1 - par, 0).wait()
        from_sibling(OTHERS, F_SIB, (x, y))
        from_sibling(OTHERS, F_PASS_X, xn)
        from_sibling(OTHERS, F_PASS_Y, yn)
        diagonal(OTHERS)
        from_sibling(OTHERS, F_PASS_D, dg)
        for cp in weights_out[1:]:
            cp.start()
        finish(W_IN_ONLY)
        finish(OTHERS)
        for cp in weights_out:
            cp.wait()
        for k in range(N_DEV):
            dwf_ref[:, k * DW_BLK:(k + 1) * DW_BLK] = dw_blk[k]


def _inproj(x, norm_g, w_in_t, w_out, pw_w, conv_dw):
    bl, s, _ = x.shape
    tm = TILE_IN
    nt = s // tm
    n_tiles = bl * nt
    dma = pltpu.SemaphoreType.DMA
    return pl.pallas_call(
        functools.partial(_inproj_body, n_tiles),
        name="inproj",
        grid=(N_SLABS, n_tiles),
        in_specs=[pl.BlockSpec(memory_space=pl.ANY)] + [VMEM] * 5,
        out_specs=[pl.BlockSpec(memory_space=pl.ANY)] * 4 + [VMEM],
        out_shape=[jax.ShapeDtypeStruct((bl, s, D_IN), F32), jax.ShapeDtypeStruct((D_IN, D_MODEL), BF16),
                   jax.ShapeDtypeStruct((D_MODEL, D_MODEL), BF16), jax.ShapeDtypeStruct((W_MIX, W_MIX), BF16),
                   jax.ShapeDtypeStruct((CONV_WIDTH, W_MIX), F32)],
        scratch_shapes=[pltpu.VMEM((n_tiles * tm, D_MODEL), BF16), pltpu.VMEM((D_IN, D_MODEL), BF16),
                        pltpu.VMEM((D_MODEL, D_MODEL), BF16), pltpu.VMEM((W_MIX, W_MIX), BF16),
                        pltpu.VMEM((N_DEV, CONV_WIDTH, DW_BLK), F32), pltpu.VMEM((2, tm, SLAB), F32),
                        pltpu.VMEM((2, tm, D_MODEL), F32), dma((2,)), dma((2,)), dma((3,)), dma((4, 7)), dma((4, 7))],
        compiler_params=pltpu.CompilerParams(dimension_semantics=("arbitrary", "arbitrary"), vmem_limit_bytes=VMEM_LIMIT),
    )(x, _row(norm_g), w_in_t, w_out, pw_w, conv_dw)


B_PLAIN, B_ON, B_MERGED = range(3)


def _routes():
    x, y, c = lax.axis_index("x"), lax.axis_index("y"), lax.axis_index("c")
    xn, yn = (1 - x, y), (x, 1 - y)
    pick = lambda a, b: tuple(jnp.where(c == 1, p, q) for p, q in zip(a, b))
    return pick(xn, yn), pick(yn, xn), (1 - x, 1 - y), (x, y), c


def _chip_index(chip):
    return 2 * chip[0] + chip[1]


def _copy(src, dst, send_sems, recv_sems, a, k, to):
    return pltpu.make_async_remote_copy(src_ref=src, dst_ref=dst, send_sem=send_sems.at[a, k], recv_sem=recv_sems.at[a, k],
                                        device_id=to, device_id_type=MESH)


def _rsb_send(plain, on, rb_plain, rb_on, send_b, recv_b):
    first, _, _, _, c = _routes()
    for k, srcs, dsts in ((B_ON, on, rb_on), (B_PLAIN, plain, rb_plain)):
        for a, (s, d) in enumerate(zip(srcs, dsts)):
            _copy(s, d, send_b, recv_b, a, k, (*first, c)).start()


def _rsb_merge(second_sums, merged, rb_on, rb_merged, send_b, recv_b):
    _, second, _, _, c = _routes()
    for a, (own, m, got, dst) in enumerate(zip(second_sums, merged, rb_on, rb_merged)):
        _copy(got, got, send_b, recv_b, a, B_ON, (*second, c)).wait_recv()
        m[...] = (own[...] + got[...].astype(F32)).astype(BF16)
        _copy(m, dst, send_b, recv_b, a, B_MERGED, (*second, c)).start()


def _rsb_total(own_sums, plain, on, merged, rb_plain, rb_merged, send_b, recv_b):
    _, second, _, _, c = _routes()
    out = []
    for a, (own, p, g) in enumerate(zip(own_sums, rb_plain, rb_merged)):
        _copy(p, p, send_b, recv_b, a, B_PLAIN, (*second, c)).wait_recv()
        _copy(g, g, send_b, recv_b, a, B_MERGED, (*second, c)).wait_recv()
        out.append(own[...] + p[...].astype(F32) + g[...].astype(F32))
    for k, srcs in ((B_PLAIN, plain), (B_ON, on), (B_MERGED, merged)):
        for a, s in enumerate(srcs):
            _copy(s, s, send_b, recv_b, a, k, (*second, c)).wait_send()
    return out


R_LOSS, R_FINAL_G, R_POOL, R_CONV, R_LN = range(5)
VEC_ROWS = 16
DW_ROWS = 32
A_PART, A_X, A_Y, A_ON, A_HALF = range(5)
HALF_DTYPES = (F32, BF16, BF16)
STEP_SUMS, STEP_MERGE, STEP_HALF = 1, 3, 5


def _bwd2a_body(n_steps, x_ref, dproj_ref, ng_ref, gwout_hbm, gpww_hbm, loss_ref, gfg_ref, gvec_ref, gdw_ref, gpoolw_ref,
                win_own_ref, win_staged_ref, owout_ref, opww_ref, ovec_ref, odw_ref, opoolw_ref,
                acc, st_win, a_win, own_wout, own_pww, a_wout, a_pww, st_wout, st_pww, rb_wout, rb_pww, part_vec, part_dw,
                part_pw, sib_vec, sib_dw, sib_pw, res_vec, res_dw, res_pw, half_vec, half_dw, half_pw,
                own_sems, send_a, recv_a, send_b, recv_b, send_r, recv_r, send_w, recv_w):
    step = pl.program_id(0) * pl.num_programs(1) + pl.program_id(1)
    x, y, c = lax.axis_index("x"), lax.axis_index("y"), lax.axis_index("c")
    sibling = (x, y, 1 - c)
    xn, yn = (1 - x, y), (x, 1 - y)
    first, second, diag, own, _ = _routes()
    rows = lambda ref, k, n: ref.at[pl.ds(pl.multiple_of(k * n, n), n), :]

    blocks = lambda k: (rows(gwout_hbm, k, W_OUT_BLK), rows(gpww_hbm, k, PW_BLK))
    mine = lambda chip: (own_wout.at[chip], own_pww.at[chip])
    mine_copies = lambda chip: [pltpu.make_async_copy(s, d, own_sems.at[a, chip])
                                for a, (s, d) in enumerate(zip(blocks(2 * chip + c), mine(chip)))]
    landed = lambda chip: (a_wout.at[chip], a_pww.at[chip])
    staged = lambda k: (st_wout.at[k], st_pww.at[k])
    arrived = lambda k: (rb_wout.at[k], rb_pww.at[k])

    parts, sibs, halves = (part_vec, part_dw, part_pw), (sib_vec, sib_dw, sib_pw), (half_vec, half_dw, half_pw)
    results = (res_vec, res_dw, res_pw)
    half_rows = lambda ref: pl.ds(pl.multiple_of(c * (ref.shape[0] // 2), SUBLANES), ref.shape[0] // 2)
    slot = lambda chip: tuple(h.at[_chip_index(chip)] for h in halves)

    @pl.when(step == 0)
    def _():
        acc[...] = jnp.zeros_like(acc)
        sum8 = lambda v: jnp.sum(v, axis=0, keepdims=True)
        part_vec[...] = jnp.zeros((VEC_ROWS, D_MODEL), F32)
        part_vec[R_LOSS:R_LOSS + 1, :] = jnp.broadcast_to(loss_ref[0:1, 0:1], (1, D_MODEL))
        part_vec[R_FINAL_G:R_FINAL_G + 1, :] = sum8(gfg_ref[...])
        for row, (lo, hi) in ((R_POOL, (V_POOL_B, V_POOL_SCALE)), (R_CONV, (V_CONV_B, V_LN_G)), (R_LN, (V_LN_B, V_PW_B))):
            part_vec[row:row + 1, 0:W_MIX] = sum8(gvec_ref[lo])
            part_vec[row:row + 1, W_MIX:] = sum8(gvec_ref[hi])
        part_dw[0:CONV_WIDTH, :] = jnp.sum(gdw_ref[...], axis=1)
        part_dw[CONV_WIDTH:, :] = jnp.zeros((DW_ROWS - CONV_WIDTH, W_MIX), F32)
        part_pw[...] = gpoolw_ref[...].reshape(len(POOL_WINDOWS) * POOL_GC, POOL_GC)
        for a, (p, s) in enumerate(zip(parts, sibs)):
            _copy(p, s, send_r, recv_r, a, A_PART, sibling).start()
        for chip in range(4):
            for a, (s, d) in enumerate(zip(blocks(2 * chip + (1 - c)), landed(chip))):
                _copy(s, d, send_a, recv_a, a, chip, sibling).start()
            for cp in mine_copies(chip):
                cp.start()

    x_t = x_ref[0]
    r1 = lax.rsqrt(jnp.mean(x_t * x_t, axis=-1, keepdims=True) + RMS_EPS)
    hn = (x_t * r1 * ng_ref[...]).astype(BF16)
    acc[...] += lax.dot_general(dproj_ref[0], hn, TN_DIMS, preferred_element_type=F32)

    @pl.when(step == STEP_SUMS)
    def _():
        for a, (p, s, h) in enumerate(zip(parts, sibs, slot(own))):
            _copy(p, s, send_r, recv_r, a, A_PART, sibling).wait_recv()
            h[...] = (p[half_rows(p), :] + s[half_rows(s), :]).astype(h.dtype)
            for k, to in ((A_X, xn), (A_Y, yn)):
                _copy(h, h, send_r, recv_r, a, k, (*to, c)).start()
        for chip in range(4):
            for cp in mine_copies(chip):
                cp.wait()
            for a, (m, l) in enumerate(zip(mine(chip), landed(chip))):
                _copy(l, l, send_a, recv_a, a, chip, sibling).wait_recv()
                l[...] = m[...] + l[...]
        for k, chip in ((B_ON, diag), (B_PLAIN, first)):
            for s, l in zip(staged(k), landed(_chip_index(chip))):
                s[...] = l[...].astype(BF16)
        _rsb_send(staged(B_PLAIN), staged(B_ON), arrived(B_PLAIN), arrived(B_ON), send_b, recv_b)

    @pl.when(step == STEP_MERGE)
    def _():
        _rsb_merge(landed(_chip_index(second)), staged(B_MERGED), arrived(B_ON), arrived(B_MERGED), send_b, recv_b)
        for a in range(3):
            for k, chip in ((A_X, xn), (A_Y, yn)):
                h = slot(chip)[a]
                _copy(h, h, send_r, recv_r, a, k, sibling).wait_recv()
            h = slot(first)[a]
            _copy(h, h, send_r, recv_r, a, A_ON, (*second, c)).start()

    @pl.when(step == STEP_HALF)
    def _():
        for a, (r, hs) in enumerate(zip(results, halves)):
            h = slot(diag)[a]
            _copy(h, h, send_r, recv_r, a, A_ON, sibling).wait_recv()
            mine = r.at[half_rows(r), :]
            chip_sums = [hs[k].astype(F32) for k in range(4)]
            mine[...] = (chip_sums[0] + chip_sums[1]) + (chip_sums[2] + chip_sums[3])
            _copy(mine, mine, send_r, recv_r, a, A_HALF, sibling).start()

    @pl.when(step == n_steps - 1)
    def _():
        block = lambda k: rows(acc, k, W_IN_BLK)
        w_copy = lambda chip: _copy(st_win.at[chip], a_win.at[chip], send_w, recv_w, 0, chip, sibling)
        for chip in range(4):
            st_win[chip] = block(2 * chip + (1 - c))[...].astype(BF16)
            w_copy(chip).start()

        owout_ref[...], opww_ref[...] = _rsb_total(landed(_chip_index(own)), staged(B_PLAIN), staged(B_ON), staged(B_MERGED),
                                                   arrived(B_PLAIN), arrived(B_MERGED), send_b, recv_b)
        for a, (r, p, s) in enumerate(zip(results, parts, sibs)):
            theirs = r.at[pl.ds(pl.multiple_of((1 - c) * (r.shape[0] // 2), SUBLANES), r.shape[0] // 2), :]
            _copy(theirs, theirs, send_r, recv_r, a, A_HALF, sibling).wait_recv()
            _copy(p, s, send_r, recv_r, a, A_PART, sibling).wait_send()
            for k, h in ((A_X, slot(own)[a]), (A_Y, slot(own)[a]), (A_ON, slot(first)[a])):
                _copy(h, h, send_r, recv_r, a, k, sibling).wait_send()
            mine = r.at[half_rows(r), :]
            _copy(mine, mine, send_r, recv_r, a, A_HALF, sibling).wait_send()
        for chip in range(4):
            for a, (s, d) in enumerate(zip(blocks(2 * chip + (1 - c)), landed(chip))):
                _copy(s, d, send_a, recv_a, a, chip, sibling).wait_send()
        ovec_ref[...] = res_vec[...]
        opoolw_ref[...] = res_pw[...]
        for k in range(N_DEV):
            @pl.when(4 * x + 2 * y + c == k)
            def _():
                odw_ref[...] = res_dw[0:CONV_WIDTH, k * DW_BLK:(k + 1) * DW_BLK]

        for chip in range(4):
            w_copy(chip).wait()
        chip_sum = lambda chip: block(2 * _chip_index(chip) + c)[...] + a_win[_chip_index(chip)].astype(F32)
        win_own_ref[...] = chip_sum(own)
        for j, chip in enumerate((xn, yn, diag)):
            win_staged_ref[j] = chip_sum(chip).astype(BF16)


def _bwd2a(x, dproj, norm_g, gwout, gpww, loss8, gfg8, gvec, gdw8, gpoolw):
    bl, s, _ = x.shape
    tm = TILE_M
    nt = s // tm
    tok = lambda n: pl.BlockSpec((1, tm, n), lambda b, i: (b, i, 0))
    n_pw = len(POOL_WINDOWS) * POOL_GC
    win = (W_IN_BLK, D_MODEL)
    blk = ((W_OUT_BLK, D_MODEL), (PW_BLK, W_MIX))
    rep = ((VEC_ROWS, D_MODEL), (DW_ROWS, W_MIX), (n_pw, POOL_GC))
    half = lambda sh: (sh[0] // 2,) + sh[1:]
    dma = pltpu.SemaphoreType.DMA
    assert bl * nt > STEP_HALF + 1, "the exchanges' phases need their grid steps"
    return pl.pallas_call(
        functools.partial(_bwd2a_body, bl * nt),
        name="bwd2a",
        grid=(bl, nt),
        in_specs=[tok(D_MODEL), tok(D_IN), VMEM] + [pl.BlockSpec(memory_space=pl.ANY)] * 2 + [VMEM] * 5,
        out_specs=[VMEM] * 7,
        out_shape=[jax.ShapeDtypeStruct(win, F32), jax.ShapeDtypeStruct((3,) + win, BF16)]
        + [jax.ShapeDtypeStruct(sh, F32) for sh in blk + (rep[0], (CONV_WIDTH, DW_BLK), rep[2])],
        scratch_shapes=[pltpu.VMEM((D_IN, D_MODEL), F32), pltpu.VMEM((4,) + win, BF16), pltpu.VMEM((4,) + win, BF16)]
        + [pltpu.VMEM((4,) + sh, F32) for sh in blk] * 2
        + [pltpu.VMEM((3,) + sh, BF16) for sh in blk] * 2
        + [pltpu.VMEM(sh, F32) for sh in rep] * 3
        + [pltpu.VMEM((4,) + half(sh), dt) for sh, dt in zip(rep, HALF_DTYPES)]
        + [dma((2, 4)), dma((2, 4)), dma((2, 4)), dma((2, 3)), dma((2, 3)), dma((3, 5)), dma((3, 5)), dma((1, 4)), dma((1, 4))],
        compiler_params=pltpu.CompilerParams(dimension_semantics=("arbitrary", "arbitrary"), vmem_limit_bytes=VMEM_LIMIT),
    )(x, dproj, _row(norm_g), gwout, gpww, loss8, gfg8, gvec, gdw8, gpoolw)


HBM = pl.BlockSpec(memory_space=pltpu.HBM)
SEM = pl.BlockSpec(memory_space=pltpu.SEMAPHORE)
EFFECT = pltpu.SideEffectType.DATAFLOW_SIDE_EFFECTING


def _owner_copies(src_ref, land_ref, send_sems, recv_sems):
    x, y, c = lax.axis_index("x"), lax.axis_index("y"), lax.axis_index("c")
    owners = ((1 - x, y, c), (x, 1 - y, c), (1 - x, 1 - y, c))
    return [pltpu.make_async_remote_copy(src_ref=src_ref.at[j], dst_ref=land_ref.at[j], send_sem=send_sems.at[j],
                                         recv_sem=recv_sems.at[j], device_id=owners[j], device_id_type=MESH)
            for j in range(3)]


def _rs_start(staged):
    def body(src_ref, land_ref, send_sems, recv_sems, src_thru, land_thru, token):
        for cp in _owner_copies(src_ref, land_ref, send_sems, recv_sems):
            cp.start()
        token[...] = jnp.zeros_like(token)

    dma = pltpu.SemaphoreType.DMA
    return pl.pallas_call(
        body, name="rs_start",
        out_shape=(dma((3,)), dma((3,)), pltpu.HBM(staged.shape, staged.dtype), pltpu.HBM(staged.shape, staged.dtype),
                   jax.ShapeDtypeStruct((SUBLANES, LANES), F32)),
        in_specs=(HBM, HBM), out_specs=(SEM, SEM, HBM, HBM, VMEM), input_output_aliases={0: 2, 1: 3},
        compiler_params=pltpu.CompilerParams(has_side_effects=EFFECT),
    )(pltpu.with_memory_space_constraint(staged, pltpu.HBM),
      pltpu.with_memory_space_constraint(lax.empty(staged.shape, staged.dtype), pltpu.HBM))


def _rs_wait(send_sems, recv_sems, src_thru, land_thru, after):
    def body(src_ref, land_ref, send_sems, recv_sems, after_ref, src_dead, got_ref):
        for cp in _owner_copies(src_ref, land_ref, send_sems, recv_sems):
            cp.wait_send()
            cp.wait_recv()

    return pl.pallas_call(
        body, name="rs_wait",
        out_shape=(pltpu.HBM(src_thru.shape, src_thru.dtype), pltpu.HBM(land_thru.shape, land_thru.dtype)),
        in_specs=(HBM, HBM, SEM, SEM, pl.BlockSpec(memory_space=pl.ANY)), out_specs=(HBM, HBM),
        input_output_aliases={0: 0, 1: 1},
        compiler_params=pltpu.CompilerParams(has_side_effects=EFFECT),
    )(src_thru, land_thru, send_sems, recv_sems, after)[1]


RING = 3


def _bwd2b_body(n_steps, tiles_per_seq, x_hbm, dh_hbm, dproj_hbm, ng_ref, wint_ref, token_ref, gx_ref, gng_ref,
                x_ring, dh_ring, dproj_ring, sems):
    step = pl.program_id(0) * pl.num_programs(1) + pl.program_id(1)
    tm = TILE_M

    def tile_copies(k):
        k = jnp.asarray(k, jnp.int32)
        slot = k % RING
        rows = pl.ds(pl.multiple_of((k % tiles_per_seq) * tm, tm), tm)
        return [pltpu.make_async_copy(src.at[k // tiles_per_seq, rows, :], ring.at[slot], sems.at[j, slot])
                for j, (src, ring) in enumerate(((x_hbm, x_ring), (dh_hbm, dh_ring), (dproj_hbm, dproj_ring)))]

    @pl.when(step == 0)
    def _():
        gng_ref[...] = jnp.zeros_like(gng_ref) + token_ref[0:1, 0:1]
        for k in range(RING - 1):
            for cp in tile_copies(k):
                cp.start()

    @pl.when(step + RING - 1 < n_steps)
    def _():
        for cp in tile_copies(step + RING - 1):
            cp.start()

    for cp in tile_copies(step):
        cp.wait()
    slot = step % RING
    x_t = x_ring[slot]
    r1 = lax.rsqrt(jnp.mean(x_t * x_t, axis=-1, keepdims=True) + RMS_EPS)
    xr = x_t * r1
    dhn = jnp.dot(dproj_ring[slot], wint_ref[...], preferred_element_type=F32)
    gng_ref[...] += _colsum8(dhn * xr)
    gy = dhn * ng_ref[...]
    gx_ref[0] = dh_ring[slot] + r1 * (gy - xr * jnp.mean(gy * xr, axis=-1, keepdims=True))


def _bwd2b(x, dh, dproj, norm_g, w_in_t, token):
    bl, s, _ = x.shape
    tm = TILE_M
    nt = s // tm
    assert bl * nt >= RING
    hbm = pl.BlockSpec(memory_space=pl.ANY)
    return pl.pallas_call(
        functools.partial(_bwd2b_body, bl * nt, nt),
        name="bwd2b",
        grid=(bl, nt),
        in_specs=[hbm, hbm, hbm, VMEM, VMEM, VMEM],
        out_specs=[pl.BlockSpec((1, tm, D_MODEL), lambda b, i: (b, i, 0)),
                   pl.BlockSpec((SUBLANES, D_MODEL), lambda b, i: (0, 0))],
        out_shape=[jax.ShapeDtypeStruct((bl, s, D_MODEL), F32), jax.ShapeDtypeStruct((SUBLANES, D_MODEL), F32)],
        scratch_shapes=[pltpu.VMEM((RING, tm, D_MODEL), F32), pltpu.VMEM((RING, tm, D_MODEL), F32),
                        pltpu.VMEM((RING, tm, D_IN), BF16), pltpu.SemaphoreType.DMA((3, RING))],
        compiler_params=pltpu.CompilerParams(dimension_semantics=("arbitrary", "arbitrary"), vmem_limit_bytes=VMEM_LIMIT),
    )(x, dh, dproj, _row(norm_g), w_in_t, token)


def _partial_copies(part_ref, land_ref, send_sems, recv_sems):
    x, y, c = lax.axis_index("x"), lax.axis_index("y"), lax.axis_index("c")
    other_chips = [(1 - x, y), (x, 1 - y), (1 - x, 1 - y)]
    peers = [(x, y, 1 - c)] + [(*chip, c) for chip in other_chips] + [(*chip, 1 - c) for chip in other_chips]
    return [pltpu.make_async_remote_copy(src_ref=part_ref, dst_ref=land_ref.at[4 * x + 2 * y + c], send_sem=send_sems.at[r],
                                         recv_sem=recv_sems.at[r], device_id=peers[r], device_id_type=MESH)
            for r in range(N_PEER)]


def _partials_start(part):
    def body(part_ref, land_ref, send_sems, recv_sems, part_thru, land_thru):
        for cp in _partial_copies(part_ref, land_ref, send_sems, recv_sems):
            cp.start()

    dma = pltpu.SemaphoreType.DMA
    land = (N_DEV,) + part.shape
    return pl.pallas_call(
        body, name="partials_start",
        out_shape=(dma((N_PEER,)), dma((N_PEER,)), pltpu.HBM(part.shape, part.dtype), pltpu.HBM(land, part.dtype)),
        in_specs=(HBM, HBM), out_specs=(SEM, SEM, HBM, HBM), input_output_aliases={0: 2, 1: 3},
        compiler_params=pltpu.CompilerParams(has_side_effects=EFFECT),
    )(pltpu.with_memory_space_constraint(part, pltpu.HBM),
      pltpu.with_memory_space_constraint(lax.empty(land, part.dtype), pltpu.HBM))


def _partials_wait(send_sems, recv_sems, part_thru, land_thru, after):
    def body(part_ref, land_ref, send_sems, recv_sems, after_ref, part_out, land_out):
        for cp in _partial_copies(part_ref, land_ref, send_sems, recv_sems):
            cp.wait_send()
            cp.wait_recv()

    return pl.pallas_call(
        body, name="partials_wait",
        out_shape=(pltpu.HBM(part_thru.shape, part_thru.dtype), pltpu.HBM(land_thru.shape, land_thru.dtype)),
        in_specs=(HBM, HBM, SEM, SEM, pl.BlockSpec(memory_space=pl.ANY)), out_specs=(HBM, HBM),
        input_output_aliases={0: 0, 1: 1},
        compiler_params=pltpu.CompilerParams(has_side_effects=EFFECT),
    )(part_thru, land_thru, send_sems, recv_sems, after)


ADAM_LR = 0.001
ADAM_B1 = 0.9
ADAM_B2 = 0.999
ADAM_EPS = 1e-08
ADAM_WD = 0.01
ADAM_STEP = 10


PARAMS = ("norm_g", "w_in", "pool_w", "pool_b", "pool_scale", "conv_dw", "conv_b", "ln_g", "ln_b", "pw_w", "pw_b", "w_out",
          "final_g")


LATE_PARAMS = ("norm_g",)
MAIN_PARAMS = tuple(p for p in PARAMS if p not in LATE_PARAMS)


def _adamw_refs(params, n_sources, refs):
    n = len(params)
    ws, ms, vs = (dict(zip(params, refs[k * n:(k + 1) * n])) for k in range(3))
    outs = refs[3 * n + n_sources:]
    gs, ds, m2s, v2s = (dict(zip(params, outs[k * n:(k + 1) * n])) for k in range(4))

    def update(name, g, at=slice(None)):
        m = ADAM_B1 * ms[name][at] + (1.0 - ADAM_B1) * g
        v = ADAM_B2 * vs[name][at] + (1.0 - ADAM_B2) * (g * g)
        m_hat = m / (1.0 - ADAM_B1 ** ADAM_STEP)
        v_hat = v / (1.0 - ADAM_B2 ** ADAM_STEP)
        gs[name][at] = g
        ds[name][at] = -ADAM_LR * (m_hat / (jnp.sqrt(v_hat) + ADAM_EPS) + ADAM_WD * ws[name][at])
        m2s[name][at] = m
        v2s[name][at] = v

    return update, refs[3 * n:3 * n + n_sources], outs[4 * n:]


def _adamw_late_body(*refs):
    update, (gng_ref, gng_others_ref), _ = _adamw_refs(LATE_PARAMS, 2, refs)
    me = 4 * lax.axis_index("x") + 2 * lax.axis_index("y") + lax.axis_index("c")
    g_norm = jnp.zeros(gng_ref.shape, F32)
    for k in range(N_DEV):
        g_norm = g_norm + jnp.where(me == k, gng_ref[...], gng_others_ref[k])
    update("norm_g", jnp.sum(g_norm, axis=0, keepdims=True))


def _adamw_body(*refs):
    update, sources, (loss_ref,) = _adamw_refs(MAIN_PARAMS, 7, refs)
    gwin_ref, gwin_others_ref, gwout_ref, gpww_ref, vec_ref, gdw_ref, gpoolw_ref = sources
    ws = dict(zip(MAIN_PARAMS, refs))
    row = lambda r, lo, hi: vec_ref[r:r + 1, lo:hi]
    loss_ref[...] = row(R_LOSS, 0, 1)
    g_w_in = gwin_ref[...]
    for j in range(3):
        g_w_in = g_w_in + gwin_others_ref[j].astype(F32)
    update("w_in", g_w_in)
    update("pool_w", gpoolw_ref[...].reshape(ws["pool_w"].shape))
    for g in range(len(POOL_WINDOWS)):
        update("pool_b", row(R_POOL, g * POOL_GC, (g + 1) * POOL_GC), at=slice(g, g + 1))
    update("pool_scale", row(R_POOL, W_MIX, 2 * W_MIX))
    update("conv_dw", gdw_ref[...])
    update("conv_b", row(R_CONV, 0, W_MIX))
    update("ln_g", row(R_CONV, W_MIX, 2 * W_MIX))
    update("ln_b", row(R_LN, 0, W_MIX))
    update("pw_w", gpww_ref[...])
    update("pw_b", row(R_LN, W_MIX, 2 * W_MIX))
    update("w_out", gwout_ref[...])
    update("final_g", row(R_FINAL_G, 0, D_MODEL))


def _adamw(body, name, params, ws, ms, vs, grad_sources, extra_out_shapes=()):
    n = len(params)
    flat = lambda a: a.reshape(1, -1) if a.ndim == 1 else a
    args = [flat(d[p]) for d in (ws, ms, vs) for p in params]
    outs = pl.pallas_call(
        body,
        name=name,
        in_specs=[VMEM] * (3 * n + len(grad_sources)),
        out_specs=[VMEM] * (4 * n + len(extra_out_shapes)),
        out_shape=[jax.ShapeDtypeStruct(a.shape, F32) for a in args[:n]] * 4 + list(extra_out_shapes),
        compiler_params=pltpu.CompilerParams(vmem_limit_bytes=VMEM_LIMIT),
    )(*args, *grad_sources)
    groups = tuple({p: outs[k * n + j].reshape(ws[p].shape) for j, p in enumerate(params)} for k in range(4))
    return groups, outs[4 * n:]


def kernel(x, norm_g, w_in, pool_w, pool_b, pool_scale, conv_dw, conv_b, ln_g, ln_b, pw_w, pw_b, w_out, final_g, loss_target, m_norm_g, m_w_in, m_pool_w, m_pool_b, m_pool_scale, m_conv_dw, m_conv_b, m_ln_g, m_ln_b, m_pw_w, m_pw_b, m_w_out, m_final_g, v_norm_g, v_w_in, v_pool_w, v_pool_b, v_pool_scale, v_conv_dw, v_conv_b, v_ln_g, v_ln_b, v_pw_w, v_pw_b, v_w_out, v_final_g):
    proj, w_in_t, w_out_b, pw_w_b, conv_dw_f = _inproj(x, norm_g, w_in.T, w_out, pw_w, conv_dw)
    h1, z, ob, dpool, dh, loss8, gfg8 = _fwd(x, loss_target, proj, pool_w, pool_b, pool_scale, conv_dw_f,
                                             conv_b, ln_g, ln_b, pw_w_b, pw_b, w_out_b, final_g)
    dproj, gwout, gpww, gpoolw, gvec, gdw8 = _bwd1(dh, proj, h1, z, ob, dpool, w_out_b, pw_w_b, pool_w, pool_scale,
                                                   conv_dw_f, ln_g, ln_b)
    win_own, win_staged, g_w_out, g_pw_w, vec, g_conv_dw, poolw_sum = _bwd2a(x, dproj, norm_g, gwout, gpww, loss8, gfg8, gvec,
                                                                             gdw8, gpoolw)
    send_sems, recv_sems, staged_thru, land_thru, token = _rs_start(win_staged)
    grad_x, gng8 = _bwd2b(x, dh, dproj, norm_g, w_in_t, token)
    ng_send, ng_recv, gng8, gng_land = _partials_start(gng8)
    win_others = _rs_wait(send_sems, recv_sems, staged_thru, land_thru, gng8)

    weights = dict(norm_g=norm_g, w_in=w_in.T, pool_w=pool_w, pool_b=pool_b, pool_scale=pool_scale, conv_dw=conv_dw, conv_b=conv_b,
                   ln_g=ln_g, ln_b=ln_b, pw_w=pw_w, pw_b=pw_b, w_out=w_out, final_g=final_g)
    m_in = dict(norm_g=m_norm_g, w_in=m_w_in.T, pool_w=m_pool_w, pool_b=m_pool_b, pool_scale=m_pool_scale, conv_dw=m_conv_dw,
                conv_b=m_conv_b, ln_g=m_ln_g, ln_b=m_ln_b, pw_w=m_pw_w, pw_b=m_pw_b, w_out=m_w_out, final_g=m_final_g)
    v_in = dict(norm_g=v_norm_g, w_in=v_w_in.T, pool_w=v_pool_w, pool_b=v_pool_b, pool_scale=v_pool_scale, conv_dw=v_conv_dw,
                conv_b=v_conv_b, ln_g=v_ln_g, ln_b=v_ln_b, pw_w=v_pw_w, pw_b=v_pw_b, w_out=v_w_out, final_g=v_final_g)
    main, (loss,) = _adamw(functools.partial(_adamw_body), "adamw", MAIN_PARAMS, weights, m_in, v_in,
                           (win_own, win_others, g_w_out, g_pw_w, vec, g_conv_dw, poolw_sum), [jax.ShapeDtypeStruct((1, 1), F32)])
    gng8, gng_others = _partials_wait(ng_send, ng_recv, gng8, gng_land, main[0]["w_out"])
    late, _ = _adamw(functools.partial(_adamw_late_body), "adamw_norm_g", LATE_PARAMS, weights, m_in, v_in, (gng8, gng_others))
    groups = [[(late[k] if p in LATE_PARAMS else main[k])[p] for p in PARAMS] for k in range(4)]
    k_in = PARAMS.index("w_in")
    for group in groups:
        group[k_in] = group[k_in].T
    grads, delta, new_m, new_v = groups
    return (loss.reshape(()), grad_x, *grads, *delta, *new_m, *new_v)
```
